```python
import math
import jax, jax.numpy as jnp
from jax import lax
import numpy as np

D_MODEL = 1024
BATCH = 2
SEQ = 8192
DEPTH = 1

CHUNK = 64

D_MIX = D_MODEL
SB_WIDTH = D_MIX // 2
SB_HEAD_DIM = 64
SB_HEADS = SB_WIDTH // SB_HEAD_DIM
Q_BLOCK = 128
SSM_WIDTH = D_MIX - SB_WIDTH
SSM_GROUP = 16
SSM_GROUPS = SSM_WIDTH // SSM_GROUP
SSM_STATE = 64
DT_MIN = 1e-3
DT_MAX = 1e-1
N_EXPERTS = 32
TOP_K = 4
D_FF = D_MODEL
SWIGLU_LIMIT = 7.0
SWIGLU_ALPHA = 1.702
EXPERT_BLOCK = 128
EPS = 1e-5

kernel_name = "hybrid_sb_s5_moe_block"


def rmsnorm(x, g):
    x32 = x.astype(jnp.float32)
    y = x32 * lax.rsqrt(jnp.mean(x32 * x32, axis=-1, keepdims=True) + EPS)
    return (y * g.astype(jnp.float32)).astype(x.dtype)


def stick_breaking_attention(q, k, v):
    bsz, seq, _ = q.shape
    def heads(a):
        return a.astype(jnp.float32).reshape(bsz, seq, SB_HEADS, SB_HEAD_DIM).transpose(0, 2, 1, 3)
    qh, kh, vh = heads(q), heads(k), heads(v)
    n_blk = seq // Q_BLOCK
    q_blocks = qh.reshape(bsz, SB_HEADS, n_blk, Q_BLOCK, SB_HEAD_DIM).transpose(2, 0, 1, 3, 4)
    starts = jnp.arange(n_blk, dtype=jnp.int32) * Q_BLOCK
    key_pos = jnp.arange(seq, dtype=jnp.int32)
    scale = 1.0 / math.sqrt(SB_HEAD_DIM)

    def one_block(args):
        qb, t0 = args
        z = jnp.einsum('bhqd,bhkd->bhqk', qb, kh) * scale
        t = t0 + jnp.arange(Q_BLOCK, dtype=jnp.int32)
        causal = key_pos[None, :] < t[:, None]
        log_keep = jnp.where(causal, -jax.nn.softplus(z), 0.0)
        between = lax.cumsum(log_keep, axis=3, reverse=True) - log_keep
        a = jnp.where(causal, jnp.exp(jax.nn.log_sigmoid(z) + between), 0.0)
        return jnp.einsum('bhqk,bhkd->bhqd', a, vh)

    out = lax.map(one_block, (q_blocks, starts))
    return out.transpose(1, 0, 3, 2, 4).reshape(bsz, seq, SB_WIDTH)


def s5_mixer(u, lam_re, lam_im, log_dt, b_re, b_im, c_re, c_im, d_skip, w_glu):
    bsz, seq, _ = u.shape
    u32 = u.astype(jnp.float32).reshape(bsz, seq, SSM_GROUPS, SSM_GROUP)
    lam = lax.complex(lam_re.astype(jnp.float32), lam_im.astype(jnp.float32))
    dt = jnp.exp(log_dt.astype(jnp.float32))[:, None]
    lam_bar = jnp.exp(lam * dt)
    b_mat = lax.complex(b_re.astype(jnp.float32), b_im.astype(jnp.float32))
    c_mat = lax.complex(c_re.astype(jnp.float32), c_im.astype(jnp.float32))
    b_bar = ((lam_bar - 1.0) / lam)[:, :, None] * b_mat
    bu = jnp.einsum('gpc,bsgc->bsgp', b_bar, u32.astype(jnp.complex64))
    a = jnp.broadcast_to(lam_bar, bu.shape)

    def combine(left, right):
        a_l, b_l = left
        a_r, b_r = right
        return a_r * a_l, a_r * b_l + b_r

    _, states = lax.associative_scan(combine, (a, bu), axis=1)
    y = jnp.real(jnp.einsum('gcp,bsgp->bsgc', c_mat, states))
    y = y + d_skip.astype(jnp.float32).reshape(SSM_GROUPS, SSM_GROUP) * u32
    y = jax.nn.gelu(y.reshape(bsz, seq, SSM_WIDTH))
    ab = y @ w_glu.astype(jnp.float32)
    y_a, y_b = jnp.split(ab, 2, axis=-1)
    return y_a * jax.nn.sigmoid(y_b)


def moe_ffn(xn, w_router, b_router, w_gate, b_gate, w_up, b_up, w_down, b_down):
    bsz, seq, d = xn.shape
    n_tok = bsz * seq
    xt = xn.reshape(n_tok, d)
    logits = (xt @ w_router + b_router).astype(jnp.float32)
    top_val, top_idx = lax.top_k(logits, TOP_K)
    gates = jax.nn.softmax(top_val, axis=-1)
    n_assign = n_tok * TOP_K
    e_flat = top_idx.reshape(n_assign)
    tok_flat = jnp.repeat(jnp.arange(n_tok, dtype=jnp.int32), TOP_K)
    g_flat = gates.reshape(n_assign)
    order = jnp.argsort(e_flat)
    e_sorted, tok_sorted, g_sorted = e_flat[order], tok_flat[order], g_flat[order]
    counts = jnp.bincount(e_flat, length=N_EXPERTS)
    starts = jnp.cumsum(counts) - counts
    padded = ((counts + EXPERT_BLOCK - 1) // EXPERT_BLOCK) * EXPERT_BLOCK
    pad_starts = jnp.cumsum(padded) - padded
    pad_ends = pad_starts + padded
    rank = jnp.arange(n_assign, dtype=jnp.int32) - starts[e_sorted]
    dest = pad_starts[e_sorted] + rank
    n_rows = n_assign + N_EXPERTS * EXPERT_BLOCK
    n_blocks = n_rows // EXPERT_BLOCK
    row_tok = jnp.full((n_rows,), n_tok, dtype=jnp.int32).at[dest].set(tok_sorted)
    x_pad = jnp.concatenate([xt, jnp.zeros((1, d), xt.dtype)], axis=0)
    x_rows = x_pad[row_tok].reshape(n_blocks, EXPERT_BLOCK, d)
    block_start = jnp.arange(n_blocks, dtype=jnp.int32) * EXPERT_BLOCK
    block_expert = jnp.minimum(jnp.searchsorted(pad_ends, block_start, side='right'), N_EXPERTS - 1)

    def expert_block(args):
        xb, e = args
        gate = xb @ w_gate[e] + b_gate[e]
        up = xb @ w_up[e] + b_up[e]
        gate = jnp.minimum(gate, SWIGLU_LIMIT)
        up = jnp.clip(up, -SWIGLU_LIMIT, SWIGLU_LIMIT)
        glu = gate * jax.nn.sigmoid(SWIGLU_ALPHA * gate)
        return ((up + 1.0) * glu) @ w_down[e] + b_down[e]

    y_rows = lax.map(expert_block, (x_rows, block_expert)).reshape(n_rows, d)
    y_assign = y_rows[dest].astype(jnp.float32) * g_sorted[:, None]
    out = jax.ops.segment_sum(y_assign, tok_sorted, num_segments=n_tok)
    return out.reshape(bsz, seq, d).astype(xn.dtype)


def setup_inputs(seed: int = 0) -> dict:
    key = jax.random.key(seed)
    ks = jax.random.split(key, 26)
    L, D, W, G, C, P, E, F = DEPTH, D_MODEL, SSM_WIDTH, SSM_GROUPS, SSM_GROUP, SSM_STATE, N_EXPERTS, D_FF
    n_in = 3 * SB_WIDTH + SSM_WIDTH
    nrm = lambda k, shape: jax.random.normal(k, shape, jnp.float32)
    return {
        "x": nrm(ks[0], (BATCH, SEQ, D)),
        "ln1_g": 1.0 + 0.01 * nrm(ks[1], (L, D)),
        "w_in": nrm(ks[2], (L, D, n_in)) * D ** -0.5,
        "lam_re": -0.5 + 0.01 * nrm(ks[3], (L, G, P)),
        "lam_im": jnp.pi * jnp.arange(P, dtype=jnp.float32) + 0.01 * nrm(ks[4], (L, G, P)),
        "log_dt": jax.random.uniform(ks[5], (L, G), jnp.float32, math.log(DT_MIN), math.log(DT_MAX)),
        "ssm_b_re": nrm(ks[6], (L, G, P, C)) * (2.0 * C) ** -0.5,
        "ssm_b_im": nrm(ks[7], (L, G, P, C)) * (2.0 * C) ** -0.5,
        "ssm_c_re": nrm(ks[8], (L, G, C, P)) * (2.0 * P) ** -0.5,
        "ssm_c_im": nrm(ks[9], (L, G, C, P)) * (2.0 * P) ** -0.5,
        "ssm_d": nrm(ks[10], (L, W)),
        "w_glu": nrm(ks[11], (L, W, 2 * W)) * W ** -0.5,
        "g_sb": 1.0 + 0.01 * nrm(ks[12], (L, SB_WIDTH)),
        "g_ssm": 1.0 + 0.01 * nrm(ks[13], (L, SSM_WIDTH)),
        "w_out": nrm(ks[14], (L, D_MIX, D)) * D_MIX ** -0.5,
        "ln2_g": 1.0 + 0.01 * nrm(ks[15], (L, D)),
        "w_router": nrm(ks[16], (L, D, E)) * D ** -0.5,
        "b_router": 0.01 * nrm(ks[17], (L, E)),
        "w_gate": nrm(ks[18], (L, E, D, F)) * D ** -0.5,
        "b_gate": 0.01 * nrm(ks[19], (L, E, F)),
        "w_up": nrm(ks[20], (L, E, D, F)) * D ** -0.5,
        "b_up": 0.01 * nrm(ks[21], (L, E, F)),
        "w_down": nrm(ks[22], (L, E, F, D)) * F ** -0.5,
        "b_down": 0.01 * nrm(ks[23], (L, E, D)),
        "ln_f_g": 1.0 + 0.01 * nrm(ks[24], (D,)),
    }


def reference(x, ln1_g, w_in, lam_re, lam_im, log_dt, ssm_b_re, ssm_b_im, ssm_c_re, ssm_c_im,
              ssm_d, w_glu, g_sb, g_ssm, w_out, ln2_g, w_router, b_router, w_gate, b_gate,
              w_up, b_up, w_down, b_down, ln_f_g):
    for l in range(DEPTH):
        h = rmsnorm(x, ln1_g[l])
        proj = h @ w_in[l]
        q = proj[..., 0:SB_WIDTH]
        k = proj[..., SB_WIDTH:2 * SB_WIDTH]
        v = proj[..., 2 * SB_WIDTH:3 * SB_WIDTH]
        u = proj[..., 3 * SB_WIDTH:]
        y_sb = stick_breaking_attention(q, k, v)
        y_ssm = s5_mixer(u, lam_re[l], lam_im[l], log_dt[l], ssm_b_re[l], ssm_b_im[l],
                         ssm_c_re[l], ssm_c_im[l], ssm_d[l], w_glu[l])
        mixed = jnp.concatenate([rmsnorm(y_sb, g_sb[l]), rmsnorm(y_ssm, g_ssm[l])], axis=-1)
        x = x + (mixed @ w_out[l].astype(jnp.float32)).astype(x.dtype)
        x = x + moe_ffn(rmsnorm(x, ln2_g[l]), w_router[l], b_router[l], w_gate[l], b_gate[l],
                        w_up[l], b_up[l], w_down[l], b_down[l])
    return rmsnorm(x, ln_f_g)
```

```python
import functools
import math

import jax
import jax.numpy as jnp
from jax import lax
from jax.experimental import pallas as pl
from jax.experimental.pallas import tpu as pltpu

F32 = jnp.float32
BF16 = jnp.bfloat16
I32 = jnp.int32

EPS = 1e-5
SB_HEAD_DIM = 64
SSM_GROUP = 16
SSM_STATE = 64
TOP_K = 4
SWIGLU_LIMIT = 7.0
SWIGLU_ALPHA = 1.702

LANES = 128
HEADS_PER_BLOCK = LANES // SB_HEAD_DIM
ATTN_BLOCK = 256
SSM_CHUNK = 64
TOKEN_TILE = 512
EXPERT_ROWS = 256
COMBINE_TILE = 256
MIB = 1024 * 1024


def _rms(x, g):
    return x * lax.rsqrt(jnp.mean(x * x, axis=-1, keepdims=True) + EPS) * g


def _in_proj_kernel(x_ref, g_ref, w_ref, q_ref, k_ref, v_ref, u_ref, *, sb, scale):
    h = _rms(x_ref[...], g_ref[...])
    proj = jnp.dot(h.astype(BF16), w_ref[...], preferred_element_type=F32)
    q_ref[...] = (proj[:, :sb] * scale).astype(BF16)
    k_ref[...] = proj[:, sb:2 * sb].astype(BF16)
    v_ref[...] = proj[:, 2 * sb:3 * sb].astype(BF16)
    u_ref[...] = proj[:, 3 * sb:].astype(BF16)


def _in_proj(x2, ln1_g, w_in, sb, ssm_w):
    t, d = x2.shape
    n_in = w_in.shape[1]
    tm = min(TOKEN_TILE, t)
    scale = 1.0 / math.sqrt(SB_HEAD_DIM)
    return pl.pallas_call(
        functools.partial(_in_proj_kernel, sb=sb, scale=scale),
        grid=(t // tm,),
        in_specs=[
            pl.BlockSpec((tm, d), lambda i: (i, 0)),
            pl.BlockSpec((1, d), lambda i: (0, 0)),
            pl.BlockSpec((d, n_in), lambda i: (0, 0)),
        ],
        out_specs=[
            pl.BlockSpec((tm, sb), lambda i: (i, 0)),
            pl.BlockSpec((tm, sb), lambda i: (i, 0)),
            pl.BlockSpec((tm, sb), lambda i: (i, 0)),
            pl.BlockSpec((tm, ssm_w), lambda i: (i, 0)),
        ],
        out_shape=[
            jax.ShapeDtypeStruct((t, sb), BF16),
            jax.ShapeDtypeStruct((t, sb), BF16),
            jax.ShapeDtypeStruct((t, sb), BF16),
            jax.ShapeDtypeStruct((t, ssm_w), BF16),
        ],
        compiler_params=pltpu.CompilerParams(
            dimension_semantics=("arbitrary",), vmem_limit_bytes=40 * MIB),
        name="in_proj",
    )(x2, ln1_g.reshape(1, d), w_in.astype(BF16))


def _attn_kernel(q_ref, k_ref, v_ref, tri_ref, o_ref, acc_ref, r_ref, *, blk):
    i = pl.program_id(2)
    q2 = q_ref[...]
    lane = lax.broadcasted_iota(I32, (1, LANES), 1)
    head_mask = [lane < SB_HEAD_DIM, lane >= SB_HEAD_DIM]
    zero = jnp.zeros((), BF16)
    q_heads = [jnp.where(m, q2, zero) for m in head_mask]
    row = lax.broadcasted_iota(I32, (blk, blk), 0)
    col = lax.broadcasted_iota(I32, (blk, blk), 1)
    causal = col < row

    acc_ref[...] = jnp.zeros_like(acc_ref)
    r_ref[...] = jnp.zeros_like(r_ref)

    def visit(j, diagonal):
        start = pl.multiple_of(j * blk, blk)
        ks = k_ref[pl.ds(start, blk), :]
        vs = v_ref[pl.ds(start, blk), :]
        acc = acc_ref[...]
        for h in range(HEADS_PER_BLOCK):
            z = lax.dot_general(q_heads[h], ks, (((1,), (1,)), ((), ())),
                                preferred_element_type=F32)
            sp = jnp.maximum(z, 0.0) + jnp.log(1.0 + jnp.exp(-jnp.abs(z)))
            log_keep = -sp
            if diagonal:
                log_keep = jnp.where(causal, log_keep, 0.0)
            cr = jnp.dot(log_keep.astype(BF16), tri_ref[...], preferred_element_type=F32)
            r = r_ref[h]
            between = cr[:, :blk] + jnp.concatenate([r] * (blk // LANES), axis=1)
            a = jnp.exp(z - sp + between)
            if diagonal:
                a = jnp.where(causal, a, 0.0)
            r_ref[h] = r + cr[:, blk:]
            vh = jnp.where(head_mask[h], vs, zero)
            acc = acc + jnp.dot(a.astype(BF16), vh, preferred_element_type=F32)
        acc_ref[...] = acc

    visit(i, True)

    def body(step, carry):
        visit(i - 1 - step, False)
        return carry

    lax.fori_loop(0, i, body, 0)
    o_ref[...] = acc_ref[...].astype(o_ref.dtype)


def _attention(q, k, v, bsz, seq):
    sb = q.shape[-1]
    blk = min(ATTN_BLOCK, seq)
    n_pairs = sb // LANES
    q3, k3, v3 = (a.reshape(bsz, seq, sb) for a in (q, k, v))
    jj = lax.broadcasted_iota(I32, (blk, blk + LANES), 0)
    ss = lax.broadcasted_iota(I32, (blk, blk + LANES), 1)
    tri = ((jj > ss) | (ss >= blk)).astype(BF16)
    out = pl.pallas_call(
        functools.partial(_attn_kernel, blk=blk),
        grid=(bsz, n_pairs, seq // blk),
        in_specs=[
            pl.BlockSpec((None, blk, LANES), lambda b, p, i: (b, i, p)),
            pl.BlockSpec((None, seq, LANES), lambda b, p, i: (b, 0, p)),
            pl.BlockSpec((None, seq, LANES), lambda b, p, i: (b, 0, p)),
            pl.BlockSpec((blk, blk + LANES), lambda b, p, i: (0, 0)),
        ],
        out_specs=pl.BlockSpec((None, blk, LANES), lambda b, p, i: (b, i, p)),
        out_shape=jax.ShapeDtypeStruct((bsz, seq, sb), BF16),
        scratch_shapes=[
            pltpu.VMEM((blk, LANES), F32),
            pltpu.VMEM((HEADS_PER_BLOCK, blk, LANES), F32),
        ],
        compiler_params=pltpu.CompilerParams(
            dimension_semantics=("arbitrary", "arbitrary", "arbitrary"),
            vmem_limit_bytes=40 * MIB),
        name="sb_attention",
    )(q3, k3, v3, tri)
    return out.reshape(bsz * seq, sb)


def _ssm_tables(lam_re, lam_im, log_dt, b_re, b_im, c_re, c_im, chunk, n_chunks):
    g, p = lam_re.shape
    c = b_re.shape[-1]
    lam = lax.complex(lam_re.astype(F32), lam_im.astype(F32))
    dt = jnp.exp(log_dt.astype(F32))[:, None]
    lam_dt = lam * dt
    lam_bar = jnp.exp(lam_dt)
    b_bar = ((lam_bar - 1.0) / lam)[:, :, None] * lax.complex(b_re.astype(F32), b_im.astype(F32))
    c_mat = lax.complex(c_re.astype(F32), c_im.astype(F32))
    steps = jnp.arange(chunk + 1, dtype=F32)
    pw = jnp.exp(lam_dt[:, None, :] * steps[None, :, None])
    npw = jnp.exp(-lam_dt[:, None, :] * steps[None, :chunk, None])

    def split_cols(zc):
        return jnp.concatenate([jnp.real(zc), jnp.imag(zc)], axis=-1).reshape(g, chunk * c, 2 * p)

    def split_rows(zc):
        m = jnp.concatenate([jnp.real(zc), -jnp.imag(zc)], axis=-1).reshape(g, chunk * c, 2 * p)
        return jnp.swapaxes(m, 1, 2)

    b_t = jnp.swapaxes(b_bar, 1, 2)
    src = split_cols(npw[:, :, None, :] * b_t[:, None, :, :])
    dst = split_rows(pw[:, :chunk, None, :] * c_mat[:, None, :, :])
    to_state = split_cols(pw[:, chunk - 1::-1, None, :][:, :chunk] * b_t[:, None, :, :])
    from_state = split_rows(pw[:, 1:, None, :] * c_mat[:, None, :, :])
    n_steps = max(1, (n_chunks - 1).bit_length())
    powers = []
    cur = pw[:, chunk, :]
    for _ in range(n_steps):
        powers.append(cur)
        cur = cur * cur
    lam_pow = jnp.stack(powers, axis=1)
    a1 = jnp.concatenate([jnp.real(lam_pow), jnp.real(lam_pow)], axis=-1)
    a2 = jnp.concatenate([-jnp.imag(lam_pow), jnp.imag(lam_pow)], axis=-1)
    return (src.astype(BF16), dst.astype(BF16), to_state.astype(BF16),
            from_state.astype(BF16), a1, a2)


def _ssm_kernel(u_ref, src_ref, dst_ref, ts_ref, fs_ref, a1_ref, a2_ref, y_ref, toep_ref,
                *, n_chunks, n_steps, group):
    lc = toep_ref.shape[0]
    p2 = ts_ref.shape[1]
    group_shift = group.bit_length() - 1
    cb = min(256, lc)
    for j in range(lc // cb):
        blk = jnp.dot(src_ref[...], dst_ref[:, j * cb:(j + 1) * cb], preferred_element_type=F32)
        s_idx = lax.broadcasted_iota(I32, (lc, cb), 0) >> group_shift
        t_idx = (lax.broadcasted_iota(I32, (lc, cb), 1) + j * cb) >> group_shift
        toep_ref[:, j * cb:(j + 1) * cb] = jnp.where(s_idx <= t_idx, blk, 0.0).astype(BF16)
    u = u_ref[...]
    y = jnp.dot(u, toep_ref[...], preferred_element_type=F32)
    z = jnp.dot(u, ts_ref[...], preferred_element_type=F32)
    n = lax.broadcasted_iota(I32, z.shape, 0) & (n_chunks - 1)
    x = jnp.where(n >= 1, pltpu.roll(z, 1, 0), 0.0)
    for k in range(n_steps):
        sh = 1 << k
        xs = jnp.where(n >= sh, pltpu.roll(x, sh, 0), 0.0)
        x = x + a1_ref[k:k + 1, :] * xs + a2_ref[k:k + 1, :] * pltpu.roll(xs, p2 // 2, 1)
    y = y + jnp.dot(x.astype(BF16), fs_ref[...], preferred_element_type=F32)
    y_ref[...] = y.astype(y_ref.dtype)


def _ssm(u, bsz, seq, lam_re, lam_im, log_dt, b_re, b_im, c_re, c_im):
    g, p = lam_re.shape
    c = b_re.shape[-1]
    chunk = min(SSM_CHUNK, seq)
    n_chunks = seq // chunk
    assert n_chunks & (n_chunks - 1) == 0 and c & (c - 1) == 0, (n_chunks, c)
    rows = bsz * n_chunks
    lc = chunk * c
    src, dst, to_state, from_state, a1, a2 = _ssm_tables(
        lam_re, lam_im, log_dt, b_re, b_im, c_re, c_im, chunk, n_chunks)
    n_steps = a1.shape[1]
    u_g = u.reshape(bsz, n_chunks, chunk, g, c).transpose(3, 0, 1, 2, 4).reshape(g, rows, lc)
    y_g = pl.pallas_call(
        functools.partial(_ssm_kernel, n_chunks=n_chunks, n_steps=n_steps, group=c),
        grid=(g,),
        in_specs=[
            pl.BlockSpec((None, rows, lc), lambda i: (i, 0, 0)),
            pl.BlockSpec((None, lc, 2 * p), lambda i: (i, 0, 0)),
            pl.BlockSpec((None, 2 * p, lc), lambda i: (i, 0, 0)),
            pl.BlockSpec((None, lc, 2 * p), lambda i: (i, 0, 0)),
            pl.BlockSpec((None, 2 * p, lc), lambda i: (i, 0, 0)),
            pl.BlockSpec((None, n_steps, 2 * p), lambda i: (i, 0, 0)),
            pl.BlockSpec((None, n_steps, 2 * p), lambda i: (i, 0, 0)),
        ],
        out_specs=pl.BlockSpec((None, rows, lc), lambda i: (i, 0, 0)),
        out_shape=jax.ShapeDtypeStruct((g, rows, lc), BF16),
        scratch_shapes=[pltpu.VMEM((lc, lc), BF16)],
        compiler_params=pltpu.CompilerParams(
            dimension_semantics=("arbitrary",), vmem_limit_bytes=40 * MIB),
        name="s5_chunked_scan",
    )(u_g, src, dst, to_state, from_state, a1, a2)
    return y_g.reshape(g, bsz, n_chunks, chunk, c).transpose(1, 2, 3, 0, 4).reshape(bsz * seq, g * c)


def _post_kernel(x_ref, ysb_ref, yss_ref, u_ref, d_ref, wglu_ref, gsb_ref, gssm_ref, wout_ref,
                 ln2_ref, wr_ref, br_ref, x1_ref, xn_ref, idx_ref, gate_ref, *, sb, n_exp):
    u = u_ref[...].astype(F32)
    y = yss_ref[...].astype(F32) + d_ref[...] * u
    y = y * (0.5 * (1.0 + jnp.tanh(math.sqrt(2.0 / math.pi) * (y + 0.044715 * (y * y * y)))))
    ab = jnp.dot(y.astype(BF16), wglu_ref[...], preferred_element_type=F32)
    w = ab.shape[1] // 2
    y_ssm = ab[:, :w] * (1.0 / (1.0 + jnp.exp(-ab[:, w:])))
    m_sb = _rms(ysb_ref[...].astype(F32), gsb_ref[...])
    m_ssm = _rms(y_ssm, gssm_ref[...])
    x1 = (x_ref[...]
          + jnp.dot(m_sb.astype(BF16), wout_ref[:sb, :], preferred_element_type=F32)
          + jnp.dot(m_ssm.astype(BF16), wout_ref[sb:, :], preferred_element_type=F32))
    x1_ref[...] = x1
    xn = _rms(x1, ln2_ref[...])
    xn_ref[...] = xn
    logits = jnp.dot(xn, wr_ref[...], preferred_element_type=F32,
                     precision=lax.Precision.HIGHEST) + br_ref[...]
    lane = lax.broadcasted_iota(I32, logits.shape, 1).astype(F32)
    out_lane = lax.broadcasted_iota(I32, (logits.shape[0], LANES), 1)
    idx_out = jnp.zeros((logits.shape[0], LANES), F32)
    val_out = jnp.zeros((logits.shape[0], LANES), F32)
    top = None
    denom = None
    work = logits
    for k in range(TOP_K):
        m = jnp.max(work, axis=-1, keepdims=True)
        sel = jnp.min(jnp.where(work == m, lane, float(n_exp)), axis=-1, keepdims=True)
        work = jnp.where(lane == sel, -jnp.inf, work)
        if k == 0:
            top = m
        e = jnp.exp(m - top)
        denom = e if denom is None else denom + e
        idx_out = jnp.where(out_lane == k, sel, idx_out)
        val_out = jnp.where(out_lane == k, e, val_out)
    idx_ref[...] = idx_out[:, :TOP_K].astype(I32)
    gate_ref[...] = (val_out / denom)[:, :TOP_K]


def _post(x2, y_sb, y_ss, u, ssm_d, w_glu, g_sb, g_ssm, w_out, ln2_g, w_router, b_router):
    t, d = x2.shape
    sb = y_sb.shape[1]
    w = y_ss.shape[1]
    n_exp = w_router.shape[1]
    tm = min(TOKEN_TILE, t)
    row = lambda i: (i, 0)
    fixed = lambda i: (0, 0)
    return pl.pallas_call(
        functools.partial(_post_kernel, sb=sb, n_exp=n_exp),
        grid=(t // tm,),
        in_specs=[
            pl.BlockSpec((tm, d), row),
            pl.BlockSpec((tm, sb), row),
            pl.BlockSpec((tm, w), row),
            pl.BlockSpec((tm, w), row),
            pl.BlockSpec((1, w), fixed),
            pl.BlockSpec((w, 2 * w), fixed),
            pl.BlockSpec((1, sb), fixed),
            pl.BlockSpec((1, w), fixed),
            pl.BlockSpec((sb + w, d), fixed),
            pl.BlockSpec((1, d), fixed),
            pl.BlockSpec((d, n_exp), fixed),
            pl.BlockSpec((1, n_exp), fixed),
        ],
        out_specs=[
            pl.BlockSpec((tm, d), row),
            pl.BlockSpec((tm, d), row),
            pl.BlockSpec((tm, TOP_K), row),
            pl.BlockSpec((tm, TOP_K), row),
        ],
        out_shape=[
            jax.ShapeDtypeStruct((t, d), F32),
            jax.ShapeDtypeStruct((t, d), F32),
            jax.ShapeDtypeStruct((t, TOP_K), I32),
            jax.ShapeDtypeStruct((t, TOP_K), F32),
        ],
        compiler_params=pltpu.CompilerParams(
            dimension_semantics=("arbitrary",), vmem_limit_bytes=48 * MIB),
        name="post_mixer_router",
    )(x2, y_sb, y_ss, u, ssm_d.reshape(1, w), w_glu.astype(BF16), g_sb.reshape(1, sb),
      g_ssm.reshape(1, w), w_out.astype(BF16), ln2_g.reshape(1, d), w_router,
      b_router.reshape(1, n_exp))


def _routing(top_idx, n_exp, rows_per_block):
    t, k = top_idx.shape
    n_assign = t * k
    e_flat = top_idx.reshape(n_assign)
    onehot = (e_flat[:, None] == jnp.arange(n_exp, dtype=I32)[None, :]).astype(I32)
    csum = jnp.cumsum(onehot, axis=0)
    counts = csum[-1]
    rank = jnp.take_along_axis(csum, e_flat[:, None], axis=1)[:, 0] - 1
    padded = ((counts + rows_per_block - 1) // rows_per_block) * rows_per_block
    pad_ends = jnp.cumsum(padded)
    pad_starts = pad_ends - padded
    dest = (pad_starts[e_flat] + rank).astype(I32)
    n_rows = n_assign + n_exp * rows_per_block
    n_blocks = n_rows // rows_per_block
    tok = jnp.arange(n_assign, dtype=I32) // k
    row_tok = jnp.zeros((n_rows,), I32).at[dest].set(tok)
    block_start = jnp.arange(n_blocks, dtype=I32) * rows_per_block
    block_expert = jnp.minimum(
        jnp.searchsorted(pad_ends, block_start, side="right"), n_exp - 1).astype(I32)
    n_used = (pad_ends[-1] // rows_per_block).astype(I32).reshape(1)
    return dest, row_tok, block_expert, n_used, n_blocks


def _gather_rows(idx_smem, idx_base, src_hbm, dst_vmem, sem, n_rows):
    def issue(r, carry):
        pltpu.make_async_copy(src_hbm.at[pl.ds(idx_smem[idx_base + r], 1), :],
                              dst_vmem.at[pl.ds(r, 1), :], sem).start()
        return carry
    lax.fori_loop(0, n_rows, issue, 0, unroll=8)


def _wait_rows(src_hbm, dst_vmem, sem, n_rows):
    pltpu.make_async_copy(src_hbm.at[pl.ds(0, n_rows), :], dst_vmem, sem).wait()


def _staged_gather_step(i, n, idx_hbm, src_hbm, idx_smem, buf, idx_sem, row_sem, n_rows):
    def slot_base(b):
        return pl.multiple_of((b % 3) * n_rows, n_rows)

    def idx_copy(b):
        src = idx_hbm.at[pl.ds(pl.multiple_of(b * n_rows, n_rows), n_rows)]
        return pltpu.make_async_copy(src, idx_smem.at[pl.ds(slot_base(b), n_rows)],
                                     idx_sem.at[b % 3])

    @pl.when(i == 0)
    def _():
        idx_copy(0).start()
        idx_copy(0).wait()
        _gather_rows(idx_smem, 0, src_hbm, buf.at[0], row_sem.at[0], n_rows)

        @pl.when(n > 1)
        def _():
            idx_copy(1).start()

    @pl.when(i + 1 < n)
    def _():
        idx_copy(i + 1).wait()

        @pl.when(i + 2 < n)
        def _():
            idx_copy(i + 2).start()
        nxt = (i + 1) % 2
        _gather_rows(idx_smem, slot_base(i + 1), src_hbm, buf.at[nxt], row_sem.at[nxt], n_rows)

    cur = i % 2
    _wait_rows(src_hbm, buf.at[cur], row_sem.at[cur], n_rows)
    return cur


def _expert_kernel(be_ref, nused_ref, rowtok_hbm, xn_hbm, wg_ref, bg_ref, wu_ref, bu_ref,
                   wd_ref, bd_ref, y_ref, idx_smem, xbuf, idx_sem, row_sem, wg_bf, wu_bf, wd_bf,
                   *, rows):
    i = pl.program_id(0)
    n = nused_ref[0]

    @pl.when(i < n)
    def _():
        cur = _staged_gather_step(i, n, rowtok_hbm, xn_hbm, idx_smem, xbuf, idx_sem, row_sem, rows)
        changed = jnp.logical_or(i == 0, be_ref[i] != be_ref[jnp.maximum(i - 1, 0)])

        @pl.when(changed)
        def _():
            wg_bf[...] = wg_ref[...].astype(BF16)
            wu_bf[...] = wu_ref[...].astype(BF16)
            wd_bf[...] = wd_ref[...].astype(BF16)

        x = xbuf[cur].astype(BF16)
        gate = jnp.dot(x, wg_bf[...], preferred_element_type=F32) + bg_ref[...]
        up = jnp.dot(x, wu_bf[...], preferred_element_type=F32) + bu_ref[...]
        gate = jnp.minimum(gate, SWIGLU_LIMIT)
        up = jnp.clip(up, -SWIGLU_LIMIT, SWIGLU_LIMIT)
        glu = gate * (1.0 / (1.0 + jnp.exp(-SWIGLU_ALPHA * gate)))
        hidden = ((up + 1.0) * glu).astype(BF16)
        y_ref[...] = jnp.dot(hidden, wd_bf[...], preferred_element_type=F32) + bd_ref[...]

    @pl.when(i >= n)
    def _():
        y_ref[...] = jnp.zeros_like(y_ref)


def _experts(xn, row_tok, block_expert, n_used, n_blocks, w_gate, b_gate, w_up, b_up,
             w_down, b_down):
    t, d = xn.shape
    n_exp, _, f = w_gate.shape
    rows = EXPERT_ROWS
    wmap = lambda i, be, nu: (be[i], 0, 0)
    grid_spec = pltpu.PrefetchScalarGridSpec(
        num_scalar_prefetch=2,
        grid=(n_blocks,),
        in_specs=[
            pl.BlockSpec(memory_space=pl.ANY),
            pl.BlockSpec(memory_space=pl.ANY),
            pl.BlockSpec((None, d, f), wmap),
            pl.BlockSpec((None, 1, f), wmap),
            pl.BlockSpec((None, d, f), wmap),
            pl.BlockSpec((None, 1, f), wmap),
            pl.BlockSpec((None, f, d), wmap),
            pl.BlockSpec((None, 1, d), wmap),
        ],
        out_specs=pl.BlockSpec((rows, d), lambda i, be, nu: (i, 0)),
        scratch_shapes=[
            pltpu.SMEM((3 * rows,), I32),
            pltpu.VMEM((2, rows, d), F32),
            pltpu.SemaphoreType.DMA((3,)),
            pltpu.SemaphoreType.DMA((2,)),
            pltpu.VMEM((d, f), BF16),
            pltpu.VMEM((d, f), BF16),
            pltpu.VMEM((f, d), BF16),
        ],
    )
    return pl.pallas_call(
        functools.partial(_expert_kernel, rows=rows),
        grid_spec=grid_spec,
        out_shape=jax.ShapeDtypeStruct((n_blocks * rows, d), F32),
        compiler_params=pltpu.CompilerParams(
            dimension_semantics=("arbitrary",), vmem_limit_bytes=52 * MIB),
        name="moe_experts",
    )(block_expert, n_used, row_tok, xn,
      w_gate, b_gate.reshape(n_exp, 1, f), w_up, b_up.reshape(n_exp, 1, f),
      w_down, b_down.reshape(n_exp, 1, d))


def _combine_kernel(dest_hbm, y_hbm, x1_ref, gate_ref, lnf_ref, o_ref, idx_smem, ybuf,
                    idx_sem, row_sem, *, tile):
    i = pl.program_id(0)
    n = pl.num_programs(0)
    cur = _staged_gather_step(i, n, dest_hbm, y_hbm, idx_smem, ybuf, idx_sem, row_sem,
                              TOP_K * tile)
    gates = gate_ref[...]
    x = x1_ref[...]
    for k in range(TOP_K):
        x = x + gates[:, k:k + 1] * ybuf[cur, k * tile:(k + 1) * tile, :]
    o_ref[...] = _rms(x, lnf_ref[...])


def _combine(dest, y_rows, x1, gates, ln_f_g):
    t, d = x1.shape
    tile = min(COMBINE_TILE, t)
    n_tiles = t // tile
    dest_tiles = dest.reshape(n_tiles, tile, TOP_K).transpose(0, 2, 1).reshape(n_tiles * TOP_K * tile)
    return pl.pallas_call(
        functools.partial(_combine_kernel, tile=tile),
        grid=(n_tiles,),
        in_specs=[
            pl.BlockSpec(memory_space=pl.ANY),
            pl.BlockSpec(memory_space=pl.ANY),
            pl.BlockSpec((tile, d), lambda i: (i, 0)),
            pl.BlockSpec((tile, TOP_K), lambda i: (i, 0)),
            pl.BlockSpec((1, d), lambda i: (0, 0)),
        ],
        out_specs=pl.BlockSpec((tile, d), lambda i: (i, 0)),
        out_shape=jax.ShapeDtypeStruct((t, d), F32),
        scratch_shapes=[
            pltpu.SMEM((3 * TOP_K * tile,), I32),
            pltpu.VMEM((2, TOP_K * tile, d), F32),
            pltpu.SemaphoreType.DMA((3,)),
            pltpu.SemaphoreType.DMA((2,)),
        ],
        compiler_params=pltpu.CompilerParams(
            dimension_semantics=("arbitrary",), vmem_limit_bytes=40 * MIB),
        name="moe_combine",
    )(dest_tiles, y_rows, x1, gates, ln_f_g.reshape(1, d))


def kernel(x, ln1_g, w_in, lam_re, lam_im, log_dt, ssm_b_re, ssm_b_im, ssm_c_re, ssm_c_im,
           ssm_d, w_glu, g_sb, g_ssm, w_out, ln2_g, w_router, b_router, w_gate, b_gate,
           w_up, b_up, w_down, b_down, ln_f_g):
    bsz, seq, d = x.shape
    assert ln1_g.shape[0] == 1, "depth-1 block only"
    ssm_w = ssm_d.shape[1]
    sb = g_sb.shape[1]
    n_exp = w_router.shape[2]
    x2 = x.reshape(bsz * seq, d)
    q, k, v, u = _in_proj(x2, ln1_g[0], w_in[0], sb, ssm_w)
    y_sb = _attention(q, k, v, bsz, seq)
    y_ss = _ssm(u, bsz, seq, lam_re[0], lam_im[0], log_dt[0], ssm_b_re[0], ssm_b_im[0],
                ssm_c_re[0], ssm_c_im[0])
    x1, xn, top_idx, gates = _post(x2, y_sb, y_ss, u, ssm_d[0], w_glu[0], g_sb[0], g_ssm[0],
                                   w_out[0], ln2_g[0], w_router[0], b_router[0])
    dest, row_tok, block_expert, n_used, n_blocks = _routing(top_idx, n_exp, EXPERT_ROWS)
    y_rows = _experts(xn, row_tok, block_expert, n_used, n_blocks, w_gate[0], b_gate[0],
                      w_up[0], b_up[0], w_down[0], b_down[0])
    out = _combine(dest, y_rows, x1, gates, ln_f_g)
    return out.reshape(bsz, seq, d)
```

```python
import functools
import math

import jax
import jax.numpy as jnp
from jax import lax
from jax.experimental import pallas as pl
from jax.experimental.pallas import tpu as pltpu

F32 = jnp.float32
BF16 = jnp.bfloat16
I32 = jnp.int32

EPS = 1e-5
SB_HEAD_DIM = 64
SSM_GROUP = 16
SSM_STATE = 64
TOP_K = 4
SWIGLU_LIMIT = 7.0
SWIGLU_ALPHA = 1.702

LANES = 128
HEADS_PER_BLOCK = LANES // SB_HEAD_DIM
ATTN_BLOCK = 256
SSM_CHUNK = 64
TOKEN_TILE = 512
EXPERT_ROWS = 256
COMBINE_TILE = 256
MIB = 1024 * 1024
LOG2E = 1.4426950408889634


def _rms(x, g):
    return x * lax.rsqrt(jnp.mean(x * x, axis=-1, keepdims=True) + EPS) * g


def _in_proj_kernel(x_ref, g_ref, w_ref, q_ref, k_ref, v_ref, u_ref, *, sb, scale):
    h = _rms(x_ref[...], g_ref[...])
    proj = jnp.dot(h.astype(BF16), w_ref[...], preferred_element_type=F32)
    q_ref[...] = (proj[:, :sb] * scale).astype(BF16)
    k_ref[...] = proj[:, sb:2 * sb].astype(BF16)
    v_ref[...] = proj[:, 2 * sb:3 * sb].astype(BF16)
    u_ref[...] = proj[:, 3 * sb:].astype(BF16)


def _in_proj(x2, ln1_g, w_in, sb, ssm_w):
    t, d = x2.shape
    n_in = w_in.shape[1]
    tm = min(TOKEN_TILE, t)
    scale = LOG2E / math.sqrt(SB_HEAD_DIM)
    return pl.pallas_call(
        functools.partial(_in_proj_kernel, sb=sb, scale=scale),
        grid=(t // tm,),
        in_specs=[
            pl.BlockSpec((tm, d), lambda i: (i, 0)),
            pl.BlockSpec((1, d), lambda i: (0, 0)),
            pl.BlockSpec((d, n_in), lambda i: (0, 0)),
        ],
        out_specs=[
            pl.BlockSpec((tm, sb), lambda i: (i, 0)),
            pl.BlockSpec((tm, sb), lambda i: (i, 0)),
            pl.BlockSpec((tm, sb), lambda i: (i, 0)),
            pl.BlockSpec((tm, ssm_w), lambda i: (i, 0)),
        ],
        out_shape=[
            jax.ShapeDtypeStruct((t, sb), BF16),
            jax.ShapeDtypeStruct((t, sb), BF16),
            jax.ShapeDtypeStruct((t, sb), BF16),
            jax.ShapeDtypeStruct((t, ssm_w), BF16),
        ],
        compiler_params=pltpu.CompilerParams(
            dimension_semantics=("arbitrary",), vmem_limit_bytes=40 * MIB),
        name="in_proj",
    )(x2, ln1_g.reshape(1, d), w_in.astype(BF16))


def _attn_kernel(ti_ref, tj_ref, q_ref, k_ref, v_ref, tri_ref, mask_ref, o_ref,
                 wbuf, dbuf, spbuf, lbuf, rsbuf, acc_ref, r_ref, *, blk, n_items):
    lane = lax.broadcasted_iota(I32, (1, LANES), 1)
    head_mask = [lane < SB_HEAD_DIM, lane >= SB_HEAD_DIM]
    zero = jnp.zeros((), BF16)
    sign_bit = jnp.uint32(0x80000000)
    last = n_items - 1
    ALL, CAUSAL, NONE = 0, 1, 2

    for ref in (wbuf, dbuf, spbuf, lbuf, rsbuf, acc_ref, r_ref):
        ref[...] = jnp.zeros_like(ref)

    def both_heads(x):
        return jnp.concatenate([jnp.where(m, x, zero) for m in head_mask], axis=0)

    def item(idx):
        c = jnp.clip(idx, 0, last)
        return ti_ref[c], tj_ref[c]

    def step(p, slot):
        other = 1 - slot
        i1, j1 = item(p)
        q2 = q_ref[pl.ds(pl.multiple_of(i1 * blk, blk), blk), :]
        ks = k_ref[pl.ds(pl.multiple_of(j1 * blk, blk), blk), :]
        wbuf[slot] = lax.dot_general(both_heads(q2), ks, (((1,), (1,)), ((), ())),
                                     preferred_element_type=F32)
        spb = spbuf[other]
        later = jnp.dot(spb, tri_ref[...], preferred_element_type=F32)
        lbuf[slot] = later
        rsbuf[slot] = later[:, 0:1] + spb[:, 0:1].astype(F32)
        i4, j4 = item(p - 3)
        valid = p - 3 <= last
        first = jnp.logical_and(i4 == j4, valid)
        mask4 = mask_ref[jnp.where(valid, (i4 == j4).astype(I32), NONE)]
        vs = v_ref[pl.ds(pl.multiple_of(j4 * blk, blk), blk), :]
        r_prev = jnp.where(first, 0.0, r_ref[...])
        a = jnp.exp2(dbuf[other] - (lbuf[other] + r_prev))
        ab = a.astype(BF16) * jnp.concatenate([mask4, mask4], axis=0)
        r_ref[...] = r_prev + rsbuf[other]
        a_cat = jnp.concatenate([ab[:blk], ab[blk:]], axis=1)
        acc = jnp.where(first, 0.0, acc_ref[...]) + jnp.dot(
            a_cat, both_heads(vs), preferred_element_type=F32)
        acc_ref[...] = acc
        o_ref[pl.ds(pl.multiple_of(i4 * blk, blk), blk), :] = acc.astype(o_ref.dtype)
        i2, j2 = item(p - 1)
        mask2 = mask_ref[(i2 == j2).astype(I32)]
        w = wbuf[other]
        neg_abs = lax.bitcast_convert_type(
            lax.bitcast_convert_type(w, jnp.uint32) | sign_bit, F32)
        sp2 = jnp.maximum(w, 0.0) + jnp.log(1.0 + jnp.exp2(neg_abs)) * LOG2E
        dbuf[other] = w - sp2
        spbuf[slot] = sp2.astype(BF16) * jnp.concatenate([mask2, mask2], axis=0)

    def two_steps(pair, carry):
        step(2 * pair, 0)
        step(2 * pair + 1, 1)
        return carry

    lax.fori_loop(0, (n_items + 3 + 1) // 2, two_steps, 0)


def _attention(q, k, v, bsz, seq):
    sb = q.shape[-1]
    blk = min(ATTN_BLOCK, seq)
    n_blk = seq // blk
    n_pairs = sb // LANES
    q3, k3, v3 = (a.reshape(bsz, seq, sb) for a in (q, k, v))
    jj = lax.broadcasted_iota(I32, (blk, blk), 0)
    ss = lax.broadcasted_iota(I32, (blk, blk), 1)
    tri = (jj > ss).astype(BF16)
    masks = jnp.stack([jnp.ones((blk, blk), BF16), (ss < jj).astype(BF16),
                       jnp.zeros((blk, blk), BF16)])
    items = [(i, j) for i in range(n_blk) for j in range(i, -1, -1)]
    item_i = jnp.asarray([i for i, _ in items], I32)
    item_j = jnp.asarray([j for _, j in items], I32)
    whole = lambda b, p, ti, tj: (b, 0, p)
    grid_spec = pltpu.PrefetchScalarGridSpec(
        num_scalar_prefetch=2,
        grid=(bsz, n_pairs),
        in_specs=[
            pl.BlockSpec((None, seq, LANES), whole),
            pl.BlockSpec((None, seq, LANES), whole),
            pl.BlockSpec((None, seq, LANES), whole),
            pl.BlockSpec((blk, blk), lambda b, p, ti, tj: (0, 0)),
            pl.BlockSpec((3, blk, blk), lambda b, p, ti, tj: (0, 0, 0)),
        ],
        out_specs=pl.BlockSpec((None, seq, LANES), whole),
        scratch_shapes=[
            pltpu.VMEM((2, HEADS_PER_BLOCK * blk, blk), F32),
            pltpu.VMEM((2, HEADS_PER_BLOCK * blk, blk), F32),
            pltpu.VMEM((2, HEADS_PER_BLOCK * blk, blk), BF16),
            pltpu.VMEM((2, HEADS_PER_BLOCK * blk, blk), F32),
            pltpu.VMEM((2, HEADS_PER_BLOCK * blk, 1), F32),
            pltpu.VMEM((blk, LANES), F32),
            pltpu.VMEM((HEADS_PER_BLOCK * blk, 1), F32),
        ],
    )
    out = pl.pallas_call(
        functools.partial(_attn_kernel, blk=blk, n_items=len(items)),
        grid_spec=grid_spec,
        out_shape=jax.ShapeDtypeStruct((bsz, seq, sb), BF16),
        compiler_params=pltpu.CompilerParams(
            dimension_semantics=("arbitrary", "arbitrary"), vmem_limit_bytes=40 * MIB),
        name="sb_attention",
    )(item_i, item_j, q3, k3, v3, tri, masks)
    return out.reshape(bsz * seq, sb)


def _ssm_tables(lam_re, lam_im, log_dt, b_re, b_im, c_re, c_im, chunk, n_chunks):
    g, p = lam_re.shape
    c = b_re.shape[-1]
    lam = lax.complex(lam_re.astype(F32), lam_im.astype(F32))
    dt = jnp.exp(log_dt.astype(F32))[:, None]
    lam_dt = lam * dt
    lam_bar = jnp.exp(lam_dt)
    b_bar = ((lam_bar - 1.0) / lam)[:, :, None] * lax.complex(b_re.astype(F32), b_im.astype(F32))
    c_mat = lax.complex(c_re.astype(F32), c_im.astype(F32))
    steps = jnp.arange(chunk + 1, dtype=F32)
    pw = jnp.exp(lam_dt[:, None, :] * steps[None, :, None])
    npw = jnp.exp(-lam_dt[:, None, :] * steps[None, :chunk, None])

    def split_cols(zc):
        return jnp.concatenate([jnp.real(zc), jnp.imag(zc)], axis=-1).reshape(g, chunk * c, 2 * p)

    def split_rows(zc):
        m = jnp.concatenate([jnp.real(zc), -jnp.imag(zc)], axis=-1).reshape(g, chunk * c, 2 * p)
        return jnp.swapaxes(m, 1, 2)

    b_t = jnp.swapaxes(b_bar, 1, 2)
    src = split_cols(npw[:, :, None, :] * b_t[:, None, :, :])
    dst = split_rows(pw[:, :chunk, None, :] * c_mat[:, None, :, :])
    to_state = split_cols(pw[:, chunk - 1::-1, None, :][:, :chunk] * b_t[:, None, :, :])
    from_state = split_rows(pw[:, 1:, None, :] * c_mat[:, None, :, :])
    n_steps = max(1, (n_chunks - 1).bit_length())
    powers = []
    cur = pw[:, chunk, :]
    for _ in range(n_steps):
        powers.append(cur)
        cur = cur * cur
    lam_pow = jnp.stack(powers, axis=1)
    a1 = jnp.concatenate([jnp.real(lam_pow), jnp.real(lam_pow)], axis=-1)
    a2 = jnp.concatenate([-jnp.imag(lam_pow), jnp.imag(lam_pow)], axis=-1)
    return (src.astype(BF16), dst.astype(BF16), to_state.astype(BF16),
            from_state.astype(BF16), a1, a2)


def _ssm_kernel(u_ref, src_ref, dst_ref, ts_ref, fs_ref, a1_ref, a2_ref, y_ref, toep_ref,
                *, n_chunks, n_steps, group):
    lc = toep_ref.shape[0]
    p2 = ts_ref.shape[1]
    group_shift = group.bit_length() - 1
    cb = min(256, lc)
    for j in range(lc // cb):
        blk = jnp.dot(src_ref[...], dst_ref[:, j * cb:(j + 1) * cb], preferred_element_type=F32)
        s_idx = lax.broadcasted_iota(I32, (lc, cb), 0) >> group_shift
        t_idx = (lax.broadcasted_iota(I32, (lc, cb), 1) + j * cb) >> group_shift
        toep_ref[:, j * cb:(j + 1) * cb] = jnp.where(s_idx <= t_idx, blk, 0.0).astype(BF16)
    u = u_ref[...]
    y = jnp.dot(u, toep_ref[...], preferred_element_type=F32)
    z = jnp.dot(u, ts_ref[...], preferred_element_type=F32)
    n = lax.broadcasted_iota(I32, z.shape, 0) & (n_chunks - 1)
    x = jnp.where(n >= 1, pltpu.roll(z, 1, 0), 0.0)
    for k in range(n_steps):
        sh = 1 << k
        xs = jnp.where(n >= sh, pltpu.roll(x, sh, 0), 0.0)
        x = x + a1_ref[k:k + 1, :] * xs + a2_ref[k:k + 1, :] * pltpu.roll(xs, p2 // 2, 1)
    y = y + jnp.dot(x.astype(BF16), fs_ref[...], preferred_element_type=F32)
    y_ref[...] = y.astype(y_ref.dtype)


def _ssm(u, bsz, seq, lam_re, lam_im, log_dt, b_re, b_im, c_re, c_im):
    g, p = lam_re.shape
    c = b_re.shape[-1]
    chunk = min(SSM_CHUNK, seq)
    n_chunks = seq // chunk
    assert n_chunks & (n_chunks - 1) == 0 and c & (c - 1) == 0, (n_chunks, c)
    rows = bsz * n_chunks
    lc = chunk * c
    src, dst, to_state, from_state, a1, a2 = _ssm_tables(
        lam_re, lam_im, log_dt, b_re, b_im, c_re, c_im, chunk, n_chunks)
    n_steps = a1.shape[1]
    u_g = u.reshape(bsz, n_chunks, chunk, g, c).transpose(3, 0, 1, 2, 4).reshape(g, rows, lc)
    y_g = pl.pallas_call(
        functools.partial(_ssm_kernel, n_chunks=n_chunks, n_steps=n_steps, group=c),
        grid=(g,),
        in_specs=[
            pl.BlockSpec((None, rows, lc), lambda i: (i, 0, 0)),
            pl.BlockSpec((None, lc, 2 * p), lambda i: (i, 0, 0)),
            pl.BlockSpec((None, 2 * p, lc), lambda i: (i, 0, 0)),
            pl.BlockSpec((None, lc, 2 * p), lambda i: (i, 0, 0)),
            pl.BlockSpec((None, 2 * p, lc), lambda i: (i, 0, 0)),
            pl.BlockSpec((None, n_steps, 2 * p), lambda i: (i, 0, 0)),
            pl.BlockSpec((None, n_steps, 2 * p), lambda i: (i, 0, 0)),
        ],
        out_specs=pl.BlockSpec((None, rows, lc), lambda i: (i, 0, 0)),
        out_shape=jax.ShapeDtypeStruct((g, rows, lc), BF16),
        scratch_shapes=[pltpu.VMEM((lc, lc), BF16)],
        compiler_params=pltpu.CompilerParams(
            dimension_semantics=("arbitrary",), vmem_limit_bytes=40 * MIB),
        name="s5_chunked_scan",
    )(u_g, src, dst, to_state, from_state, a1, a2)
    return y_g.reshape(g, bsz, n_chunks, chunk, c).transpose(1, 2, 3, 0, 4).reshape(bsz * seq, g * c)


def _post_kernel(x_ref, ysb_ref, yss_ref, u_ref, d_ref, wglu_ref, gsb_ref, gssm_ref, wout_ref,
                 ln2_ref, wr_ref, br_ref, x1_ref, xn_ref, idx_ref, gate_ref, *, sb, n_exp):
    u = u_ref[...].astype(F32)
    y = yss_ref[...].astype(F32) + d_ref[...] * u
    y = y * (0.5 * (1.0 + jnp.tanh(math.sqrt(2.0 / math.pi) * (y + 0.044715 * (y * y * y)))))
    ab = jnp.dot(y.astype(BF16), wglu_ref[...], preferred_element_type=F32)
    w = ab.shape[1] // 2
    y_ssm = ab[:, :w] * (1.0 / (1.0 + jnp.exp(-ab[:, w:])))
    m_sb = _rms(ysb_ref[...].astype(F32), gsb_ref[...])
    m_ssm = _rms(y_ssm, gssm_ref[...])
    x1 = (x_ref[...]
          + jnp.dot(m_sb.astype(BF16), wout_ref[:sb, :], preferred_element_type=F32)
          + jnp.dot(m_ssm.astype(BF16), wout_ref[sb:, :], preferred_element_type=F32))
    x1_ref[...] = x1
    xn = _rms(x1, ln2_ref[...])
    xn_ref[...] = xn
    logits = jnp.dot(xn, wr_ref[...], preferred_element_type=F32,
                     precision=lax.Precision.HIGHEST) + br_ref[...]
    lane = lax.broadcasted_iota(I32, logits.shape, 1).astype(F32)
    out_lane = lax.broadcasted_iota(I32, (logits.shape[0], LANES), 1)
    idx_out = jnp.zeros((logits.shape[0], LANES), F32)
    val_out = jnp.zeros((logits.shape[0], LANES), F32)
    top = None
    denom = None
    work = logits
    for k in range(TOP_K):
        m = jnp.max(work, axis=-1, keepdims=True)
        sel = jnp.min(jnp.where(work == m, lane, float(n_exp)), axis=-1, keepdims=True)
        work = jnp.where(lane == sel, -jnp.inf, work)
        if k == 0:
            top = m
        e = jnp.exp(m - top)
        denom = e if denom is None else denom + e
        idx_out = jnp.where(out_lane == k, sel, idx_out)
        val_out = jnp.where(out_lane == k, e, val_out)
    idx_ref[...] = idx_out[:, :TOP_K].astype(I32)
    gate_ref[...] = (val_out / denom)[:, :TOP_K]


def _post(x2, y_sb, y_ss, u, ssm_d, w_glu, g_sb, g_ssm, w_out, ln2_g, w_router, b_router):
    t, d = x2.shape
    sb = y_sb.shape[1]
    w = y_ss.shape[1]
    n_exp = w_router.shape[1]
    tm = min(TOKEN_TILE, t)
    row = lambda i: (i, 0)
    fixed = lambda i: (0, 0)
    return pl.pallas_call(
        functools.partial(_post_kernel, sb=sb, n_exp=n_exp),
        grid=(t // tm,),
        in_specs=[
            pl.BlockSpec((tm, d), row),
            pl.BlockSpec((tm, sb), row),
            pl.BlockSpec((tm, w), row),
            pl.BlockSpec((tm, w), row),
            pl.BlockSpec((1, w), fixed),
            pl.BlockSpec((w, 2 * w), fixed),
            pl.BlockSpec((1, sb), fixed),
            pl.BlockSpec((1, w), fixed),
            pl.BlockSpec((sb + w, d), fixed),
            pl.BlockSpec((1, d), fixed),
            pl.BlockSpec((d, n_exp), fixed),
            pl.BlockSpec((1, n_exp), fixed),
        ],
        out_specs=[
            pl.BlockSpec((tm, d), row),
            pl.BlockSpec((tm, d), row),
            pl.BlockSpec((tm, TOP_K), row),
            pl.BlockSpec((tm, TOP_K), row),
        ],
        out_shape=[
            jax.ShapeDtypeStruct((t, d), F32),
            jax.ShapeDtypeStruct((t, d), F32),
            jax.ShapeDtypeStruct((t, TOP_K), I32),
            jax.ShapeDtypeStruct((t, TOP_K), F32),
        ],
        compiler_params=pltpu.CompilerParams(
            dimension_semantics=("arbitrary",), vmem_limit_bytes=48 * MIB),
        name="post_mixer_router",
    )(x2, y_sb, y_ss, u, ssm_d.reshape(1, w), w_glu.astype(BF16), g_sb.reshape(1, sb),
      g_ssm.reshape(1, w), w_out.astype(BF16), ln2_g.reshape(1, d), w_router,
      b_router.reshape(1, n_exp))


def _routing(top_idx, n_exp, rows_per_block):
    t, k = top_idx.shape
    n_assign = t * k
    e_flat = top_idx.reshape(n_assign)
    onehot = (e_flat[:, None] == jnp.arange(n_exp, dtype=I32)[None, :]).astype(I32)
    csum = jnp.cumsum(onehot, axis=0)
    counts = csum[-1]
    rank = jnp.take_along_axis(csum, e_flat[:, None], axis=1)[:, 0] - 1
    padded = ((counts + rows_per_block - 1) // rows_per_block) * rows_per_block
    pad_ends = jnp.cumsum(padded)
    pad_starts = pad_ends - padded
    dest = (pad_starts[e_flat] + rank).astype(I32)
    n_rows = n_assign + n_exp * rows_per_block
    n_blocks = n_rows // rows_per_block
    tok = jnp.arange(n_assign, dtype=I32) // k
    row_tok = jnp.zeros((n_rows,), I32).at[dest].set(tok)
    block_start = jnp.arange(n_blocks, dtype=I32) * rows_per_block
    block_expert = jnp.minimum(
        jnp.searchsorted(pad_ends, block_start, side="right"), n_exp - 1).astype(I32)
    n_used = (pad_ends[-1] // rows_per_block).astype(I32).reshape(1)
    return dest, row_tok, block_expert, n_used, n_blocks


def _gather_rows(idx_smem, idx_base, src_hbm, dst_vmem, sem, n_rows):
    def issue(r, carry):
        pltpu.make_async_copy(src_hbm.at[pl.ds(idx_smem[idx_base + r], 1), :],
                              dst_vmem.at[pl.ds(r, 1), :], sem).start()
        return carry
    lax.fori_loop(0, n_rows, issue, 0, unroll=8)


def _wait_rows(src_hbm, dst_vmem, sem, n_rows):
    pltpu.make_async_copy(src_hbm.at[pl.ds(0, n_rows), :], dst_vmem, sem).wait()


def _staged_gather_step(i, n, idx_hbm, src_hbm, idx_smem, buf, idx_sem, row_sem, n_rows):
    def slot_base(b):
        return pl.multiple_of((b % 3) * n_rows, n_rows)

    def idx_copy(b):
        src = idx_hbm.at[pl.ds(pl.multiple_of(b * n_rows, n_rows), n_rows)]
        return pltpu.make_async_copy(src, idx_smem.at[pl.ds(slot_base(b), n_rows)],
                                     idx_sem.at[b % 3])

    @pl.when(i == 0)
    def _():
        idx_copy(0).start()
        idx_copy(0).wait()
        _gather_rows(idx_smem, 0, src_hbm, buf.at[0], row_sem.at[0], n_rows)

        @pl.when(n > 1)
        def _():
            idx_copy(1).start()

    @pl.when(i + 1 < n)
    def _():
        idx_copy(i + 1).wait()

        @pl.when(i + 2 < n)
        def _():
            idx_copy(i + 2).start()
        nxt = (i + 1) % 2
        _gather_rows(idx_smem, slot_base(i + 1), src_hbm, buf.at[nxt], row_sem.at[nxt], n_rows)

    cur = i % 2
    _wait_rows(src_hbm, buf.at[cur], row_sem.at[cur], n_rows)
    return cur


def _expert_kernel(be_ref, nused_ref, rowtok_hbm, xn_hbm, wg_ref, bg_ref, wu_ref, bu_ref,
                   wd_ref, bd_ref, y_ref, idx_smem, xbuf, idx_sem, row_sem, wg_bf, wu_bf, wd_bf,
                   *, rows):
    i = pl.program_id(0)
    n = nused_ref[0]

    @pl.when(i < n)
    def _():
        cur = _staged_gather_step(i, n, rowtok_hbm, xn_hbm, idx_smem, xbuf, idx_sem, row_sem, rows)
        changed = jnp.logical_or(i == 0, be_ref[i] != be_ref[jnp.maximum(i - 1, 0)])

        @pl.when(changed)
        def _():
            wg_bf[...] = wg_ref[...].astype(BF16)
            wu_bf[...] = wu_ref[...].astype(BF16)
            wd_bf[...] = wd_ref[...].astype(BF16)

        x = xbuf[cur].astype(BF16)
        gate = jnp.dot(x, wg_bf[...], preferred_element_type=F32) + bg_ref[...]
        up = jnp.dot(x, wu_bf[...], preferred_element_type=F32) + bu_ref[...]
        gate = jnp.minimum(gate, SWIGLU_LIMIT)
        up = jnp.clip(up, -SWIGLU_LIMIT, SWIGLU_LIMIT)
        glu = gate * (1.0 / (1.0 + jnp.exp(-SWIGLU_ALPHA * gate)))
        hidden = ((up + 1.0) * glu).astype(BF16)
        y_ref[...] = jnp.dot(hidden, wd_bf[...], preferred_element_type=F32) + bd_ref[...]

    @pl.when(i >= n)
    def _():
        y_ref[...] = jnp.zeros_like(y_ref)


def _experts(xn, row_tok, block_expert, n_used, n_blocks, w_gate, b_gate, w_up, b_up,
             w_down, b_down):
    t, d = xn.shape
    n_exp, _, f = w_gate.shape
    rows = EXPERT_ROWS
    wmap = lambda i, be, nu: (be[i], 0, 0)
    grid_spec = pltpu.PrefetchScalarGridSpec(
        num_scalar_prefetch=2,
        grid=(n_blocks,),
        in_specs=[
            pl.BlockSpec(memory_space=pl.ANY),
            pl.BlockSpec(memory_space=pl.ANY),
            pl.BlockSpec((None, d, f), wmap),
            pl.BlockSpec((None, 1, f), wmap),
            pl.BlockSpec((None, d, f), wmap),
            pl.BlockSpec((None, 1, f), wmap),
            pl.BlockSpec((None, f, d), wmap),
            pl.BlockSpec((None, 1, d), wmap),
        ],
        out_specs=pl.BlockSpec((rows, d), lambda i, be, nu: (i, 0)),
        scratch_shapes=[
            pltpu.SMEM((3 * rows,), I32),
            pltpu.VMEM((2, rows, d), F32),
            pltpu.SemaphoreType.DMA((3,)),
            pltpu.SemaphoreType.DMA((2,)),
            pltpu.VMEM((d, f), BF16),
            pltpu.VMEM((d, f), BF16),
            pltpu.VMEM((f, d), BF16),
        ],
    )
    return pl.pallas_call(
        functools.partial(_expert_kernel, rows=rows),
        grid_spec=grid_spec,
        out_shape=jax.ShapeDtypeStruct((n_blocks * rows, d), F32),
        compiler_params=pltpu.CompilerParams(
            dimension_semantics=("arbitrary",), vmem_limit_bytes=52 * MIB),
        name="moe_experts",
    )(block_expert, n_used, row_tok, xn,
      w_gate, b_gate.reshape(n_exp, 1, f), w_up, b_up.reshape(n_exp, 1, f),
      w_down, b_down.reshape(n_exp, 1, d))


def _combine_kernel(dest_hbm, y_hbm, x1_ref, gate_ref, lnf_ref, o_ref, idx_smem, ybuf,
                    idx_sem, row_sem, *, tile):
    i = pl.program_id(0)
    n = pl.num_programs(0)
    cur = _staged_gather_step(i, n, dest_hbm, y_hbm, idx_smem, ybuf, idx_sem, row_sem,
                              TOP_K * tile)
    gates = gate_ref[...]
    x = x1_ref[...]
    for k in range(TOP_K):
        x = x + gates[:, k:k + 1] * ybuf[cur, k * tile:(k + 1) * tile, :]
    o_ref[...] = _rms(x, lnf_ref[...])


def _combine(dest, y_rows, x1, gates, ln_f_g):
    t, d = x1.shape
    tile = min(COMBINE_TILE, t)
    n_tiles = t // tile
    dest_tiles = dest.reshape(n_tiles, tile, TOP_K).transpose(0, 2, 1).reshape(n_tiles * TOP_K * tile)
    return pl.pallas_call(
        functools.partial(_combine_kernel, tile=tile),
        grid=(n_tiles,),
        in_specs=[
            pl.BlockSpec(memory_space=pl.ANY),
            pl.BlockSpec(memory_space=pl.ANY),
            pl.BlockSpec((tile, d), lambda i: (i, 0)),
            pl.BlockSpec((tile, TOP_K), lambda i: (i, 0)),
            pl.BlockSpec((1, d), lambda i: (0, 0)),
        ],
        out_specs=pl.BlockSpec((tile, d), lambda i: (i, 0)),
        out_shape=jax.ShapeDtypeStruct((t, d), F32),
        scratch_shapes=[
            pltpu.SMEM((3 * TOP_K * tile,), I32),
            pltpu.VMEM((2, TOP_K * tile, d), F32),
            pltpu.SemaphoreType.DMA((3,)),
            pltpu.SemaphoreType.DMA((2,)),
        ],
        compiler_params=pltpu.CompilerParams(
            dimension_semantics=("arbitrary",), vmem_limit_bytes=40 * MIB),
        name="moe_combine",
    )(dest_tiles, y_rows, x1, gates, ln_f_g.reshape(1, d))


def kernel(x, ln1_g, w_in, lam_re, lam_im, log_dt, ssm_b_re, ssm_b_im, ssm_c_re, ssm_c_im,
           ssm_d, w_glu, g_sb, g_ssm, w_out, ln2_g, w_router, b_router, w_gate, b_gate,
           w_up, b_up, w_down, b_down, ln_f_g):
    bsz, seq, d = x.shape
    assert ln1_g.shape[0] == 1, "depth-1 block only"
    ssm_w = ssm_d.shape[1]
    sb = g_sb.shape[1]
    n_exp = w_router.shape[2]
    x2 = x.reshape(bsz * seq, d)
    q, k, v, u = _in_proj(x2, ln1_g[0], w_in[0], sb, ssm_w)
    y_sb = _attention(q, k, v, bsz, seq)
    y_ss = _ssm(u, bsz, seq, lam_re[0], lam_im[0], log_dt[0], ssm_b_re[0], ssm_b_im[0],
                ssm_c_re[0], ssm_c_im[0])
    x1, xn, top_idx, gates = _post(x2, y_sb, y_ss, u, ssm_d[0], w_glu[0], g_sb[0], g_ssm[0],
                                   w_out[0], ln2_g[0], w_router[0], b_router[0])
    dest, row_tok, block_expert, n_used, n_blocks = _routing(top_idx, n_exp, EXPERT_ROWS)
    y_rows = _experts(xn, row_tok, block_expert, n_used, n_blocks, w_gate[0], b_gate[0],
                      w_up[0], b_up[0], w_down[0], b_down[0])
    out = _combine(dest, y_rows, x1, gates, ln_f_g)
    return out.reshape(bsz, seq, d)
```

```python
import functools
import math

import jax
import jax.numpy as jnp
from jax import lax
from jax.experimental import pallas as pl
from jax.experimental.pallas import tpu as pltpu

F32 = jnp.float32
BF16 = jnp.bfloat16
I32 = jnp.int32

EPS = 1e-5
SB_HEAD_DIM = 64
SSM_GROUP = 16
SSM_STATE = 64
TOP_K = 4
SWIGLU_LIMIT = 7.0
SWIGLU_ALPHA = 1.702

LANES = 128
SUBLANES = 8
HEADS_PER_BLOCK = LANES // SB_HEAD_DIM
ATTN_BLOCK = 256
ATTN_UNROLL = 4
SSM_CHUNK = 64
TOKEN_TILE = 512
EXPERT_ROWS = 256
COMBINE_TILE = 256
DISPATCH_TILE = 1024
MIB = 1024 * 1024
LOG2E = 1.4426950408889634


def _rms(x, g):
    return x * lax.rsqrt(jnp.mean(x * x, axis=-1, keepdims=True) + EPS) * g


def _in_proj_kernel(x_ref, g_ref, w_ref, q_ref, k_ref, v_ref, u_ref, *, sb, scale):
    h = _rms(x_ref[...], g_ref[...])
    proj = jnp.dot(h.astype(BF16), w_ref[...], preferred_element_type=F32)
    q_ref[...] = (proj[:, :sb] * scale).astype(BF16)
    k_ref[...] = proj[:, sb:2 * sb].astype(BF16)
    v_ref[...] = proj[:, 2 * sb:3 * sb].astype(BF16)
    u_ref[...] = proj[:, 3 * sb:].astype(BF16)


def _in_proj(x2, ln1_g, w_in, sb, ssm_w):
    t, d = x2.shape
    n_in = w_in.shape[1]
    tm = min(TOKEN_TILE, t)
    scale = LOG2E / math.sqrt(SB_HEAD_DIM)
    return pl.pallas_call(
        functools.partial(_in_proj_kernel, sb=sb, scale=scale),
        grid=(t // tm,),
        in_specs=[
            pl.BlockSpec((tm, d), lambda i: (i, 0)),
            pl.BlockSpec((1, d), lambda i: (0, 0)),
            pl.BlockSpec((d, n_in), lambda i: (0, 0)),
        ],
        out_specs=[
            pl.BlockSpec((tm, sb), lambda i: (i, 0)),
            pl.BlockSpec((tm, sb), lambda i: (i, 0)),
            pl.BlockSpec((tm, sb), lambda i: (i, 0)),
            pl.BlockSpec((tm, ssm_w), lambda i: (i, 0)),
        ],
        out_shape=[
            jax.ShapeDtypeStruct((t, sb), BF16),
            jax.ShapeDtypeStruct((t, sb), BF16),
            jax.ShapeDtypeStruct((t, sb), BF16),
            jax.ShapeDtypeStruct((t, ssm_w), BF16),
        ],
        compiler_params=pltpu.CompilerParams(
            dimension_semantics=("arbitrary",), vmem_limit_bytes=40 * MIB),
        name="in_proj",
    )(x2, ln1_g.reshape(1, d), w_in.astype(BF16))


def _attn_kernel(ti_ref, tj_ref, q_ref, k_ref, v_ref, tri_ref, mask_ref, o_ref,
                 wbuf, dbuf, spbuf, lbuf, rsbuf, acc_ref, r_ref, *, blk, n_items):
    lane = lax.broadcasted_iota(I32, (1, LANES), 1)
    head_mask = [lane < SB_HEAD_DIM, lane >= SB_HEAD_DIM]
    zero = jnp.zeros((), BF16)
    sign_bit = jnp.uint32(0x80000000)
    last = n_items - 1
    ALL, CAUSAL, NONE = 0, 1, 2

    for ref in (wbuf, dbuf, spbuf, lbuf, rsbuf, acc_ref, r_ref):
        ref[...] = jnp.zeros_like(ref)

    def both_heads(x):
        return jnp.concatenate([jnp.where(m, x, zero) for m in head_mask], axis=0)

    def item(idx):
        c = jnp.clip(idx, 0, last)
        return ti_ref[c], tj_ref[c]

    def step(p, slot):
        other = 1 - slot
        i1, j1 = item(p)
        q2 = q_ref[pl.ds(pl.multiple_of(i1 * blk, blk), blk), :]
        ks = k_ref[pl.ds(pl.multiple_of(j1 * blk, blk), blk), :]
        wbuf[slot] = lax.dot_general(both_heads(q2), ks, (((1,), (1,)), ((), ())),
                                     preferred_element_type=F32)
        spb = spbuf[other]
        later = jnp.dot(spb, tri_ref[...], preferred_element_type=F32)
        lbuf[slot] = later
        rsbuf[slot] = later[:, 0:1] + spb[:, 0:1].astype(F32)
        i4, j4 = item(p - 3)
        valid = p - 3 <= last
        first = jnp.logical_and(i4 == j4, valid)
        mask4 = mask_ref[jnp.where(valid, (i4 == j4).astype(I32), NONE)]
        vs = v_ref[pl.ds(pl.multiple_of(j4 * blk, blk), blk), :]
        r_prev = jnp.where(first, 0.0, r_ref[...])
        a = jnp.exp2(dbuf[other] - (lbuf[other] + r_prev))
        ab = a.astype(BF16) * jnp.concatenate([mask4, mask4], axis=0)
        r_ref[...] = r_prev + rsbuf[other]
        a_cat = jnp.concatenate([ab[:blk], ab[blk:]], axis=1)
        acc = jnp.where(first, 0.0, acc_ref[...]) + jnp.dot(
            a_cat, both_heads(vs), preferred_element_type=F32)
        acc_ref[...] = acc
        o_ref[pl.ds(pl.multiple_of(i4 * blk, blk), blk), :] = acc.astype(o_ref.dtype)
        i2, j2 = item(p - 1)
        mask2 = mask_ref[(i2 == j2).astype(I32)]
        w = wbuf[other]
        neg_abs = lax.bitcast_convert_type(
            lax.bitcast_convert_type(w, jnp.uint32) | sign_bit, F32)
        sp2 = jnp.maximum(w, 0.0) + jnp.log(1.0 + jnp.exp2(neg_abs)) * LOG2E
        dbuf[other] = w - sp2
        spbuf[slot] = sp2.astype(BF16) * jnp.concatenate([mask2, mask2], axis=0)

    def unrolled(it, carry):
        for s in range(ATTN_UNROLL):
            step(ATTN_UNROLL * it + s, s % 2)
        return carry

    lax.fori_loop(0, pl.cdiv(n_items + 3, ATTN_UNROLL), unrolled, 0)


def _attention(q, k, v, bsz, seq):
    sb = q.shape[-1]
    blk = min(ATTN_BLOCK, seq)
    n_blk = seq // blk
    n_pairs = sb // LANES
    q3, k3, v3 = (a.reshape(bsz, seq, sb) for a in (q, k, v))
    jj = lax.broadcasted_iota(I32, (blk, blk), 0)
    ss = lax.broadcasted_iota(I32, (blk, blk), 1)
    tri = (jj > ss).astype(BF16)
    masks = jnp.stack([jnp.ones((blk, blk), BF16), (ss < jj).astype(BF16),
                       jnp.zeros((blk, blk), BF16)])
    items = [(i, j) for i in range(n_blk) for j in range(i, -1, -1)]
    item_i = jnp.asarray([i for i, _ in items], I32)
    item_j = jnp.asarray([j for _, j in items], I32)
    whole = lambda b, p, ti, tj: (b, 0, p)
    grid_spec = pltpu.PrefetchScalarGridSpec(
        num_scalar_prefetch=2,
        grid=(bsz, n_pairs),
        in_specs=[
            pl.BlockSpec((None, seq, LANES), whole),
            pl.BlockSpec((None, seq, LANES), whole),
            pl.BlockSpec((None, seq, LANES), whole),
            pl.BlockSpec((blk, blk), lambda b, p, ti, tj: (0, 0)),
            pl.BlockSpec((3, blk, blk), lambda b, p, ti, tj: (0, 0, 0)),
        ],
        out_specs=pl.BlockSpec((None, seq, LANES), whole),
        scratch_shapes=[
            pltpu.VMEM((2, HEADS_PER_BLOCK * blk, blk), F32),
            pltpu.VMEM((2, HEADS_PER_BLOCK * blk, blk), F32),
            pltpu.VMEM((2, HEADS_PER_BLOCK * blk, blk), BF16),
            pltpu.VMEM((2, HEADS_PER_BLOCK * blk, blk), F32),
            pltpu.VMEM((2, HEADS_PER_BLOCK * blk, 1), F32),
            pltpu.VMEM((blk, LANES), F32),
            pltpu.VMEM((HEADS_PER_BLOCK * blk, 1), F32),
        ],
    )
    out = pl.pallas_call(
        functools.partial(_attn_kernel, blk=blk, n_items=len(items)),
        grid_spec=grid_spec,
        out_shape=jax.ShapeDtypeStruct((bsz, seq, sb), BF16),
        compiler_params=pltpu.CompilerParams(
            dimension_semantics=("arbitrary", "arbitrary"), vmem_limit_bytes=40 * MIB),
        name="sb_attention",
    )(item_i, item_j, q3, k3, v3, tri, masks)
    return out.reshape(bsz * seq, sb)


def _ssm_tables(lam_re, lam_im, log_dt, b_re, b_im, c_re, c_im, chunk, n_chunks):
    g, p = lam_re.shape
    c = b_re.shape[-1]
    lam = lax.complex(lam_re.astype(F32), lam_im.astype(F32))
    dt = jnp.exp(log_dt.astype(F32))[:, None]
    lam_dt = lam * dt
    lam_bar = jnp.exp(lam_dt)
    b_bar = ((lam_bar - 1.0) / lam)[:, :, None] * lax.complex(b_re.astype(F32), b_im.astype(F32))
    c_mat = lax.complex(c_re.astype(F32), c_im.astype(F32))
    steps = jnp.arange(chunk + 1, dtype=F32)
    pw = jnp.exp(lam_dt[:, None, :] * steps[None, :, None])
    npw = jnp.exp(-lam_dt[:, None, :] * steps[None, :chunk, None])

    def split_cols(zc):
        return jnp.concatenate([jnp.real(zc), jnp.imag(zc)], axis=-1).reshape(g, chunk * c, 2 * p)

    def split_rows(zc):
        m = jnp.concatenate([jnp.real(zc), -jnp.imag(zc)], axis=-1).reshape(g, chunk * c, 2 * p)
        return jnp.swapaxes(m, 1, 2)

    b_t = jnp.swapaxes(b_bar, 1, 2)
    src = split_cols(npw[:, :, None, :] * b_t[:, None, :, :])
    dst = split_rows(pw[:, :chunk, None, :] * c_mat[:, None, :, :])
    to_state = split_cols(pw[:, chunk - 1::-1, None, :] * b_t[:, None, :, :])
    from_state = split_rows(pw[:, 1:, None, :] * c_mat[:, None, :, :])
    n_steps = max(1, (n_chunks - 1).bit_length())
    powers = []
    cur = pw[:, chunk, :]
    for _ in range(n_steps):
        powers.append(cur)
        cur = cur * cur
    lam_pow = jnp.stack(powers, axis=1)
    a1 = jnp.concatenate([jnp.real(lam_pow), jnp.real(lam_pow)], axis=-1)
    a2 = jnp.concatenate([-jnp.imag(lam_pow), jnp.imag(lam_pow)], axis=-1)
    return (src.astype(BF16), dst.astype(BF16), to_state.astype(BF16),
            from_state.astype(BF16), a1, a2)


def _ssm_kernel(u_ref, src_ref, dst_ref, ts_ref, fs_ref, a1_ref, a2_ref, y_ref, toep_ref,
                *, n_chunks, n_steps, group):
    lc = toep_ref.shape[0]
    p2 = ts_ref.shape[1]
    group_shift = group.bit_length() - 1
    cb = min(256, lc)
    for j in range(lc // cb):
        blk = jnp.dot(src_ref[...], dst_ref[:, j * cb:(j + 1) * cb], preferred_element_type=F32)
        s_idx = lax.broadcasted_iota(I32, (lc, cb), 0) >> group_shift
        t_idx = (lax.broadcasted_iota(I32, (lc, cb), 1) + j * cb) >> group_shift
        toep_ref[:, j * cb:(j + 1) * cb] = jnp.where(s_idx <= t_idx, blk, 0.0).astype(BF16)
    u = u_ref[...]
    y = jnp.dot(u, toep_ref[...], preferred_element_type=F32)
    z = jnp.dot(u, ts_ref[...], preferred_element_type=F32)
    n = lax.broadcasted_iota(I32, z.shape, 0) & (n_chunks - 1)
    x = jnp.where(n >= 1, pltpu.roll(z, 1, 0), 0.0)
    for k in range(n_steps):
        sh = 1 << k
        xs = jnp.where(n >= sh, pltpu.roll(x, sh, 0), 0.0)
        x = x + a1_ref[k:k + 1, :] * xs + a2_ref[k:k + 1, :] * pltpu.roll(xs, p2 // 2, 1)
    y = y + jnp.dot(x.astype(BF16), fs_ref[...], preferred_element_type=F32)
    y_ref[...] = y.astype(y_ref.dtype)


def _ssm(u, bsz, seq, lam_re, lam_im, log_dt, b_re, b_im, c_re, c_im):
    g, p = lam_re.shape
    c = b_re.shape[-1]
    chunk = min(SSM_CHUNK, seq)
    n_chunks = seq // chunk
    assert n_chunks & (n_chunks - 1) == 0 and c & (c - 1) == 0, (n_chunks, c)
    rows = bsz * n_chunks
    lc = chunk * c
    src, dst, to_state, from_state, a1, a2 = _ssm_tables(
        lam_re, lam_im, log_dt, b_re, b_im, c_re, c_im, chunk, n_chunks)
    n_steps = a1.shape[1]
    u_g = u.reshape(bsz, n_chunks, chunk, g, c).transpose(3, 0, 1, 2, 4).reshape(g, rows, lc)
    y_g = pl.pallas_call(
        functools.partial(_ssm_kernel, n_chunks=n_chunks, n_steps=n_steps, group=c),
        grid=(g,),
        in_specs=[
            pl.BlockSpec((None, rows, lc), lambda i: (i, 0, 0)),
            pl.BlockSpec((None, lc, 2 * p), lambda i: (i, 0, 0)),
            pl.BlockSpec((None, 2 * p, lc), lambda i: (i, 0, 0)),
            pl.BlockSpec((None, lc, 2 * p), lambda i: (i, 0, 0)),
            pl.BlockSpec((None, 2 * p, lc), lambda i: (i, 0, 0)),
            pl.BlockSpec((None, n_steps, 2 * p), lambda i: (i, 0, 0)),
            pl.BlockSpec((None, n_steps, 2 * p), lambda i: (i, 0, 0)),
        ],
        out_specs=pl.BlockSpec((None, rows, lc), lambda i: (i, 0, 0)),
        out_shape=jax.ShapeDtypeStruct((g, rows, lc), BF16),
        scratch_shapes=[pltpu.VMEM((lc, lc), BF16)],
        compiler_params=pltpu.CompilerParams(
            dimension_semantics=("arbitrary",), vmem_limit_bytes=40 * MIB),
        name="s5_chunked_scan",
    )(u_g, src, dst, to_state, from_state, a1, a2)
    return y_g.reshape(g, bsz, n_chunks, chunk, c).transpose(1, 2, 3, 0, 4).reshape(bsz * seq, g * c)


def _post_kernel(x_ref, ysb_ref, yss_ref, u_ref, d_ref, wglu_ref, gsb_ref, gssm_ref, wout_ref,
                 ln2_ref, wr_ref, br_ref, tri_ref, x1_ref, xn_ref, idx_ref, gate_ref, rank_ref,
                 count_ref, running_ref, *, sb, n_exp):
    @pl.when(pl.program_id(0) == 0)
    def _():
        running_ref[...] = jnp.zeros_like(running_ref)

    u = u_ref[...].astype(F32)
    y = yss_ref[...].astype(F32) + d_ref[...] * u
    y = y * (0.5 * (1.0 + jnp.tanh(math.sqrt(2.0 / math.pi) * (y + 0.044715 * (y * y * y)))))
    ab = jnp.dot(y.astype(BF16), wglu_ref[...], preferred_element_type=F32)
    w = ab.shape[1] // 2
    y_ssm = ab[:, :w] * (1.0 / (1.0 + jnp.exp(-ab[:, w:])))
    m_sb = _rms(ysb_ref[...].astype(F32), gsb_ref[...])
    m_ssm = _rms(y_ssm, gssm_ref[...])
    x1 = (x_ref[...]
          + jnp.dot(m_sb.astype(BF16), wout_ref[:sb, :], preferred_element_type=F32)
          + jnp.dot(m_ssm.astype(BF16), wout_ref[sb:, :], preferred_element_type=F32))
    x1_ref[...] = x1
    xn = _rms(x1, ln2_ref[...])
    xn_ref[...] = xn
    logits = jnp.dot(xn, wr_ref[...], preferred_element_type=F32,
                     precision=lax.Precision.HIGHEST) + br_ref[...]
    lane = lax.broadcasted_iota(I32, logits.shape, 1).astype(F32)
    out_lane = lax.broadcasted_iota(I32, (logits.shape[0], LANES), 1)
    idx_out = jnp.zeros((logits.shape[0], LANES), F32)
    val_out = jnp.zeros((logits.shape[0], LANES), F32)
    top = None
    denom = None
    work = logits
    chosen = []
    for k in range(TOP_K):
        m = jnp.max(work, axis=-1, keepdims=True)
        sel = jnp.min(jnp.where(work == m, lane, float(n_exp)), axis=-1, keepdims=True)
        hit = lane == sel
        chosen.append(hit)
        work = jnp.where(hit, -jnp.inf, work)
        if k == 0:
            top = m
        e = jnp.exp(m - top)
        denom = e if denom is None else denom + e
        idx_out = jnp.where(out_lane == k, sel, idx_out)
        val_out = jnp.where(out_lane == k, e, val_out)
    idx_ref[...] = idx_out[:, :TOP_K].astype(I32)
    gate_ref[...] = (val_out / denom)[:, :TOP_K]
    member = functools.reduce(jnp.logical_or, chosen).astype(BF16)
    before = running_ref[...] + jnp.dot(tri_ref[...], member, preferred_element_type=F32)
    rank_out = jnp.zeros((logits.shape[0], LANES), F32)
    for k in range(TOP_K):
        rk = jnp.sum(jnp.where(chosen[k], before, 0.0), axis=-1, keepdims=True)
        rank_out = jnp.where(out_lane == k, rk, rank_out)
    rank_ref[...] = rank_out[:, :TOP_K].astype(I32)
    running = running_ref[...] + jnp.sum(member.astype(F32), axis=0, keepdims=True)
    running_ref[...] = running
    count_ref[...] = running


def _post(x2, y_sb, y_ss, u, ssm_d, w_glu, g_sb, g_ssm, w_out, ln2_g, w_router, b_router):
    t, d = x2.shape
    sb = y_sb.shape[1]
    w = y_ss.shape[1]
    n_exp = w_router.shape[1]
    tm = min(TOKEN_TILE, t)
    row = lambda i: (i, 0)
    fixed = lambda i: (0, 0)
    earlier = (lax.broadcasted_iota(I32, (tm, tm), 1)
               < lax.broadcasted_iota(I32, (tm, tm), 0)).astype(BF16)
    return pl.pallas_call(
        functools.partial(_post_kernel, sb=sb, n_exp=n_exp),
        grid=(t // tm,),
        in_specs=[
            pl.BlockSpec((tm, d), row),
            pl.BlockSpec((tm, sb), row),
            pl.BlockSpec((tm, w), row),
            pl.BlockSpec((tm, w), row),
            pl.BlockSpec((1, w), fixed),
            pl.BlockSpec((w, 2 * w), fixed),
            pl.BlockSpec((1, sb), fixed),
            pl.BlockSpec((1, w), fixed),
            pl.BlockSpec((sb + w, d), fixed),
            pl.BlockSpec((1, d), fixed),
            pl.BlockSpec((d, n_exp), fixed),
            pl.BlockSpec((1, n_exp), fixed),
            pl.BlockSpec((tm, tm), fixed),
        ],
        out_specs=[
            pl.BlockSpec((tm, d), row),
            pl.BlockSpec((tm, d), row),
            pl.BlockSpec((tm, TOP_K), row),
            pl.BlockSpec((tm, TOP_K), row),
            pl.BlockSpec((tm, TOP_K), row),
            pl.BlockSpec((1, n_exp), fixed),
        ],
        out_shape=[
            jax.ShapeDtypeStruct((t, d), F32),
            jax.ShapeDtypeStruct((t, d), F32),
            jax.ShapeDtypeStruct((t, TOP_K), I32),
            jax.ShapeDtypeStruct((t, TOP_K), F32),
            jax.ShapeDtypeStruct((t, TOP_K), I32),
            jax.ShapeDtypeStruct((1, n_exp), F32),
        ],
        scratch_shapes=[pltpu.VMEM((1, n_exp), F32)],
        compiler_params=pltpu.CompilerParams(
            dimension_semantics=("arbitrary",), vmem_limit_bytes=48 * MIB),
        name="post_mixer_router",
    )(x2, y_sb, y_ss, u, ssm_d.reshape(1, w), w_glu.astype(BF16), g_sb.reshape(1, sb),
      g_ssm.reshape(1, w), w_out.astype(BF16), ln2_g.reshape(1, d), w_router,
      b_router.reshape(1, n_exp), earlier)


def _plan(top_idx, rank, counts, rows_per_block):
    t, k = top_idx.shape
    n_exp = counts.shape[-1]
    counts = counts.reshape(n_exp).astype(I32)
    padded = ((counts + rows_per_block - 1) // rows_per_block) * rows_per_block
    pad_ends = jnp.cumsum(padded)
    pad_starts = pad_ends - padded
    dest = (pad_starts[top_idx] + rank).astype(I32)
    n_rows = t * k + n_exp * rows_per_block
    n_blocks = n_rows // rows_per_block
    block_start = jnp.arange(n_blocks, dtype=I32) * rows_per_block
    block_expert = jnp.minimum(
        jnp.searchsorted(pad_ends, block_start, side="right"), n_exp - 1).astype(I32)
    n_used = (pad_ends[-1] // rows_per_block).astype(I32).reshape(1)
    fill_start = jnp.concatenate([jnp.maximum(pad_ends - rows_per_block, 0).astype(I32), n_used])
    return dest, block_expert, n_used, fill_start, n_blocks


def _dispatch_kernel(fill_ref, dest_hbm, xn_ref, xs_hbm, idx_smem, zeros_ref, idx_sem, row_sem,
                     fill_sem, *, tile, n_exp, fill_rows):
    i = pl.program_id(0)
    n = pl.num_programs(0)
    per_tile = TOP_K * tile

    def idx_copy(b):
        src = dest_hbm.at[pl.ds(pl.multiple_of(b * per_tile, per_tile), per_tile)]
        dst = idx_smem.at[pl.ds(pl.multiple_of((b % 2) * per_tile, per_tile), per_tile)]
        return pltpu.make_async_copy(src, dst, idx_sem.at[b % 2])

    @pl.when(i == 0)
    def _():
        idx_copy(0).start()
        zeros_ref[...] = jnp.zeros_like(zeros_ref)
        for e in range(n_exp):
            start = pl.multiple_of(fill_ref[e], 8)
            pltpu.make_async_copy(zeros_ref, xs_hbm.at[pl.ds(start, fill_rows), 0, :],
                                  fill_sem).start()
        for e in range(n_exp):
            pltpu.make_async_copy(zeros_ref, xs_hbm.at[pl.ds(0, fill_rows), 0, :],
                                  fill_sem).wait()
        n_blocks = xs_hbm.shape[0] // fill_rows

        def fill_unused(b, carry):
            cp = pltpu.make_async_copy(
                zeros_ref, xs_hbm.at[pl.ds(pl.multiple_of(b * fill_rows, fill_rows), fill_rows), 0, :],
                fill_sem)
            cp.start()
            cp.wait()
            return carry

        lax.fori_loop(fill_ref[n_exp], n_blocks, fill_unused, 0)

    idx_copy(i).wait()

    @pl.when(i + 1 < n)
    def _():
        idx_copy(i + 1).start()

    base = pl.multiple_of((i % 2) * per_tile, per_tile)
    quarter = tile // TOP_K
    for part in range(TOP_K):
        def issue(g, carry, part=part):
            group = part * (quarter // SUBLANES) + g
            for s in range(SUBLANES):
                for k in range(TOP_K):
                    dst_row = idx_smem[base + (group * SUBLANES + s) * TOP_K + k]
                    pltpu.make_async_copy(xn_ref.at[group, pl.ds(s, 1), :], xs_hbm.at[dst_row],
                                          row_sem.at[part]).start()
            return carry

        lax.fori_loop(0, quarter // SUBLANES, issue, 0)
    for part in range(TOP_K):
        pltpu.make_async_copy(xn_ref, xn_ref, row_sem.at[part]).wait()


def _dispatch(xn, dest, fill_start, n_rows):
    t, d = xn.shape
    tile = min(DISPATCH_TILE, t)
    n_exp = fill_start.shape[0] - 1
    grid_spec = pltpu.PrefetchScalarGridSpec(
        num_scalar_prefetch=1,
        grid=(t // tile,),
        in_specs=[
            pl.BlockSpec(memory_space=pl.ANY),
            pl.BlockSpec((tile // SUBLANES, SUBLANES, d), lambda i, fs: (i, 0, 0)),
        ],
        out_specs=pl.BlockSpec(memory_space=pl.ANY),
        scratch_shapes=[
            pltpu.SMEM((2 * TOP_K * tile,), I32),
            pltpu.VMEM((EXPERT_ROWS, d), F32),
            pltpu.SemaphoreType.DMA((2,)),
            pltpu.SemaphoreType.DMA((TOP_K,)),
            pltpu.SemaphoreType.DMA,
        ],
    )
    return pl.pallas_call(
        functools.partial(_dispatch_kernel, tile=tile, n_exp=n_exp, fill_rows=EXPERT_ROWS),
        grid_spec=grid_spec,
        out_shape=jax.ShapeDtypeStruct((n_rows, 1, d), F32),
        compiler_params=pltpu.CompilerParams(
            dimension_semantics=("arbitrary",), vmem_limit_bytes=40 * MIB),
        name="moe_dispatch",
    )(fill_start, dest.reshape(t * TOP_K), xn.reshape(t // SUBLANES, SUBLANES, d))


def _expert_kernel(be_ref, nused_ref, xs_hbm, wg_ref, bg_ref, wu_ref, bu_ref, wd_ref, bd_ref,
                   y_hbm, xbuf, ybuf, x_sem, y_sem, wg_bf, wu_bf, wd_bf, *, rows):
    i = pl.program_id(0)
    n = nused_ref[0]

    def x_copy(b):
        return pltpu.make_async_copy(
            xs_hbm.at[pl.ds(pl.multiple_of(b * rows, rows), rows), 0, :], xbuf.at[b % 2],
            x_sem.at[b % 2])

    def y_copy(b):
        return pltpu.make_async_copy(
            ybuf.at[b % 2], y_hbm.at[pl.ds(pl.multiple_of(b * rows, rows), rows), 0, :],
            y_sem.at[b % 2])

    @pl.when(i < n)
    def _():
        @pl.when(i == 0)
        def _():
            x_copy(0).start()

        @pl.when(i + 1 < n)
        def _():
            x_copy(i + 1).start()

        changed = jnp.logical_or(i == 0, be_ref[i] != be_ref[jnp.maximum(i - 1, 0)])

        @pl.when(changed)
        def _():
            wg_bf[...] = wg_ref[...].astype(BF16)
            wu_bf[...] = wu_ref[...].astype(BF16)
            wd_bf[...] = wd_ref[...].astype(BF16)

        x_copy(i).wait()
        x = xbuf[i % 2].astype(BF16)
        gate = jnp.dot(x, wg_bf[...], preferred_element_type=F32) + bg_ref[...]
        up = jnp.dot(x, wu_bf[...], preferred_element_type=F32) + bu_ref[...]
        gate = jnp.minimum(gate, SWIGLU_LIMIT)
        up = jnp.clip(up, -SWIGLU_LIMIT, SWIGLU_LIMIT)
        glu = gate * (1.0 / (1.0 + jnp.exp(-SWIGLU_ALPHA * gate)))
        hidden = ((up + 1.0) * glu).astype(BF16)
        y = jnp.dot(hidden, wd_bf[...], preferred_element_type=F32) + bd_ref[...]

        @pl.when(i >= 2)
        def _():
            y_copy(i - 2).wait()
        ybuf[i % 2] = y
        y_copy(i).start()

        @pl.when(i == n - 1)
        def _():
            y_copy(i).wait()

            @pl.when(i >= 1)
            def _():
                y_copy(i - 1).wait()

    @pl.when(i >= n)
    def _():
        ybuf[i % 2] = jnp.zeros((rows, ybuf.shape[2]), F32)
        y_copy(i).start()
        y_copy(i).wait()


def _experts(x_sorted, block_expert, n_used, n_blocks, w_gate, b_gate, w_up, b_up, w_down, b_down):
    n_rows, _, d = x_sorted.shape
    n_exp, _, f = w_gate.shape
    rows = EXPERT_ROWS
    wmap = lambda i, be, nu: (be[i], 0, 0)
    grid_spec = pltpu.PrefetchScalarGridSpec(
        num_scalar_prefetch=2,
        grid=(n_blocks,),
        in_specs=[
            pl.BlockSpec(memory_space=pl.ANY),
            pl.BlockSpec((None, d, f), wmap),
            pl.BlockSpec((None, 1, f), wmap),
            pl.BlockSpec((None, d, f), wmap),
            pl.BlockSpec((None, 1, f), wmap),
            pl.BlockSpec((None, f, d), wmap),
            pl.BlockSpec((None, 1, d), wmap),
        ],
        out_specs=pl.BlockSpec(memory_space=pl.ANY),
        scratch_shapes=[
            pltpu.VMEM((2, rows, d), F32),
            pltpu.VMEM((2, rows, d), F32),
            pltpu.SemaphoreType.DMA((2,)),
            pltpu.SemaphoreType.DMA((2,)),
            pltpu.VMEM((d, f), BF16),
            pltpu.VMEM((d, f), BF16),
            pltpu.VMEM((f, d), BF16),
        ],
    )
    return pl.pallas_call(
        functools.partial(_expert_kernel, rows=rows),
        grid_spec=grid_spec,
        out_shape=jax.ShapeDtypeStruct((n_rows, 1, d), F32),
        compiler_params=pltpu.CompilerParams(
            dimension_semantics=("arbitrary",), vmem_limit_bytes=52 * MIB),
        name="moe_experts",
    )(block_expert, n_used, x_sorted,
      w_gate, b_gate.reshape(n_exp, 1, f), w_up, b_up.reshape(n_exp, 1, f),
      w_down, b_down.reshape(n_exp, 1, d))


def _combine_kernel(dest_hbm, y_hbm, x1_ref, gate_ref, lnf_ref, o_ref, idx_smem, ybuf,
                    idx_sem, row_sem, *, tile):
    i = pl.program_id(0)
    n = pl.num_programs(0)
    per_tile = TOP_K * tile

    def slot_base(b):
        return pl.multiple_of((b % 3) * per_tile, per_tile)

    def idx_copy(b):
        src = dest_hbm.at[pl.ds(pl.multiple_of(b * per_tile, per_tile), per_tile)]
        return pltpu.make_async_copy(src, idx_smem.at[pl.ds(slot_base(b), per_tile)],
                                     idx_sem.at[b % 3])

    def gather(b):
        base = slot_base(b)
        buf = ybuf.at[b % 2]
        sem = row_sem.at[b % 2]

        def issue(g, carry):
            for s in range(SUBLANES):
                pltpu.make_async_copy(y_hbm.at[idx_smem[base + g * SUBLANES + s]],
                                      buf.at[g, pl.ds(s, 1), :], sem).start()
            return carry

        lax.fori_loop(0, per_tile // SUBLANES, issue, 0)

    @pl.when(i == 0)
    def _():
        idx_copy(0).start()
        idx_copy(0).wait()
        gather(0)

        @pl.when(n > 1)
        def _():
            idx_copy(1).start()

    @pl.when(i + 1 < n)
    def _():
        idx_copy(i + 1).wait()

        @pl.when(i + 2 < n)
        def _():
            idx_copy(i + 2).start()
        gather(i + 1)

    cur = i % 2
    pltpu.make_async_copy(ybuf.at[1 - cur], ybuf.at[cur], row_sem.at[cur]).wait()
    gates = gate_ref[...]
    x = x1_ref[...]
    groups = tile // SUBLANES
    for k in range(TOP_K):
        yk = ybuf[cur, k * groups:(k + 1) * groups].reshape(tile, x.shape[1])
        x = x + gates[:, k:k + 1] * yk
    o_ref[...] = _rms(x, lnf_ref[...])


def _combine(dest, y_rows, x1, gates, ln_f_g):
    t, d = x1.shape
    tile = min(COMBINE_TILE, t)
    n_tiles = t // tile
    dest_tiles = dest.reshape(n_tiles, tile, TOP_K).transpose(0, 2, 1).reshape(n_tiles * TOP_K * tile)
    return pl.pallas_call(
        functools.partial(_combine_kernel, tile=tile),
        grid=(n_tiles,),
        in_specs=[
            pl.BlockSpec(memory_space=pl.ANY),
            pl.BlockSpec(memory_space=pl.ANY),
            pl.BlockSpec((tile, d), lambda i: (i, 0)),
            pl.BlockSpec((tile, TOP_K), lambda i: (i, 0)),
            pl.BlockSpec((1, d), lambda i: (0, 0)),
        ],
        out_specs=pl.BlockSpec((tile, d), lambda i: (i, 0)),
        out_shape=jax.ShapeDtypeStruct((t, d), F32),
        scratch_shapes=[
            pltpu.SMEM((3 * TOP_K * tile,), I32),
            pltpu.VMEM((2, TOP_K * tile // SUBLANES, SUBLANES, d), F32),
            pltpu.SemaphoreType.DMA((3,)),
            pltpu.SemaphoreType.DMA((2,)),
        ],
        compiler_params=pltpu.CompilerParams(
            dimension_semantics=("arbitrary",), vmem_limit_bytes=40 * MIB),
        name="moe_combine",
    )(dest_tiles, y_rows, x1, gates, ln_f_g.reshape(1, d))


def kernel(x, ln1_g, w_in, lam_re, lam_im, log_dt, ssm_b_re, ssm_b_im, ssm_c_re, ssm_c_im,
           ssm_d, w_glu, g_sb, g_ssm, w_out, ln2_g, w_router, b_router, w_gate, b_gate,
           w_up, b_up, w_down, b_down, ln_f_g):
    bsz, seq, d = x.shape
    assert ln1_g.shape[0] == 1, "depth-1 block only"
    ssm_w = ssm_d.shape[1]
    sb = g_sb.shape[1]
    x2 = x.reshape(bsz * seq, d)
    q, k, v, u = _in_proj(x2, ln1_g[0], w_in[0], sb, ssm_w)
    y_sb = _attention(q, k, v, bsz, seq)
    y_ss = _ssm(u, bsz, seq, lam_re[0], lam_im[0], log_dt[0], ssm_b_re[0], ssm_b_im[0],
                ssm_c_re[0], ssm_c_im[0])
    x1, xn, top_idx, gates, rank, counts = _post(
        x2, y_sb, y_ss, u, ssm_d[0], w_glu[0], g_sb[0], g_ssm[0], w_out[0], ln2_g[0],
        w_router[0], b_router[0])
    dest, block_expert, n_used, fill_start, n_blocks = _plan(top_idx, rank, counts, EXPERT_ROWS)
    x_sorted = _dispatch(xn, dest, fill_start, n_blocks * EXPERT_ROWS)
    y_rows = _experts(x_sorted, block_expert, n_used, n_blocks, w_gate[0], b_gate[0],
                      w_up[0], b_up[0], w_down[0], b_down[0])
    out = _combine(dest, y_rows, x1, gates, ln_f_g)
    return out.reshape(bsz, seq, d)
```

```python
import functools
import math

import jax
import jax.numpy as jnp
from jax import lax
from jax.experimental import pallas as pl
from jax.experimental.pallas import tpu as pltpu

F32 = jnp.float32
BF16 = jnp.bfloat16
I32 = jnp.int32

EPS = 1e-5
SB_HEAD_DIM = 64
SSM_GROUP = 16
SSM_STATE = 64
TOP_K = 4
SWIGLU_LIMIT = 7.0
SWIGLU_ALPHA = 1.702

LANES = 128
SUBLANES = 8
HEADS_PER_BLOCK = LANES // SB_HEAD_DIM
ATTN_BLOCK = 256
ATTN_UNROLL = 4
SSM_CHUNK = 64
TOKEN_TILE = 512
EXPERT_ROWS = 256
COMBINE_TILE = 256
DISPATCH_TILE = 1024
MIB = 1024 * 1024
LOG2E = 1.4426950408889634


def _rms(x, g):
    return x * lax.rsqrt(jnp.mean(x * x, axis=-1, keepdims=True) + EPS) * g


def _in_proj_kernel(x_ref, g_ref, w_ref, q_ref, k_ref, v_ref, u_ref, w_bf, *, sb, scale):
    @pl.when(pl.program_id(0) == 0)
    def _():
        w_bf[...] = w_ref[...].astype(BF16)

    h = _rms(x_ref[...], g_ref[...])
    proj = jnp.dot(h.astype(BF16), w_bf[...], preferred_element_type=F32)
    q_ref[...] = (proj[:, :sb] * scale).astype(BF16)
    k_ref[...] = proj[:, sb:2 * sb].astype(BF16)
    v_ref[...] = proj[:, 2 * sb:3 * sb].astype(BF16)
    u_ref[...] = proj[:, 3 * sb:].astype(BF16)


def _in_proj(x2, ln1_g, w_in, sb, ssm_w):
    t, d = x2.shape
    n_in = w_in.shape[1]
    tm = min(TOKEN_TILE, t)
    scale = LOG2E / math.sqrt(SB_HEAD_DIM)
    return pl.pallas_call(
        functools.partial(_in_proj_kernel, sb=sb, scale=scale),
        grid=(t // tm,),
        in_specs=[
            pl.BlockSpec((tm, d), lambda i: (i, 0)),
            pl.BlockSpec((1, d), lambda i: (0, 0)),
            pl.BlockSpec((d, n_in), lambda i: (0, 0)),
        ],
        out_specs=[
            pl.BlockSpec((tm, sb), lambda i: (i, 0)),
            pl.BlockSpec((tm, sb), lambda i: (i, 0)),
            pl.BlockSpec((tm, sb), lambda i: (i, 0)),
            pl.BlockSpec((tm, ssm_w), lambda i: (i, 0)),
        ],
        out_shape=[
            jax.ShapeDtypeStruct((t, sb), BF16),
            jax.ShapeDtypeStruct((t, sb), BF16),
            jax.ShapeDtypeStruct((t, sb), BF16),
            jax.ShapeDtypeStruct((t, ssm_w), BF16),
        ],
        scratch_shapes=[pltpu.VMEM((d, n_in), BF16)],
        compiler_params=pltpu.CompilerParams(
            dimension_semantics=("arbitrary",), vmem_limit_bytes=48 * MIB),
        name="in_proj",
    )(x2, ln1_g.reshape(1, d), w_in)


def _attn_kernel(ti_ref, tj_ref, q_ref, k_ref, v_ref, tri_ref, mask_ref, o_ref,
                 wbuf, dbuf, spbuf, lbuf, rsbuf, acc_ref, r_ref, *, blk, n_items):
    lane = lax.broadcasted_iota(I32, (1, LANES), 1)
    head_mask = [lane < SB_HEAD_DIM, lane >= SB_HEAD_DIM]
    zero = jnp.zeros((), BF16)
    sign_bit = jnp.uint32(0x80000000)
    last = n_items - 1
    ALL, CAUSAL, NONE = 0, 1, 2

    for ref in (wbuf, dbuf, spbuf, lbuf, rsbuf, acc_ref, r_ref):
        ref[...] = jnp.zeros_like(ref)

    def both_heads(x):
        return jnp.concatenate([jnp.where(m, x, zero) for m in head_mask], axis=0)

    def item(idx):
        c = jnp.clip(idx, 0, last)
        return ti_ref[c], tj_ref[c]

    def step(p, slot):
        other = 1 - slot
        i1, j1 = item(p)
        q2 = q_ref[pl.ds(pl.multiple_of(i1 * blk, blk), blk), :]
        ks = k_ref[pl.ds(pl.multiple_of(j1 * blk, blk), blk), :]
        wbuf[slot] = lax.dot_general(both_heads(q2), ks, (((1,), (1,)), ((), ())),
                                     preferred_element_type=F32)
        spb = spbuf[other]
        later = jnp.dot(spb, tri_ref[...], preferred_element_type=F32)
        lbuf[slot] = later
        rsbuf[slot] = later[:, 0:1] + spb[:, 0:1].astype(F32)
        i4, j4 = item(p - 3)
        valid = p - 3 <= last
        first = jnp.logical_and(i4 == j4, valid)
        mask4 = mask_ref[jnp.where(valid, (i4 == j4).astype(I32), NONE)]
        vs = v_ref[pl.ds(pl.multiple_of(j4 * blk, blk), blk), :]
        r_prev = jnp.where(first, 0.0, r_ref[...])
        a = jnp.exp2(dbuf[other] - (lbuf[other] + r_prev))
        ab = a.astype(BF16) * jnp.concatenate([mask4, mask4], axis=0)
        r_ref[...] = r_prev + rsbuf[other]
        a_cat = jnp.concatenate([ab[:blk], ab[blk:]], axis=1)
        acc = jnp.where(first, 0.0, acc_ref[...]) + jnp.dot(
            a_cat, both_heads(vs), preferred_element_type=F32)
        acc_ref[...] = acc
        o_ref[pl.ds(pl.multiple_of(i4 * blk, blk), blk), :] = acc.astype(o_ref.dtype)
        i2, j2 = item(p - 1)
        mask2 = mask_ref[(i2 == j2).astype(I32)]
        w = wbuf[other]
        neg_abs = lax.bitcast_convert_type(
            lax.bitcast_convert_type(w, jnp.uint32) | sign_bit, F32)
        sp2 = jnp.maximum(w, 0.0) + jnp.log(1.0 + jnp.exp2(neg_abs)) * LOG2E
        dbuf[other] = w - sp2
        spbuf[slot] = sp2.astype(BF16) * jnp.concatenate([mask2, mask2], axis=0)

    def unrolled(it, carry):
        for s in range(ATTN_UNROLL):
            step(ATTN_UNROLL * it + s, s % 2)
        return carry

    lax.fori_loop(0, pl.cdiv(n_items + 3, ATTN_UNROLL), unrolled, 0)


def _attention(q, k, v, bsz, seq):
    sb = q.shape[-1]
    blk = min(ATTN_BLOCK, seq)
    n_blk = seq // blk
    n_pairs = sb // LANES
    q3, k3, v3 = (a.reshape(bsz, seq, sb) for a in (q, k, v))
    jj = lax.broadcasted_iota(I32, (blk, blk), 0)
    ss = lax.broadcasted_iota(I32, (blk, blk), 1)
    tri = (jj > ss).astype(BF16)
    masks = jnp.stack([jnp.ones((blk, blk), BF16), (ss < jj).astype(BF16),
                       jnp.zeros((blk, blk), BF16)])
    items = [(i, j) for i in range(n_blk) for j in range(i, -1, -1)]
    item_i = jnp.asarray([i for i, _ in items], I32)
    item_j = jnp.asarray([j for _, j in items], I32)
    whole = lambda b, p, ti, tj: (b, 0, p)
    grid_spec = pltpu.PrefetchScalarGridSpec(
        num_scalar_prefetch=2,
        grid=(bsz, n_pairs),
        in_specs=[
            pl.BlockSpec((None, seq, LANES), whole),
            pl.BlockSpec((None, seq, LANES), whole),
            pl.BlockSpec((None, seq, LANES), whole),
            pl.BlockSpec((blk, blk), lambda b, p, ti, tj: (0, 0)),
            pl.BlockSpec((3, blk, blk), lambda b, p, ti, tj: (0, 0, 0)),
        ],
        out_specs=pl.BlockSpec((None, seq, LANES), whole),
        scratch_shapes=[
            pltpu.VMEM((2, HEADS_PER_BLOCK * blk, blk), F32),
            pltpu.VMEM((2, HEADS_PER_BLOCK * blk, blk), F32),
            pltpu.VMEM((2, HEADS_PER_BLOCK * blk, blk), BF16),
            pltpu.VMEM((2, HEADS_PER_BLOCK * blk, blk), F32),
            pltpu.VMEM((2, HEADS_PER_BLOCK * blk, 1), F32),
            pltpu.VMEM((blk, LANES), F32),
            pltpu.VMEM((HEADS_PER_BLOCK * blk, 1), F32),
        ],
    )
    out = pl.pallas_call(
        functools.partial(_attn_kernel, blk=blk, n_items=len(items)),
        grid_spec=grid_spec,
        out_shape=jax.ShapeDtypeStruct((bsz, seq, sb), BF16),
        compiler_params=pltpu.CompilerParams(
            dimension_semantics=("arbitrary", "arbitrary"), vmem_limit_bytes=40 * MIB),
        name="sb_attention",
    )(item_i, item_j, q3, k3, v3, tri, masks)
    return out.reshape(bsz * seq, sb)


def _ssm_params(lam_re, lam_im, log_dt, b_re, b_im, c_re, c_im, chunk, n_chunks):
    lam = lax.complex(lam_re.astype(F32), lam_im.astype(F32))
    dt = jnp.exp(log_dt.astype(F32))[:, None]
    lam_dt = lam * dt
    lam_bar = jnp.exp(lam_dt)
    b_bar = ((lam_bar - 1.0) / lam)[:, :, None] * lax.complex(b_re.astype(F32), b_im.astype(F32))
    b_t = jnp.swapaxes(b_bar, 1, 2)
    c_mat = lax.complex(c_re.astype(F32), c_im.astype(F32))
    steps = jnp.arange(chunk + 1, dtype=F32)
    pw = jnp.exp(lam_dt[:, None, :] * steps[None, :, None])
    npw = jnp.exp(-lam_dt[:, None, :] * steps[None, :chunk, None])

    def halves(lo, hi):
        return jnp.concatenate([lo, hi], axis=-1)

    def b_side(z):
        return [halves(jnp.real(z), jnp.real(z)), halves(-jnp.imag(z), jnp.imag(z))]

    def c_side(z):
        return [halves(jnp.real(z), jnp.imag(z)), halves(jnp.imag(z), jnp.real(z))]

    time_terms = jnp.stack(
        b_side(npw)
        + b_side(pw[:, chunk - 1::-1])
        + c_side(pw[:, :chunk])
        + c_side(pw[:, 1:]), axis=1)
    chan_terms = jnp.stack(
        [halves(jnp.real(b_t), jnp.imag(b_t)), halves(jnp.imag(b_t), jnp.real(b_t)),
         halves(jnp.real(c_mat), -jnp.real(c_mat)), halves(-jnp.imag(c_mat), -jnp.imag(c_mat))],
        axis=1)
    n_steps = max(1, (n_chunks - 1).bit_length())
    powers = []
    cur = pw[:, chunk, :]
    for _ in range(n_steps):
        powers.append(cur)
        cur = cur * cur
    lam_pow = jnp.stack(powers, axis=1)
    a1 = halves(jnp.real(lam_pow), jnp.real(lam_pow))
    a2 = halves(-jnp.imag(lam_pow), jnp.imag(lam_pow))
    return time_terms, chan_terms, a1, a2


def _ssm_kernel(u_ref, time_ref, chan_ref, a1_ref, a2_ref, y_ref, toep_ref,
                *, n_chunks, n_steps, group):
    lc = toep_ref.shape[0]
    p2 = a1_ref.shape[1]
    group_shift = group.bit_length() - 1
    nt = (((1,), (1,)), ((), ()))

    def table(t1, t2, v1, v2):
        full = (time_ref[t1][:, None, :] * chan_ref[v1][None, :, :]
                + time_ref[t2][:, None, :] * chan_ref[v2][None, :, :])
        return full.reshape(lc, p2).astype(BF16)

    src = table(0, 1, 0, 1)
    to_state = table(2, 3, 0, 1)
    dst = table(4, 5, 2, 3)
    from_state = table(6, 7, 2, 3)
    cb = min(256, lc)
    for j in range(lc // cb):
        blk = lax.dot_general(src, dst[j * cb:(j + 1) * cb], nt, preferred_element_type=F32)
        s_idx = lax.broadcasted_iota(I32, (lc, cb), 0) >> group_shift
        t_idx = (lax.broadcasted_iota(I32, (lc, cb), 1) + j * cb) >> group_shift
        toep_ref[:, j * cb:(j + 1) * cb] = jnp.where(s_idx <= t_idx, blk, 0.0).astype(BF16)
    u = u_ref[...]
    y = jnp.dot(u, toep_ref[...], preferred_element_type=F32)
    z = jnp.dot(u, to_state, preferred_element_type=F32)
    n = lax.broadcasted_iota(I32, z.shape, 0) & (n_chunks - 1)
    x = jnp.where(n >= 1, pltpu.roll(z, 1, 0), 0.0)
    for k in range(n_steps):
        sh = 1 << k
        xs = jnp.where(n >= sh, pltpu.roll(x, sh, 0), 0.0)
        x = x + a1_ref[k:k + 1, :] * xs + a2_ref[k:k + 1, :] * pltpu.roll(xs, p2 // 2, 1)
    y = y + lax.dot_general(x.astype(BF16), from_state, nt, preferred_element_type=F32)
    y_ref[...] = y.astype(y_ref.dtype)


def _ssm(u, bsz, seq, lam_re, lam_im, log_dt, b_re, b_im, c_re, c_im):
    g, p = lam_re.shape
    c = b_re.shape[-1]
    chunk = min(SSM_CHUNK, seq)
    n_chunks = seq // chunk
    assert n_chunks & (n_chunks - 1) == 0 and c & (c - 1) == 0, (n_chunks, c)
    rows = bsz * n_chunks
    lc = chunk * c
    time_terms, chan_terms, a1, a2 = _ssm_params(
        lam_re, lam_im, log_dt, b_re, b_im, c_re, c_im, chunk, n_chunks)
    n_steps = a1.shape[1]
    u_g = u.reshape(bsz, n_chunks, chunk, g, c).transpose(3, 0, 1, 2, 4).reshape(g, rows, lc)
    per_group = lambda i: (i, 0, 0)
    y_g = pl.pallas_call(
        functools.partial(_ssm_kernel, n_chunks=n_chunks, n_steps=n_steps, group=c),
        grid=(g,),
        in_specs=[
            pl.BlockSpec((None, rows, lc), per_group),
            pl.BlockSpec((None,) + time_terms.shape[1:], lambda i: (i, 0, 0, 0)),
            pl.BlockSpec((None,) + chan_terms.shape[1:], lambda i: (i, 0, 0, 0)),
            pl.BlockSpec((None, n_steps, 2 * p), per_group),
            pl.BlockSpec((None, n_steps, 2 * p), per_group),
        ],
        out_specs=pl.BlockSpec((None, rows, lc), per_group),
        out_shape=jax.ShapeDtypeStruct((g, rows, lc), BF16),
        scratch_shapes=[pltpu.VMEM((lc, lc), BF16)],
        compiler_params=pltpu.CompilerParams(
            dimension_semantics=("arbitrary",), vmem_limit_bytes=40 * MIB),
        name="s5_chunked_scan",
    )(u_g, time_terms, chan_terms, a1, a2)
    return y_g.reshape(g, bsz, n_chunks, chunk, c).transpose(1, 2, 3, 0, 4).reshape(bsz * seq, g * c)


def _post_kernel(x_ref, ysb_ref, yss_ref, u_ref, d_ref, wglu_ref, gsb_ref, gssm_ref, wout_ref,
                 ln2_ref, wr_ref, br_ref, tri_ref, x1_ref, xn_ref, idx_ref, gate_ref, rank_ref,
                 count_ref, running_ref, wglu_bf, wout_bf, *, sb, n_exp):
    @pl.when(pl.program_id(0) == 0)
    def _():
        running_ref[...] = jnp.zeros_like(running_ref)
        wglu_bf[...] = wglu_ref[...].astype(BF16)
        wout_bf[...] = wout_ref[...].astype(BF16)

    u = u_ref[...].astype(F32)
    y = yss_ref[...].astype(F32) + d_ref[...] * u
    y = y * (0.5 * (1.0 + jnp.tanh(math.sqrt(2.0 / math.pi) * (y + 0.044715 * (y * y * y)))))
    ab = jnp.dot(y.astype(BF16), wglu_bf[...], preferred_element_type=F32)
    w = ab.shape[1] // 2
    y_ssm = ab[:, :w] * (1.0 / (1.0 + jnp.exp(-ab[:, w:])))
    m_sb = _rms(ysb_ref[...].astype(F32), gsb_ref[...])
    m_ssm = _rms(y_ssm, gssm_ref[...])
    x1 = (x_ref[...]
          + jnp.dot(m_sb.astype(BF16), wout_bf[:sb, :], preferred_element_type=F32)
          + jnp.dot(m_ssm.astype(BF16), wout_bf[sb:, :], preferred_element_type=F32))
    x1_ref[...] = x1
    xn = _rms(x1, ln2_ref[...])
    xn_ref[...] = xn
    logits = jnp.dot(xn, wr_ref[...], preferred_element_type=F32,
                     precision=lax.Precision.HIGHEST) + br_ref[...]
    lane = lax.broadcasted_iota(I32, logits.shape, 1).astype(F32)
    out_lane = lax.broadcasted_iota(I32, (logits.shape[0], LANES), 1)
    idx_out = jnp.zeros((logits.shape[0], LANES), F32)
    val_out = jnp.zeros((logits.shape[0], LANES), F32)
    top = None
    denom = None
    work = logits
    chosen = []
    for k in range(TOP_K):
        m = jnp.max(work, axis=-1, keepdims=True)
        sel = jnp.min(jnp.where(work == m, lane, float(n_exp)), axis=-1, keepdims=True)
        hit = lane == sel
        chosen.append(hit)
        work = jnp.where(hit, -jnp.inf, work)
        if k == 0:
            top = m
        e = jnp.exp(m - top)
        denom = e if denom is None else denom + e
        idx_out = jnp.where(out_lane == k, sel, idx_out)
        val_out = jnp.where(out_lane == k, e, val_out)
    idx_ref[...] = idx_out[:, :TOP_K].astype(I32)
    gate_ref[...] = (val_out / denom)[:, :TOP_K]
    member = functools.reduce(jnp.logical_or, chosen).astype(BF16)
    before = running_ref[...] + jnp.dot(tri_ref[...], member, preferred_element_type=F32)
    rank_out = jnp.zeros((logits.shape[0], LANES), F32)
    for k in range(TOP_K):
        rk = jnp.sum(jnp.where(chosen[k], before, 0.0), axis=-1, keepdims=True)
        rank_out = jnp.where(out_lane == k, rk, rank_out)
    rank_ref[...] = rank_out[:, :TOP_K].astype(I32)
    running = running_ref[...] + jnp.sum(member.astype(F32), axis=0, keepdims=True)
    running_ref[...] = running
    count_ref[...] = running


def _post(x2, y_sb, y_ss, u, ssm_d, w_glu, g_sb, g_ssm, w_out, ln2_g, w_router, b_router):
    t, d = x2.shape
    sb = y_sb.shape[1]
    w = y_ss.shape[1]
    n_exp = w_router.shape[1]
    tm = min(TOKEN_TILE, t)
    row = lambda i: (i, 0)
    fixed = lambda i: (0, 0)
    earlier = (lax.broadcasted_iota(I32, (tm, tm), 1)
               < lax.broadcasted_iota(I32, (tm, tm), 0)).astype(BF16)
    return pl.pallas_call(
        functools.partial(_post_kernel, sb=sb, n_exp=n_exp),
        grid=(t // tm,),
        in_specs=[
            pl.BlockSpec((tm, d), row),
            pl.BlockSpec((tm, sb), row),
            pl.BlockSpec((tm, w), row),
            pl.BlockSpec((tm, w), row),
            pl.BlockSpec((1, w), fixed),
            pl.BlockSpec((w, 2 * w), fixed),
            pl.BlockSpec((1, sb), fixed),
            pl.BlockSpec((1, w), fixed),
            pl.BlockSpec((sb + w, d), fixed),
            pl.BlockSpec((1, d), fixed),
            pl.BlockSpec((d, n_exp), fixed),
            pl.BlockSpec((1, n_exp), fixed),
            pl.BlockSpec((tm, tm), fixed),
        ],
        out_specs=[
            pl.BlockSpec((tm, d), row),
            pl.BlockSpec((tm, d), row),
            pl.BlockSpec((tm, TOP_K), row),
            pl.BlockSpec((tm, TOP_K), row),
            pl.BlockSpec((tm, TOP_K), row),
            pl.BlockSpec((1, n_exp), fixed),
        ],
        out_shape=[
            jax.ShapeDtypeStruct((t, d), F32),
            jax.ShapeDtypeStruct((t, d), F32),
            jax.ShapeDtypeStruct((t, TOP_K), I32),
            jax.ShapeDtypeStruct((t, TOP_K), F32),
            jax.ShapeDtypeStruct((t, TOP_K), I32),
            jax.ShapeDtypeStruct((1, n_exp), F32),
        ],
        scratch_shapes=[pltpu.VMEM((1, n_exp), F32), pltpu.VMEM((w, 2 * w), BF16),
                        pltpu.VMEM((sb + w, d), BF16)],
        compiler_params=pltpu.CompilerParams(
            dimension_semantics=("arbitrary",), vmem_limit_bytes=48 * MIB),
        name="post_mixer_router",
    )(x2, y_sb, y_ss, u, ssm_d.reshape(1, w), w_glu, g_sb.reshape(1, sb),
      g_ssm.reshape(1, w), w_out, ln2_g.reshape(1, d), w_router,
      b_router.reshape(1, n_exp), earlier)


def _plan(top_idx, rank, counts, rows_per_block):
    t, k = top_idx.shape
    n_exp = counts.shape[-1]
    counts = counts.reshape(n_exp).astype(I32)
    padded = ((counts + rows_per_block - 1) // rows_per_block) * rows_per_block
    pad_ends = jnp.cumsum(padded)
    pad_starts = pad_ends - padded
    dest = (pad_starts[top_idx] + rank).astype(I32)
    n_rows = t * k + n_exp * rows_per_block
    n_blocks = n_rows // rows_per_block
    block_start = jnp.arange(n_blocks, dtype=I32) * rows_per_block
    block_expert = jnp.minimum(
        jnp.sum((pad_ends[None, :] <= block_start[:, None]).astype(I32), axis=1), n_exp - 1)
    n_used = (pad_ends[-1] // rows_per_block).astype(I32).reshape(1)
    fill_start = jnp.concatenate([jnp.maximum(pad_ends - rows_per_block, 0).astype(I32), n_used])
    return dest, block_expert, n_used, fill_start, n_blocks


def _dispatch_kernel(fill_ref, dest_hbm, xn_ref, xs_hbm, idx_smem, zeros_ref, idx_sem, row_sem,
                     fill_sem, *, tile, n_exp, fill_rows):
    i = pl.program_id(0)
    n = pl.num_programs(0)
    per_tile = TOP_K * tile

    def idx_copy(b):
        src = dest_hbm.at[pl.ds(pl.multiple_of(b * per_tile, per_tile), per_tile)]
        dst = idx_smem.at[pl.ds(pl.multiple_of((b % 2) * per_tile, per_tile), per_tile)]
        return pltpu.make_async_copy(src, dst, idx_sem.at[b % 2])

    @pl.when(i == 0)
    def _():
        idx_copy(0).start()
        zeros_ref[...] = jnp.zeros_like(zeros_ref)
        for e in range(n_exp):
            start = pl.multiple_of(fill_ref[e], 8)
            pltpu.make_async_copy(zeros_ref, xs_hbm.at[pl.ds(start, fill_rows), 0, :],
                                  fill_sem).start()
        for e in range(n_exp):
            pltpu.make_async_copy(zeros_ref, xs_hbm.at[pl.ds(0, fill_rows), 0, :],
                                  fill_sem).wait()
        n_blocks = xs_hbm.shape[0] // fill_rows

        def fill_unused(b, carry):
            cp = pltpu.make_async_copy(
                zeros_ref, xs_hbm.at[pl.ds(pl.multiple_of(b * fill_rows, fill_rows), fill_rows), 0, :],
                fill_sem)
            cp.start()
            cp.wait()
            return carry

        lax.fori_loop(fill_ref[n_exp], n_blocks, fill_unused, 0)

    idx_copy(i).wait()

    @pl.when(i + 1 < n)
    def _():
        idx_copy(i + 1).start()

    base = pl.multiple_of((i % 2) * per_tile, per_tile)
    quarter = tile // TOP_K
    for part in range(TOP_K):
        def issue(g, carry, part=part):
            group = part * (quarter // SUBLANES) + g
            for s in range(SUBLANES):
                for k in range(TOP_K):
                    dst_row = idx_smem[base + (group * SUBLANES + s) * TOP_K + k]
                    pltpu.make_async_copy(xn_ref.at[group, pl.ds(s, 1), :], xs_hbm.at[dst_row],
                                          row_sem.at[part]).start(priority=k % 2)
            return carry

        lax.fori_loop(0, quarter // SUBLANES, issue, 0)
    for part in range(TOP_K):
        pltpu.make_async_copy(xn_ref, xn_ref, row_sem.at[part]).wait()


def _dispatch(xn, dest, fill_start, n_rows):
    t, d = xn.shape
    tile = min(DISPATCH_TILE, t)
    n_exp = fill_start.shape[0] - 1
    grid_spec = pltpu.PrefetchScalarGridSpec(
        num_scalar_prefetch=1,
        grid=(t // tile,),
        in_specs=[
            pl.BlockSpec(memory_space=pl.ANY),
            pl.BlockSpec((tile // SUBLANES, SUBLANES, d), lambda i, fs: (i, 0, 0)),
        ],
        out_specs=pl.BlockSpec(memory_space=pl.ANY),
        scratch_shapes=[
            pltpu.SMEM((2 * TOP_K * tile,), I32),
            pltpu.VMEM((EXPERT_ROWS, d), F32),
            pltpu.SemaphoreType.DMA((2,)),
            pltpu.SemaphoreType.DMA((TOP_K,)),
            pltpu.SemaphoreType.DMA,
        ],
    )
    return pl.pallas_call(
        functools.partial(_dispatch_kernel, tile=tile, n_exp=n_exp, fill_rows=EXPERT_ROWS),
        grid_spec=grid_spec,
        out_shape=jax.ShapeDtypeStruct((n_rows, 1, d), F32),
        compiler_params=pltpu.CompilerParams(
            dimension_semantics=("arbitrary",), vmem_limit_bytes=40 * MIB),
        name="moe_dispatch",
    )(fill_start, dest.reshape(t * TOP_K), xn.reshape(t // SUBLANES, SUBLANES, d))


def _expert_kernel(be_ref, nused_ref, xs_hbm, wg_ref, bg_ref, wu_ref, bu_ref, wd_ref, bd_ref,
                   y_hbm, xbuf, ybuf, x_sem, y_sem, wg_bf, wu_bf, wd_bf, *, rows):
    i = pl.program_id(0)
    n = nused_ref[0]

    def x_copy(b):
        return pltpu.make_async_copy(
            xs_hbm.at[pl.ds(pl.multiple_of(b * rows, rows), rows), 0, :], xbuf.at[b % 2],
            x_sem.at[b % 2])

    def y_copy(b):
        return pltpu.make_async_copy(
            ybuf.at[b % 2], y_hbm.at[pl.ds(pl.multiple_of(b * rows, rows), rows), 0, :],
            y_sem.at[b % 2])

    @pl.when(i < n)
    def _():
        @pl.when(i == 0)
        def _():
            x_copy(0).start()

        @pl.when(i + 1 < n)
        def _():
            x_copy(i + 1).start()

        changed = jnp.logical_or(i == 0, be_ref[i] != be_ref[jnp.maximum(i - 1, 0)])

        @pl.when(changed)
        def _():
            wg_bf[...] = wg_ref[...].astype(BF16)
            wu_bf[...] = wu_ref[...].astype(BF16)
            wd_bf[...] = wd_ref[...].astype(BF16)

        x_copy(i).wait()
        x = xbuf[i % 2].astype(BF16)
        gate = jnp.dot(x, wg_bf[...], preferred_element_type=F32) + bg_ref[...]
        up = jnp.dot(x, wu_bf[...], preferred_element_type=F32) + bu_ref[...]
        gate = jnp.minimum(gate, SWIGLU_LIMIT)
        up = jnp.clip(up, -SWIGLU_LIMIT, SWIGLU_LIMIT)
        glu = gate * (1.0 / (1.0 + jnp.exp(-SWIGLU_ALPHA * gate)))
        hidden = ((up + 1.0) * glu).astype(BF16)
        y = jnp.dot(hidden, wd_bf[...], preferred_element_type=F32) + bd_ref[...]

        @pl.when(i >= 2)
        def _():
            y_copy(i - 2).wait()
        ybuf[i % 2] = y
        y_copy(i).start()

        @pl.when(i == n - 1)
        def _():
            y_copy(i).wait()

            @pl.when(i >= 1)
            def _():
                y_copy(i - 1).wait()

    @pl.when(i >= n)
    def _():
        ybuf[i % 2] = jnp.zeros((rows, ybuf.shape[2]), F32)
        y_copy(i).start()
        y_copy(i).wait()


def _experts(x_sorted, block_expert, n_used, n_blocks, w_gate, b_gate, w_up, b_up, w_down, b_down):
    n_rows, _, d = x_sorted.shape
    n_exp, _, f = w_gate.shape
    rows = EXPERT_ROWS
    wmap = lambda i, be, nu: (be[i], 0, 0)
    grid_spec = pltpu.PrefetchScalarGridSpec(
        num_scalar_prefetch=2,
        grid=(n_blocks,),
        in_specs=[
            pl.BlockSpec(memory_space=pl.ANY),
            pl.BlockSpec((None, d, f), wmap),
            pl.BlockSpec((None, 1, f), wmap),
            pl.BlockSpec((None, d, f), wmap),
            pl.BlockSpec((None, 1, f), wmap),
            pl.BlockSpec((None, f, d), wmap),
            pl.BlockSpec((None, 1, d), wmap),
        ],
        out_specs=pl.BlockSpec(memory_space=pl.ANY),
        scratch_shapes=[
            pltpu.VMEM((2, rows, d), F32),
            pltpu.VMEM((2, rows, d), F32),
            pltpu.SemaphoreType.DMA((2,)),
            pltpu.SemaphoreType.DMA((2,)),
            pltpu.VMEM((d, f), BF16),
            pltpu.VMEM((d, f), BF16),
            pltpu.VMEM((f, d), BF16),
        ],
    )
    return pl.pallas_call(
        functools.partial(_expert_kernel, rows=rows),
        grid_spec=grid_spec,
        out_shape=jax.ShapeDtypeStruct((n_rows, 1, d), F32),
        compiler_params=pltpu.CompilerParams(
            dimension_semantics=("arbitrary",), vmem_limit_bytes=52 * MIB),
        name="moe_experts",
    )(block_expert, n_used, x_sorted,
      w_gate, b_gate.reshape(n_exp, 1, f), w_up, b_up.reshape(n_exp, 1, f),
      w_down, b_down.reshape(n_exp, 1, d))


def _combine_kernel(dest_hbm, y_hbm, x1_ref, gate_ref, lnf_ref, o_ref, idx_smem, ybuf,
                    idx_sem, row_sem, *, tile):
    i = pl.program_id(0)
    n = pl.num_programs(0)
    per_tile = TOP_K * tile

    def slot_base(b):
        return pl.multiple_of((b % 3) * per_tile, per_tile)

    def idx_copy(b):
        src = dest_hbm.at[pl.ds(pl.multiple_of(b * per_tile, per_tile), per_tile)]
        return pltpu.make_async_copy(src, idx_smem.at[pl.ds(slot_base(b), per_tile)],
                                     idx_sem.at[b % 3])

    def gather(b):
        base = slot_base(b)
        buf = ybuf.at[b % 2]
        sem = row_sem.at[b % 2]

        def issue(g, carry):
            for s in range(SUBLANES):
                pltpu.make_async_copy(y_hbm.at[idx_smem[base + g * SUBLANES + s]],
                                      buf.at[g, pl.ds(s, 1), :], sem).start(priority=s % 2)
            return carry

        lax.fori_loop(0, per_tile // SUBLANES, issue, 0)

    @pl.when(i == 0)
    def _():
        idx_copy(0).start()
        idx_copy(0).wait()
        gather(0)

        @pl.when(n > 1)
        def _():
            idx_copy(1).start()

    @pl.when(i + 1 < n)
    def _():
        idx_copy(i + 1).wait()

        @pl.when(i + 2 < n)
        def _():
            idx_copy(i + 2).start()
        gather(i + 1)

    cur = i % 2
    pltpu.make_async_copy(ybuf.at[1 - cur], ybuf.at[cur], row_sem.at[cur]).wait()
    gates = gate_ref[...]
    x = x1_ref[...]
    groups = tile // SUBLANES
    for k in range(TOP_K):
        yk = ybuf[cur, k * groups:(k + 1) * groups].reshape(tile, x.shape[1])
        x = x + gates[:, k:k + 1] * yk
    o_ref[...] = _rms(x, lnf_ref[...])


def _combine(dest, y_rows, x1, gates, ln_f_g):
    t, d = x1.shape
    tile = min(COMBINE_TILE, t)
    n_tiles = t // tile
    dest_tiles = dest.reshape(n_tiles, tile, TOP_K).transpose(0, 2, 1).reshape(n_tiles * TOP_K * tile)
    return pl.pallas_call(
        functools.partial(_combine_kernel, tile=tile),
        grid=(n_tiles,),
        in_specs=[
            pl.BlockSpec(memory_space=pl.ANY),
            pl.BlockSpec(memory_space=pl.ANY),
            pl.BlockSpec((tile, d), lambda i: (i, 0)),
            pl.BlockSpec((tile, TOP_K), lambda i: (i, 0)),
            pl.BlockSpec((1, d), lambda i: (0, 0)),
        ],
        out_specs=pl.BlockSpec((tile, d), lambda i: (i, 0)),
        out_shape=jax.ShapeDtypeStruct((t, d), F32),
        scratch_shapes=[
            pltpu.SMEM((3 * TOP_K * tile,), I32),
            pltpu.VMEM((2, TOP_K * tile // SUBLANES, SUBLANES, d), F32),
            pltpu.SemaphoreType.DMA((3,)),
            pltpu.SemaphoreType.DMA((2,)),
        ],
        compiler_params=pltpu.CompilerParams(
            dimension_semantics=("arbitrary",), vmem_limit_bytes=40 * MIB),
        name="moe_combine",
    )(dest_tiles, y_rows, x1, gates, ln_f_g.reshape(1, d))


def kernel(x, ln1_g, w_in, lam_re, lam_im, log_dt, ssm_b_re, ssm_b_im, ssm_c_re, ssm_c_im,
           ssm_d, w_glu, g_sb, g_ssm, w_out, ln2_g, w_router, b_router, w_gate, b_gate,
           w_up, b_up, w_down, b_down, ln_f_g):
    bsz, seq, d = x.shape
    assert ln1_g.shape[0] == 1, "depth-1 block only"
    ssm_w = ssm_d.shape[1]
    sb = g_sb.shape[1]
    x2 = x.reshape(bsz * seq, d)
    q, k, v, u = _in_proj(x2, ln1_g[0], w_in[0], sb, ssm_w)
    y_sb = _attention(q, k, v, bsz, seq)
    y_ss = _ssm(u, bsz, seq, lam_re[0], lam_im[0], log_dt[0], ssm_b_re[0], ssm_b_im[0],
                ssm_c_re[0], ssm_c_im[0])
    x1, xn, top_idx, gates, rank, counts = _post(
        x2, y_sb, y_ss, u, ssm_d[0], w_glu[0], g_sb[0], g_ssm[0], w_out[0], ln2_g[0],
        w_router[0], b_router[0])
    dest, block_expert, n_used, fill_start, n_blocks = _plan(top_idx, rank, counts, EXPERT_ROWS)
    x_sorted = _dispatch(xn, dest, fill_start, n_blocks * EXPERT_ROWS)
    y_rows = _experts(x_sorted, block_expert, n_used, n_blocks, w_gate[0], b_gate[0],
                      w_up[0], b_up[0], w_down[0], b_down[0])
    out = _combine(dest, y_rows, x1, gates, ln_f_g)
    return out.reshape(bsz, seq, d)
```

```python
import functools
import math

import jax
import jax.numpy as jnp
from jax import lax
from jax.experimental import pallas as pl
from jax.experimental.pallas import tpu as pltpu

F32 = jnp.float32
BF16 = jnp.bfloat16
I32 = jnp.int32

EPS = 1e-5
SB_HEAD_DIM = 64
SSM_GROUP = 16
SSM_STATE = 64
TOP_K = 4
SWIGLU_LIMIT = 7.0
SWIGLU_ALPHA = 1.702

LANES = 128
SUBLANES = 8
HEADS_PER_BLOCK = LANES // SB_HEAD_DIM
ATTN_BLOCK = 256
ATTN_UNROLL = 4
SSM_CHUNK = 64
TOKEN_TILE = 512
EXPERT_ROWS = 256
COMBINE_TILE = 256
DISPATCH_TILE = 1024
MIB = 1024 * 1024
LOG2E = 1.4426950408889634


def _rms(x, g):
    return x * lax.rsqrt(jnp.mean(x * x, axis=-1, keepdims=True) + EPS) * g


def _in_proj_kernel(x_ref, g_ref, w_ref, q_ref, k_ref, v_ref, u_ref, w_bf, *, sb, scale):
    @pl.when(pl.program_id(0) == 0)
    def _():
        w_bf[...] = w_ref[...].astype(BF16)

    h = _rms(x_ref[...], g_ref[...])
    proj = jnp.dot(h.astype(BF16), w_bf[...], preferred_element_type=F32)
    k_ref[...] = proj[:, sb:2 * sb].astype(BF16)
    u_ref[...] = proj[:, 3 * sb:].astype(BF16)
    lane = lax.broadcasted_iota(I32, (1, LANES), 1)
    q = (proj[:, :sb] * scale).astype(BF16)
    v = proj[:, 2 * sb:3 * sb].astype(BF16)
    zero = jnp.zeros((), BF16)
    for pair in range(sb // LANES):
        cols = slice(pair * LANES, (pair + 1) * LANES)
        for head in range(HEADS_PER_BLOCK):
            own = (lane // SB_HEAD_DIM) == head
            out = slice((pair * HEADS_PER_BLOCK + head) * LANES,
                        (pair * HEADS_PER_BLOCK + head + 1) * LANES)
            q_ref[:, out] = jnp.where(own, q[:, cols], zero)
            v_ref[:, out] = jnp.where(own, v[:, cols], zero)


def _in_proj(x2, ln1_g, w_in, sb, ssm_w):
    t, d = x2.shape
    n_in = w_in.shape[1]
    tm = min(TOKEN_TILE, t)
    scale = LOG2E / math.sqrt(SB_HEAD_DIM)
    return pl.pallas_call(
        functools.partial(_in_proj_kernel, sb=sb, scale=scale),
        grid=(t // tm,),
        in_specs=[
            pl.BlockSpec((tm, d), lambda i: (i, 0)),
            pl.BlockSpec((1, d), lambda i: (0, 0)),
            pl.BlockSpec((d, n_in), lambda i: (0, 0)),
        ],
        out_specs=[
            pl.BlockSpec((tm, HEADS_PER_BLOCK * sb), lambda i: (i, 0)),
            pl.BlockSpec((tm, sb), lambda i: (i, 0)),
            pl.BlockSpec((tm, HEADS_PER_BLOCK * sb), lambda i: (i, 0)),
            pl.BlockSpec((tm, ssm_w), lambda i: (i, 0)),
        ],
        out_shape=[
            jax.ShapeDtypeStruct((t, HEADS_PER_BLOCK * sb), BF16),
            jax.ShapeDtypeStruct((t, sb), BF16),
            jax.ShapeDtypeStruct((t, HEADS_PER_BLOCK * sb), BF16),
            jax.ShapeDtypeStruct((t, ssm_w), BF16),
        ],
        scratch_shapes=[pltpu.VMEM((d, n_in), BF16)],
        compiler_params=pltpu.CompilerParams(
            dimension_semantics=("arbitrary",), vmem_limit_bytes=48 * MIB),
        name="in_proj",
    )(x2, ln1_g.reshape(1, d), w_in)


def _attn_kernel(ti_ref, tj_ref, q_ref, k_ref, v_ref, tri_ref, mask_ref, o_ref,
                 dbuf, spbuf, lbuf, rsbuf, acc_ref, r_ref, *, blk, n_items):
    sign_bit = jnp.uint32(0x80000000)
    last = n_items - 1
    ALL, CAUSAL, NONE = 0, 1, 2

    for ref in (dbuf, spbuf, lbuf, rsbuf, acc_ref, r_ref):
        ref[...] = jnp.zeros_like(ref)

    def both_heads(ref, start):
        rows = pl.ds(pl.multiple_of(start, blk), blk)
        return jnp.concatenate([ref[rows, h * LANES:(h + 1) * LANES]
                                for h in range(HEADS_PER_BLOCK)], axis=0)

    def item(idx):
        c = jnp.clip(idx, 0, last)
        return ti_ref[c], tj_ref[c]

    def step(p, slot):
        other = 1 - slot
        i1, j1 = item(p)
        ks = k_ref[pl.ds(pl.multiple_of(j1 * blk, blk), blk), :]
        w = lax.dot_general(both_heads(q_ref, i1 * blk), ks, (((1,), (1,)), ((), ())),
                            preferred_element_type=F32)
        i3, j3 = item(p - 2)
        valid = p - 2 <= last
        first = jnp.logical_and(i3 == j3, valid)
        mask3 = mask_ref[jnp.where(valid, (i3 == j3).astype(I32), NONE)]
        r_prev = jnp.where(first, 0.0, r_ref[...])
        r_wide = jnp.concatenate([r_prev] * (blk // LANES), axis=1)
        a = jnp.exp2(dbuf[slot] - (lbuf[other] + r_wide))
        ab = a.astype(BF16) * jnp.concatenate([mask3, mask3], axis=0)
        r_ref[...] = r_prev + rsbuf[other]
        a_cat = jnp.concatenate([ab[:blk], ab[blk:]], axis=1)
        acc = jnp.where(first, 0.0, acc_ref[...]) + jnp.dot(
            a_cat, both_heads(v_ref, j3 * blk), preferred_element_type=F32)
        acc_ref[...] = acc
        o_ref[pl.ds(pl.multiple_of(i3 * blk, blk), blk), :] = acc.astype(o_ref.dtype)
        sums = jnp.dot(spbuf[other], tri_ref[...], preferred_element_type=F32)
        lbuf[slot] = sums[:, :blk]
        rsbuf[slot] = sums[:, blk:]
        mask1 = mask_ref[(i1 == j1).astype(I32)]
        neg_abs = lax.bitcast_convert_type(
            lax.bitcast_convert_type(w, jnp.uint32) | sign_bit, F32)
        sp2 = jnp.maximum(w, 0.0) + jnp.log(1.0 + jnp.exp2(neg_abs)) * LOG2E
        dbuf[slot] = w - sp2
        spbuf[slot] = sp2.astype(BF16) * jnp.concatenate([mask1, mask1], axis=0)

    def unrolled(it, carry):
        for s in range(ATTN_UNROLL):
            step(ATTN_UNROLL * it + s, s % 2)
        return carry

    lax.fori_loop(0, pl.cdiv(n_items + 2, ATTN_UNROLL), unrolled, 0)


def _attention(q, k, v, bsz, seq):
    sb = k.shape[-1]
    blk = min(ATTN_BLOCK, seq)
    n_blk = seq // blk
    n_pairs = sb // LANES
    q3, k3, v3 = (a.reshape(bsz, seq, a.shape[-1]) for a in (q, k, v))
    per_head = HEADS_PER_BLOCK * LANES
    jj = lax.broadcasted_iota(I32, (blk, blk), 0)
    ss = lax.broadcasted_iota(I32, (blk, blk), 1)
    tri = jnp.concatenate([(jj > ss).astype(BF16), jnp.ones((blk, LANES), BF16)], axis=1)
    masks = jnp.stack([jnp.ones((blk, blk), BF16), (ss < jj).astype(BF16),
                       jnp.zeros((blk, blk), BF16)])
    items = [(i, j) for i in range(n_blk) for j in range(i, -1, -1)]
    item_i = jnp.asarray([i for i, _ in items], I32)
    item_j = jnp.asarray([j for _, j in items], I32)
    whole = lambda b, p, ti, tj: (b, 0, p)
    grid_spec = pltpu.PrefetchScalarGridSpec(
        num_scalar_prefetch=2,
        grid=(bsz, n_pairs),
        in_specs=[
            pl.BlockSpec((None, seq, per_head), whole),
            pl.BlockSpec((None, seq, LANES), whole),
            pl.BlockSpec((None, seq, per_head), whole),
            pl.BlockSpec((blk, blk + LANES), lambda b, p, ti, tj: (0, 0)),
            pl.BlockSpec((3, blk, blk), lambda b, p, ti, tj: (0, 0, 0)),
        ],
        out_specs=pl.BlockSpec((None, seq, LANES), whole),
        scratch_shapes=[
            pltpu.VMEM((2, HEADS_PER_BLOCK * blk, blk), F32),
            pltpu.VMEM((2, HEADS_PER_BLOCK * blk, blk), BF16),
            pltpu.VMEM((2, HEADS_PER_BLOCK * blk, blk), F32),
            pltpu.VMEM((2, HEADS_PER_BLOCK * blk, LANES), F32),
            pltpu.VMEM((blk, LANES), F32),
            pltpu.VMEM((HEADS_PER_BLOCK * blk, LANES), F32),
        ],
    )
    out = pl.pallas_call(
        functools.partial(_attn_kernel, blk=blk, n_items=len(items)),
        grid_spec=grid_spec,
        out_shape=jax.ShapeDtypeStruct((bsz, seq, sb), BF16),
        compiler_params=pltpu.CompilerParams(
            dimension_semantics=("arbitrary", "arbitrary"), vmem_limit_bytes=40 * MIB),
        name="sb_attention",
    )(item_i, item_j, q3, k3, v3, tri, masks)
    return out.reshape(bsz * seq, sb)


def _ssm_params(lam_re, lam_im, log_dt, b_re, b_im, c_re, c_im, chunk, n_chunks):
    lam = lax.complex(lam_re.astype(F32), lam_im.astype(F32))
    dt = jnp.exp(log_dt.astype(F32))[:, None]
    lam_dt = lam * dt
    lam_bar = jnp.exp(lam_dt)
    b_bar = ((lam_bar - 1.0) / lam)[:, :, None] * lax.complex(b_re.astype(F32), b_im.astype(F32))
    b_t = jnp.swapaxes(b_bar, 1, 2)
    c_mat = lax.complex(c_re.astype(F32), c_im.astype(F32))
    steps = jnp.arange(chunk + 1, dtype=F32)
    pw = jnp.exp(lam_dt[:, None, :] * steps[None, :, None])
    npw = jnp.exp(-lam_dt[:, None, :] * steps[None, :chunk, None])

    def halves(lo, hi):
        return jnp.concatenate([lo, hi], axis=-1)

    def b_side(z):
        return [halves(jnp.real(z), jnp.real(z)), halves(-jnp.imag(z), jnp.imag(z))]

    def c_side(z):
        return [halves(jnp.real(z), jnp.imag(z)), halves(jnp.imag(z), jnp.real(z))]

    time_terms = jnp.stack(
        b_side(npw)
        + b_side(pw[:, chunk - 1::-1])
        + c_side(pw[:, :chunk])
        + c_side(pw[:, 1:]), axis=1)
    chan_terms = jnp.stack(
        [halves(jnp.real(b_t), jnp.imag(b_t)), halves(jnp.imag(b_t), jnp.real(b_t)),
         halves(jnp.real(c_mat), -jnp.real(c_mat)), halves(-jnp.imag(c_mat), -jnp.imag(c_mat))],
        axis=1)
    n_steps = max(1, (n_chunks - 1).bit_length())
    powers = []
    cur = pw[:, chunk, :]
    for _ in range(n_steps):
        powers.append(cur)
        cur = cur * cur
    lam_pow = jnp.stack(powers, axis=1)
    a1 = halves(jnp.real(lam_pow), jnp.real(lam_pow))
    a2 = halves(-jnp.imag(lam_pow), jnp.imag(lam_pow))
    return time_terms, chan_terms, a1, a2


def _ssm_kernel(u_ref, time_ref, chan_ref, a1_ref, a2_ref, y_ref, toep_ref,
                *, n_chunks, n_steps, group):
    lc = toep_ref.shape[0]
    p2 = a1_ref.shape[1]
    group_shift = group.bit_length() - 1
    nt = (((1,), (1,)), ((), ()))

    def table(t1, t2, v1, v2):
        full = (time_ref[t1][:, None, :] * chan_ref[v1][None, :, :]
                + time_ref[t2][:, None, :] * chan_ref[v2][None, :, :])
        return full.reshape(lc, p2).astype(BF16)

    src = table(0, 1, 0, 1)
    to_state = table(2, 3, 0, 1)
    dst = table(4, 5, 2, 3)
    from_state = table(6, 7, 2, 3)
    cb = min(256, lc)
    for j in range(lc // cb):
        blk = lax.dot_general(src, dst[j * cb:(j + 1) * cb], nt, preferred_element_type=F32)
        s_idx = lax.broadcasted_iota(I32, (lc, cb), 0) >> group_shift
        t_idx = (lax.broadcasted_iota(I32, (lc, cb), 1) + j * cb) >> group_shift
        toep_ref[:, j * cb:(j + 1) * cb] = jnp.where(s_idx <= t_idx, blk, 0.0).astype(BF16)
    u = u_ref[...]
    y = jnp.dot(u, toep_ref[...], preferred_element_type=F32)
    z = jnp.dot(u, to_state, preferred_element_type=F32)
    n = lax.broadcasted_iota(I32, z.shape, 0) & (n_chunks - 1)
    x = jnp.where(n >= 1, pltpu.roll(z, 1, 0), 0.0)
    for k in range(n_steps):
        sh = 1 << k
        xs = jnp.where(n >= sh, pltpu.roll(x, sh, 0), 0.0)
        x = x + a1_ref[k:k + 1, :] * xs + a2_ref[k:k + 1, :] * pltpu.roll(xs, p2 // 2, 1)
    y = y + lax.dot_general(x.astype(BF16), from_state, nt, preferred_element_type=F32)
    y_ref[...] = y.astype(y_ref.dtype)


def _ssm(u, bsz, seq, lam_re, lam_im, log_dt, b_re, b_im, c_re, c_im):
    g, p = lam_re.shape
    c = b_re.shape[-1]
    chunk = min(SSM_CHUNK, seq)
    n_chunks = seq // chunk
    assert n_chunks & (n_chunks - 1) == 0 and c & (c - 1) == 0, (n_chunks, c)
    rows = bsz * n_chunks
    lc = chunk * c
    time_terms, chan_terms, a1, a2 = _ssm_params(
        lam_re, lam_im, log_dt, b_re, b_im, c_re, c_im, chunk, n_chunks)
    n_steps = a1.shape[1]
    u_g = u.reshape(bsz, n_chunks, chunk, g, c).transpose(3, 0, 1, 2, 4).reshape(g, rows, lc)
    per_group = lambda i: (i, 0, 0)
    y_g = pl.pallas_call(
        functools.partial(_ssm_kernel, n_chunks=n_chunks, n_steps=n_steps, group=c),
        grid=(g,),
        in_specs=[
            pl.BlockSpec((None, rows, lc), per_group),
            pl.BlockSpec((None,) + time_terms.shape[1:], lambda i: (i, 0, 0, 0)),
            pl.BlockSpec((None,) + chan_terms.shape[1:], lambda i: (i, 0, 0, 0)),
            pl.BlockSpec((None, n_steps, 2 * p), per_group),
            pl.BlockSpec((None, n_steps, 2 * p), per_group),
        ],
        out_specs=pl.BlockSpec((None, rows, lc), per_group),
        out_shape=jax.ShapeDtypeStruct((g, rows, lc), BF16),
        scratch_shapes=[pltpu.VMEM((lc, lc), BF16)],
        compiler_params=pltpu.CompilerParams(
            dimension_semantics=("arbitrary",), vmem_limit_bytes=40 * MIB),
        name="s5_chunked_scan",
    )(u_g, time_terms, chan_terms, a1, a2)
    return y_g.reshape(g, bsz, n_chunks, chunk, c).transpose(1, 2, 3, 0, 4).reshape(bsz * seq, g * c)


def _post_kernel(x_ref, ysb_ref, yss_ref, u_ref, d_ref, wglu_ref, gsb_ref, gssm_ref, wout_ref,
                 ln2_ref, wr_ref, br_ref, tri_ref, x1_ref, xn_ref, idx_ref, gate_ref, rank_ref,
                 count_ref, running_ref, wglu_bf, wout_bf, *, sb, n_exp):
    @pl.when(pl.program_id(0) == 0)
    def _():
        running_ref[...] = jnp.zeros_like(running_ref)
        wglu_bf[...] = wglu_ref[...].astype(BF16)
        wout_bf[...] = wout_ref[...].astype(BF16)

    u = u_ref[...].astype(F32)
    y = yss_ref[...].astype(F32) + d_ref[...] * u
    y = y * (0.5 * (1.0 + jnp.tanh(math.sqrt(2.0 / math.pi) * (y + 0.044715 * (y * y * y)))))
    ab = jnp.dot(y.astype(BF16), wglu_bf[...], preferred_element_type=F32)
    w = ab.shape[1] // 2
    y_ssm = ab[:, :w] * (1.0 / (1.0 + jnp.exp(-ab[:, w:])))
    m_sb = _rms(ysb_ref[...].astype(F32), gsb_ref[...])
    m_ssm = _rms(y_ssm, gssm_ref[...])
    x1 = (x_ref[...]
          + jnp.dot(m_sb.astype(BF16), wout_bf[:sb, :], preferred_element_type=F32)
          + jnp.dot(m_ssm.astype(BF16), wout_bf[sb:, :], preferred_element_type=F32))
    x1_ref[...] = x1
    xn = _rms(x1, ln2_ref[...])
    xn_ref[...] = xn
    logits = jnp.dot(xn, wr_ref[...], preferred_element_type=F32,
                     precision=lax.Precision.HIGHEST) + br_ref[...]
    lane = lax.broadcasted_iota(I32, logits.shape, 1).astype(F32)
    out_lane = lax.broadcasted_iota(I32, (logits.shape[0], LANES), 1)
    idx_out = jnp.zeros((logits.shape[0], LANES), F32)
    val_out = jnp.zeros((logits.shape[0], LANES), F32)
    top = None
    denom = None
    work = logits
    chosen = []
    for k in range(TOP_K):
        m = jnp.max(work, axis=-1, keepdims=True)
        sel = jnp.min(jnp.where(work == m, lane, float(n_exp)), axis=-1, keepdims=True)
        hit = lane == sel
        chosen.append(hit)
        work = jnp.where(hit, -jnp.inf, work)
        if k == 0:
            top = m
        e = jnp.exp(m - top)
        denom = e if denom is None else denom + e
        idx_out = jnp.where(out_lane == k, sel, idx_out)
        val_out = jnp.where(out_lane == k, e, val_out)
    idx_ref[...] = idx_out[:, :TOP_K].astype(I32)
    gate_ref[...] = (val_out / denom)[:, :TOP_K]
    member = functools.reduce(jnp.logical_or, chosen).astype(BF16)
    before = running_ref[...] + jnp.dot(tri_ref[...], member, preferred_element_type=F32)
    rank_out = jnp.zeros((logits.shape[0], LANES), F32)
    for k in range(TOP_K):
        rk = jnp.sum(jnp.where(chosen[k], before, 0.0), axis=-1, keepdims=True)
        rank_out = jnp.where(out_lane == k, rk, rank_out)
    rank_ref[...] = rank_out[:, :TOP_K].astype(I32)
    running = running_ref[...] + jnp.sum(member.astype(F32), axis=0, keepdims=True)
    running_ref[...] = running
    count_ref[...] = running


def _post(x2, y_sb, y_ss, u, ssm_d, w_glu, g_sb, g_ssm, w_out, ln2_g, w_router, b_router):
    t, d = x2.shape
    sb = y_sb.shape[1]
    w = y_ss.shape[1]
    n_exp = w_router.shape[1]
    tm = min(TOKEN_TILE, t)
    row = lambda i: (i, 0)
    fixed = lambda i: (0, 0)
    earlier = (lax.broadcasted_iota(I32, (tm, tm), 1)
               < lax.broadcasted_iota(I32, (tm, tm), 0)).astype(BF16)
    return pl.pallas_call(
        functools.partial(_post_kernel, sb=sb, n_exp=n_exp),
        grid=(t // tm,),
        in_specs=[
            pl.BlockSpec((tm, d), row),
            pl.BlockSpec((tm, sb), row),
            pl.BlockSpec((tm, w), row),
            pl.BlockSpec((tm, w), row),
            pl.BlockSpec((1, w), fixed),
            pl.BlockSpec((w, 2 * w), fixed),
            pl.BlockSpec((1, sb), fixed),
            pl.BlockSpec((1, w), fixed),
            pl.BlockSpec((sb + w, d), fixed),
            pl.BlockSpec((1, d), fixed),
            pl.BlockSpec((d, n_exp), fixed),
            pl.BlockSpec((1, n_exp), fixed),
            pl.BlockSpec((tm, tm), fixed),
        ],
        out_specs=[
            pl.BlockSpec((tm, d), row),
            pl.BlockSpec((tm, d), row),
            pl.BlockSpec((tm, TOP_K), row),
            pl.BlockSpec((tm, TOP_K), row),
            pl.BlockSpec((tm, TOP_K), row),
            pl.BlockSpec((1, n_exp), fixed),
        ],
        out_shape=[
            jax.ShapeDtypeStruct((t, d), F32),
            jax.ShapeDtypeStruct((t, d), F32),
            jax.ShapeDtypeStruct((t, TOP_K), I32),
            jax.ShapeDtypeStruct((t, TOP_K), F32),
            jax.ShapeDtypeStruct((t, TOP_K), I32),
            jax.ShapeDtypeStruct((1, n_exp), F32),
        ],
        scratch_shapes=[pltpu.VMEM((1, n_exp), F32), pltpu.VMEM((w, 2 * w), BF16),
                        pltpu.VMEM((sb + w, d), BF16)],
        compiler_params=pltpu.CompilerParams(
            dimension_semantics=("arbitrary",), vmem_limit_bytes=48 * MIB),
        name="post_mixer_router",
    )(x2, y_sb, y_ss, u, ssm_d.reshape(1, w), w_glu, g_sb.reshape(1, sb),
      g_ssm.reshape(1, w), w_out, ln2_g.reshape(1, d), w_router,
      b_router.reshape(1, n_exp), earlier)


def _plan(top_idx, rank, counts, rows_per_block):
    t, k = top_idx.shape
    n_exp = counts.shape[-1]
    counts = counts.reshape(n_exp).astype(I32)
    padded = ((counts + rows_per_block - 1) // rows_per_block) * rows_per_block
    pad_ends = jnp.cumsum(padded)
    pad_starts = pad_ends - padded
    start_of = jnp.sum(jnp.where(top_idx[..., None] == jnp.arange(n_exp, dtype=I32),
                                 pad_starts, 0), axis=-1)
    dest = (start_of + rank).astype(I32)
    n_rows = t * k + n_exp * rows_per_block
    n_blocks = n_rows // rows_per_block
    block_start = jnp.arange(n_blocks, dtype=I32) * rows_per_block
    block_expert = jnp.minimum(
        jnp.sum((pad_ends[None, :] <= block_start[:, None]).astype(I32), axis=1), n_exp - 1)
    n_used = (pad_ends[-1] // rows_per_block).astype(I32).reshape(1)
    fill_start = jnp.concatenate([jnp.maximum(pad_ends - rows_per_block, 0).astype(I32), n_used])
    return dest, block_expert, n_used, fill_start, n_blocks


def _dispatch_kernel(fill_ref, dest_hbm, xn_ref, xs_hbm, idx_smem, zeros_ref, idx_sem, row_sem,
                     fill_sem, *, tile, n_exp, fill_rows):
    i = pl.program_id(0)
    n = pl.num_programs(0)
    per_tile = TOP_K * tile

    def idx_copy(b):
        src = dest_hbm.at[pl.ds(pl.multiple_of(b * per_tile, per_tile), per_tile)]
        dst = idx_smem.at[pl.ds(pl.multiple_of((b % 2) * per_tile, per_tile), per_tile)]
        return pltpu.make_async_copy(src, dst, idx_sem.at[b % 2])

    @pl.when(i == 0)
    def _():
        idx_copy(0).start()
        zeros_ref[...] = jnp.zeros_like(zeros_ref)
        for e in range(n_exp):
            start = pl.multiple_of(fill_ref[e], 8)
            pltpu.make_async_copy(zeros_ref, xs_hbm.at[pl.ds(start, fill_rows), 0, :],
                                  fill_sem).start()
        for e in range(n_exp):
            pltpu.make_async_copy(zeros_ref, xs_hbm.at[pl.ds(0, fill_rows), 0, :],
                                  fill_sem).wait()
        n_blocks = xs_hbm.shape[0] // fill_rows

        def fill_unused(b, carry):
            cp = pltpu.make_async_copy(
                zeros_ref, xs_hbm.at[pl.ds(pl.multiple_of(b * fill_rows, fill_rows), fill_rows), 0, :],
                fill_sem)
            cp.start()
            cp.wait()
            return carry

        lax.fori_loop(fill_ref[n_exp], n_blocks, fill_unused, 0)

    idx_copy(i).wait()

    @pl.when(i + 1 < n)
    def _():
        idx_copy(i + 1).start()

    base = pl.multiple_of((i % 2) * per_tile, per_tile)
    quarter = tile // TOP_K
    for part in range(TOP_K):
        def issue(g, carry, part=part):
            group = part * (quarter // SUBLANES) + g
            for s in range(SUBLANES):
                for k in range(TOP_K):
                    dst_row = idx_smem[base + (group * SUBLANES + s) * TOP_K + k]
                    pltpu.make_async_copy(xn_ref.at[group, pl.ds(s, 1), :], xs_hbm.at[dst_row],
                                          row_sem.at[part]).start(priority=k % 2)
            return carry

        lax.fori_loop(0, quarter // SUBLANES, issue, 0)
    for part in range(TOP_K):
        pltpu.make_async_copy(xn_ref, xn_ref, row_sem.at[part]).wait()


def _dispatch(xn, dest, fill_start, n_rows):
    t, d = xn.shape
    tile = min(DISPATCH_TILE, t)
    n_exp = fill_start.shape[0] - 1
    grid_spec = pltpu.PrefetchScalarGridSpec(
        num_scalar_prefetch=1,
        grid=(t // tile,),
        in_specs=[
            pl.BlockSpec(memory_space=pl.ANY),
            pl.BlockSpec((tile // SUBLANES, SUBLANES, d), lambda i, fs: (i, 0, 0)),
        ],
        out_specs=pl.BlockSpec(memory_space=pl.ANY),
        scratch_shapes=[
            pltpu.SMEM((2 * TOP_K * tile,), I32),
            pltpu.VMEM((EXPERT_ROWS, d), F32),
            pltpu.SemaphoreType.DMA((2,)),
            pltpu.SemaphoreType.DMA((TOP_K,)),
            pltpu.SemaphoreType.DMA,
        ],
    )
    return pl.pallas_call(
        functools.partial(_dispatch_kernel, tile=tile, n_exp=n_exp, fill_rows=EXPERT_ROWS),
        grid_spec=grid_spec,
        out_shape=jax.ShapeDtypeStruct((n_rows, 1, d), F32),
        compiler_params=pltpu.CompilerParams(
            dimension_semantics=("arbitrary",), vmem_limit_bytes=40 * MIB),
        name="moe_dispatch",
    )(fill_start, dest.reshape(t * TOP_K), xn.reshape(t // SUBLANES, SUBLANES, d))


def _expert_kernel(be_ref, nused_ref, xs_hbm, wg_ref, bg_ref, wu_ref, bu_ref, wd_ref, bd_ref,
                   y_hbm, xbuf, ybuf, x_sem, y_sem, wg_bf, wu_bf, wd_bf, *, rows):
    i = pl.program_id(0)
    n = nused_ref[0]

    def x_copy(b):
        return pltpu.make_async_copy(
            xs_hbm.at[pl.ds(pl.multiple_of(b * rows, rows), rows), 0, :], xbuf.at[b % 2],
            x_sem.at[b % 2])

    def y_copy(b):
        return pltpu.make_async_copy(
            ybuf.at[b % 2], y_hbm.at[pl.ds(pl.multiple_of(b * rows, rows), rows), 0, :],
            y_sem.at[b % 2])

    @pl.when(i < n)
    def _():
        @pl.when(i == 0)
        def _():
            x_copy(0).start()

        @pl.when(i + 1 < n)
        def _():
            x_copy(i + 1).start()

        changed = jnp.logical_or(i == 0, be_ref[i] != be_ref[jnp.maximum(i - 1, 0)])

        @pl.when(changed)
        def _():
            wg_bf[...] = wg_ref[...].astype(BF16)
            wu_bf[...] = wu_ref[...].astype(BF16)
            wd_bf[...] = wd_ref[...].astype(BF16)

        x_copy(i).wait()
        x = xbuf[i % 2].astype(BF16)
        gate = jnp.dot(x, wg_bf[...], preferred_element_type=F32) + bg_ref[...]
        up = jnp.dot(x, wu_bf[...], preferred_element_type=F32) + bu_ref[...]
        gate = jnp.minimum(gate, SWIGLU_LIMIT)
        up = jnp.clip(up, -SWIGLU_LIMIT, SWIGLU_LIMIT)
        glu = gate * (1.0 / (1.0 + jnp.exp(-SWIGLU_ALPHA * gate)))
        hidden = ((up + 1.0) * glu).astype(BF16)
        y = jnp.dot(hidden, wd_bf[...], preferred_element_type=F32) + bd_ref[...]

        @pl.when(i >= 2)
        def _():
            y_copy(i - 2).wait()
        ybuf[i % 2] = y
        y_copy(i).start()

        @pl.when(i == n - 1)
        def _():
            y_copy(i).wait()

            @pl.when(i >= 1)
            def _():
                y_copy(i - 1).wait()

    @pl.when(i >= n)
    def _():
        ybuf[i % 2] = jnp.zeros((rows, ybuf.shape[2]), F32)
        y_copy(i).start()
        y_copy(i).wait()


def _experts(x_sorted, block_expert, n_used, n_blocks, w_gate, b_gate, w_up, b_up, w_down, b_down):
    n_rows, _, d = x_sorted.shape
    n_exp, _, f = w_gate.shape
    rows = EXPERT_ROWS
    wmap = lambda i, be, nu: (be[i], 0, 0)
    grid_spec = pltpu.PrefetchScalarGridSpec(
        num_scalar_prefetch=2,
        grid=(n_blocks,),
        in_specs=[
            pl.BlockSpec(memory_space=pl.ANY),
            pl.BlockSpec((None, d, f), wmap),
            pl.BlockSpec((None, 1, f), wmap),
            pl.BlockSpec((None, d, f), wmap),
            pl.BlockSpec((None, 1, f), wmap),
            pl.BlockSpec((None, f, d), wmap),
            pl.BlockSpec((None, 1, d), wmap),
        ],
        out_specs=pl.BlockSpec(memory_space=pl.ANY),
        scratch_shapes=[
            pltpu.VMEM((2, rows, d), F32),
            pltpu.VMEM((2, rows, d), F32),
            pltpu.SemaphoreType.DMA((2,)),
            pltpu.SemaphoreType.DMA((2,)),
            pltpu.VMEM((d, f), BF16),
            pltpu.VMEM((d, f), BF16),
            pltpu.VMEM((f, d), BF16),
        ],
    )
    return pl.pallas_call(
        functools.partial(_expert_kernel, rows=rows),
        grid_spec=grid_spec,
        out_shape=jax.ShapeDtypeStruct((n_rows, 1, d), F32),
        compiler_params=pltpu.CompilerParams(
            dimension_semantics=("arbitrary",), vmem_limit_bytes=52 * MIB),
        name="moe_experts",
    )(block_expert, n_used, x_sorted,
      w_gate, b_gate.reshape(n_exp, 1, f), w_up, b_up.reshape(n_exp, 1, f),
      w_down, b_down.reshape(n_exp, 1, d))


def _combine_kernel(dest_hbm, y_hbm, x1_ref, gate_ref, lnf_ref, o_ref, idx_smem, ybuf,
                    idx_sem, row_sem, *, tile):
    i = pl.program_id(0)
    n = pl.num_programs(0)
    per_tile = TOP_K * tile

    def slot_base(b):
        return pl.multiple_of((b % 3) * per_tile, per_tile)

    def idx_copy(b):
        src = dest_hbm.at[pl.ds(pl.multiple_of(b * per_tile, per_tile), per_tile)]
        return pltpu.make_async_copy(src, idx_smem.at[pl.ds(slot_base(b), per_tile)],
                                     idx_sem.at[b % 3])

    def gather(b):
        base = slot_base(b)
        buf = ybuf.at[b % 2]
        sem = row_sem.at[b % 2]

        def issue(g, carry):
            for s in range(SUBLANES):
                pltpu.make_async_copy(y_hbm.at[idx_smem[base + g * SUBLANES + s]],
                                      buf.at[g, pl.ds(s, 1), :], sem).start(priority=s % 2)
            return carry

        lax.fori_loop(0, per_tile // SUBLANES, issue, 0)

    @pl.when(i == 0)
    def _():
        idx_copy(0).start()
        idx_copy(0).wait()
        gather(0)

        @pl.when(n > 1)
        def _():
            idx_copy(1).start()

    @pl.when(i + 1 < n)
    def _():
        idx_copy(i + 1).wait()

        @pl.when(i + 2 < n)
        def _():
            idx_copy(i + 2).start()
        gather(i + 1)

    cur = i % 2
    pltpu.make_async_copy(ybuf.at[1 - cur], ybuf.at[cur], row_sem.at[cur]).wait()
    gates = gate_ref[...]
    x = x1_ref[...]
    groups = tile // SUBLANES
    for k in range(TOP_K):
        yk = ybuf[cur, k * groups:(k + 1) * groups].reshape(tile, x.shape[1])
        x = x + gates[:, k:k + 1] * yk
    o_ref[...] = _rms(x, lnf_ref[...])


def _combine(dest, y_rows, x1, gates, ln_f_g):
    t, d = x1.shape
    tile = min(COMBINE_TILE, t)
    n_tiles = t // tile
    dest_tiles = dest.reshape(n_tiles, tile, TOP_K).transpose(0, 2, 1).reshape(n_tiles * TOP_K * tile)
    return pl.pallas_call(
        functools.partial(_combine_kernel, tile=tile),
        grid=(n_tiles,),
        in_specs=[
            pl.BlockSpec(memory_space=pl.ANY),
            pl.BlockSpec(memory_space=pl.ANY),
            pl.BlockSpec((tile, d), lambda i: (i, 0)),
            pl.BlockSpec((tile, TOP_K), lambda i: (i, 0)),
            pl.BlockSpec((1, d), lambda i: (0, 0)),
        ],
        out_specs=pl.BlockSpec((tile, d), lambda i: (i, 0)),
        out_shape=jax.ShapeDtypeStruct((t, d), F32),
        scratch_shapes=[
            pltpu.SMEM((3 * TOP_K * tile,), I32),
            pltpu.VMEM((2, TOP_K * tile // SUBLANES, SUBLANES, d), F32),
            pltpu.SemaphoreType.DMA((3,)),
            pltpu.SemaphoreType.DMA((2,)),
        ],
        compiler_params=pltpu.CompilerParams(
            dimension_semantics=("arbitrary",), vmem_limit_bytes=40 * MIB),
        name="moe_combine",
    )(dest_tiles, y_rows, x1, gates, ln_f_g.reshape(1, d))


def kernel(x, ln1_g, w_in, lam_re, lam_im, log_dt, ssm_b_re, ssm_b_im, ssm_c_re, ssm_c_im,
           ssm_d, w_glu, g_sb, g_ssm, w_out, ln2_g, w_router, b_router, w_gate, b_gate,
           w_up, b_up, w_down, b_down, ln_f_g):
    bsz, seq, d = x.shape
    assert ln1_g.shape[0] == 1, "depth-1 block only"
    ssm_w = ssm_d.shape[1]
    sb = g_sb.shape[1]
    x2 = x.reshape(bsz * seq, d)
    q, k, v, u = _in_proj(x2, ln1_g[0], w_in[0], sb, ssm_w)
    y_sb = _attention(q, k, v, bsz, seq)
    y_ss = _ssm(u, bsz, seq, lam_re[0], lam_im[0], log_dt[0], ssm_b_re[0], ssm_b_im[0],
                ssm_c_re[0], ssm_c_im[0])
    x1, xn, top_idx, gates, rank, counts = _post(
        x2, y_sb, y_ss, u, ssm_d[0], w_glu[0], g_sb[0], g_ssm[0], w_out[0], ln2_g[0],
        w_router[0], b_router[0])
    dest, block_expert, n_used, fill_start, n_blocks = _plan(top_idx, rank, counts, EXPERT_ROWS)
    x_sorted = _dispatch(xn, dest, fill_start, n_blocks * EXPERT_ROWS)
    y_rows = _experts(x_sorted, block_expert, n_used, n_blocks, w_gate[0], b_gate[0],
                      w_up[0], b_up[0], w_down[0], b_down[0])
    out = _combine(dest, y_rows, x1, gates, ln_f_g)
    return out.reshape(bsz, seq, d)
```

```python
import functools
import math

import jax
import jax.numpy as jnp
from jax import lax
from jax.experimental import pallas as pl
from jax.experimental.pallas import tpu as pltpu

F32 = jnp.float32
BF16 = jnp.bfloat16
I32 = jnp.int32

EPS = 1e-5
SB_HEAD_DIM = 64
SSM_GROUP = 16
SSM_STATE = 64
TOP_K = 4
SWIGLU_LIMIT = 7.0
SWIGLU_ALPHA = 1.702

LANES = 128
SUBLANES = 8
HEADS_PER_BLOCK = LANES // SB_HEAD_DIM
ATTN_BLOCK = 256
ATTN_UNROLL = 4
SSM_CHUNK = 64
TOKEN_TILE = 512
EXPERT_ROWS = 256
COMBINE_TILE = 256
DISPATCH_TILE = 1024
MIB = 1024 * 1024
LOG2E = 1.4426950408889634


def _rms(x, g):
    return x * lax.rsqrt(jnp.mean(x * x, axis=-1, keepdims=True) + EPS) * g


def _in_proj_kernel(x_ref, g_ref, w_ref, q_ref, k_ref, v_ref, u_ref, w_bf, *, sb, scale):
    @pl.when(pl.program_id(0) == 0)
    def _():
        w_bf[...] = w_ref[...].astype(BF16)

    h = _rms(x_ref[...], g_ref[...])
    proj = jnp.dot(h.astype(BF16), w_bf[...], preferred_element_type=F32)
    k_ref[...] = proj[:, sb:2 * sb].astype(BF16)
    u_ref[...] = proj[:, 3 * sb:].astype(BF16)
    lane = lax.broadcasted_iota(I32, (1, LANES), 1)
    q = (proj[:, :sb] * scale).astype(BF16)
    v = proj[:, 2 * sb:3 * sb].astype(BF16)
    zero = jnp.zeros((), BF16)
    for pair in range(sb // LANES):
        cols = slice(pair * LANES, (pair + 1) * LANES)
        for head in range(HEADS_PER_BLOCK):
            own = (lane // SB_HEAD_DIM) == head
            out = slice((pair * HEADS_PER_BLOCK + head) * LANES,
                        (pair * HEADS_PER_BLOCK + head + 1) * LANES)
            q_ref[:, out] = jnp.where(own, q[:, cols], zero)
            v_ref[:, out] = jnp.where(own, v[:, cols], zero)


def _in_proj(x2, ln1_g, w_in, sb, ssm_w):
    t, d = x2.shape
    n_in = w_in.shape[1]
    tm = min(TOKEN_TILE, t)
    scale = LOG2E / math.sqrt(SB_HEAD_DIM)
    return pl.pallas_call(
        functools.partial(_in_proj_kernel, sb=sb, scale=scale),
        grid=(t // tm,),
        in_specs=[
            pl.BlockSpec((tm, d), lambda i: (i, 0)),
            pl.BlockSpec((1, d), lambda i: (0, 0)),
            pl.BlockSpec((d, n_in), lambda i: (0, 0)),
        ],
        out_specs=[
            pl.BlockSpec((tm, HEADS_PER_BLOCK * sb), lambda i: (i, 0)),
            pl.BlockSpec((tm, sb), lambda i: (i, 0)),
            pl.BlockSpec((tm, HEADS_PER_BLOCK * sb), lambda i: (i, 0)),
            pl.BlockSpec((tm, ssm_w), lambda i: (i, 0)),
        ],
        out_shape=[
            jax.ShapeDtypeStruct((t, HEADS_PER_BLOCK * sb), BF16),
            jax.ShapeDtypeStruct((t, sb), BF16),
            jax.ShapeDtypeStruct((t, HEADS_PER_BLOCK * sb), BF16),
            jax.ShapeDtypeStruct((t, ssm_w), BF16),
        ],
        scratch_shapes=[pltpu.VMEM((d, n_in), BF16)],
        compiler_params=pltpu.CompilerParams(
            dimension_semantics=("arbitrary",), vmem_limit_bytes=48 * MIB),
        name="in_proj",
    )(x2, ln1_g.reshape(1, d), w_in)


def _attn_kernel(ti_ref, tj_ref, q_ref, k_ref, v_ref, tri_ref, mask_ref, o_ref,
                 dbuf, spbuf, lbuf, rsbuf, acc_ref, r_ref, *, blk, n_items):
    sign_bit = jnp.uint32(0x80000000)
    last = n_items - 1
    ALL, CAUSAL, NONE = 0, 1, 2

    for ref in (dbuf, spbuf, lbuf, rsbuf, acc_ref, r_ref):
        ref[...] = jnp.zeros_like(ref)

    def both_heads(ref, start):
        rows = pl.ds(pl.multiple_of(start, blk), blk)
        return jnp.concatenate([ref[rows, h * LANES:(h + 1) * LANES]
                                for h in range(HEADS_PER_BLOCK)], axis=0)

    def item(idx):
        c = jnp.clip(idx, 0, last)
        return ti_ref[c], tj_ref[c]

    def step(p, slot):
        other = 1 - slot
        i1, j1 = item(p)
        ks = k_ref[pl.ds(pl.multiple_of(j1 * blk, blk), blk), :]
        w = lax.dot_general(both_heads(q_ref, i1 * blk), ks, (((1,), (1,)), ((), ())),
                            preferred_element_type=F32)
        i3, j3 = item(p - 2)
        valid = p - 2 <= last
        first = jnp.logical_and(i3 == j3, valid)
        mask3 = mask_ref[jnp.where(valid, (i3 == j3).astype(I32), NONE)]
        r_prev = jnp.where(first, 0.0, r_ref[...])
        r_wide = jnp.concatenate([r_prev] * (blk // LANES), axis=1)
        a = jnp.exp2(dbuf[slot] - (lbuf[other] + r_wide))
        ab = a.astype(BF16) * jnp.concatenate([mask3, mask3], axis=0)
        r_ref[...] = r_prev + rsbuf[other]
        a_cat = jnp.concatenate([ab[:blk], ab[blk:]], axis=1)
        acc = jnp.where(first, 0.0, acc_ref[...]) + jnp.dot(
            a_cat, both_heads(v_ref, j3 * blk), preferred_element_type=F32)
        acc_ref[...] = acc
        o_ref[pl.ds(pl.multiple_of(i3 * blk, blk), blk), :] = acc.astype(o_ref.dtype)
        sums = jnp.dot(spbuf[other], tri_ref[...], preferred_element_type=F32)
        lbuf[slot] = sums[:, :blk]
        rsbuf[slot] = sums[:, blk:]
        mask1 = mask_ref[(i1 == j1).astype(I32)]
        neg_abs = lax.bitcast_convert_type(
            lax.bitcast_convert_type(w, jnp.uint32) | sign_bit, F32)
        sp2 = jnp.maximum(w, 0.0) + jnp.log(1.0 + jnp.exp2(neg_abs)) * LOG2E
        dbuf[slot] = w - sp2
        spbuf[slot] = sp2.astype(BF16) * jnp.concatenate([mask1, mask1], axis=0)

    def unrolled(it, carry):
        for s in range(ATTN_UNROLL):
            step(ATTN_UNROLL * it + s, s % 2)
        return carry

    lax.fori_loop(0, pl.cdiv(n_items + 2, ATTN_UNROLL), unrolled, 0)


def _attention(q, k, v, bsz, seq):
    sb = k.shape[-1]
    blk = min(ATTN_BLOCK, seq)
    n_blk = seq // blk
    n_pairs = sb // LANES
    q3, k3, v3 = (a.reshape(bsz, seq, a.shape[-1]) for a in (q, k, v))
    per_head = HEADS_PER_BLOCK * LANES
    jj = lax.broadcasted_iota(I32, (blk, blk), 0)
    ss = lax.broadcasted_iota(I32, (blk, blk), 1)
    tri = jnp.concatenate([(jj > ss).astype(BF16), jnp.ones((blk, LANES), BF16)], axis=1)
    masks = jnp.stack([jnp.ones((blk, blk), BF16), (ss < jj).astype(BF16),
                       jnp.zeros((blk, blk), BF16)])
    items = [(i, j) for i in range(n_blk) for j in range(i, -1, -1)]
    item_i = jnp.asarray([i for i, _ in items], I32)
    item_j = jnp.asarray([j for _, j in items], I32)
    whole = lambda b, p, ti, tj: (b, 0, p)
    grid_spec = pltpu.PrefetchScalarGridSpec(
        num_scalar_prefetch=2,
        grid=(bsz, n_pairs),
        in_specs=[
            pl.BlockSpec((None, seq, per_head), whole),
            pl.BlockSpec((None, seq, LANES), whole),
            pl.BlockSpec((None, seq, per_head), whole),
            pl.BlockSpec((blk, blk + LANES), lambda b, p, ti, tj: (0, 0)),
            pl.BlockSpec((3, blk, blk), lambda b, p, ti, tj: (0, 0, 0)),
        ],
        out_specs=pl.BlockSpec((None, seq, LANES), whole),
        scratch_shapes=[
            pltpu.VMEM((2, HEADS_PER_BLOCK * blk, blk), F32),
            pltpu.VMEM((2, HEADS_PER_BLOCK * blk, blk), BF16),
            pltpu.VMEM((2, HEADS_PER_BLOCK * blk, blk), F32),
            pltpu.VMEM((2, HEADS_PER_BLOCK * blk, LANES), F32),
            pltpu.VMEM((blk, LANES), F32),
            pltpu.VMEM((HEADS_PER_BLOCK * blk, LANES), F32),
        ],
    )
    out = pl.pallas_call(
        functools.partial(_attn_kernel, blk=blk, n_items=len(items)),
        grid_spec=grid_spec,
        out_shape=jax.ShapeDtypeStruct((bsz, seq, sb), BF16),
        compiler_params=pltpu.CompilerParams(
            dimension_semantics=("arbitrary", "arbitrary"), vmem_limit_bytes=40 * MIB),
        name="sb_attention",
    )(item_i, item_j, q3, k3, v3, tri, masks)
    return out.reshape(bsz * seq, sb)


def _ssm_params(lam_re, lam_im, log_dt, b_re, b_im, c_re, c_im, chunk, n_chunks):
    lam = lax.complex(lam_re.astype(F32), lam_im.astype(F32))
    dt = jnp.exp(log_dt.astype(F32))[:, None]
    lam_dt = lam * dt
    lam_bar = jnp.exp(lam_dt)
    b_bar = ((lam_bar - 1.0) / lam)[:, :, None] * lax.complex(b_re.astype(F32), b_im.astype(F32))
    b_t = jnp.swapaxes(b_bar, 1, 2)
    c_mat = lax.complex(c_re.astype(F32), c_im.astype(F32))
    steps = jnp.arange(chunk + 1, dtype=F32)
    pw = jnp.exp(lam_dt[:, None, :] * steps[None, :, None])
    npw = jnp.exp(-lam_dt[:, None, :] * steps[None, :chunk, None])

    def halves(lo, hi):
        return jnp.concatenate([lo, hi], axis=-1)

    def b_side(z):
        return [halves(jnp.real(z), jnp.real(z)), halves(-jnp.imag(z), jnp.imag(z))]

    def c_side(z):
        return [halves(jnp.real(z), jnp.imag(z)), halves(jnp.imag(z), jnp.real(z))]

    time_terms = jnp.stack(
        b_side(npw)
        + b_side(pw[:, chunk - 1::-1])
        + c_side(pw[:, :chunk])
        + c_side(pw[:, 1:]), axis=1)
    chan_terms = jnp.stack(
        [halves(jnp.real(b_t), jnp.imag(b_t)), halves(jnp.imag(b_t), jnp.real(b_t)),
         halves(jnp.real(c_mat), -jnp.real(c_mat)), halves(-jnp.imag(c_mat), -jnp.imag(c_mat))],
        axis=1)
    n_steps = max(1, (n_chunks - 1).bit_length())
    powers = []
    cur = pw[:, chunk, :]
    for _ in range(n_steps):
        powers.append(cur)
        cur = cur * cur
    lam_pow = jnp.stack(powers, axis=1)
    a1 = halves(jnp.real(lam_pow), jnp.real(lam_pow))
    a2 = halves(-jnp.imag(lam_pow), jnp.imag(lam_pow))
    return time_terms, chan_terms, a1, a2


def _ssm_kernel(u_ref, time_ref, chan_ref, a1_ref, a2_ref, y_ref, toep_ref,
                *, n_chunks, n_steps, group):
    lc = toep_ref.shape[0]
    p2 = a1_ref.shape[1]
    group_shift = group.bit_length() - 1
    nt = (((1,), (1,)), ((), ()))

    def table(t1, t2, v1, v2):
        full = (time_ref[t1][:, None, :] * chan_ref[v1][None, :, :]
                + time_ref[t2][:, None, :] * chan_ref[v2][None, :, :])
        return full.reshape(lc, p2).astype(BF16)

    src = table(0, 1, 0, 1)
    to_state = table(2, 3, 0, 1)
    dst = table(4, 5, 2, 3)
    from_state = table(6, 7, 2, 3)
    cb = min(256, lc)
    for j in range(lc // cb):
        blk = lax.dot_general(src, dst[j * cb:(j + 1) * cb], nt, preferred_element_type=F32)
        s_idx = lax.broadcasted_iota(I32, (lc, cb), 0) >> group_shift
        t_idx = (lax.broadcasted_iota(I32, (lc, cb), 1) + j * cb) >> group_shift
        toep_ref[:, j * cb:(j + 1) * cb] = jnp.where(s_idx <= t_idx, blk, 0.0).astype(BF16)
    u = u_ref[...]
    y = jnp.dot(u, toep_ref[...], preferred_element_type=F32)
    z = jnp.dot(u, to_state, preferred_element_type=F32)
    n = lax.broadcasted_iota(I32, z.shape, 0) & (n_chunks - 1)
    x = jnp.where(n >= 1, pltpu.roll(z, 1, 0), 0.0)
    for k in range(n_steps):
        sh = 1 << k
        xs = jnp.where(n >= sh, pltpu.roll(x, sh, 0), 0.0)
        x = x + a1_ref[k:k + 1, :] * xs + a2_ref[k:k + 1, :] * pltpu.roll(xs, p2 // 2, 1)
    y = y + lax.dot_general(x.astype(BF16), from_state, nt, preferred_element_type=F32)
    y_ref[...] = y.astype(y_ref.dtype)


def _ssm(u, bsz, seq, lam_re, lam_im, log_dt, b_re, b_im, c_re, c_im):
    g, p = lam_re.shape
    c = b_re.shape[-1]
    chunk = min(SSM_CHUNK, seq)
    n_chunks = seq // chunk
    assert n_chunks & (n_chunks - 1) == 0 and c & (c - 1) == 0, (n_chunks, c)
    rows = bsz * n_chunks
    lc = chunk * c
    time_terms, chan_terms, a1, a2 = _ssm_params(
        lam_re, lam_im, log_dt, b_re, b_im, c_re, c_im, chunk, n_chunks)
    n_steps = a1.shape[1]
    u_g = u.reshape(bsz, n_chunks, chunk, g, c).transpose(3, 0, 1, 2, 4).reshape(g, rows, lc)
    per_group = lambda i: (i, 0, 0)
    y_g = pl.pallas_call(
        functools.partial(_ssm_kernel, n_chunks=n_chunks, n_steps=n_steps, group=c),
        grid=(g,),
        in_specs=[
            pl.BlockSpec((None, rows, lc), per_group),
            pl.BlockSpec((None,) + time_terms.shape[1:], lambda i: (i, 0, 0, 0)),
            pl.BlockSpec((None,) + chan_terms.shape[1:], lambda i: (i, 0, 0, 0)),
            pl.BlockSpec((None, n_steps, 2 * p), per_group),
            pl.BlockSpec((None, n_steps, 2 * p), per_group),
        ],
        out_specs=pl.BlockSpec((None, rows, lc), per_group),
        out_shape=jax.ShapeDtypeStruct((g, rows, lc), BF16),
        scratch_shapes=[pltpu.VMEM((lc, lc), BF16)],
        compiler_params=pltpu.CompilerParams(
            dimension_semantics=("arbitrary",), vmem_limit_bytes=40 * MIB),
        name="s5_chunked_scan",
    )(u_g, time_terms, chan_terms, a1, a2)
    return y_g.reshape(g, bsz, n_chunks, chunk, c).transpose(1, 2, 3, 0, 4).reshape(bsz * seq, g * c)


def _post_kernel(x_ref, ysb_ref, yss_ref, u_ref, d_ref, wglu_ref, gsb_ref, gssm_ref, wout_ref,
                 ln2_ref, wr_ref, br_ref, tri_ref, x1_ref, xn_ref, idx_ref, gate_ref, rank_ref,
                 count_ref, running_ref, wglu_bf, wout_bf, *, sb, n_exp):
    @pl.when(pl.program_id(0) == 0)
    def _():
        running_ref[...] = jnp.zeros_like(running_ref)
        wglu_bf[...] = wglu_ref[...].astype(BF16)
        wout_bf[...] = wout_ref[...].astype(BF16)

    u = u_ref[...].astype(F32)
    y = yss_ref[...].astype(F32) + d_ref[...] * u
    y = y * (0.5 * (1.0 + jnp.tanh(math.sqrt(2.0 / math.pi) * (y + 0.044715 * (y * y * y)))))
    ab = jnp.dot(y.astype(BF16), wglu_bf[...], preferred_element_type=F32)
    w = ab.shape[1] // 2
    y_ssm = ab[:, :w] * (1.0 / (1.0 + jnp.exp(-ab[:, w:])))
    m_sb = _rms(ysb_ref[...].astype(F32), gsb_ref[...])
    m_ssm = _rms(y_ssm, gssm_ref[...])
    x1 = (x_ref[...]
          + jnp.dot(m_sb.astype(BF16), wout_bf[:sb, :], preferred_element_type=F32)
          + jnp.dot(m_ssm.astype(BF16), wout_bf[sb:, :], preferred_element_type=F32))
    x1_ref[...] = x1
    xn = _rms(x1, ln2_ref[...])
    xn_ref[...] = xn
    def split(a):
        hi = a.astype(BF16)
        return hi, (a - hi.astype(F32)).astype(BF16)

    xn_hi, xn_lo = split(xn)
    wr_hi, wr_lo = split(wr_ref[...])
    logits = (jnp.dot(xn_hi, wr_hi, preferred_element_type=F32)
              + jnp.dot(xn_hi, wr_lo, preferred_element_type=F32)
              + jnp.dot(xn_lo, wr_hi, preferred_element_type=F32)) + br_ref[...]
    lane = lax.broadcasted_iota(I32, logits.shape, 1).astype(F32)
    out_lane = lax.broadcasted_iota(I32, (logits.shape[0], LANES), 1)
    idx_out = jnp.zeros((logits.shape[0], LANES), F32)
    val_out = jnp.zeros((logits.shape[0], LANES), F32)
    top = None
    denom = None
    work = logits
    chosen = []
    for k in range(TOP_K):
        m = jnp.max(work, axis=-1, keepdims=True)
        sel = jnp.min(jnp.where(work == m, lane, float(n_exp)), axis=-1, keepdims=True)
        hit = lane == sel
        chosen.append(hit)
        work = jnp.where(hit, -jnp.inf, work)
        if k == 0:
            top = m
        e = jnp.exp(m - top)
        denom = e if denom is None else denom + e
        idx_out = jnp.where(out_lane == k, sel, idx_out)
        val_out = jnp.where(out_lane == k, e, val_out)
    idx_ref[...] = idx_out[:, :TOP_K].astype(I32)
    gate_ref[...] = (val_out / denom)[:, :TOP_K]
    member = functools.reduce(jnp.logical_or, chosen).astype(BF16)
    before = running_ref[...] + jnp.dot(tri_ref[...], member, preferred_element_type=F32)
    rank_out = jnp.zeros((logits.shape[0], LANES), F32)
    for k in range(TOP_K):
        rk = jnp.sum(jnp.where(chosen[k], before, 0.0), axis=-1, keepdims=True)
        rank_out = jnp.where(out_lane == k, rk, rank_out)
    rank_ref[...] = rank_out[:, :TOP_K].astype(I32)
    running = running_ref[...] + jnp.sum(member.astype(F32), axis=0, keepdims=True)
    running_ref[...] = running
    count_ref[...] = running


def _post(x2, y_sb, y_ss, u, ssm_d, w_glu, g_sb, g_ssm, w_out, ln2_g, w_router, b_router):
    t, d = x2.shape
    sb = y_sb.shape[1]
    w = y_ss.shape[1]
    n_exp = w_router.shape[1]
    tm = min(TOKEN_TILE, t)
    row = lambda i: (i, 0)
    fixed = lambda i: (0, 0)
    earlier = (lax.broadcasted_iota(I32, (tm, tm), 1)
               < lax.broadcasted_iota(I32, (tm, tm), 0)).astype(BF16)
    return pl.pallas_call(
        functools.partial(_post_kernel, sb=sb, n_exp=n_exp),
        grid=(t // tm,),
        in_specs=[
            pl.BlockSpec((tm, d), row),
            pl.BlockSpec((tm, sb), row),
            pl.BlockSpec((tm, w), row),
            pl.BlockSpec((tm, w), row),
            pl.BlockSpec((1, w), fixed),
            pl.BlockSpec((w, 2 * w), fixed),
            pl.BlockSpec((1, sb), fixed),
            pl.BlockSpec((1, w), fixed),
            pl.BlockSpec((sb + w, d), fixed),
            pl.BlockSpec((1, d), fixed),
            pl.BlockSpec((d, n_exp), fixed),
            pl.BlockSpec((1, n_exp), fixed),
            pl.BlockSpec((tm, tm), fixed),
        ],
        out_specs=[
            pl.BlockSpec((tm, d), row),
            pl.BlockSpec((tm, d), row),
            pl.BlockSpec((tm, TOP_K), row),
            pl.BlockSpec((tm, TOP_K), row),
            pl.BlockSpec((tm, TOP_K), row),
            pl.BlockSpec((1, n_exp), fixed),
        ],
        out_shape=[
            jax.ShapeDtypeStruct((t, d), F32),
            jax.ShapeDtypeStruct((t, d), F32),
            jax.ShapeDtypeStruct((t, TOP_K), I32),
            jax.ShapeDtypeStruct((t, TOP_K), F32),
            jax.ShapeDtypeStruct((t, TOP_K), I32),
            jax.ShapeDtypeStruct((1, n_exp), F32),
        ],
        scratch_shapes=[pltpu.VMEM((1, n_exp), F32), pltpu.VMEM((w, 2 * w), BF16),
                        pltpu.VMEM((sb + w, d), BF16)],
        compiler_params=pltpu.CompilerParams(
            dimension_semantics=("arbitrary",), vmem_limit_bytes=48 * MIB),
        name="post_mixer_router",
    )(x2, y_sb, y_ss, u, ssm_d.reshape(1, w), w_glu, g_sb.reshape(1, sb),
      g_ssm.reshape(1, w), w_out, ln2_g.reshape(1, d), w_router,
      b_router.reshape(1, n_exp), earlier)


def _plan(top_idx, rank, counts, rows_per_block):
    t, k = top_idx.shape
    n_exp = counts.shape[-1]
    counts = counts.reshape(n_exp).astype(I32)
    padded = ((counts + rows_per_block - 1) // rows_per_block) * rows_per_block
    pad_ends = jnp.cumsum(padded)
    pad_starts = pad_ends - padded
    start_of = jnp.sum(jnp.where(top_idx[..., None] == jnp.arange(n_exp, dtype=I32),
                                 pad_starts, 0), axis=-1)
    dest = (start_of + rank).astype(I32)
    n_rows = t * k + n_exp * rows_per_block
    n_blocks = n_rows // rows_per_block
    block_start = jnp.arange(n_blocks, dtype=I32) * rows_per_block
    block_expert = jnp.minimum(
        jnp.sum((pad_ends[None, :] <= block_start[:, None]).astype(I32), axis=1), n_exp - 1)
    n_used = (pad_ends[-1] // rows_per_block).astype(I32).reshape(1)
    fill_start = jnp.concatenate([jnp.maximum(pad_ends - rows_per_block, 0).astype(I32), n_used])
    experts = jnp.arange(n_exp, dtype=I32)
    nonempty = counts > 0
    order = jnp.cumsum(nonempty.astype(I32)) - nonempty.astype(I32)
    later_ne = jnp.where((experts[None, :] > experts[:, None]) & nonempty[None, :],
                         experts[None, :], n_exp)
    next_tbl = jnp.min(later_ne, axis=1)
    onehot = block_expert[:, None] == experts[None, :]
    w_slot = jnp.sum(jnp.where(onehot, order & 1, 0), axis=1).astype(I32)
    w_next = jnp.sum(jnp.where(onehot, next_tbl, 0), axis=1).astype(I32)
    return dest, (block_expert, n_used, w_slot, w_next), fill_start, n_blocks


def _dispatch_kernel(fill_ref, dest_hbm, xn_ref, xs_hbm, idx_smem, zeros_ref, idx_sem, row_sem,
                     fill_sem, *, tile, n_exp, fill_rows):
    i = pl.program_id(0)
    n = pl.num_programs(0)
    per_tile = TOP_K * tile

    def idx_copy(b):
        src = dest_hbm.at[pl.ds(pl.multiple_of(b * per_tile, per_tile), per_tile)]
        dst = idx_smem.at[pl.ds(pl.multiple_of((b % 2) * per_tile, per_tile), per_tile)]
        return pltpu.make_async_copy(src, dst, idx_sem.at[b % 2])

    @pl.when(i == 0)
    def _():
        idx_copy(0).start()
        zeros_ref[...] = jnp.zeros_like(zeros_ref)
        for e in range(n_exp):
            start = pl.multiple_of(fill_ref[e], 8)
            pltpu.make_async_copy(zeros_ref, xs_hbm.at[pl.ds(start, fill_rows), 0, :],
                                  fill_sem).start()
        for e in range(n_exp):
            pltpu.make_async_copy(zeros_ref, xs_hbm.at[pl.ds(0, fill_rows), 0, :],
                                  fill_sem).wait()
        n_blocks = xs_hbm.shape[0] // fill_rows

        def fill_unused(b, carry):
            cp = pltpu.make_async_copy(
                zeros_ref, xs_hbm.at[pl.ds(pl.multiple_of(b * fill_rows, fill_rows), fill_rows), 0, :],
                fill_sem)
            cp.start()
            cp.wait()
            return carry

        lax.fori_loop(fill_ref[n_exp], n_blocks, fill_unused, 0)

    idx_copy(i).wait()

    @pl.when(i + 1 < n)
    def _():
        idx_copy(i + 1).start()

    base = pl.multiple_of((i % 2) * per_tile, per_tile)
    quarter = tile // TOP_K
    for part in range(TOP_K):
        def issue(g, carry, part=part):
            group = part * (quarter // SUBLANES) + g
            for s in range(SUBLANES):
                for k in range(TOP_K):
                    dst_row = idx_smem[base + (group * SUBLANES + s) * TOP_K + k]
                    pltpu.make_async_copy(xn_ref.at[group, pl.ds(s, 1), :], xs_hbm.at[dst_row],
                                          row_sem.at[part]).start(priority=k % 2)
            return carry

        lax.fori_loop(0, quarter // SUBLANES, issue, 0)
    for part in range(TOP_K):
        pltpu.make_async_copy(xn_ref, xn_ref, row_sem.at[part]).wait()


def _dispatch(xn, dest, fill_start, n_rows):
    t, d = xn.shape
    tile = min(DISPATCH_TILE, t)
    n_exp = fill_start.shape[0] - 1
    grid_spec = pltpu.PrefetchScalarGridSpec(
        num_scalar_prefetch=1,
        grid=(t // tile,),
        in_specs=[
            pl.BlockSpec(memory_space=pl.ANY),
            pl.BlockSpec((tile // SUBLANES, SUBLANES, d), lambda i, fs: (i, 0, 0)),
        ],
        out_specs=pl.BlockSpec(memory_space=pl.ANY),
        scratch_shapes=[
            pltpu.SMEM((2 * TOP_K * tile,), I32),
            pltpu.VMEM((EXPERT_ROWS, d), F32),
            pltpu.SemaphoreType.DMA((2,)),
            pltpu.SemaphoreType.DMA((TOP_K,)),
            pltpu.SemaphoreType.DMA,
        ],
    )
    return pl.pallas_call(
        functools.partial(_dispatch_kernel, tile=tile, n_exp=n_exp, fill_rows=EXPERT_ROWS),
        grid_spec=grid_spec,
        out_shape=jax.ShapeDtypeStruct((n_rows, 1, d), F32),
        compiler_params=pltpu.CompilerParams(
            dimension_semantics=("arbitrary",), vmem_limit_bytes=40 * MIB),
        name="moe_dispatch",
    )(fill_start, dest.reshape(t * TOP_K), xn.reshape(t // SUBLANES, SUBLANES, d))


def _expert_kernel(be_ref, nused_ref, wslot_ref, wnext_ref, xs_hbm, wg_hbm, bg_ref, wu_hbm, bu_ref,
                   wd_hbm, bd_ref, y_hbm, xbuf, ybuf, x_sem, y_sem, wg32, wu32, wd32, w_sem,
                   wg_bf, wu_bf, wd_bf, *, rows, n_exp):
    i = pl.program_id(0)
    n = nused_ref[0]

    def w_copies(e, slot):
        return [pltpu.make_async_copy(src.at[e], dst.at[slot], w_sem.at[slot])
                for src, dst in ((wg_hbm, wg32), (wu_hbm, wu32), (wd_hbm, wd32))]

    def x_copy(b):
        return pltpu.make_async_copy(
            xs_hbm.at[pl.ds(pl.multiple_of(b * rows, rows), rows), 0, :], xbuf.at[b % 2],
            x_sem.at[b % 2])

    def y_copy(b):
        return pltpu.make_async_copy(
            ybuf.at[b % 2], y_hbm.at[pl.ds(pl.multiple_of(b * rows, rows), rows), 0, :],
            y_sem.at[b % 2])

    @pl.when(i < n)
    def _():
        @pl.when(i == 0)
        def _():
            x_copy(0).start()
            for cp in w_copies(be_ref[0], wslot_ref[0]):
                cp.start()

        @pl.when(i + 1 < n)
        def _():
            x_copy(i + 1).start()

        changed = jnp.logical_or(i == 0, be_ref[i] != be_ref[jnp.maximum(i - 1, 0)])

        @pl.when(changed)
        def _():
            slot = wslot_ref[i]
            for cp in w_copies(be_ref[i], slot):
                cp.wait()

            @pl.when(wnext_ref[i] < n_exp)
            def _():
                for cp in w_copies(wnext_ref[i], 1 - slot):
                    cp.start()
            wg_bf[...] = wg32[slot].astype(BF16)
            wu_bf[...] = wu32[slot].astype(BF16)
            wd_bf[...] = wd32[slot].astype(BF16)

        x_copy(i).wait()
        x = xbuf[i % 2].astype(BF16)
        gate = jnp.dot(x, wg_bf[...], preferred_element_type=F32) + bg_ref[...]
        up = jnp.dot(x, wu_bf[...], preferred_element_type=F32) + bu_ref[...]
        gate = jnp.minimum(gate, SWIGLU_LIMIT)
        up = jnp.clip(up, -SWIGLU_LIMIT, SWIGLU_LIMIT)
        glu = gate * (1.0 / (1.0 + jnp.exp(-SWIGLU_ALPHA * gate)))
        hidden = ((up + 1.0) * glu).astype(BF16)
        y = jnp.dot(hidden, wd_bf[...], preferred_element_type=F32) + bd_ref[...]

        @pl.when(i >= 2)
        def _():
            y_copy(i - 2).wait()
        ybuf[i % 2] = y
        y_copy(i).start()

        @pl.when(i == n - 1)
        def _():
            y_copy(i).wait()

            @pl.when(i >= 1)
            def _():
                y_copy(i - 1).wait()

    @pl.when(i >= n)
    def _():
        ybuf[i % 2] = jnp.zeros((rows, ybuf.shape[2]), F32)
        y_copy(i).start()
        y_copy(i).wait()


def _experts(x_sorted, block_plan, n_blocks, w_gate, b_gate, w_up, b_up, w_down, b_down):
    n_rows, _, d = x_sorted.shape
    n_exp, _, f = w_gate.shape
    rows = EXPERT_ROWS
    bmap = lambda i, be, nu, ws, wn: (be[i], 0, 0)
    hbm = pl.BlockSpec(memory_space=pl.ANY)
    grid_spec = pltpu.PrefetchScalarGridSpec(
        num_scalar_prefetch=4,
        grid=(n_blocks,),
        in_specs=[
            hbm,
            hbm, pl.BlockSpec((None, 1, f), bmap),
            hbm, pl.BlockSpec((None, 1, f), bmap),
            hbm, pl.BlockSpec((None, 1, d), bmap),
        ],
        out_specs=hbm,
        scratch_shapes=[
            pltpu.VMEM((2, rows, d), F32),
            pltpu.VMEM((2, rows, d), F32),
            pltpu.SemaphoreType.DMA((2,)),
            pltpu.SemaphoreType.DMA((2,)),
            pltpu.VMEM((2, d, f), F32),
            pltpu.VMEM((2, d, f), F32),
            pltpu.VMEM((2, f, d), F32),
            pltpu.SemaphoreType.DMA((2,)),
            pltpu.VMEM((d, f), BF16),
            pltpu.VMEM((d, f), BF16),
            pltpu.VMEM((f, d), BF16),
        ],
    )
    return pl.pallas_call(
        functools.partial(_expert_kernel, rows=rows, n_exp=n_exp),
        grid_spec=grid_spec,
        out_shape=jax.ShapeDtypeStruct((n_rows, 1, d), F32),
        compiler_params=pltpu.CompilerParams(
            dimension_semantics=("arbitrary",), vmem_limit_bytes=52 * MIB),
        name="moe_experts",
    )(*block_plan, x_sorted,
      w_gate, b_gate.reshape(n_exp, 1, f), w_up, b_up.reshape(n_exp, 1, f),
      w_down, b_down.reshape(n_exp, 1, d))


def _combine_kernel(dest_hbm, y_hbm, x1_ref, gate_ref, lnf_ref, o_ref, idx_smem, ybuf,
                    idx_sem, row_sem, *, tile):
    i = pl.program_id(0)
    n = pl.num_programs(0)
    per_tile = TOP_K * tile

    def slot_base(b):
        return pl.multiple_of((b % 3) * per_tile, per_tile)

    def idx_copy(b):
        src = dest_hbm.at[pl.ds(pl.multiple_of(b * per_tile, per_tile), per_tile)]
        return pltpu.make_async_copy(src, idx_smem.at[pl.ds(slot_base(b), per_tile)],
                                     idx_sem.at[b % 3])

    def gather(b):
        base = slot_base(b)
        buf = ybuf.at[b % 2]
        sem = row_sem.at[b % 2]

        def issue(g, carry):
            for s in range(SUBLANES):
                pltpu.make_async_copy(y_hbm.at[idx_smem[base + g * SUBLANES + s]],
                                      buf.at[g, pl.ds(s, 1), :], sem).start(priority=s % 2)
            return carry

        lax.fori_loop(0, per_tile // SUBLANES, issue, 0)

    @pl.when(i == 0)
    def _():
        idx_copy(0).start()
        idx_copy(0).wait()
        gather(0)

        @pl.when(n > 1)
        def _():
            idx_copy(1).start()

    @pl.when(i + 1 < n)
    def _():
        idx_copy(i + 1).wait()

        @pl.when(i + 2 < n)
        def _():
            idx_copy(i + 2).start()
        gather(i + 1)

    cur = i % 2
    pltpu.make_async_copy(ybuf.at[1 - cur], ybuf.at[cur], row_sem.at[cur]).wait()
    gates = gate_ref[...]
    x = x1_ref[...]
    groups = tile // SUBLANES
    for k in range(TOP_K):
        yk = ybuf[cur, k * groups:(k + 1) * groups].reshape(tile, x.shape[1])
        x = x + gates[:, k:k + 1] * yk
    o_ref[...] = _rms(x, lnf_ref[...])


def _combine(dest, y_rows, x1, gates, ln_f_g):
    t, d = x1.shape
    tile = min(COMBINE_TILE, t)
    n_tiles = t // tile
    dest_tiles = dest.reshape(n_tiles, tile, TOP_K).transpose(0, 2, 1).reshape(n_tiles * TOP_K * tile)
    return pl.pallas_call(
        functools.partial(_combine_kernel, tile=tile),
        grid=(n_tiles,),
        in_specs=[
            pl.BlockSpec(memory_space=pl.ANY),
            pl.BlockSpec(memory_space=pl.ANY),
            pl.BlockSpec((tile, d), lambda i: (i, 0)),
            pl.BlockSpec((tile, TOP_K), lambda i: (i, 0)),
            pl.BlockSpec((1, d), lambda i: (0, 0)),
        ],
        out_specs=pl.BlockSpec((tile, d), lambda i: (i, 0)),
        out_shape=jax.ShapeDtypeStruct((t, d), F32),
        scratch_shapes=[
            pltpu.SMEM((3 * TOP_K * tile,), I32),
            pltpu.VMEM((2, TOP_K * tile // SUBLANES, SUBLANES, d), F32),
            pltpu.SemaphoreType.DMA((3,)),
            pltpu.SemaphoreType.DMA((2,)),
        ],
        compiler_params=pltpu.CompilerParams(
            dimension_semantics=("arbitrary",), vmem_limit_bytes=40 * MIB),
        name="moe_combine",
    )(dest_tiles, y_rows, x1, gates, ln_f_g.reshape(1, d))


def kernel(x, ln1_g, w_in, lam_re, lam_im, log_dt, ssm_b_re, ssm_b_im, ssm_c_re, ssm_c_im,
           ssm_d, w_glu, g_sb, g_ssm, w_out, ln2_g, w_router, b_router, w_gate, b_gate,
           w_up, b_up, w_down, b_down, ln_f_g):
    bsz, seq, d = x.shape
    assert ln1_g.shape[0] == 1, "depth-1 block only"
    ssm_w = ssm_d.shape[1]
    sb = g_sb.shape[1]
    x2 = x.reshape(bsz * seq, d)
    q, k, v, u = _in_proj(x2, ln1_g[0], w_in[0], sb, ssm_w)
    y_sb = _attention(q, k, v, bsz, seq)
    y_ss = _ssm(u, bsz, seq, lam_re[0], lam_im[0], log_dt[0], ssm_b_re[0], ssm_b_im[0],
                ssm_c_re[0], ssm_c_im[0])
    x1, xn, top_idx, gates, rank, counts = _post(
        x2, y_sb, y_ss, u, ssm_d[0], w_glu[0], g_sb[0], g_ssm[0], w_out[0], ln2_g[0],
        w_router[0], b_router[0])
    dest, block_plan, fill_start, n_blocks = _plan(top_idx, rank, counts, EXPERT_ROWS)
    x_sorted = _dispatch(xn, dest, fill_start, n_blocks * EXPERT_ROWS)
    y_rows = _experts(x_sorted, block_plan, n_blocks, w_gate[0], b_gate[0],
                      w_up[0], b_up[0], w_down[0], b_down[0])
    out = _combine(dest, y_rows, x1, gates, ln_f_g)
    return out.reshape(bsz, seq, d)
```

```python
import functools
import math

import jax
import jax.numpy as jnp
from jax import lax
from jax.experimental import pallas as pl
from jax.experimental.pallas import tpu as pltpu

F32 = jnp.float32
BF16 = jnp.bfloat16
I32 = jnp.int32

EPS = 1e-5
SB_HEAD_DIM = 64
SSM_GROUP = 16
SSM_STATE = 64
TOP_K = 4
SWIGLU_LIMIT = 7.0
SWIGLU_ALPHA = 1.702

LANES = 128
SUBLANES = 8
HEADS_PER_BLOCK = LANES // SB_HEAD_DIM
ATTN_BLOCK = 256
ATTN_EXP_FLOOR = 160.0
SSM_CHUNK = 64
TOKEN_TILE = 512
EXPERT_ROWS = 256
COMBINE_TILE = 256
DISPATCH_TILE = 1024
MIB = 1024 * 1024
LOG2E = 1.4426950408889634


def _rms(x, g):
    return x * lax.rsqrt(jnp.mean(x * x, axis=-1, keepdims=True) + EPS) * g


def _in_proj_kernel(x_ref, g_ref, w_ref, q_ref, k_ref, v_ref, u_ref, w_bf, *, sb, scale):
    @pl.when(pl.program_id(0) == 0)
    def _():
        w_bf[...] = w_ref[...].astype(BF16)

    h = _rms(x_ref[...], g_ref[...])
    proj = jnp.dot(h.astype(BF16), w_bf[...], preferred_element_type=F32)
    k_ref[...] = proj[:, sb:2 * sb].astype(BF16)
    u_ref[...] = proj[:, 3 * sb:].astype(BF16)
    lane = lax.broadcasted_iota(I32, (1, LANES), 1)
    q = (proj[:, :sb] * scale).astype(BF16)
    v = proj[:, 2 * sb:3 * sb].astype(BF16)
    zero = jnp.zeros((), BF16)
    for pair in range(sb // LANES):
        cols = slice(pair * LANES, (pair + 1) * LANES)
        for head in range(HEADS_PER_BLOCK):
            own = (lane // SB_HEAD_DIM) == head
            out = slice((pair * HEADS_PER_BLOCK + head) * LANES,
                        (pair * HEADS_PER_BLOCK + head + 1) * LANES)
            q_ref[:, out] = jnp.where(own, q[:, cols], zero)
            v_ref[:, out] = jnp.where(own, v[:, cols], zero)


def _in_proj(x2, ln1_g, w_in, sb, ssm_w):
    t, d = x2.shape
    n_in = w_in.shape[1]
    tm = min(TOKEN_TILE, t)
    scale = LOG2E / math.sqrt(SB_HEAD_DIM)
    return pl.pallas_call(
        functools.partial(_in_proj_kernel, sb=sb, scale=scale),
        grid=(t // tm,),
        in_specs=[
            pl.BlockSpec((tm, d), lambda i: (i, 0)),
            pl.BlockSpec((1, d), lambda i: (0, 0)),
            pl.BlockSpec((d, n_in), lambda i: (0, 0)),
        ],
        out_specs=[
            pl.BlockSpec((tm, HEADS_PER_BLOCK * sb), lambda i: (i, 0)),
            pl.BlockSpec((tm, sb), lambda i: (i, 0)),
            pl.BlockSpec((tm, HEADS_PER_BLOCK * sb), lambda i: (i, 0)),
            pl.BlockSpec((tm, ssm_w), lambda i: (i, 0)),
        ],
        out_shape=[
            jax.ShapeDtypeStruct((t, HEADS_PER_BLOCK * sb), BF16),
            jax.ShapeDtypeStruct((t, sb), BF16),
            jax.ShapeDtypeStruct((t, HEADS_PER_BLOCK * sb), BF16),
            jax.ShapeDtypeStruct((t, ssm_w), BF16),
        ],
        scratch_shapes=[pltpu.VMEM((d, n_in), BF16)],
        compiler_params=pltpu.CompilerParams(
            dimension_semantics=("arbitrary",), vmem_limit_bytes=48 * MIB),
        name="in_proj",
    )(x2, ln1_g.reshape(1, d), w_in)


def _attn_kernel(ti_ref, tj_ref, q_ref, k_ref, v_ref, tri_ref, mask_ref, o_ref,
                 dbuf, spbuf, lbuf, rsbuf, acc_ref, r_ref, *, blk, n_items):
    sign_bit = jnp.uint32(0x80000000)
    last = n_items - 1
    ALL, CAUSAL, NONE = 0, 1, 2

    for ref in (dbuf, spbuf, lbuf, rsbuf, acc_ref, r_ref):
        ref[...] = jnp.zeros_like(ref)

    def both_heads(ref, start):
        rows = pl.ds(pl.multiple_of(start, blk), blk)
        return jnp.concatenate([ref[rows, h * LANES:(h + 1) * LANES]
                                for h in range(HEADS_PER_BLOCK)], axis=0)

    def item(idx):
        c = jnp.clip(idx, 0, last)
        return ti_ref[c], tj_ref[c]

    def step(c1, c2, c3, slot):
        other = 1 - slot
        i1, j1 = item(c1)
        ks = k_ref[pl.ds(pl.multiple_of(j1 * blk, blk), blk), :]
        w = lax.dot_general(both_heads(q_ref, i1 * blk), ks, (((1,), (1,)), ((), ())),
                            preferred_element_type=F32)
        i3, j3 = item(c3)
        valid = c3 <= last
        first = jnp.logical_and(i3 == j3, valid)
        mask3 = mask_ref[jnp.where(valid, (i3 == j3).astype(I32), NONE)]
        r_prev = jnp.where(first, 0.0, r_ref[...])
        r_new = r_prev + rsbuf[other]
        r_ref[...] = r_new
        done = jnp.logical_and(valid, jnp.min(r_new) >= ATTN_EXP_FLOOR)
        next_block = lax.shift_right_logical((i3 + 1) * (i3 + 2), 1)
        c1_next = jnp.minimum(jnp.where(done, jnp.maximum(c1 + 1, next_block), c1 + 1), n_items)
        r_wide = jnp.concatenate([r_prev] * (blk // LANES), axis=1)
        a = jnp.exp2(dbuf[slot] - (lbuf[other] + r_wide))
        ab = a.astype(BF16) * jnp.concatenate([mask3, mask3], axis=0)
        a_cat = jnp.concatenate([ab[:blk], ab[blk:]], axis=1)
        acc = jnp.where(first, 0.0, acc_ref[...]) + jnp.dot(
            a_cat, both_heads(v_ref, j3 * blk), preferred_element_type=F32)
        acc_ref[...] = acc
        o_ref[pl.ds(pl.multiple_of(i3 * blk, blk), blk), :] = acc.astype(o_ref.dtype)
        sums = jnp.dot(spbuf[other], tri_ref[...], preferred_element_type=F32)
        lbuf[slot] = sums[:, :blk]
        rsbuf[slot] = sums[:, blk:]
        mask1 = mask_ref[(i1 == j1).astype(I32)]
        neg_abs = lax.bitcast_convert_type(
            lax.bitcast_convert_type(w, jnp.uint32) | sign_bit, F32)
        sp2 = jnp.maximum(w, 0.0) + jnp.log(1.0 + jnp.exp2(neg_abs)) * LOG2E
        dbuf[slot] = w - sp2
        spbuf[slot] = sp2.astype(BF16) * jnp.concatenate([mask1, mask1], axis=0)
        return c1_next, c1, c2

    def in_flight(state):
        c1, c2, c3 = state
        return jnp.logical_or(c1 <= last, jnp.logical_or(c2 <= last, c3 <= last))

    def four_steps(state):
        for slot in (0, 1, 0, 1):
            state = step(*state, slot)
        return state

    none = jnp.int32(n_items)
    lax.while_loop(in_flight, four_steps, (jnp.int32(0), none, none))


def _attention(q, k, v, bsz, seq):
    sb = k.shape[-1]
    blk = min(ATTN_BLOCK, seq)
    n_blk = seq // blk
    n_pairs = sb // LANES
    q3, k3, v3 = (a.reshape(bsz, seq, a.shape[-1]) for a in (q, k, v))
    per_head = HEADS_PER_BLOCK * LANES
    jj = lax.broadcasted_iota(I32, (blk, blk), 0)
    ss = lax.broadcasted_iota(I32, (blk, blk), 1)
    tri = jnp.concatenate([(jj > ss).astype(BF16), jnp.ones((blk, LANES), BF16)], axis=1)
    masks = jnp.stack([jnp.ones((blk, blk), BF16), (ss < jj).astype(BF16),
                       jnp.zeros((blk, blk), BF16)])
    items = [(i, j) for i in range(n_blk) for j in range(i, -1, -1)]
    item_i = jnp.asarray([i for i, _ in items], I32)
    item_j = jnp.asarray([j for _, j in items], I32)
    whole = lambda b, p, ti, tj: (b, 0, p)
    grid_spec = pltpu.PrefetchScalarGridSpec(
        num_scalar_prefetch=2,
        grid=(bsz, n_pairs),
        in_specs=[
            pl.BlockSpec((None, seq, per_head), whole),
            pl.BlockSpec((None, seq, LANES), whole),
            pl.BlockSpec((None, seq, per_head), whole),
            pl.BlockSpec((blk, blk + LANES), lambda b, p, ti, tj: (0, 0)),
            pl.BlockSpec((3, blk, blk), lambda b, p, ti, tj: (0, 0, 0)),
        ],
        out_specs=pl.BlockSpec((None, seq, LANES), whole),
        scratch_shapes=[
            pltpu.VMEM((2, HEADS_PER_BLOCK * blk, blk), F32),
            pltpu.VMEM((2, HEADS_PER_BLOCK * blk, blk), BF16),
            pltpu.VMEM((2, HEADS_PER_BLOCK * blk, blk), F32),
            pltpu.VMEM((2, HEADS_PER_BLOCK * blk, LANES), F32),
            pltpu.VMEM((blk, LANES), F32),
            pltpu.VMEM((HEADS_PER_BLOCK * blk, LANES), F32),
        ],
    )
    out = pl.pallas_call(
        functools.partial(_attn_kernel, blk=blk, n_items=len(items)),
        grid_spec=grid_spec,
        out_shape=jax.ShapeDtypeStruct((bsz, seq, sb), BF16),
        compiler_params=pltpu.CompilerParams(
            dimension_semantics=("arbitrary", "arbitrary"), vmem_limit_bytes=40 * MIB),
        name="sb_attention",
    )(item_i, item_j, q3, k3, v3, tri, masks)
    return out.reshape(bsz * seq, sb)


def _ssm_params(lam_re, lam_im, log_dt, b_re, b_im, c_re, c_im, chunk, n_chunks):
    lam = lax.complex(lam_re.astype(F32), lam_im.astype(F32))
    dt = jnp.exp(log_dt.astype(F32))[:, None]
    lam_dt = lam * dt
    lam_bar = jnp.exp(lam_dt)
    b_bar = ((lam_bar - 1.0) / lam)[:, :, None] * lax.complex(b_re.astype(F32), b_im.astype(F32))
    b_t = jnp.swapaxes(b_bar, 1, 2)
    c_mat = lax.complex(c_re.astype(F32), c_im.astype(F32))
    steps = jnp.arange(chunk + 1, dtype=F32)
    pw = jnp.exp(lam_dt[:, None, :] * steps[None, :, None])
    npw = jnp.exp(-lam_dt[:, None, :] * steps[None, :chunk, None])

    def halves(lo, hi):
        return jnp.concatenate([lo, hi], axis=-1)

    def b_side(z):
        return [halves(jnp.real(z), jnp.real(z)), halves(-jnp.imag(z), jnp.imag(z))]

    def c_side(z):
        return [halves(jnp.real(z), jnp.imag(z)), halves(jnp.imag(z), jnp.real(z))]

    time_terms = jnp.stack(
        b_side(npw)
        + b_side(pw[:, chunk - 1::-1])
        + c_side(pw[:, :chunk])
        + c_side(pw[:, 1:]), axis=1)
    chan_terms = jnp.stack(
        [halves(jnp.real(b_t), jnp.imag(b_t)), halves(jnp.imag(b_t), jnp.real(b_t)),
         halves(jnp.real(c_mat), -jnp.real(c_mat)), halves(-jnp.imag(c_mat), -jnp.imag(c_mat))],
        axis=1)
    n_steps = max(1, (n_chunks - 1).bit_length())
    powers = []
    cur = pw[:, chunk, :]
    for _ in range(n_steps):
        powers.append(cur)
        cur = cur * cur
    lam_pow = jnp.stack(powers, axis=1)
    a1 = halves(jnp.real(lam_pow), jnp.real(lam_pow))
    a2 = halves(-jnp.imag(lam_pow), jnp.imag(lam_pow))
    return time_terms, chan_terms, a1, a2


def _ssm_kernel(u_ref, time_ref, chan_ref, a1_ref, a2_ref, y_ref, toep_ref,
                *, n_chunks, n_steps, group):
    lc = toep_ref.shape[0]
    p2 = a1_ref.shape[1]
    group_shift = group.bit_length() - 1
    nt = (((1,), (1,)), ((), ()))

    def table(t1, t2, v1, v2):
        full = (time_ref[t1][:, None, :] * chan_ref[v1][None, :, :]
                + time_ref[t2][:, None, :] * chan_ref[v2][None, :, :])
        return full.reshape(lc, p2).astype(BF16)

    src = table(0, 1, 0, 1)
    to_state = table(2, 3, 0, 1)
    dst = table(4, 5, 2, 3)
    from_state = table(6, 7, 2, 3)
    cb = min(256, lc)
    for j in range(lc // cb):
        blk = lax.dot_general(src, dst[j * cb:(j + 1) * cb], nt, preferred_element_type=F32)
        s_idx = lax.broadcasted_iota(I32, (lc, cb), 0) >> group_shift
        t_idx = (lax.broadcasted_iota(I32, (lc, cb), 1) + j * cb) >> group_shift
        toep_ref[:, j * cb:(j + 1) * cb] = jnp.where(s_idx <= t_idx, blk, 0.0).astype(BF16)
    u = u_ref[...]
    y = jnp.dot(u, toep_ref[...], preferred_element_type=F32)
    z = jnp.dot(u, to_state, preferred_element_type=F32)
    n = lax.broadcasted_iota(I32, z.shape, 0) & (n_chunks - 1)
    x = jnp.where(n >= 1, pltpu.roll(z, 1, 0), 0.0)
    for k in range(n_steps):
        sh = 1 << k
        xs = jnp.where(n >= sh, pltpu.roll(x, sh, 0), 0.0)
        x = x + a1_ref[k:k + 1, :] * xs + a2_ref[k:k + 1, :] * pltpu.roll(xs, p2 // 2, 1)
    y = y + lax.dot_general(x.astype(BF16), from_state, nt, preferred_element_type=F32)
    y_ref[...] = y.astype(y_ref.dtype)


def _ssm(u, bsz, seq, lam_re, lam_im, log_dt, b_re, b_im, c_re, c_im):
    g, p = lam_re.shape
    c = b_re.shape[-1]
    chunk = min(SSM_CHUNK, seq)
    n_chunks = seq // chunk
    assert n_chunks & (n_chunks - 1) == 0 and c & (c - 1) == 0, (n_chunks, c)
    rows = bsz * n_chunks
    lc = chunk * c
    time_terms, chan_terms, a1, a2 = _ssm_params(
        lam_re, lam_im, log_dt, b_re, b_im, c_re, c_im, chunk, n_chunks)
    n_steps = a1.shape[1]
    u_g = u.reshape(bsz, n_chunks, chunk, g, c).transpose(3, 0, 1, 2, 4).reshape(g, rows, lc)
    per_group = lambda i: (i, 0, 0)
    y_g = pl.pallas_call(
        functools.partial(_ssm_kernel, n_chunks=n_chunks, n_steps=n_steps, group=c),
        grid=(g,),
        in_specs=[
            pl.BlockSpec((None, rows, lc), per_group),
            pl.BlockSpec((None,) + time_terms.shape[1:], lambda i: (i, 0, 0, 0)),
            pl.BlockSpec((None,) + chan_terms.shape[1:], lambda i: (i, 0, 0, 0)),
            pl.BlockSpec((None, n_steps, 2 * p), per_group),
            pl.BlockSpec((None, n_steps, 2 * p), per_group),
        ],
        out_specs=pl.BlockSpec((None, rows, lc), per_group),
        out_shape=jax.ShapeDtypeStruct((g, rows, lc), BF16),
        scratch_shapes=[pltpu.VMEM((lc, lc), BF16)],
        compiler_params=pltpu.CompilerParams(
            dimension_semantics=("arbitrary",), vmem_limit_bytes=40 * MIB),
        name="s5_chunked_scan",
    )(u_g, time_terms, chan_terms, a1, a2)
    return y_g.reshape(g, bsz, n_chunks, chunk, c).transpose(1, 2, 3, 0, 4).reshape(bsz * seq, g * c)


def _post_kernel(x_ref, ysb_ref, yss_ref, u_ref, d_ref, wglu_ref, gsb_ref, gssm_ref, wout_ref,
                 ln2_ref, wr_ref, br_ref, tri_ref, x1_ref, xn_ref, idx_ref, gate_ref, rank_ref,
                 count_ref, running_ref, wglu_bf, wout_bf, *, sb, n_exp):
    @pl.when(pl.program_id(0) == 0)
    def _():
        running_ref[...] = jnp.zeros_like(running_ref)
        wglu_bf[...] = wglu_ref[...].astype(BF16)
        wout_bf[...] = wout_ref[...].astype(BF16)

    u = u_ref[...].astype(F32)
    y = yss_ref[...].astype(F32) + d_ref[...] * u
    y = y * (0.5 * (1.0 + jnp.tanh(math.sqrt(2.0 / math.pi) * (y + 0.044715 * (y * y * y)))))
    ab = jnp.dot(y.astype(BF16), wglu_bf[...], preferred_element_type=F32)
    w = ab.shape[1] // 2
    y_ssm = ab[:, :w] * (1.0 / (1.0 + jnp.exp(-ab[:, w:])))
    m_sb = _rms(ysb_ref[...].astype(F32), gsb_ref[...])
    m_ssm = _rms(y_ssm, gssm_ref[...])
    x1 = (x_ref[...]
          + jnp.dot(m_sb.astype(BF16), wout_bf[:sb, :], preferred_element_type=F32)
          + jnp.dot(m_ssm.astype(BF16), wout_bf[sb:, :], preferred_element_type=F32))
    x1_ref[...] = x1
    xn = _rms(x1, ln2_ref[...])
    xn_ref[...] = xn
    def split(a):
        hi = a.astype(BF16)
        return hi, (a - hi.astype(F32)).astype(BF16)

    xn_hi, xn_lo = split(xn)
    wr_hi, wr_lo = split(wr_ref[...])
    logits = (jnp.dot(xn_hi, wr_hi, preferred_element_type=F32)
              + jnp.dot(xn_hi, wr_lo, preferred_element_type=F32)
              + jnp.dot(xn_lo, wr_hi, preferred_element_type=F32)) + br_ref[...]
    lane = lax.broadcasted_iota(I32, logits.shape, 1).astype(F32)
    out_lane = lax.broadcasted_iota(I32, (logits.shape[0], LANES), 1)
    idx_out = jnp.zeros((logits.shape[0], LANES), F32)
    val_out = jnp.zeros((logits.shape[0], LANES), F32)
    top = None
    denom = None
    work = logits
    chosen = []
    for k in range(TOP_K):
        m = jnp.max(work, axis=-1, keepdims=True)
        sel = jnp.min(jnp.where(work == m, lane, float(n_exp)), axis=-1, keepdims=True)
        hit = lane == sel
        chosen.append(hit)
        work = jnp.where(hit, -jnp.inf, work)
        if k == 0:
            top = m
        e = jnp.exp(m - top)
        denom = e if denom is None else denom + e
        idx_out = jnp.where(out_lane == k, sel, idx_out)
        val_out = jnp.where(out_lane == k, e, val_out)
    idx_ref[...] = idx_out[:, :TOP_K].astype(I32)
    gate_ref[...] = (val_out / denom)[:, :TOP_K]
    member = functools.reduce(jnp.logical_or, chosen).astype(BF16)
    before = running_ref[...] + jnp.dot(tri_ref[...], member, preferred_element_type=F32)
    rank_out = jnp.zeros((logits.shape[0], LANES), F32)
    for k in range(TOP_K):
        rk = jnp.sum(jnp.where(chosen[k], before, 0.0), axis=-1, keepdims=True)
        rank_out = jnp.where(out_lane == k, rk, rank_out)
    rank_ref[...] = rank_out[:, :TOP_K].astype(I32)
    running = running_ref[...] + jnp.sum(member.astype(F32), axis=0, keepdims=True)
    running_ref[...] = running
    count_ref[...] = running


def _post(x2, y_sb, y_ss, u, ssm_d, w_glu, g_sb, g_ssm, w_out, ln2_g, w_router, b_router):
    t, d = x2.shape
    sb = y_sb.shape[1]
    w = y_ss.shape[1]
    n_exp = w_router.shape[1]
    tm = min(TOKEN_TILE, t)
    row = lambda i: (i, 0)
    fixed = lambda i: (0, 0)
    earlier = (lax.broadcasted_iota(I32, (tm, tm), 1)
               < lax.broadcasted_iota(I32, (tm, tm), 0)).astype(BF16)
    return pl.pallas_call(
        functools.partial(_post_kernel, sb=sb, n_exp=n_exp),
        grid=(t // tm,),
        in_specs=[
            pl.BlockSpec((tm, d), row),
            pl.BlockSpec((tm, sb), row),
            pl.BlockSpec((tm, w), row),
            pl.BlockSpec((tm, w), row),
            pl.BlockSpec((1, w), fixed),
            pl.BlockSpec((w, 2 * w), fixed),
            pl.BlockSpec((1, sb), fixed),
            pl.BlockSpec((1, w), fixed),
            pl.BlockSpec((sb + w, d), fixed),
            pl.BlockSpec((1, d), fixed),
            pl.BlockSpec((d, n_exp), fixed),
            pl.BlockSpec((1, n_exp), fixed),
            pl.BlockSpec((tm, tm), fixed),
        ],
        out_specs=[
            pl.BlockSpec((tm, d), row),
            pl.BlockSpec((tm, d), row),
            pl.BlockSpec((tm, TOP_K), row),
            pl.BlockSpec((tm, TOP_K), row),
            pl.BlockSpec((tm, TOP_K), row),
            pl.BlockSpec((1, n_exp), fixed),
        ],
        out_shape=[
            jax.ShapeDtypeStruct((t, d), F32),
            jax.ShapeDtypeStruct((t, d), F32),
            jax.ShapeDtypeStruct((t, TOP_K), I32),
            jax.ShapeDtypeStruct((t, TOP_K), F32),
            jax.ShapeDtypeStruct((t, TOP_K), I32),
            jax.ShapeDtypeStruct((1, n_exp), F32),
        ],
        scratch_shapes=[pltpu.VMEM((1, n_exp), F32), pltpu.VMEM((w, 2 * w), BF16),
                        pltpu.VMEM((sb + w, d), BF16)],
        compiler_params=pltpu.CompilerParams(
            dimension_semantics=("arbitrary",), vmem_limit_bytes=48 * MIB),
        name="post_mixer_router",
    )(x2, y_sb, y_ss, u, ssm_d.reshape(1, w), w_glu, g_sb.reshape(1, sb),
      g_ssm.reshape(1, w), w_out, ln2_g.reshape(1, d), w_router,
      b_router.reshape(1, n_exp), earlier)


def _plan(top_idx, rank, counts, rows_per_block):
    t, k = top_idx.shape
    n_exp = counts.shape[-1]
    counts = counts.reshape(n_exp).astype(I32)
    padded = ((counts + rows_per_block - 1) // rows_per_block) * rows_per_block
    pad_ends = jnp.cumsum(padded)
    pad_starts = pad_ends - padded
    start_of = jnp.sum(jnp.where(top_idx[..., None] == jnp.arange(n_exp, dtype=I32),
                                 pad_starts, 0), axis=-1)
    dest = (start_of + rank).astype(I32)
    n_rows = t * k + n_exp * rows_per_block
    n_blocks = n_rows // rows_per_block
    block_start = jnp.arange(n_blocks, dtype=I32) * rows_per_block
    block_expert = jnp.minimum(
        jnp.sum((pad_ends[None, :] <= block_start[:, None]).astype(I32), axis=1), n_exp - 1)
    n_used = (pad_ends[-1] // rows_per_block).astype(I32).reshape(1)
    fill_start = jnp.concatenate([jnp.maximum(pad_ends - rows_per_block, 0).astype(I32), n_used])
    experts = jnp.arange(n_exp, dtype=I32)
    nonempty = counts > 0
    order = jnp.cumsum(nonempty.astype(I32)) - nonempty.astype(I32)
    later_ne = jnp.where((experts[None, :] > experts[:, None]) & nonempty[None, :],
                         experts[None, :], n_exp)
    next_tbl = jnp.min(later_ne, axis=1)
    onehot = block_expert[:, None] == experts[None, :]
    w_slot = jnp.sum(jnp.where(onehot, order & 1, 0), axis=1).astype(I32)
    w_next = jnp.sum(jnp.where(onehot, next_tbl, 0), axis=1).astype(I32)
    return dest, (block_expert, n_used, w_slot, w_next), fill_start, n_blocks


def _dispatch_kernel(fill_ref, dest_hbm, xn_ref, xs_hbm, idx_smem, zeros_ref, idx_sem, row_sem,
                     fill_sem, *, tile, n_exp, fill_rows):
    i = pl.program_id(0)
    n = pl.num_programs(0)
    per_tile = TOP_K * tile

    def idx_copy(b):
        src = dest_hbm.at[pl.ds(pl.multiple_of(b * per_tile, per_tile), per_tile)]
        dst = idx_smem.at[pl.ds(pl.multiple_of((b % 2) * per_tile, per_tile), per_tile)]
        return pltpu.make_async_copy(src, dst, idx_sem.at[b % 2])

    @pl.when(i == 0)
    def _():
        idx_copy(0).start()
        zeros_ref[...] = jnp.zeros_like(zeros_ref)
        for e in range(n_exp):
            start = pl.multiple_of(fill_ref[e], 8)
            pltpu.make_async_copy(zeros_ref, xs_hbm.at[pl.ds(start, fill_rows), 0, :],
                                  fill_sem).start()
        for e in range(n_exp):
            pltpu.make_async_copy(zeros_ref, xs_hbm.at[pl.ds(0, fill_rows), 0, :],
                                  fill_sem).wait()
        n_blocks = xs_hbm.shape[0] // fill_rows

        def fill_unused(b, carry):
            cp = pltpu.make_async_copy(
                zeros_ref, xs_hbm.at[pl.ds(pl.multiple_of(b * fill_rows, fill_rows), fill_rows), 0, :],
                fill_sem)
            cp.start()
            cp.wait()
            return carry

        lax.fori_loop(fill_ref[n_exp], n_blocks, fill_unused, 0)

    idx_copy(i).wait()

    @pl.when(i + 1 < n)
    def _():
        idx_copy(i + 1).start()

    base = pl.multiple_of((i % 2) * per_tile, per_tile)
    quarter = tile // TOP_K
    for part in range(TOP_K):
        def issue(g, carry, part=part):
            group = part * (quarter // SUBLANES) + g
            for s in range(SUBLANES):
                for k in range(TOP_K):
                    dst_row = idx_smem[base + (group * SUBLANES + s) * TOP_K + k]
                    pltpu.make_async_copy(xn_ref.at[group, pl.ds(s, 1), :], xs_hbm.at[dst_row],
                                          row_sem.at[part]).start(priority=k % 2)
            return carry

        lax.fori_loop(0, quarter // SUBLANES, issue, 0)
    for part in range(TOP_K):
        pltpu.make_async_copy(xn_ref, xn_ref, row_sem.at[part]).wait()


def _dispatch(xn, dest, fill_start, n_rows):
    t, d = xn.shape
    tile = min(DISPATCH_TILE, t)
    n_exp = fill_start.shape[0] - 1
    grid_spec = pltpu.PrefetchScalarGridSpec(
        num_scalar_prefetch=1,
        grid=(t // tile,),
        in_specs=[
            pl.BlockSpec(memory_space=pl.ANY),
            pl.BlockSpec((tile // SUBLANES, SUBLANES, d), lambda i, fs: (i, 0, 0)),
        ],
        out_specs=pl.BlockSpec(memory_space=pl.ANY),
        scratch_shapes=[
            pltpu.SMEM((2 * TOP_K * tile,), I32),
            pltpu.VMEM((EXPERT_ROWS, d), F32),
            pltpu.SemaphoreType.DMA((2,)),
            pltpu.SemaphoreType.DMA((TOP_K,)),
            pltpu.SemaphoreType.DMA,
        ],
    )
    return pl.pallas_call(
        functools.partial(_dispatch_kernel, tile=tile, n_exp=n_exp, fill_rows=EXPERT_ROWS),
        grid_spec=grid_spec,
        out_shape=jax.ShapeDtypeStruct((n_rows, 1, d), F32),
        compiler_params=pltpu.CompilerParams(
            dimension_semantics=("arbitrary",), vmem_limit_bytes=40 * MIB),
        name="moe_dispatch",
    )(fill_start, dest.reshape(t * TOP_K), xn.reshape(t // SUBLANES, SUBLANES, d))


def _expert_kernel(be_ref, nused_ref, wslot_ref, wnext_ref, xs_hbm, wg_hbm, bg_ref, wu_hbm, bu_ref,
                   wd_hbm, bd_ref, y_hbm, xbuf, ybuf, x_sem, y_sem, wg32, wu32, wd32, w_sem,
                   wg_bf, wu_bf, wd_bf, *, rows, n_exp):
    i = pl.program_id(0)
    n = nused_ref[0]

    def w_copies(e, slot):
        return [pltpu.make_async_copy(src.at[e], dst.at[slot], w_sem.at[slot])
                for src, dst in ((wg_hbm, wg32), (wu_hbm, wu32), (wd_hbm, wd32))]

    def x_copy(b):
        return pltpu.make_async_copy(
            xs_hbm.at[pl.ds(pl.multiple_of(b * rows, rows), rows), 0, :], xbuf.at[b % 2],
            x_sem.at[b % 2])

    def y_copy(b):
        return pltpu.make_async_copy(
            ybuf.at[b % 2], y_hbm.at[pl.ds(pl.multiple_of(b * rows, rows), rows), 0, :],
            y_sem.at[b % 2])

    @pl.when(i < n)
    def _():
        @pl.when(i == 0)
        def _():
            x_copy(0).start()
            for cp in w_copies(be_ref[0], wslot_ref[0]):
                cp.start()

        @pl.when(i + 1 < n)
        def _():
            x_copy(i + 1).start()

        changed = jnp.logical_or(i == 0, be_ref[i] != be_ref[jnp.maximum(i - 1, 0)])

        @pl.when(changed)
        def _():
            slot = wslot_ref[i]
            for cp in w_copies(be_ref[i], slot):
                cp.wait()

            @pl.when(wnext_ref[i] < n_exp)
            def _():
                for cp in w_copies(wnext_ref[i], 1 - slot):
                    cp.start()
            wg_bf[...] = wg32[slot].astype(BF16)
            wu_bf[...] = wu32[slot].astype(BF16)
            wd_bf[...] = wd32[slot].astype(BF16)

        x_copy(i).wait()
        x = xbuf[i % 2].astype(BF16)
        gate = jnp.dot(x, wg_bf[...], preferred_element_type=F32) + bg_ref[...]
        up = jnp.dot(x, wu_bf[...], preferred_element_type=F32) + bu_ref[...]
        gate = jnp.minimum(gate, SWIGLU_LIMIT)
        up = jnp.clip(up, -SWIGLU_LIMIT, SWIGLU_LIMIT)
        glu = gate * (1.0 / (1.0 + jnp.exp(-SWIGLU_ALPHA * gate)))
        hidden = ((up + 1.0) * glu).astype(BF16)
        y = jnp.dot(hidden, wd_bf[...], preferred_element_type=F32) + bd_ref[...]

        @pl.when(i >= 2)
        def _():
            y_copy(i - 2).wait()
        ybuf[i % 2] = y
        y_copy(i).start()

        @pl.when(i == n - 1)
        def _():
            y_copy(i).wait()

            @pl.when(i >= 1)
            def _():
                y_copy(i - 1).wait()

    @pl.when(i >= n)
    def _():
        ybuf[i % 2] = jnp.zeros((rows, ybuf.shape[2]), F32)
        y_copy(i).start()
        y_copy(i).wait()


def _experts(x_sorted, block_plan, n_blocks, w_gate, b_gate, w_up, b_up, w_down, b_down):
    n_rows, _, d = x_sorted.shape
    n_exp, _, f = w_gate.shape
    rows = EXPERT_ROWS
    bmap = lambda i, be, nu, ws, wn: (be[i], 0, 0)
    hbm = pl.BlockSpec(memory_space=pl.ANY)
    grid_spec = pltpu.PrefetchScalarGridSpec(
        num_scalar_prefetch=4,
        grid=(n_blocks,),
        in_specs=[
            hbm,
            hbm, pl.BlockSpec((None, 1, f), bmap),
            hbm, pl.BlockSpec((None, 1, f), bmap),
            hbm, pl.BlockSpec((None, 1, d), bmap),
        ],
        out_specs=hbm,
        scratch_shapes=[
            pltpu.VMEM((2, rows, d), F32),
            pltpu.VMEM((2, rows, d), F32),
            pltpu.SemaphoreType.DMA((2,)),
            pltpu.SemaphoreType.DMA((2,)),
            pltpu.VMEM((2, d, f), F32),
            pltpu.VMEM((2, d, f), F32),
            pltpu.VMEM((2, f, d), F32),
            pltpu.SemaphoreType.DMA((2,)),
            pltpu.VMEM((d, f), BF16),
            pltpu.VMEM((d, f), BF16),
            pltpu.VMEM((f, d), BF16),
        ],
    )
    return pl.pallas_call(
        functools.partial(_expert_kernel, rows=rows, n_exp=n_exp),
        grid_spec=grid_spec,
        out_shape=jax.ShapeDtypeStruct((n_rows, 1, d), F32),
        compiler_params=pltpu.CompilerParams(
            dimension_semantics=("arbitrary",), vmem_limit_bytes=52 * MIB),
        name="moe_experts",
    )(*block_plan, x_sorted,
      w_gate, b_gate.reshape(n_exp, 1, f), w_up, b_up.reshape(n_exp, 1, f),
      w_down, b_down.reshape(n_exp, 1, d))


def _combine_kernel(dest_hbm, y_hbm, x1_ref, gate_ref, lnf_ref, o_ref, idx_smem, ybuf,
                    idx_sem, row_sem, *, tile):
    i = pl.program_id(0)
    n = pl.num_programs(0)
    per_tile = TOP_K * tile

    def slot_base(b):
        return pl.multiple_of((b % 3) * per_tile, per_tile)

    def idx_copy(b):
        src = dest_hbm.at[pl.ds(pl.multiple_of(b * per_tile, per_tile), per_tile)]
        return pltpu.make_async_copy(src, idx_smem.at[pl.ds(slot_base(b), per_tile)],
                                     idx_sem.at[b % 3])

    def gather(b):
        base = slot_base(b)
        buf = ybuf.at[b % 2]
        sem = row_sem.at[b % 2]

        def issue(g, carry):
            for s in range(SUBLANES):
                pltpu.make_async_copy(y_hbm.at[idx_smem[base + g * SUBLANES + s]],
                                      buf.at[g, pl.ds(s, 1), :], sem).start(priority=s % 2)
            return carry

        lax.fori_loop(0, per_tile // SUBLANES, issue, 0)

    @pl.when(i == 0)
    def _():
        idx_copy(0).start()
        idx_copy(0).wait()
        gather(0)

        @pl.when(n > 1)
        def _():
            idx_copy(1).start()

    @pl.when(i + 1 < n)
    def _():
        idx_copy(i + 1).wait()

        @pl.when(i + 2 < n)
        def _():
            idx_copy(i + 2).start()
        gather(i + 1)

    cur = i % 2
    pltpu.make_async_copy(ybuf.at[1 - cur], ybuf.at[cur], row_sem.at[cur]).wait()
    gates = gate_ref[...]
    x = x1_ref[...]
    groups = tile // SUBLANES
    for k in range(TOP_K):
        yk = ybuf[cur, k * groups:(k + 1) * groups].reshape(tile, x.shape[1])
        x = x + gates[:, k:k + 1] * yk
    o_ref[...] = _rms(x, lnf_ref[...])


def _combine(dest, y_rows, x1, gates, ln_f_g):
    t, d = x1.shape
    tile = min(COMBINE_TILE, t)
    n_tiles = t // tile
    dest_tiles = dest.reshape(n_tiles, tile, TOP_K).transpose(0, 2, 1).reshape(n_tiles * TOP_K * tile)
    return pl.pallas_call(
        functools.partial(_combine_kernel, tile=tile),
        grid=(n_tiles,),
        in_specs=[
            pl.BlockSpec(memory_space=pl.ANY),
            pl.BlockSpec(memory_space=pl.ANY),
            pl.BlockSpec((tile, d), lambda i: (i, 0)),
            pl.BlockSpec((tile, TOP_K), lambda i: (i, 0)),
            pl.BlockSpec((1, d), lambda i: (0, 0)),
        ],
        out_specs=pl.BlockSpec((tile, d), lambda i: (i, 0)),
        out_shape=jax.ShapeDtypeStruct((t, d), F32),
        scratch_shapes=[
            pltpu.SMEM((3 * TOP_K * tile,), I32),
            pltpu.VMEM((2, TOP_K * tile // SUBLANES, SUBLANES, d), F32),
            pltpu.SemaphoreType.DMA((3,)),
            pltpu.SemaphoreType.DMA((2,)),
        ],
        compiler_params=pltpu.CompilerParams(
            dimension_semantics=("arbitrary",), vmem_limit_bytes=40 * MIB),
        name="moe_combine",
    )(dest_tiles, y_rows, x1, gates, ln_f_g.reshape(1, d))


def kernel(x, ln1_g, w_in, lam_re, lam_im, log_dt, ssm_b_re, ssm_b_im, ssm_c_re, ssm_c_im,
           ssm_d, w_glu, g_sb, g_ssm, w_out, ln2_g, w_router, b_router, w_gate, b_gate,
           w_up, b_up, w_down, b_down, ln_f_g):
    bsz, seq, d = x.shape
    assert ln1_g.shape[0] == 1, "depth-1 block only"
    ssm_w = ssm_d.shape[1]
    sb = g_sb.shape[1]
    x2 = x.reshape(bsz * seq, d)
    q, k, v, u = _in_proj(x2, ln1_g[0], w_in[0], sb, ssm_w)
    y_sb = _attention(q, k, v, bsz, seq)
    y_ss = _ssm(u, bsz, seq, lam_re[0], lam_im[0], log_dt[0], ssm_b_re[0], ssm_b_im[0],
                ssm_c_re[0], ssm_c_im[0])
    x1, xn, top_idx, gates, rank, counts = _post(
        x2, y_sb, y_ss, u, ssm_d[0], w_glu[0], g_sb[0], g_ssm[0], w_out[0], ln2_g[0],
        w_router[0], b_router[0])
    dest, block_plan, fill_start, n_blocks = _plan(top_idx, rank, counts, EXPERT_ROWS)
    x_sorted = _dispatch(xn, dest, fill_start, n_blocks * EXPERT_ROWS)
    y_rows = _experts(x_sorted, block_plan, n_blocks, w_gate[0], b_gate[0],
                      w_up[0], b_up[0], w_down[0], b_down[0])
    out = _combine(dest, y_rows, x1, gates, ln_f_g)
    return out.reshape(bsz, seq, d)
```

```python
import functools
import math

import jax
import jax.numpy as jnp
from jax import lax
from jax.experimental import pallas as pl
from jax.experimental.pallas import tpu as pltpu

F32 = jnp.float32
BF16 = jnp.bfloat16
I32 = jnp.int32

EPS = 1e-5
SB_HEAD_DIM = 64
SSM_GROUP = 16
SSM_STATE = 64
TOP_K = 4
SWIGLU_LIMIT = 7.0
SWIGLU_ALPHA = 1.702

LANES = 128
SUBLANES = 8
HEADS_PER_BLOCK = LANES // SB_HEAD_DIM
ATTN_BLOCK = 256
ATTN_EXP_FLOOR = 160.0
SSM_CHUNK = 64
SSM_FOLD = 8
TOKEN_TILE = 512
EXPERT_ROWS = 256
COMBINE_TILE = 256
DISPATCH_TILE = 1024
MIB = 1024 * 1024
LOG2E = 1.4426950408889634


def _rms(x, g):
    return x * lax.rsqrt(jnp.mean(x * x, axis=-1, keepdims=True) + EPS) * g


def _in_proj_kernel(x_ref, g_ref, w_ref, q_ref, k_ref, v_ref, u_ref, w_bf, *, sb, scale):
    @pl.when(pl.program_id(0) == 0)
    def _():
        w_bf[...] = w_ref[...].astype(BF16)

    h = _rms(x_ref[...], g_ref[...])
    proj = jnp.dot(h.astype(BF16), w_bf[...], preferred_element_type=F32)
    k_ref[...] = proj[:, sb:2 * sb].astype(BF16)
    u_ref[...] = proj[:, 3 * sb:]
    lane = lax.broadcasted_iota(I32, (1, LANES), 1)
    q = (proj[:, :sb] * scale).astype(BF16)
    v = proj[:, 2 * sb:3 * sb].astype(BF16)
    zero = jnp.zeros((), BF16)
    for pair in range(sb // LANES):
        cols = slice(pair * LANES, (pair + 1) * LANES)
        for head in range(HEADS_PER_BLOCK):
            own = (lane // SB_HEAD_DIM) == head
            out = slice((pair * HEADS_PER_BLOCK + head) * LANES,
                        (pair * HEADS_PER_BLOCK + head + 1) * LANES)
            q_ref[:, out] = jnp.where(own, q[:, cols], zero)
            v_ref[:, out] = jnp.where(own, v[:, cols], zero)


def _in_proj(x2, ln1_g, w_in, sb, ssm_w):
    t, d = x2.shape
    n_in = w_in.shape[1]
    tm = min(TOKEN_TILE, t)
    scale = LOG2E / math.sqrt(SB_HEAD_DIM)
    return pl.pallas_call(
        functools.partial(_in_proj_kernel, sb=sb, scale=scale),
        grid=(t // tm,),
        in_specs=[
            pl.BlockSpec((tm, d), lambda i: (i, 0)),
            pl.BlockSpec((1, d), lambda i: (0, 0)),
            pl.BlockSpec((d, n_in), lambda i: (0, 0)),
        ],
        out_specs=[
            pl.BlockSpec((tm, HEADS_PER_BLOCK * sb), lambda i: (i, 0)),
            pl.BlockSpec((tm, sb), lambda i: (i, 0)),
            pl.BlockSpec((tm, HEADS_PER_BLOCK * sb), lambda i: (i, 0)),
            pl.BlockSpec((tm, ssm_w), lambda i: (i, 0)),
        ],
        out_shape=[
            jax.ShapeDtypeStruct((t, HEADS_PER_BLOCK * sb), BF16),
            jax.ShapeDtypeStruct((t, sb), BF16),
            jax.ShapeDtypeStruct((t, HEADS_PER_BLOCK * sb), BF16),
            jax.ShapeDtypeStruct((t, ssm_w), F32),
        ],
        scratch_shapes=[pltpu.VMEM((d, n_in), BF16)],
        compiler_params=pltpu.CompilerParams(
            dimension_semantics=("arbitrary",), vmem_limit_bytes=48 * MIB),
        name="in_proj",
    )(x2, ln1_g.reshape(1, d), w_in)


def _attn_kernel(ti_ref, tj_ref, q_ref, k_ref, v_ref, tri_ref, mask_ref, o_ref,
                 dbuf, spbuf, lbuf, rsbuf, acc_ref, r_ref, *, blk, n_items):
    sign_bit = jnp.uint32(0x80000000)
    last = n_items - 1
    ALL, CAUSAL, NONE = 0, 1, 2

    for ref in (dbuf, spbuf, lbuf, rsbuf, acc_ref, r_ref):
        ref[...] = jnp.zeros_like(ref)

    def both_heads(ref, start):
        rows = pl.ds(pl.multiple_of(start, blk), blk)
        return jnp.concatenate([ref[rows, h * LANES:(h + 1) * LANES]
                                for h in range(HEADS_PER_BLOCK)], axis=0)

    def item(idx):
        c = jnp.clip(idx, 0, last)
        return ti_ref[c], tj_ref[c]

    def step(c1, c2, c3, slot):
        other = 1 - slot
        i1, j1 = item(c1)
        ks = k_ref[pl.ds(pl.multiple_of(j1 * blk, blk), blk), :]
        w = lax.dot_general(both_heads(q_ref, i1 * blk), ks, (((1,), (1,)), ((), ())),
                            preferred_element_type=F32)
        i3, j3 = item(c3)
        valid = c3 <= last
        first = jnp.logical_and(i3 == j3, valid)
        mask3 = mask_ref[jnp.where(valid, (i3 == j3).astype(I32), NONE)]
        r_prev = jnp.where(first, 0.0, r_ref[...])
        r_new = r_prev + rsbuf[other]
        r_ref[...] = r_new
        done = jnp.logical_and(valid, jnp.min(r_new) >= ATTN_EXP_FLOOR)
        next_block = lax.shift_right_logical((i3 + 1) * (i3 + 2), 1)
        c1_next = jnp.minimum(jnp.where(done, jnp.maximum(c1 + 1, next_block), c1 + 1), n_items)
        r_wide = jnp.concatenate([r_prev] * (blk // LANES), axis=1)
        a = jnp.exp2(dbuf[slot] - (lbuf[other] + r_wide))
        ab = a.astype(BF16) * jnp.concatenate([mask3, mask3], axis=0)
        a_cat = jnp.concatenate([ab[:blk], ab[blk:]], axis=1)
        acc = jnp.where(first, 0.0, acc_ref[...]) + jnp.dot(
            a_cat, both_heads(v_ref, j3 * blk), preferred_element_type=F32)
        acc_ref[...] = acc
        o_ref[pl.ds(pl.multiple_of(i3 * blk, blk), blk), :] = acc.astype(o_ref.dtype)
        sums = jnp.dot(spbuf[other], tri_ref[...], preferred_element_type=F32)
        lbuf[slot] = sums[:, :blk]
        rsbuf[slot] = sums[:, blk:]
        mask1 = mask_ref[(i1 == j1).astype(I32)]
        neg_abs = lax.bitcast_convert_type(
            lax.bitcast_convert_type(w, jnp.uint32) | sign_bit, F32)
        sp2 = jnp.maximum(w, 0.0) + jnp.log(1.0 + jnp.exp2(neg_abs)) * LOG2E
        dbuf[slot] = w - sp2
        spbuf[slot] = sp2.astype(BF16) * jnp.concatenate([mask1, mask1], axis=0)
        return c1_next, c1, c2

    def in_flight(state):
        c1, c2, c3 = state
        return jnp.logical_or(c1 <= last, jnp.logical_or(c2 <= last, c3 <= last))

    def four_steps(state):
        for slot in (0, 1, 0, 1):
            state = step(*state, slot)
        return state

    none = jnp.int32(n_items)
    lax.while_loop(in_flight, four_steps, (jnp.int32(0), none, none))


def _attention(q, k, v, bsz, seq):
    sb = k.shape[-1]
    blk = min(ATTN_BLOCK, seq)
    n_blk = seq // blk
    n_pairs = sb // LANES
    q3, k3, v3 = (a.reshape(bsz, seq, a.shape[-1]) for a in (q, k, v))
    per_head = HEADS_PER_BLOCK * LANES
    jj = lax.broadcasted_iota(I32, (blk, blk), 0)
    ss = lax.broadcasted_iota(I32, (blk, blk), 1)
    tri = jnp.concatenate([(jj > ss).astype(BF16), jnp.ones((blk, LANES), BF16)], axis=1)
    masks = jnp.stack([jnp.ones((blk, blk), BF16), (ss < jj).astype(BF16),
                       jnp.zeros((blk, blk), BF16)])
    items = [(i, j) for i in range(n_blk) for j in range(i, -1, -1)]
    item_i = jnp.asarray([i for i, _ in items], I32)
    item_j = jnp.asarray([j for _, j in items], I32)
    whole = lambda b, p, ti, tj: (b, 0, p)
    grid_spec = pltpu.PrefetchScalarGridSpec(
        num_scalar_prefetch=2,
        grid=(bsz, n_pairs),
        in_specs=[
            pl.BlockSpec((None, seq, per_head), whole),
            pl.BlockSpec((None, seq, LANES), whole),
            pl.BlockSpec((None, seq, per_head), whole),
            pl.BlockSpec((blk, blk + LANES), lambda b, p, ti, tj: (0, 0)),
            pl.BlockSpec((3, blk, blk), lambda b, p, ti, tj: (0, 0, 0)),
        ],
        out_specs=pl.BlockSpec((None, seq, LANES), whole),
        scratch_shapes=[
            pltpu.VMEM((2, HEADS_PER_BLOCK * blk, blk), F32),
            pltpu.VMEM((2, HEADS_PER_BLOCK * blk, blk), BF16),
            pltpu.VMEM((2, HEADS_PER_BLOCK * blk, blk), F32),
            pltpu.VMEM((2, HEADS_PER_BLOCK * blk, LANES), F32),
            pltpu.VMEM((blk, LANES), F32),
            pltpu.VMEM((HEADS_PER_BLOCK * blk, LANES), F32),
        ],
    )
    out = pl.pallas_call(
        functools.partial(_attn_kernel, blk=blk, n_items=len(items)),
        grid_spec=grid_spec,
        out_shape=jax.ShapeDtypeStruct((bsz, seq, sb), BF16),
        compiler_params=pltpu.CompilerParams(
            dimension_semantics=("arbitrary", "arbitrary"), vmem_limit_bytes=40 * MIB),
        name="sb_attention",
    )(item_i, item_j, q3, k3, v3, tri, masks)
    return out.reshape(bsz * seq, sb)


def _ssm_params(lam_re, lam_im, log_dt, b_re, b_im, c_re, c_im, chunk, n_chunks):
    lam = lax.complex(lam_re.astype(F32), lam_im.astype(F32))
    dt = jnp.exp(log_dt.astype(F32))[:, None]
    lam_dt = lam * dt
    lam_bar = jnp.exp(lam_dt)
    b_bar = ((lam_bar - 1.0) / lam)[:, :, None] * lax.complex(b_re.astype(F32), b_im.astype(F32))
    b_t = jnp.swapaxes(b_bar, 1, 2)
    c_mat = lax.complex(c_re.astype(F32), c_im.astype(F32))
    steps = jnp.arange(chunk + 1, dtype=F32)
    pw = jnp.exp(lam_dt[:, None, :] * steps[None, :, None])
    npw = jnp.exp(-lam_dt[:, None, :] * steps[None, :chunk, None])

    def halves(lo, hi):
        return jnp.concatenate([lo, hi], axis=-1)

    def b_side(z):
        return [halves(jnp.real(z), jnp.real(z)), halves(-jnp.imag(z), jnp.imag(z))]

    def c_side(z):
        return [halves(jnp.real(z), jnp.imag(z)), halves(jnp.imag(z), jnp.real(z))]

    time_terms = jnp.stack(
        b_side(npw)
        + b_side(pw[:, chunk - 1::-1])
        + c_side(pw[:, :chunk])
        + c_side(pw[:, 1:]), axis=1)
    chan_terms = jnp.stack(
        [halves(jnp.real(b_t), jnp.imag(b_t)), halves(jnp.imag(b_t), jnp.real(b_t)),
         halves(jnp.real(c_mat), -jnp.real(c_mat)), halves(-jnp.imag(c_mat), -jnp.imag(c_mat))],
        axis=1)
    n_steps = max(1, (n_chunks - 1).bit_length())
    powers = []
    cur = pw[:, chunk, :]
    for _ in range(n_steps):
        powers.append(cur)
        cur = cur * cur
    lam_pow = jnp.stack(powers, axis=1)
    a1 = halves(jnp.real(lam_pow), jnp.real(lam_pow))
    a2 = halves(-jnp.imag(lam_pow), jnp.imag(lam_pow))
    return time_terms, chan_terms, a1, a2


def _ssm_kernel(u_ref, time_ref, chan_ref, a1_ref, a2_ref, perm_ref, perm_t_ref, y_ref,
                u8_ref, y8_ref, toep_ref, *, n_chunks, n_steps, group):
    seq = u_ref.shape[0]
    tiles = seq // SSM_FOLD
    lc = toep_ref.shape[0]
    per_chunk = (lc // group) // SSM_FOLD
    p2 = a1_ref.shape[2]
    group_shift = group.bit_length() - 1
    nt = (((1,), (1,)), ((), ()))

    folded = jnp.concatenate(
        [u_ref[pl.ds(s, tiles, stride=SSM_FOLD), :] for s in range(SSM_FOLD)], axis=1)
    regrouped = jnp.dot(folded.astype(BF16), perm_ref[...], preferred_element_type=F32)
    n_groups = LANES // group
    for g in range(n_groups):
        u8_ref[g] = regrouped[:, g * LANES:(g + 1) * LANES]

    for g in range(n_groups):
        def table(t1, t2, v1, v2):
            full = (time_ref[g, t1][:, None, :] * chan_ref[g, v1][None, :, :]
                    + time_ref[g, t2][:, None, :] * chan_ref[g, v2][None, :, :])
            return full.reshape(lc, p2).astype(BF16)

        src = table(0, 1, 0, 1)
        to_state = table(2, 3, 0, 1)
        dst = table(4, 5, 2, 3)
        from_state = table(6, 7, 2, 3)
        cb = min(256, lc)
        for j in range(lc // cb):
            blk = lax.dot_general(src, dst[j * cb:(j + 1) * cb], nt, preferred_element_type=F32)
            s_idx = lax.broadcasted_iota(I32, (lc, cb), 0) >> group_shift
            t_idx = (lax.broadcasted_iota(I32, (lc, cb), 1) + j * cb) >> group_shift
            toep_ref[:, j * cb:(j + 1) * cb] = jnp.where(s_idx <= t_idx, blk, 0.0).astype(BF16)
        u = jnp.concatenate(
            [u8_ref[g, pl.ds(tau, n_chunks, stride=per_chunk), :] for tau in range(per_chunk)],
            axis=1).astype(BF16)
        y = jnp.dot(u, toep_ref[...], preferred_element_type=F32)
        z = jnp.dot(u, to_state, preferred_element_type=F32)
        n = lax.broadcasted_iota(I32, z.shape, 0)
        x = jnp.where(n >= 1, pltpu.roll(z, 1, 0), 0.0)
        for k in range(n_steps):
            sh = 1 << k
            xs = jnp.where(n >= sh, pltpu.roll(x, sh, 0), 0.0)
            x = (x + a1_ref[g, k:k + 1, :] * xs
                 + a2_ref[g, k:k + 1, :] * pltpu.roll(xs, p2 // 2, 1))
        y = y + lax.dot_general(x.astype(BF16), from_state, nt, preferred_element_type=F32)
        for tau in range(per_chunk):
            y8_ref[g, pl.ds(tau, n_chunks, stride=per_chunk), :] = y[:, tau * LANES:(tau + 1) * LANES]

    y8 = jnp.concatenate([y8_ref[g] for g in range(n_groups)], axis=1)
    unfolded = jnp.dot(y8.astype(BF16), perm_t_ref[...], preferred_element_type=F32)
    for s in range(SSM_FOLD):
        y_ref[pl.ds(s, tiles, stride=SSM_FOLD), :] = unfolded[:, s * LANES:(s + 1) * LANES]


def _ssm(u, bsz, seq, lam_re, lam_im, log_dt, b_re, b_im, c_re, c_im):
    g, p = lam_re.shape
    c = b_re.shape[-1]
    chunk = min(SSM_CHUNK, seq)
    n_chunks = seq // chunk
    assert n_chunks & (n_chunks - 1) == 0 and c & (c - 1) == 0, (n_chunks, c)
    assert SSM_FOLD * c == LANES and chunk % SSM_FOLD == 0
    lc = chunk * c
    per_block = LANES // c
    time_terms, chan_terms, a1, a2 = _ssm_params(
        lam_re, lam_im, log_dt, b_re, b_im, c_re, c_im, chunk, n_chunks)
    n_steps = a1.shape[1]
    src_idx = jnp.arange(SSM_FOLD * LANES, dtype=I32)
    s_of, g_of, c_of = src_idx // LANES, (src_idx % LANES) // c, src_idx % c
    dst_idx = g_of * LANES + s_of * c + c_of
    perm = (dst_idx[:, None] == src_idx[None, :]).astype(BF16)
    block = lambda cb, b: (cb, 0, 0)
    fixed = lambda cb, b: (0, 0)
    y = pl.pallas_call(
        functools.partial(_ssm_kernel, n_chunks=n_chunks, n_steps=n_steps, group=c),
        grid=(g // per_block, bsz),
        in_specs=[
            pl.BlockSpec((None, seq, LANES), lambda cb, b: (b, 0, cb)),
            pl.BlockSpec((per_block,) + time_terms.shape[1:], lambda cb, b: (cb, 0, 0, 0)),
            pl.BlockSpec((per_block,) + chan_terms.shape[1:], lambda cb, b: (cb, 0, 0, 0)),
            pl.BlockSpec((per_block, n_steps, 2 * p), block),
            pl.BlockSpec((per_block, n_steps, 2 * p), block),
            pl.BlockSpec((SSM_FOLD * LANES, SSM_FOLD * LANES), fixed),
            pl.BlockSpec((SSM_FOLD * LANES, SSM_FOLD * LANES), fixed),
        ],
        out_specs=pl.BlockSpec((None, seq, LANES), lambda cb, b: (b, 0, cb)),
        out_shape=jax.ShapeDtypeStruct((bsz, seq, g * c), F32),
        scratch_shapes=[
            pltpu.VMEM((per_block, seq // SSM_FOLD, LANES), F32),
            pltpu.VMEM((per_block, seq // SSM_FOLD, LANES), F32),
            pltpu.VMEM((lc, lc), BF16),
        ],
        compiler_params=pltpu.CompilerParams(
            dimension_semantics=("arbitrary", "arbitrary"), vmem_limit_bytes=56 * MIB),
        name="s5_chunked_scan",
    )(u.reshape(bsz, seq, g * c), time_terms, chan_terms, a1, a2, perm, perm.T)
    return y.reshape(bsz * seq, g * c)


def _post_kernel(x_ref, ysb_ref, yss_ref, u_ref, d_ref, wglu_ref, gsb_ref, gssm_ref, wout_ref,
                 ln2_ref, wr_ref, br_ref, tri_ref, x1_ref, xn_ref, idx_ref, gate_ref, rank_ref,
                 count_ref, running_ref, wglu_bf, wout_bf, *, sb, n_exp):
    @pl.when(pl.program_id(0) == 0)
    def _():
        running_ref[...] = jnp.zeros_like(running_ref)
        wglu_bf[...] = wglu_ref[...].astype(BF16)
        wout_bf[...] = wout_ref[...].astype(BF16)

    u = u_ref[...].astype(F32)
    y = yss_ref[...].astype(F32) + d_ref[...] * u
    y = y * (0.5 * (1.0 + jnp.tanh(math.sqrt(2.0 / math.pi) * (y + 0.044715 * (y * y * y)))))
    ab = jnp.dot(y.astype(BF16), wglu_bf[...], preferred_element_type=F32)
    w = ab.shape[1] // 2
    y_ssm = ab[:, :w] * (1.0 / (1.0 + jnp.exp(-ab[:, w:])))
    m_sb = _rms(ysb_ref[...].astype(F32), gsb_ref[...])
    m_ssm = _rms(y_ssm, gssm_ref[...])
    x1 = (x_ref[...]
          + jnp.dot(m_sb.astype(BF16), wout_bf[:sb, :], preferred_element_type=F32)
          + jnp.dot(m_ssm.astype(BF16), wout_bf[sb:, :], preferred_element_type=F32))
    x1_ref[...] = x1
    xn = _rms(x1, ln2_ref[...])
    xn_ref[...] = xn
    def split(a):
        hi = a.astype(BF16)
        return hi, (a - hi.astype(F32)).astype(BF16)

    xn_hi, xn_lo = split(xn)
    wr_hi, wr_lo = split(wr_ref[...])
    logits = (jnp.dot(xn_hi, wr_hi, preferred_element_type=F32)
              + jnp.dot(xn_hi, wr_lo, preferred_element_type=F32)
              + jnp.dot(xn_lo, wr_hi, preferred_element_type=F32)) + br_ref[...]
    lane = lax.broadcasted_iota(I32, logits.shape, 1).astype(F32)
    out_lane = lax.broadcasted_iota(I32, (logits.shape[0], LANES), 1)
    idx_out = jnp.zeros((logits.shape[0], LANES), F32)
    val_out = jnp.zeros((logits.shape[0], LANES), F32)
    top = None
    denom = None
    work = logits
    chosen = []
    for k in range(TOP_K):
        m = jnp.max(work, axis=-1, keepdims=True)
        sel = jnp.min(jnp.where(work == m, lane, float(n_exp)), axis=-1, keepdims=True)
        hit = lane == sel
        chosen.append(hit)
        work = jnp.where(hit, -jnp.inf, work)
        if k == 0:
            top = m
        e = jnp.exp(m - top)
        denom = e if denom is None else denom + e
        idx_out = jnp.where(out_lane == k, sel, idx_out)
        val_out = jnp.where(out_lane == k, e, val_out)
    idx_ref[...] = idx_out[:, :TOP_K].astype(I32)
    gate_ref[...] = (val_out / denom)[:, :TOP_K]
    member = functools.reduce(jnp.logical_or, chosen).astype(BF16)
    before = running_ref[...] + jnp.dot(tri_ref[...], member, preferred_element_type=F32)
    rank_out = jnp.zeros((logits.shape[0], LANES), F32)
    for k in range(TOP_K):
        rk = jnp.sum(jnp.where(chosen[k], before, 0.0), axis=-1, keepdims=True)
        rank_out = jnp.where(out_lane == k, rk, rank_out)
    rank_ref[...] = rank_out[:, :TOP_K].astype(I32)
    running = running_ref[...] + jnp.sum(member.astype(F32), axis=0, keepdims=True)
    running_ref[...] = running
    count_ref[...] = running


def _post(x2, y_sb, y_ss, u, ssm_d, w_glu, g_sb, g_ssm, w_out, ln2_g, w_router, b_router):
    t, d = x2.shape
    sb = y_sb.shape[1]
    w = y_ss.shape[1]
    n_exp = w_router.shape[1]
    tm = min(TOKEN_TILE, t)
    row = lambda i: (i, 0)
    fixed = lambda i: (0, 0)
    earlier = (lax.broadcasted_iota(I32, (tm, tm), 1)
               < lax.broadcasted_iota(I32, (tm, tm), 0)).astype(BF16)
    return pl.pallas_call(
        functools.partial(_post_kernel, sb=sb, n_exp=n_exp),
        grid=(t // tm,),
        in_specs=[
            pl.BlockSpec((tm, d), row),
            pl.BlockSpec((tm, sb), row),
            pl.BlockSpec((tm, w), row),
            pl.BlockSpec((tm, w), row),
            pl.BlockSpec((1, w), fixed),
            pl.BlockSpec((w, 2 * w), fixed),
            pl.BlockSpec((1, sb), fixed),
            pl.BlockSpec((1, w), fixed),
            pl.BlockSpec((sb + w, d), fixed),
            pl.BlockSpec((1, d), fixed),
            pl.BlockSpec((d, n_exp), fixed),
            pl.BlockSpec((1, n_exp), fixed),
            pl.BlockSpec((tm, tm), fixed),
        ],
        out_specs=[
            pl.BlockSpec((tm, d), row),
            pl.BlockSpec((tm, d), row),
            pl.BlockSpec((tm, TOP_K), row),
            pl.BlockSpec((tm, TOP_K), row),
            pl.BlockSpec((tm, TOP_K), row),
            pl.BlockSpec((1, n_exp), fixed),
        ],
        out_shape=[
            jax.ShapeDtypeStruct((t, d), F32),
            jax.ShapeDtypeStruct((t, d), F32),
            jax.ShapeDtypeStruct((t, TOP_K), I32),
            jax.ShapeDtypeStruct((t, TOP_K), F32),
            jax.ShapeDtypeStruct((t, TOP_K), I32),
            jax.ShapeDtypeStruct((1, n_exp), F32),
        ],
        scratch_shapes=[pltpu.VMEM((1, n_exp), F32), pltpu.VMEM((w, 2 * w), BF16),
                        pltpu.VMEM((sb + w, d), BF16)],
        compiler_params=pltpu.CompilerParams(
            dimension_semantics=("arbitrary",), vmem_limit_bytes=48 * MIB),
        name="post_mixer_router",
    )(x2, y_sb, y_ss, u, ssm_d.reshape(1, w), w_glu, g_sb.reshape(1, sb),
      g_ssm.reshape(1, w), w_out, ln2_g.reshape(1, d), w_router,
      b_router.reshape(1, n_exp), earlier)


def _plan(top_idx, rank, counts, rows_per_block):
    t, k = top_idx.shape
    n_exp = counts.shape[-1]
    counts = counts.reshape(n_exp).astype(I32)
    padded = ((counts + rows_per_block - 1) // rows_per_block) * rows_per_block
    pad_ends = jnp.cumsum(padded)
    pad_starts = pad_ends - padded
    start_of = jnp.sum(jnp.where(top_idx[..., None] == jnp.arange(n_exp, dtype=I32),
                                 pad_starts, 0), axis=-1)
    dest = (start_of + rank).astype(I32)
    n_rows = t * k + n_exp * rows_per_block
    n_blocks = n_rows // rows_per_block
    block_start = jnp.arange(n_blocks, dtype=I32) * rows_per_block
    block_expert = jnp.minimum(
        jnp.sum((pad_ends[None, :] <= block_start[:, None]).astype(I32), axis=1), n_exp - 1)
    n_used = (pad_ends[-1] // rows_per_block).astype(I32).reshape(1)
    fill_start = jnp.concatenate([jnp.maximum(pad_ends - rows_per_block, 0).astype(I32), n_used])
    experts = jnp.arange(n_exp, dtype=I32)
    nonempty = counts > 0
    order = jnp.cumsum(nonempty.astype(I32)) - nonempty.astype(I32)
    later_ne = jnp.where((experts[None, :] > experts[:, None]) & nonempty[None, :],
                         experts[None, :], n_exp)
    next_tbl = jnp.min(later_ne, axis=1)
    onehot = block_expert[:, None] == experts[None, :]
    w_slot = jnp.sum(jnp.where(onehot, order & 1, 0), axis=1).astype(I32)
    w_next = jnp.sum(jnp.where(onehot, next_tbl, 0), axis=1).astype(I32)
    return dest, (block_expert, n_used, w_slot, w_next), fill_start, n_blocks


def _dispatch_kernel(fill_ref, dest_hbm, xn_ref, xs_hbm, idx_smem, zeros_ref, idx_sem, row_sem,
                     fill_sem, *, tile, n_exp, fill_rows):
    i = pl.program_id(0)
    n = pl.num_programs(0)
    per_tile = TOP_K * tile

    def idx_copy(b):
        src = dest_hbm.at[pl.ds(pl.multiple_of(b * per_tile, per_tile), per_tile)]
        dst = idx_smem.at[pl.ds(pl.multiple_of((b % 2) * per_tile, per_tile), per_tile)]
        return pltpu.make_async_copy(src, dst, idx_sem.at[b % 2])

    @pl.when(i == 0)
    def _():
        idx_copy(0).start()
        zeros_ref[...] = jnp.zeros_like(zeros_ref)
        for e in range(n_exp):
            start = pl.multiple_of(fill_ref[e], 8)
            pltpu.make_async_copy(zeros_ref, xs_hbm.at[pl.ds(start, fill_rows), 0, :],
                                  fill_sem).start()
        for e in range(n_exp):
            pltpu.make_async_copy(zeros_ref, xs_hbm.at[pl.ds(0, fill_rows), 0, :],
                                  fill_sem).wait()
        n_blocks = xs_hbm.shape[0] // fill_rows

        def fill_unused(b, carry):
            cp = pltpu.make_async_copy(
                zeros_ref, xs_hbm.at[pl.ds(pl.multiple_of(b * fill_rows, fill_rows), fill_rows), 0, :],
                fill_sem)
            cp.start()
            cp.wait()
            return carry

        lax.fori_loop(fill_ref[n_exp], n_blocks, fill_unused, 0)

    idx_copy(i).wait()

    @pl.when(i + 1 < n)
    def _():
        idx_copy(i + 1).start()

    base = pl.multiple_of((i % 2) * per_tile, per_tile)
    quarter = tile // TOP_K
    for part in range(TOP_K):
        def issue(g, carry, part=part):
            group = part * (quarter // SUBLANES) + g
            for s in range(SUBLANES):
                for k in range(TOP_K):
                    dst_row = idx_smem[base + (group * SUBLANES + s) * TOP_K + k]
                    pltpu.make_async_copy(xn_ref.at[group, pl.ds(s, 1), :], xs_hbm.at[dst_row],
                                          row_sem.at[part]).start(priority=k % 2)
            return carry

        lax.fori_loop(0, quarter // SUBLANES, issue, 0)
    for part in range(TOP_K):
        pltpu.make_async_copy(xn_ref, xn_ref, row_sem.at[part]).wait()


def _dispatch(xn, dest, fill_start, n_rows):
    t, d = xn.shape
    tile = min(DISPATCH_TILE, t)
    n_exp = fill_start.shape[0] - 1
    grid_spec = pltpu.PrefetchScalarGridSpec(
        num_scalar_prefetch=1,
        grid=(t // tile,),
        in_specs=[
            pl.BlockSpec(memory_space=pl.ANY),
            pl.BlockSpec((tile // SUBLANES, SUBLANES, d), lambda i, fs: (i, 0, 0)),
        ],
        out_specs=pl.BlockSpec(memory_space=pl.ANY),
        scratch_shapes=[
            pltpu.SMEM((2 * TOP_K * tile,), I32),
            pltpu.VMEM((EXPERT_ROWS, d), F32),
            pltpu.SemaphoreType.DMA((2,)),
            pltpu.SemaphoreType.DMA((TOP_K,)),
            pltpu.SemaphoreType.DMA,
        ],
    )
    return pl.pallas_call(
        functools.partial(_dispatch_kernel, tile=tile, n_exp=n_exp, fill_rows=EXPERT_ROWS),
        grid_spec=grid_spec,
        out_shape=jax.ShapeDtypeStruct((n_rows, 1, d), F32),
        compiler_params=pltpu.CompilerParams(
            dimension_semantics=("arbitrary",), vmem_limit_bytes=40 * MIB),
        name="moe_dispatch",
    )(fill_start, dest.reshape(t * TOP_K), xn.reshape(t // SUBLANES, SUBLANES, d))


def _expert_kernel(be_ref, nused_ref, wslot_ref, wnext_ref, xs_hbm, wg_hbm, bg_ref, wu_hbm, bu_ref,
                   wd_hbm, bd_ref, y_hbm, xbuf, ybuf, x_sem, y_sem, wg32, wu32, wd32, w_sem,
                   wg_bf, wu_bf, wd_bf, *, rows, n_exp):
    i = pl.program_id(0)
    n = nused_ref[0]

    def w_copies(e, slot):
        return [pltpu.make_async_copy(src.at[e], dst.at[slot], w_sem.at[slot])
                for src, dst in ((wg_hbm, wg32), (wu_hbm, wu32), (wd_hbm, wd32))]

    def x_copy(b):
        return pltpu.make_async_copy(
            xs_hbm.at[pl.ds(pl.multiple_of(b * rows, rows), rows), 0, :], xbuf.at[b % 2],
            x_sem.at[b % 2])

    def y_copy(b):
        return pltpu.make_async_copy(
            ybuf.at[b % 2], y_hbm.at[pl.ds(pl.multiple_of(b * rows, rows), rows), 0, :],
            y_sem.at[b % 2])

    @pl.when(i < n)
    def _():
        @pl.when(i == 0)
        def _():
            x_copy(0).start()
            for cp in w_copies(be_ref[0], wslot_ref[0]):
                cp.start()

        @pl.when(i + 1 < n)
        def _():
            x_copy(i + 1).start()

        changed = jnp.logical_or(i == 0, be_ref[i] != be_ref[jnp.maximum(i - 1, 0)])

        @pl.when(changed)
        def _():
            slot = wslot_ref[i]
            for cp in w_copies(be_ref[i], slot):
                cp.wait()

            @pl.when(wnext_ref[i] < n_exp)
            def _():
                for cp in w_copies(wnext_ref[i], 1 - slot):
                    cp.start()
            wg_bf[...] = wg32[slot].astype(BF16)
            wu_bf[...] = wu32[slot].astype(BF16)
            wd_bf[...] = wd32[slot].astype(BF16)

        x_copy(i).wait()
        x = xbuf[i % 2].astype(BF16)
        gate = jnp.dot(x, wg_bf[...], preferred_element_type=F32) + bg_ref[...]
        up = jnp.dot(x, wu_bf[...], preferred_element_type=F32) + bu_ref[...]
        gate = jnp.minimum(gate, SWIGLU_LIMIT)
        up = jnp.clip(up, -SWIGLU_LIMIT, SWIGLU_LIMIT)
        glu = gate * (1.0 / (1.0 + jnp.exp(-SWIGLU_ALPHA * gate)))
        hidden = ((up + 1.0) * glu).astype(BF16)
        y = jnp.dot(hidden, wd_bf[...], preferred_element_type=F32) + bd_ref[...]

        @pl.when(i >= 2)
        def _():
            y_copy(i - 2).wait()
        ybuf[i % 2] = y
        y_copy(i).start()

        @pl.when(i == n - 1)
        def _():
            y_copy(i).wait()

            @pl.when(i >= 1)
            def _():
                y_copy(i - 1).wait()

    @pl.when(i >= n)
    def _():
        ybuf[i % 2] = jnp.zeros((rows, ybuf.shape[2]), F32)
        y_copy(i).start()
        y_copy(i).wait()


def _experts(x_sorted, block_plan, n_blocks, w_gate, b_gate, w_up, b_up, w_down, b_down):
    n_rows, _, d = x_sorted.shape
    n_exp, _, f = w_gate.shape
    rows = EXPERT_ROWS
    bmap = lambda i, be, nu, ws, wn: (be[i], 0, 0)
    hbm = pl.BlockSpec(memory_space=pl.ANY)
    grid_spec = pltpu.PrefetchScalarGridSpec(
        num_scalar_prefetch=4,
        grid=(n_blocks,),
        in_specs=[
            hbm,
            hbm, pl.BlockSpec((None, 1, f), bmap),
            hbm, pl.BlockSpec((None, 1, f), bmap),
            hbm, pl.BlockSpec((None, 1, d), bmap),
        ],
        out_specs=hbm,
        scratch_shapes=[
            pltpu.VMEM((2, rows, d), F32),
            pltpu.VMEM((2, rows, d), F32),
            pltpu.SemaphoreType.DMA((2,)),
            pltpu.SemaphoreType.DMA((2,)),
            pltpu.VMEM((2, d, f), F32),
            pltpu.VMEM((2, d, f), F32),
            pltpu.VMEM((2, f, d), F32),
            pltpu.SemaphoreType.DMA((2,)),
            pltpu.VMEM((d, f), BF16),
            pltpu.VMEM((d, f), BF16),
            pltpu.VMEM((f, d), BF16),
        ],
    )
    return pl.pallas_call(
        functools.partial(_expert_kernel, rows=rows, n_exp=n_exp),
        grid_spec=grid_spec,
        out_shape=jax.ShapeDtypeStruct((n_rows, 1, d), F32),
        compiler_params=pltpu.CompilerParams(
            dimension_semantics=("arbitrary",), vmem_limit_bytes=52 * MIB),
        name="moe_experts",
    )(*block_plan, x_sorted,
      w_gate, b_gate.reshape(n_exp, 1, f), w_up, b_up.reshape(n_exp, 1, f),
      w_down, b_down.reshape(n_exp, 1, d))


def _combine_kernel(dest_hbm, y_hbm, x1_ref, gate_ref, lnf_ref, o_ref, idx_smem, ybuf,
                    idx_sem, row_sem, *, tile):
    i = pl.program_id(0)
    n = pl.num_programs(0)
    per_tile = TOP_K * tile

    def slot_base(b):
        return pl.multiple_of((b % 3) * per_tile, per_tile)

    def idx_copy(b):
        src = dest_hbm.at[pl.ds(pl.multiple_of(b * per_tile, per_tile), per_tile)]
        return pltpu.make_async_copy(src, idx_smem.at[pl.ds(slot_base(b), per_tile)],
                                     idx_sem.at[b % 3])

    def gather(b):
        base = slot_base(b)
        buf = ybuf.at[b % 2]
        sem = row_sem.at[b % 2]

        def issue(g, carry):
            for s in range(SUBLANES):
                pltpu.make_async_copy(y_hbm.at[idx_smem[base + g * SUBLANES + s]],
                                      buf.at[g, pl.ds(s, 1), :], sem).start(priority=s % 2)
            return carry

        lax.fori_loop(0, per_tile // SUBLANES, issue, 0)

    @pl.when(i == 0)
    def _():
        idx_copy(0).start()
        idx_copy(0).wait()
        gather(0)

        @pl.when(n > 1)
        def _():
            idx_copy(1).start()

    @pl.when(i + 1 < n)
    def _():
        idx_copy(i + 1).wait()

        @pl.when(i + 2 < n)
        def _():
            idx_copy(i + 2).start()
        gather(i + 1)

    cur = i % 2
    pltpu.make_async_copy(ybuf.at[1 - cur], ybuf.at[cur], row_sem.at[cur]).wait()
    gates = gate_ref[...]
    x = x1_ref[...]
    groups = tile // SUBLANES
    for k in range(TOP_K):
        yk = ybuf[cur, k * groups:(k + 1) * groups].reshape(tile, x.shape[1])
        x = x + gates[:, k:k + 1] * yk
    o_ref[...] = _rms(x, lnf_ref[...])


def _combine(dest, y_rows, x1, gates, ln_f_g):
    t, d = x1.shape
    tile = min(COMBINE_TILE, t)
    n_tiles = t // tile
    dest_tiles = dest.reshape(n_tiles, tile, TOP_K).transpose(0, 2, 1).reshape(n_tiles * TOP_K * tile)
    return pl.pallas_call(
        functools.partial(_combine_kernel, tile=tile),
        grid=(n_tiles,),
        in_specs=[
            pl.BlockSpec(memory_space=pl.ANY),
            pl.BlockSpec(memory_space=pl.ANY),
            pl.BlockSpec((tile, d), lambda i: (i, 0)),
            pl.BlockSpec((tile, TOP_K), lambda i: (i, 0)),
            pl.BlockSpec((1, d), lambda i: (0, 0)),
        ],
        out_specs=pl.BlockSpec((tile, d), lambda i: (i, 0)),
        out_shape=jax.ShapeDtypeStruct((t, d), F32),
        scratch_shapes=[
            pltpu.SMEM((3 * TOP_K * tile,), I32),
            pltpu.VMEM((2, TOP_K * tile // SUBLANES, SUBLANES, d), F32),
            pltpu.SemaphoreType.DMA((3,)),
            pltpu.SemaphoreType.DMA((2,)),
        ],
        compiler_params=pltpu.CompilerParams(
            dimension_semantics=("arbitrary",), vmem_limit_bytes=40 * MIB),
        name="moe_combine",
    )(dest_tiles, y_rows, x1, gates, ln_f_g.reshape(1, d))


def kernel(x, ln1_g, w_in, lam_re, lam_im, log_dt, ssm_b_re, ssm_b_im, ssm_c_re, ssm_c_im,
           ssm_d, w_glu, g_sb, g_ssm, w_out, ln2_g, w_router, b_router, w_gate, b_gate,
           w_up, b_up, w_down, b_down, ln_f_g):
    bsz, seq, d = x.shape
    assert ln1_g.shape[0] == 1, "depth-1 block only"
    ssm_w = ssm_d.shape[1]
    sb = g_sb.shape[1]
    x2 = x.reshape(bsz * seq, d)
    q, k, v, u = _in_proj(x2, ln1_g[0], w_in[0], sb, ssm_w)
    y_sb = _attention(q, k, v, bsz, seq)
    y_ss = _ssm(u, bsz, seq, lam_re[0], lam_im[0], log_dt[0], ssm_b_re[0], ssm_b_im[0],
                ssm_c_re[0], ssm_c_im[0])
    x1, xn, top_idx, gates, rank, counts = _post(
        x2, y_sb, y_ss, u, ssm_d[0], w_glu[0], g_sb[0], g_ssm[0], w_out[0], ln2_g[0],
        w_router[0], b_router[0])
    dest, block_plan, fill_start, n_blocks = _plan(top_idx, rank, counts, EXPERT_ROWS)
    x_sorted = _dispatch(xn, dest, fill_start, n_blocks * EXPERT_ROWS)
    y_rows = _experts(x_sorted, block_plan, n_blocks, w_gate[0], b_gate[0],
                      w_up[0], b_up[0], w_down[0], b_down[0])
    out = _combine(dest, y_rows, x1, gates, ln_f_g)
    return out.reshape(bsz, seq, d)
```

```python
import functools
import math

import jax
import jax.numpy as jnp
from jax import lax
from jax.experimental import pallas as pl
from jax.experimental.pallas import tpu as pltpu

F32 = jnp.float32
BF16 = jnp.bfloat16
I32 = jnp.int32

EPS = 1e-5
SB_HEAD_DIM = 64
SSM_GROUP = 16
SSM_STATE = 64
TOP_K = 4
SWIGLU_LIMIT = 7.0
SWIGLU_ALPHA = 1.702

LANES = 128
SUBLANES = 8
HEADS_PER_BLOCK = LANES // SB_HEAD_DIM
ATTN_BLOCK = 256
ATTN_EXP_FLOOR = 160.0
SSM_CHUNK = 64
SSM_FOLD = 8
TOKEN_TILE = 512
EXPERT_ROWS = 256
COMBINE_TILE = 256
DISPATCH_TILE = 1024
MIB = 1024 * 1024
LOG2E = 1.4426950408889634


def _rms(x, g):
    return x * lax.rsqrt(jnp.mean(x * x, axis=-1, keepdims=True) + EPS) * g


def _in_proj_kernel(x_ref, g_ref, w_ref, q_ref, k_ref, v_ref, u_ref, w_bf, *, sb, scale):
    @pl.when(pl.program_id(0) == 0)
    def _():
        w_bf[...] = w_ref[...].astype(BF16)

    h = _rms(x_ref[...], g_ref[...])
    proj = jnp.dot(h.astype(BF16), w_bf[...], preferred_element_type=F32)
    k_ref[...] = proj[:, sb:2 * sb].astype(BF16)
    u_ref[...] = proj[:, 3 * sb:]
    lane = lax.broadcasted_iota(I32, (1, LANES), 1)
    q = (proj[:, :sb] * scale).astype(BF16)
    v = proj[:, 2 * sb:3 * sb].astype(BF16)
    zero = jnp.zeros((), BF16)
    for pair in range(sb // LANES):
        cols = slice(pair * LANES, (pair + 1) * LANES)
        for head in range(HEADS_PER_BLOCK):
            own = (lane // SB_HEAD_DIM) == head
            out = slice((pair * HEADS_PER_BLOCK + head) * LANES,
                        (pair * HEADS_PER_BLOCK + head + 1) * LANES)
            q_ref[:, out] = jnp.where(own, q[:, cols], zero)
            v_ref[:, out] = jnp.where(own, v[:, cols], zero)


def _in_proj(x2, ln1_g, w_in, sb, ssm_w):
    t, d = x2.shape
    n_in = w_in.shape[1]
    tm = min(TOKEN_TILE, t)
    scale = LOG2E / math.sqrt(SB_HEAD_DIM)
    return pl.pallas_call(
        functools.partial(_in_proj_kernel, sb=sb, scale=scale),
        grid=(t // tm,),
        in_specs=[
            pl.BlockSpec((tm, d), lambda i: (i, 0)),
            pl.BlockSpec((1, d), lambda i: (0, 0)),
            pl.BlockSpec((d, n_in), lambda i: (0, 0)),
        ],
        out_specs=[
            pl.BlockSpec((tm, HEADS_PER_BLOCK * sb), lambda i: (i, 0)),
            pl.BlockSpec((tm, sb), lambda i: (i, 0)),
            pl.BlockSpec((tm, HEADS_PER_BLOCK * sb), lambda i: (i, 0)),
            pl.BlockSpec((tm, ssm_w), lambda i: (i, 0)),
        ],
        out_shape=[
            jax.ShapeDtypeStruct((t, HEADS_PER_BLOCK * sb), BF16),
            jax.ShapeDtypeStruct((t, sb), BF16),
            jax.ShapeDtypeStruct((t, HEADS_PER_BLOCK * sb), BF16),
            jax.ShapeDtypeStruct((t, ssm_w), F32),
        ],
        scratch_shapes=[pltpu.VMEM((d, n_in), BF16)],
        compiler_params=pltpu.CompilerParams(
            dimension_semantics=("arbitrary",), vmem_limit_bytes=48 * MIB),
        name="in_proj",
    )(x2, ln1_g.reshape(1, d), w_in)


def _attn_kernel(ti_ref, tj_ref, q_ref, k_ref, v_ref, tri_ref, mask_ref, o_ref,
                 dbuf, spbuf, lbuf, rsbuf, acc_ref, r_ref, *, blk, n_blk, n_items):
    sign_bit = jnp.uint32(0x80000000)
    last = n_items - 1
    ALL, CAUSAL, NONE = 0, 1, 2

    for ref in (dbuf, spbuf, lbuf, rsbuf, acc_ref, r_ref):
        ref[...] = jnp.zeros_like(ref)

    def both_heads(ref, start):
        rows = pl.ds(pl.multiple_of(start, blk), blk)
        return jnp.concatenate([ref[rows, h * LANES:(h + 1) * LANES]
                                for h in range(HEADS_PER_BLOCK)], axis=0)

    def item(idx):
        c = jnp.clip(idx, 0, last)
        return ti_ref[c], tj_ref[c]

    half = n_blk // 2
    stream_end = (half * (half + 1) // 2, n_items)

    def step(state, x):
        cur, p1, p2, out_blk = list(state[:2]), state[2], state[3], list(state[4:])
        slot, other = x, 1 - x
        c1 = jnp.where(cur[x] < stream_end[x], cur[x], n_items)
        c3 = p2
        i1, j1 = item(c1)
        ks = k_ref[pl.ds(pl.multiple_of(j1 * blk, blk), blk), :]
        w = lax.dot_general(both_heads(q_ref, i1 * blk), ks, (((1,), (1,)), ((), ())),
                            preferred_element_type=F32)
        i3, j3 = item(c3)
        valid = c3 <= last
        first = jnp.logical_and(i3 == j3, valid)
        mask3 = mask_ref[jnp.where(valid, (i3 == j3).astype(I32), NONE)]
        i3 = jnp.where(valid, i3, out_blk[x])
        r_prev = jnp.where(first, 0.0, r_ref[x])
        r_new = r_prev + rsbuf[other]
        r_ref[x] = r_new
        done = jnp.logical_and(valid, jnp.min(r_new) >= ATTN_EXP_FLOOR)
        next_block = lax.shift_right_logical((i3 + 1) * (i3 + 2), 1)
        advanced = jnp.where(done, jnp.maximum(cur[x] + 1, next_block), cur[x] + 1)
        cur[x] = jnp.minimum(advanced, stream_end[x])
        out_blk[x] = i3
        r_wide = jnp.concatenate([r_prev] * (blk // LANES), axis=1)
        a = jnp.exp2(dbuf[slot] - (lbuf[other] + r_wide))
        ab = a.astype(BF16) * jnp.concatenate([mask3, mask3], axis=0)
        a_cat = jnp.concatenate([ab[:blk], ab[blk:]], axis=1)
        acc = jnp.where(first, 0.0, acc_ref[x]) + jnp.dot(
            a_cat, both_heads(v_ref, j3 * blk), preferred_element_type=F32)
        acc_ref[x] = acc
        o_ref[pl.ds(pl.multiple_of(i3 * blk, blk), blk), :] = acc.astype(o_ref.dtype)
        sums = jnp.dot(spbuf[other], tri_ref[...], preferred_element_type=F32)
        lbuf[slot] = sums[:, :blk]
        rsbuf[slot] = sums[:, blk:]
        mask1 = mask_ref[(i1 == j1).astype(I32)]
        neg_abs = lax.bitcast_convert_type(
            lax.bitcast_convert_type(w, jnp.uint32) | sign_bit, F32)
        sp2 = jnp.maximum(w, 0.0) + jnp.log(1.0 + jnp.exp2(neg_abs)) * LOG2E
        dbuf[slot] = w - sp2
        spbuf[slot] = sp2.astype(BF16) * jnp.concatenate([mask1, mask1], axis=0)
        return cur[0], cur[1], c1, p1, out_blk[0], out_blk[1]

    def in_flight(state):
        cur0, cur1, p1, p2 = state[:4]
        return functools.reduce(jnp.logical_or, [cur0 < stream_end[0], cur1 < stream_end[1],
                                                 p1 <= last, p2 <= last])

    def four_steps(state):
        for x in (0, 1, 0, 1):
            state = step(state, x)
        return state

    none = jnp.int32(n_items)
    lax.while_loop(in_flight, four_steps,
                   (jnp.int32(0), jnp.int32(stream_end[0]), none, none,
                    jnp.int32(0), jnp.int32(half)))


def _attention(q, k, v, bsz, seq):
    sb = k.shape[-1]
    blk = min(ATTN_BLOCK, seq)
    n_blk = seq // blk
    n_pairs = sb // LANES
    q3, k3, v3 = (a.reshape(bsz, seq, a.shape[-1]) for a in (q, k, v))
    per_head = HEADS_PER_BLOCK * LANES
    jj = lax.broadcasted_iota(I32, (blk, blk), 0)
    ss = lax.broadcasted_iota(I32, (blk, blk), 1)
    tri = jnp.concatenate([(jj > ss).astype(BF16), jnp.ones((blk, LANES), BF16)], axis=1)
    masks = jnp.stack([jnp.ones((blk, blk), BF16), (ss < jj).astype(BF16),
                       jnp.zeros((blk, blk), BF16)])
    items = [(i, j) for i in range(n_blk) for j in range(i, -1, -1)]
    item_i = jnp.asarray([i for i, _ in items], I32)
    item_j = jnp.asarray([j for _, j in items], I32)
    whole = lambda b, p, ti, tj: (b, 0, p)
    grid_spec = pltpu.PrefetchScalarGridSpec(
        num_scalar_prefetch=2,
        grid=(bsz, n_pairs),
        in_specs=[
            pl.BlockSpec((None, seq, per_head), whole),
            pl.BlockSpec((None, seq, LANES), whole),
            pl.BlockSpec((None, seq, per_head), whole),
            pl.BlockSpec((blk, blk + LANES), lambda b, p, ti, tj: (0, 0)),
            pl.BlockSpec((3, blk, blk), lambda b, p, ti, tj: (0, 0, 0)),
        ],
        out_specs=pl.BlockSpec((None, seq, LANES), whole),
        scratch_shapes=[
            pltpu.VMEM((2, HEADS_PER_BLOCK * blk, blk), F32),
            pltpu.VMEM((2, HEADS_PER_BLOCK * blk, blk), BF16),
            pltpu.VMEM((2, HEADS_PER_BLOCK * blk, blk), F32),
            pltpu.VMEM((2, HEADS_PER_BLOCK * blk, LANES), F32),
            pltpu.VMEM((2, blk, LANES), F32),
            pltpu.VMEM((2, HEADS_PER_BLOCK * blk, LANES), F32),
        ],
    )
    out = pl.pallas_call(
        functools.partial(_attn_kernel, blk=blk, n_blk=n_blk, n_items=len(items)),
        grid_spec=grid_spec,
        out_shape=jax.ShapeDtypeStruct((bsz, seq, sb), BF16),
        compiler_params=pltpu.CompilerParams(
            dimension_semantics=("arbitrary", "arbitrary"), vmem_limit_bytes=40 * MIB),
        name="sb_attention",
    )(item_i, item_j, q3, k3, v3, tri, masks)
    return out.reshape(bsz * seq, sb)


def _ssm_params(lam_re, lam_im, log_dt, b_re, b_im, c_re, c_im, chunk, n_chunks):
    lam = lax.complex(lam_re.astype(F32), lam_im.astype(F32))
    dt = jnp.exp(log_dt.astype(F32))[:, None]
    lam_dt = lam * dt
    lam_bar = jnp.exp(lam_dt)
    b_bar = ((lam_bar - 1.0) / lam)[:, :, None] * lax.complex(b_re.astype(F32), b_im.astype(F32))
    b_t = jnp.swapaxes(b_bar, 1, 2)
    c_mat = lax.complex(c_re.astype(F32), c_im.astype(F32))
    steps = jnp.arange(chunk + 1, dtype=F32)
    pw = jnp.exp(lam_dt[:, None, :] * steps[None, :, None])
    npw = jnp.exp(-lam_dt[:, None, :] * steps[None, :chunk, None])

    def halves(lo, hi):
        return jnp.concatenate([lo, hi], axis=-1)

    def b_side(z):
        return [halves(jnp.real(z), jnp.real(z)), halves(-jnp.imag(z), jnp.imag(z))]

    def c_side(z):
        return [halves(jnp.real(z), jnp.imag(z)), halves(jnp.imag(z), jnp.real(z))]

    time_terms = jnp.stack(
        b_side(npw)
        + b_side(pw[:, chunk - 1::-1])
        + c_side(pw[:, :chunk])
        + c_side(pw[:, 1:]), axis=1)
    chan_terms = jnp.stack(
        [halves(jnp.real(b_t), jnp.imag(b_t)), halves(jnp.imag(b_t), jnp.real(b_t)),
         halves(jnp.real(c_mat), -jnp.real(c_mat)), halves(-jnp.imag(c_mat), -jnp.imag(c_mat))],
        axis=1)
    n_steps = max(1, (n_chunks - 1).bit_length())
    powers = []
    cur = pw[:, chunk, :]
    for _ in range(n_steps):
        powers.append(cur)
        cur = cur * cur
    lam_pow = jnp.stack(powers, axis=1)
    a1 = halves(jnp.real(lam_pow), jnp.real(lam_pow))
    a2 = halves(-jnp.imag(lam_pow), jnp.imag(lam_pow))
    return time_terms, chan_terms, a1, a2


def _ssm_kernel(u_ref, time_ref, chan_ref, a1_ref, a2_ref, perm_ref, perm_t_ref, y_ref,
                u8_ref, y8_ref, toep_ref, *, n_chunks, n_steps, group):
    seq = u_ref.shape[0]
    tiles = seq // SSM_FOLD
    lc = toep_ref.shape[0]
    per_chunk = (lc // group) // SSM_FOLD
    p2 = a1_ref.shape[2]
    group_shift = group.bit_length() - 1
    nt = (((1,), (1,)), ((), ()))

    folded = jnp.concatenate(
        [u_ref[pl.ds(s, tiles, stride=SSM_FOLD), :] for s in range(SSM_FOLD)], axis=1)
    regrouped = jnp.dot(folded.astype(BF16), perm_ref[...], preferred_element_type=F32)
    n_groups = LANES // group
    for g in range(n_groups):
        u8_ref[g] = regrouped[:, g * LANES:(g + 1) * LANES]

    for g in range(n_groups):
        def table(t1, t2, v1, v2):
            full = (time_ref[g, t1][:, None, :] * chan_ref[g, v1][None, :, :]
                    + time_ref[g, t2][:, None, :] * chan_ref[g, v2][None, :, :])
            return full.reshape(lc, p2).astype(BF16)

        src = table(0, 1, 0, 1)
        to_state = table(2, 3, 0, 1)
        dst = table(4, 5, 2, 3)
        from_state = table(6, 7, 2, 3)
        cb = min(256, lc)
        for j in range(lc // cb):
            blk = lax.dot_general(src, dst[j * cb:(j + 1) * cb], nt, preferred_element_type=F32)
            s_idx = lax.broadcasted_iota(I32, (lc, cb), 0) >> group_shift
            t_idx = (lax.broadcasted_iota(I32, (lc, cb), 1) + j * cb) >> group_shift
            toep_ref[:, j * cb:(j + 1) * cb] = jnp.where(s_idx <= t_idx, blk, 0.0).astype(BF16)
        u = jnp.concatenate(
            [u8_ref[g, pl.ds(tau, n_chunks, stride=per_chunk), :] for tau in range(per_chunk)],
            axis=1).astype(BF16)
        y = jnp.dot(u, toep_ref[...], preferred_element_type=F32)
        z = jnp.dot(u, to_state, preferred_element_type=F32)
        n = lax.broadcasted_iota(I32, z.shape, 0)
        x = jnp.where(n >= 1, pltpu.roll(z, 1, 0), 0.0)
        for k in range(n_steps):
            sh = 1 << k
            xs = jnp.where(n >= sh, pltpu.roll(x, sh, 0), 0.0)
            x = (x + a1_ref[g, k:k + 1, :] * xs
                 + a2_ref[g, k:k + 1, :] * pltpu.roll(xs, p2 // 2, 1))
        y = y + lax.dot_general(x.astype(BF16), from_state, nt, preferred_element_type=F32)
        for tau in range(per_chunk):
            y8_ref[g, pl.ds(tau, n_chunks, stride=per_chunk), :] = y[:, tau * LANES:(tau + 1) * LANES]

    y8 = jnp.concatenate([y8_ref[g] for g in range(n_groups)], axis=1)
    unfolded = jnp.dot(y8.astype(BF16), perm_t_ref[...], preferred_element_type=F32)
    for s in range(SSM_FOLD):
        y_ref[pl.ds(s, tiles, stride=SSM_FOLD), :] = unfolded[:, s * LANES:(s + 1) * LANES]


def _ssm(u, bsz, seq, lam_re, lam_im, log_dt, b_re, b_im, c_re, c_im):
    g, p = lam_re.shape
    c = b_re.shape[-1]
    chunk = min(SSM_CHUNK, seq)
    n_chunks = seq // chunk
    assert n_chunks & (n_chunks - 1) == 0 and c & (c - 1) == 0, (n_chunks, c)
    assert SSM_FOLD * c == LANES and chunk % SSM_FOLD == 0
    lc = chunk * c
    per_block = LANES // c
    time_terms, chan_terms, a1, a2 = _ssm_params(
        lam_re, lam_im, log_dt, b_re, b_im, c_re, c_im, chunk, n_chunks)
    n_steps = a1.shape[1]
    src_idx = jnp.arange(SSM_FOLD * LANES, dtype=I32)
    s_of, g_of, c_of = src_idx // LANES, (src_idx % LANES) // c, src_idx % c
    dst_idx = g_of * LANES + s_of * c + c_of
    perm = (dst_idx[:, None] == src_idx[None, :]).astype(BF16)
    block = lambda cb, b: (cb, 0, 0)
    fixed = lambda cb, b: (0, 0)
    y = pl.pallas_call(
        functools.partial(_ssm_kernel, n_chunks=n_chunks, n_steps=n_steps, group=c),
        grid=(g // per_block, bsz),
        in_specs=[
            pl.BlockSpec((None, seq, LANES), lambda cb, b: (b, 0, cb)),
            pl.BlockSpec((per_block,) + time_terms.shape[1:], lambda cb, b: (cb, 0, 0, 0)),
            pl.BlockSpec((per_block,) + chan_terms.shape[1:], lambda cb, b: (cb, 0, 0, 0)),
            pl.BlockSpec((per_block, n_steps, 2 * p), block),
            pl.BlockSpec((per_block, n_steps, 2 * p), block),
            pl.BlockSpec((SSM_FOLD * LANES, SSM_FOLD * LANES), fixed),
            pl.BlockSpec((SSM_FOLD * LANES, SSM_FOLD * LANES), fixed),
        ],
        out_specs=pl.BlockSpec((None, seq, LANES), lambda cb, b: (b, 0, cb)),
        out_shape=jax.ShapeDtypeStruct((bsz, seq, g * c), F32),
        scratch_shapes=[
            pltpu.VMEM((per_block, seq // SSM_FOLD, LANES), F32),
            pltpu.VMEM((per_block, seq // SSM_FOLD, LANES), F32),
            pltpu.VMEM((lc, lc), BF16),
        ],
        compiler_params=pltpu.CompilerParams(
            dimension_semantics=("arbitrary", "arbitrary"), vmem_limit_bytes=56 * MIB),
        name="s5_chunked_scan",
    )(u.reshape(bsz, seq, g * c), time_terms, chan_terms, a1, a2, perm, perm.T)
    return y.reshape(bsz * seq, g * c)


def _post_kernel(x_ref, ysb_ref, yss_ref, u_ref, d_ref, wglu_ref, gsb_ref, gssm_ref, wout_ref,
                 ln2_ref, wr_ref, br_ref, tri_ref, x1_ref, xn_ref, idx_ref, gate_ref, rank_ref,
                 count_ref, running_ref, wglu_bf, wout_bf, *, sb, n_exp):
    @pl.when(pl.program_id(0) == 0)
    def _():
        running_ref[...] = jnp.zeros_like(running_ref)
        wglu_bf[...] = wglu_ref[...].astype(BF16)
        wout_bf[...] = wout_ref[...].astype(BF16)

    u = u_ref[...].astype(F32)
    y = yss_ref[...].astype(F32) + d_ref[...] * u
    y = y * (0.5 * (1.0 + jnp.tanh(math.sqrt(2.0 / math.pi) * (y + 0.044715 * (y * y * y)))))
    ab = jnp.dot(y.astype(BF16), wglu_bf[...], preferred_element_type=F32)
    w = ab.shape[1] // 2
    y_ssm = ab[:, :w] * (1.0 / (1.0 + jnp.exp(-ab[:, w:])))
    m_sb = _rms(ysb_ref[...].astype(F32), gsb_ref[...])
    m_ssm = _rms(y_ssm, gssm_ref[...])
    x1 = (x_ref[...]
          + jnp.dot(m_sb.astype(BF16), wout_bf[:sb, :], preferred_element_type=F32)
          + jnp.dot(m_ssm.astype(BF16), wout_bf[sb:, :], preferred_element_type=F32))
    x1_ref[...] = x1
    xn = _rms(x1, ln2_ref[...])
    xn_ref[...] = xn
    def split(a):
        hi = a.astype(BF16)
        return hi, (a - hi.astype(F32)).astype(BF16)

    xn_hi, xn_lo = split(xn)
    wr_hi, wr_lo = split(wr_ref[...])
    logits = (jnp.dot(xn_hi, wr_hi, preferred_element_type=F32)
              + jnp.dot(xn_hi, wr_lo, preferred_element_type=F32)
              + jnp.dot(xn_lo, wr_hi, preferred_element_type=F32)) + br_ref[...]
    lane = lax.broadcasted_iota(I32, logits.shape, 1).astype(F32)
    out_lane = lax.broadcasted_iota(I32, (logits.shape[0], LANES), 1)
    idx_out = jnp.zeros((logits.shape[0], LANES), F32)
    val_out = jnp.zeros((logits.shape[0], LANES), F32)
    top = None
    denom = None
    work = logits
    chosen = []
    for k in range(TOP_K):
        m = jnp.max(work, axis=-1, keepdims=True)
        sel = jnp.min(jnp.where(work == m, lane, float(n_exp)), axis=-1, keepdims=True)
        hit = lane == sel
        chosen.append(hit)
        work = jnp.where(hit, -jnp.inf, work)
        if k == 0:
            top = m
        e = jnp.exp(m - top)
        denom = e if denom is None else denom + e
        idx_out = jnp.where(out_lane == k, sel, idx_out)
        val_out = jnp.where(out_lane == k, e, val_out)
    idx_ref[...] = idx_out[:, :TOP_K].astype(I32)
    gate_ref[...] = (val_out / denom)[:, :TOP_K]
    member = functools.reduce(jnp.logical_or, chosen).astype(BF16)
    before = running_ref[...] + jnp.dot(tri_ref[...], member, preferred_element_type=F32)
    rank_out = jnp.zeros((logits.shape[0], LANES), F32)
    for k in range(TOP_K):
        rk = jnp.sum(jnp.where(chosen[k], before, 0.0), axis=-1, keepdims=True)
        rank_out = jnp.where(out_lane == k, rk, rank_out)
    rank_ref[...] = rank_out[:, :TOP_K].astype(I32)
    running = running_ref[...] + jnp.sum(member.astype(F32), axis=0, keepdims=True)
    running_ref[...] = running
    count_ref[...] = running


def _post(x2, y_sb, y_ss, u, ssm_d, w_glu, g_sb, g_ssm, w_out, ln2_g, w_router, b_router):
    t, d = x2.shape
    sb = y_sb.shape[1]
    w = y_ss.shape[1]
    n_exp = w_router.shape[1]
    tm = min(TOKEN_TILE, t)
    row = lambda i: (i, 0)
    fixed = lambda i: (0, 0)
    earlier = (lax.broadcasted_iota(I32, (tm, tm), 1)
               < lax.broadcasted_iota(I32, (tm, tm), 0)).astype(BF16)
    return pl.pallas_call(
        functools.partial(_post_kernel, sb=sb, n_exp=n_exp),
        grid=(t // tm,),
        in_specs=[
            pl.BlockSpec((tm, d), row),
            pl.BlockSpec((tm, sb), row),
            pl.BlockSpec((tm, w), row),
            pl.BlockSpec((tm, w), row),
            pl.BlockSpec((1, w), fixed),
            pl.BlockSpec((w, 2 * w), fixed),
            pl.BlockSpec((1, sb), fixed),
            pl.BlockSpec((1, w), fixed),
            pl.BlockSpec((sb + w, d), fixed),
            pl.BlockSpec((1, d), fixed),
            pl.BlockSpec((d, n_exp), fixed),
            pl.BlockSpec((1, n_exp), fixed),
            pl.BlockSpec((tm, tm), fixed),
        ],
        out_specs=[
            pl.BlockSpec((tm, d), row),
            pl.BlockSpec((tm, d), row),
            pl.BlockSpec((tm, TOP_K), row),
            pl.BlockSpec((tm, TOP_K), row),
            pl.BlockSpec((tm, TOP_K), row),
            pl.BlockSpec((1, n_exp), fixed),
        ],
        out_shape=[
            jax.ShapeDtypeStruct((t, d), F32),
            jax.ShapeDtypeStruct((t, d), F32),
            jax.ShapeDtypeStruct((t, TOP_K), I32),
            jax.ShapeDtypeStruct((t, TOP_K), F32),
            jax.ShapeDtypeStruct((t, TOP_K), I32),
            jax.ShapeDtypeStruct((1, n_exp), F32),
        ],
        scratch_shapes=[pltpu.VMEM((1, n_exp), F32), pltpu.VMEM((w, 2 * w), BF16),
                        pltpu.VMEM((sb + w, d), BF16)],
        compiler_params=pltpu.CompilerParams(
            dimension_semantics=("arbitrary",), vmem_limit_bytes=48 * MIB),
        name="post_mixer_router",
    )(x2, y_sb, y_ss, u, ssm_d.reshape(1, w), w_glu, g_sb.reshape(1, sb),
      g_ssm.reshape(1, w), w_out, ln2_g.reshape(1, d), w_router,
      b_router.reshape(1, n_exp), earlier)


def _plan(top_idx, rank, counts, rows_per_block):
    t, k = top_idx.shape
    n_exp = counts.shape[-1]
    counts = counts.reshape(n_exp).astype(I32)
    padded = ((counts + rows_per_block - 1) // rows_per_block) * rows_per_block
    pad_ends = jnp.cumsum(padded)
    pad_starts = pad_ends - padded
    start_of = jnp.sum(jnp.where(top_idx[..., None] == jnp.arange(n_exp, dtype=I32),
                                 pad_starts, 0), axis=-1)
    dest = (start_of + rank).astype(I32)
    n_rows = t * k + n_exp * rows_per_block
    n_blocks = n_rows // rows_per_block
    block_start = jnp.arange(n_blocks, dtype=I32) * rows_per_block
    block_expert = jnp.minimum(
        jnp.sum((pad_ends[None, :] <= block_start[:, None]).astype(I32), axis=1), n_exp - 1)
    n_used = (pad_ends[-1] // rows_per_block).astype(I32).reshape(1)
    fill_start = jnp.concatenate([jnp.maximum(pad_ends - rows_per_block, 0).astype(I32), n_used])
    experts = jnp.arange(n_exp, dtype=I32)
    nonempty = counts > 0
    order = jnp.cumsum(nonempty.astype(I32)) - nonempty.astype(I32)
    later_ne = jnp.where((experts[None, :] > experts[:, None]) & nonempty[None, :],
                         experts[None, :], n_exp)
    next_tbl = jnp.min(later_ne, axis=1)
    onehot = block_expert[:, None] == experts[None, :]
    w_slot = jnp.sum(jnp.where(onehot, order & 1, 0), axis=1).astype(I32)
    w_next = jnp.sum(jnp.where(onehot, next_tbl, 0), axis=1).astype(I32)
    return dest, (block_expert, n_used, w_slot, w_next), fill_start, n_blocks


def _dispatch_kernel(fill_ref, dest_hbm, xn_ref, xs_hbm, idx_smem, zeros_ref, idx_sem, row_sem,
                     fill_sem, *, tile, n_exp, fill_rows):
    i = pl.program_id(0)
    n = pl.num_programs(0)
    per_tile = TOP_K * tile

    def idx_copy(b):
        src = dest_hbm.at[pl.ds(pl.multiple_of(b * per_tile, per_tile), per_tile)]
        dst = idx_smem.at[pl.ds(pl.multiple_of((b % 2) * per_tile, per_tile), per_tile)]
        return pltpu.make_async_copy(src, dst, idx_sem.at[b % 2])

    @pl.when(i == 0)
    def _():
        idx_copy(0).start()
        zeros_ref[...] = jnp.zeros_like(zeros_ref)
        for e in range(n_exp):
            start = pl.multiple_of(fill_ref[e], 8)
            pltpu.make_async_copy(zeros_ref, xs_hbm.at[pl.ds(start, fill_rows), 0, :],
                                  fill_sem).start()
        for e in range(n_exp):
            pltpu.make_async_copy(zeros_ref, xs_hbm.at[pl.ds(0, fill_rows), 0, :],
                                  fill_sem).wait()
        n_blocks = xs_hbm.shape[0] // fill_rows

        def fill_unused(b, carry):
            cp = pltpu.make_async_copy(
                zeros_ref, xs_hbm.at[pl.ds(pl.multiple_of(b * fill_rows, fill_rows), fill_rows), 0, :],
                fill_sem)
            cp.start()
            cp.wait()
            return carry

        lax.fori_loop(fill_ref[n_exp], n_blocks, fill_unused, 0)

    idx_copy(i).wait()

    @pl.when(i + 1 < n)
    def _():
        idx_copy(i + 1).start()

    base = pl.multiple_of((i % 2) * per_tile, per_tile)
    quarter = tile // TOP_K
    for part in range(TOP_K):
        def issue(g, carry, part=part):
            group = part * (quarter // SUBLANES) + g
            for s in range(SUBLANES):
                for k in range(TOP_K):
                    dst_row = idx_smem[base + (group * SUBLANES + s) * TOP_K + k]
                    pltpu.make_async_copy(xn_ref.at[group, pl.ds(s, 1), :], xs_hbm.at[dst_row],
                                          row_sem.at[part]).start(priority=k % 2)
            return carry

        lax.fori_loop(0, quarter // SUBLANES, issue, 0)
    for part in range(TOP_K):
        pltpu.make_async_copy(xn_ref, xn_ref, row_sem.at[part]).wait()


def _dispatch(xn, dest, fill_start, n_rows):
    t, d = xn.shape
    tile = min(DISPATCH_TILE, t)
    n_exp = fill_start.shape[0] - 1
    grid_spec = pltpu.PrefetchScalarGridSpec(
        num_scalar_prefetch=1,
        grid=(t // tile,),
        in_specs=[
            pl.BlockSpec(memory_space=pl.ANY),
            pl.BlockSpec((tile // SUBLANES, SUBLANES, d), lambda i, fs: (i, 0, 0)),
        ],
        out_specs=pl.BlockSpec(memory_space=pl.ANY),
        scratch_shapes=[
            pltpu.SMEM((2 * TOP_K * tile,), I32),
            pltpu.VMEM((EXPERT_ROWS, d), F32),
            pltpu.SemaphoreType.DMA((2,)),
            pltpu.SemaphoreType.DMA((TOP_K,)),
            pltpu.SemaphoreType.DMA,
        ],
    )
    return pl.pallas_call(
        functools.partial(_dispatch_kernel, tile=tile, n_exp=n_exp, fill_rows=EXPERT_ROWS),
        grid_spec=grid_spec,
        out_shape=jax.ShapeDtypeStruct((n_rows, 1, d), F32),
        compiler_params=pltpu.CompilerParams(
            dimension_semantics=("arbitrary",), vmem_limit_bytes=40 * MIB),
        name="moe_dispatch",
    )(fill_start, dest.reshape(t * TOP_K), xn.reshape(t // SUBLANES, SUBLANES, d))


def _expert_kernel(be_ref, nused_ref, wslot_ref, wnext_ref, xs_hbm, wg_hbm, bg_ref, wu_hbm, bu_ref,
                   wd_hbm, bd_ref, y_hbm, xbuf, ybuf, x_sem, y_sem, wg32, wu32, wd32, w_sem,
                   wg_bf, wu_bf, wd_bf, *, rows, n_exp):
    i = pl.program_id(0)
    n = nused_ref[0]

    def w_copies(e, slot):
        return [pltpu.make_async_copy(src.at[e], dst.at[slot], w_sem.at[slot])
                for src, dst in ((wg_hbm, wg32), (wu_hbm, wu32), (wd_hbm, wd32))]

    def x_copy(b):
        return pltpu.make_async_copy(
            xs_hbm.at[pl.ds(pl.multiple_of(b * rows, rows), rows), 0, :], xbuf.at[b % 2],
            x_sem.at[b % 2])

    def y_copy(b):
        return pltpu.make_async_copy(
            ybuf.at[b % 2], y_hbm.at[pl.ds(pl.multiple_of(b * rows, rows), rows), 0, :],
            y_sem.at[b % 2])

    @pl.when(i < n)
    def _():
        @pl.when(i == 0)
        def _():
            x_copy(0).start()
            for cp in w_copies(be_ref[0], wslot_ref[0]):
                cp.start()

        @pl.when(i + 1 < n)
        def _():
            x_copy(i + 1).start()

        changed = jnp.logical_or(i == 0, be_ref[i] != be_ref[jnp.maximum(i - 1, 0)])

        @pl.when(changed)
        def _():
            slot = wslot_ref[i]
            for cp in w_copies(be_ref[i], slot):
                cp.wait()

            @pl.when(wnext_ref[i] < n_exp)
            def _():
                for cp in w_copies(wnext_ref[i], 1 - slot):
                    cp.start()
            wg_bf[...] = wg32[slot].astype(BF16)
            wu_bf[...] = wu32[slot].astype(BF16)
            wd_bf[...] = wd32[slot].astype(BF16)

        x_copy(i).wait()
        x = xbuf[i % 2].astype(BF16)
        gate = jnp.dot(x, wg_bf[...], preferred_element_type=F32) + bg_ref[...]
        up = jnp.dot(x, wu_bf[...], preferred_element_type=F32) + bu_ref[...]
        gate = jnp.minimum(gate, SWIGLU_LIMIT)
        up = jnp.clip(up, -SWIGLU_LIMIT, SWIGLU_LIMIT)
        glu = gate * (1.0 / (1.0 + jnp.exp(-SWIGLU_ALPHA * gate)))
        hidden = ((up + 1.0) * glu).astype(BF16)
        y = jnp.dot(hidden, wd_bf[...], preferred_element_type=F32) + bd_ref[...]

        @pl.when(i >= 2)
        def _():
            y_copy(i - 2).wait()
        ybuf[i % 2] = y
        y_copy(i).start()

        @pl.when(i == n - 1)
        def _():
            y_copy(i).wait()

            @pl.when(i >= 1)
            def _():
                y_copy(i - 1).wait()

    @pl.when(i >= n)
    def _():
        ybuf[i % 2] = jnp.zeros((rows, ybuf.shape[2]), F32)
        y_copy(i).start()
        y_copy(i).wait()


def _experts(x_sorted, block_plan, n_blocks, w_gate, b_gate, w_up, b_up, w_down, b_down):
    n_rows, _, d = x_sorted.shape
    n_exp, _, f = w_gate.shape
    rows = EXPERT_ROWS
    bmap = lambda i, be, nu, ws, wn: (be[i], 0, 0)
    hbm = pl.BlockSpec(memory_space=pl.ANY)
    grid_spec = pltpu.PrefetchScalarGridSpec(
        num_scalar_prefetch=4,
        grid=(n_blocks,),
        in_specs=[
            hbm,
            hbm, pl.BlockSpec((None, 1, f), bmap),
            hbm, pl.BlockSpec((None, 1, f), bmap),
            hbm, pl.BlockSpec((None, 1, d), bmap),
        ],
        out_specs=hbm,
        scratch_shapes=[
            pltpu.VMEM((2, rows, d), F32),
            pltpu.VMEM((2, rows, d), F32),
            pltpu.SemaphoreType.DMA((2,)),
            pltpu.SemaphoreType.DMA((2,)),
            pltpu.VMEM((2, d, f), F32),
            pltpu.VMEM((2, d, f), F32),
            pltpu.VMEM((2, f, d), F32),
            pltpu.SemaphoreType.DMA((2,)),
            pltpu.VMEM((d, f), BF16),
            pltpu.VMEM((d, f), BF16),
            pltpu.VMEM((f, d), BF16),
        ],
    )
    return pl.pallas_call(
        functools.partial(_expert_kernel, rows=rows, n_exp=n_exp),
        grid_spec=grid_spec,
        out_shape=jax.ShapeDtypeStruct((n_rows, 1, d), F32),
        compiler_params=pltpu.CompilerParams(
            dimension_semantics=("arbitrary",), vmem_limit_bytes=52 * MIB),
        name="moe_experts",
    )(*block_plan, x_sorted,
      w_gate, b_gate.reshape(n_exp, 1, f), w_up, b_up.reshape(n_exp, 1, f),
      w_down, b_down.reshape(n_exp, 1, d))


def _combine_kernel(dest_hbm, y_hbm, x1_ref, gate_ref, lnf_ref, o_ref, idx_smem, ybuf,
                    idx_sem, row_sem, *, tile):
    i = pl.program_id(0)
    n = pl.num_programs(0)
    per_tile = TOP_K * tile

    def slot_base(b):
        return pl.multiple_of((b % 3) * per_tile, per_tile)

    def idx_copy(b):
        src = dest_hbm.at[pl.ds(pl.multiple_of(b * per_tile, per_tile), per_tile)]
        return pltpu.make_async_copy(src, idx_smem.at[pl.ds(slot_base(b), per_tile)],
                                     idx_sem.at[b % 3])

    def gather(b):
        base = slot_base(b)
        buf = ybuf.at[b % 2]
        sem = row_sem.at[b % 2]

        def issue(g, carry):
            for s in range(SUBLANES):
                pltpu.make_async_copy(y_hbm.at[idx_smem[base + g * SUBLANES + s]],
                                      buf.at[g, pl.ds(s, 1), :], sem).start(priority=s % 2)
            return carry

        lax.fori_loop(0, per_tile // SUBLANES, issue, 0)

    @pl.when(i == 0)
    def _():
        idx_copy(0).start()
        idx_copy(0).wait()
        gather(0)

        @pl.when(n > 1)
        def _():
            idx_copy(1).start()

    @pl.when(i + 1 < n)
    def _():
        idx_copy(i + 1).wait()

        @pl.when(i + 2 < n)
        def _():
            idx_copy(i + 2).start()
        gather(i + 1)

    cur = i % 2
    pltpu.make_async_copy(ybuf.at[1 - cur], ybuf.at[cur], row_sem.at[cur]).wait()
    gates = gate_ref[...]
    x = x1_ref[...]
    groups = tile // SUBLANES
    for k in range(TOP_K):
        yk = ybuf[cur, k * groups:(k + 1) * groups].reshape(tile, x.shape[1])
        x = x + gates[:, k:k + 1] * yk
    o_ref[...] = _rms(x, lnf_ref[...])


def _combine(dest, y_rows, x1, gates, ln_f_g):
    t, d = x1.shape
    tile = min(COMBINE_TILE, t)
    n_tiles = t // tile
    dest_tiles = dest.reshape(n_tiles, tile, TOP_K).transpose(0, 2, 1).reshape(n_tiles * TOP_K * tile)
    return pl.pallas_call(
        functools.partial(_combine_kernel, tile=tile),
        grid=(n_tiles,),
        in_specs=[
            pl.BlockSpec(memory_space=pl.ANY),
            pl.BlockSpec(memory_space=pl.ANY),
            pl.BlockSpec((tile, d), lambda i: (i, 0)),
            pl.BlockSpec((tile, TOP_K), lambda i: (i, 0)),
            pl.BlockSpec((1, d), lambda i: (0, 0)),
        ],
        out_specs=pl.BlockSpec((tile, d), lambda i: (i, 0)),
        out_shape=jax.ShapeDtypeStruct((t, d), F32),
        scratch_shapes=[
            pltpu.SMEM((3 * TOP_K * tile,), I32),
            pltpu.VMEM((2, TOP_K * tile // SUBLANES, SUBLANES, d), F32),
            pltpu.SemaphoreType.DMA((3,)),
            pltpu.SemaphoreType.DMA((2,)),
        ],
        compiler_params=pltpu.CompilerParams(
            dimension_semantics=("arbitrary",), vmem_limit_bytes=40 * MIB),
        name="moe_combine",
    )(dest_tiles, y_rows, x1, gates, ln_f_g.reshape(1, d))


def kernel(x, ln1_g, w_in, lam_re, lam_im, log_dt, ssm_b_re, ssm_b_im, ssm_c_re, ssm_c_im,
           ssm_d, w_glu, g_sb, g_ssm, w_out, ln2_g, w_router, b_router, w_gate, b_gate,
           w_up, b_up, w_down, b_down, ln_f_g):
    bsz, seq, d = x.shape
    assert ln1_g.shape[0] == 1, "depth-1 block only"
    ssm_w = ssm_d.shape[1]
    sb = g_sb.shape[1]
    x2 = x.reshape(bsz * seq, d)
    q, k, v, u = _in_proj(x2, ln1_g[0], w_in[0], sb, ssm_w)
    y_sb = _attention(q, k, v, bsz, seq)
    y_ss = _ssm(u, bsz, seq, lam_re[0], lam_im[0], log_dt[0], ssm_b_re[0], ssm_b_im[0],
                ssm_c_re[0], ssm_c_im[0])
    x1, xn, top_idx, gates, rank, counts = _post(
        x2, y_sb, y_ss, u, ssm_d[0], w_glu[0], g_sb[0], g_ssm[0], w_out[0], ln2_g[0],
        w_router[0], b_router[0])
    dest, block_plan, fill_start, n_blocks = _plan(top_idx, rank, counts, EXPERT_ROWS)
    x_sorted = _dispatch(xn, dest, fill_start, n_blocks * EXPERT_ROWS)
    y_rows = _experts(x_sorted, block_plan, n_blocks, w_gate[0], b_gate[0],
                      w_up[0], b_up[0], w_down[0], b_down[0])
    out = _combine(dest, y_rows, x1, gates, ln_f_g)
    return out.reshape(bsz, seq, d)
```

```python
import functools
import math

import jax
import jax.numpy as jnp
from jax import lax
from jax.experimental import pallas as pl
from jax.experimental.pallas import tpu as pltpu

F32 = jnp.float32
BF16 = jnp.bfloat16
I32 = jnp.int32

EPS = 1e-5
SB_HEAD_DIM = 64
SSM_GROUP = 16
SSM_STATE = 64
TOP_K = 4
SWIGLU_LIMIT = 7.0
SWIGLU_ALPHA = 1.702

LANES = 128
SUBLANES = 8
HEADS_PER_BLOCK = LANES // SB_HEAD_DIM
ATTN_BLOCK = 256
ATTN_EXP_FLOOR = 160.0
SSM_CHUNK = 64
SSM_FOLD = 8
TOKEN_TILE = 512
EXPERT_ROWS = 256
COMBINE_TILE = 256
MIB = 1024 * 1024
LOG2E = 1.4426950408889634


def _rms(x, g):
    return x * lax.rsqrt(jnp.mean(x * x, axis=-1, keepdims=True) + EPS) * g


def _in_proj_kernel(x_ref, g_ref, w_ref, q_ref, k_ref, v_ref, u_ref, w_bf, *, sb, scale):
    @pl.when(pl.program_id(0) == 0)
    def _():
        w_bf[...] = w_ref[...].astype(BF16)

    h = _rms(x_ref[...], g_ref[...])
    proj = jnp.dot(h.astype(BF16), w_bf[...], preferred_element_type=F32)
    k_ref[...] = proj[:, sb:2 * sb].astype(BF16)
    u_ref[...] = proj[:, 3 * sb:]
    lane = lax.broadcasted_iota(I32, (1, LANES), 1)
    q = (proj[:, :sb] * scale).astype(BF16)
    v = proj[:, 2 * sb:3 * sb].astype(BF16)
    zero = jnp.zeros((), BF16)
    for pair in range(sb // LANES):
        cols = slice(pair * LANES, (pair + 1) * LANES)
        for head in range(HEADS_PER_BLOCK):
            own = (lane // SB_HEAD_DIM) == head
            out = slice((pair * HEADS_PER_BLOCK + head) * LANES,
                        (pair * HEADS_PER_BLOCK + head + 1) * LANES)
            q_ref[:, out] = jnp.where(own, q[:, cols], zero)
            v_ref[:, out] = jnp.where(own, v[:, cols], zero)


def _in_proj(x2, ln1_g, w_in, sb, ssm_w):
    t, d = x2.shape
    n_in = w_in.shape[1]
    tm = min(TOKEN_TILE, t)
    scale = LOG2E / math.sqrt(SB_HEAD_DIM)
    return pl.pallas_call(
        functools.partial(_in_proj_kernel, sb=sb, scale=scale),
        grid=(t // tm,),
        in_specs=[
            pl.BlockSpec((tm, d), lambda i: (i, 0)),
            pl.BlockSpec((1, d), lambda i: (0, 0)),
            pl.BlockSpec((d, n_in), lambda i: (0, 0)),
        ],
        out_specs=[
            pl.BlockSpec((tm, HEADS_PER_BLOCK * sb), lambda i: (i, 0)),
            pl.BlockSpec((tm, sb), lambda i: (i, 0)),
            pl.BlockSpec((tm, HEADS_PER_BLOCK * sb), lambda i: (i, 0)),
            pl.BlockSpec((tm, ssm_w), lambda i: (i, 0)),
        ],
        out_shape=[
            jax.ShapeDtypeStruct((t, HEADS_PER_BLOCK * sb), BF16),
            jax.ShapeDtypeStruct((t, sb), BF16),
            jax.ShapeDtypeStruct((t, HEADS_PER_BLOCK * sb), BF16),
            jax.ShapeDtypeStruct((t, ssm_w), F32),
        ],
        scratch_shapes=[pltpu.VMEM((d, n_in), BF16)],
        compiler_params=pltpu.CompilerParams(
            dimension_semantics=("arbitrary",), vmem_limit_bytes=48 * MIB),
        name="in_proj",
    )(x2, ln1_g.reshape(1, d), w_in)


def _attn_kernel(ti_ref, tj_ref, q_ref, k_ref, v_ref, tri_ref, mask_ref, o_ref,
                 dbuf, spbuf, lbuf, rsbuf, acc_ref, r_ref, *, blk, n_blk, n_items):
    sign_bit = jnp.uint32(0x80000000)
    last = n_items - 1
    ALL, CAUSAL, NONE = 0, 1, 2

    for ref in (dbuf, spbuf, lbuf, rsbuf, acc_ref, r_ref):
        ref[...] = jnp.zeros_like(ref)

    def both_heads(ref, start):
        rows = pl.ds(pl.multiple_of(start, blk), blk)
        return jnp.concatenate([ref[rows, h * LANES:(h + 1) * LANES]
                                for h in range(HEADS_PER_BLOCK)], axis=0)

    def item(idx):
        c = jnp.clip(idx, 0, last)
        return ti_ref[c], tj_ref[c]

    half = n_blk // 2
    stream_end = (half * (half + 1) // 2, n_items)

    def step(state, x):
        cur, p1, p2, out_blk = list(state[:2]), state[2], state[3], list(state[4:])
        slot, other = x, 1 - x
        c1 = jnp.where(cur[x] < stream_end[x], cur[x], n_items)
        c3 = p2
        i1, j1 = item(c1)
        ks = k_ref[pl.ds(pl.multiple_of(j1 * blk, blk), blk), :]
        w = lax.dot_general(both_heads(q_ref, i1 * blk), ks, (((1,), (1,)), ((), ())),
                            preferred_element_type=F32)
        i3, j3 = item(c3)
        valid = c3 <= last
        first = jnp.logical_and(i3 == j3, valid)
        mask3 = mask_ref[jnp.where(valid, (i3 == j3).astype(I32), NONE)]
        i3 = jnp.where(valid, i3, out_blk[x])
        r_prev = jnp.where(first, 0.0, r_ref[x])
        r_new = r_prev + rsbuf[other]
        r_ref[x] = r_new
        done = jnp.logical_and(valid, jnp.min(r_new) >= ATTN_EXP_FLOOR)
        next_block = lax.shift_right_logical((i3 + 1) * (i3 + 2), 1)
        advanced = jnp.where(done, jnp.maximum(cur[x] + 1, next_block), cur[x] + 1)
        cur[x] = jnp.minimum(advanced, stream_end[x])
        out_blk[x] = i3
        r_wide = jnp.concatenate([r_prev] * (blk // LANES), axis=1)
        a = jnp.exp2(dbuf[slot] - (lbuf[other] + r_wide))
        ab = a.astype(BF16) * jnp.concatenate([mask3, mask3], axis=0)
        a_cat = jnp.concatenate([ab[:blk], ab[blk:]], axis=1)
        acc = jnp.where(first, 0.0, acc_ref[x]) + jnp.dot(
            a_cat, both_heads(v_ref, j3 * blk), preferred_element_type=F32)
        acc_ref[x] = acc
        o_ref[pl.ds(pl.multiple_of(i3 * blk, blk), blk), :] = acc.astype(o_ref.dtype)
        sums = jnp.dot(spbuf[other], tri_ref[...], preferred_element_type=F32)
        lbuf[slot] = sums[:, :blk]
        rsbuf[slot] = sums[:, blk:]
        mask1 = mask_ref[(i1 == j1).astype(I32)]
        neg_abs = lax.bitcast_convert_type(
            lax.bitcast_convert_type(w, jnp.uint32) | sign_bit, F32)
        sp2 = jnp.maximum(w, 0.0) + jnp.log(1.0 + jnp.exp2(neg_abs)) * LOG2E
        dbuf[slot] = w - sp2
        spbuf[slot] = sp2.astype(BF16) * jnp.concatenate([mask1, mask1], axis=0)
        return cur[0], cur[1], c1, p1, out_blk[0], out_blk[1]

    def in_flight(state):
        cur0, cur1, p1, p2 = state[:4]
        return functools.reduce(jnp.logical_or, [cur0 < stream_end[0], cur1 < stream_end[1],
                                                 p1 <= last, p2 <= last])

    def four_steps(state):
        for x in (0, 1, 0, 1):
            state = step(state, x)
        return state

    none = jnp.int32(n_items)
    lax.while_loop(in_flight, four_steps,
                   (jnp.int32(0), jnp.int32(stream_end[0]), none, none,
                    jnp.int32(0), jnp.int32(half)))


def _attention(q, k, v, bsz, seq):
    sb = k.shape[-1]
    blk = min(ATTN_BLOCK, seq)
    n_blk = seq // blk
    n_pairs = sb // LANES
    q3, k3, v3 = (a.reshape(bsz, seq, a.shape[-1]) for a in (q, k, v))
    per_head = HEADS_PER_BLOCK * LANES
    jj = lax.broadcasted_iota(I32, (blk, blk), 0)
    ss = lax.broadcasted_iota(I32, (blk, blk), 1)
    tri = jnp.concatenate([(jj > ss).astype(BF16), jnp.ones((blk, LANES), BF16)], axis=1)
    masks = jnp.stack([jnp.ones((blk, blk), BF16), (ss < jj).astype(BF16),
                       jnp.zeros((blk, blk), BF16)])
    items = [(i, j) for i in range(n_blk) for j in range(i, -1, -1)]
    item_i = jnp.asarray([i for i, _ in items], I32)
    item_j = jnp.asarray([j for _, j in items], I32)
    whole = lambda b, p, ti, tj: (b, 0, p)
    grid_spec = pltpu.PrefetchScalarGridSpec(
        num_scalar_prefetch=2,
        grid=(bsz, n_pairs),
        in_specs=[
            pl.BlockSpec((None, seq, per_head), whole),
            pl.BlockSpec((None, seq, LANES), whole),
            pl.BlockSpec((None, seq, per_head), whole),
            pl.BlockSpec((blk, blk + LANES), lambda b, p, ti, tj: (0, 0)),
            pl.BlockSpec((3, blk, blk), lambda b, p, ti, tj: (0, 0, 0)),
        ],
        out_specs=pl.BlockSpec((None, seq, LANES), whole),
        scratch_shapes=[
            pltpu.VMEM((2, HEADS_PER_BLOCK * blk, blk), F32),
            pltpu.VMEM((2, HEADS_PER_BLOCK * blk, blk), BF16),
            pltpu.VMEM((2, HEADS_PER_BLOCK * blk, blk), F32),
            pltpu.VMEM((2, HEADS_PER_BLOCK * blk, LANES), F32),
            pltpu.VMEM((2, blk, LANES), F32),
            pltpu.VMEM((2, HEADS_PER_BLOCK * blk, LANES), F32),
        ],
    )
    out = pl.pallas_call(
        functools.partial(_attn_kernel, blk=blk, n_blk=n_blk, n_items=len(items)),
        grid_spec=grid_spec,
        out_shape=jax.ShapeDtypeStruct((bsz, seq, sb), BF16),
        compiler_params=pltpu.CompilerParams(
            dimension_semantics=("arbitrary", "arbitrary"), vmem_limit_bytes=40 * MIB),
        name="sb_attention",
    )(item_i, item_j, q3, k3, v3, tri, masks)
    return out.reshape(bsz * seq, sb)


def _ssm_params(lam_re, lam_im, log_dt, b_re, b_im, c_re, c_im, chunk, n_chunks):
    lam = lax.complex(lam_re.astype(F32), lam_im.astype(F32))
    dt = jnp.exp(log_dt.astype(F32))[:, None]
    lam_dt = lam * dt
    lam_bar = jnp.exp(lam_dt)
    b_bar = ((lam_bar - 1.0) / lam)[:, :, None] * lax.complex(b_re.astype(F32), b_im.astype(F32))
    b_t = jnp.swapaxes(b_bar, 1, 2)
    c_mat = lax.complex(c_re.astype(F32), c_im.astype(F32))
    steps = jnp.arange(chunk + 1, dtype=F32)
    pw = jnp.exp(lam_dt[:, None, :] * steps[None, :, None])
    npw = jnp.exp(-lam_dt[:, None, :] * steps[None, :chunk, None])

    def halves(lo, hi):
        return jnp.concatenate([lo, hi], axis=-1)

    def b_side(z):
        return [halves(jnp.real(z), jnp.real(z)), halves(-jnp.imag(z), jnp.imag(z))]

    def c_side(z):
        return [halves(jnp.real(z), jnp.imag(z)), halves(jnp.imag(z), jnp.real(z))]

    time_terms = jnp.stack(
        b_side(npw)
        + b_side(pw[:, chunk - 1::-1])
        + c_side(pw[:, :chunk])
        + c_side(pw[:, 1:]), axis=1)
    chan_terms = jnp.stack(
        [halves(jnp.real(b_t), jnp.imag(b_t)), halves(jnp.imag(b_t), jnp.real(b_t)),
         halves(jnp.real(c_mat), -jnp.real(c_mat)), halves(-jnp.imag(c_mat), -jnp.imag(c_mat))],
        axis=1)
    n_steps = max(1, (n_chunks - 1).bit_length())
    powers = []
    cur = pw[:, chunk, :]
    for _ in range(n_steps):
        powers.append(cur)
        cur = cur * cur
    lam_pow = jnp.stack(powers, axis=1)
    a1 = halves(jnp.real(lam_pow), jnp.real(lam_pow))
    a2 = halves(-jnp.imag(lam_pow), jnp.imag(lam_pow))
    return time_terms, chan_terms, a1, a2


def _ssm_kernel(u_ref, time_ref, chan_ref, a1_ref, a2_ref, perm_ref, perm_t_ref, y_ref,
                u8_ref, y8_ref, toep_ref, *, n_chunks, n_steps, group):
    seq = u_ref.shape[0]
    tiles = seq // SSM_FOLD
    lc = toep_ref.shape[0]
    per_chunk = (lc // group) // SSM_FOLD
    p2 = a1_ref.shape[2]
    group_shift = group.bit_length() - 1
    nt = (((1,), (1,)), ((), ()))

    folded = jnp.concatenate(
        [u_ref[pl.ds(s, tiles, stride=SSM_FOLD), :] for s in range(SSM_FOLD)], axis=1)
    regrouped = jnp.dot(folded.astype(BF16), perm_ref[...], preferred_element_type=F32)
    n_groups = LANES // group
    for g in range(n_groups):
        u8_ref[g] = regrouped[:, g * LANES:(g + 1) * LANES]

    for g in range(n_groups):
        def table(t1, t2, v1, v2):
            full = (time_ref[g, t1][:, None, :] * chan_ref[g, v1][None, :, :]
                    + time_ref[g, t2][:, None, :] * chan_ref[g, v2][None, :, :])
            return full.reshape(lc, p2).astype(BF16)

        src = table(0, 1, 0, 1)
        to_state = table(2, 3, 0, 1)
        dst = table(4, 5, 2, 3)
        from_state = table(6, 7, 2, 3)
        cb = min(256, lc)
        for j in range(lc // cb):
            blk = lax.dot_general(src, dst[j * cb:(j + 1) * cb], nt, preferred_element_type=F32)
            s_idx = lax.broadcasted_iota(I32, (lc, cb), 0) >> group_shift
            t_idx = (lax.broadcasted_iota(I32, (lc, cb), 1) + j * cb) >> group_shift
            toep_ref[:, j * cb:(j + 1) * cb] = jnp.where(s_idx <= t_idx, blk, 0.0).astype(BF16)
        u = jnp.concatenate(
            [u8_ref[g, pl.ds(tau, n_chunks, stride=per_chunk), :] for tau in range(per_chunk)],
            axis=1).astype(BF16)
        y = jnp.dot(u, toep_ref[...], preferred_element_type=F32)
        z = jnp.dot(u, to_state, preferred_element_type=F32)
        n = lax.broadcasted_iota(I32, z.shape, 0)
        x = jnp.where(n >= 1, pltpu.roll(z, 1, 0), 0.0)
        for k in range(n_steps):
            sh = 1 << k
            xs = jnp.where(n >= sh, pltpu.roll(x, sh, 0), 0.0)
            x = (x + a1_ref[g, k:k + 1, :] * xs
                 + a2_ref[g, k:k + 1, :] * pltpu.roll(xs, p2 // 2, 1))
        y = y + lax.dot_general(x.astype(BF16), from_state, nt, preferred_element_type=F32)
        for tau in range(per_chunk):
            y8_ref[g, pl.ds(tau, n_chunks, stride=per_chunk), :] = y[:, tau * LANES:(tau + 1) * LANES]

    y8 = jnp.concatenate([y8_ref[g] for g in range(n_groups)], axis=1)
    unfolded = jnp.dot(y8.astype(BF16), perm_t_ref[...], preferred_element_type=F32)
    for s in range(SSM_FOLD):
        y_ref[pl.ds(s, tiles, stride=SSM_FOLD), :] = unfolded[:, s * LANES:(s + 1) * LANES]


def _ssm(u, bsz, seq, lam_re, lam_im, log_dt, b_re, b_im, c_re, c_im):
    g, p = lam_re.shape
    c = b_re.shape[-1]
    chunk = min(SSM_CHUNK, seq)
    n_chunks = seq // chunk
    assert n_chunks & (n_chunks - 1) == 0 and c & (c - 1) == 0, (n_chunks, c)
    assert SSM_FOLD * c == LANES and chunk % SSM_FOLD == 0
    lc = chunk * c
    per_block = LANES // c
    time_terms, chan_terms, a1, a2 = _ssm_params(
        lam_re, lam_im, log_dt, b_re, b_im, c_re, c_im, chunk, n_chunks)
    n_steps = a1.shape[1]
    src_idx = jnp.arange(SSM_FOLD * LANES, dtype=I32)
    s_of, g_of, c_of = src_idx // LANES, (src_idx % LANES) // c, src_idx % c
    dst_idx = g_of * LANES + s_of * c + c_of
    perm = (dst_idx[:, None] == src_idx[None, :]).astype(BF16)
    block = lambda cb, b: (cb, 0, 0)
    fixed = lambda cb, b: (0, 0)
    y = pl.pallas_call(
        functools.partial(_ssm_kernel, n_chunks=n_chunks, n_steps=n_steps, group=c),
        grid=(g // per_block, bsz),
        in_specs=[
            pl.BlockSpec((None, seq, LANES), lambda cb, b: (b, 0, cb)),
            pl.BlockSpec((per_block,) + time_terms.shape[1:], lambda cb, b: (cb, 0, 0, 0)),
            pl.BlockSpec((per_block,) + chan_terms.shape[1:], lambda cb, b: (cb, 0, 0, 0)),
            pl.BlockSpec((per_block, n_steps, 2 * p), block),
            pl.BlockSpec((per_block, n_steps, 2 * p), block),
            pl.BlockSpec((SSM_FOLD * LANES, SSM_FOLD * LANES), fixed),
            pl.BlockSpec((SSM_FOLD * LANES, SSM_FOLD * LANES), fixed),
        ],
        out_specs=pl.BlockSpec((None, seq, LANES), lambda cb, b: (b, 0, cb)),
        out_shape=jax.ShapeDtypeStruct((bsz, seq, g * c), F32),
        scratch_shapes=[
            pltpu.VMEM((per_block, seq // SSM_FOLD, LANES), F32),
            pltpu.VMEM((per_block, seq // SSM_FOLD, LANES), F32),
            pltpu.VMEM((lc, lc), BF16),
        ],
        compiler_params=pltpu.CompilerParams(
            dimension_semantics=("arbitrary", "arbitrary"), vmem_limit_bytes=56 * MIB),
        name="s5_chunked_scan",
    )(u.reshape(bsz, seq, g * c), time_terms, chan_terms, a1, a2, perm, perm.T)
    return y.reshape(bsz * seq, g * c)


def _post_kernel(x_ref, ysb_ref, yss_ref, u_ref, d_ref, wglu_ref, gsb_ref, gssm_ref, wout_ref,
                 ln2_ref, wr_ref, br_ref, tri_ref, x1_ref, xn_ref, idx_ref, gate_ref, rank_ref,
                 count_ref, wglu_bf, wout_bf, *, sb, n_exp):
    @pl.when(pl.program_id(0) == 0)
    def _():
        wglu_bf[...] = wglu_ref[...].astype(BF16)
        wout_bf[...] = wout_ref[...].astype(BF16)

    u = u_ref[...].astype(F32)
    y = yss_ref[...].astype(F32) + d_ref[...] * u
    y = y * (0.5 * (1.0 + jnp.tanh(math.sqrt(2.0 / math.pi) * (y + 0.044715 * (y * y * y)))))
    ab = jnp.dot(y.astype(BF16), wglu_bf[...], preferred_element_type=F32)
    w = ab.shape[1] // 2
    y_ssm = ab[:, :w] * (1.0 / (1.0 + jnp.exp(-ab[:, w:])))
    m_sb = _rms(ysb_ref[...].astype(F32), gsb_ref[...])
    m_ssm = _rms(y_ssm, gssm_ref[...])
    x1 = (x_ref[...]
          + jnp.dot(m_sb.astype(BF16), wout_bf[:sb, :], preferred_element_type=F32)
          + jnp.dot(m_ssm.astype(BF16), wout_bf[sb:, :], preferred_element_type=F32))
    x1_ref[...] = x1
    xn = _rms(x1, ln2_ref[...])
    xn_ref[...] = xn
    def split(a):
        hi = a.astype(BF16)
        return hi, (a - hi.astype(F32)).astype(BF16)

    xn_hi, xn_lo = split(xn)
    wr_hi, wr_lo = split(wr_ref[...])
    logits = (jnp.dot(xn_hi, wr_hi, preferred_element_type=F32)
              + jnp.dot(xn_hi, wr_lo, preferred_element_type=F32)
              + jnp.dot(xn_lo, wr_hi, preferred_element_type=F32)) + br_ref[...]
    lane = lax.broadcasted_iota(I32, logits.shape, 1).astype(F32)
    out_lane = lax.broadcasted_iota(I32, (logits.shape[0], LANES), 1)
    idx_out = jnp.zeros((logits.shape[0], LANES), F32)
    val_out = jnp.zeros((logits.shape[0], LANES), F32)
    top = None
    denom = None
    work = logits
    chosen = []
    for k in range(TOP_K):
        m = jnp.max(work, axis=-1, keepdims=True)
        sel = jnp.min(jnp.where(work == m, lane, float(n_exp)), axis=-1, keepdims=True)
        hit = lane == sel
        chosen.append(hit)
        work = jnp.where(hit, -jnp.inf, work)
        if k == 0:
            top = m
        e = jnp.exp(m - top)
        denom = e if denom is None else denom + e
        idx_out = jnp.where(out_lane == k, sel, idx_out)
        val_out = jnp.where(out_lane == k, e, val_out)
    idx_ref[...] = idx_out[:, :TOP_K].astype(I32)
    gate_ref[...] = (val_out / denom)[:, :TOP_K]
    member = functools.reduce(jnp.logical_or, chosen).astype(BF16)
    before = jnp.dot(tri_ref[...], member, preferred_element_type=F32)
    rank_out = jnp.zeros((logits.shape[0], LANES), F32)
    for k in range(TOP_K):
        rk = jnp.sum(jnp.where(chosen[k], before, 0.0), axis=-1, keepdims=True)
        rank_out = jnp.where(out_lane == k, rk, rank_out)
    rank_ref[...] = rank_out[:, :TOP_K].astype(I32)
    count_ref[...] = jnp.sum(member.astype(F32), axis=0, keepdims=True)


def _post(x2, y_sb, y_ss, u, ssm_d, w_glu, g_sb, g_ssm, w_out, ln2_g, w_router, b_router):
    t, d = x2.shape
    sb = y_sb.shape[1]
    w = y_ss.shape[1]
    n_exp = w_router.shape[1]
    tm = min(TOKEN_TILE, t)
    row = lambda i: (i, 0)
    fixed = lambda i: (0, 0)
    earlier = (lax.broadcasted_iota(I32, (tm, tm), 1)
               < lax.broadcasted_iota(I32, (tm, tm), 0)).astype(BF16)
    return pl.pallas_call(
        functools.partial(_post_kernel, sb=sb, n_exp=n_exp),
        grid=(t // tm,),
        in_specs=[
            pl.BlockSpec((tm, d), row),
            pl.BlockSpec((tm, sb), row),
            pl.BlockSpec((tm, w), row),
            pl.BlockSpec((tm, w), row),
            pl.BlockSpec((1, w), fixed),
            pl.BlockSpec((w, 2 * w), fixed),
            pl.BlockSpec((1, sb), fixed),
            pl.BlockSpec((1, w), fixed),
            pl.BlockSpec((sb + w, d), fixed),
            pl.BlockSpec((1, d), fixed),
            pl.BlockSpec((d, n_exp), fixed),
            pl.BlockSpec((1, n_exp), fixed),
            pl.BlockSpec((tm, tm), fixed),
        ],
        out_specs=[
            pl.BlockSpec((tm, d), row),
            pl.BlockSpec((tm, d), row),
            pl.BlockSpec((tm, TOP_K), row),
            pl.BlockSpec((tm, TOP_K), row),
            pl.BlockSpec((tm, TOP_K), row),
            pl.BlockSpec((None, 1, n_exp), lambda i: (i, 0, 0)),
        ],
        out_shape=[
            jax.ShapeDtypeStruct((t, d), F32),
            jax.ShapeDtypeStruct((t, d), F32),
            jax.ShapeDtypeStruct((t, TOP_K), I32),
            jax.ShapeDtypeStruct((t, TOP_K), F32),
            jax.ShapeDtypeStruct((t, TOP_K), I32),
            jax.ShapeDtypeStruct((t // tm, 1, n_exp), F32),
        ],
        scratch_shapes=[pltpu.VMEM((w, 2 * w), BF16), pltpu.VMEM((sb + w, d), BF16)],
        compiler_params=pltpu.CompilerParams(
            dimension_semantics=("arbitrary",), vmem_limit_bytes=48 * MIB),
        name="post_mixer_router",
    )(x2, y_sb, y_ss, u, ssm_d.reshape(1, w), w_glu, g_sb.reshape(1, sb),
      g_ssm.reshape(1, w), w_out, ln2_g.reshape(1, d), w_router,
      b_router.reshape(1, n_exp), earlier)


def _plan(top_idx, lrank, tile_counts, rows_per_block):
    t, k = top_idx.shape
    n_tiles, _, n_exp = tile_counts.shape
    tile = t // n_tiles
    experts = jnp.arange(n_exp, dtype=I32)
    cnt = tile_counts.reshape(n_tiles, n_exp).astype(I32)
    cnt8 = (cnt + SUBLANES - 1) // SUBLANES * SUBLANES
    seg_len = jnp.sum(cnt8, axis=0)
    padded = ((seg_len + rows_per_block - 1) // rows_per_block) * rows_per_block
    pad_ends = jnp.cumsum(padded)
    pad_starts = pad_ends - padded
    run_start = pad_starts[None, :] + jnp.cumsum(cnt8, axis=0) - cnt8
    local_start = jnp.cumsum(cnt8, axis=1) - cnt8
    hit = top_idx.reshape(n_tiles, tile, k, 1) == experts
    lr = lrank.reshape(n_tiles, tile, k)
    lpos = jnp.sum(jnp.where(hit, local_start[:, None, None, :], 0), axis=-1) + lr
    dest = jnp.sum(jnp.where(hit, run_start[:, None, None, :], 0), axis=-1) + lr
    n_rows = t * k + n_tiles * n_exp * SUBLANES + n_exp * rows_per_block
    n_blocks = n_rows // rows_per_block
    block_start = jnp.arange(n_blocks, dtype=I32) * rows_per_block
    block_expert = jnp.minimum(
        jnp.sum((pad_ends[None, :] <= block_start[:, None]).astype(I32), axis=1), n_exp - 1)
    n_used = (pad_ends[-1] // rows_per_block).astype(I32).reshape(1)
    fill_start = jnp.concatenate([jnp.maximum(pad_ends - rows_per_block, 0).astype(I32), n_used])
    nonempty = seg_len > 0
    order = jnp.cumsum(nonempty.astype(I32)) - nonempty.astype(I32)
    later_ne = jnp.where((experts[None, :] > experts[:, None]) & nonempty[None, :],
                         experts[None, :], n_exp)
    next_tbl = jnp.min(later_ne, axis=1)
    onehot = block_expert[:, None] == experts[None, :]
    w_slot = jnp.sum(jnp.where(onehot, order & 1, 0), axis=1).astype(I32)
    w_next = jnp.sum(jnp.where(onehot, next_tbl, 0), axis=1).astype(I32)
    runs = (local_start.reshape(-1).astype(I32), run_start.reshape(-1).astype(I32),
            (cnt8 // SUBLANES).reshape(-1).astype(I32), fill_start)
    return (dest.reshape(t, k).astype(I32), lpos.transpose(0, 2, 1).astype(I32), runs,
            (block_expert, n_used, w_slot, w_next), n_blocks)


def _dispatch_kernel(lstart_ref, rstart_ref, n8_ref, fill_ref, lpos_ref, xn_ref, xs_hbm,
                     xloc, zeros_ref, row_sem, fill_sem, *, n_exp, fill_rows):
    i = pl.program_id(0)
    tile = xn_ref.shape[0]
    chunk = 256

    @pl.when(i == 0)
    def _():
        zeros_ref[...] = jnp.zeros_like(zeros_ref)
        for e in range(n_exp):
            start = pl.multiple_of(fill_ref[e], SUBLANES)
            pltpu.make_async_copy(zeros_ref, xs_hbm.at[pl.ds(start, fill_rows), :],
                                  fill_sem).start()
        for e in range(n_exp):
            pltpu.make_async_copy(zeros_ref, xs_hbm.at[pl.ds(0, fill_rows), :], fill_sem).wait()
        n_blocks = xs_hbm.shape[0] // fill_rows

        def fill_unused(b, carry):
            cp = pltpu.make_async_copy(
                zeros_ref, xs_hbm.at[pl.ds(pl.multiple_of(b * fill_rows, fill_rows), fill_rows), :],
                fill_sem)
            cp.start()
            cp.wait()
            return carry

        lax.fori_loop(fill_ref[n_exp], n_blocks, fill_unused, 0)

    xb = xn_ref[...].astype(BF16)
    for c in range(xloc.shape[0] // chunk):
        pos = lax.broadcasted_iota(I32, (chunk, tile), 0) + c * chunk
        sel = functools.reduce(jnp.logical_or,
                               [lpos_ref[k:k + 1, :] == pos for k in range(TOP_K)])
        xloc[c * chunk:(c + 1) * chunk, :] = jnp.dot(sel.astype(BF16), xb,
                                                     preferred_element_type=F32)

    def run_copies(wait):
        for e in range(n_exp):
            n8 = n8_ref[i * n_exp + e]
            src0 = lstart_ref[i * n_exp + e]
            dst0 = rstart_ref[i * n_exp + e]
            n32 = lax.shift_right_logical(n8, 2)

            def piece(off, size):
                cp = pltpu.make_async_copy(
                    xloc.at[pl.ds(pl.multiple_of(src0 + off, SUBLANES), size), :],
                    xs_hbm.at[pl.ds(pl.multiple_of(dst0 + off, SUBLANES), size), :], row_sem)
                cp.wait() if wait else cp.start()

            def wide_piece(j, carry):
                piece(j * 32, 32)
                return carry

            def narrow_piece(j, carry):
                piece(j * SUBLANES, SUBLANES)
                return carry

            lax.fori_loop(0, n32, wide_piece, 0)
            lax.fori_loop(n32 * 4, n8, narrow_piece, 0)

    run_copies(wait=False)
    run_copies(wait=True)


def _dispatch(xn, lpos_t, runs, n_rows):
    t, d = xn.shape
    n_tiles, _, tile = lpos_t.shape
    n_exp = runs[3].shape[0] - 1
    loc_rows = tile * TOP_K + n_exp * SUBLANES
    assert loc_rows % 256 == 0
    grid_spec = pltpu.PrefetchScalarGridSpec(
        num_scalar_prefetch=4,
        grid=(n_tiles,),
        in_specs=[
            pl.BlockSpec((None, TOP_K, tile), lambda i, *_: (i, 0, 0)),
            pl.BlockSpec((tile, d), lambda i, *_: (i, 0)),
        ],
        out_specs=pl.BlockSpec(memory_space=pl.ANY),
        scratch_shapes=[
            pltpu.VMEM((loc_rows, d), F32),
            pltpu.VMEM((EXPERT_ROWS, d), F32),
            pltpu.SemaphoreType.DMA,
            pltpu.SemaphoreType.DMA,
        ],
    )
    return pl.pallas_call(
        functools.partial(_dispatch_kernel, n_exp=n_exp, fill_rows=EXPERT_ROWS),
        grid_spec=grid_spec,
        out_shape=jax.ShapeDtypeStruct((n_rows, d), F32),
        compiler_params=pltpu.CompilerParams(
            dimension_semantics=("arbitrary",), vmem_limit_bytes=40 * MIB),
        name="moe_dispatch",
    )(*runs, lpos_t, xn)


def _expert_kernel(be_ref, nused_ref, wslot_ref, wnext_ref, xs_hbm, wg_hbm, bg_ref, wu_hbm, bu_ref,
                   wd_hbm, bd_ref, y_hbm, xbuf, ybuf, x_sem, y_sem, wg32, wu32, wd32, w_sem,
                   wg_bf, wu_bf, wd_bf, *, rows, n_exp):
    i = pl.program_id(0)
    n = nused_ref[0]

    def w_copies(e, slot):
        return [pltpu.make_async_copy(src.at[e], dst.at[slot], w_sem.at[slot])
                for src, dst in ((wg_hbm, wg32), (wu_hbm, wu32), (wd_hbm, wd32))]

    def x_copy(b):
        return pltpu.make_async_copy(
            xs_hbm.at[pl.ds(pl.multiple_of(b * rows, rows), rows), :], xbuf.at[b % 2],
            x_sem.at[b % 2])

    def y_copy(b):
        return pltpu.make_async_copy(
            ybuf.at[b % 2], y_hbm.at[pl.ds(pl.multiple_of(b * rows, rows), rows), 0, :],
            y_sem.at[b % 2])

    @pl.when(i < n)
    def _():
        @pl.when(i == 0)
        def _():
            x_copy(0).start()
            for cp in w_copies(be_ref[0], wslot_ref[0]):
                cp.start()

        @pl.when(i + 1 < n)
        def _():
            x_copy(i + 1).start()

        changed = jnp.logical_or(i == 0, be_ref[i] != be_ref[jnp.maximum(i - 1, 0)])

        @pl.when(changed)
        def _():
            slot = wslot_ref[i]
            for cp in w_copies(be_ref[i], slot):
                cp.wait()

            @pl.when(wnext_ref[i] < n_exp)
            def _():
                for cp in w_copies(wnext_ref[i], 1 - slot):
                    cp.start()
            wg_bf[...] = wg32[slot].astype(BF16)
            wu_bf[...] = wu32[slot].astype(BF16)
            wd_bf[...] = wd32[slot].astype(BF16)

        x_copy(i).wait()
        x = xbuf[i % 2].astype(BF16)
        gate = jnp.dot(x, wg_bf[...], preferred_element_type=F32) + bg_ref[...]
        up = jnp.dot(x, wu_bf[...], preferred_element_type=F32) + bu_ref[...]
        gate = jnp.minimum(gate, SWIGLU_LIMIT)
        up = jnp.clip(up, -SWIGLU_LIMIT, SWIGLU_LIMIT)
        glu = gate * (1.0 / (1.0 + jnp.exp(-SWIGLU_ALPHA * gate)))
        hidden = ((up + 1.0) * glu).astype(BF16)
        y = jnp.dot(hidden, wd_bf[...], preferred_element_type=F32) + bd_ref[...]

        @pl.when(i >= 2)
        def _():
            y_copy(i - 2).wait()
        ybuf[i % 2] = y
        y_copy(i).start()

        @pl.when(i == n - 1)
        def _():
            y_copy(i).wait()

            @pl.when(i >= 1)
            def _():
                y_copy(i - 1).wait()

    @pl.when(i >= n)
    def _():
        ybuf[i % 2] = jnp.zeros((rows, ybuf.shape[2]), F32)
        y_copy(i).start()
        y_copy(i).wait()


def _experts(x_sorted, block_plan, n_blocks, w_gate, b_gate, w_up, b_up, w_down, b_down):
    n_rows, d = x_sorted.shape
    n_exp, _, f = w_gate.shape
    rows = EXPERT_ROWS
    bmap = lambda i, be, nu, ws, wn: (be[i], 0, 0)
    hbm = pl.BlockSpec(memory_space=pl.ANY)
    grid_spec = pltpu.PrefetchScalarGridSpec(
        num_scalar_prefetch=4,
        grid=(n_blocks,),
        in_specs=[
            hbm,
            hbm, pl.BlockSpec((None, 1, f), bmap),
            hbm, pl.BlockSpec((None, 1, f), bmap),
            hbm, pl.BlockSpec((None, 1, d), bmap),
        ],
        out_specs=hbm,
        scratch_shapes=[
            pltpu.VMEM((2, rows, d), F32),
            pltpu.VMEM((2, rows, d), F32),
            pltpu.SemaphoreType.DMA((2,)),
            pltpu.SemaphoreType.DMA((2,)),
            pltpu.VMEM((2, d, f), F32),
            pltpu.VMEM((2, d, f), F32),
            pltpu.VMEM((2, f, d), F32),
            pltpu.SemaphoreType.DMA((2,)),
            pltpu.VMEM((d, f), BF16),
            pltpu.VMEM((d, f), BF16),
            pltpu.VMEM((f, d), BF16),
        ],
    )
    return pl.pallas_call(
        functools.partial(_expert_kernel, rows=rows, n_exp=n_exp),
        grid_spec=grid_spec,
        out_shape=jax.ShapeDtypeStruct((n_rows, 1, d), F32),
        compiler_params=pltpu.CompilerParams(
            dimension_semantics=("arbitrary",), vmem_limit_bytes=52 * MIB),
        name="moe_experts",
    )(*block_plan, x_sorted,
      w_gate, b_gate.reshape(n_exp, 1, f), w_up, b_up.reshape(n_exp, 1, f),
      w_down, b_down.reshape(n_exp, 1, d))


def _combine_kernel(dest_hbm, y_hbm, x1_ref, gate_ref, lnf_ref, o_ref, idx_smem, ybuf,
                    idx_sem, row_sem, *, tile):
    i = pl.program_id(0)
    n = pl.num_programs(0)
    per_tile = TOP_K * tile

    def slot_base(b):
        return pl.multiple_of((b % 3) * per_tile, per_tile)

    def idx_copy(b):
        src = dest_hbm.at[pl.ds(pl.multiple_of(b * per_tile, per_tile), per_tile)]
        return pltpu.make_async_copy(src, idx_smem.at[pl.ds(slot_base(b), per_tile)],
                                     idx_sem.at[b % 3])

    def gather(b):
        base = slot_base(b)
        buf = ybuf.at[b % 2]
        sem = row_sem.at[b % 2]

        def issue(g, carry):
            for s in range(SUBLANES):
                pltpu.make_async_copy(y_hbm.at[idx_smem[base + g * SUBLANES + s]],
                                      buf.at[g, pl.ds(s, 1), :], sem).start(priority=s % 2)
            return carry

        lax.fori_loop(0, per_tile // SUBLANES, issue, 0)

    @pl.when(i == 0)
    def _():
        idx_copy(0).start()
        idx_copy(0).wait()
        gather(0)

        @pl.when(n > 1)
        def _():
            idx_copy(1).start()

    @pl.when(i + 1 < n)
    def _():
        idx_copy(i + 1).wait()

        @pl.when(i + 2 < n)
        def _():
            idx_copy(i + 2).start()
        gather(i + 1)

    cur = i % 2
    pltpu.make_async_copy(ybuf.at[1 - cur], ybuf.at[cur], row_sem.at[cur]).wait()
    gates = gate_ref[...]
    x = x1_ref[...]
    groups = tile // SUBLANES
    for k in range(TOP_K):
        yk = ybuf[cur, k * groups:(k + 1) * groups].reshape(tile, x.shape[1])
        x = x + gates[:, k:k + 1] * yk
    o_ref[...] = _rms(x, lnf_ref[...])


def _combine(dest, y_rows, x1, gates, ln_f_g):
    t, d = x1.shape
    tile = min(COMBINE_TILE, t)
    n_tiles = t // tile
    dest_tiles = dest.reshape(n_tiles, tile, TOP_K).transpose(0, 2, 1).reshape(n_tiles * TOP_K * tile)
    return pl.pallas_call(
        functools.partial(_combine_kernel, tile=tile),
        grid=(n_tiles,),
        in_specs=[
            pl.BlockSpec(memory_space=pl.ANY),
            pl.BlockSpec(memory_space=pl.ANY),
            pl.BlockSpec((tile, d), lambda i: (i, 0)),
            pl.BlockSpec((tile, TOP_K), lambda i: (i, 0)),
            pl.BlockSpec((1, d), lambda i: (0, 0)),
        ],
        out_specs=pl.BlockSpec((tile, d), lambda i: (i, 0)),
        out_shape=jax.ShapeDtypeStruct((t, d), F32),
        scratch_shapes=[
            pltpu.SMEM((3 * TOP_K * tile,), I32),
            pltpu.VMEM((2, TOP_K * tile // SUBLANES, SUBLANES, d), F32),
            pltpu.SemaphoreType.DMA((3,)),
            pltpu.SemaphoreType.DMA((2,)),
        ],
        compiler_params=pltpu.CompilerParams(
            dimension_semantics=("arbitrary",), vmem_limit_bytes=40 * MIB),
        name="moe_combine",
    )(dest_tiles, y_rows, x1, gates, ln_f_g.reshape(1, d))


def kernel(x, ln1_g, w_in, lam_re, lam_im, log_dt, ssm_b_re, ssm_b_im, ssm_c_re, ssm_c_im,
           ssm_d, w_glu, g_sb, g_ssm, w_out, ln2_g, w_router, b_router, w_gate, b_gate,
           w_up, b_up, w_down, b_down, ln_f_g):
    bsz, seq, d = x.shape
    assert ln1_g.shape[0] == 1, "depth-1 block only"
    ssm_w = ssm_d.shape[1]
    sb = g_sb.shape[1]
    x2 = x.reshape(bsz * seq, d)
    q, k, v, u = _in_proj(x2, ln1_g[0], w_in[0], sb, ssm_w)
    y_sb = _attention(q, k, v, bsz, seq)
    y_ss = _ssm(u, bsz, seq, lam_re[0], lam_im[0], log_dt[0], ssm_b_re[0], ssm_b_im[0],
                ssm_c_re[0], ssm_c_im[0])
    x1, xn, top_idx, gates, lrank, tile_counts = _post(
        x2, y_sb, y_ss, u, ssm_d[0], w_glu[0], g_sb[0], g_ssm[0], w_out[0], ln2_g[0],
        w_router[0], b_router[0])
    dest, lpos_t, runs, block_plan, n_blocks = _plan(top_idx, lrank, tile_counts, EXPERT_ROWS)
    x_sorted = _dispatch(xn, lpos_t, runs, n_blocks * EXPERT_ROWS)
    y_rows = _experts(x_sorted, block_plan, n_blocks, w_gate[0], b_gate[0],
                      w_up[0], b_up[0], w_down[0], b_down[0])
    out = _combine(dest, y_rows, x1, gates, ln_f_g)
    return out.reshape(bsz, seq, d)
```

```python
import functools
import math

import jax
import jax.numpy as jnp
from jax import lax
from jax.experimental import pallas as pl
from jax.experimental.pallas import tpu as pltpu

F32 = jnp.float32
BF16 = jnp.bfloat16
I32 = jnp.int32

EPS = 1e-5
SB_HEAD_DIM = 64
SSM_GROUP = 16
SSM_STATE = 64
TOP_K = 4
SWIGLU_LIMIT = 7.0
SWIGLU_ALPHA = 1.702

LANES = 128
SUBLANES = 8
HEADS_PER_BLOCK = LANES // SB_HEAD_DIM
ATTN_BLOCK = 256
ATTN_EXP_FLOOR = 160.0
SSM_CHUNK = 16
SSM_FOLD = 8
TOKEN_TILE = 512
EXPERT_ROWS = 256
COMBINE_TILE = 256
DISPATCH_TILE = 1024
MIB = 1024 * 1024
LOG2E = 1.4426950408889634


def _rms(x, g):
    return x * lax.rsqrt(jnp.mean(x * x, axis=-1, keepdims=True) + EPS) * g


def _in_proj_kernel(x_ref, g_ref, w_ref, q_ref, k_ref, v_ref, u_ref, w_bf, *, sb, scale):
    @pl.when(pl.program_id(0) == 0)
    def _():
        w_bf[...] = w_ref[...].astype(BF16)

    h = _rms(x_ref[...], g_ref[...])
    proj = jnp.dot(h.astype(BF16), w_bf[...], preferred_element_type=F32)
    k_ref[...] = proj[:, sb:2 * sb].astype(BF16)
    u_ref[...] = proj[:, 3 * sb:]
    lane = lax.broadcasted_iota(I32, (1, LANES), 1)
    q = (proj[:, :sb] * scale).astype(BF16)
    v = proj[:, 2 * sb:3 * sb].astype(BF16)
    zero = jnp.zeros((), BF16)
    for pair in range(sb // LANES):
        cols = slice(pair * LANES, (pair + 1) * LANES)
        for head in range(HEADS_PER_BLOCK):
            own = (lane // SB_HEAD_DIM) == head
            out = slice((pair * HEADS_PER_BLOCK + head) * LANES,
                        (pair * HEADS_PER_BLOCK + head + 1) * LANES)
            q_ref[:, out] = jnp.where(own, q[:, cols], zero)
            v_ref[:, out] = jnp.where(own, v[:, cols], zero)


def _in_proj(x2, ln1_g, w_in, sb, ssm_w):
    t, d = x2.shape
    n_in = w_in.shape[1]
    tm = min(TOKEN_TILE, t)
    scale = LOG2E / math.sqrt(SB_HEAD_DIM)
    return pl.pallas_call(
        functools.partial(_in_proj_kernel, sb=sb, scale=scale),
        grid=(t // tm,),
        in_specs=[
            pl.BlockSpec((tm, d), lambda i: (i, 0)),
            pl.BlockSpec((1, d), lambda i: (0, 0)),
            pl.BlockSpec((d, n_in), lambda i: (0, 0)),
        ],
        out_specs=[
            pl.BlockSpec((tm, HEADS_PER_BLOCK * sb), lambda i: (i, 0)),
            pl.BlockSpec((tm, sb), lambda i: (i, 0)),
            pl.BlockSpec((tm, HEADS_PER_BLOCK * sb), lambda i: (i, 0)),
            pl.BlockSpec((tm, ssm_w), lambda i: (i, 0)),
        ],
        out_shape=[
            jax.ShapeDtypeStruct((t, HEADS_PER_BLOCK * sb), BF16),
            jax.ShapeDtypeStruct((t, sb), BF16),
            jax.ShapeDtypeStruct((t, HEADS_PER_BLOCK * sb), BF16),
            jax.ShapeDtypeStruct((t, ssm_w), F32),
        ],
        scratch_shapes=[pltpu.VMEM((d, n_in), BF16)],
        compiler_params=pltpu.CompilerParams(
            dimension_semantics=("arbitrary",), vmem_limit_bytes=48 * MIB),
        name="in_proj",
    )(x2, ln1_g.reshape(1, d), w_in)


def _attn_kernel(ti_ref, tj_ref, q_ref, k_ref, v_ref, tri_ref, mask_ref, o_ref,
                 dbuf, spbuf, lbuf, rsbuf, acc_ref, r_ref, *, blk, n_blk, n_items):
    sign_bit = jnp.uint32(0x80000000)
    last = n_items - 1
    ALL, CAUSAL, NONE = 0, 1, 2

    for ref in (dbuf, spbuf, lbuf, rsbuf, acc_ref, r_ref):
        ref[...] = jnp.zeros_like(ref)

    def both_heads(ref, start):
        rows = pl.ds(pl.multiple_of(start, blk), blk)
        return jnp.concatenate([ref[rows, h * LANES:(h + 1) * LANES]
                                for h in range(HEADS_PER_BLOCK)], axis=0)

    def item(idx):
        c = jnp.clip(idx, 0, last)
        return ti_ref[c], tj_ref[c]

    half = n_blk // 2
    stream_end = (half * (half + 1) // 2, n_items)

    def step(state, x):
        cur, p1, p2, out_blk = list(state[:2]), state[2], state[3], list(state[4:])
        slot, other = x, 1 - x
        c1 = jnp.where(cur[x] < stream_end[x], cur[x], n_items)
        c3 = p2
        i1, j1 = item(c1)
        ks = k_ref[pl.ds(pl.multiple_of(j1 * blk, blk), blk), :]
        w = lax.dot_general(both_heads(q_ref, i1 * blk), ks, (((1,), (1,)), ((), ())),
                            preferred_element_type=F32)
        i3, j3 = item(c3)
        valid = c3 <= last
        first = jnp.logical_and(i3 == j3, valid)
        mask3 = mask_ref[jnp.where(valid, (i3 == j3).astype(I32), NONE)]
        i3 = jnp.where(valid, i3, out_blk[x])
        r_prev = jnp.where(first, 0.0, r_ref[x])
        r_new = r_prev + rsbuf[other]
        r_ref[x] = r_new
        done = jnp.logical_and(valid, jnp.min(r_new) >= ATTN_EXP_FLOOR)
        next_block = lax.shift_right_logical((i3 + 1) * (i3 + 2), 1)
        advanced = jnp.where(done, jnp.maximum(cur[x] + 1, next_block), cur[x] + 1)
        cur[x] = jnp.minimum(advanced, stream_end[x])
        out_blk[x] = i3
        r_wide = jnp.concatenate([r_prev] * (blk // LANES), axis=1)
        a = jnp.exp2(dbuf[slot] - (lbuf[other] + r_wide))
        ab = a.astype(BF16) * jnp.concatenate([mask3, mask3], axis=0)
        a_cat = jnp.concatenate([ab[:blk], ab[blk:]], axis=1)
        acc = jnp.where(first, 0.0, acc_ref[x]) + jnp.dot(
            a_cat, both_heads(v_ref, j3 * blk), preferred_element_type=F32)
        acc_ref[x] = acc
        o_ref[pl.ds(pl.multiple_of(i3 * blk, blk), blk), :] = acc.astype(o_ref.dtype)
        sums = jnp.dot(spbuf[other], tri_ref[...], preferred_element_type=F32)
        lbuf[slot] = sums[:, :blk]
        rsbuf[slot] = sums[:, blk:]
        mask1 = mask_ref[(i1 == j1).astype(I32)]
        neg_abs = lax.bitcast_convert_type(
            lax.bitcast_convert_type(w, jnp.uint32) | sign_bit, F32)
        sp2 = jnp.maximum(w, 0.0) + jnp.log(1.0 + jnp.exp2(neg_abs)) * LOG2E
        dbuf[slot] = w - sp2
        spbuf[slot] = sp2.astype(BF16) * jnp.concatenate([mask1, mask1], axis=0)
        return cur[0], cur[1], c1, p1, out_blk[0], out_blk[1]

    def in_flight(state):
        cur0, cur1, p1, p2 = state[:4]
        return functools.reduce(jnp.logical_or, [cur0 < stream_end[0], cur1 < stream_end[1],
                                                 p1 <= last, p2 <= last])

    def four_steps(state):
        for x in (0, 1, 0, 1):
            state = step(state, x)
        return state

    none = jnp.int32(n_items)
    lax.while_loop(in_flight, four_steps,
                   (jnp.int32(0), jnp.int32(stream_end[0]), none, none,
                    jnp.int32(0), jnp.int32(half)))


def _attention(q, k, v, bsz, seq):
    sb = k.shape[-1]
    blk = min(ATTN_BLOCK, seq)
    n_blk = seq // blk
    n_pairs = sb // LANES
    q3, k3, v3 = (a.reshape(bsz, seq, a.shape[-1]) for a in (q, k, v))
    per_head = HEADS_PER_BLOCK * LANES
    jj = lax.broadcasted_iota(I32, (blk, blk), 0)
    ss = lax.broadcasted_iota(I32, (blk, blk), 1)
    tri = jnp.concatenate([(jj > ss).astype(BF16), jnp.ones((blk, LANES), BF16)], axis=1)
    masks = jnp.stack([jnp.ones((blk, blk), BF16), (ss < jj).astype(BF16),
                       jnp.zeros((blk, blk), BF16)])
    items = [(i, j) for i in range(n_blk) for j in range(i, -1, -1)]
    item_i = jnp.asarray([i for i, _ in items], I32)
    item_j = jnp.asarray([j for _, j in items], I32)
    whole = lambda b, p, ti, tj: (b, 0, p)
    grid_spec = pltpu.PrefetchScalarGridSpec(
        num_scalar_prefetch=2,
        grid=(bsz, n_pairs),
        in_specs=[
            pl.BlockSpec((None, seq, per_head), whole),
            pl.BlockSpec((None, seq, LANES), whole),
            pl.BlockSpec((None, seq, per_head), whole),
            pl.BlockSpec((blk, blk + LANES), lambda b, p, ti, tj: (0, 0)),
            pl.BlockSpec((3, blk, blk), lambda b, p, ti, tj: (0, 0, 0)),
        ],
        out_specs=pl.BlockSpec((None, seq, LANES), whole),
        scratch_shapes=[
            pltpu.VMEM((2, HEADS_PER_BLOCK * blk, blk), F32),
            pltpu.VMEM((2, HEADS_PER_BLOCK * blk, blk), BF16),
            pltpu.VMEM((2, HEADS_PER_BLOCK * blk, blk), F32),
            pltpu.VMEM((2, HEADS_PER_BLOCK * blk, LANES), F32),
            pltpu.VMEM((2, blk, LANES), F32),
            pltpu.VMEM((2, HEADS_PER_BLOCK * blk, LANES), F32),
        ],
    )
    out = pl.pallas_call(
        functools.partial(_attn_kernel, blk=blk, n_blk=n_blk, n_items=len(items)),
        grid_spec=grid_spec,
        out_shape=jax.ShapeDtypeStruct((bsz, seq, sb), BF16),
        compiler_params=pltpu.CompilerParams(
            dimension_semantics=("arbitrary", "arbitrary"), vmem_limit_bytes=40 * MIB),
        name="sb_attention",
    )(item_i, item_j, q3, k3, v3, tri, masks)
    return out.reshape(bsz * seq, sb)


def _ssm_params(lam_re, lam_im, log_dt, b_re, b_im, c_re, c_im, chunk, n_chunks):
    lam = lax.complex(lam_re.astype(F32), lam_im.astype(F32))
    dt = jnp.exp(log_dt.astype(F32))[:, None]
    lam_dt = lam * dt
    lam_bar = jnp.exp(lam_dt)
    b_bar = ((lam_bar - 1.0) / lam)[:, :, None] * lax.complex(b_re.astype(F32), b_im.astype(F32))
    b_t = jnp.swapaxes(b_bar, 1, 2)
    c_mat = lax.complex(c_re.astype(F32), c_im.astype(F32))
    steps = jnp.arange(chunk + 1, dtype=F32)
    pw = jnp.exp(lam_dt[:, None, :] * steps[None, :, None])
    npw = jnp.exp(-lam_dt[:, None, :] * steps[None, :chunk, None])

    def halves(lo, hi):
        return jnp.concatenate([lo, hi], axis=-1)

    def b_side(z):
        return [halves(jnp.real(z), jnp.real(z)), halves(-jnp.imag(z), jnp.imag(z))]

    def c_side(z):
        return [halves(jnp.real(z), jnp.imag(z)), halves(jnp.imag(z), jnp.real(z))]

    time_terms = jnp.stack(
        b_side(npw)
        + b_side(pw[:, chunk - 1::-1])
        + c_side(pw[:, :chunk])
        + c_side(pw[:, 1:]), axis=1)
    chan_terms = jnp.stack(
        [halves(jnp.real(b_t), jnp.imag(b_t)), halves(jnp.imag(b_t), jnp.real(b_t)),
         halves(jnp.real(c_mat), -jnp.real(c_mat)), halves(-jnp.imag(c_mat), -jnp.imag(c_mat))],
        axis=1)
    n_steps = max(1, (n_chunks - 1).bit_length())
    powers = []
    cur = pw[:, chunk, :]
    for _ in range(n_steps):
        powers.append(cur)
        cur = cur * cur
    lam_pow = jnp.stack(powers, axis=1)
    a1 = halves(jnp.real(lam_pow), jnp.real(lam_pow))
    a2 = halves(-jnp.imag(lam_pow), jnp.imag(lam_pow))
    return time_terms, chan_terms, a1, a2


def _ssm_kernel(u_ref, time_ref, chan_ref, a1_ref, a2_ref, perm_ref, perm_t_ref, y_ref,
                u8_ref, y8_ref, toep_ref, *, n_chunks, n_steps, group):
    seq = u_ref.shape[0]
    tiles = seq // SSM_FOLD
    lc = toep_ref.shape[0]
    per_chunk = (lc // group) // SSM_FOLD
    p2 = a1_ref.shape[2]
    group_shift = group.bit_length() - 1
    nt = (((1,), (1,)), ((), ()))

    folded = jnp.concatenate(
        [u_ref[pl.ds(s, tiles, stride=SSM_FOLD), :] for s in range(SSM_FOLD)], axis=1)
    regrouped = jnp.dot(folded.astype(BF16), perm_ref[...], preferred_element_type=F32)
    n_groups = LANES // group
    for g in range(n_groups):
        u8_ref[g] = regrouped[:, g * LANES:(g + 1) * LANES]

    for g in range(n_groups):
        def table(t1, t2, v1, v2):
            full = (time_ref[g, t1][:, None, :] * chan_ref[g, v1][None, :, :]
                    + time_ref[g, t2][:, None, :] * chan_ref[g, v2][None, :, :])
            return full.reshape(lc, p2).astype(BF16)

        src = table(0, 1, 0, 1)
        to_state = table(2, 3, 0, 1)
        dst = table(4, 5, 2, 3)
        from_state = table(6, 7, 2, 3)
        cb = min(256, lc)
        for j in range(lc // cb):
            blk = lax.dot_general(src, dst[j * cb:(j + 1) * cb], nt, preferred_element_type=F32)
            s_idx = lax.broadcasted_iota(I32, (lc, cb), 0) >> group_shift
            t_idx = (lax.broadcasted_iota(I32, (lc, cb), 1) + j * cb) >> group_shift
            toep_ref[:, j * cb:(j + 1) * cb] = jnp.where(s_idx <= t_idx, blk, 0.0).astype(BF16)
        u = jnp.concatenate(
            [u8_ref[g, pl.ds(tau, n_chunks, stride=per_chunk), :] for tau in range(per_chunk)],
            axis=1).astype(BF16)
        y = jnp.dot(u, toep_ref[...], preferred_element_type=F32)
        z = jnp.dot(u, to_state, preferred_element_type=F32)
        n = lax.broadcasted_iota(I32, z.shape, 0)
        x = jnp.where(n >= 1, pltpu.roll(z, 1, 0), 0.0)
        for k in range(n_steps):
            sh = 1 << k
            xs = jnp.where(n >= sh, pltpu.roll(x, sh, 0), 0.0)
            x = (x + a1_ref[g, k:k + 1, :] * xs
                 + a2_ref[g, k:k + 1, :] * pltpu.roll(xs, p2 // 2, 1))
        y = y + lax.dot_general(x.astype(BF16), from_state, nt, preferred_element_type=F32)
        for tau in range(per_chunk):
            y8_ref[g, pl.ds(tau, n_chunks, stride=per_chunk), :] = y[:, tau * LANES:(tau + 1) * LANES]

    y8 = jnp.concatenate([y8_ref[g] for g in range(n_groups)], axis=1)
    unfolded = jnp.dot(y8.astype(BF16), perm_t_ref[...], preferred_element_type=F32)
    for s in range(SSM_FOLD):
        y_ref[pl.ds(s, tiles, stride=SSM_FOLD), :] = unfolded[:, s * LANES:(s + 1) * LANES]


def _ssm(u, bsz, seq, lam_re, lam_im, log_dt, b_re, b_im, c_re, c_im):
    g, p = lam_re.shape
    c = b_re.shape[-1]
    chunk = min(SSM_CHUNK, seq)
    n_chunks = seq // chunk
    assert n_chunks & (n_chunks - 1) == 0 and c & (c - 1) == 0, (n_chunks, c)
    assert SSM_FOLD * c == LANES and chunk % SSM_FOLD == 0
    lc = chunk * c
    per_block = LANES // c
    time_terms, chan_terms, a1, a2 = _ssm_params(
        lam_re, lam_im, log_dt, b_re, b_im, c_re, c_im, chunk, n_chunks)
    n_steps = a1.shape[1]
    src_idx = jnp.arange(SSM_FOLD * LANES, dtype=I32)
    s_of, g_of, c_of = src_idx // LANES, (src_idx % LANES) // c, src_idx % c
    dst_idx = g_of * LANES + s_of * c + c_of
    perm = (dst_idx[:, None] == src_idx[None, :]).astype(BF16)
    block = lambda cb, b: (cb, 0, 0)
    fixed = lambda cb, b: (0, 0)
    y = pl.pallas_call(
        functools.partial(_ssm_kernel, n_chunks=n_chunks, n_steps=n_steps, group=c),
        grid=(g // per_block, bsz),
        in_specs=[
            pl.BlockSpec((None, seq, LANES), lambda cb, b: (b, 0, cb)),
            pl.BlockSpec((per_block,) + time_terms.shape[1:], lambda cb, b: (cb, 0, 0, 0)),
            pl.BlockSpec((per_block,) + chan_terms.shape[1:], lambda cb, b: (cb, 0, 0, 0)),
            pl.BlockSpec((per_block, n_steps, 2 * p), block),
            pl.BlockSpec((per_block, n_steps, 2 * p), block),
            pl.BlockSpec((SSM_FOLD * LANES, SSM_FOLD * LANES), fixed),
            pl.BlockSpec((SSM_FOLD * LANES, SSM_FOLD * LANES), fixed),
        ],
        out_specs=pl.BlockSpec((None, seq, LANES), lambda cb, b: (b, 0, cb)),
        out_shape=jax.ShapeDtypeStruct((bsz, seq, g * c), F32),
        scratch_shapes=[
            pltpu.VMEM((per_block, seq // SSM_FOLD, LANES), F32),
            pltpu.VMEM((per_block, seq // SSM_FOLD, LANES), F32),
            pltpu.VMEM((lc, lc), BF16),
        ],
        compiler_params=pltpu.CompilerParams(
            dimension_semantics=("arbitrary", "arbitrary"), vmem_limit_bytes=56 * MIB),
        name="s5_chunked_scan",
    )(u.reshape(bsz, seq, g * c), time_terms, chan_terms, a1, a2, perm, perm.T)
    return y.reshape(bsz * seq, g * c)


def _post_kernel(x_ref, ysb_ref, yss_ref, u_ref, d_ref, wglu_ref, gsb_ref, gssm_ref, wout_ref,
                 ln2_ref, wr_ref, br_ref, tri_ref, x1_ref, xn_ref, idx_ref, gate_ref, rank_ref,
                 count_ref, running_ref, wglu_bf, wout_bf, *, sb, n_exp):
    @pl.when(pl.program_id(0) == 0)
    def _():
        running_ref[...] = jnp.zeros_like(running_ref)
        wglu_bf[...] = wglu_ref[...].astype(BF16)
        wout_bf[...] = wout_ref[...].astype(BF16)

    u = u_ref[...].astype(F32)
    y = yss_ref[...].astype(F32) + d_ref[...] * u
    y = y * (0.5 * (1.0 + jnp.tanh(math.sqrt(2.0 / math.pi) * (y + 0.044715 * (y * y * y)))))
    ab = jnp.dot(y.astype(BF16), wglu_bf[...], preferred_element_type=F32)
    w = ab.shape[1] // 2
    y_ssm = ab[:, :w] * (1.0 / (1.0 + jnp.exp(-ab[:, w:])))
    m_sb = _rms(ysb_ref[...].astype(F32), gsb_ref[...])
    m_ssm = _rms(y_ssm, gssm_ref[...])
    x1 = (x_ref[...]
          + jnp.dot(m_sb.astype(BF16), wout_bf[:sb, :], preferred_element_type=F32)
          + jnp.dot(m_ssm.astype(BF16), wout_bf[sb:, :], preferred_element_type=F32))
    x1_ref[...] = x1
    xn = _rms(x1, ln2_ref[...])
    xn_ref[...] = xn
    def split(a):
        hi = a.astype(BF16)
        return hi, (a - hi.astype(F32)).astype(BF16)

    xn_hi, xn_lo = split(xn)
    wr_hi, wr_lo = split(wr_ref[...])
    logits = (jnp.dot(xn_hi, wr_hi, preferred_element_type=F32)
              + jnp.dot(xn_hi, wr_lo, preferred_element_type=F32)
              + jnp.dot(xn_lo, wr_hi, preferred_element_type=F32)) + br_ref[...]
    lane = lax.broadcasted_iota(I32, logits.shape, 1).astype(F32)
    out_lane = lax.broadcasted_iota(I32, (logits.shape[0], LANES), 1)
    idx_out = jnp.zeros((logits.shape[0], LANES), F32)
    val_out = jnp.zeros((logits.shape[0], LANES), F32)
    top = None
    denom = None
    work = logits
    chosen = []
    for k in range(TOP_K):
        m = jnp.max(work, axis=-1, keepdims=True)
        sel = jnp.min(jnp.where(work == m, lane, float(n_exp)), axis=-1, keepdims=True)
        hit = lane == sel
        chosen.append(hit)
        work = jnp.where(hit, -jnp.inf, work)
        if k == 0:
            top = m
        e = jnp.exp(m - top)
        denom = e if denom is None else denom + e
        idx_out = jnp.where(out_lane == k, sel, idx_out)
        val_out = jnp.where(out_lane == k, e, val_out)
    idx_ref[...] = idx_out[:, :TOP_K].astype(I32)
    gate_ref[...] = (val_out / denom)[:, :TOP_K]
    member = functools.reduce(jnp.logical_or, chosen).astype(BF16)
    before = running_ref[...] + jnp.dot(tri_ref[...], member, preferred_element_type=F32)
    rank_out = jnp.zeros((logits.shape[0], LANES), F32)
    for k in range(TOP_K):
        rk = jnp.sum(jnp.where(chosen[k], before, 0.0), axis=-1, keepdims=True)
        rank_out = jnp.where(out_lane == k, rk, rank_out)
    rank_ref[...] = rank_out[:, :TOP_K].astype(I32)
    running = running_ref[...] + jnp.sum(member.astype(F32), axis=0, keepdims=True)
    running_ref[...] = running
    count_ref[...] = running


def _post(x2, y_sb, y_ss, u, ssm_d, w_glu, g_sb, g_ssm, w_out, ln2_g, w_router, b_router):
    t, d = x2.shape
    sb = y_sb.shape[1]
    w = y_ss.shape[1]
    n_exp = w_router.shape[1]
    tm = min(TOKEN_TILE, t)
    row = lambda i: (i, 0)
    fixed = lambda i: (0, 0)
    earlier = (lax.broadcasted_iota(I32, (tm, tm), 1)
               < lax.broadcasted_iota(I32, (tm, tm), 0)).astype(BF16)
    return pl.pallas_call(
        functools.partial(_post_kernel, sb=sb, n_exp=n_exp),
        grid=(t // tm,),
        in_specs=[
            pl.BlockSpec((tm, d), row),
            pl.BlockSpec((tm, sb), row),
            pl.BlockSpec((tm, w), row),
            pl.BlockSpec((tm, w), row),
            pl.BlockSpec((1, w), fixed),
            pl.BlockSpec((w, 2 * w), fixed),
            pl.BlockSpec((1, sb), fixed),
            pl.BlockSpec((1, w), fixed),
            pl.BlockSpec((sb + w, d), fixed),
            pl.BlockSpec((1, d), fixed),
            pl.BlockSpec((d, n_exp), fixed),
            pl.BlockSpec((1, n_exp), fixed),
            pl.BlockSpec((tm, tm), fixed),
        ],
        out_specs=[
            pl.BlockSpec((tm, d), row),
            pl.BlockSpec((tm, d), row),
            pl.BlockSpec((tm, TOP_K), row),
            pl.BlockSpec((tm, TOP_K), row),
            pl.BlockSpec((tm, TOP_K), row),
            pl.BlockSpec((1, n_exp), fixed),
        ],
        out_shape=[
            jax.ShapeDtypeStruct((t, d), F32),
            jax.ShapeDtypeStruct((t, d), F32),
            jax.ShapeDtypeStruct((t, TOP_K), I32),
            jax.ShapeDtypeStruct((t, TOP_K), F32),
            jax.ShapeDtypeStruct((t, TOP_K), I32),
            jax.ShapeDtypeStruct((1, n_exp), F32),
        ],
        scratch_shapes=[pltpu.VMEM((1, n_exp), F32), pltpu.VMEM((w, 2 * w), BF16),
                        pltpu.VMEM((sb + w, d), BF16)],
        compiler_params=pltpu.CompilerParams(
            dimension_semantics=("arbitrary",), vmem_limit_bytes=48 * MIB),
        name="post_mixer_router",
    )(x2, y_sb, y_ss, u, ssm_d.reshape(1, w), w_glu, g_sb.reshape(1, sb),
      g_ssm.reshape(1, w), w_out, ln2_g.reshape(1, d), w_router,
      b_router.reshape(1, n_exp), earlier)


def _plan(top_idx, rank, counts, rows_per_block):
    t, k = top_idx.shape
    n_exp = counts.shape[-1]
    counts = counts.reshape(n_exp).astype(I32)
    padded = ((counts + rows_per_block - 1) // rows_per_block) * rows_per_block
    pad_ends = jnp.cumsum(padded)
    pad_starts = pad_ends - padded
    start_of = jnp.sum(jnp.where(top_idx[..., None] == jnp.arange(n_exp, dtype=I32),
                                 pad_starts, 0), axis=-1)
    dest = (start_of + rank).astype(I32)
    n_rows = t * k + n_exp * rows_per_block
    n_blocks = n_rows // rows_per_block
    block_start = jnp.arange(n_blocks, dtype=I32) * rows_per_block
    block_expert = jnp.minimum(
        jnp.sum((pad_ends[None, :] <= block_start[:, None]).astype(I32), axis=1), n_exp - 1)
    n_used = (pad_ends[-1] // rows_per_block).astype(I32).reshape(1)
    fill_start = jnp.concatenate([jnp.maximum(pad_ends - rows_per_block, 0).astype(I32), n_used])
    experts = jnp.arange(n_exp, dtype=I32)
    nonempty = counts > 0
    order = jnp.cumsum(nonempty.astype(I32)) - nonempty.astype(I32)
    later_ne = jnp.where((experts[None, :] > experts[:, None]) & nonempty[None, :],
                         experts[None, :], n_exp)
    next_tbl = jnp.min(later_ne, axis=1)
    onehot = block_expert[:, None] == experts[None, :]
    w_slot = jnp.sum(jnp.where(onehot, order & 1, 0), axis=1).astype(I32)
    w_next = jnp.sum(jnp.where(onehot, next_tbl, 0), axis=1).astype(I32)
    return dest, (block_expert, n_used, w_slot, w_next), fill_start, n_blocks


def _dispatch_kernel(fill_ref, dest_hbm, xn_ref, xs_hbm, idx_smem, zeros_ref, idx_sem, row_sem,
                     fill_sem, *, tile, n_exp, fill_rows):
    i = pl.program_id(0)
    n = pl.num_programs(0)
    per_tile = TOP_K * tile

    def idx_copy(b):
        src = dest_hbm.at[pl.ds(pl.multiple_of(b * per_tile, per_tile), per_tile)]
        dst = idx_smem.at[pl.ds(pl.multiple_of((b % 2) * per_tile, per_tile), per_tile)]
        return pltpu.make_async_copy(src, dst, idx_sem.at[b % 2])

    @pl.when(i == 0)
    def _():
        idx_copy(0).start()
        zeros_ref[...] = jnp.zeros_like(zeros_ref)
        for e in range(n_exp):
            start = pl.multiple_of(fill_ref[e], 8)
            pltpu.make_async_copy(zeros_ref, xs_hbm.at[pl.ds(start, fill_rows), 0, :],
                                  fill_sem).start()
        for e in range(n_exp):
            pltpu.make_async_copy(zeros_ref, xs_hbm.at[pl.ds(0, fill_rows), 0, :],
                                  fill_sem).wait()
        n_blocks = xs_hbm.shape[0] // fill_rows

        def fill_unused(b, carry):
            cp = pltpu.make_async_copy(
                zeros_ref, xs_hbm.at[pl.ds(pl.multiple_of(b * fill_rows, fill_rows), fill_rows), 0, :],
                fill_sem)
            cp.start()
            cp.wait()
            return carry

        lax.fori_loop(fill_ref[n_exp], n_blocks, fill_unused, 0)

    idx_copy(i).wait()

    @pl.when(i + 1 < n)
    def _():
        idx_copy(i + 1).start()

    base = pl.multiple_of((i % 2) * per_tile, per_tile)
    quarter = tile // TOP_K
    for part in range(TOP_K):
        def issue(g, carry, part=part):
            group = part * (quarter // SUBLANES) + g
            for s in range(SUBLANES):
                for k in range(TOP_K):
                    dst_row = idx_smem[base + (group * SUBLANES + s) * TOP_K + k]
                    pltpu.make_async_copy(xn_ref.at[group, pl.ds(s, 1), :], xs_hbm.at[dst_row],
                                          row_sem.at[part]).start(priority=k % 2)
            return carry

        lax.fori_loop(0, quarter // SUBLANES, issue, 0)
    for part in range(TOP_K):
        pltpu.make_async_copy(xn_ref, xn_ref, row_sem.at[part]).wait()


def _dispatch(xn, dest, fill_start, n_rows):
    t, d = xn.shape
    tile = min(DISPATCH_TILE, t)
    n_exp = fill_start.shape[0] - 1
    grid_spec = pltpu.PrefetchScalarGridSpec(
        num_scalar_prefetch=1,
        grid=(t // tile,),
        in_specs=[
            pl.BlockSpec(memory_space=pl.ANY),
            pl.BlockSpec((tile // SUBLANES, SUBLANES, d), lambda i, fs: (i, 0, 0)),
        ],
        out_specs=pl.BlockSpec(memory_space=pl.ANY),
        scratch_shapes=[
            pltpu.SMEM((2 * TOP_K * tile,), I32),
            pltpu.VMEM((EXPERT_ROWS, d), F32),
            pltpu.SemaphoreType.DMA((2,)),
            pltpu.SemaphoreType.DMA((TOP_K,)),
            pltpu.SemaphoreType.DMA,
        ],
    )
    return pl.pallas_call(
        functools.partial(_dispatch_kernel, tile=tile, n_exp=n_exp, fill_rows=EXPERT_ROWS),
        grid_spec=grid_spec,
        out_shape=jax.ShapeDtypeStruct((n_rows, 1, d), F32),
        compiler_params=pltpu.CompilerParams(
            dimension_semantics=("arbitrary",), vmem_limit_bytes=40 * MIB),
        name="moe_dispatch",
    )(fill_start, dest.reshape(t * TOP_K), xn.reshape(t // SUBLANES, SUBLANES, d))


def _expert_kernel(be_ref, nused_ref, wslot_ref, wnext_ref, xs_hbm, wg_hbm, bg_ref, wu_hbm, bu_ref,
                   wd_hbm, bd_ref, y_hbm, xbuf, ybuf, x_sem, y_sem, wg32, wu32, wd32, w_sem,
                   wg_bf, wu_bf, wd_bf, *, rows, n_exp):
    i = pl.program_id(0)
    n = nused_ref[0]

    def w_copies(e, slot):
        return [pltpu.make_async_copy(src.at[e], dst.at[slot], w_sem.at[slot])
                for src, dst in ((wg_hbm, wg32), (wu_hbm, wu32), (wd_hbm, wd32))]

    def x_copy(b):
        return pltpu.make_async_copy(
            xs_hbm.at[pl.ds(pl.multiple_of(b * rows, rows), rows), 0, :], xbuf.at[b % 2],
            x_sem.at[b % 2])

    def y_copy(b):
        return pltpu.make_async_copy(
            ybuf.at[b % 2], y_hbm.at[pl.ds(pl.multiple_of(b * rows, rows), rows), 0, :],
            y_sem.at[b % 2])

    @pl.when(i < n)
    def _():
        @pl.when(i == 0)
        def _():
            x_copy(0).start()
            for cp in w_copies(be_ref[0], wslot_ref[0]):
                cp.start()

        @pl.when(i + 1 < n)
        def _():
            x_copy(i + 1).start()

        changed = jnp.logical_or(i == 0, be_ref[i] != be_ref[jnp.maximum(i - 1, 0)])

        @pl.when(changed)
        def _():
            slot = wslot_ref[i]
            for cp in w_copies(be_ref[i], slot):
                cp.wait()

            @pl.when(wnext_ref[i] < n_exp)
            def _():
                for cp in w_copies(wnext_ref[i], 1 - slot):
                    cp.start()
            wg_bf[...] = wg32[slot].astype(BF16)
            wu_bf[...] = wu32[slot].astype(BF16)
            wd_bf[...] = wd32[slot].astype(BF16)

        x_copy(i).wait()
        x = xbuf[i % 2].astype(BF16)
        gate = jnp.dot(x, wg_bf[...], preferred_element_type=F32) + bg_ref[...]
        up = jnp.dot(x, wu_bf[...], preferred_element_type=F32) + bu_ref[...]
        gate = jnp.minimum(gate, SWIGLU_LIMIT)
        up = jnp.clip(up, -SWIGLU_LIMIT, SWIGLU_LIMIT)
        glu = gate * (1.0 / (1.0 + jnp.exp(-SWIGLU_ALPHA * gate)))
        hidden = ((up + 1.0) * glu).astype(BF16)
        y = jnp.dot(hidden, wd_bf[...], preferred_element_type=F32) + bd_ref[...]

        @pl.when(i >= 2)
        def _():
            y_copy(i - 2).wait()
        ybuf[i % 2] = y
        y_copy(i).start()

        @pl.when(i == n - 1)
        def _():
            y_copy(i).wait()

            @pl.when(i >= 1)
            def _():
                y_copy(i - 1).wait()

    @pl.when(i >= n)
    def _():
        ybuf[i % 2] = jnp.zeros((rows, ybuf.shape[2]), F32)
        y_copy(i).start()
        y_copy(i).wait()


def _experts(x_sorted, block_plan, n_blocks, w_gate, b_gate, w_up, b_up, w_down, b_down):
    n_rows, _, d = x_sorted.shape
    n_exp, _, f = w_gate.shape
    rows = EXPERT_ROWS
    bmap = lambda i, be, nu, ws, wn: (be[i], 0, 0)
    hbm = pl.BlockSpec(memory_space=pl.ANY)
    grid_spec = pltpu.PrefetchScalarGridSpec(
        num_scalar_prefetch=4,
        grid=(n_blocks,),
        in_specs=[
            hbm,
            hbm, pl.BlockSpec((None, 1, f), bmap),
            hbm, pl.BlockSpec((None, 1, f), bmap),
            hbm, pl.BlockSpec((None, 1, d), bmap),
        ],
        out_specs=hbm,
        scratch_shapes=[
            pltpu.VMEM((2, rows, d), F32),
            pltpu.VMEM((2, rows, d), F32),
            pltpu.SemaphoreType.DMA((2,)),
            pltpu.SemaphoreType.DMA((2,)),
            pltpu.VMEM((2, d, f), F32),
            pltpu.VMEM((2, d, f), F32),
            pltpu.VMEM((2, f, d), F32),
            pltpu.SemaphoreType.DMA((2,)),
            pltpu.VMEM((d, f), BF16),
            pltpu.VMEM((d, f), BF16),
            pltpu.VMEM((f, d), BF16),
        ],
    )
    return pl.pallas_call(
        functools.partial(_expert_kernel, rows=rows, n_exp=n_exp),
        grid_spec=grid_spec,
        out_shape=jax.ShapeDtypeStruct((n_rows, 1, d), F32),
        compiler_params=pltpu.CompilerParams(
            dimension_semantics=("arbitrary",), vmem_limit_bytes=52 * MIB),
        name="moe_experts",
    )(*block_plan, x_sorted,
      w_gate, b_gate.reshape(n_exp, 1, f), w_up, b_up.reshape(n_exp, 1, f),
      w_down, b_down.reshape(n_exp, 1, d))


def _combine_kernel(dest_hbm, y_hbm, x1_ref, gate_ref, lnf_ref, o_ref, idx_smem, ybuf,
                    idx_sem, row_sem, *, tile):
    i = pl.program_id(0)
    n = pl.num_programs(0)
    per_tile = TOP_K * tile

    def slot_base(b):
        return pl.multiple_of((b % 3) * per_tile, per_tile)

    def idx_copy(b):
        src = dest_hbm.at[pl.ds(pl.multiple_of(b * per_tile, per_tile), per_tile)]
        return pltpu.make_async_copy(src, idx_smem.at[pl.ds(slot_base(b), per_tile)],
                                     idx_sem.at[b % 3])

    def gather(b):
        base = slot_base(b)
        buf = ybuf.at[b % 2]
        sem = row_sem.at[b % 2]

        def issue(g, carry):
            for s in range(SUBLANES):
                pltpu.make_async_copy(y_hbm.at[idx_smem[base + g * SUBLANES + s]],
                                      buf.at[g, pl.ds(s, 1), :], sem).start(priority=s % 2)
            return carry

        lax.fori_loop(0, per_tile // SUBLANES, issue, 0)

    @pl.when(i == 0)
    def _():
        idx_copy(0).start()
        idx_copy(0).wait()
        gather(0)

        @pl.when(n > 1)
        def _():
            idx_copy(1).start()

    @pl.when(i + 1 < n)
    def _():
        idx_copy(i + 1).wait()

        @pl.when(i + 2 < n)
        def _():
            idx_copy(i + 2).start()
        gather(i + 1)

    cur = i % 2
    pltpu.make_async_copy(ybuf.at[1 - cur], ybuf.at[cur], row_sem.at[cur]).wait()
    gates = gate_ref[...]
    x = x1_ref[...]
    groups = tile // SUBLANES
    for k in range(TOP_K):
        yk = ybuf[cur, k * groups:(k + 1) * groups].reshape(tile, x.shape[1])
        x = x + gates[:, k:k + 1] * yk
    o_ref[...] = _rms(x, lnf_ref[...])


def _combine(dest, y_rows, x1, gates, ln_f_g):
    t, d = x1.shape
    tile = min(COMBINE_TILE, t)
    n_tiles = t // tile
    dest_tiles = dest.reshape(n_tiles, tile, TOP_K).transpose(0, 2, 1).reshape(n_tiles * TOP_K * tile)
    return pl.pallas_call(
        functools.partial(_combine_kernel, tile=tile),
        grid=(n_tiles,),
        in_specs=[
            pl.BlockSpec(memory_space=pl.ANY),
            pl.BlockSpec(memory_space=pl.ANY),
            pl.BlockSpec((tile, d), lambda i: (i, 0)),
            pl.BlockSpec((tile, TOP_K), lambda i: (i, 0)),
            pl.BlockSpec((1, d), lambda i: (0, 0)),
        ],
        out_specs=pl.BlockSpec((tile, d), lambda i: (i, 0)),
        out_shape=jax.ShapeDtypeStruct((t, d), F32),
        scratch_shapes=[
            pltpu.SMEM((3 * TOP_K * tile,), I32),
            pltpu.VMEM((2, TOP_K * tile // SUBLANES, SUBLANES, d), F32),
            pltpu.SemaphoreType.DMA((3,)),
            pltpu.SemaphoreType.DMA((2,)),
        ],
        compiler_params=pltpu.CompilerParams(
            dimension_semantics=("arbitrary",), vmem_limit_bytes=40 * MIB),
        name="moe_combine",
    )(dest_tiles, y_rows, x1, gates, ln_f_g.reshape(1, d))


def kernel(x, ln1_g, w_in, lam_re, lam_im, log_dt, ssm_b_re, ssm_b_im, ssm_c_re, ssm_c_im,
           ssm_d, w_glu, g_sb, g_ssm, w_out, ln2_g, w_router, b_router, w_gate, b_gate,
           w_up, b_up, w_down, b_down, ln_f_g):
    bsz, seq, d = x.shape
    assert ln1_g.shape[0] == 1, "depth-1 block only"
    ssm_w = ssm_d.shape[1]
    sb = g_sb.shape[1]
    x2 = x.reshape(bsz * seq, d)
    q, k, v, u = _in_proj(x2, ln1_g[0], w_in[0], sb, ssm_w)
    y_sb = _attention(q, k, v, bsz, seq)
    y_ss = _ssm(u, bsz, seq, lam_re[0], lam_im[0], log_dt[0], ssm_b_re[0], ssm_b_im[0],
                ssm_c_re[0], ssm_c_im[0])
    x1, xn, top_idx, gates, rank, counts = _post(
        x2, y_sb, y_ss, u, ssm_d[0], w_glu[0], g_sb[0], g_ssm[0], w_out[0], ln2_g[0],
        w_router[0], b_router[0])
    dest, block_plan, fill_start, n_blocks = _plan(top_idx, rank, counts, EXPERT_ROWS)
    x_sorted = _dispatch(xn, dest, fill_start, n_blocks * EXPERT_ROWS)
    y_rows = _experts(x_sorted, block_plan, n_blocks, w_gate[0], b_gate[0],
                      w_up[0], b_up[0], w_down[0], b_down[0])
    out = _combine(dest, y_rows, x1, gates, ln_f_g)
    return out.reshape(bsz, seq, d)
```

```python
import functools
import math

import jax
import jax.numpy as jnp
from jax import lax
from jax.experimental import pallas as pl
from jax.experimental.pallas import tpu as pltpu

F32 = jnp.float32
BF16 = jnp.bfloat16
I32 = jnp.int32

EPS = 1e-5
SB_HEAD_DIM = 64
SSM_GROUP = 16
SSM_STATE = 64
TOP_K = 4
SWIGLU_LIMIT = 7.0
SWIGLU_ALPHA = 1.702

LANES = 128
SUBLANES = 8
HEADS_PER_BLOCK = LANES // SB_HEAD_DIM
ATTN_BLOCK = 256
ATTN_EXP_FLOOR = 160.0
SSM_CHUNK = 16
SSM_FOLD = 8
TOKEN_TILE = 512
EXPERT_ROWS = 256
COMBINE_TILE = 256
SPARE_BLOCKS = 2
MIB = 1024 * 1024
LOG2E = 1.4426950408889634


def _rms(x, g):
    return x * lax.rsqrt(jnp.mean(x * x, axis=-1, keepdims=True) + EPS) * g


def _in_proj_kernel(x_ref, g_ref, w_ref, q_ref, k_ref, v_ref, u_ref, w_bf, *, sb, scale):
    @pl.when(pl.program_id(0) == 0)
    def _():
        w_bf[...] = w_ref[...].astype(BF16)

    h = _rms(x_ref[...], g_ref[...])
    proj = jnp.dot(h.astype(BF16), w_bf[...], preferred_element_type=F32)
    k_ref[...] = proj[:, sb:2 * sb].astype(BF16)
    u_ref[...] = proj[:, 3 * sb:]
    lane = lax.broadcasted_iota(I32, (1, LANES), 1)
    q = (proj[:, :sb] * scale).astype(BF16)
    v = proj[:, 2 * sb:3 * sb].astype(BF16)
    zero = jnp.zeros((), BF16)
    for pair in range(sb // LANES):
        cols = slice(pair * LANES, (pair + 1) * LANES)
        for head in range(HEADS_PER_BLOCK):
            own = (lane // SB_HEAD_DIM) == head
            out = slice((pair * HEADS_PER_BLOCK + head) * LANES,
                        (pair * HEADS_PER_BLOCK + head + 1) * LANES)
            q_ref[:, out] = jnp.where(own, q[:, cols], zero)
            v_ref[:, out] = jnp.where(own, v[:, cols], zero)


def _in_proj(x2, ln1_g, w_in, sb, ssm_w):
    t, d = x2.shape
    n_in = w_in.shape[1]
    tm = min(TOKEN_TILE, t)
    scale = LOG2E / math.sqrt(SB_HEAD_DIM)
    return pl.pallas_call(
        functools.partial(_in_proj_kernel, sb=sb, scale=scale),
        grid=(t // tm,),
        in_specs=[
            pl.BlockSpec((tm, d), lambda i: (i, 0)),
            pl.BlockSpec((1, d), lambda i: (0, 0)),
            pl.BlockSpec((d, n_in), lambda i: (0, 0)),
        ],
        out_specs=[
            pl.BlockSpec((tm, HEADS_PER_BLOCK * sb), lambda i: (i, 0)),
            pl.BlockSpec((tm, sb), lambda i: (i, 0)),
            pl.BlockSpec((tm, HEADS_PER_BLOCK * sb), lambda i: (i, 0)),
            pl.BlockSpec((tm, ssm_w), lambda i: (i, 0)),
        ],
        out_shape=[
            jax.ShapeDtypeStruct((t, HEADS_PER_BLOCK * sb), BF16),
            jax.ShapeDtypeStruct((t, sb), BF16),
            jax.ShapeDtypeStruct((t, HEADS_PER_BLOCK * sb), BF16),
            jax.ShapeDtypeStruct((t, ssm_w), F32),
        ],
        scratch_shapes=[pltpu.VMEM((d, n_in), BF16)],
        compiler_params=pltpu.CompilerParams(
            dimension_semantics=("arbitrary",), vmem_limit_bytes=48 * MIB),
        name="in_proj",
    )(x2, ln1_g.reshape(1, d), w_in)


def _attn_kernel(ti_ref, tj_ref, q_ref, k_ref, v_ref, tri_ref, mask_ref, o_ref,
                 dbuf, spbuf, lbuf, rsbuf, acc_ref, r_ref, *, blk, n_blk, n_items):
    sign_bit = jnp.uint32(0x80000000)
    last = n_items - 1
    ALL, CAUSAL, NONE = 0, 1, 2

    for ref in (dbuf, spbuf, lbuf, rsbuf, acc_ref, r_ref):
        ref[...] = jnp.zeros_like(ref)

    def both_heads(ref, start):
        rows = pl.ds(pl.multiple_of(start, blk), blk)
        return jnp.concatenate([ref[rows, h * LANES:(h + 1) * LANES]
                                for h in range(HEADS_PER_BLOCK)], axis=0)

    def item(idx):
        c = jnp.clip(idx, 0, last)
        return ti_ref[c], tj_ref[c]

    half = n_blk // 2
    stream_end = (half * (half + 1) // 2, n_items)

    def step(state, x):
        cur, p1, p2, out_blk = list(state[:2]), state[2], state[3], list(state[4:])
        slot, other = x, 1 - x
        c1 = jnp.where(cur[x] < stream_end[x], cur[x], n_items)
        c3 = p2
        i1, j1 = item(c1)
        ks = k_ref[pl.ds(pl.multiple_of(j1 * blk, blk), blk), :]
        w = lax.dot_general(both_heads(q_ref, i1 * blk), ks, (((1,), (1,)), ((), ())),
                            preferred_element_type=F32)
        i3, j3 = item(c3)
        valid = c3 <= last
        first = jnp.logical_and(i3 == j3, valid)
        mask3 = mask_ref[jnp.where(valid, (i3 == j3).astype(I32), NONE)]
        i3 = jnp.where(valid, i3, out_blk[x])
        r_prev = jnp.where(first, 0.0, r_ref[x])
        r_new = r_prev + rsbuf[other]
        r_ref[x] = r_new
        done = jnp.logical_and(valid, jnp.min(r_new) >= ATTN_EXP_FLOOR)
        next_block = lax.shift_right_logical((i3 + 1) * (i3 + 2), 1)
        advanced = jnp.where(done, jnp.maximum(cur[x] + 1, next_block), cur[x] + 1)
        cur[x] = jnp.minimum(advanced, stream_end[x])
        out_blk[x] = i3
        r_wide = jnp.concatenate([r_prev] * (blk // LANES), axis=1)
        a = jnp.exp2(dbuf[slot] - (lbuf[other] + r_wide))
        ab = a.astype(BF16) * jnp.concatenate([mask3, mask3], axis=0)
        a_cat = jnp.concatenate([ab[:blk], ab[blk:]], axis=1)
        acc = jnp.where(first, 0.0, acc_ref[x]) + jnp.dot(
            a_cat, both_heads(v_ref, j3 * blk), preferred_element_type=F32)
        acc_ref[x] = acc
        o_ref[pl.ds(pl.multiple_of(i3 * blk, blk), blk), :] = acc.astype(o_ref.dtype)
        sums = jnp.dot(spbuf[other], tri_ref[...], preferred_element_type=F32)
        lbuf[slot] = sums[:, :blk]
        rsbuf[slot] = sums[:, blk:]
        mask1 = mask_ref[(i1 == j1).astype(I32)]
        neg_abs = lax.bitcast_convert_type(
            lax.bitcast_convert_type(w, jnp.uint32) | sign_bit, F32)
        sp2 = jnp.maximum(w, 0.0) + jnp.log(1.0 + jnp.exp2(neg_abs)) * LOG2E
        dbuf[slot] = w - sp2
        spbuf[slot] = sp2.astype(BF16) * jnp.concatenate([mask1, mask1], axis=0)
        return cur[0], cur[1], c1, p1, out_blk[0], out_blk[1]

    def in_flight(state):
        cur0, cur1, p1, p2 = state[:4]
        return functools.reduce(jnp.logical_or, [cur0 < stream_end[0], cur1 < stream_end[1],
                                                 p1 <= last, p2 <= last])

    def four_steps(state):
        for x in (0, 1, 0, 1):
            state = step(state, x)
        return state

    none = jnp.int32(n_items)
    lax.while_loop(in_flight, four_steps,
                   (jnp.int32(0), jnp.int32(stream_end[0]), none, none,
                    jnp.int32(0), jnp.int32(half)))


def _attention(q, k, v, bsz, seq):
    sb = k.shape[-1]
    blk = min(ATTN_BLOCK, seq)
    n_blk = seq // blk
    n_pairs = sb // LANES
    q3, k3, v3 = (a.reshape(bsz, seq, a.shape[-1]) for a in (q, k, v))
    per_head = HEADS_PER_BLOCK * LANES
    jj = lax.broadcasted_iota(I32, (blk, blk), 0)
    ss = lax.broadcasted_iota(I32, (blk, blk), 1)
    tri = jnp.concatenate([(jj > ss).astype(BF16), jnp.ones((blk, LANES), BF16)], axis=1)
    masks = jnp.stack([jnp.ones((blk, blk), BF16), (ss < jj).astype(BF16),
                       jnp.zeros((blk, blk), BF16)])
    items = [(i, j) for i in range(n_blk) for j in range(i, -1, -1)]
    item_i = jnp.asarray([i for i, _ in items], I32)
    item_j = jnp.asarray([j for _, j in items], I32)
    whole = lambda b, p, ti, tj: (b, 0, p)
    grid_spec = pltpu.PrefetchScalarGridSpec(
        num_scalar_prefetch=2,
        grid=(bsz, n_pairs),
        in_specs=[
            pl.BlockSpec((None, seq, per_head), whole),
            pl.BlockSpec((None, seq, LANES), whole),
            pl.BlockSpec((None, seq, per_head), whole),
            pl.BlockSpec((blk, blk + LANES), lambda b, p, ti, tj: (0, 0)),
            pl.BlockSpec((3, blk, blk), lambda b, p, ti, tj: (0, 0, 0)),
        ],
        out_specs=pl.BlockSpec((None, seq, LANES), whole),
        scratch_shapes=[
            pltpu.VMEM((2, HEADS_PER_BLOCK * blk, blk), F32),
            pltpu.VMEM((2, HEADS_PER_BLOCK * blk, blk), BF16),
            pltpu.VMEM((2, HEADS_PER_BLOCK * blk, blk), F32),
            pltpu.VMEM((2, HEADS_PER_BLOCK * blk, LANES), F32),
            pltpu.VMEM((2, blk, LANES), F32),
            pltpu.VMEM((2, HEADS_PER_BLOCK * blk, LANES), F32),
        ],
    )
    out = pl.pallas_call(
        functools.partial(_attn_kernel, blk=blk, n_blk=n_blk, n_items=len(items)),
        grid_spec=grid_spec,
        out_shape=jax.ShapeDtypeStruct((bsz, seq, sb), BF16),
        compiler_params=pltpu.CompilerParams(
            dimension_semantics=("arbitrary", "arbitrary"), vmem_limit_bytes=40 * MIB),
        name="sb_attention",
    )(item_i, item_j, q3, k3, v3, tri, masks)
    return out.reshape(bsz * seq, sb)


def _ssm_params(lam_re, lam_im, log_dt, b_re, b_im, c_re, c_im, chunk, n_chunks):
    lam = lax.complex(lam_re.astype(F32), lam_im.astype(F32))
    dt = jnp.exp(log_dt.astype(F32))[:, None]
    lam_dt = lam * dt
    lam_bar = jnp.exp(lam_dt)
    b_bar = ((lam_bar - 1.0) / lam)[:, :, None] * lax.complex(b_re.astype(F32), b_im.astype(F32))
    b_t = jnp.swapaxes(b_bar, 1, 2)
    c_mat = lax.complex(c_re.astype(F32), c_im.astype(F32))
    steps = jnp.arange(chunk + 1, dtype=F32)
    pw = jnp.exp(lam_dt[:, None, :] * steps[None, :, None])
    npw = jnp.exp(-lam_dt[:, None, :] * steps[None, :chunk, None])

    def halves(lo, hi):
        return jnp.concatenate([lo, hi], axis=-1)

    def b_side(z):
        return [halves(jnp.real(z), jnp.real(z)), halves(-jnp.imag(z), jnp.imag(z))]

    def c_side(z):
        return [halves(jnp.real(z), jnp.imag(z)), halves(jnp.imag(z), jnp.real(z))]

    time_terms = jnp.stack(
        b_side(npw)
        + b_side(pw[:, chunk - 1::-1])
        + c_side(pw[:, :chunk])
        + c_side(pw[:, 1:]), axis=1)
    chan_terms = jnp.stack(
        [halves(jnp.real(b_t), jnp.imag(b_t)), halves(jnp.imag(b_t), jnp.real(b_t)),
         halves(jnp.real(c_mat), -jnp.real(c_mat)), halves(-jnp.imag(c_mat), -jnp.imag(c_mat))],
        axis=1)
    n_steps = max(1, (n_chunks - 1).bit_length())
    powers = []
    cur = pw[:, chunk, :]
    for _ in range(n_steps):
        powers.append(cur)
        cur = cur * cur
    lam_pow = jnp.stack(powers, axis=1)
    a1 = halves(jnp.real(lam_pow), jnp.real(lam_pow))
    a2 = halves(-jnp.imag(lam_pow), jnp.imag(lam_pow))
    return time_terms, chan_terms, a1, a2


def _ssm_kernel(u_ref, time_ref, chan_ref, a1_ref, a2_ref, perm_ref, perm_t_ref, y_ref,
                u8_ref, y8_ref, toep_ref, *, n_chunks, n_steps, group):
    seq = u_ref.shape[0]
    tiles = seq // SSM_FOLD
    lc = toep_ref.shape[0]
    per_chunk = (lc // group) // SSM_FOLD
    p2 = a1_ref.shape[2]
    group_shift = group.bit_length() - 1
    nt = (((1,), (1,)), ((), ()))

    folded = jnp.concatenate(
        [u_ref[pl.ds(s, tiles, stride=SSM_FOLD), :] for s in range(SSM_FOLD)], axis=1)
    regrouped = jnp.dot(folded.astype(BF16), perm_ref[...], preferred_element_type=F32)
    n_groups = LANES // group
    for g in range(n_groups):
        u8_ref[g] = regrouped[:, g * LANES:(g + 1) * LANES]

    for g in range(n_groups):
        def table(t1, t2, v1, v2):
            full = (time_ref[g, t1][:, None, :] * chan_ref[g, v1][None, :, :]
                    + time_ref[g, t2][:, None, :] * chan_ref[g, v2][None, :, :])
            return full.reshape(lc, p2).astype(BF16)

        src = table(0, 1, 0, 1)
        to_state = table(2, 3, 0, 1)
        dst = table(4, 5, 2, 3)
        from_state = table(6, 7, 2, 3)
        cb = min(256, lc)
        for j in range(lc // cb):
            blk = lax.dot_general(src, dst[j * cb:(j + 1) * cb], nt, preferred_element_type=F32)
            s_idx = lax.broadcasted_iota(I32, (lc, cb), 0) >> group_shift
            t_idx = (lax.broadcasted_iota(I32, (lc, cb), 1) + j * cb) >> group_shift
            toep_ref[:, j * cb:(j + 1) * cb] = jnp.where(s_idx <= t_idx, blk, 0.0).astype(BF16)
        u = jnp.concatenate(
            [u8_ref[g, pl.ds(tau, n_chunks, stride=per_chunk), :] for tau in range(per_chunk)],
            axis=1).astype(BF16)
        y = jnp.dot(u, toep_ref[...], preferred_element_type=F32)
        z = jnp.dot(u, to_state, preferred_element_type=F32)
        n = lax.broadcasted_iota(I32, z.shape, 0)
        x = jnp.where(n >= 1, pltpu.roll(z, 1, 0), 0.0)
        for k in range(n_steps):
            sh = 1 << k
            xs = jnp.where(n >= sh, pltpu.roll(x, sh, 0), 0.0)
            x = (x + a1_ref[g, k:k + 1, :] * xs
                 + a2_ref[g, k:k + 1, :] * pltpu.roll(xs, p2 // 2, 1))
        y = y + lax.dot_general(x.astype(BF16), from_state, nt, preferred_element_type=F32)
        for tau in range(per_chunk):
            y8_ref[g, pl.ds(tau, n_chunks, stride=per_chunk), :] = y[:, tau * LANES:(tau + 1) * LANES]

    y8 = jnp.concatenate([y8_ref[g] for g in range(n_groups)], axis=1)
    unfolded = jnp.dot(y8.astype(BF16), perm_t_ref[...], preferred_element_type=F32)
    for s in range(SSM_FOLD):
        y_ref[pl.ds(s, tiles, stride=SSM_FOLD), :] = unfolded[:, s * LANES:(s + 1) * LANES]


def _ssm(u, bsz, seq, lam_re, lam_im, log_dt, b_re, b_im, c_re, c_im):
    g, p = lam_re.shape
    c = b_re.shape[-1]
    chunk = min(SSM_CHUNK, seq)
    n_chunks = seq // chunk
    assert n_chunks & (n_chunks - 1) == 0 and c & (c - 1) == 0, (n_chunks, c)
    assert SSM_FOLD * c == LANES and chunk % SSM_FOLD == 0
    lc = chunk * c
    per_block = LANES // c
    time_terms, chan_terms, a1, a2 = _ssm_params(
        lam_re, lam_im, log_dt, b_re, b_im, c_re, c_im, chunk, n_chunks)
    n_steps = a1.shape[1]
    src_idx = jnp.arange(SSM_FOLD * LANES, dtype=I32)
    s_of, g_of, c_of = src_idx // LANES, (src_idx % LANES) // c, src_idx % c
    dst_idx = g_of * LANES + s_of * c + c_of
    perm = (dst_idx[:, None] == src_idx[None, :]).astype(BF16)
    block = lambda cb, b: (cb, 0, 0)
    fixed = lambda cb, b: (0, 0)
    y = pl.pallas_call(
        functools.partial(_ssm_kernel, n_chunks=n_chunks, n_steps=n_steps, group=c),
        grid=(g // per_block, bsz),
        in_specs=[
            pl.BlockSpec((None, seq, LANES), lambda cb, b: (b, 0, cb)),
            pl.BlockSpec((per_block,) + time_terms.shape[1:], lambda cb, b: (cb, 0, 0, 0)),
            pl.BlockSpec((per_block,) + chan_terms.shape[1:], lambda cb, b: (cb, 0, 0, 0)),
            pl.BlockSpec((per_block, n_steps, 2 * p), block),
            pl.BlockSpec((per_block, n_steps, 2 * p), block),
            pl.BlockSpec((SSM_FOLD * LANES, SSM_FOLD * LANES), fixed),
            pl.BlockSpec((SSM_FOLD * LANES, SSM_FOLD * LANES), fixed),
        ],
        out_specs=pl.BlockSpec((None, seq, LANES), lambda cb, b: (b, 0, cb)),
        out_shape=jax.ShapeDtypeStruct((bsz, seq, g * c), F32),
        scratch_shapes=[
            pltpu.VMEM((per_block, seq // SSM_FOLD, LANES), F32),
            pltpu.VMEM((per_block, seq // SSM_FOLD, LANES), F32),
            pltpu.VMEM((lc, lc), BF16),
        ],
        compiler_params=pltpu.CompilerParams(
            dimension_semantics=("arbitrary", "arbitrary"), vmem_limit_bytes=56 * MIB),
        name="s5_chunked_scan",
    )(u.reshape(bsz, seq, g * c), time_terms, chan_terms, a1, a2, perm, perm.T)
    return y.reshape(bsz * seq, g * c)


def _post_kernel(x_ref, ysb_ref, yss_ref, u_ref, d_ref, wglu_ref, gsb_ref, gssm_ref, wout_ref,
                 ln2_ref, wr_ref, br_ref, x1_ref, xn_hbm, idx_ref, gate_ref, count_ref,
                 running_ref, wglu_bf, wout_bf, xn_buf, xn_sem, *, sb, n_exp):
    step = pl.program_id(0)
    tm = x_ref.shape[0]

    def xn_copy(b):
        return pltpu.make_async_copy(
            xn_buf.at[b % 2], xn_hbm.at[pl.ds(pl.multiple_of(b * tm, tm), tm), 0, :],
            xn_sem.at[b % 2])

    @pl.when(step == 0)
    def _():
        running_ref[...] = jnp.zeros_like(running_ref)
        wglu_bf[...] = wglu_ref[...].astype(BF16)
        wout_bf[...] = wout_ref[...].astype(BF16)

    u = u_ref[...].astype(F32)
    y = yss_ref[...].astype(F32) + d_ref[...] * u
    y = y * (0.5 * (1.0 + jnp.tanh(math.sqrt(2.0 / math.pi) * (y + 0.044715 * (y * y * y)))))
    ab = jnp.dot(y.astype(BF16), wglu_bf[...], preferred_element_type=F32)
    w = ab.shape[1] // 2
    y_ssm = ab[:, :w] * (1.0 / (1.0 + jnp.exp(-ab[:, w:])))
    m_sb = _rms(ysb_ref[...].astype(F32), gsb_ref[...])
    m_ssm = _rms(y_ssm, gssm_ref[...])
    x1 = (x_ref[...]
          + jnp.dot(m_sb.astype(BF16), wout_bf[:sb, :], preferred_element_type=F32)
          + jnp.dot(m_ssm.astype(BF16), wout_bf[sb:, :], preferred_element_type=F32))
    x1_ref[...] = x1
    xn = _rms(x1, ln2_ref[...])

    @pl.when(step >= 2)
    def _():
        xn_copy(step - 2).wait()
    xn_buf[step % 2] = xn
    xn_copy(step).start()

    @pl.when(step == pl.num_programs(0) - 1)
    def _():
        xn_copy(step).wait()

        @pl.when(step >= 1)
        def _():
            xn_copy(step - 1).wait()
    def split(a):
        hi = a.astype(BF16)
        return hi, (a - hi.astype(F32)).astype(BF16)

    xn_hi, xn_lo = split(xn)
    wr_hi, wr_lo = split(wr_ref[...])
    logits = (jnp.dot(xn_hi, wr_hi, preferred_element_type=F32)
              + jnp.dot(xn_hi, wr_lo, preferred_element_type=F32)
              + jnp.dot(xn_lo, wr_hi, preferred_element_type=F32)) + br_ref[...]
    lane = lax.broadcasted_iota(I32, logits.shape, 1).astype(F32)
    out_lane = lax.broadcasted_iota(I32, (logits.shape[0], LANES), 1)
    idx_out = jnp.zeros((logits.shape[0], LANES), F32)
    val_out = jnp.zeros((logits.shape[0], LANES), F32)
    top = None
    denom = None
    work = logits
    chosen = []
    for k in range(TOP_K):
        m = jnp.max(work, axis=-1, keepdims=True)
        sel = jnp.min(jnp.where(work == m, lane, float(n_exp)), axis=-1, keepdims=True)
        hit = lane == sel
        chosen.append(hit)
        work = jnp.where(hit, -jnp.inf, work)
        if k == 0:
            top = m
        e = jnp.exp(m - top)
        denom = e if denom is None else denom + e
        idx_out = jnp.where(out_lane == k, sel, idx_out)
        val_out = jnp.where(out_lane == k, e, val_out)
    idx_ref[...] = idx_out[:, :TOP_K].astype(I32)
    gate_ref[...] = (val_out / denom)[:, :TOP_K]
    member = functools.reduce(jnp.logical_or, chosen)
    running = running_ref[...] + jnp.sum(member.astype(F32), axis=0, keepdims=True)
    running_ref[...] = running
    count_ref[...] = running


def _post(x2, y_sb, y_ss, u, ssm_d, w_glu, g_sb, g_ssm, w_out, ln2_g, w_router, b_router):
    t, d = x2.shape
    sb = y_sb.shape[1]
    w = y_ss.shape[1]
    n_exp = w_router.shape[1]
    tm = min(TOKEN_TILE, t)
    row = lambda i: (i, 0)
    fixed = lambda i: (0, 0)
    return pl.pallas_call(
        functools.partial(_post_kernel, sb=sb, n_exp=n_exp),
        grid=(t // tm,),
        in_specs=[
            pl.BlockSpec((tm, d), row),
            pl.BlockSpec((tm, sb), row),
            pl.BlockSpec((tm, w), row),
            pl.BlockSpec((tm, w), row),
            pl.BlockSpec((1, w), fixed),
            pl.BlockSpec((w, 2 * w), fixed),
            pl.BlockSpec((1, sb), fixed),
            pl.BlockSpec((1, w), fixed),
            pl.BlockSpec((sb + w, d), fixed),
            pl.BlockSpec((1, d), fixed),
            pl.BlockSpec((d, n_exp), fixed),
            pl.BlockSpec((1, n_exp), fixed),
        ],
        out_specs=[
            pl.BlockSpec((tm, d), row),
            pl.BlockSpec(memory_space=pl.ANY),
            pl.BlockSpec((tm, TOP_K), row),
            pl.BlockSpec((tm, TOP_K), row),
            pl.BlockSpec((1, n_exp), fixed),
        ],
        out_shape=[
            jax.ShapeDtypeStruct((t, d), F32),
            jax.ShapeDtypeStruct((t, 1, d), F32),
            jax.ShapeDtypeStruct((t, TOP_K), I32),
            jax.ShapeDtypeStruct((t, TOP_K), F32),
            jax.ShapeDtypeStruct((1, n_exp), F32),
        ],
        scratch_shapes=[pltpu.VMEM((1, n_exp), F32), pltpu.VMEM((w, 2 * w), BF16),
                        pltpu.VMEM((sb + w, d), BF16), pltpu.VMEM((2, tm, d), F32),
                        pltpu.SemaphoreType.DMA((2,))],
        compiler_params=pltpu.CompilerParams(
            dimension_semantics=("arbitrary",), vmem_limit_bytes=48 * MIB),
        name="post_mixer_router",
    )(x2, y_sb, y_ss, u, ssm_d.reshape(1, w), w_glu, g_sb.reshape(1, sb),
      g_ssm.reshape(1, w), w_out, ln2_g.reshape(1, d), w_router,
      b_router.reshape(1, n_exp))


def _plan(top_idx, counts, rows):
    t, k = top_idx.shape
    n_assign = t * k
    n_exp = counts.shape[-1]
    k_shift = k.bit_length() - 1
    experts = jnp.arange(n_exp, dtype=I32)
    counts = counts.reshape(n_exp).astype(I32)
    order = jnp.argsort(top_idx.reshape(n_assign), stable=True).astype(I32)
    starts = jnp.cumsum(counts) - counts
    padded = ((counts + rows - 1) // rows) * rows
    pad_ends = jnp.cumsum(padded)
    pad_starts = pad_ends - padded
    pad_before = jnp.cumsum(padded - counts) - (padded - counts)
    n_blocks = (n_assign + n_exp * rows) // rows
    block_start = jnp.arange(n_blocks, dtype=I32) * rows
    block_expert = jnp.minimum(
        jnp.sum((pad_ends[None, :] <= block_start[:, None]).astype(I32), axis=1), n_exp - 1)
    n_used = (pad_ends[-1] // rows).astype(I32).reshape(1)
    onehot = block_expert[:, None] == experts[None, :]
    pick = lambda table: jnp.sum(jnp.where(onehot, table[None, :], 0), axis=1)
    in_expert = (block_start - pick(pad_starts))[:, None] + jnp.arange(rows, dtype=I32)[None, :]
    valid = in_expert < pick(counts)[:, None]
    assign = order[jnp.clip(pick(starts)[:, None] + in_expert, 0, n_assign - 1)]
    token = lax.shift_right_logical(assign, k_shift)
    row_tok = jnp.where(valid, token, 0)
    row_out = jnp.where(valid, (assign & (k - 1)) * t + token,
                        n_assign + pick(pad_before)[:, None] + in_expert - pick(counts)[:, None])
    spare = n_assign + n_exp * rows + jnp.arange(SPARE_BLOCKS * rows, dtype=I32).reshape(-1, rows)
    info = jnp.concatenate([
        jnp.concatenate([row_tok, row_out], axis=1),
        jnp.concatenate([jnp.zeros_like(spare), spare], axis=1)]).reshape(-1).astype(I32)
    nonempty = counts > 0
    order_ne = jnp.cumsum(nonempty.astype(I32)) - nonempty.astype(I32)
    later_ne = jnp.where((experts[None, :] > experts[:, None]) & nonempty[None, :],
                         experts[None, :], n_exp)
    w_slot = pick(order_ne & 1).astype(I32)
    w_next = pick(jnp.min(later_ne, axis=1)).astype(I32)
    return info, (block_expert, n_used, w_slot, w_next), n_blocks


def _expert_kernel(be_ref, nused_ref, wslot_ref, wnext_ref, info_hbm, xn_hbm, wg_hbm, bg_ref,
                   wu_hbm, bu_ref, wd_hbm, bd_ref, y_hbm, info_smem, xbuf, ybuf, zeros_ref,
                   info_sem, x_sem, y_sem, fill_sem, wg32, wu32, wd32, w_sem, wg_bf, wu_bf, wd_bf,
                   *, rows, n_exp, n_blocks, n_real):
    i = pl.program_id(0)
    n = nused_ref[0]
    per = 2 * rows
    d = xbuf.shape[3]
    ring = 4

    def info_copy(b, slot):
        return pltpu.make_async_copy(info_hbm.at[pl.ds(pl.multiple_of(b * per, per), per)],
                                     info_smem.at[pl.ds(pl.multiple_of(slot * per, per), per)],
                                     info_sem.at[slot])

    def gather(info_slot, buf_slot):
        base = info_slot * per
        for r in range(rows):
            pltpu.make_async_copy(
                xn_hbm.at[info_smem[base + r]],
                xbuf.at[buf_slot, r // SUBLANES, pl.ds(r % SUBLANES, 1), :],
                x_sem.at[buf_slot]).start(priority=r % 2)

    def scatter(info_slot, buf_slot):
        base = info_slot * per + rows
        for r in range(rows):
            pltpu.make_async_copy(
                ybuf.at[buf_slot, r // SUBLANES, pl.ds(r % SUBLANES, 1), :],
                y_hbm.at[info_smem[base + r]], y_sem.at[buf_slot]).start(priority=r % 2)

    def wait_rows(buf, slot, sem):
        pltpu.make_async_copy(buf.at[slot], buf.at[slot], sem.at[slot]).wait()

    def w_copies(e, slot):
        return [pltpu.make_async_copy(src.at[e], dst.at[slot], w_sem.at[slot])
                for src, dst in ((wg_hbm, wg32), (wu_hbm, wu32), (wd_hbm, wd32))]

    @pl.when(i == 0)
    def _():
        for cp in w_copies(be_ref[0], wslot_ref[0]):
            cp.start()
        info_copy(0, 0).start()
        for s in range(SPARE_BLOCKS):
            info_copy(n_blocks + s, ring + s).start()
        zeros_ref[...] = jnp.zeros_like(zeros_ref)
        ybuf[...] = jnp.zeros_like(ybuf)
        fills = [pltpu.make_async_copy(
            zeros_ref, y_hbm.at[pl.ds(n_real + blk * rows, rows), 0, :], fill_sem)
            for blk in range(n_exp + SPARE_BLOCKS)]
        for cp in fills:
            cp.start()
        for cp in fills:
            cp.wait()
        info_copy(0, 0).wait()
        for s in range(SPARE_BLOCKS):
            info_copy(n_blocks + s, ring + s).wait()

        @pl.when(n > 1)
        def _():
            info_copy(1, 1).start()
        gather(0, 0)
        scatter(ring, 0)

    @pl.when(i < n)
    def _():
        nxt = jnp.minimum(i + 1, n - 1)

        @pl.when(i + 1 < n)
        def _():
            info_copy(i + 1, (i + 1) % ring).wait()

        @pl.when(i + 2 < n)
        def _():
            info_copy(i + 2, (i + 2) % ring).start()

        changed = jnp.logical_or(i == 0, be_ref[i] != be_ref[jnp.maximum(i - 1, 0)])

        @pl.when(changed)
        def _():
            slot = wslot_ref[i]
            for cp in w_copies(be_ref[i], slot):
                cp.wait()

            @pl.when(wnext_ref[i] < n_exp)
            def _():
                for cp in w_copies(wnext_ref[i], 1 - slot):
                    cp.start()
            wg_bf[...] = wg32[slot].astype(BF16)
            wu_bf[...] = wu32[slot].astype(BF16)
            wd_bf[...] = wd32[slot].astype(BF16)

        cur = i % 2
        other = 1 - cur
        gather(nxt % ring, other)
        scatter(jnp.where(i == 0, ring + 1, (i + ring - 1) % ring), other)
        wait_rows(xbuf, cur, x_sem)
        x = xbuf[cur].reshape(rows, d).astype(BF16)
        gate = jnp.dot(x, wg_bf[...], preferred_element_type=F32) + bg_ref[...]
        up = jnp.dot(x, wu_bf[...], preferred_element_type=F32) + bu_ref[...]
        gate = jnp.minimum(gate, SWIGLU_LIMIT)
        up = jnp.clip(up, -SWIGLU_LIMIT, SWIGLU_LIMIT)
        glu = gate * (1.0 / (1.0 + jnp.exp(-SWIGLU_ALPHA * gate)))
        hidden = ((up + 1.0) * glu).astype(BF16)
        y = jnp.dot(hidden, wd_bf[...], preferred_element_type=F32) + bd_ref[...]
        wait_rows(ybuf, cur, y_sem)
        ybuf[cur] = y.reshape(rows // SUBLANES, SUBLANES, d)

        @pl.when(i == n - 1)
        def _():
            scatter(i % ring, cur)
            wait_rows(ybuf, cur, y_sem)
            wait_rows(ybuf, other, y_sem)
            wait_rows(xbuf, other, x_sem)


def _experts(xn_rows, info, block_plan, n_blocks, w_gate, b_gate, w_up, b_up, w_down, b_down):
    t, _, d = xn_rows.shape
    n_exp, _, f = w_gate.shape
    rows = EXPERT_ROWS
    n_real = t * TOP_K
    bmap = lambda i, be, nu, ws, wn: (be[i], 0, 0)
    hbm = pl.BlockSpec(memory_space=pl.ANY)
    grid_spec = pltpu.PrefetchScalarGridSpec(
        num_scalar_prefetch=4,
        grid=(n_blocks,),
        in_specs=[
            hbm, hbm,
            hbm, pl.BlockSpec((None, 1, f), bmap),
            hbm, pl.BlockSpec((None, 1, f), bmap),
            hbm, pl.BlockSpec((None, 1, d), bmap),
        ],
        out_specs=hbm,
        scratch_shapes=[
            pltpu.SMEM(((4 + SPARE_BLOCKS) * 2 * rows,), I32),
            pltpu.VMEM((2, rows // SUBLANES, SUBLANES, d), F32),
            pltpu.VMEM((2, rows // SUBLANES, SUBLANES, d), F32),
            pltpu.VMEM((rows, d), F32),
            pltpu.SemaphoreType.DMA((4 + SPARE_BLOCKS,)),
            pltpu.SemaphoreType.DMA((2,)),
            pltpu.SemaphoreType.DMA((2,)),
            pltpu.SemaphoreType.DMA,
            pltpu.VMEM((2, d, f), F32),
            pltpu.VMEM((2, d, f), F32),
            pltpu.VMEM((2, f, d), F32),
            pltpu.SemaphoreType.DMA((2,)),
            pltpu.VMEM((d, f), BF16),
            pltpu.VMEM((d, f), BF16),
            pltpu.VMEM((f, d), BF16),
        ],
    )
    return pl.pallas_call(
        functools.partial(_expert_kernel, rows=rows, n_exp=n_exp, n_blocks=n_blocks, n_real=n_real),
        grid_spec=grid_spec,
        out_shape=jax.ShapeDtypeStruct((n_real + (n_exp + SPARE_BLOCKS) * rows, 1, d), F32),
        compiler_params=pltpu.CompilerParams(
            dimension_semantics=("arbitrary",), vmem_limit_bytes=56 * MIB),
        name="moe_experts",
    )(*block_plan, info, xn_rows,
      w_gate, b_gate.reshape(n_exp, 1, f), w_up, b_up.reshape(n_exp, 1, f),
      w_down, b_down.reshape(n_exp, 1, d))


def _combine_kernel(y_hbm, x1_ref, gate_ref, lnf_ref, o_ref, ybuf, y_sem, *, tile, n_tok):
    i = pl.program_id(0)
    n = pl.num_programs(0)

    def copies(b):
        return [pltpu.make_async_copy(
            y_hbm.at[pl.ds(pl.multiple_of(k * n_tok + b * tile, tile), tile), 0, :],
            ybuf.at[b % 2, k], y_sem.at[b % 2]) for k in range(TOP_K)]

    @pl.when(i == 0)
    def _():
        for cp in copies(0):
            cp.start()

    @pl.when(i + 1 < n)
    def _():
        for cp in copies(i + 1):
            cp.start()

    for cp in copies(i):
        cp.wait()
    gates = gate_ref[...]
    x = x1_ref[...]
    for k in range(TOP_K):
        x = x + gates[:, k:k + 1] * ybuf[i % 2, k]
    o_ref[...] = _rms(x, lnf_ref[...])


def _combine(y_rows, x1, gates, ln_f_g):
    t, d = x1.shape
    tile = min(COMBINE_TILE, t)
    return pl.pallas_call(
        functools.partial(_combine_kernel, tile=tile, n_tok=t),
        grid=(t // tile,),
        in_specs=[
            pl.BlockSpec(memory_space=pl.ANY),
            pl.BlockSpec((tile, d), lambda i: (i, 0)),
            pl.BlockSpec((tile, TOP_K), lambda i: (i, 0)),
            pl.BlockSpec((1, d), lambda i: (0, 0)),
        ],
        out_specs=pl.BlockSpec((tile, d), lambda i: (i, 0)),
        out_shape=jax.ShapeDtypeStruct((t, d), F32),
        scratch_shapes=[
            pltpu.VMEM((2, TOP_K, tile, d), F32),
            pltpu.SemaphoreType.DMA((2,)),
        ],
        compiler_params=pltpu.CompilerParams(
            dimension_semantics=("arbitrary",), vmem_limit_bytes=40 * MIB),
        name="moe_combine",
    )(y_rows, x1, gates, ln_f_g.reshape(1, d))


def kernel(x, ln1_g, w_in, lam_re, lam_im, log_dt, ssm_b_re, ssm_b_im, ssm_c_re, ssm_c_im,
           ssm_d, w_glu, g_sb, g_ssm, w_out, ln2_g, w_router, b_router, w_gate, b_gate,
           w_up, b_up, w_down, b_down, ln_f_g):
    bsz, seq, d = x.shape
    assert ln1_g.shape[0] == 1, "depth-1 block only"
    ssm_w = ssm_d.shape[1]
    sb = g_sb.shape[1]
    x2 = x.reshape(bsz * seq, d)
    q, k, v, u = _in_proj(x2, ln1_g[0], w_in[0], sb, ssm_w)
    y_sb = _attention(q, k, v, bsz, seq)
    y_ss = _ssm(u, bsz, seq, lam_re[0], lam_im[0], log_dt[0], ssm_b_re[0], ssm_b_im[0],
                ssm_c_re[0], ssm_c_im[0])
    x1, xn_rows, top_idx, gates, counts = _post(
        x2, y_sb, y_ss, u, ssm_d[0], w_glu[0], g_sb[0], g_ssm[0], w_out[0], ln2_g[0],
        w_router[0], b_router[0])
    info, block_plan, n_blocks = _plan(top_idx, counts, EXPERT_ROWS)
    y_rows = _experts(xn_rows, info, block_plan, n_blocks, w_gate[0], b_gate[0],
                      w_up[0], b_up[0], w_down[0], b_down[0])
    out = _combine(y_rows, x1, gates, ln_f_g)
    return out.reshape(bsz, seq, d)
```

```python
import functools
import math

import jax
import jax.numpy as jnp
from jax import lax
from jax.experimental import pallas as pl
from jax.experimental.pallas import tpu as pltpu

F32 = jnp.float32
BF16 = jnp.bfloat16
I32 = jnp.int32

EPS = 1e-5
SB_HEAD_DIM = 64
SSM_GROUP = 16
SSM_STATE = 64
TOP_K = 4
SWIGLU_LIMIT = 7.0
SWIGLU_ALPHA = 1.702

LANES = 128
SUBLANES = 8
HEADS_PER_BLOCK = LANES // SB_HEAD_DIM
ATTN_BLOCK = 256
ATTN_EXP_FLOOR = 160.0
SSM_CHUNK = 16
SSM_FOLD = 8
TOKEN_TILE = 512
EXPERT_ROWS = 256
COMBINE_TILE = 256
DISPATCH_TILE = 1024
MIB = 1024 * 1024
LOG2E = 1.4426950408889634


def _rms(x, g):
    return x * lax.rsqrt(jnp.mean(x * x, axis=-1, keepdims=True) + EPS) * g


def _in_proj_kernel(x_ref, g_ref, w_ref, q_ref, k_ref, v_ref, u_ref, w_bf, *, sb, scale):
    @pl.when(pl.program_id(0) == 0)
    def _():
        w_bf[...] = w_ref[...].astype(BF16)

    h = _rms(x_ref[...], g_ref[...])
    proj = jnp.dot(h.astype(BF16), w_bf[...], preferred_element_type=F32)
    k_ref[...] = proj[:, sb:2 * sb].astype(BF16)
    u_ref[...] = proj[:, 3 * sb:]
    lane = lax.broadcasted_iota(I32, (1, LANES), 1)
    q = (proj[:, :sb] * scale).astype(BF16)
    v = proj[:, 2 * sb:3 * sb].astype(BF16)
    zero = jnp.zeros((), BF16)
    for pair in range(sb // LANES):
        cols = slice(pair * LANES, (pair + 1) * LANES)
        for head in range(HEADS_PER_BLOCK):
            own = (lane // SB_HEAD_DIM) == head
            out = slice((pair * HEADS_PER_BLOCK + head) * LANES,
                        (pair * HEADS_PER_BLOCK + head + 1) * LANES)
            q_ref[:, out] = jnp.where(own, q[:, cols], zero)
            v_ref[:, out] = jnp.where(own, v[:, cols], zero)


def _in_proj(x2, ln1_g, w_in, sb, ssm_w):
    t, d = x2.shape
    n_in = w_in.shape[1]
    tm = min(TOKEN_TILE, t)
    scale = LOG2E / math.sqrt(SB_HEAD_DIM)
    return pl.pallas_call(
        functools.partial(_in_proj_kernel, sb=sb, scale=scale),
        grid=(t // tm,),
        in_specs=[
            pl.BlockSpec((tm, d), lambda i: (i, 0)),
            pl.BlockSpec((1, d), lambda i: (0, 0)),
            pl.BlockSpec((d, n_in), lambda i: (0, 0)),
        ],
        out_specs=[
            pl.BlockSpec((tm, HEADS_PER_BLOCK * sb), lambda i: (i, 0)),
            pl.BlockSpec((tm, sb), lambda i: (i, 0)),
            pl.BlockSpec((tm, HEADS_PER_BLOCK * sb), lambda i: (i, 0)),
            pl.BlockSpec((tm, ssm_w), lambda i: (i, 0)),
        ],
        out_shape=[
            jax.ShapeDtypeStruct((t, HEADS_PER_BLOCK * sb), BF16),
            jax.ShapeDtypeStruct((t, sb), BF16),
            jax.ShapeDtypeStruct((t, HEADS_PER_BLOCK * sb), BF16),
            jax.ShapeDtypeStruct((t, ssm_w), F32),
        ],
        scratch_shapes=[pltpu.VMEM((d, n_in), BF16)],
        compiler_params=pltpu.CompilerParams(
            dimension_semantics=("arbitrary",), vmem_limit_bytes=48 * MIB),
        name="in_proj",
    )(x2, ln1_g.reshape(1, d), w_in)


def _attn_kernel(ti_ref, tj_ref, q_ref, k_ref, v_ref, tri_ref, mask_ref, o_ref,
                 dbuf, spbuf, lbuf, rsbuf, acc_ref, r_ref, *, blk, n_blk, n_items):
    sign_bit = jnp.uint32(0x80000000)
    last = n_items - 1
    ALL, CAUSAL, NONE = 0, 1, 2

    for ref in (dbuf, spbuf, lbuf, rsbuf, acc_ref, r_ref):
        ref[...] = jnp.zeros_like(ref)

    def both_heads(ref, start):
        rows = pl.ds(pl.multiple_of(start, blk), blk)
        return jnp.concatenate([ref[rows, h * LANES:(h + 1) * LANES]
                                for h in range(HEADS_PER_BLOCK)], axis=0)

    def item(idx):
        c = jnp.clip(idx, 0, last)
        return ti_ref[c], tj_ref[c]

    half = n_blk // 2
    stream_end = (half * (half + 1) // 2, n_items)

    def step(state, x):
        cur, p1, p2, out_blk = list(state[:2]), state[2], state[3], list(state[4:])
        slot, other = x, 1 - x
        c1 = jnp.where(cur[x] < stream_end[x], cur[x], n_items)
        c3 = p2
        i1, j1 = item(c1)
        ks = k_ref[pl.ds(pl.multiple_of(j1 * blk, blk), blk), :]
        w = lax.dot_general(both_heads(q_ref, i1 * blk), ks, (((1,), (1,)), ((), ())),
                            preferred_element_type=F32)
        i3, j3 = item(c3)
        valid = c3 <= last
        first = jnp.logical_and(i3 == j3, valid)
        mask3 = mask_ref[jnp.where(valid, (i3 == j3).astype(I32), NONE)]
        i3 = jnp.where(valid, i3, out_blk[x])
        r_prev = jnp.where(first, 0.0, r_ref[x])
        r_new = r_prev + rsbuf[other]
        r_ref[x] = r_new
        done = jnp.logical_and(valid, jnp.min(r_new) >= ATTN_EXP_FLOOR)
        next_block = lax.shift_right_logical((i3 + 1) * (i3 + 2), 1)
        advanced = jnp.where(done, jnp.maximum(cur[x] + 1, next_block), cur[x] + 1)
        cur[x] = jnp.minimum(advanced, stream_end[x])
        out_blk[x] = i3
        r_wide = jnp.concatenate([r_prev] * (blk // LANES), axis=1)
        a = jnp.exp2(dbuf[slot] - (lbuf[other] + r_wide))
        ab = a.astype(BF16) * jnp.concatenate([mask3, mask3], axis=0)
        a_cat = jnp.concatenate([ab[:blk], ab[blk:]], axis=1)
        acc = jnp.where(first, 0.0, acc_ref[x]) + jnp.dot(
            a_cat, both_heads(v_ref, j3 * blk), preferred_element_type=F32)
        acc_ref[x] = acc
        o_ref[pl.ds(pl.multiple_of(i3 * blk, blk), blk), :] = acc.astype(o_ref.dtype)
        sums = jnp.dot(spbuf[other], tri_ref[...], preferred_element_type=F32)
        lbuf[slot] = sums[:, :blk]
        rsbuf[slot] = sums[:, blk:]
        mask1 = mask_ref[(i1 == j1).astype(I32)]
        neg_abs = lax.bitcast_convert_type(
            lax.bitcast_convert_type(w, jnp.uint32) | sign_bit, F32)
        sp2 = jnp.maximum(w, 0.0) + jnp.log(1.0 + jnp.exp2(neg_abs)) * LOG2E
        dbuf[slot] = w - sp2
        spbuf[slot] = sp2.astype(BF16) * jnp.concatenate([mask1, mask1], axis=0)
        return cur[0], cur[1], c1, p1, out_blk[0], out_blk[1]

    def in_flight(state):
        cur0, cur1, p1, p2 = state[:4]
        return functools.reduce(jnp.logical_or, [cur0 < stream_end[0], cur1 < stream_end[1],
                                                 p1 <= last, p2 <= last])

    def four_steps(state):
        for x in (0, 1, 0, 1):
            state = step(state, x)
        return state

    none = jnp.int32(n_items)
    lax.while_loop(in_flight, four_steps,
                   (jnp.int32(0), jnp.int32(stream_end[0]), none, none,
                    jnp.int32(0), jnp.int32(half)))


def _attention(q, k, v, bsz, seq):
    sb = k.shape[-1]
    blk = min(ATTN_BLOCK, seq)
    n_blk = seq // blk
    n_pairs = sb // LANES
    q3, k3, v3 = (a.reshape(bsz, seq, a.shape[-1]) for a in (q, k, v))
    per_head = HEADS_PER_BLOCK * LANES
    jj = lax.broadcasted_iota(I32, (blk, blk), 0)
    ss = lax.broadcasted_iota(I32, (blk, blk), 1)
    tri = jnp.concatenate([(jj > ss).astype(BF16), jnp.ones((blk, LANES), BF16)], axis=1)
    masks = jnp.stack([jnp.ones((blk, blk), BF16), (ss < jj).astype(BF16),
                       jnp.zeros((blk, blk), BF16)])
    items = [(i, j) for i in range(n_blk) for j in range(i, -1, -1)]
    item_i = jnp.asarray([i for i, _ in items], I32)
    item_j = jnp.asarray([j for _, j in items], I32)
    whole = lambda b, p, ti, tj: (b, 0, p)
    grid_spec = pltpu.PrefetchScalarGridSpec(
        num_scalar_prefetch=2,
        grid=(bsz, n_pairs),
        in_specs=[
            pl.BlockSpec((None, seq, per_head), whole),
            pl.BlockSpec((None, seq, LANES), whole),
            pl.BlockSpec((None, seq, per_head), whole),
            pl.BlockSpec((blk, blk + LANES), lambda b, p, ti, tj: (0, 0)),
            pl.BlockSpec((3, blk, blk), lambda b, p, ti, tj: (0, 0, 0)),
        ],
        out_specs=pl.BlockSpec((None, seq, LANES), whole),
        scratch_shapes=[
            pltpu.VMEM((2, HEADS_PER_BLOCK * blk, blk), F32),
            pltpu.VMEM((2, HEADS_PER_BLOCK * blk, blk), BF16),
            pltpu.VMEM((2, HEADS_PER_BLOCK * blk, blk), F32),
            pltpu.VMEM((2, HEADS_PER_BLOCK * blk, LANES), F32),
            pltpu.VMEM((2, blk, LANES), F32),
            pltpu.VMEM((2, HEADS_PER_BLOCK * blk, LANES), F32),
        ],
    )
    out = pl.pallas_call(
        functools.partial(_attn_kernel, blk=blk, n_blk=n_blk, n_items=len(items)),
        grid_spec=grid_spec,
        out_shape=jax.ShapeDtypeStruct((bsz, seq, sb), BF16),
        compiler_params=pltpu.CompilerParams(
            dimension_semantics=("arbitrary", "arbitrary"), vmem_limit_bytes=40 * MIB),
        name="sb_attention",
    )(item_i, item_j, q3, k3, v3, tri, masks)
    return out.reshape(bsz * seq, sb)


def _ssm_params(lam_re, lam_im, log_dt, b_re, b_im, c_re, c_im, chunk, n_chunks):
    lam = lax.complex(lam_re.astype(F32), lam_im.astype(F32))
    dt = jnp.exp(log_dt.astype(F32))[:, None]
    lam_dt = lam * dt
    lam_bar = jnp.exp(lam_dt)
    b_bar = ((lam_bar - 1.0) / lam)[:, :, None] * lax.complex(b_re.astype(F32), b_im.astype(F32))
    b_t = jnp.swapaxes(b_bar, 1, 2)
    c_mat = lax.complex(c_re.astype(F32), c_im.astype(F32))
    steps = jnp.arange(chunk + 1, dtype=F32)
    pw = jnp.exp(lam_dt[:, None, :] * steps[None, :, None])
    npw = jnp.exp(-lam_dt[:, None, :] * steps[None, :chunk, None])

    def halves(lo, hi):
        return jnp.concatenate([lo, hi], axis=-1)

    def b_side(z):
        return [halves(jnp.real(z), jnp.real(z)), halves(-jnp.imag(z), jnp.imag(z))]

    def c_side(z):
        return [halves(jnp.real(z), jnp.imag(z)), halves(jnp.imag(z), jnp.real(z))]

    time_terms = jnp.stack(
        b_side(npw)
        + b_side(pw[:, chunk - 1::-1])
        + c_side(pw[:, :chunk])
        + c_side(pw[:, 1:]), axis=1)
    chan_terms = jnp.stack(
        [halves(jnp.real(b_t), jnp.imag(b_t)), halves(jnp.imag(b_t), jnp.real(b_t)),
         halves(jnp.real(c_mat), -jnp.real(c_mat)), halves(-jnp.imag(c_mat), -jnp.imag(c_mat))],
        axis=1)
    n_steps = max(1, (n_chunks - 1).bit_length())
    powers = []
    cur = pw[:, chunk, :]
    for _ in range(n_steps):
        powers.append(cur)
        cur = cur * cur
    lam_pow = jnp.stack(powers, axis=1)
    a1 = halves(jnp.real(lam_pow), jnp.real(lam_pow))
    a2 = halves(-jnp.imag(lam_pow), jnp.imag(lam_pow))
    return time_terms, chan_terms, a1, a2


def _ssm_kernel(u_ref, time_ref, chan_ref, a1_ref, a2_ref, perm_ref, perm_t_ref, y_ref,
                u8_ref, y8_ref, toep_ref, *, n_chunks, n_steps, group):
    seq = u_ref.shape[0]
    tiles = seq // SSM_FOLD
    lc = toep_ref.shape[0]
    per_chunk = (lc // group) // SSM_FOLD
    p2 = a1_ref.shape[2]
    group_shift = group.bit_length() - 1
    nt = (((1,), (1,)), ((), ()))

    folded = jnp.concatenate(
        [u_ref[pl.ds(s, tiles, stride=SSM_FOLD), :] for s in range(SSM_FOLD)], axis=1)
    regrouped = jnp.dot(folded.astype(BF16), perm_ref[...], preferred_element_type=F32)
    n_groups = LANES // group
    for g in range(n_groups):
        u8_ref[g] = regrouped[:, g * LANES:(g + 1) * LANES]

    for g in range(n_groups):
        def table(t1, t2, v1, v2):
            full = (time_ref[g, t1][:, None, :] * chan_ref[g, v1][None, :, :]
                    + time_ref[g, t2][:, None, :] * chan_ref[g, v2][None, :, :])
            return full.reshape(lc, p2).astype(BF16)

        src = table(0, 1, 0, 1)
        to_state = table(2, 3, 0, 1)
        dst = table(4, 5, 2, 3)
        from_state = table(6, 7, 2, 3)
        cb = min(256, lc)
        for j in range(lc // cb):
            blk = lax.dot_general(src, dst[j * cb:(j + 1) * cb], nt, preferred_element_type=F32)
            s_idx = lax.broadcasted_iota(I32, (lc, cb), 0) >> group_shift
            t_idx = (lax.broadcasted_iota(I32, (lc, cb), 1) + j * cb) >> group_shift
            toep_ref[:, j * cb:(j + 1) * cb] = jnp.where(s_idx <= t_idx, blk, 0.0).astype(BF16)
        u = jnp.concatenate(
            [u8_ref[g, pl.ds(tau, n_chunks, stride=per_chunk), :] for tau in range(per_chunk)],
            axis=1).astype(BF16)
        y = jnp.dot(u, toep_ref[...], preferred_element_type=F32)
        z = jnp.dot(u, to_state, preferred_element_type=F32)
        n = lax.broadcasted_iota(I32, z.shape, 0)
        x = jnp.where(n >= 1, pltpu.roll(z, 1, 0), 0.0)
        for k in range(n_steps):
            sh = 1 << k
            xs = jnp.where(n >= sh, pltpu.roll(x, sh, 0), 0.0)
            x = (x + a1_ref[g, k:k + 1, :] * xs
                 + a2_ref[g, k:k + 1, :] * pltpu.roll(xs, p2 // 2, 1))
        y = y + lax.dot_general(x.astype(BF16), from_state, nt, preferred_element_type=F32)
        for tau in range(per_chunk):
            y8_ref[g, pl.ds(tau, n_chunks, stride=per_chunk), :] = y[:, tau * LANES:(tau + 1) * LANES]

    y8 = jnp.concatenate([y8_ref[g] for g in range(n_groups)], axis=1)
    unfolded = jnp.dot(y8.astype(BF16), perm_t_ref[...], preferred_element_type=F32)
    for s in range(SSM_FOLD):
        y_ref[pl.ds(s, tiles, stride=SSM_FOLD), :] = unfolded[:, s * LANES:(s + 1) * LANES]


def _ssm(u, bsz, seq, lam_re, lam_im, log_dt, b_re, b_im, c_re, c_im):
    g, p = lam_re.shape
    c = b_re.shape[-1]
    chunk = min(SSM_CHUNK, seq)
    n_chunks = seq // chunk
    assert n_chunks & (n_chunks - 1) == 0 and c & (c - 1) == 0, (n_chunks, c)
    assert SSM_FOLD * c == LANES and chunk % SSM_FOLD == 0
    lc = chunk * c
    per_block = LANES // c
    time_terms, chan_terms, a1, a2 = _ssm_params(
        lam_re, lam_im, log_dt, b_re, b_im, c_re, c_im, chunk, n_chunks)
    n_steps = a1.shape[1]
    src_idx = jnp.arange(SSM_FOLD * LANES, dtype=I32)
    s_of, g_of, c_of = src_idx // LANES, (src_idx % LANES) // c, src_idx % c
    dst_idx = g_of * LANES + s_of * c + c_of
    perm = (dst_idx[:, None] == src_idx[None, :]).astype(BF16)
    block = lambda cb, b: (cb, 0, 0)
    fixed = lambda cb, b: (0, 0)
    y = pl.pallas_call(
        functools.partial(_ssm_kernel, n_chunks=n_chunks, n_steps=n_steps, group=c),
        grid=(g // per_block, bsz),
        in_specs=[
            pl.BlockSpec((None, seq, LANES), lambda cb, b: (b, 0, cb)),
            pl.BlockSpec((per_block,) + time_terms.shape[1:], lambda cb, b: (cb, 0, 0, 0)),
            pl.BlockSpec((per_block,) + chan_terms.shape[1:], lambda cb, b: (cb, 0, 0, 0)),
            pl.BlockSpec((per_block, n_steps, 2 * p), block),
            pl.BlockSpec((per_block, n_steps, 2 * p), block),
            pl.BlockSpec((SSM_FOLD * LANES, SSM_FOLD * LANES), fixed),
            pl.BlockSpec((SSM_FOLD * LANES, SSM_FOLD * LANES), fixed),
        ],
        out_specs=pl.BlockSpec((None, seq, LANES), lambda cb, b: (b, 0, cb)),
        out_shape=jax.ShapeDtypeStruct((bsz, seq, g * c), F32),
        scratch_shapes=[
            pltpu.VMEM((per_block, seq // SSM_FOLD, LANES), F32),
            pltpu.VMEM((per_block, seq // SSM_FOLD, LANES), F32),
            pltpu.VMEM((lc, lc), BF16),
        ],
        compiler_params=pltpu.CompilerParams(
            dimension_semantics=("arbitrary", "arbitrary"), vmem_limit_bytes=56 * MIB),
        name="s5_chunked_scan",
    )(u.reshape(bsz, seq, g * c), time_terms, chan_terms, a1, a2, perm, perm.T)
    return y.reshape(bsz * seq, g * c)


def _post_kernel(x_ref, ysb_ref, yss_ref, u_ref, d_ref, wglu_ref, gsb_ref, gssm_ref, wout_ref,
                 ln2_ref, wr_ref, br_ref, tri_ref, x1_ref, xn_ref, idx_ref, gate_ref, rank_ref,
                 count_ref, running_ref, wglu_bf, wout_bf, *, sb, n_exp):
    @pl.when(pl.program_id(0) == 0)
    def _():
        running_ref[...] = jnp.zeros_like(running_ref)
        wglu_bf[...] = wglu_ref[...].astype(BF16)
        wout_bf[...] = wout_ref[...].astype(BF16)

    u = u_ref[...].astype(F32)
    y = yss_ref[...].astype(F32) + d_ref[...] * u
    y = y * (0.5 * (1.0 + jnp.tanh(math.sqrt(2.0 / math.pi) * (y + 0.044715 * (y * y * y)))))
    ab = jnp.dot(y.astype(BF16), wglu_bf[...], preferred_element_type=F32)
    w = ab.shape[1] // 2
    y_ssm = ab[:, :w] * (1.0 / (1.0 + jnp.exp(-ab[:, w:])))
    m_sb = _rms(ysb_ref[...].astype(F32), gsb_ref[...])
    m_ssm = _rms(y_ssm, gssm_ref[...])
    x1 = (x_ref[...]
          + jnp.dot(m_sb.astype(BF16), wout_bf[:sb, :], preferred_element_type=F32)
          + jnp.dot(m_ssm.astype(BF16), wout_bf[sb:, :], preferred_element_type=F32))
    x1_ref[...] = x1
    xn = _rms(x1, ln2_ref[...])
    xn_ref[...] = xn
    def split(a):
        hi = a.astype(BF16)
        return hi, (a - hi.astype(F32)).astype(BF16)

    xn_hi, xn_lo = split(xn)
    wr_hi, wr_lo = split(wr_ref[...])
    logits = (jnp.dot(xn_hi, wr_hi, preferred_element_type=F32)
              + jnp.dot(xn_hi, wr_lo, preferred_element_type=F32)
              + jnp.dot(xn_lo, wr_hi, preferred_element_type=F32)) + br_ref[...]
    lane = lax.broadcasted_iota(I32, logits.shape, 1).astype(F32)
    out_lane = lax.broadcasted_iota(I32, (logits.shape[0], LANES), 1)
    idx_out = jnp.zeros((logits.shape[0], LANES), F32)
    val_out = jnp.zeros((logits.shape[0], LANES), F32)
    top = None
    denom = None
    work = logits
    chosen = []
    for k in range(TOP_K):
        m = jnp.max(work, axis=-1, keepdims=True)
        sel = jnp.min(jnp.where(work == m, lane, float(n_exp)), axis=-1, keepdims=True)
        hit = lane == sel
        chosen.append(hit)
        work = jnp.where(hit, -jnp.inf, work)
        if k == 0:
            top = m
        e = jnp.exp(m - top)
        denom = e if denom is None else denom + e
        idx_out = jnp.where(out_lane == k, sel, idx_out)
        val_out = jnp.where(out_lane == k, e, val_out)
    idx_ref[...] = jnp.transpose(idx_out)[:SUBLANES].astype(I32)
    gate_ref[...] = (val_out / denom)[:, :TOP_K]
    member = functools.reduce(jnp.logical_or, chosen).astype(BF16)
    before = running_ref[...] + jnp.dot(tri_ref[...], member, preferred_element_type=F32)
    rank_out = jnp.zeros((logits.shape[0], LANES), F32)
    for k in range(TOP_K):
        rk = jnp.sum(jnp.where(chosen[k], before, 0.0), axis=-1, keepdims=True)
        rank_out = jnp.where(out_lane == k, rk, rank_out)
    rank_ref[...] = jnp.transpose(rank_out)[:SUBLANES].astype(I32)
    running = running_ref[...] + jnp.sum(member.astype(F32), axis=0, keepdims=True)
    running_ref[...] = running
    count_ref[...] = running


def _post(x2, y_sb, y_ss, u, ssm_d, w_glu, g_sb, g_ssm, w_out, ln2_g, w_router, b_router):
    t, d = x2.shape
    sb = y_sb.shape[1]
    w = y_ss.shape[1]
    n_exp = w_router.shape[1]
    tm = min(TOKEN_TILE, t)
    row = lambda i: (i, 0)
    fixed = lambda i: (0, 0)
    earlier = (lax.broadcasted_iota(I32, (tm, tm), 1)
               < lax.broadcasted_iota(I32, (tm, tm), 0)).astype(BF16)
    return pl.pallas_call(
        functools.partial(_post_kernel, sb=sb, n_exp=n_exp),
        grid=(t // tm,),
        in_specs=[
            pl.BlockSpec((tm, d), row),
            pl.BlockSpec((tm, sb), row),
            pl.BlockSpec((tm, w), row),
            pl.BlockSpec((tm, w), row),
            pl.BlockSpec((1, w), fixed),
            pl.BlockSpec((w, 2 * w), fixed),
            pl.BlockSpec((1, sb), fixed),
            pl.BlockSpec((1, w), fixed),
            pl.BlockSpec((sb + w, d), fixed),
            pl.BlockSpec((1, d), fixed),
            pl.BlockSpec((d, n_exp), fixed),
            pl.BlockSpec((1, n_exp), fixed),
            pl.BlockSpec((tm, tm), fixed),
        ],
        out_specs=[
            pl.BlockSpec((tm, d), row),
            pl.BlockSpec((tm, d), row),
            pl.BlockSpec((SUBLANES, tm), lambda i: (0, i)),
            pl.BlockSpec((tm, TOP_K), row),
            pl.BlockSpec((SUBLANES, tm), lambda i: (0, i)),
            pl.BlockSpec((1, n_exp), fixed),
        ],
        out_shape=[
            jax.ShapeDtypeStruct((t, d), F32),
            jax.ShapeDtypeStruct((t, d), F32),
            jax.ShapeDtypeStruct((SUBLANES, t), I32),
            jax.ShapeDtypeStruct((t, TOP_K), F32),
            jax.ShapeDtypeStruct((SUBLANES, t), I32),
            jax.ShapeDtypeStruct((1, n_exp), F32),
        ],
        scratch_shapes=[pltpu.VMEM((1, n_exp), F32), pltpu.VMEM((w, 2 * w), BF16),
                        pltpu.VMEM((sb + w, d), BF16)],
        compiler_params=pltpu.CompilerParams(
            dimension_semantics=("arbitrary",), vmem_limit_bytes=48 * MIB),
        name="post_mixer_router",
    )(x2, y_sb, y_ss, u, ssm_d.reshape(1, w), w_glu, g_sb.reshape(1, sb),
      g_ssm.reshape(1, w), w_out, ln2_g.reshape(1, d), w_router,
      b_router.reshape(1, n_exp), earlier)


def _plan(top_idx, rank, counts, rows_per_block):
    k, t = top_idx.shape
    n_exp = counts.shape[-1]
    counts = counts.reshape(n_exp).astype(I32)
    padded = ((counts + rows_per_block - 1) // rows_per_block) * rows_per_block
    pad_ends = jnp.cumsum(padded)
    pad_starts = pad_ends - padded
    start_of = jnp.sum(jnp.where(top_idx[None] == jnp.arange(n_exp, dtype=I32)[:, None, None],
                                 pad_starts[:, None, None], 0), axis=0)
    dest = (start_of + rank).astype(I32)
    n_rows = t * k + n_exp * rows_per_block
    n_blocks = n_rows // rows_per_block
    block_start = jnp.arange(n_blocks, dtype=I32) * rows_per_block
    block_expert = jnp.minimum(
        jnp.sum((pad_ends[None, :] <= block_start[:, None]).astype(I32), axis=1), n_exp - 1)
    n_used = (pad_ends[-1] // rows_per_block).astype(I32).reshape(1)
    fill_start = jnp.concatenate([jnp.maximum(pad_ends - rows_per_block, 0).astype(I32), n_used])
    experts = jnp.arange(n_exp, dtype=I32)
    nonempty = counts > 0
    order = jnp.cumsum(nonempty.astype(I32)) - nonempty.astype(I32)
    later_ne = jnp.where((experts[None, :] > experts[:, None]) & nonempty[None, :],
                         experts[None, :], n_exp)
    next_tbl = jnp.min(later_ne, axis=1)
    onehot = block_expert[:, None] == experts[None, :]
    w_slot = jnp.sum(jnp.where(onehot, order & 1, 0), axis=1).astype(I32)
    w_next = jnp.sum(jnp.where(onehot, next_tbl, 0), axis=1).astype(I32)
    return dest, (block_expert, n_used, w_slot, w_next), fill_start, n_blocks


def _by_tile(dest, tile):
    k, t = dest.shape
    return dest.reshape(k, t // tile, tile).transpose(1, 0, 2).reshape(k * t)


def _dispatch_kernel(fill_ref, dest_hbm, xn_ref, xs_hbm, idx_smem, zeros_ref, idx_sem, row_sem,
                     fill_sem, *, tile, n_exp, fill_rows):
    i = pl.program_id(0)
    n = pl.num_programs(0)
    per_tile = TOP_K * tile

    def idx_copy(b):
        src = dest_hbm.at[pl.ds(pl.multiple_of(b * per_tile, per_tile), per_tile)]
        dst = idx_smem.at[pl.ds(pl.multiple_of((b % 2) * per_tile, per_tile), per_tile)]
        return pltpu.make_async_copy(src, dst, idx_sem.at[b % 2])

    @pl.when(i == 0)
    def _():
        idx_copy(0).start()
        zeros_ref[...] = jnp.zeros_like(zeros_ref)
        for e in range(n_exp):
            start = pl.multiple_of(fill_ref[e], 8)
            pltpu.make_async_copy(zeros_ref, xs_hbm.at[pl.ds(start, fill_rows), 0, :],
                                  fill_sem).start()
        for e in range(n_exp):
            pltpu.make_async_copy(zeros_ref, xs_hbm.at[pl.ds(0, fill_rows), 0, :],
                                  fill_sem).wait()
        n_blocks = xs_hbm.shape[0] // fill_rows

        def fill_unused(b, carry):
            cp = pltpu.make_async_copy(
                zeros_ref, xs_hbm.at[pl.ds(pl.multiple_of(b * fill_rows, fill_rows), fill_rows), 0, :],
                fill_sem)
            cp.start()
            cp.wait()
            return carry

        lax.fori_loop(fill_ref[n_exp], n_blocks, fill_unused, 0)

    idx_copy(i).wait()

    @pl.when(i + 1 < n)
    def _():
        idx_copy(i + 1).start()

    base = pl.multiple_of((i % 2) * per_tile, per_tile)
    quarter = tile // TOP_K
    for part in range(TOP_K):
        def issue(g, carry, part=part):
            group = part * (quarter // SUBLANES) + g
            for s in range(SUBLANES):
                for k in range(TOP_K):
                    dst_row = idx_smem[base + k * tile + group * SUBLANES + s]
                    pltpu.make_async_copy(xn_ref.at[group, pl.ds(s, 1), :], xs_hbm.at[dst_row],
                                          row_sem.at[part]).start(priority=k % 2)
            return carry

        lax.fori_loop(0, quarter // SUBLANES, issue, 0)
    for part in range(TOP_K):
        pltpu.make_async_copy(xn_ref, xn_ref, row_sem.at[part]).wait()


def _dispatch(xn, dest, fill_start, n_rows):
    t, d = xn.shape
    tile = min(DISPATCH_TILE, t)
    n_exp = fill_start.shape[0] - 1
    grid_spec = pltpu.PrefetchScalarGridSpec(
        num_scalar_prefetch=1,
        grid=(t // tile,),
        in_specs=[
            pl.BlockSpec(memory_space=pl.ANY),
            pl.BlockSpec((tile // SUBLANES, SUBLANES, d), lambda i, fs: (i, 0, 0)),
        ],
        out_specs=pl.BlockSpec(memory_space=pl.ANY),
        scratch_shapes=[
            pltpu.SMEM((2 * TOP_K * tile,), I32),
            pltpu.VMEM((EXPERT_ROWS, d), F32),
            pltpu.SemaphoreType.DMA((2,)),
            pltpu.SemaphoreType.DMA((TOP_K,)),
            pltpu.SemaphoreType.DMA,
        ],
    )
    return pl.pallas_call(
        functools.partial(_dispatch_kernel, tile=tile, n_exp=n_exp, fill_rows=EXPERT_ROWS),
        grid_spec=grid_spec,
        out_shape=jax.ShapeDtypeStruct((n_rows, 1, d), F32),
        compiler_params=pltpu.CompilerParams(
            dimension_semantics=("arbitrary",), vmem_limit_bytes=40 * MIB),
        name="moe_dispatch",
    )(fill_start, _by_tile(dest, tile), xn.reshape(t // SUBLANES, SUBLANES, d))


def _expert_kernel(be_ref, nused_ref, wslot_ref, wnext_ref, xs_hbm, wg_hbm, bg_ref, wu_hbm, bu_ref,
                   wd_hbm, bd_ref, y_hbm, xbuf, ybuf, x_sem, y_sem, wg32, wu32, wd32, w_sem,
                   wg_bf, wu_bf, wd_bf, *, rows, n_exp):
    i = pl.program_id(0)
    n = nused_ref[0]

    def w_copies(e, slot):
        return [pltpu.make_async_copy(src.at[e], dst.at[slot], w_sem.at[slot])
                for src, dst in ((wg_hbm, wg32), (wu_hbm, wu32), (wd_hbm, wd32))]

    def x_copy(b):
        return pltpu.make_async_copy(
            xs_hbm.at[pl.ds(pl.multiple_of(b * rows, rows), rows), 0, :], xbuf.at[b % 2],
            x_sem.at[b % 2])

    def y_copy(b):
        return pltpu.make_async_copy(
            ybuf.at[b % 2], y_hbm.at[pl.ds(pl.multiple_of(b * rows, rows), rows), 0, :],
            y_sem.at[b % 2])

    @pl.when(i < n)
    def _():
        @pl.when(i == 0)
        def _():
            x_copy(0).start()
            for cp in w_copies(be_ref[0], wslot_ref[0]):
                cp.start()

        @pl.when(i + 1 < n)
        def _():
            x_copy(i + 1).start()

        changed = jnp.logical_or(i == 0, be_ref[i] != be_ref[jnp.maximum(i - 1, 0)])

        @pl.when(changed)
        def _():
            slot = wslot_ref[i]
            for cp in w_copies(be_ref[i], slot):
                cp.wait()

            @pl.when(wnext_ref[i] < n_exp)
            def _():
                for cp in w_copies(wnext_ref[i], 1 - slot):
                    cp.start()
            wg_bf[...] = wg32[slot].astype(BF16)
            wu_bf[...] = wu32[slot].astype(BF16)
            wd_bf[...] = wd32[slot].astype(BF16)

        @pl.when(i >= 2)
        def _():
            y_copy(i - 2).wait()

        x_copy(i).wait()
        x = xbuf[i % 2].astype(BF16)
        gate = jnp.dot(x, wg_bf[...], preferred_element_type=F32) + bg_ref[...]
        up = jnp.dot(x, wu_bf[...], preferred_element_type=F32) + bu_ref[...]
        gate = jnp.minimum(gate, SWIGLU_LIMIT)
        up = jnp.clip(up, -SWIGLU_LIMIT, SWIGLU_LIMIT)
        glu = gate * (1.0 / (1.0 + jnp.exp(-SWIGLU_ALPHA * gate)))
        hidden = ((up + 1.0) * glu).astype(BF16)
        ybuf[i % 2] = jnp.dot(hidden, wd_bf[...], preferred_element_type=F32) + bd_ref[...]
        y_copy(i).start()

        @pl.when(i == n - 1)
        def _():
            y_copy(i).wait()

            @pl.when(i >= 1)
            def _():
                y_copy(i - 1).wait()

    @pl.when(i >= n)
    def _():
        ybuf[i % 2] = jnp.zeros((rows, ybuf.shape[2]), F32)
        y_copy(i).start()
        y_copy(i).wait()


def _experts(x_sorted, block_plan, n_blocks, w_gate, b_gate, w_up, b_up, w_down, b_down):
    n_rows, _, d = x_sorted.shape
    n_exp, _, f = w_gate.shape
    rows = EXPERT_ROWS
    bmap = lambda i, be, nu, ws, wn: (be[i], 0, 0)
    hbm = pl.BlockSpec(memory_space=pl.ANY)
    grid_spec = pltpu.PrefetchScalarGridSpec(
        num_scalar_prefetch=4,
        grid=(n_blocks,),
        in_specs=[
            hbm,
            hbm, pl.BlockSpec((None, 1, f), bmap),
            hbm, pl.BlockSpec((None, 1, f), bmap),
            hbm, pl.BlockSpec((None, 1, d), bmap),
        ],
        out_specs=hbm,
        scratch_shapes=[
            pltpu.VMEM((2, rows, d), F32),
            pltpu.VMEM((2, rows, d), F32),
            pltpu.SemaphoreType.DMA((2,)),
            pltpu.SemaphoreType.DMA((2,)),
            pltpu.VMEM((2, d, f), F32),
            pltpu.VMEM((2, d, f), F32),
            pltpu.VMEM((2, f, d), F32),
            pltpu.SemaphoreType.DMA((2,)),
            pltpu.VMEM((d, f), BF16),
            pltpu.VMEM((d, f), BF16),
            pltpu.VMEM((f, d), BF16),
        ],
    )
    return pl.pallas_call(
        functools.partial(_expert_kernel, rows=rows, n_exp=n_exp),
        grid_spec=grid_spec,
        out_shape=jax.ShapeDtypeStruct((n_rows, 1, d), F32),
        compiler_params=pltpu.CompilerParams(
            dimension_semantics=("arbitrary",), vmem_limit_bytes=52 * MIB),
        name="moe_experts",
    )(*block_plan, x_sorted,
      w_gate, b_gate.reshape(n_exp, 1, f), w_up, b_up.reshape(n_exp, 1, f),
      w_down, b_down.reshape(n_exp, 1, d))


def _combine_kernel(dest_hbm, y_hbm, x1_ref, gate_ref, lnf_ref, o_ref, idx_smem, ybuf,
                    idx_sem, row_sem, *, tile):
    i = pl.program_id(0)
    n = pl.num_programs(0)
    per_tile = TOP_K * tile

    def slot_base(b):
        return pl.multiple_of((b % 3) * per_tile, per_tile)

    def idx_copy(b):
        src = dest_hbm.at[pl.ds(pl.multiple_of(b * per_tile, per_tile), per_tile)]
        return pltpu.make_async_copy(src, idx_smem.at[pl.ds(slot_base(b), per_tile)],
                                     idx_sem.at[b % 3])

    def gather(b):
        base = slot_base(b)
        buf = ybuf.at[b % 2]
        sem = row_sem.at[b % 2]

        def issue(g, carry):
            for s in range(SUBLANES):
                pltpu.make_async_copy(y_hbm.at[idx_smem[base + g * SUBLANES + s]],
                                      buf.at[g, pl.ds(s, 1), :], sem).start(priority=s % 2)
            return carry

        lax.fori_loop(0, per_tile // SUBLANES, issue, 0)

    @pl.when(i == 0)
    def _():
        idx_copy(0).start()
        idx_copy(0).wait()
        gather(0)

        @pl.when(n > 1)
        def _():
            idx_copy(1).start()

    @pl.when(i + 1 < n)
    def _():
        idx_copy(i + 1).wait()

        @pl.when(i + 2 < n)
        def _():
            idx_copy(i + 2).start()
        gather(i + 1)

    cur = i % 2
    pltpu.make_async_copy(ybuf.at[1 - cur], ybuf.at[cur], row_sem.at[cur]).wait()
    gates = gate_ref[...]
    x = x1_ref[...]
    groups = tile // SUBLANES
    for k in range(TOP_K):
        yk = ybuf[cur, k * groups:(k + 1) * groups].reshape(tile, x.shape[1])
        x = x + gates[:, k:k + 1] * yk
    o_ref[...] = _rms(x, lnf_ref[...])


def _combine(dest, y_rows, x1, gates, ln_f_g):
    t, d = x1.shape
    tile = min(COMBINE_TILE, t)
    n_tiles = t // tile
    dest_tiles = _by_tile(dest, tile)
    return pl.pallas_call(
        functools.partial(_combine_kernel, tile=tile),
        grid=(n_tiles,),
        in_specs=[
            pl.BlockSpec(memory_space=pl.ANY),
            pl.BlockSpec(memory_space=pl.ANY),
            pl.BlockSpec((tile, d), lambda i: (i, 0)),
            pl.BlockSpec((tile, TOP_K), lambda i: (i, 0)),
            pl.BlockSpec((1, d), lambda i: (0, 0)),
        ],
        out_specs=pl.BlockSpec((tile, d), lambda i: (i, 0)),
        out_shape=jax.ShapeDtypeStruct((t, d), F32),
        scratch_shapes=[
            pltpu.SMEM((3 * TOP_K * tile,), I32),
            pltpu.VMEM((2, TOP_K * tile // SUBLANES, SUBLANES, d), F32),
            pltpu.SemaphoreType.DMA((3,)),
            pltpu.SemaphoreType.DMA((2,)),
        ],
        compiler_params=pltpu.CompilerParams(
            dimension_semantics=("arbitrary",), vmem_limit_bytes=40 * MIB),
        name="moe_combine",
    )(dest_tiles, y_rows, x1, gates, ln_f_g.reshape(1, d))


def kernel(x, ln1_g, w_in, lam_re, lam_im, log_dt, ssm_b_re, ssm_b_im, ssm_c_re, ssm_c_im,
           ssm_d, w_glu, g_sb, g_ssm, w_out, ln2_g, w_router, b_router, w_gate, b_gate,
           w_up, b_up, w_down, b_down, ln_f_g):
    bsz, seq, d = x.shape
    assert ln1_g.shape[0] == 1, "depth-1 block only"
    ssm_w = ssm_d.shape[1]
    sb = g_sb.shape[1]
    x2 = x.reshape(bsz * seq, d)
    q, k, v, u = _in_proj(x2, ln1_g[0], w_in[0], sb, ssm_w)
    y_sb = _attention(q, k, v, bsz, seq)
    y_ss = _ssm(u, bsz, seq, lam_re[0], lam_im[0], log_dt[0], ssm_b_re[0], ssm_b_im[0],
                ssm_c_re[0], ssm_c_im[0])
    x1, xn, top_idx, gates, rank, counts = _post(
        x2, y_sb, y_ss, u, ssm_d[0], w_glu[0], g_sb[0], g_ssm[0], w_out[0], ln2_g[0],
        w_router[0], b_router[0])
    dest, block_plan, fill_start, n_blocks = _plan(top_idx[:TOP_K], rank[:TOP_K], counts,
                                                   EXPERT_ROWS)
    x_sorted = _dispatch(xn, dest, fill_start, n_blocks * EXPERT_ROWS)
    y_rows = _experts(x_sorted, block_plan, n_blocks, w_gate[0], b_gate[0],
                      w_up[0], b_up[0], w_down[0], b_down[0])
    out = _combine(dest, y_rows, x1, gates, ln_f_g)
    return out.reshape(bsz, seq, d)
```

```python
import functools
import math

import jax
import jax.numpy as jnp
from jax import lax
from jax.experimental import pallas as pl
from jax.experimental.pallas import tpu as pltpu

F32 = jnp.float32
BF16 = jnp.bfloat16
I32 = jnp.int32

EPS = 1e-5
SB_HEAD_DIM = 64
SSM_GROUP = 16
SSM_STATE = 64
TOP_K = 4
SWIGLU_LIMIT = 7.0
SWIGLU_ALPHA = 1.702

LANES = 128
SUBLANES = 8
HEADS_PER_BLOCK = LANES // SB_HEAD_DIM
ATTN_BLOCK = 256
ATTN_EXP_FLOOR = 160.0
SSM_CHUNK = 16
SSM_FOLD = 8
TOKEN_TILE = 512
EXPERT_ROWS = 256
COMBINE_TILE = 256
DISPATCH_TILE = 1024
MIB = 1024 * 1024
LOG2E = 1.4426950408889634


def _rms(x, g):
    return x * lax.rsqrt(jnp.mean(x * x, axis=-1, keepdims=True) + EPS) * g


def _in_proj_kernel(x_ref, g_ref, w_ref, q_ref, k_ref, v_ref, u_ref, w_bf, *, sb, scale):
    @pl.when(pl.program_id(0) == 0)
    def _():
        w_bf[...] = w_ref[...].astype(BF16)

    h = _rms(x_ref[...], g_ref[...])
    proj = jnp.dot(h.astype(BF16), w_bf[...], preferred_element_type=F32)
    k_ref[...] = proj[:, sb:2 * sb].astype(BF16)
    u_ref[...] = proj[:, 3 * sb:]
    lane = lax.broadcasted_iota(I32, (1, LANES), 1)
    q = (proj[:, :sb] * scale).astype(BF16)
    v = proj[:, 2 * sb:3 * sb].astype(BF16)
    zero = jnp.zeros((), BF16)
    for pair in range(sb // LANES):
        cols = slice(pair * LANES, (pair + 1) * LANES)
        for head in range(HEADS_PER_BLOCK):
            own = (lane // SB_HEAD_DIM) == head
            out = slice((pair * HEADS_PER_BLOCK + head) * LANES,
                        (pair * HEADS_PER_BLOCK + head + 1) * LANES)
            q_ref[:, out] = jnp.where(own, q[:, cols], zero)
            v_ref[:, out] = jnp.where(own, v[:, cols], zero)


def _in_proj(x2, ln1_g, w_in, sb, ssm_w):
    t, d = x2.shape
    n_in = w_in.shape[1]
    tm = min(TOKEN_TILE, t)
    scale = LOG2E / math.sqrt(SB_HEAD_DIM)
    return pl.pallas_call(
        functools.partial(_in_proj_kernel, sb=sb, scale=scale),
        grid=(t // tm,),
        in_specs=[
            pl.BlockSpec((tm, d), lambda i: (i, 0)),
            pl.BlockSpec((1, d), lambda i: (0, 0)),
            pl.BlockSpec((d, n_in), lambda i: (0, 0)),
        ],
        out_specs=[
            pl.BlockSpec((tm, HEADS_PER_BLOCK * sb), lambda i: (i, 0)),
            pl.BlockSpec((tm, sb), lambda i: (i, 0)),
            pl.BlockSpec((tm, HEADS_PER_BLOCK * sb), lambda i: (i, 0)),
            pl.BlockSpec((tm, ssm_w), lambda i: (i, 0)),
        ],
        out_shape=[
            jax.ShapeDtypeStruct((t, HEADS_PER_BLOCK * sb), BF16),
            jax.ShapeDtypeStruct((t, sb), BF16),
            jax.ShapeDtypeStruct((t, HEADS_PER_BLOCK * sb), BF16),
            jax.ShapeDtypeStruct((t, ssm_w), F32),
        ],
        scratch_shapes=[pltpu.VMEM((d, n_in), BF16)],
        compiler_params=pltpu.CompilerParams(
            dimension_semantics=("arbitrary",), vmem_limit_bytes=48 * MIB),
        name="in_proj",
    )(x2, ln1_g.reshape(1, d), w_in)


def _attn_kernel(ti_ref, tj_ref, q_ref, k_ref, v_ref, tri_ref, mask_ref, o_ref,
                 dbuf, spbuf, lbuf, rsbuf, acc_ref, r_ref, *, blk, n_blk, n_items):
    sign_bit = jnp.uint32(0x80000000)
    last = n_items - 1
    ALL, CAUSAL, NONE = 0, 1, 2

    for ref in (dbuf, spbuf, lbuf, rsbuf, acc_ref, r_ref):
        ref[...] = jnp.zeros_like(ref)

    def both_heads(ref, start):
        rows = pl.ds(pl.multiple_of(start, blk), blk)
        return jnp.concatenate([ref[rows, h * LANES:(h + 1) * LANES]
                                for h in range(HEADS_PER_BLOCK)], axis=0)

    def item(idx):
        c = jnp.clip(idx, 0, last)
        return ti_ref[c], tj_ref[c]

    half = n_blk // 2
    stream_end = (half * (half + 1) // 2, n_items)

    def step(state, x):
        cur, p1, p2, out_blk = list(state[:2]), state[2], state[3], list(state[4:])
        slot, other = x, 1 - x
        c1 = jnp.where(cur[x] < stream_end[x], cur[x], n_items)
        c3 = p2
        i1, j1 = item(c1)
        ks = k_ref[pl.ds(pl.multiple_of(j1 * blk, blk), blk), :]
        w = lax.dot_general(both_heads(q_ref, i1 * blk), ks, (((1,), (1,)), ((), ())),
                            preferred_element_type=F32)
        i3, j3 = item(c3)
        valid = c3 <= last
        first = jnp.logical_and(i3 == j3, valid)
        mask3 = mask_ref[jnp.where(valid, (i3 == j3).astype(I32), NONE)]
        i3 = jnp.where(valid, i3, out_blk[x])
        r_prev = jnp.where(first, 0.0, r_ref[x])
        r_new = r_prev + rsbuf[other]
        r_ref[x] = r_new
        done = jnp.logical_and(valid, jnp.min(r_new) >= ATTN_EXP_FLOOR)
        next_block = lax.shift_right_logical((i3 + 1) * (i3 + 2), 1)
        advanced = jnp.where(done, jnp.maximum(cur[x] + 1, next_block), cur[x] + 1)
        cur[x] = jnp.minimum(advanced, stream_end[x])
        out_blk[x] = i3
        r_wide = jnp.concatenate([r_prev] * (blk // LANES), axis=1)
        a = jnp.exp2(dbuf[slot] - (lbuf[other] + r_wide))
        ab = a.astype(BF16) * jnp.concatenate([mask3, mask3], axis=0)
        a_cat = jnp.concatenate([ab[:blk], ab[blk:]], axis=1)
        acc = jnp.where(first, 0.0, acc_ref[x]) + jnp.dot(
            a_cat, both_heads(v_ref, j3 * blk), preferred_element_type=F32)
        acc_ref[x] = acc
        o_ref[pl.ds(pl.multiple_of(i3 * blk, blk), blk), :] = acc.astype(o_ref.dtype)
        sums = jnp.dot(spbuf[other], tri_ref[...], preferred_element_type=F32)
        lbuf[slot] = sums[:, :blk]
        rsbuf[slot] = sums[:, blk:]
        mask1 = mask_ref[(i1 == j1).astype(I32)]
        neg_abs = lax.bitcast_convert_type(
            lax.bitcast_convert_type(w, jnp.uint32) | sign_bit, F32)
        sp2 = jnp.maximum(w, 0.0) + jnp.log(1.0 + jnp.exp2(neg_abs)) * LOG2E
        dbuf[slot] = w - sp2
        spbuf[slot] = sp2.astype(BF16) * jnp.concatenate([mask1, mask1], axis=0)
        return cur[0], cur[1], c1, p1, out_blk[0], out_blk[1]

    def in_flight(state):
        cur0, cur1, p1, p2 = state[:4]
        return functools.reduce(jnp.logical_or, [cur0 < stream_end[0], cur1 < stream_end[1],
                                                 p1 <= last, p2 <= last])

    def four_steps(state):
        for x in (0, 1, 0, 1):
            state = step(state, x)
        return state

    none = jnp.int32(n_items)
    lax.while_loop(in_flight, four_steps,
                   (jnp.int32(0), jnp.int32(stream_end[0]), none, none,
                    jnp.int32(0), jnp.int32(half)))


def _attention(q, k, v, bsz, seq):
    sb = k.shape[-1]
    blk = min(ATTN_BLOCK, seq)
    n_blk = seq // blk
    n_pairs = sb // LANES
    q3, k3, v3 = (a.reshape(bsz, seq, a.shape[-1]) for a in (q, k, v))
    per_head = HEADS_PER_BLOCK * LANES
    jj = lax.broadcasted_iota(I32, (blk, blk), 0)
    ss = lax.broadcasted_iota(I32, (blk, blk), 1)
    tri = jnp.concatenate([(jj > ss).astype(BF16), jnp.ones((blk, LANES), BF16)], axis=1)
    masks = jnp.stack([jnp.ones((blk, blk), BF16), (ss < jj).astype(BF16),
                       jnp.zeros((blk, blk), BF16)])
    items = [(i, j) for i in range(n_blk) for j in range(i, -1, -1)]
    item_i = jnp.asarray([i for i, _ in items], I32)
    item_j = jnp.asarray([j for _, j in items], I32)
    whole = lambda b, p, ti, tj: (b, 0, p)
    grid_spec = pltpu.PrefetchScalarGridSpec(
        num_scalar_prefetch=2,
        grid=(bsz, n_pairs),
        in_specs=[
            pl.BlockSpec((None, seq, per_head), whole),
            pl.BlockSpec((None, seq, LANES), whole),
            pl.BlockSpec((None, seq, per_head), whole),
            pl.BlockSpec((blk, blk + LANES), lambda b, p, ti, tj: (0, 0)),
            pl.BlockSpec((3, blk, blk), lambda b, p, ti, tj: (0, 0, 0)),
        ],
        out_specs=pl.BlockSpec((None, seq, LANES), whole),
        scratch_shapes=[
            pltpu.VMEM((2, HEADS_PER_BLOCK * blk, blk), F32),
            pltpu.VMEM((2, HEADS_PER_BLOCK * blk, blk), BF16),
            pltpu.VMEM((2, HEADS_PER_BLOCK * blk, blk), F32),
            pltpu.VMEM((2, HEADS_PER_BLOCK * blk, LANES), F32),
            pltpu.VMEM((2, blk, LANES), F32),
            pltpu.VMEM((2, HEADS_PER_BLOCK * blk, LANES), F32),
        ],
    )
    out = pl.pallas_call(
        functools.partial(_attn_kernel, blk=blk, n_blk=n_blk, n_items=len(items)),
        grid_spec=grid_spec,
        out_shape=jax.ShapeDtypeStruct((bsz, seq, sb), BF16),
        compiler_params=pltpu.CompilerParams(
            dimension_semantics=("arbitrary", "arbitrary"), vmem_limit_bytes=40 * MIB),
        name="sb_attention",
    )(item_i, item_j, q3, k3, v3, tri, masks)
    return out.reshape(bsz * seq, sb)


def _ssm_params(lam_re, lam_im, log_dt, b_re, b_im, c_re, c_im, chunk, n_chunks):
    lam = lax.complex(lam_re.astype(F32), lam_im.astype(F32))
    dt = jnp.exp(log_dt.astype(F32))[:, None]
    lam_dt = lam * dt
    lam_bar = jnp.exp(lam_dt)
    b_bar = ((lam_bar - 1.0) / lam)[:, :, None] * lax.complex(b_re.astype(F32), b_im.astype(F32))
    b_t = jnp.swapaxes(b_bar, 1, 2)
    c_mat = lax.complex(c_re.astype(F32), c_im.astype(F32))
    steps = jnp.arange(chunk + 1, dtype=F32)
    pw = jnp.exp(lam_dt[:, None, :] * steps[None, :, None])
    npw = jnp.exp(-lam_dt[:, None, :] * steps[None, :chunk, None])

    def halves(lo, hi):
        return jnp.concatenate([lo, hi], axis=-1)

    def b_side(z):
        return [halves(jnp.real(z), jnp.real(z)), halves(-jnp.imag(z), jnp.imag(z))]

    def c_side(z):
        return [halves(jnp.real(z), jnp.imag(z)), halves(jnp.imag(z), jnp.real(z))]

    time_terms = jnp.stack(
        b_side(npw)
        + b_side(pw[:, chunk - 1::-1])
        + c_side(pw[:, :chunk])
        + c_side(pw[:, 1:]), axis=1)
    chan_terms = jnp.stack(
        [halves(jnp.real(b_t), jnp.imag(b_t)), halves(jnp.imag(b_t), jnp.real(b_t)),
         halves(jnp.real(c_mat), -jnp.real(c_mat)), halves(-jnp.imag(c_mat), -jnp.imag(c_mat))],
        axis=1)
    n_steps = max(1, (n_chunks - 1).bit_length())
    powers = []
    cur = pw[:, chunk, :]
    for _ in range(n_steps):
        powers.append(cur)
        cur = cur * cur
    lam_pow = jnp.stack(powers, axis=1)
    a1 = halves(jnp.real(lam_pow), jnp.real(lam_pow))
    a2 = halves(-jnp.imag(lam_pow), jnp.imag(lam_pow))
    return time_terms, chan_terms, a1, a2


def _ssm_kernel(u_ref, time_ref, chan_ref, a1_ref, a2_ref, perm_ref, perm_t_ref, y_ref,
                u8_ref, y8_ref, toep_ref, *, n_chunks, n_steps, group):
    seq = u_ref.shape[0]
    tiles = seq // SSM_FOLD
    lc = toep_ref.shape[0]
    per_chunk = (lc // group) // SSM_FOLD
    p2 = a1_ref.shape[2]
    group_shift = group.bit_length() - 1
    nt = (((1,), (1,)), ((), ()))

    folded = jnp.concatenate(
        [u_ref[pl.ds(s, tiles, stride=SSM_FOLD), :] for s in range(SSM_FOLD)], axis=1)
    regrouped = jnp.dot(folded.astype(BF16), perm_ref[...], preferred_element_type=F32)
    n_groups = LANES // group
    for g in range(n_groups):
        u8_ref[g] = regrouped[:, g * LANES:(g + 1) * LANES]

    for g in range(n_groups):
        def table(t1, t2, v1, v2):
            full = (time_ref[g, t1][:, None, :] * chan_ref[g, v1][None, :, :]
                    + time_ref[g, t2][:, None, :] * chan_ref[g, v2][None, :, :])
            return full.reshape(lc, p2).astype(BF16)

        src = table(0, 1, 0, 1)
        to_state = table(2, 3, 0, 1)
        dst = table(4, 5, 2, 3)
        from_state = table(6, 7, 2, 3)
        cb = min(256, lc)
        for j in range(lc // cb):
            blk = lax.dot_general(src, dst[j * cb:(j + 1) * cb], nt, preferred_element_type=F32)
            s_idx = lax.broadcasted_iota(I32, (lc, cb), 0) >> group_shift
            t_idx = (lax.broadcasted_iota(I32, (lc, cb), 1) + j * cb) >> group_shift
            toep_ref[:, j * cb:(j + 1) * cb] = jnp.where(s_idx <= t_idx, blk, 0.0).astype(BF16)
        u = jnp.concatenate(
            [u8_ref[g, pl.ds(tau, n_chunks, stride=per_chunk), :] for tau in range(per_chunk)],
            axis=1).astype(BF16)
        y = jnp.dot(u, toep_ref[...], preferred_element_type=F32)
        z = jnp.dot(u, to_state, preferred_element_type=F32)
        n = lax.broadcasted_iota(I32, z.shape, 0)
        x = jnp.where(n >= 1, pltpu.roll(z, 1, 0), 0.0)
        for k in range(n_steps):
            sh = 1 << k
            xs = jnp.where(n >= sh, pltpu.roll(x, sh, 0), 0.0)
            x = (x + a1_ref[g, k:k + 1, :] * xs
                 + a2_ref[g, k:k + 1, :] * pltpu.roll(xs, p2 // 2, 1))
        y = y + lax.dot_general(x.astype(BF16), from_state, nt, preferred_element_type=F32)
        for tau in range(per_chunk):
            y8_ref[g, pl.ds(tau, n_chunks, stride=per_chunk), :] = y[:, tau * LANES:(tau + 1) * LANES]

    y8 = jnp.concatenate([y8_ref[g] for g in range(n_groups)], axis=1)
    unfolded = jnp.dot(y8.astype(BF16), perm_t_ref[...], preferred_element_type=F32)
    for s in range(SSM_FOLD):
        y_ref[pl.ds(s, tiles, stride=SSM_FOLD), :] = unfolded[:, s * LANES:(s + 1) * LANES]


def _ssm(u, bsz, seq, lam_re, lam_im, log_dt, b_re, b_im, c_re, c_im):
    g, p = lam_re.shape
    c = b_re.shape[-1]
    chunk = min(SSM_CHUNK, seq)
    n_chunks = seq // chunk
    assert n_chunks & (n_chunks - 1) == 0 and c & (c - 1) == 0, (n_chunks, c)
    assert SSM_FOLD * c == LANES and chunk % SSM_FOLD == 0
    lc = chunk * c
    per_block = LANES // c
    time_terms, chan_terms, a1, a2 = _ssm_params(
        lam_re, lam_im, log_dt, b_re, b_im, c_re, c_im, chunk, n_chunks)
    n_steps = a1.shape[1]
    src_idx = jnp.arange(SSM_FOLD * LANES, dtype=I32)
    s_of, g_of, c_of = src_idx // LANES, (src_idx % LANES) // c, src_idx % c
    dst_idx = g_of * LANES + s_of * c + c_of
    perm = (dst_idx[:, None] == src_idx[None, :]).astype(BF16)
    block = lambda cb, b: (cb, 0, 0)
    fixed = lambda cb, b: (0, 0)
    y = pl.pallas_call(
        functools.partial(_ssm_kernel, n_chunks=n_chunks, n_steps=n_steps, group=c),
        grid=(g // per_block, bsz),
        in_specs=[
            pl.BlockSpec((None, seq, LANES), lambda cb, b: (b, 0, cb)),
            pl.BlockSpec((per_block,) + time_terms.shape[1:], lambda cb, b: (cb, 0, 0, 0)),
            pl.BlockSpec((per_block,) + chan_terms.shape[1:], lambda cb, b: (cb, 0, 0, 0)),
            pl.BlockSpec((per_block, n_steps, 2 * p), block),
            pl.BlockSpec((per_block, n_steps, 2 * p), block),
            pl.BlockSpec((SSM_FOLD * LANES, SSM_FOLD * LANES), fixed),
            pl.BlockSpec((SSM_FOLD * LANES, SSM_FOLD * LANES), fixed),
        ],
        out_specs=pl.BlockSpec((None, seq, LANES), lambda cb, b: (b, 0, cb)),
        out_shape=jax.ShapeDtypeStruct((bsz, seq, g * c), F32),
        scratch_shapes=[
            pltpu.VMEM((per_block, seq // SSM_FOLD, LANES), F32),
            pltpu.VMEM((per_block, seq // SSM_FOLD, LANES), F32),
            pltpu.VMEM((lc, lc), BF16),
        ],
        compiler_params=pltpu.CompilerParams(
            dimension_semantics=("arbitrary", "arbitrary"), vmem_limit_bytes=56 * MIB),
        name="s5_chunked_scan",
    )(u.reshape(bsz, seq, g * c), time_terms, chan_terms, a1, a2, perm, perm.T)
    return y.reshape(bsz * seq, g * c)


def _post_kernel(x_ref, ysb_ref, yss_ref, u_ref, d_ref, wglu_ref, gsb_ref, gssm_ref, wout_ref,
                 ln2_ref, wrt_ref, br_ref, tri_ref, x1_ref, xn_ref, idx_ref, gate_ref, rank_ref,
                 count_ref, running_ref, wglu_bf, wout_bf, *, sb, n_exp):
    @pl.when(pl.program_id(0) == 0)
    def _():
        running_ref[...] = jnp.zeros_like(running_ref)
        wglu_bf[...] = wglu_ref[...].astype(BF16)
        wout_bf[...] = wout_ref[...].astype(BF16)

    u = u_ref[...].astype(F32)
    y = yss_ref[...].astype(F32) + d_ref[...] * u
    y = y * (0.5 * (1.0 + jnp.tanh(math.sqrt(2.0 / math.pi) * (y + 0.044715 * (y * y * y)))))
    ab = jnp.dot(y.astype(BF16), wglu_bf[...], preferred_element_type=F32)
    w = ab.shape[1] // 2
    y_ssm = ab[:, :w] * (1.0 / (1.0 + jnp.exp(-ab[:, w:])))
    m_sb = _rms(ysb_ref[...].astype(F32), gsb_ref[...])
    m_ssm = _rms(y_ssm, gssm_ref[...])
    x1 = (x_ref[...]
          + jnp.dot(m_sb.astype(BF16), wout_bf[:sb, :], preferred_element_type=F32)
          + jnp.dot(m_ssm.astype(BF16), wout_bf[sb:, :], preferred_element_type=F32))
    x1_ref[...] = x1
    xn = _rms(x1, ln2_ref[...])
    xn_ref[...] = xn
    def split(a):
        hi = a.astype(BF16)
        return hi, (a - hi.astype(F32)).astype(BF16)

    xn_hi, xn_lo = split(xn)
    wt_hi, wt_lo = split(wrt_ref[...])
    nt = (((1,), (1,)), ((), ()))
    both = lax.dot_general(jnp.concatenate([wt_hi, wt_lo], axis=0), xn_hi, nt,
                           preferred_element_type=F32)
    low = lax.dot_general(wt_hi, xn_lo, nt, preferred_element_type=F32)
    logits = (both[:n_exp] + both[n_exp:] + low) + br_ref[...]
    tokens = logits.shape[1]
    expert = lax.broadcasted_iota(I32, logits.shape, 0).astype(F32)
    out_row = lax.broadcasted_iota(I32, (SUBLANES, tokens), 0)
    idx_out = jnp.zeros((SUBLANES, tokens), F32)
    val_out = jnp.zeros((SUBLANES, tokens), F32)
    top = None
    denom = None
    work = logits
    chosen = []
    for k in range(TOP_K):
        m = jnp.max(work, axis=0, keepdims=True)
        sel = jnp.min(jnp.where(work == m, expert, float(n_exp)), axis=0, keepdims=True)
        hit = expert == sel
        chosen.append(hit)
        work = jnp.where(hit, -jnp.inf, work)
        if k == 0:
            top = m
        e = jnp.exp(m - top)
        denom = e if denom is None else denom + e
        idx_out = jnp.where(out_row == k, sel, idx_out)
        val_out = jnp.where(out_row == k, e, val_out)
    idx_ref[...] = idx_out.astype(I32)
    gates = jnp.concatenate([val_out / denom, jnp.zeros((LANES - SUBLANES, tokens), F32)], axis=0)
    gate_ref[...] = jnp.transpose(gates)[:, :TOP_K]
    member = functools.reduce(jnp.logical_or, chosen).astype(BF16)
    before = running_ref[...] + jnp.dot(member, tri_ref[...], preferred_element_type=F32)
    rank_out = jnp.zeros((SUBLANES, tokens), F32)
    for k in range(TOP_K):
        rk = jnp.sum(jnp.where(chosen[k], before, 0.0), axis=0, keepdims=True)
        rank_out = jnp.where(out_row == k, rk, rank_out)
    rank_ref[...] = rank_out.astype(I32)
    running = running_ref[...] + jnp.sum(member.astype(F32), axis=1, keepdims=True)
    running_ref[...] = running
    count_ref[...] = running


def _post(x2, y_sb, y_ss, u, ssm_d, w_glu, g_sb, g_ssm, w_out, ln2_g, w_router, b_router):
    t, d = x2.shape
    sb = y_sb.shape[1]
    w = y_ss.shape[1]
    n_exp = w_router.shape[1]
    tm = min(TOKEN_TILE, t)
    row = lambda i: (i, 0)
    fixed = lambda i: (0, 0)
    earlier = (lax.broadcasted_iota(I32, (tm, tm), 0)
               < lax.broadcasted_iota(I32, (tm, tm), 1)).astype(BF16)
    return pl.pallas_call(
        functools.partial(_post_kernel, sb=sb, n_exp=n_exp),
        grid=(t // tm,),
        in_specs=[
            pl.BlockSpec((tm, d), row),
            pl.BlockSpec((tm, sb), row),
            pl.BlockSpec((tm, w), row),
            pl.BlockSpec((tm, w), row),
            pl.BlockSpec((1, w), fixed),
            pl.BlockSpec((w, 2 * w), fixed),
            pl.BlockSpec((1, sb), fixed),
            pl.BlockSpec((1, w), fixed),
            pl.BlockSpec((sb + w, d), fixed),
            pl.BlockSpec((1, d), fixed),
            pl.BlockSpec((n_exp, d), fixed),
            pl.BlockSpec((n_exp, 1), fixed),
            pl.BlockSpec((tm, tm), fixed),
        ],
        out_specs=[
            pl.BlockSpec((tm, d), row),
            pl.BlockSpec((tm, d), row),
            pl.BlockSpec((SUBLANES, tm), lambda i: (0, i)),
            pl.BlockSpec((tm, TOP_K), row),
            pl.BlockSpec((SUBLANES, tm), lambda i: (0, i)),
            pl.BlockSpec((n_exp, 1), fixed),
        ],
        out_shape=[
            jax.ShapeDtypeStruct((t, d), F32),
            jax.ShapeDtypeStruct((t, d), F32),
            jax.ShapeDtypeStruct((SUBLANES, t), I32),
            jax.ShapeDtypeStruct((t, TOP_K), F32),
            jax.ShapeDtypeStruct((SUBLANES, t), I32),
            jax.ShapeDtypeStruct((n_exp, 1), F32),
        ],
        scratch_shapes=[pltpu.VMEM((n_exp, 1), F32), pltpu.VMEM((w, 2 * w), BF16),
                        pltpu.VMEM((sb + w, d), BF16)],
        compiler_params=pltpu.CompilerParams(
            dimension_semantics=("arbitrary",), vmem_limit_bytes=48 * MIB),
        name="post_mixer_router",
    )(x2, y_sb, y_ss, u, ssm_d.reshape(1, w), w_glu, g_sb.reshape(1, sb),
      g_ssm.reshape(1, w), w_out, ln2_g.reshape(1, d), w_router.T,
      b_router.reshape(n_exp, 1), earlier)


def _plan(top_idx, rank, counts, rows_per_block):
    k, t = top_idx.shape
    n_exp = counts.shape[0]
    counts = counts.reshape(n_exp).astype(I32)
    padded = ((counts + rows_per_block - 1) // rows_per_block) * rows_per_block
    pad_ends = jnp.cumsum(padded)
    pad_starts = pad_ends - padded
    start_of = jnp.sum(jnp.where(top_idx[None] == jnp.arange(n_exp, dtype=I32)[:, None, None],
                                 pad_starts[:, None, None], 0), axis=0)
    dest = (start_of + rank).astype(I32)
    n_rows = t * k + n_exp * rows_per_block
    n_blocks = n_rows // rows_per_block
    block_start = jnp.arange(n_blocks, dtype=I32) * rows_per_block
    block_expert = jnp.minimum(
        jnp.sum((pad_ends[None, :] <= block_start[:, None]).astype(I32), axis=1), n_exp - 1)
    n_used = (pad_ends[-1] // rows_per_block).astype(I32).reshape(1)
    fill_start = jnp.concatenate([jnp.maximum(pad_ends - rows_per_block, 0).astype(I32), n_used])
    experts = jnp.arange(n_exp, dtype=I32)
    nonempty = counts > 0
    order = jnp.cumsum(nonempty.astype(I32)) - nonempty.astype(I32)
    later_ne = jnp.where((experts[None, :] > experts[:, None]) & nonempty[None, :],
                         experts[None, :], n_exp)
    next_tbl = jnp.min(later_ne, axis=1)
    onehot = block_expert[:, None] == experts[None, :]
    w_slot = jnp.sum(jnp.where(onehot, order & 1, 0), axis=1).astype(I32)
    w_next = jnp.sum(jnp.where(onehot, next_tbl, 0), axis=1).astype(I32)
    return dest, (block_expert, n_used, w_slot, w_next), fill_start, n_blocks


def _by_tile(dest, tile):
    k, t = dest.shape
    return dest.reshape(k, t // tile, tile).transpose(1, 0, 2).reshape(k * t)


def _dispatch_kernel(fill_ref, dest_hbm, xn_ref, xs_hbm, idx_smem, zeros_ref, idx_sem, row_sem,
                     fill_sem, *, tile, n_exp, fill_rows):
    i = pl.program_id(0)
    n = pl.num_programs(0)
    per_tile = TOP_K * tile

    def idx_copy(b):
        src = dest_hbm.at[pl.ds(pl.multiple_of(b * per_tile, per_tile), per_tile)]
        dst = idx_smem.at[pl.ds(pl.multiple_of((b % 2) * per_tile, per_tile), per_tile)]
        return pltpu.make_async_copy(src, dst, idx_sem.at[b % 2])

    @pl.when(i == 0)
    def _():
        idx_copy(0).start()
        zeros_ref[...] = jnp.zeros_like(zeros_ref)
        for e in range(n_exp):
            start = pl.multiple_of(fill_ref[e], 8)
            pltpu.make_async_copy(zeros_ref, xs_hbm.at[pl.ds(start, fill_rows), 0, :],
                                  fill_sem).start()
        for e in range(n_exp):
            pltpu.make_async_copy(zeros_ref, xs_hbm.at[pl.ds(0, fill_rows), 0, :],
                                  fill_sem).wait()
        n_blocks = xs_hbm.shape[0] // fill_rows

        def fill_unused(b, carry):
            cp = pltpu.make_async_copy(
                zeros_ref, xs_hbm.at[pl.ds(pl.multiple_of(b * fill_rows, fill_rows), fill_rows), 0, :],
                fill_sem)
            cp.start()
            cp.wait()
            return carry

        lax.fori_loop(fill_ref[n_exp], n_blocks, fill_unused, 0)

    idx_copy(i).wait()

    @pl.when(i + 1 < n)
    def _():
        idx_copy(i + 1).start()

    base = pl.multiple_of((i % 2) * per_tile, per_tile)
    quarter = tile // TOP_K
    for part in range(TOP_K):
        def issue(g, carry, part=part):
            group = part * (quarter // SUBLANES) + g
            for s in range(SUBLANES):
                for k in range(TOP_K):
                    dst_row = idx_smem[base + k * tile + group * SUBLANES + s]
                    pltpu.make_async_copy(xn_ref.at[group, pl.ds(s, 1), :], xs_hbm.at[dst_row],
                                          row_sem.at[part]).start(priority=k % 2)
            return carry

        lax.fori_loop(0, quarter // SUBLANES, issue, 0)
    for part in range(TOP_K):
        pltpu.make_async_copy(xn_ref, xn_ref, row_sem.at[part]).wait()


def _dispatch(xn, dest, fill_start, n_rows):
    t, d = xn.shape
    tile = min(DISPATCH_TILE, t)
    n_exp = fill_start.shape[0] - 1
    grid_spec = pltpu.PrefetchScalarGridSpec(
        num_scalar_prefetch=1,
        grid=(t // tile,),
        in_specs=[
            pl.BlockSpec(memory_space=pl.ANY),
            pl.BlockSpec((tile // SUBLANES, SUBLANES, d), lambda i, fs: (i, 0, 0)),
        ],
        out_specs=pl.BlockSpec(memory_space=pl.ANY),
        scratch_shapes=[
            pltpu.SMEM((2 * TOP_K * tile,), I32),
            pltpu.VMEM((EXPERT_ROWS, d), F32),
            pltpu.SemaphoreType.DMA((2,)),
            pltpu.SemaphoreType.DMA((TOP_K,)),
            pltpu.SemaphoreType.DMA,
        ],
    )
    return pl.pallas_call(
        functools.partial(_dispatch_kernel, tile=tile, n_exp=n_exp, fill_rows=EXPERT_ROWS),
        grid_spec=grid_spec,
        out_shape=jax.ShapeDtypeStruct((n_rows, 1, d), F32),
        compiler_params=pltpu.CompilerParams(
            dimension_semantics=("arbitrary",), vmem_limit_bytes=40 * MIB),
        name="moe_dispatch",
    )(fill_start, _by_tile(dest, tile), xn.reshape(t // SUBLANES, SUBLANES, d))


def _expert_kernel(be_ref, nused_ref, wslot_ref, wnext_ref, xs_hbm, wg_hbm, bg_ref, wu_hbm, bu_ref,
                   wd_hbm, bd_ref, y_hbm, xbuf, ybuf, x_sem, y_sem, wg32, wu32, wd32, w_sem,
                   wg_bf, wu_bf, wd_bf, *, rows, n_exp):
    i = pl.program_id(0)
    n = nused_ref[0]

    def w_copies(e, slot):
        return [pltpu.make_async_copy(src.at[e], dst.at[slot], w_sem.at[slot])
                for src, dst in ((wg_hbm, wg32), (wu_hbm, wu32), (wd_hbm, wd32))]

    def x_copy(b):
        return pltpu.make_async_copy(
            xs_hbm.at[pl.ds(pl.multiple_of(b * rows, rows), rows), 0, :], xbuf.at[b % 2],
            x_sem.at[b % 2])

    def y_copy(b):
        return pltpu.make_async_copy(
            ybuf.at[b % 2], y_hbm.at[pl.ds(pl.multiple_of(b * rows, rows), rows), 0, :],
            y_sem.at[b % 2])

    @pl.when(i < n)
    def _():
        @pl.when(i == 0)
        def _():
            x_copy(0).start()
            for cp in w_copies(be_ref[0], wslot_ref[0]):
                cp.start()

        @pl.when(i + 1 < n)
        def _():
            x_copy(i + 1).start()

        changed = jnp.logical_or(i == 0, be_ref[i] != be_ref[jnp.maximum(i - 1, 0)])

        @pl.when(changed)
        def _():
            slot = wslot_ref[i]
            for cp in w_copies(be_ref[i], slot):
                cp.wait()

            @pl.when(wnext_ref[i] < n_exp)
            def _():
                for cp in w_copies(wnext_ref[i], 1 - slot):
                    cp.start()
            wg_bf[...] = wg32[slot].astype(BF16)
            wu_bf[...] = wu32[slot].astype(BF16)
            wd_bf[...] = wd32[slot].astype(BF16)

        @pl.when(i >= 2)
        def _():
            y_copy(i - 2).wait()

        x_copy(i).wait()
        x = xbuf[i % 2].astype(BF16)
        gate = jnp.dot(x, wg_bf[...], preferred_element_type=F32) + bg_ref[...]
        up = jnp.dot(x, wu_bf[...], preferred_element_type=F32) + bu_ref[...]
        gate = jnp.minimum(gate, SWIGLU_LIMIT)
        up = jnp.clip(up, -SWIGLU_LIMIT, SWIGLU_LIMIT)
        glu = gate * (1.0 / (1.0 + jnp.exp(-SWIGLU_ALPHA * gate)))
        hidden = ((up + 1.0) * glu).astype(BF16)
        ybuf[i % 2] = jnp.dot(hidden, wd_bf[...], preferred_element_type=F32) + bd_ref[...]
        y_copy(i).start()

        @pl.when(i == n - 1)
        def _():
            y_copy(i).wait()

            @pl.when(i >= 1)
            def _():
                y_copy(i - 1).wait()

    @pl.when(i >= n)
    def _():
        ybuf[i % 2] = jnp.zeros((rows, ybuf.shape[2]), F32)
        y_copy(i).start()
        y_copy(i).wait()


def _experts(x_sorted, block_plan, n_blocks, w_gate, b_gate, w_up, b_up, w_down, b_down):
    n_rows, _, d = x_sorted.shape
    n_exp, _, f = w_gate.shape
    rows = EXPERT_ROWS
    bmap = lambda i, be, nu, ws, wn: (be[i], 0, 0)
    hbm = pl.BlockSpec(memory_space=pl.ANY)
    grid_spec = pltpu.PrefetchScalarGridSpec(
        num_scalar_prefetch=4,
        grid=(n_blocks,),
        in_specs=[
            hbm,
            hbm, pl.BlockSpec((None, 1, f), bmap),
            hbm, pl.BlockSpec((None, 1, f), bmap),
            hbm, pl.BlockSpec((None, 1, d), bmap),
        ],
        out_specs=hbm,
        scratch_shapes=[
            pltpu.VMEM((2, rows, d), F32),
            pltpu.VMEM((2, rows, d), F32),
            pltpu.SemaphoreType.DMA((2,)),
            pltpu.SemaphoreType.DMA((2,)),
            pltpu.VMEM((2, d, f), F32),
            pltpu.VMEM((2, d, f), F32),
            pltpu.VMEM((2, f, d), F32),
            pltpu.SemaphoreType.DMA((2,)),
            pltpu.VMEM((d, f), BF16),
            pltpu.VMEM((d, f), BF16),
            pltpu.VMEM((f, d), BF16),
        ],
    )
    return pl.pallas_call(
        functools.partial(_expert_kernel, rows=rows, n_exp=n_exp),
        grid_spec=grid_spec,
        out_shape=jax.ShapeDtypeStruct((n_rows, 1, d), F32),
        compiler_params=pltpu.CompilerParams(
            dimension_semantics=("arbitrary",), vmem_limit_bytes=52 * MIB),
        name="moe_experts",
    )(*block_plan, x_sorted,
      w_gate, b_gate.reshape(n_exp, 1, f), w_up, b_up.reshape(n_exp, 1, f),
      w_down, b_down.reshape(n_exp, 1, d))


def _combine_kernel(dest_hbm, y_hbm, x1_ref, gate_ref, lnf_ref, o_ref, idx_smem, ybuf,
                    idx_sem, row_sem, *, tile):
    i = pl.program_id(0)
    n = pl.num_programs(0)
    per_tile = TOP_K * tile

    def slot_base(b):
        return pl.multiple_of((b % 3) * per_tile, per_tile)

    def idx_copy(b):
        src = dest_hbm.at[pl.ds(pl.multiple_of(b * per_tile, per_tile), per_tile)]
        return pltpu.make_async_copy(src, idx_smem.at[pl.ds(slot_base(b), per_tile)],
                                     idx_sem.at[b % 3])

    def gather(b):
        base = slot_base(b)
        buf = ybuf.at[b % 2]
        sem = row_sem.at[b % 2]

        def issue(g, carry):
            for s in range(SUBLANES):
                pltpu.make_async_copy(y_hbm.at[idx_smem[base + g * SUBLANES + s]],
                                      buf.at[g, pl.ds(s, 1), :], sem).start(priority=s % 2)
            return carry

        lax.fori_loop(0, per_tile // SUBLANES, issue, 0)

    @pl.when(i == 0)
    def _():
        idx_copy(0).start()
        idx_copy(0).wait()
        gather(0)

        @pl.when(n > 1)
        def _():
            idx_copy(1).start()

    @pl.when(i + 1 < n)
    def _():
        idx_copy(i + 1).wait()

        @pl.when(i + 2 < n)
        def _():
            idx_copy(i + 2).start()
        gather(i + 1)

    cur = i % 2
    pltpu.make_async_copy(ybuf.at[1 - cur], ybuf.at[cur], row_sem.at[cur]).wait()
    gates = gate_ref[...]
    x = x1_ref[...]
    groups = tile // SUBLANES
    for k in range(TOP_K):
        yk = ybuf[cur, k * groups:(k + 1) * groups].reshape(tile, x.shape[1])
        x = x + gates[:, k:k + 1] * yk
    o_ref[...] = _rms(x, lnf_ref[...])


def _combine(dest, y_rows, x1, gates, ln_f_g):
    t, d = x1.shape
    tile = min(COMBINE_TILE, t)
    n_tiles = t // tile
    dest_tiles = _by_tile(dest, tile)
    return pl.pallas_call(
        functools.partial(_combine_kernel, tile=tile),
        grid=(n_tiles,),
        in_specs=[
            pl.BlockSpec(memory_space=pl.ANY),
            pl.BlockSpec(memory_space=pl.ANY),
            pl.BlockSpec((tile, d), lambda i: (i, 0)),
            pl.BlockSpec((tile, TOP_K), lambda i: (i, 0)),
            pl.BlockSpec((1, d), lambda i: (0, 0)),
        ],
        out_specs=pl.BlockSpec((tile, d), lambda i: (i, 0)),
        out_shape=jax.ShapeDtypeStruct((t, d), F32),
        scratch_shapes=[
            pltpu.SMEM((3 * TOP_K * tile,), I32),
            pltpu.VMEM((2, TOP_K * tile // SUBLANES, SUBLANES, d), F32),
            pltpu.SemaphoreType.DMA((3,)),
            pltpu.SemaphoreType.DMA((2,)),
        ],
        compiler_params=pltpu.CompilerParams(
            dimension_semantics=("arbitrary",), vmem_limit_bytes=40 * MIB),
        name="moe_combine",
    )(dest_tiles, y_rows, x1, gates, ln_f_g.reshape(1, d))


def kernel(x, ln1_g, w_in, lam_re, lam_im, log_dt, ssm_b_re, ssm_b_im, ssm_c_re, ssm_c_im,
           ssm_d, w_glu, g_sb, g_ssm, w_out, ln2_g, w_router, b_router, w_gate, b_gate,
           w_up, b_up, w_down, b_down, ln_f_g):
    bsz, seq, d = x.shape
    assert ln1_g.shape[0] == 1, "depth-1 block only"
    ssm_w = ssm_d.shape[1]
    sb = g_sb.shape[1]
    x2 = x.reshape(bsz * seq, d)
    q, k, v, u = _in_proj(x2, ln1_g[0], w_in[0], sb, ssm_w)
    y_sb = _attention(q, k, v, bsz, seq)
    y_ss = _ssm(u, bsz, seq, lam_re[0], lam_im[0], log_dt[0], ssm_b_re[0], ssm_b_im[0],
                ssm_c_re[0], ssm_c_im[0])
    x1, xn, top_idx, gates, rank, counts = _post(
        x2, y_sb, y_ss, u, ssm_d[0], w_glu[0], g_sb[0], g_ssm[0], w_out[0], ln2_g[0],
        w_router[0], b_router[0])
    dest, block_plan, fill_start, n_blocks = _plan(top_idx[:TOP_K], rank[:TOP_K], counts,
                                                   EXPERT_ROWS)
    x_sorted = _dispatch(xn, dest, fill_start, n_blocks * EXPERT_ROWS)
    y_rows = _experts(x_sorted, block_plan, n_blocks, w_gate[0], b_gate[0],
                      w_up[0], b_up[0], w_down[0], b_down[0])
    out = _combine(dest, y_rows, x1, gates, ln_f_g)
    return out.reshape(bsz, seq, d)
```

```python
import functools
import math

import jax
import jax.numpy as jnp
from jax import lax
from jax.experimental import pallas as pl
from jax.experimental.pallas import tpu as pltpu

F32 = jnp.float32
BF16 = jnp.bfloat16
I32 = jnp.int32

EPS = 1e-5
SB_HEAD_DIM = 64
SSM_GROUP = 16
SSM_STATE = 64
TOP_K = 4
SWIGLU_LIMIT = 7.0
SWIGLU_ALPHA = 1.702

LANES = 128
SUBLANES = 8
HEADS_PER_BLOCK = LANES // SB_HEAD_DIM
ATTN_BLOCK = 256
ATTN_EXP_FLOOR = 160.0
ATTN_STREAMS = 4
SSM_CHUNK = 16
SSM_FOLD = 8
TOKEN_TILE = 512
EXPERT_ROWS = 256
COMBINE_TILE = 256
DISPATCH_TILE = 1024
MIB = 1024 * 1024
LOG2E = 1.4426950408889634


def _rms(x, g):
    return x * lax.rsqrt(jnp.mean(x * x, axis=-1, keepdims=True) + EPS) * g


def _in_proj_kernel(x_ref, g_ref, w_ref, q_ref, k_ref, v_ref, u_ref, w_bf, *, sb, scale):
    @pl.when(pl.program_id(0) == 0)
    def _():
        w_bf[...] = w_ref[...].astype(BF16)

    h = _rms(x_ref[...], g_ref[...])
    proj = jnp.dot(h.astype(BF16), w_bf[...], preferred_element_type=F32)
    k_ref[...] = proj[:, sb:2 * sb].astype(BF16)
    u_ref[...] = proj[:, 3 * sb:]
    lane = lax.broadcasted_iota(I32, (1, LANES), 1)
    q = (proj[:, :sb] * scale).astype(BF16)
    v = proj[:, 2 * sb:3 * sb].astype(BF16)
    zero = jnp.zeros((), BF16)
    for pair in range(sb // LANES):
        cols = slice(pair * LANES, (pair + 1) * LANES)
        for head in range(HEADS_PER_BLOCK):
            own = (lane // SB_HEAD_DIM) == head
            out = slice((pair * HEADS_PER_BLOCK + head) * LANES,
                        (pair * HEADS_PER_BLOCK + head + 1) * LANES)
            q_ref[:, out] = jnp.where(own, q[:, cols], zero)
            v_ref[:, out] = jnp.where(own, v[:, cols], zero)


def _in_proj(x2, ln1_g, w_in, sb, ssm_w):
    t, d = x2.shape
    n_in = w_in.shape[1]
    tm = min(TOKEN_TILE, t)
    scale = LOG2E / math.sqrt(SB_HEAD_DIM)
    return pl.pallas_call(
        functools.partial(_in_proj_kernel, sb=sb, scale=scale),
        grid=(t // tm,),
        in_specs=[
            pl.BlockSpec((tm, d), lambda i: (i, 0)),
            pl.BlockSpec((1, d), lambda i: (0, 0)),
            pl.BlockSpec((d, n_in), lambda i: (0, 0)),
        ],
        out_specs=[
            pl.BlockSpec((tm, HEADS_PER_BLOCK * sb), lambda i: (i, 0)),
            pl.BlockSpec((tm, sb), lambda i: (i, 0)),
            pl.BlockSpec((tm, HEADS_PER_BLOCK * sb), lambda i: (i, 0)),
            pl.BlockSpec((tm, ssm_w), lambda i: (i, 0)),
        ],
        out_shape=[
            jax.ShapeDtypeStruct((t, HEADS_PER_BLOCK * sb), BF16),
            jax.ShapeDtypeStruct((t, sb), BF16),
            jax.ShapeDtypeStruct((t, HEADS_PER_BLOCK * sb), BF16),
            jax.ShapeDtypeStruct((t, ssm_w), F32),
        ],
        scratch_shapes=[pltpu.VMEM((d, n_in), BF16)],
        compiler_params=pltpu.CompilerParams(
            dimension_semantics=("arbitrary",), vmem_limit_bytes=48 * MIB),
        name="in_proj",
    )(x2, ln1_g.reshape(1, d), w_in)


def _attn_kernel(ti_ref, tj_ref, tn_ref, q_ref, k_ref, v_ref, tri_ref, mask_ref, o_ref,
                 dbuf, spbuf, lbuf, rsbuf, acc_ref, r_ref, *, blk, n_items, streams):
    sign_bit = jnp.uint32(0x80000000)
    last = n_items - 1
    ALL, CAUSAL, NONE = 0, 1, 2

    for ref in (dbuf, spbuf, lbuf, rsbuf, acc_ref, r_ref):
        ref[...] = jnp.zeros_like(ref)

    def both_heads(ref, start):
        rows = pl.ds(pl.multiple_of(start, blk), blk)
        return jnp.concatenate([ref[rows, h * LANES:(h + 1) * LANES]
                                for h in range(HEADS_PER_BLOCK)], axis=0)

    def item(idx):
        c = jnp.clip(idx, 0, last)
        return ti_ref[c], tj_ref[c], tn_ref[c]

    n_streams = len(streams)
    stream_end = [end for _, end in streams]

    def step(state, x):
        cur, p1, p2 = list(state[:n_streams]), state[n_streams], state[n_streams + 1]
        out_blk = list(state[n_streams + 2:])
        slot, other = x % 2, 1 - x % 2
        x3 = (x - 2) % n_streams
        c1 = jnp.where(cur[x] < stream_end[x], cur[x], n_items)
        cur[x] = jnp.minimum(cur[x] + 1, stream_end[x])
        c3 = p2
        i1, j1, _ = item(c1)
        ks = k_ref[pl.ds(pl.multiple_of(j1 * blk, blk), blk), :]
        w = lax.dot_general(both_heads(q_ref, i1 * blk), ks, (((1,), (1,)), ((), ())),
                            preferred_element_type=F32)
        i3, j3, next_block = item(c3)
        valid = c3 <= last
        first = jnp.logical_and(i3 == j3, valid)
        mask3 = mask_ref[jnp.where(valid, (i3 == j3).astype(I32), NONE)]
        i3 = jnp.where(valid, i3, out_blk[x3])
        r_prev = jnp.where(first, 0.0, r_ref[x3])
        r_new = r_prev + rsbuf[other]
        r_ref[x3] = r_new
        done = jnp.logical_and(valid, jnp.min(r_new) >= ATTN_EXP_FLOOR)
        cur[x3] = jnp.where(done, jnp.maximum(cur[x3], next_block), cur[x3])
        out_blk[x3] = i3
        r_wide = jnp.concatenate([r_prev] * (blk // LANES), axis=1)
        a = jnp.exp2(dbuf[slot] - (lbuf[other] + r_wide))
        ab = a.astype(BF16) * jnp.concatenate([mask3, mask3], axis=0)
        a_cat = jnp.concatenate([ab[:blk], ab[blk:]], axis=1)
        acc = jnp.where(first, 0.0, acc_ref[x3]) + jnp.dot(
            a_cat, both_heads(v_ref, j3 * blk), preferred_element_type=F32)
        acc_ref[x3] = acc
        o_ref[pl.ds(pl.multiple_of(i3 * blk, blk), blk), :] = acc.astype(o_ref.dtype)
        sums = jnp.dot(spbuf[other], tri_ref[...], preferred_element_type=F32)
        lbuf[slot] = sums[:, :blk]
        rsbuf[slot] = sums[:, blk:]
        mask1 = mask_ref[(i1 == j1).astype(I32)]
        neg_abs = lax.bitcast_convert_type(
            lax.bitcast_convert_type(w, jnp.uint32) | sign_bit, F32)
        sp2 = jnp.maximum(w, 0.0) + jnp.log(1.0 + jnp.exp2(neg_abs)) * LOG2E
        dbuf[slot] = w - sp2
        spbuf[slot] = sp2.astype(BF16) * jnp.concatenate([mask1, mask1], axis=0)
        return (*cur, c1, p1, *out_blk)

    def in_flight(state):
        p1, p2 = state[n_streams], state[n_streams + 1]
        live = [state[x] < stream_end[x] for x in range(n_streams)]
        return functools.reduce(jnp.logical_or, live + [p1 <= last, p2 <= last])

    def one_round(state):
        for x in range(n_streams):
            state = step(state, x)
        return state

    none = jnp.int32(n_items)
    first_items = [jnp.int32(start) for start, _ in streams]
    first_blocks = [ti_ref[start] for start, _ in streams]
    lax.while_loop(in_flight, one_round, (*first_items, none, none, *first_blocks))


def _attention(q, k, v, bsz, seq):
    sb = k.shape[-1]
    blk = min(ATTN_BLOCK, seq)
    n_blk = seq // blk
    n_pairs = sb // LANES
    q3, k3, v3 = (a.reshape(bsz, seq, a.shape[-1]) for a in (q, k, v))
    per_head = HEADS_PER_BLOCK * LANES
    jj = lax.broadcasted_iota(I32, (blk, blk), 0)
    ss = lax.broadcasted_iota(I32, (blk, blk), 1)
    tri = jnp.concatenate([(jj > ss).astype(BF16), jnp.ones((blk, LANES), BF16)], axis=1)
    masks = jnp.stack([jnp.ones((blk, blk), BF16), (ss < jj).astype(BF16),
                       jnp.zeros((blk, blk), BF16)])
    n_streams = ATTN_STREAMS if n_blk % ATTN_STREAMS == 0 else 2
    assert n_blk % n_streams == 0, (n_blk, n_streams)
    items, next_block, streams = [], [], []
    for x in range(n_streams):
        start = len(items)
        for i in range(x, n_blk, n_streams):
            block_end = len(items) + i + 1
            items += [(i, j) for j in range(i, -1, -1)]
            next_block += [block_end] * (i + 1)
        streams.append((start, len(items)))
    item_i = jnp.asarray([i for i, _ in items], I32)
    item_j = jnp.asarray([j for _, j in items], I32)
    item_next = jnp.asarray(next_block, I32)
    whole = lambda b, p, ti, tj, tn: (b, 0, p)
    grid_spec = pltpu.PrefetchScalarGridSpec(
        num_scalar_prefetch=3,
        grid=(bsz, n_pairs),
        in_specs=[
            pl.BlockSpec((None, seq, per_head), whole),
            pl.BlockSpec((None, seq, LANES), whole),
            pl.BlockSpec((None, seq, per_head), whole),
            pl.BlockSpec((blk, blk + LANES), lambda b, p, ti, tj, tn: (0, 0)),
            pl.BlockSpec((3, blk, blk), lambda b, p, ti, tj, tn: (0, 0, 0)),
        ],
        out_specs=pl.BlockSpec((None, seq, LANES), whole),
        scratch_shapes=[
            pltpu.VMEM((2, HEADS_PER_BLOCK * blk, blk), F32),
            pltpu.VMEM((2, HEADS_PER_BLOCK * blk, blk), BF16),
            pltpu.VMEM((2, HEADS_PER_BLOCK * blk, blk), F32),
            pltpu.VMEM((2, HEADS_PER_BLOCK * blk, LANES), F32),
            pltpu.VMEM((n_streams, blk, LANES), F32),
            pltpu.VMEM((n_streams, HEADS_PER_BLOCK * blk, LANES), F32),
        ],
    )
    out = pl.pallas_call(
        functools.partial(_attn_kernel, blk=blk, n_items=len(items), streams=tuple(streams)),
        grid_spec=grid_spec,
        out_shape=jax.ShapeDtypeStruct((bsz, seq, sb), BF16),
        compiler_params=pltpu.CompilerParams(
            dimension_semantics=("arbitrary", "arbitrary"), vmem_limit_bytes=40 * MIB),
        name="sb_attention",
    )(item_i, item_j, item_next, q3, k3, v3, tri, masks)
    return out.reshape(bsz * seq, sb)


def _ssm_params(lam_re, lam_im, log_dt, b_re, b_im, c_re, c_im, chunk, n_chunks):
    lam = lax.complex(lam_re.astype(F32), lam_im.astype(F32))
    dt = jnp.exp(log_dt.astype(F32))[:, None]
    lam_dt = lam * dt
    lam_bar = jnp.exp(lam_dt)
    b_bar = ((lam_bar - 1.0) / lam)[:, :, None] * lax.complex(b_re.astype(F32), b_im.astype(F32))
    b_t = jnp.swapaxes(b_bar, 1, 2)
    c_mat = lax.complex(c_re.astype(F32), c_im.astype(F32))
    steps = jnp.arange(chunk + 1, dtype=F32)
    pw = jnp.exp(lam_dt[:, None, :] * steps[None, :, None])
    npw = jnp.exp(-lam_dt[:, None, :] * steps[None, :chunk, None])

    def halves(lo, hi):
        return jnp.concatenate([lo, hi], axis=-1)

    def b_side(z):
        return [halves(jnp.real(z), jnp.real(z)), halves(-jnp.imag(z), jnp.imag(z))]

    def c_side(z):
        return [halves(jnp.real(z), jnp.imag(z)), halves(jnp.imag(z), jnp.real(z))]

    time_terms = jnp.stack(
        b_side(npw)
        + b_side(pw[:, chunk - 1::-1])
        + c_side(pw[:, :chunk])
        + c_side(pw[:, 1:]), axis=1)
    chan_terms = jnp.stack(
        [halves(jnp.real(b_t), jnp.imag(b_t)), halves(jnp.imag(b_t), jnp.real(b_t)),
         halves(jnp.real(c_mat), -jnp.real(c_mat)), halves(-jnp.imag(c_mat), -jnp.imag(c_mat))],
        axis=1)
    n_steps = max(1, (n_chunks - 1).bit_length())
    powers = []
    cur = pw[:, chunk, :]
    for _ in range(n_steps):
        powers.append(cur)
        cur = cur * cur
    lam_pow = jnp.stack(powers, axis=1)
    a1 = halves(jnp.real(lam_pow), jnp.real(lam_pow))
    a2 = halves(-jnp.imag(lam_pow), jnp.imag(lam_pow))
    return time_terms, chan_terms, a1, a2


def _ssm_kernel(u_ref, time_ref, chan_ref, a1_ref, a2_ref, perm_ref, perm_t_ref, y_ref,
                u8_ref, y8_ref, toep_ref, *, n_chunks, n_steps, group):
    seq = u_ref.shape[0]
    tiles = seq // SSM_FOLD
    lc = toep_ref.shape[0]
    per_chunk = (lc // group) // SSM_FOLD
    p2 = a1_ref.shape[2]
    group_shift = group.bit_length() - 1
    nt = (((1,), (1,)), ((), ()))

    folded = jnp.concatenate(
        [u_ref[pl.ds(s, tiles, stride=SSM_FOLD), :] for s in range(SSM_FOLD)], axis=1)
    regrouped = jnp.dot(folded.astype(BF16), perm_ref[...], preferred_element_type=F32)
    n_groups = LANES // group
    for g in range(n_groups):
        u8_ref[g] = regrouped[:, g * LANES:(g + 1) * LANES]

    for g in range(n_groups):
        def table(t1, t2, v1, v2):
            full = (time_ref[g, t1][:, None, :] * chan_ref[g, v1][None, :, :]
                    + time_ref[g, t2][:, None, :] * chan_ref[g, v2][None, :, :])
            return full.reshape(lc, p2).astype(BF16)

        src = table(0, 1, 0, 1)
        to_state = table(2, 3, 0, 1)
        dst = table(4, 5, 2, 3)
        from_state = table(6, 7, 2, 3)
        cb = min(256, lc)
        for j in range(lc // cb):
            blk = lax.dot_general(src, dst[j * cb:(j + 1) * cb], nt, preferred_element_type=F32)
            s_idx = lax.broadcasted_iota(I32, (lc, cb), 0) >> group_shift
            t_idx = (lax.broadcasted_iota(I32, (lc, cb), 1) + j * cb) >> group_shift
            toep_ref[:, j * cb:(j + 1) * cb] = jnp.where(s_idx <= t_idx, blk, 0.0).astype(BF16)
        u = jnp.concatenate(
            [u8_ref[g, pl.ds(tau, n_chunks, stride=per_chunk), :] for tau in range(per_chunk)],
            axis=1).astype(BF16)
        y = jnp.dot(u, toep_ref[...], preferred_element_type=F32)
        z = jnp.dot(u, to_state, preferred_element_type=F32)
        n = lax.broadcasted_iota(I32, z.shape, 0)
        x = jnp.where(n >= 1, pltpu.roll(z, 1, 0), 0.0)
        for k in range(n_steps):
            sh = 1 << k
            xs = jnp.where(n >= sh, pltpu.roll(x, sh, 0), 0.0)
            x = (x + a1_ref[g, k:k + 1, :] * xs
                 + a2_ref[g, k:k + 1, :] * pltpu.roll(xs, p2 // 2, 1))
        y = y + lax.dot_general(x.astype(BF16), from_state, nt, preferred_element_type=F32)
        for tau in range(per_chunk):
            y8_ref[g, pl.ds(tau, n_chunks, stride=per_chunk), :] = y[:, tau * LANES:(tau + 1) * LANES]

    y8 = jnp.concatenate([y8_ref[g] for g in range(n_groups)], axis=1)
    unfolded = jnp.dot(y8.astype(BF16), perm_t_ref[...], preferred_element_type=F32)
    for s in range(SSM_FOLD):
        y_ref[pl.ds(s, tiles, stride=SSM_FOLD), :] = unfolded[:, s * LANES:(s + 1) * LANES]


def _ssm(u, bsz, seq, lam_re, lam_im, log_dt, b_re, b_im, c_re, c_im):
    g, p = lam_re.shape
    c = b_re.shape[-1]
    chunk = min(SSM_CHUNK, seq)
    n_chunks = seq // chunk
    assert n_chunks & (n_chunks - 1) == 0 and c & (c - 1) == 0, (n_chunks, c)
    assert SSM_FOLD * c == LANES and chunk % SSM_FOLD == 0
    lc = chunk * c
    per_block = LANES // c
    time_terms, chan_terms, a1, a2 = _ssm_params(
        lam_re, lam_im, log_dt, b_re, b_im, c_re, c_im, chunk, n_chunks)
    n_steps = a1.shape[1]
    src_idx = jnp.arange(SSM_FOLD * LANES, dtype=I32)
    s_of, g_of, c_of = src_idx // LANES, (src_idx % LANES) // c, src_idx % c
    dst_idx = g_of * LANES + s_of * c + c_of
    perm = (dst_idx[:, None] == src_idx[None, :]).astype(BF16)
    block = lambda cb, b: (cb, 0, 0)
    fixed = lambda cb, b: (0, 0)
    y = pl.pallas_call(
        functools.partial(_ssm_kernel, n_chunks=n_chunks, n_steps=n_steps, group=c),
        grid=(g // per_block, bsz),
        in_specs=[
            pl.BlockSpec((None, seq, LANES), lambda cb, b: (b, 0, cb)),
            pl.BlockSpec((per_block,) + time_terms.shape[1:], lambda cb, b: (cb, 0, 0, 0)),
            pl.BlockSpec((per_block,) + chan_terms.shape[1:], lambda cb, b: (cb, 0, 0, 0)),
            pl.BlockSpec((per_block, n_steps, 2 * p), block),
            pl.BlockSpec((per_block, n_steps, 2 * p), block),
            pl.BlockSpec((SSM_FOLD * LANES, SSM_FOLD * LANES), fixed),
            pl.BlockSpec((SSM_FOLD * LANES, SSM_FOLD * LANES), fixed),
        ],
        out_specs=pl.BlockSpec((None, seq, LANES), lambda cb, b: (b, 0, cb)),
        out_shape=jax.ShapeDtypeStruct((bsz, seq, g * c), F32),
        scratch_shapes=[
            pltpu.VMEM((per_block, seq // SSM_FOLD, LANES), F32),
            pltpu.VMEM((per_block, seq // SSM_FOLD, LANES), F32),
            pltpu.VMEM((lc, lc), BF16),
        ],
        compiler_params=pltpu.CompilerParams(
            dimension_semantics=("arbitrary", "arbitrary"), vmem_limit_bytes=56 * MIB),
        name="s5_chunked_scan",
    )(u.reshape(bsz, seq, g * c), time_terms, chan_terms, a1, a2, perm, perm.T)
    return y.reshape(bsz * seq, g * c)


def _post_kernel(x_ref, ysb_ref, yss_ref, u_ref, d_ref, wglu_ref, gsb_ref, gssm_ref, wout_ref,
                 ln2_ref, wrt_ref, br_ref, tri_ref, x1_ref, xn_ref, idx_ref, gate_ref, rank_ref,
                 count_ref, running_ref, wglu_bf, wout_bf, *, sb, n_exp):
    @pl.when(pl.program_id(0) == 0)
    def _():
        running_ref[...] = jnp.zeros_like(running_ref)
        wglu_bf[...] = wglu_ref[...].astype(BF16)
        wout_bf[...] = wout_ref[...].astype(BF16)

    u = u_ref[...].astype(F32)
    y = yss_ref[...].astype(F32) + d_ref[...] * u
    y = y * (0.5 * (1.0 + jnp.tanh(math.sqrt(2.0 / math.pi) * (y + 0.044715 * (y * y * y)))))
    ab = jnp.dot(y.astype(BF16), wglu_bf[...], preferred_element_type=F32)
    w = ab.shape[1] // 2
    y_ssm = ab[:, :w] * (1.0 / (1.0 + jnp.exp(-ab[:, w:])))
    m_sb = _rms(ysb_ref[...].astype(F32), gsb_ref[...])
    m_ssm = _rms(y_ssm, gssm_ref[...])
    x1 = (x_ref[...]
          + jnp.dot(m_sb.astype(BF16), wout_bf[:sb, :], preferred_element_type=F32)
          + jnp.dot(m_ssm.astype(BF16), wout_bf[sb:, :], preferred_element_type=F32))
    x1_ref[...] = x1
    xn = _rms(x1, ln2_ref[...])
    xn_ref[...] = xn
    def split(a):
        hi = a.astype(BF16)
        return hi, (a - hi.astype(F32)).astype(BF16)

    xn_hi, xn_lo = split(xn)
    wt_hi, wt_lo = split(wrt_ref[...])
    nt = (((1,), (1,)), ((), ()))
    both = lax.dot_general(jnp.concatenate([wt_hi, wt_lo], axis=0), xn_hi, nt,
                           preferred_element_type=F32)
    low = lax.dot_general(wt_hi, xn_lo, nt, preferred_element_type=F32)
    logits = (both[:n_exp] + both[n_exp:] + low) + br_ref[...]
    tokens = logits.shape[1]
    expert = lax.broadcasted_iota(I32, logits.shape, 0).astype(F32)
    out_row = lax.broadcasted_iota(I32, (SUBLANES, tokens), 0)
    idx_out = jnp.zeros((SUBLANES, tokens), F32)
    val_out = jnp.zeros((SUBLANES, tokens), F32)
    top = None
    denom = None
    work = logits
    chosen = []
    for k in range(TOP_K):
        m = jnp.max(work, axis=0, keepdims=True)
        sel = jnp.min(jnp.where(work == m, expert, float(n_exp)), axis=0, keepdims=True)
        hit = expert == sel
        chosen.append(hit)
        work = jnp.where(hit, -jnp.inf, work)
        if k == 0:
            top = m
        e = jnp.exp(m - top)
        denom = e if denom is None else denom + e
        idx_out = jnp.where(out_row == k, sel, idx_out)
        val_out = jnp.where(out_row == k, e, val_out)
    idx_ref[...] = idx_out.astype(I32)
    gates = jnp.concatenate([val_out / denom, jnp.zeros((LANES - SUBLANES, tokens), F32)], axis=0)
    gate_ref[...] = jnp.transpose(gates)[:, :TOP_K]
    member = functools.reduce(jnp.logical_or, chosen).astype(BF16)
    before = running_ref[...] + jnp.dot(member, tri_ref[...], preferred_element_type=F32)
    rank_out = jnp.zeros((SUBLANES, tokens), F32)
    for k in range(TOP_K):
        rk = jnp.sum(jnp.where(chosen[k], before, 0.0), axis=0, keepdims=True)
        rank_out = jnp.where(out_row == k, rk, rank_out)
    rank_ref[...] = rank_out.astype(I32)
    running = running_ref[...] + jnp.sum(member.astype(F32), axis=1, keepdims=True)
    running_ref[...] = running
    count_ref[...] = running


def _post(x2, y_sb, y_ss, u, ssm_d, w_glu, g_sb, g_ssm, w_out, ln2_g, w_router, b_router):
    t, d = x2.shape
    sb = y_sb.shape[1]
    w = y_ss.shape[1]
    n_exp = w_router.shape[1]
    tm = min(TOKEN_TILE, t)
    row = lambda i: (i, 0)
    fixed = lambda i: (0, 0)
    earlier = (lax.broadcasted_iota(I32, (tm, tm), 0)
               < lax.broadcasted_iota(I32, (tm, tm), 1)).astype(BF16)
    return pl.pallas_call(
        functools.partial(_post_kernel, sb=sb, n_exp=n_exp),
        grid=(t // tm,),
        in_specs=[
            pl.BlockSpec((tm, d), row),
            pl.BlockSpec((tm, sb), row),
            pl.BlockSpec((tm, w), row),
            pl.BlockSpec((tm, w), row),
            pl.BlockSpec((1, w), fixed),
            pl.BlockSpec((w, 2 * w), fixed),
            pl.BlockSpec((1, sb), fixed),
            pl.BlockSpec((1, w), fixed),
            pl.BlockSpec((sb + w, d), fixed),
            pl.BlockSpec((1, d), fixed),
            pl.BlockSpec((n_exp, d), fixed),
            pl.BlockSpec((n_exp, 1), fixed),
            pl.BlockSpec((tm, tm), fixed),
        ],
        out_specs=[
            pl.BlockSpec((tm, d), row),
            pl.BlockSpec((tm, d), row),
            pl.BlockSpec((SUBLANES, tm), lambda i: (0, i)),
            pl.BlockSpec((tm, TOP_K), row),
            pl.BlockSpec((SUBLANES, tm), lambda i: (0, i)),
            pl.BlockSpec((n_exp, 1), fixed),
        ],
        out_shape=[
            jax.ShapeDtypeStruct((t, d), F32),
            jax.ShapeDtypeStruct((t, d), F32),
            jax.ShapeDtypeStruct((SUBLANES, t), I32),
            jax.ShapeDtypeStruct((t, TOP_K), F32),
            jax.ShapeDtypeStruct((SUBLANES, t), I32),
            jax.ShapeDtypeStruct((n_exp, 1), F32),
        ],
        scratch_shapes=[pltpu.VMEM((n_exp, 1), F32), pltpu.VMEM((w, 2 * w), BF16),
                        pltpu.VMEM((sb + w, d), BF16)],
        compiler_params=pltpu.CompilerParams(
            dimension_semantics=("arbitrary",), vmem_limit_bytes=48 * MIB),
        name="post_mixer_router",
    )(x2, y_sb, y_ss, u, ssm_d.reshape(1, w), w_glu, g_sb.reshape(1, sb),
      g_ssm.reshape(1, w), w_out, ln2_g.reshape(1, d), w_router.T,
      b_router.reshape(n_exp, 1), earlier)


def _plan(top_idx, rank, counts, rows_per_block):
    k, t = top_idx.shape
    n_exp = counts.shape[0]
    counts = counts.reshape(n_exp).astype(I32)
    padded = ((counts + rows_per_block - 1) // rows_per_block) * rows_per_block
    pad_ends = jnp.cumsum(padded)
    pad_starts = pad_ends - padded
    start_of = jnp.sum(jnp.where(top_idx[None] == jnp.arange(n_exp, dtype=I32)[:, None, None],
                                 pad_starts[:, None, None], 0), axis=0)
    dest = (start_of + rank).astype(I32)
    n_rows = t * k + n_exp * rows_per_block
    n_blocks = n_rows // rows_per_block
    block_start = jnp.arange(n_blocks, dtype=I32) * rows_per_block
    block_expert = jnp.minimum(
        jnp.sum((pad_ends[None, :] <= block_start[:, None]).astype(I32), axis=1), n_exp - 1)
    n_used = (pad_ends[-1] // rows_per_block).astype(I32).reshape(1)
    fill_start = jnp.concatenate([jnp.maximum(pad_ends - rows_per_block, 0).astype(I32), n_used])
    experts = jnp.arange(n_exp, dtype=I32)
    nonempty = counts > 0
    order = jnp.cumsum(nonempty.astype(I32)) - nonempty.astype(I32)
    later_ne = jnp.where((experts[None, :] > experts[:, None]) & nonempty[None, :],
                         experts[None, :], n_exp)
    next_tbl = jnp.min(later_ne, axis=1)
    onehot = block_expert[:, None] == experts[None, :]
    w_slot = jnp.sum(jnp.where(onehot, order & 1, 0), axis=1).astype(I32)
    w_next = jnp.sum(jnp.where(onehot, next_tbl, 0), axis=1).astype(I32)
    return dest, (block_expert, n_used, w_slot, w_next), fill_start, n_blocks


def _by_tile(dest, tile):
    k, t = dest.shape
    return dest.reshape(k, t // tile, tile).transpose(1, 0, 2).reshape(k * t)


def _dispatch_kernel(fill_ref, dest_hbm, xn_ref, xs_hbm, idx_smem, zeros_ref, idx_sem, row_sem,
                     fill_sem, *, tile, n_exp, fill_rows):
    i = pl.program_id(0)
    n = pl.num_programs(0)
    per_tile = TOP_K * tile

    def idx_copy(b):
        src = dest_hbm.at[pl.ds(pl.multiple_of(b * per_tile, per_tile), per_tile)]
        dst = idx_smem.at[pl.ds(pl.multiple_of((b % 2) * per_tile, per_tile), per_tile)]
        return pltpu.make_async_copy(src, dst, idx_sem.at[b % 2])

    @pl.when(i == 0)
    def _():
        idx_copy(0).start()
        zeros_ref[...] = jnp.zeros_like(zeros_ref)
        for e in range(n_exp):
            start = pl.multiple_of(fill_ref[e], 8)
            pltpu.make_async_copy(zeros_ref, xs_hbm.at[pl.ds(start, fill_rows), 0, :],
                                  fill_sem).start()
        for e in range(n_exp):
            pltpu.make_async_copy(zeros_ref, xs_hbm.at[pl.ds(0, fill_rows), 0, :],
                                  fill_sem).wait()
        n_blocks = xs_hbm.shape[0] // fill_rows

        def fill_unused(b, carry):
            cp = pltpu.make_async_copy(
                zeros_ref, xs_hbm.at[pl.ds(pl.multiple_of(b * fill_rows, fill_rows), fill_rows), 0, :],
                fill_sem)
            cp.start()
            cp.wait()
            return carry

        lax.fori_loop(fill_ref[n_exp], n_blocks, fill_unused, 0)

    idx_copy(i).wait()

    @pl.when(i + 1 < n)
    def _():
        idx_copy(i + 1).start()

    base = pl.multiple_of((i % 2) * per_tile, per_tile)
    quarter = tile // TOP_K
    for part in range(TOP_K):
        def issue(g, carry, part=part):
            group = part * (quarter // SUBLANES) + g
            for s in range(SUBLANES):
                for k in range(TOP_K):
                    dst_row = idx_smem[base + k * tile + group * SUBLANES + s]
                    pltpu.make_async_copy(xn_ref.at[group, pl.ds(s, 1), :], xs_hbm.at[dst_row],
                                          row_sem.at[part]).start(priority=k % 2)
            return carry

        lax.fori_loop(0, quarter // SUBLANES, issue, 0)
    for part in range(TOP_K):
        pltpu.make_async_copy(xn_ref, xn_ref, row_sem.at[part]).wait()


def _dispatch(xn, dest, fill_start, n_rows):
    t, d = xn.shape
    tile = min(DISPATCH_TILE, t)
    n_exp = fill_start.shape[0] - 1
    grid_spec = pltpu.PrefetchScalarGridSpec(
        num_scalar_prefetch=1,
        grid=(t // tile,),
        in_specs=[
            pl.BlockSpec(memory_space=pl.ANY),
            pl.BlockSpec((tile // SUBLANES, SUBLANES, d), lambda i, fs: (i, 0, 0)),
        ],
        out_specs=pl.BlockSpec(memory_space=pl.ANY),
        scratch_shapes=[
            pltpu.SMEM((2 * TOP_K * tile,), I32),
            pltpu.VMEM((EXPERT_ROWS, d), F32),
            pltpu.SemaphoreType.DMA((2,)),
            pltpu.SemaphoreType.DMA((TOP_K,)),
            pltpu.SemaphoreType.DMA,
        ],
    )
    return pl.pallas_call(
        functools.partial(_dispatch_kernel, tile=tile, n_exp=n_exp, fill_rows=EXPERT_ROWS),
        grid_spec=grid_spec,
        out_shape=jax.ShapeDtypeStruct((n_rows, 1, d), F32),
        compiler_params=pltpu.CompilerParams(
            dimension_semantics=("arbitrary",), vmem_limit_bytes=40 * MIB),
        name="moe_dispatch",
    )(fill_start, _by_tile(dest, tile), xn.reshape(t // SUBLANES, SUBLANES, d))


def _expert_kernel(be_ref, nused_ref, wslot_ref, wnext_ref, xs_hbm, wg_hbm, bg_ref, wu_hbm, bu_ref,
                   wd_hbm, bd_ref, y_hbm, xbuf, ybuf, x_sem, y_sem, wg32, wu32, wd32, w_sem,
                   wg_bf, wu_bf, wd_bf, *, rows, n_exp):
    i = pl.program_id(0)
    n = nused_ref[0]

    def w_copies(e, slot):
        return [pltpu.make_async_copy(src.at[e], dst.at[slot], w_sem.at[slot])
                for src, dst in ((wg_hbm, wg32), (wu_hbm, wu32), (wd_hbm, wd32))]

    def x_copy(b):
        return pltpu.make_async_copy(
            xs_hbm.at[pl.ds(pl.multiple_of(b * rows, rows), rows), 0, :], xbuf.at[b % 2],
            x_sem.at[b % 2])

    def y_copy(b):
        return pltpu.make_async_copy(
            ybuf.at[b % 2], y_hbm.at[pl.ds(pl.multiple_of(b * rows, rows), rows), 0, :],
            y_sem.at[b % 2])

    @pl.when(i < n)
    def _():
        @pl.when(i == 0)
        def _():
            x_copy(0).start()
            for cp in w_copies(be_ref[0], wslot_ref[0]):
                cp.start()

        @pl.when(i + 1 < n)
        def _():
            x_copy(i + 1).start()

        changed = jnp.logical_or(i == 0, be_ref[i] != be_ref[jnp.maximum(i - 1, 0)])

        @pl.when(changed)
        def _():
            slot = wslot_ref[i]
            for cp in w_copies(be_ref[i], slot):
                cp.wait()

            @pl.when(wnext_ref[i] < n_exp)
            def _():
                for cp in w_copies(wnext_ref[i], 1 - slot):
                    cp.start()
            wg_bf[...] = wg32[slot].astype(BF16)
            wu_bf[...] = wu32[slot].astype(BF16)
            wd_bf[...] = wd32[slot].astype(BF16)

        @pl.when(i >= 2)
        def _():
            y_copy(i - 2).wait()

        x_copy(i).wait()
        x = xbuf[i % 2].astype(BF16)
        gate = jnp.dot(x, wg_bf[...], preferred_element_type=F32) + bg_ref[...]
        up = jnp.dot(x, wu_bf[...], preferred_element_type=F32) + bu_ref[...]
        gate = jnp.minimum(gate, SWIGLU_LIMIT)
        up = jnp.clip(up, -SWIGLU_LIMIT, SWIGLU_LIMIT)
        glu = gate * (1.0 / (1.0 + jnp.exp(-SWIGLU_ALPHA * gate)))
        hidden = ((up + 1.0) * glu).astype(BF16)
        ybuf[i % 2] = jnp.dot(hidden, wd_bf[...], preferred_element_type=F32) + bd_ref[...]
        y_copy(i).start()

        @pl.when(i == n - 1)
        def _():
            y_copy(i).wait()

            @pl.when(i >= 1)
            def _():
                y_copy(i - 1).wait()

    @pl.when(i >= n)
    def _():
        ybuf[i % 2] = jnp.zeros((rows, ybuf.shape[2]), F32)
        y_copy(i).start()
        y_copy(i).wait()


def _experts(x_sorted, block_plan, n_blocks, w_gate, b_gate, w_up, b_up, w_down, b_down):
    n_rows, _, d = x_sorted.shape
    n_exp, _, f = w_gate.shape
    rows = EXPERT_ROWS
    bmap = lambda i, be, nu, ws, wn: (be[i], 0, 0)
    hbm = pl.BlockSpec(memory_space=pl.ANY)
    grid_spec = pltpu.PrefetchScalarGridSpec(
        num_scalar_prefetch=4,
        grid=(n_blocks,),
        in_specs=[
            hbm,
            hbm, pl.BlockSpec((None, 1, f), bmap),
            hbm, pl.BlockSpec((None, 1, f), bmap),
            hbm, pl.BlockSpec((None, 1, d), bmap),
        ],
        out_specs=hbm,
        scratch_shapes=[
            pltpu.VMEM((2, rows, d), F32),
            pltpu.VMEM((2, rows, d), F32),
            pltpu.SemaphoreType.DMA((2,)),
            pltpu.SemaphoreType.DMA((2,)),
            pltpu.VMEM((2, d, f), F32),
            pltpu.VMEM((2, d, f), F32),
            pltpu.VMEM((2, f, d), F32),
            pltpu.SemaphoreType.DMA((2,)),
            pltpu.VMEM((d, f), BF16),
            pltpu.VMEM((d, f), BF16),
            pltpu.VMEM((f, d), BF16),
        ],
    )
    return pl.pallas_call(
        functools.partial(_expert_kernel, rows=rows, n_exp=n_exp),
        grid_spec=grid_spec,
        out_shape=jax.ShapeDtypeStruct((n_rows, 1, d), F32),
        compiler_params=pltpu.CompilerParams(
            dimension_semantics=("arbitrary",), vmem_limit_bytes=52 * MIB),
        name="moe_experts",
    )(*block_plan, x_sorted,
      w_gate, b_gate.reshape(n_exp, 1, f), w_up, b_up.reshape(n_exp, 1, f),
      w_down, b_down.reshape(n_exp, 1, d))


def _combine_kernel(dest_hbm, y_hbm, x1_ref, gate_ref, lnf_ref, o_ref, idx_smem, ybuf,
                    idx_sem, row_sem, *, tile):
    i = pl.program_id(0)
    n = pl.num_programs(0)
    per_tile = TOP_K * tile

    def slot_base(b):
        return pl.multiple_of((b % 3) * per_tile, per_tile)

    def idx_copy(b):
        src = dest_hbm.at[pl.ds(pl.multiple_of(b * per_tile, per_tile), per_tile)]
        return pltpu.make_async_copy(src, idx_smem.at[pl.ds(slot_base(b), per_tile)],
                                     idx_sem.at[b % 3])

    def gather(b):
        base = slot_base(b)
        buf = ybuf.at[b % 2]
        sem = row_sem.at[b % 2]

        def issue(g, carry):
            for s in range(SUBLANES):
                pltpu.make_async_copy(y_hbm.at[idx_smem[base + g * SUBLANES + s]],
                                      buf.at[g, pl.ds(s, 1), :], sem).start(priority=s % 2)
            return carry

        lax.fori_loop(0, per_tile // SUBLANES, issue, 0)

    @pl.when(i == 0)
    def _():
        idx_copy(0).start()
        idx_copy(0).wait()
        gather(0)

        @pl.when(n > 1)
        def _():
            idx_copy(1).start()

    @pl.when(i + 1 < n)
    def _():
        idx_copy(i + 1).wait()

        @pl.when(i + 2 < n)
        def _():
            idx_copy(i + 2).start()
        gather(i + 1)

    cur = i % 2
    pltpu.make_async_copy(ybuf.at[1 - cur], ybuf.at[cur], row_sem.at[cur]).wait()
    gates = gate_ref[...]
    x = x1_ref[...]
    groups = tile // SUBLANES
    for k in range(TOP_K):
        yk = ybuf[cur, k * groups:(k + 1) * groups].reshape(tile, x.shape[1])
        x = x + gates[:, k:k + 1] * yk
    o_ref[...] = _rms(x, lnf_ref[...])


def _combine(dest, y_rows, x1, gates, ln_f_g):
    t, d = x1.shape
    tile = min(COMBINE_TILE, t)
    n_tiles = t // tile
    dest_tiles = _by_tile(dest, tile)
    return pl.pallas_call(
        functools.partial(_combine_kernel, tile=tile),
        grid=(n_tiles,),
        in_specs=[
            pl.BlockSpec(memory_space=pl.ANY),
            pl.BlockSpec(memory_space=pl.ANY),
            pl.BlockSpec((tile, d), lambda i: (i, 0)),
            pl.BlockSpec((tile, TOP_K), lambda i: (i, 0)),
            pl.BlockSpec((1, d), lambda i: (0, 0)),
        ],
        out_specs=pl.BlockSpec((tile, d), lambda i: (i, 0)),
        out_shape=jax.ShapeDtypeStruct((t, d), F32),
        scratch_shapes=[
            pltpu.SMEM((3 * TOP_K * tile,), I32),
            pltpu.VMEM((2, TOP_K * tile // SUBLANES, SUBLANES, d), F32),
            pltpu.SemaphoreType.DMA((3,)),
            pltpu.SemaphoreType.DMA((2,)),
        ],
        compiler_params=pltpu.CompilerParams(
            dimension_semantics=("arbitrary",), vmem_limit_bytes=40 * MIB),
        name="moe_combine",
    )(dest_tiles, y_rows, x1, gates, ln_f_g.reshape(1, d))


def kernel(x, ln1_g, w_in, lam_re, lam_im, log_dt, ssm_b_re, ssm_b_im, ssm_c_re, ssm_c_im,
           ssm_d, w_glu, g_sb, g_ssm, w_out, ln2_g, w_router, b_router, w_gate, b_gate,
           w_up, b_up, w_down, b_down, ln_f_g):
    bsz, seq, d = x.shape
    assert ln1_g.shape[0] == 1, "depth-1 block only"
    ssm_w = ssm_d.shape[1]
    sb = g_sb.shape[1]
    x2 = x.reshape(bsz * seq, d)
    q, k, v, u = _in_proj(x2, ln1_g[0], w_in[0], sb, ssm_w)
    y_sb = _attention(q, k, v, bsz, seq)
    y_ss = _ssm(u, bsz, seq, lam_re[0], lam_im[0], log_dt[0], ssm_b_re[0], ssm_b_im[0],
                ssm_c_re[0], ssm_c_im[0])
    x1, xn, top_idx, gates, rank, counts = _post(
        x2, y_sb, y_ss, u, ssm_d[0], w_glu[0], g_sb[0], g_ssm[0], w_out[0], ln2_g[0],
        w_router[0], b_router[0])
    dest, block_plan, fill_start, n_blocks = _plan(top_idx[:TOP_K], rank[:TOP_K], counts,
                                                   EXPERT_ROWS)
    x_sorted = _dispatch(xn, dest, fill_start, n_blocks * EXPERT_ROWS)
    y_rows = _experts(x_sorted, block_plan, n_blocks, w_gate[0], b_gate[0],
                      w_up[0], b_up[0], w_down[0], b_down[0])
    out = _combine(dest, y_rows, x1, gates, ln_f_g)
    return out.reshape(bsz, seq, d)
```

```python
import functools
import math

import jax
import jax.numpy as jnp
from jax import lax
from jax.experimental import pallas as pl
from jax.experimental.pallas import tpu as pltpu

F32 = jnp.float32
BF16 = jnp.bfloat16
I32 = jnp.int32

EPS = 1e-5
SB_HEAD_DIM = 64
SSM_GROUP = 16
SSM_STATE = 64
TOP_K = 4
SWIGLU_LIMIT = 7.0
SWIGLU_ALPHA = 1.702

LANES = 128
SUBLANES = 8
HEADS_PER_BLOCK = LANES // SB_HEAD_DIM
ATTN_BLOCK = 256
ATTN_EXP_FLOOR = 160.0
ATTN_STREAMS = 4
SSM_CHUNK = 16
SSM_FOLD = 8
TOKEN_TILE = 512
EXPERT_ROWS = 256
COMBINE_TILE = 256
DISPATCH_TILE = 1024
MIB = 1024 * 1024
LOG2E = 1.4426950408889634


def _rms(x, g):
    return x * lax.rsqrt(jnp.mean(x * x, axis=-1, keepdims=True) + EPS) * g


def _in_proj_kernel(x_ref, g_ref, w_ref, q_ref, k_ref, v_ref, u_ref, w_bf, *, sb, scale):
    @pl.when(pl.program_id(0) == 0)
    def _():
        w_bf[...] = w_ref[...].astype(BF16)

    h = _rms(x_ref[...], g_ref[...])
    proj = jnp.dot(h.astype(BF16), w_bf[...], preferred_element_type=F32)
    k_ref[...] = proj[:, sb:2 * sb].astype(BF16)
    u_ref[...] = proj[:, 3 * sb:]
    lane = lax.broadcasted_iota(I32, (1, LANES), 1)
    q = (proj[:, :sb] * scale).astype(BF16)
    v = proj[:, 2 * sb:3 * sb].astype(BF16)
    zero = jnp.zeros((), BF16)
    for pair in range(sb // LANES):
        cols = slice(pair * LANES, (pair + 1) * LANES)
        for head in range(HEADS_PER_BLOCK):
            own = (lane // SB_HEAD_DIM) == head
            out = slice((pair * HEADS_PER_BLOCK + head) * LANES,
                        (pair * HEADS_PER_BLOCK + head + 1) * LANES)
            q_ref[:, out] = jnp.where(own, q[:, cols], zero)
            v_ref[:, out] = jnp.where(own, v[:, cols], zero)


def _in_proj(x2, ln1_g, w_in, sb, ssm_w):
    t, d = x2.shape
    n_in = w_in.shape[1]
    tm = min(TOKEN_TILE, t)
    scale = LOG2E / math.sqrt(SB_HEAD_DIM)
    return pl.pallas_call(
        functools.partial(_in_proj_kernel, sb=sb, scale=scale),
        grid=(t // tm,),
        in_specs=[
            pl.BlockSpec((tm, d), lambda i: (i, 0)),
            pl.BlockSpec((1, d), lambda i: (0, 0)),
            pl.BlockSpec((d, n_in), lambda i: (0, 0)),
        ],
        out_specs=[
            pl.BlockSpec((tm, HEADS_PER_BLOCK * sb), lambda i: (i, 0)),
            pl.BlockSpec((tm, sb), lambda i: (i, 0)),
            pl.BlockSpec((tm, HEADS_PER_BLOCK * sb), lambda i: (i, 0)),
            pl.BlockSpec((tm, ssm_w), lambda i: (i, 0)),
        ],
        out_shape=[
            jax.ShapeDtypeStruct((t, HEADS_PER_BLOCK * sb), BF16),
            jax.ShapeDtypeStruct((t, sb), BF16),
            jax.ShapeDtypeStruct((t, HEADS_PER_BLOCK * sb), BF16),
            jax.ShapeDtypeStruct((t, ssm_w), F32),
        ],
        scratch_shapes=[pltpu.VMEM((d, n_in), BF16)],
        compiler_params=pltpu.CompilerParams(
            dimension_semantics=("arbitrary",), vmem_limit_bytes=48 * MIB),
        name="in_proj",
    )(x2, ln1_g.reshape(1, d), w_in)


def _attn_kernel(ti_ref, tj_ref, tn_ref, q_ref, k_ref, v_ref, tri_ref, mask_ref, o_ref,
                 dbuf, spbuf, lbuf, rsbuf, acc_ref, r_ref, *, blk, n_items, streams):
    sign_bit = jnp.uint32(0x80000000)
    last = n_items - 1
    ALL, CAUSAL, NONE = 0, 1, 2

    for ref in (dbuf, spbuf, lbuf, rsbuf, acc_ref, r_ref):
        ref[...] = jnp.zeros_like(ref)

    def both_heads(ref, start):
        rows = pl.ds(pl.multiple_of(start, blk), blk)
        return jnp.concatenate([ref[rows, h * LANES:(h + 1) * LANES]
                                for h in range(HEADS_PER_BLOCK)], axis=0)

    def item(idx):
        c = jnp.clip(idx, 0, last)
        return ti_ref[c], tj_ref[c], tn_ref[c]

    n_streams = len(streams)
    stream_end = [end for _, end in streams]

    def step(state, x):
        cur, p1, p2 = list(state[:n_streams]), state[n_streams], state[n_streams + 1]
        out_blk = list(state[n_streams + 2:])
        slot, other = x % 2, 1 - x % 2
        x3 = (x - 2) % n_streams
        c1 = jnp.where(cur[x] < stream_end[x], cur[x], n_items)
        cur[x] = jnp.minimum(cur[x] + 1, stream_end[x])
        c3 = p2
        i1, j1, _ = item(c1)
        ks = k_ref[pl.ds(pl.multiple_of(j1 * blk, blk), blk), :]
        w = lax.dot_general(both_heads(q_ref, i1 * blk), ks, (((1,), (1,)), ((), ())),
                            preferred_element_type=F32)
        i3, j3, next_block = item(c3)
        valid = c3 <= last
        first = jnp.logical_and(i3 == j3, valid)
        mask3 = mask_ref[jnp.where(valid, (i3 == j3).astype(I32), NONE)]
        i3 = jnp.where(valid, i3, out_blk[x3])
        r_prev = jnp.where(first, 0.0, r_ref[x3])
        r_new = r_prev + rsbuf[other]
        r_ref[x3] = r_new
        done = jnp.logical_and(valid, jnp.min(r_new) >= ATTN_EXP_FLOOR)
        cur[x3] = jnp.where(done, jnp.maximum(cur[x3], next_block), cur[x3])
        out_blk[x3] = i3
        r_wide = jnp.concatenate([r_prev] * (blk // LANES), axis=1)
        a = jnp.exp2(dbuf[slot] - (lbuf[other] + r_wide))
        ab = a.astype(BF16) * jnp.concatenate([mask3, mask3], axis=0)
        a_cat = jnp.concatenate([ab[:blk], ab[blk:]], axis=1)
        acc = jnp.where(first, 0.0, acc_ref[x3]) + jnp.dot(
            a_cat, both_heads(v_ref, j3 * blk), preferred_element_type=F32)
        acc_ref[x3] = acc
        o_ref[pl.ds(pl.multiple_of(i3 * blk, blk), blk), :] = acc.astype(o_ref.dtype)
        sums = jnp.dot(spbuf[other], tri_ref[...], preferred_element_type=F32)
        lbuf[slot] = sums[:, :blk]
        rsbuf[slot] = sums[:, blk:]
        mask1 = mask_ref[(i1 == j1).astype(I32)]
        neg_abs = lax.bitcast_convert_type(
            lax.bitcast_convert_type(w, jnp.uint32) | sign_bit, F32)
        sp2 = jnp.maximum(w, 0.0) + jnp.log(1.0 + jnp.exp2(neg_abs)) * LOG2E
        dbuf[slot] = w - sp2
        spbuf[slot] = sp2.astype(BF16) * jnp.concatenate([mask1, mask1], axis=0)
        return (*cur, c1, p1, *out_blk)

    def in_flight(state):
        p1, p2 = state[n_streams], state[n_streams + 1]
        live = [state[x] < stream_end[x] for x in range(n_streams)]
        return functools.reduce(jnp.logical_or, live + [p1 <= last, p2 <= last])

    def one_round(state):
        for x in range(n_streams):
            state = step(state, x)
        return state

    none = jnp.int32(n_items)
    first_items = [jnp.int32(start) for start, _ in streams]
    first_blocks = [ti_ref[start] for start, _ in streams]
    lax.while_loop(in_flight, one_round, (*first_items, none, none, *first_blocks))


def _attention(q, k, v, bsz, seq):
    sb = k.shape[-1]
    blk = min(ATTN_BLOCK, seq)
    n_blk = seq // blk
    n_pairs = sb // LANES
    q3, k3, v3 = (a.reshape(bsz, seq, a.shape[-1]) for a in (q, k, v))
    per_head = HEADS_PER_BLOCK * LANES
    jj = lax.broadcasted_iota(I32, (blk, blk), 0)
    ss = lax.broadcasted_iota(I32, (blk, blk), 1)
    tri = jnp.concatenate([(jj > ss).astype(BF16), jnp.ones((blk, LANES), BF16)], axis=1)
    masks = jnp.stack([jnp.ones((blk, blk), BF16), (ss < jj).astype(BF16),
                       jnp.zeros((blk, blk), BF16)])
    n_streams = ATTN_STREAMS if n_blk % ATTN_STREAMS == 0 else 2
    assert n_blk % n_streams == 0, (n_blk, n_streams)
    items, next_block, streams = [], [], []
    for x in range(n_streams):
        start = len(items)
        for i in range(x, n_blk, n_streams):
            block_end = len(items) + i + 1
            items += [(i, j) for j in range(i, -1, -1)]
            next_block += [block_end] * (i + 1)
        streams.append((start, len(items)))
    item_i = jnp.asarray([i for i, _ in items], I32)
    item_j = jnp.asarray([j for _, j in items], I32)
    item_next = jnp.asarray(next_block, I32)
    whole = lambda b, p, ti, tj, tn: (b, 0, p)
    grid_spec = pltpu.PrefetchScalarGridSpec(
        num_scalar_prefetch=3,
        grid=(bsz, n_pairs),
        in_specs=[
            pl.BlockSpec((None, seq, per_head), whole),
            pl.BlockSpec((None, seq, LANES), whole),
            pl.BlockSpec((None, seq, per_head), whole),
            pl.BlockSpec((blk, blk + LANES), lambda b, p, ti, tj, tn: (0, 0)),
            pl.BlockSpec((3, blk, blk), lambda b, p, ti, tj, tn: (0, 0, 0)),
        ],
        out_specs=pl.BlockSpec((None, seq, LANES), whole),
        scratch_shapes=[
            pltpu.VMEM((2, HEADS_PER_BLOCK * blk, blk), F32),
            pltpu.VMEM((2, HEADS_PER_BLOCK * blk, blk), BF16),
            pltpu.VMEM((2, HEADS_PER_BLOCK * blk, blk), F32),
            pltpu.VMEM((2, HEADS_PER_BLOCK * blk, LANES), F32),
            pltpu.VMEM((n_streams, blk, LANES), F32),
            pltpu.VMEM((n_streams, HEADS_PER_BLOCK * blk, LANES), F32),
        ],
    )
    out = pl.pallas_call(
        functools.partial(_attn_kernel, blk=blk, n_items=len(items), streams=tuple(streams)),
        grid_spec=grid_spec,
        out_shape=jax.ShapeDtypeStruct((bsz, seq, sb), BF16),
        compiler_params=pltpu.CompilerParams(
            dimension_semantics=("arbitrary", "arbitrary"), vmem_limit_bytes=40 * MIB),
        name="sb_attention",
    )(item_i, item_j, item_next, q3, k3, v3, tri, masks)
    return out.reshape(bsz * seq, sb)


def _ssm_params(lam_re, lam_im, log_dt, b_re, b_im, c_re, c_im, chunk, n_chunks):
    lam = lax.complex(lam_re.astype(F32), lam_im.astype(F32))
    dt = jnp.exp(log_dt.astype(F32))[:, None]
    lam_dt = lam * dt
    lam_bar = jnp.exp(lam_dt)
    b_bar = ((lam_bar - 1.0) / lam)[:, :, None] * lax.complex(b_re.astype(F32), b_im.astype(F32))
    b_t = jnp.swapaxes(b_bar, 1, 2)
    c_mat = lax.complex(c_re.astype(F32), c_im.astype(F32))
    steps = jnp.arange(chunk + 1, dtype=F32)
    pw = jnp.exp(lam_dt[:, None, :] * steps[None, :, None])
    npw = jnp.exp(-lam_dt[:, None, :] * steps[None, :chunk, None])

    def halves(lo, hi):
        return jnp.concatenate([lo, hi], axis=-1)

    def b_side(z):
        return [halves(jnp.real(z), jnp.real(z)), halves(-jnp.imag(z), jnp.imag(z))]

    def c_side(z):
        return [halves(jnp.real(z), jnp.imag(z)), halves(jnp.imag(z), jnp.real(z))]

    time_terms = jnp.stack(
        b_side(npw)
        + b_side(pw[:, chunk - 1::-1])
        + c_side(pw[:, :chunk])
        + c_side(pw[:, 1:]), axis=1)
    chan_terms = jnp.stack(
        [halves(jnp.real(b_t), jnp.imag(b_t)), halves(jnp.imag(b_t), jnp.real(b_t)),
         halves(jnp.real(c_mat), -jnp.real(c_mat)), halves(-jnp.imag(c_mat), -jnp.imag(c_mat))],
        axis=1)
    n_steps = max(1, (n_chunks - 1).bit_length())
    powers = []
    cur = pw[:, chunk, :]
    for _ in range(n_steps):
        powers.append(cur)
        cur = cur * cur
    lam_pow = jnp.stack(powers, axis=1)
    a1 = halves(jnp.real(lam_pow), jnp.real(lam_pow))
    a2 = halves(-jnp.imag(lam_pow), jnp.imag(lam_pow))
    return time_terms, chan_terms, a1, a2


def _ssm_kernel(u_ref, time_ref, chan_ref, a1_ref, a2_ref, perm_ref, perm_t_ref, y_ref,
                u8_ref, y8_ref, toep_ref, *, n_chunks, n_steps, group):
    seq = u_ref.shape[0]
    tiles = seq // SSM_FOLD
    lc = toep_ref.shape[0]
    per_chunk = (lc // group) // SSM_FOLD
    p2 = a1_ref.shape[2]
    group_shift = group.bit_length() - 1
    nt = (((1,), (1,)), ((), ()))

    folded = jnp.concatenate(
        [u_ref[pl.ds(s, tiles, stride=SSM_FOLD), :] for s in range(SSM_FOLD)], axis=1)
    regrouped = jnp.dot(folded.astype(BF16), perm_ref[...], preferred_element_type=F32)
    n_groups = LANES // group
    for g in range(n_groups):
        u8_ref[g] = regrouped[:, g * LANES:(g + 1) * LANES]

    for g in range(n_groups):
        def table(t1, t2, v1, v2):
            full = (time_ref[g, t1][:, None, :] * chan_ref[g, v1][None, :, :]
                    + time_ref[g, t2][:, None, :] * chan_ref[g, v2][None, :, :])
            return full.reshape(lc, p2).astype(BF16)

        src = table(0, 1, 0, 1)
        to_state = table(2, 3, 0, 1)
        dst = table(4, 5, 2, 3)
        from_state = table(6, 7, 2, 3)
        cb = min(256, lc)
        for j in range(lc // cb):
            blk = lax.dot_general(src, dst[j * cb:(j + 1) * cb], nt, preferred_element_type=F32)
            s_idx = lax.broadcasted_iota(I32, (lc, cb), 0) >> group_shift
            t_idx = (lax.broadcasted_iota(I32, (lc, cb), 1) + j * cb) >> group_shift
            toep_ref[:, j * cb:(j + 1) * cb] = jnp.where(s_idx <= t_idx, blk, 0.0).astype(BF16)
        u = jnp.concatenate(
            [u8_ref[g, pl.ds(tau, n_chunks, stride=per_chunk), :] for tau in range(per_chunk)],
            axis=1).astype(BF16)
        y = jnp.dot(u, toep_ref[...], preferred_element_type=F32)
        z = jnp.dot(u, to_state, preferred_element_type=F32)
        n = lax.broadcasted_iota(I32, z.shape, 0)
        x = jnp.where(n >= 1, pltpu.roll(z, 1, 0), 0.0)
        for k in range(n_steps):
            sh = 1 << k
            xs = jnp.where(n >= sh, pltpu.roll(x, sh, 0), 0.0)
            x = (x + a1_ref[g, k:k + 1, :] * xs
                 + a2_ref[g, k:k + 1, :] * pltpu.roll(xs, p2 // 2, 1))
        y = y + lax.dot_general(x.astype(BF16), from_state, nt, preferred_element_type=F32)
        for tau in range(per_chunk):
            y8_ref[g, pl.ds(tau, n_chunks, stride=per_chunk), :] = y[:, tau * LANES:(tau + 1) * LANES]

    y8 = jnp.concatenate([y8_ref[g] for g in range(n_groups)], axis=1)
    unfolded = jnp.dot(y8.astype(BF16), perm_t_ref[...], preferred_element_type=F32)
    for s in range(SSM_FOLD):
        y_ref[pl.ds(s, tiles, stride=SSM_FOLD), :] = unfolded[:, s * LANES:(s + 1) * LANES]


def _ssm(u, bsz, seq, lam_re, lam_im, log_dt, b_re, b_im, c_re, c_im):
    g, p = lam_re.shape
    c = b_re.shape[-1]
    chunk = min(SSM_CHUNK, seq)
    n_chunks = seq // chunk
    assert n_chunks & (n_chunks - 1) == 0 and c & (c - 1) == 0, (n_chunks, c)
    assert SSM_FOLD * c == LANES and chunk % SSM_FOLD == 0
    lc = chunk * c
    per_block = LANES // c
    time_terms, chan_terms, a1, a2 = _ssm_params(
        lam_re, lam_im, log_dt, b_re, b_im, c_re, c_im, chunk, n_chunks)
    n_steps = a1.shape[1]
    src_idx = jnp.arange(SSM_FOLD * LANES, dtype=I32)
    s_of, g_of, c_of = src_idx // LANES, (src_idx % LANES) // c, src_idx % c
    dst_idx = g_of * LANES + s_of * c + c_of
    perm = (dst_idx[:, None] == src_idx[None, :]).astype(BF16)
    block = lambda cb, b: (cb, 0, 0)
    fixed = lambda cb, b: (0, 0)
    y = pl.pallas_call(
        functools.partial(_ssm_kernel, n_chunks=n_chunks, n_steps=n_steps, group=c),
        grid=(g // per_block, bsz),
        in_specs=[
            pl.BlockSpec((None, seq, LANES), lambda cb, b: (b, 0, cb)),
            pl.BlockSpec((per_block,) + time_terms.shape[1:], lambda cb, b: (cb, 0, 0, 0)),
            pl.BlockSpec((per_block,) + chan_terms.shape[1:], lambda cb, b: (cb, 0, 0, 0)),
            pl.BlockSpec((per_block, n_steps, 2 * p), block),
            pl.BlockSpec((per_block, n_steps, 2 * p), block),
            pl.BlockSpec((SSM_FOLD * LANES, SSM_FOLD * LANES), fixed),
            pl.BlockSpec((SSM_FOLD * LANES, SSM_FOLD * LANES), fixed),
        ],
        out_specs=pl.BlockSpec((None, seq, LANES), lambda cb, b: (b, 0, cb)),
        out_shape=jax.ShapeDtypeStruct((bsz, seq, g * c), F32),
        scratch_shapes=[
            pltpu.VMEM((per_block, seq // SSM_FOLD, LANES), F32),
            pltpu.VMEM((per_block, seq // SSM_FOLD, LANES), F32),
            pltpu.VMEM((lc, lc), BF16),
        ],
        compiler_params=pltpu.CompilerParams(
            dimension_semantics=("arbitrary", "arbitrary"), vmem_limit_bytes=56 * MIB),
        name="s5_chunked_scan",
    )(u.reshape(bsz, seq, g * c), time_terms, chan_terms, a1, a2, perm, perm.T)
    return y.reshape(bsz * seq, g * c)


def _post_kernel(x_ref, ysb_ref, yss_ref, u_ref, d_ref, wglu_ref, gsb_ref, gssm_ref, wout_ref,
                 ln2_ref, wrt_ref, br_ref, tri_ref, x1_ref, xn_ref, idx_ref, gate_ref, rank_ref,
                 count_ref, running_ref, wglu_bf, wout_bf, *, sb, n_exp):
    @pl.when(pl.program_id(0) == 0)
    def _():
        running_ref[...] = jnp.zeros_like(running_ref)
        wglu_bf[...] = wglu_ref[...].astype(BF16)
        wout_bf[...] = wout_ref[...].astype(BF16)

    u = u_ref[...].astype(F32)
    y = yss_ref[...].astype(F32) + d_ref[...] * u
    y = y * (0.5 * (1.0 + jnp.tanh(math.sqrt(2.0 / math.pi) * (y + 0.044715 * (y * y * y)))))
    ab = jnp.dot(y.astype(BF16), wglu_bf[...], preferred_element_type=F32)
    w = ab.shape[1] // 2
    y_ssm = ab[:, :w] * (1.0 / (1.0 + jnp.exp(-ab[:, w:])))
    m_sb = _rms(ysb_ref[...].astype(F32), gsb_ref[...])
    m_ssm = _rms(y_ssm, gssm_ref[...])
    x1 = (x_ref[...]
          + jnp.dot(m_sb.astype(BF16), wout_bf[:sb, :], preferred_element_type=F32)
          + jnp.dot(m_ssm.astype(BF16), wout_bf[sb:, :], preferred_element_type=F32))
    x1_ref[...] = x1
    xn = _rms(x1, ln2_ref[...])
    xn_ref[...] = xn
    def split(a):
        hi = a.astype(BF16)
        return hi, (a - hi.astype(F32)).astype(BF16)

    xn_hi, xn_lo = split(xn)
    wt_hi, wt_lo = split(wrt_ref[...])
    nt = (((1,), (1,)), ((), ()))
    both = lax.dot_general(jnp.concatenate([wt_hi, wt_lo], axis=0), xn_hi, nt,
                           preferred_element_type=F32)
    low = lax.dot_general(wt_hi, xn_lo, nt, preferred_element_type=F32)
    logits = (both[:n_exp] + both[n_exp:] + low) + br_ref[...]
    tokens = logits.shape[1]
    expert = lax.broadcasted_iota(I32, logits.shape, 0).astype(F32)
    out_row = lax.broadcasted_iota(I32, (SUBLANES, tokens), 0)
    idx_out = jnp.zeros((SUBLANES, tokens), F32)
    val_out = jnp.zeros((SUBLANES, tokens), F32)
    top = None
    denom = None
    work = logits
    chosen = []
    for k in range(TOP_K):
        m = jnp.max(work, axis=0, keepdims=True)
        sel = jnp.min(jnp.where(work == m, expert, float(n_exp)), axis=0, keepdims=True)
        hit = expert == sel
        chosen.append(hit)
        work = jnp.where(hit, -jnp.inf, work)
        if k == 0:
            top = m
        e = jnp.exp(m - top)
        denom = e if denom is None else denom + e
        idx_out = jnp.where(out_row == k, sel, idx_out)
        val_out = jnp.where(out_row == k, e, val_out)
    idx_ref[...] = idx_out.astype(I32)
    gates = jnp.concatenate([val_out / denom, jnp.zeros((LANES - SUBLANES, tokens), F32)], axis=0)
    gate_ref[...] = jnp.transpose(gates)[:, :TOP_K]
    member = functools.reduce(jnp.logical_or, chosen).astype(BF16)
    before = running_ref[...] + jnp.dot(member, tri_ref[...], preferred_element_type=F32)
    rank_out = jnp.zeros((SUBLANES, tokens), F32)
    for k in range(TOP_K):
        rk = jnp.sum(jnp.where(chosen[k], before, 0.0), axis=0, keepdims=True)
        rank_out = jnp.where(out_row == k, rk, rank_out)
    rank_ref[...] = rank_out.astype(I32)
    running = running_ref[...] + jnp.sum(member.astype(F32), axis=1, keepdims=True)
    running_ref[...] = running
    count_ref[...] = running


def _post(x2, y_sb, y_ss, u, ssm_d, w_glu, g_sb, g_ssm, w_out, ln2_g, w_router, b_router):
    t, d = x2.shape
    sb = y_sb.shape[1]
    w = y_ss.shape[1]
    n_exp = w_router.shape[1]
    tm = min(TOKEN_TILE, t)
    row = lambda i: (i, 0)
    fixed = lambda i: (0, 0)
    earlier = (lax.broadcasted_iota(I32, (tm, tm), 0)
               < lax.broadcasted_iota(I32, (tm, tm), 1)).astype(BF16)
    return pl.pallas_call(
        functools.partial(_post_kernel, sb=sb, n_exp=n_exp),
        grid=(t // tm,),
        in_specs=[
            pl.BlockSpec((tm, d), row),
            pl.BlockSpec((tm, sb), row),
            pl.BlockSpec((tm, w), row),
            pl.BlockSpec((tm, w), row),
            pl.BlockSpec((1, w), fixed),
            pl.BlockSpec((w, 2 * w), fixed),
            pl.BlockSpec((1, sb), fixed),
            pl.BlockSpec((1, w), fixed),
            pl.BlockSpec((sb + w, d), fixed),
            pl.BlockSpec((1, d), fixed),
            pl.BlockSpec((n_exp, d), fixed),
            pl.BlockSpec((n_exp, 1), fixed),
            pl.BlockSpec((tm, tm), fixed),
        ],
        out_specs=[
            pl.BlockSpec((tm, d), row),
            pl.BlockSpec((tm, d), row),
            pl.BlockSpec((SUBLANES, tm), lambda i: (0, i)),
            pl.BlockSpec((tm, TOP_K), row),
            pl.BlockSpec((SUBLANES, tm), lambda i: (0, i)),
            pl.BlockSpec((n_exp, 1), fixed),
        ],
        out_shape=[
            jax.ShapeDtypeStruct((t, d), F32),
            jax.ShapeDtypeStruct((t, d), F32),
            jax.ShapeDtypeStruct((SUBLANES, t), I32),
            jax.ShapeDtypeStruct((t, TOP_K), F32),
            jax.ShapeDtypeStruct((SUBLANES, t), I32),
            jax.ShapeDtypeStruct((n_exp, 1), F32),
        ],
        scratch_shapes=[pltpu.VMEM((n_exp, 1), F32), pltpu.VMEM((w, 2 * w), BF16),
                        pltpu.VMEM((sb + w, d), BF16)],
        compiler_params=pltpu.CompilerParams(
            dimension_semantics=("arbitrary",), vmem_limit_bytes=48 * MIB),
        name="post_mixer_router",
    )(x2, y_sb, y_ss, u, ssm_d.reshape(1, w), w_glu, g_sb.reshape(1, sb),
      g_ssm.reshape(1, w), w_out, ln2_g.reshape(1, d), w_router.T,
      b_router.reshape(n_exp, 1), earlier)


def _plan(top_idx, rank, counts, rows_per_block):
    k, t = top_idx.shape
    n_exp = counts.shape[0]
    counts = counts.reshape(n_exp).astype(I32)
    padded = ((counts + rows_per_block - 1) // rows_per_block) * rows_per_block
    pad_ends = jnp.cumsum(padded)
    pad_starts = pad_ends - padded
    start_of = jnp.sum(jnp.where(top_idx[None] == jnp.arange(n_exp, dtype=I32)[:, None, None],
                                 pad_starts[:, None, None], 0), axis=0)
    dest = (start_of + rank).astype(I32)
    n_rows = t * k + n_exp * rows_per_block
    n_blocks = n_rows // rows_per_block
    block_start = jnp.arange(n_blocks, dtype=I32) * rows_per_block
    block_expert = jnp.minimum(
        jnp.sum((pad_ends[None, :] <= block_start[:, None]).astype(I32), axis=1), n_exp - 1)
    n_used = (pad_ends[-1] // rows_per_block).astype(I32).reshape(1)
    fill_start = jnp.concatenate([jnp.maximum(pad_ends - rows_per_block, 0).astype(I32), n_used])
    experts = jnp.arange(n_exp, dtype=I32)
    nonempty = counts > 0
    order = jnp.cumsum(nonempty.astype(I32)) - nonempty.astype(I32)
    later_ne = jnp.where((experts[None, :] > experts[:, None]) & nonempty[None, :],
                         experts[None, :], n_exp)
    next_tbl = jnp.min(later_ne, axis=1)
    onehot = block_expert[:, None] == experts[None, :]
    w_slot = jnp.sum(jnp.where(onehot, order & 1, 0), axis=1).astype(I32)
    w_next = jnp.sum(jnp.where(onehot, next_tbl, 0), axis=1).astype(I32)
    return dest, (block_expert, n_used, w_slot, w_next), fill_start, n_blocks


def _by_tile(dest, tile):
    k, t = dest.shape
    return dest.reshape(k, t // tile, tile).transpose(1, 0, 2).reshape(k * t)


def _dispatch_kernel(fill_ref, dest_hbm, xn_ref, xs_hbm, idx_smem, zeros_ref, idx_sem, row_sem,
                     fill_sem, *, tile, n_exp, fill_rows):
    i = pl.program_id(0)
    n = pl.num_programs(0)
    per_tile = TOP_K * tile

    def idx_copy(b):
        src = dest_hbm.at[pl.ds(pl.multiple_of(b * per_tile, per_tile), per_tile)]
        dst = idx_smem.at[pl.ds(pl.multiple_of((b % 2) * per_tile, per_tile), per_tile)]
        return pltpu.make_async_copy(src, dst, idx_sem.at[b % 2])

    @pl.when(i == 0)
    def _():
        idx_copy(0).start()
        zeros_ref[...] = jnp.zeros_like(zeros_ref)
        for e in range(n_exp):
            start = pl.multiple_of(fill_ref[e], 8)
            pltpu.make_async_copy(zeros_ref, xs_hbm.at[pl.ds(start, fill_rows), 0, :],
                                  fill_sem).start()
        for e in range(n_exp):
            pltpu.make_async_copy(zeros_ref, xs_hbm.at[pl.ds(0, fill_rows), 0, :],
                                  fill_sem).wait()
        n_blocks = xs_hbm.shape[0] // fill_rows

        def fill_unused(b, carry):
            cp = pltpu.make_async_copy(
                zeros_ref, xs_hbm.at[pl.ds(pl.multiple_of(b * fill_rows, fill_rows), fill_rows), 0, :],
                fill_sem)
            cp.start()
            cp.wait()
            return carry

        lax.fori_loop(fill_ref[n_exp], n_blocks, fill_unused, 0)

    idx_copy(i).wait()

    @pl.when(i + 1 < n)
    def _():
        idx_copy(i + 1).start()

    base = pl.multiple_of((i % 2) * per_tile, per_tile)
    quarter = tile // TOP_K
    for part in range(TOP_K):
        def issue(g, carry, part=part):
            group = part * (quarter // SUBLANES) + g
            for s in range(SUBLANES):
                for k in range(TOP_K):
                    dst_row = idx_smem[base + k * tile + group * SUBLANES + s]
                    pltpu.make_async_copy(xn_ref.at[group, pl.ds(s, 1), :], xs_hbm.at[dst_row],
                                          row_sem.at[part]).start(priority=k % 2)
            return carry

        lax.fori_loop(0, quarter // SUBLANES, issue, 0)
    for part in range(TOP_K):
        pltpu.make_async_copy(xn_ref, xn_ref, row_sem.at[part]).wait()


def _dispatch(xn, dest, fill_start, n_rows):
    t, d = xn.shape
    tile = min(DISPATCH_TILE, t)
    n_exp = fill_start.shape[0] - 1
    grid_spec = pltpu.PrefetchScalarGridSpec(
        num_scalar_prefetch=1,
        grid=(t // tile,),
        in_specs=[
            pl.BlockSpec(memory_space=pl.ANY),
            pl.BlockSpec((tile // SUBLANES, SUBLANES, d), lambda i, fs: (i, 0, 0)),
        ],
        out_specs=pl.BlockSpec(memory_space=pl.ANY),
        scratch_shapes=[
            pltpu.SMEM((2 * TOP_K * tile,), I32),
            pltpu.VMEM((EXPERT_ROWS, d), F32),
            pltpu.SemaphoreType.DMA((2,)),
            pltpu.SemaphoreType.DMA((TOP_K,)),
            pltpu.SemaphoreType.DMA,
        ],
    )
    return pl.pallas_call(
        functools.partial(_dispatch_kernel, tile=tile, n_exp=n_exp, fill_rows=EXPERT_ROWS),
        grid_spec=grid_spec,
        out_shape=jax.ShapeDtypeStruct((n_rows, 1, d), F32),
        compiler_params=pltpu.CompilerParams(
            dimension_semantics=("arbitrary",), vmem_limit_bytes=40 * MIB),
        name="moe_dispatch",
    )(fill_start, _by_tile(dest, tile), xn.reshape(t // SUBLANES, SUBLANES, d))


def _expert_kernel(be_ref, nused_ref, wslot_ref, wnext_ref, xs_hbm, wg_hbm, bg_ref, wu_hbm, bu_ref,
                   wd_hbm, bd_ref, y_hbm, xbuf, ybuf, x_sem, y_sem, wg32, wu32, wd32, w_sem,
                   wg_bf, wu_bf, wd_bf, *, rows, n_exp):
    i = pl.program_id(0)
    n = nused_ref[0]

    def w_copies(e, slot):
        return [pltpu.make_async_copy(src.at[e], dst.at[slot], w_sem.at[slot])
                for src, dst in ((wg_hbm, wg32), (wu_hbm, wu32), (wd_hbm, wd32))]

    def x_copy(b):
        return pltpu.make_async_copy(
            xs_hbm.at[pl.ds(pl.multiple_of(b * rows, rows), rows), 0, :], xbuf.at[b % 2],
            x_sem.at[b % 2])

    def y_copy(b):
        return pltpu.make_async_copy(
            ybuf.at[b % 2], y_hbm.at[pl.ds(pl.multiple_of(b * rows, rows), rows), 0, :],
            y_sem.at[b % 2])

    @pl.when(i < n)
    def _():
        @pl.when(i == 0)
        def _():
            x_copy(0).start()
            for cp in w_copies(be_ref[0], wslot_ref[0]):
                cp.start()

        @pl.when(i + 1 < n)
        def _():
            x_copy(i + 1).start()

        changed = jnp.logical_or(i == 0, be_ref[i] != be_ref[jnp.maximum(i - 1, 0)])

        @pl.when(changed)
        def _():
            slot = wslot_ref[i]
            for cp in w_copies(be_ref[i], slot):
                cp.wait()

            @pl.when(wnext_ref[i] < n_exp)
            def _():
                for cp in w_copies(wnext_ref[i], 1 - slot):
                    cp.start(priority=1)
            wg_bf[...] = wg32[slot].astype(BF16)
            wu_bf[...] = wu32[slot].astype(BF16)
            wd_bf[...] = wd32[slot].astype(BF16)

        @pl.when(i >= 2)
        def _():
            y_copy(i - 2).wait()

        x_copy(i).wait()
        x = xbuf[i % 2].astype(BF16)
        gate = jnp.dot(x, wg_bf[...], preferred_element_type=F32) + bg_ref[...]
        up = jnp.dot(x, wu_bf[...], preferred_element_type=F32) + bu_ref[...]
        gate = jnp.minimum(gate, SWIGLU_LIMIT)
        up = jnp.clip(up, -SWIGLU_LIMIT, SWIGLU_LIMIT)
        glu = gate * (1.0 / (1.0 + jnp.exp(-SWIGLU_ALPHA * gate)))
        hidden = ((up + 1.0) * glu).astype(BF16)
        ybuf[i % 2] = jnp.dot(hidden, wd_bf[...], preferred_element_type=F32) + bd_ref[...]
        y_copy(i).start()

        @pl.when(i == n - 1)
        def _():
            y_copy(i).wait()

            @pl.when(i >= 1)
            def _():
                y_copy(i - 1).wait()

    @pl.when(i >= n)
    def _():
        ybuf[i % 2] = jnp.zeros((rows, ybuf.shape[2]), F32)
        y_copy(i).start()
        y_copy(i).wait()


def _experts(x_sorted, block_plan, n_blocks, w_gate, b_gate, w_up, b_up, w_down, b_down):
    n_rows, _, d = x_sorted.shape
    n_exp, _, f = w_gate.shape
    rows = EXPERT_ROWS
    bmap = lambda i, be, nu, ws, wn: (be[i], 0, 0)
    hbm = pl.BlockSpec(memory_space=pl.ANY)
    grid_spec = pltpu.PrefetchScalarGridSpec(
        num_scalar_prefetch=4,
        grid=(n_blocks,),
        in_specs=[
            hbm,
            hbm, pl.BlockSpec((None, 1, f), bmap),
            hbm, pl.BlockSpec((None, 1, f), bmap),
            hbm, pl.BlockSpec((None, 1, d), bmap),
        ],
        out_specs=hbm,
        scratch_shapes=[
            pltpu.VMEM((2, rows, d), F32),
            pltpu.VMEM((2, rows, d), F32),
            pltpu.SemaphoreType.DMA((2,)),
            pltpu.SemaphoreType.DMA((2,)),
            pltpu.VMEM((2, d, f), F32),
            pltpu.VMEM((2, d, f), F32),
            pltpu.VMEM((2, f, d), F32),
            pltpu.SemaphoreType.DMA((2,)),
            pltpu.VMEM((d, f), BF16),
            pltpu.VMEM((d, f), BF16),
            pltpu.VMEM((f, d), BF16),
        ],
    )
    return pl.pallas_call(
        functools.partial(_expert_kernel, rows=rows, n_exp=n_exp),
        grid_spec=grid_spec,
        out_shape=jax.ShapeDtypeStruct((n_rows, 1, d), F32),
        compiler_params=pltpu.CompilerParams(
            dimension_semantics=("arbitrary",), vmem_limit_bytes=52 * MIB),
        name="moe_experts",
    )(*block_plan, x_sorted,
      w_gate, b_gate.reshape(n_exp, 1, f), w_up, b_up.reshape(n_exp, 1, f),
      w_down, b_down.reshape(n_exp, 1, d))


def _combine_kernel(dest_hbm, y_hbm, x1_ref, gate_ref, lnf_ref, o_ref, idx_smem, ybuf,
                    idx_sem, row_sem, *, tile):
    i = pl.program_id(0)
    n = pl.num_programs(0)
    per_tile = TOP_K * tile

    def slot_base(b):
        return pl.multiple_of((b % 3) * per_tile, per_tile)

    def idx_copy(b):
        src = dest_hbm.at[pl.ds(pl.multiple_of(b * per_tile, per_tile), per_tile)]
        return pltpu.make_async_copy(src, idx_smem.at[pl.ds(slot_base(b), per_tile)],
                                     idx_sem.at[b % 3])

    def gather(b):
        base = slot_base(b)
        buf = ybuf.at[b % 2]
        sem = row_sem.at[b % 2]

        def issue(g, carry):
            for s in range(SUBLANES):
                pltpu.make_async_copy(y_hbm.at[idx_smem[base + g * SUBLANES + s]],
                                      buf.at[g, pl.ds(s, 1), :], sem).start(priority=s % 2)
            return carry

        lax.fori_loop(0, per_tile // SUBLANES, issue, 0)

    @pl.when(i == 0)
    def _():
        idx_copy(0).start()
        idx_copy(0).wait()
        gather(0)

        @pl.when(n > 1)
        def _():
            idx_copy(1).start()

    @pl.when(i + 1 < n)
    def _():
        idx_copy(i + 1).wait()

        @pl.when(i + 2 < n)
        def _():
            idx_copy(i + 2).start()
        gather(i + 1)

    cur = i % 2
    pltpu.make_async_copy(ybuf.at[1 - cur], ybuf.at[cur], row_sem.at[cur]).wait()
    gates = gate_ref[...]
    x = x1_ref[...]
    groups = tile // SUBLANES
    for k in range(TOP_K):
        yk = ybuf[cur, k * groups:(k + 1) * groups].reshape(tile, x.shape[1])
        x = x + gates[:, k:k + 1] * yk
    o_ref[...] = _rms(x, lnf_ref[...])


def _combine(dest, y_rows, x1, gates, ln_f_g):
    t, d = x1.shape
    tile = min(COMBINE_TILE, t)
    n_tiles = t // tile
    dest_tiles = _by_tile(dest, tile)
    return pl.pallas_call(
        functools.partial(_combine_kernel, tile=tile),
        grid=(n_tiles,),
        in_specs=[
            pl.BlockSpec(memory_space=pl.ANY),
            pl.BlockSpec(memory_space=pl.ANY),
            pl.BlockSpec((tile, d), lambda i: (i, 0)),
            pl.BlockSpec((tile, TOP_K), lambda i: (i, 0)),
            pl.BlockSpec((1, d), lambda i: (0, 0)),
        ],
        out_specs=pl.BlockSpec((tile, d), lambda i: (i, 0)),
        out_shape=jax.ShapeDtypeStruct((t, d), F32),
        scratch_shapes=[
            pltpu.SMEM((3 * TOP_K * tile,), I32),
            pltpu.VMEM((2, TOP_K * tile // SUBLANES, SUBLANES, d), F32),
            pltpu.SemaphoreType.DMA((3,)),
            pltpu.SemaphoreType.DMA((2,)),
        ],
        compiler_params=pltpu.CompilerParams(
            dimension_semantics=("arbitrary",), vmem_limit_bytes=40 * MIB),
        name="moe_combine",
    )(dest_tiles, y_rows, x1, gates, ln_f_g.reshape(1, d))


def kernel(x, ln1_g, w_in, lam_re, lam_im, log_dt, ssm_b_re, ssm_b_im, ssm_c_re, ssm_c_im,
           ssm_d, w_glu, g_sb, g_ssm, w_out, ln2_g, w_router, b_router, w_gate, b_gate,
           w_up, b_up, w_down, b_down, ln_f_g):
    bsz, seq, d = x.shape
    assert ln1_g.shape[0] == 1, "depth-1 block only"
    ssm_w = ssm_d.shape[1]
    sb = g_sb.shape[1]
    x2 = x.reshape(bsz * seq, d)
    q, k, v, u = _in_proj(x2, ln1_g[0], w_in[0], sb, ssm_w)
    y_sb = _attention(q, k, v, bsz, seq)
    y_ss = _ssm(u, bsz, seq, lam_re[0], lam_im[0], log_dt[0], ssm_b_re[0], ssm_b_im[0],
                ssm_c_re[0], ssm_c_im[0])
    x1, xn, top_idx, gates, rank, counts = _post(
        x2, y_sb, y_ss, u, ssm_d[0], w_glu[0], g_sb[0], g_ssm[0], w_out[0], ln2_g[0],
        w_router[0], b_router[0])
    dest, block_plan, fill_start, n_blocks = _plan(top_idx[:TOP_K], rank[:TOP_K], counts,
                                                   EXPERT_ROWS)
    x_sorted = _dispatch(xn, dest, fill_start, n_blocks * EXPERT_ROWS)
    y_rows = _experts(x_sorted, block_plan, n_blocks, w_gate[0], b_gate[0],
                      w_up[0], b_up[0], w_down[0], b_down[0])
    out = _combine(dest, y_rows, x1, gates, ln_f_g)
    return out.reshape(bsz, seq, d)
```

```python
import functools
import math

import jax
import jax.numpy as jnp
from jax import lax
from jax.experimental import pallas as pl
from jax.experimental.pallas import tpu as pltpu

F32 = jnp.float32
BF16 = jnp.bfloat16
I32 = jnp.int32

EPS = 1e-5
SB_HEAD_DIM = 64
SSM_GROUP = 16
SSM_STATE = 64
TOP_K = 4
SWIGLU_LIMIT = 7.0
SWIGLU_ALPHA = 1.702

LANES = 128
SUBLANES = 8
HEADS_PER_BLOCK = LANES // SB_HEAD_DIM
ATTN_BLOCK = 256
ATTN_EXP_FLOOR = 160.0
ATTN_STREAMS = 4
SSM_CHUNK = 16
SSM_FOLD = 8
TOKEN_TILE = 512
EXPERT_ROWS = 256
COMBINE_TILE = 256
DISPATCH_TILE = 1024
MIB = 1024 * 1024
LOG2E = 1.4426950408889634


def _rms(x, g):
    return x * lax.rsqrt(jnp.mean(x * x, axis=-1, keepdims=True) + EPS) * g


def _in_proj_kernel(x_ref, g_ref, w_ref, q_ref, k_ref, v_ref, u_ref, w_bf, *, sb, scale):
    @pl.when(pl.program_id(0) == 0)
    def _():
        w_bf[...] = w_ref[...].astype(BF16)

    h = _rms(x_ref[...], g_ref[...])
    proj = jnp.dot(h.astype(BF16), w_bf[...], preferred_element_type=F32)
    k_ref[...] = proj[:, sb:2 * sb].astype(BF16)
    u_ref[...] = proj[:, 3 * sb:]
    lane = lax.broadcasted_iota(I32, (1, LANES), 1)
    q = (proj[:, :sb] * scale).astype(BF16)
    v = proj[:, 2 * sb:3 * sb].astype(BF16)
    zero = jnp.zeros((), BF16)
    for pair in range(sb // LANES):
        cols = slice(pair * LANES, (pair + 1) * LANES)
        for head in range(HEADS_PER_BLOCK):
            own = (lane // SB_HEAD_DIM) == head
            out = slice((pair * HEADS_PER_BLOCK + head) * LANES,
                        (pair * HEADS_PER_BLOCK + head + 1) * LANES)
            q_ref[:, out] = jnp.where(own, q[:, cols], zero)
            v_ref[:, out] = jnp.where(own, v[:, cols], zero)


def _in_proj(x2, ln1_g, w_in, sb, ssm_w):
    t, d = x2.shape
    n_in = w_in.shape[1]
    tm = min(TOKEN_TILE, t)
    scale = LOG2E / math.sqrt(SB_HEAD_DIM)
    return pl.pallas_call(
        functools.partial(_in_proj_kernel, sb=sb, scale=scale),
        grid=(t // tm,),
        in_specs=[
            pl.BlockSpec((tm, d), lambda i: (i, 0)),
            pl.BlockSpec((1, d), lambda i: (0, 0)),
            pl.BlockSpec((d, n_in), lambda i: (0, 0)),
        ],
        out_specs=[
            pl.BlockSpec((tm, HEADS_PER_BLOCK * sb), lambda i: (i, 0)),
            pl.BlockSpec((tm, sb), lambda i: (i, 0)),
            pl.BlockSpec((tm, HEADS_PER_BLOCK * sb), lambda i: (i, 0)),
            pl.BlockSpec((tm, ssm_w), lambda i: (i, 0)),
        ],
        out_shape=[
            jax.ShapeDtypeStruct((t, HEADS_PER_BLOCK * sb), BF16),
            jax.ShapeDtypeStruct((t, sb), BF16),
            jax.ShapeDtypeStruct((t, HEADS_PER_BLOCK * sb), BF16),
            jax.ShapeDtypeStruct((t, ssm_w), F32),
        ],
        scratch_shapes=[pltpu.VMEM((d, n_in), BF16)],
        compiler_params=pltpu.CompilerParams(
            dimension_semantics=("arbitrary",), vmem_limit_bytes=48 * MIB),
        name="in_proj",
    )(x2, ln1_g.reshape(1, d), w_in)


def _attn_kernel(ti_ref, tj_ref, tn_ref, q_ref, k_ref, v_ref, tri_ref, mask_ref, o_ref,
                 dbuf, spbuf, lbuf, rsbuf, acc_ref, r_ref, *, blk, n_items, streams):
    sign_bit = jnp.uint32(0x80000000)
    last = n_items - 1
    ALL, CAUSAL, NONE = 0, 1, 2

    for ref in (dbuf, spbuf, lbuf, rsbuf, acc_ref, r_ref):
        ref[...] = jnp.zeros_like(ref)

    def both_heads(ref, start):
        rows = pl.ds(pl.multiple_of(start, blk), blk)
        return jnp.concatenate([ref[rows, h * LANES:(h + 1) * LANES]
                                for h in range(HEADS_PER_BLOCK)], axis=0)

    def item(idx):
        c = jnp.clip(idx, 0, last)
        return ti_ref[c], tj_ref[c], tn_ref[c]

    n_streams = len(streams)
    stream_end = [end for _, end in streams]

    def step(state, x):
        cur, p1, p2 = list(state[:n_streams]), state[n_streams], state[n_streams + 1]
        out_blk = list(state[n_streams + 2:])
        slot, other = x % 2, 1 - x % 2
        x3 = (x - 2) % n_streams
        c1 = jnp.where(cur[x] < stream_end[x], cur[x], n_items)
        cur[x] = jnp.minimum(cur[x] + 1, stream_end[x])
        c3 = p2
        i1, j1, _ = item(c1)
        ks = k_ref[pl.ds(pl.multiple_of(j1 * blk, blk), blk), :]
        w = lax.dot_general(both_heads(q_ref, i1 * blk), ks, (((1,), (1,)), ((), ())),
                            preferred_element_type=F32)
        i3, j3, next_block = item(c3)
        valid = c3 <= last
        first = jnp.logical_and(i3 == j3, valid)
        mask3 = mask_ref[jnp.where(valid, (i3 == j3).astype(I32), NONE)]
        i3 = jnp.where(valid, i3, out_blk[x3])
        r_prev = jnp.where(first, 0.0, r_ref[x3])
        r_new = r_prev + rsbuf[other]
        r_ref[x3] = r_new
        done = jnp.logical_and(valid, jnp.min(r_new) >= ATTN_EXP_FLOOR)
        cur[x3] = jnp.where(done, jnp.maximum(cur[x3], next_block), cur[x3])
        out_blk[x3] = i3
        r_wide = jnp.concatenate([r_prev] * (blk // LANES), axis=1)
        a = jnp.exp2(dbuf[slot] - (lbuf[other] + r_wide))
        ab = a.astype(BF16) * jnp.concatenate([mask3, mask3], axis=0)
        a_cat = jnp.concatenate([ab[:blk], ab[blk:]], axis=1)
        acc = jnp.where(first, 0.0, acc_ref[x3]) + jnp.dot(
            a_cat, both_heads(v_ref, j3 * blk), preferred_element_type=F32)
        acc_ref[x3] = acc
        o_ref[pl.ds(pl.multiple_of(i3 * blk, blk), blk), :] = acc.astype(o_ref.dtype)
        sums = jnp.dot(spbuf[other], tri_ref[...], preferred_element_type=F32)
        lbuf[slot] = sums[:, :blk]
        rsbuf[slot] = sums[:, blk:]
        mask1 = mask_ref[(i1 == j1).astype(I32)]
        neg_abs = lax.bitcast_convert_type(
            lax.bitcast_convert_type(w, jnp.uint32) | sign_bit, F32)
        sp2 = jnp.maximum(w, 0.0) + jnp.log(1.0 + jnp.exp2(neg_abs)) * LOG2E
        dbuf[slot] = w - sp2
        spbuf[slot] = sp2.astype(BF16) * jnp.concatenate([mask1, mask1], axis=0)
        return (*cur, c1, p1, *out_blk)

    def in_flight(state):
        p1, p2 = state[n_streams], state[n_streams + 1]
        live = [state[x] < stream_end[x] for x in range(n_streams)]
        return functools.reduce(jnp.logical_or, live + [p1 <= last, p2 <= last])

    def one_round(state):
        for x in range(n_streams):
            state = step(state, x)
        return state

    none = jnp.int32(n_items)
    first_items = [jnp.int32(start) for start, _ in streams]
    first_blocks = [ti_ref[start] for start, _ in streams]
    lax.while_loop(in_flight, one_round, (*first_items, none, none, *first_blocks))


def _attention(q, k, v, bsz, seq):
    sb = k.shape[-1]
    blk = min(ATTN_BLOCK, seq)
    n_blk = seq // blk
    n_pairs = sb // LANES
    q3, k3, v3 = (a.reshape(bsz, seq, a.shape[-1]) for a in (q, k, v))
    per_head = HEADS_PER_BLOCK * LANES
    jj = lax.broadcasted_iota(I32, (blk, blk), 0)
    ss = lax.broadcasted_iota(I32, (blk, blk), 1)
    tri = jnp.concatenate([(jj > ss).astype(BF16), jnp.ones((blk, LANES), BF16)], axis=1)
    masks = jnp.stack([jnp.ones((blk, blk), BF16), (ss < jj).astype(BF16),
                       jnp.zeros((blk, blk), BF16)])
    n_streams = ATTN_STREAMS if n_blk % ATTN_STREAMS == 0 else 2
    assert n_blk % n_streams == 0, (n_blk, n_streams)
    items, next_block, streams = [], [], []
    for x in range(n_streams):
        start = len(items)
        for i in range(x, n_blk, n_streams):
            block_end = len(items) + i + 1
            items += [(i, j) for j in range(i, -1, -1)]
            next_block += [block_end] * (i + 1)
        streams.append((start, len(items)))
    item_i = jnp.asarray([i for i, _ in items], I32)
    item_j = jnp.asarray([j for _, j in items], I32)
    item_next = jnp.asarray(next_block, I32)
    whole = lambda b, p, ti, tj, tn: (b, 0, p)
    grid_spec = pltpu.PrefetchScalarGridSpec(
        num_scalar_prefetch=3,
        grid=(bsz, n_pairs),
        in_specs=[
            pl.BlockSpec((None, seq, per_head), whole),
            pl.BlockSpec((None, seq, LANES), whole),
            pl.BlockSpec((None, seq, per_head), whole),
            pl.BlockSpec((blk, blk + LANES), lambda b, p, ti, tj, tn: (0, 0)),
            pl.BlockSpec((3, blk, blk), lambda b, p, ti, tj, tn: (0, 0, 0)),
        ],
        out_specs=pl.BlockSpec((None, seq, LANES), whole),
        scratch_shapes=[
            pltpu.VMEM((2, HEADS_PER_BLOCK * blk, blk), F32),
            pltpu.VMEM((2, HEADS_PER_BLOCK * blk, blk), BF16),
            pltpu.VMEM((2, HEADS_PER_BLOCK * blk, blk), F32),
            pltpu.VMEM((2, HEADS_PER_BLOCK * blk, LANES), F32),
            pltpu.VMEM((n_streams, blk, LANES), F32),
            pltpu.VMEM((n_streams, HEADS_PER_BLOCK * blk, LANES), F32),
        ],
    )
    out = pl.pallas_call(
        functools.partial(_attn_kernel, blk=blk, n_items=len(items), streams=tuple(streams)),
        grid_spec=grid_spec,
        out_shape=jax.ShapeDtypeStruct((bsz, seq, sb), BF16),
        compiler_params=pltpu.CompilerParams(
            dimension_semantics=("arbitrary", "arbitrary"), vmem_limit_bytes=40 * MIB),
        name="sb_attention",
    )(item_i, item_j, item_next, q3, k3, v3, tri, masks)
    return out.reshape(bsz * seq, sb)


def _ssm_params(lam_re, lam_im, log_dt, b_re, b_im, c_re, c_im, chunk, n_chunks):
    lam = lax.complex(lam_re.astype(F32), lam_im.astype(F32))
    dt = jnp.exp(log_dt.astype(F32))[:, None]
    lam_dt = lam * dt
    lam_bar = jnp.exp(lam_dt)
    b_bar = ((lam_bar - 1.0) / lam)[:, :, None] * lax.complex(b_re.astype(F32), b_im.astype(F32))
    b_t = jnp.swapaxes(b_bar, 1, 2)
    c_mat = lax.complex(c_re.astype(F32), c_im.astype(F32))
    steps = jnp.arange(chunk + 1, dtype=F32)
    pw = jnp.exp(lam_dt[:, None, :] * steps[None, :, None])
    npw = jnp.exp(-lam_dt[:, None, :] * steps[None, :chunk, None])

    def halves(lo, hi):
        return jnp.concatenate([lo, hi], axis=-1)

    def b_side(z):
        return [halves(jnp.real(z), jnp.real(z)), halves(-jnp.imag(z), jnp.imag(z))]

    def c_side(z):
        return [halves(jnp.real(z), jnp.imag(z)), halves(jnp.imag(z), jnp.real(z))]

    time_terms = jnp.stack(
        b_side(npw)
        + b_side(pw[:, chunk - 1::-1])
        + c_side(pw[:, :chunk])
        + c_side(pw[:, 1:]), axis=1)
    chan_terms = jnp.stack(
        [halves(jnp.real(b_t), jnp.imag(b_t)), halves(jnp.imag(b_t), jnp.real(b_t)),
         halves(jnp.real(c_mat), -jnp.real(c_mat)), halves(-jnp.imag(c_mat), -jnp.imag(c_mat))],
        axis=1)
    n_steps = max(1, (n_chunks - 1).bit_length())
    powers = []
    cur = pw[:, chunk, :]
    for _ in range(n_steps):
        powers.append(cur)
        cur = cur * cur
    lam_pow = jnp.stack(powers, axis=1)
    a1 = halves(jnp.real(lam_pow), jnp.real(lam_pow))
    a2 = halves(-jnp.imag(lam_pow), jnp.imag(lam_pow))
    return time_terms, chan_terms, a1, a2


def _ssm_kernel(u_ref, time_ref, chan_ref, a1_ref, a2_ref, perm_ref, perm_t_ref, y_ref,
                u8_ref, y8_ref, toep_ref, *, n_chunks, n_steps, group):
    seq = u_ref.shape[0]
    tiles = seq // SSM_FOLD
    lc = toep_ref.shape[0]
    per_chunk = (lc // group) // SSM_FOLD
    p2 = a1_ref.shape[2]
    group_shift = group.bit_length() - 1
    nt = (((1,), (1,)), ((), ()))

    folded = jnp.concatenate(
        [u_ref[pl.ds(s, tiles, stride=SSM_FOLD), :] for s in range(SSM_FOLD)], axis=1)
    regrouped = jnp.dot(folded.astype(BF16), perm_ref[...], preferred_element_type=F32)
    n_groups = LANES // group
    for g in range(n_groups):
        u8_ref[g] = regrouped[:, g * LANES:(g + 1) * LANES]

    for g in range(n_groups):
        def table(t1, t2, v1, v2):
            full = (time_ref[g, t1][:, None, :] * chan_ref[g, v1][None, :, :]
                    + time_ref[g, t2][:, None, :] * chan_ref[g, v2][None, :, :])
            return full.reshape(lc, p2).astype(BF16)

        src = table(0, 1, 0, 1)
        to_state = table(2, 3, 0, 1)
        dst = table(4, 5, 2, 3)
        from_state = table(6, 7, 2, 3)
        cb = min(256, lc)
        for j in range(lc // cb):
            blk = lax.dot_general(src, dst[j * cb:(j + 1) * cb], nt, preferred_element_type=F32)
            s_idx = lax.broadcasted_iota(I32, (lc, cb), 0) >> group_shift
            t_idx = (lax.broadcasted_iota(I32, (lc, cb), 1) + j * cb) >> group_shift
            toep_ref[:, j * cb:(j + 1) * cb] = jnp.where(s_idx <= t_idx, blk, 0.0).astype(BF16)
        u = jnp.concatenate(
            [u8_ref[g, pl.ds(tau, n_chunks, stride=per_chunk), :] for tau in range(per_chunk)],
            axis=1).astype(BF16)
        y = jnp.dot(u, toep_ref[...], preferred_element_type=F32)
        z = jnp.dot(u, to_state, preferred_element_type=F32)
        n = lax.broadcasted_iota(I32, z.shape, 0)
        x = jnp.where(n >= 1, pltpu.roll(z, 1, 0), 0.0)
        for k in range(n_steps):
            sh = 1 << k
            xs = jnp.where(n >= sh, pltpu.roll(x, sh, 0), 0.0)
            x = (x + a1_ref[g, k:k + 1, :] * xs
                 + a2_ref[g, k:k + 1, :] * pltpu.roll(xs, p2 // 2, 1))
        y = y + lax.dot_general(x.astype(BF16), from_state, nt, preferred_element_type=F32)
        for tau in range(per_chunk):
            y8_ref[g, pl.ds(tau, n_chunks, stride=per_chunk), :] = y[:, tau * LANES:(tau + 1) * LANES]

    y8 = jnp.concatenate([y8_ref[g] for g in range(n_groups)], axis=1)
    unfolded = jnp.dot(y8.astype(BF16), perm_t_ref[...], preferred_element_type=F32)
    for s in range(SSM_FOLD):
        y_ref[pl.ds(s, tiles, stride=SSM_FOLD), :] = unfolded[:, s * LANES:(s + 1) * LANES]


def _ssm(u, bsz, seq, lam_re, lam_im, log_dt, b_re, b_im, c_re, c_im):
    g, p = lam_re.shape
    c = b_re.shape[-1]
    chunk = min(SSM_CHUNK, seq)
    n_chunks = seq // chunk
    assert n_chunks & (n_chunks - 1) == 0 and c & (c - 1) == 0, (n_chunks, c)
    assert SSM_FOLD * c == LANES and chunk % SSM_FOLD == 0
    lc = chunk * c
    per_block = LANES // c
    time_terms, chan_terms, a1, a2 = _ssm_params(
        lam_re, lam_im, log_dt, b_re, b_im, c_re, c_im, chunk, n_chunks)
    n_steps = a1.shape[1]
    src_idx = jnp.arange(SSM_FOLD * LANES, dtype=I32)
    s_of, g_of, c_of = src_idx // LANES, (src_idx % LANES) // c, src_idx % c
    dst_idx = g_of * LANES + s_of * c + c_of
    perm = (dst_idx[:, None] == src_idx[None, :]).astype(BF16)
    block = lambda cb, b: (cb, 0, 0)
    fixed = lambda cb, b: (0, 0)
    y = pl.pallas_call(
        functools.partial(_ssm_kernel, n_chunks=n_chunks, n_steps=n_steps, group=c),
        grid=(g // per_block, bsz),
        in_specs=[
            pl.BlockSpec((None, seq, LANES), lambda cb, b: (b, 0, cb)),
            pl.BlockSpec((per_block,) + time_terms.shape[1:], lambda cb, b: (cb, 0, 0, 0)),
            pl.BlockSpec((per_block,) + chan_terms.shape[1:], lambda cb, b: (cb, 0, 0, 0)),
            pl.BlockSpec((per_block, n_steps, 2 * p), block),
            pl.BlockSpec((per_block, n_steps, 2 * p), block),
            pl.BlockSpec((SSM_FOLD * LANES, SSM_FOLD * LANES), fixed),
            pl.BlockSpec((SSM_FOLD * LANES, SSM_FOLD * LANES), fixed),
        ],
        out_specs=pl.BlockSpec((None, seq, LANES), lambda cb, b: (b, 0, cb)),
        out_shape=jax.ShapeDtypeStruct((bsz, seq, g * c), F32),
        scratch_shapes=[
            pltpu.VMEM((per_block, seq // SSM_FOLD, LANES), F32),
            pltpu.VMEM((per_block, seq // SSM_FOLD, LANES), F32),
            pltpu.VMEM((lc, lc), BF16),
        ],
        compiler_params=pltpu.CompilerParams(
            dimension_semantics=("arbitrary", "arbitrary"), vmem_limit_bytes=56 * MIB),
        name="s5_chunked_scan",
    )(u.reshape(bsz, seq, g * c), time_terms, chan_terms, a1, a2, perm, perm.T)
    return y.reshape(bsz * seq, g * c)


def _post_kernel(x_ref, ysb_ref, yss_ref, u_ref, d_ref, wglu_ref, gsb_ref, gssm_ref, wout_ref,
                 ln2_ref, wrt_ref, br_ref, tri_ref, x1_ref, xn_ref, idx_ref, gate_ref, rank_ref,
                 count_ref, running_ref, wglu_bf, wout_bf, *, sb, n_exp):
    @pl.when(pl.program_id(0) == 0)
    def _():
        running_ref[...] = jnp.zeros_like(running_ref)
        wglu_bf[...] = wglu_ref[...].astype(BF16)
        wout_bf[...] = wout_ref[...].astype(BF16)

    u = u_ref[...].astype(F32)
    y = yss_ref[...].astype(F32) + d_ref[...] * u
    y = y * (0.5 * (1.0 + jnp.tanh(math.sqrt(2.0 / math.pi) * (y + 0.044715 * (y * y * y)))))
    ab = jnp.dot(y.astype(BF16), wglu_bf[...], preferred_element_type=F32)
    w = ab.shape[1] // 2
    y_ssm = ab[:, :w] * (1.0 / (1.0 + jnp.exp(-ab[:, w:])))
    m_sb = _rms(ysb_ref[...].astype(F32), gsb_ref[...])
    m_ssm = _rms(y_ssm, gssm_ref[...])
    x1 = (x_ref[...]
          + jnp.dot(m_sb.astype(BF16), wout_bf[:sb, :], preferred_element_type=F32)
          + jnp.dot(m_ssm.astype(BF16), wout_bf[sb:, :], preferred_element_type=F32))
    x1_ref[...] = x1
    xn = _rms(x1, ln2_ref[...])
    xn_ref[...] = xn
    def split(a):
        hi = a.astype(BF16)
        return hi, (a - hi.astype(F32)).astype(BF16)

    xn_hi, xn_lo = split(xn)
    wt_hi, wt_lo = split(wrt_ref[...])
    nt = (((1,), (1,)), ((), ()))
    both = lax.dot_general(jnp.concatenate([wt_hi, wt_lo], axis=0), xn_hi, nt,
                           preferred_element_type=F32)
    low = lax.dot_general(wt_hi, xn_lo, nt, preferred_element_type=F32)
    logits = (both[:n_exp] + both[n_exp:] + low) + br_ref[...]
    tokens = logits.shape[1]
    expert = lax.broadcasted_iota(I32, logits.shape, 0).astype(F32)
    out_row = lax.broadcasted_iota(I32, (SUBLANES, tokens), 0)
    idx_out = jnp.zeros((SUBLANES, tokens), F32)
    val_out = jnp.zeros((SUBLANES, tokens), F32)
    top = None
    denom = None
    work = logits
    chosen = []
    for k in range(TOP_K):
        m = jnp.max(work, axis=0, keepdims=True)
        sel = jnp.min(jnp.where(work == m, expert, float(n_exp)), axis=0, keepdims=True)
        hit = expert == sel
        chosen.append(hit)
        work = jnp.where(hit, -jnp.inf, work)
        if k == 0:
            top = m
        e = jnp.exp(m - top)
        denom = e if denom is None else denom + e
        idx_out = jnp.where(out_row == k, sel, idx_out)
        val_out = jnp.where(out_row == k, e, val_out)
    idx_ref[...] = idx_out.astype(I32)
    gates = jnp.concatenate([val_out / denom, jnp.zeros((LANES - SUBLANES, tokens), F32)], axis=0)
    gate_ref[...] = jnp.transpose(gates)[:, :TOP_K]
    member = functools.reduce(jnp.logical_or, chosen).astype(BF16)
    before = running_ref[...] + jnp.dot(member, tri_ref[...], preferred_element_type=F32)
    rank_out = jnp.zeros((SUBLANES, tokens), F32)
    for k in range(TOP_K):
        rk = jnp.sum(jnp.where(chosen[k], before, 0.0), axis=0, keepdims=True)
        rank_out = jnp.where(out_row == k, rk, rank_out)
    rank_ref[...] = rank_out.astype(I32)
    running = running_ref[...] + jnp.sum(member.astype(F32), axis=1, keepdims=True)
    running_ref[...] = running
    count_ref[...] = running


def _post(x2, y_sb, y_ss, u, ssm_d, w_glu, g_sb, g_ssm, w_out, ln2_g, w_router, b_router):
    t, d = x2.shape
    sb = y_sb.shape[1]
    w = y_ss.shape[1]
    n_exp = w_router.shape[1]
    tm = min(TOKEN_TILE, t)
    row = lambda i: (i, 0)
    fixed = lambda i: (0, 0)
    earlier = (lax.broadcasted_iota(I32, (tm, tm), 0)
               < lax.broadcasted_iota(I32, (tm, tm), 1)).astype(BF16)
    return pl.pallas_call(
        functools.partial(_post_kernel, sb=sb, n_exp=n_exp),
        grid=(t // tm,),
        in_specs=[
            pl.BlockSpec((tm, d), row),
            pl.BlockSpec((tm, sb), row),
            pl.BlockSpec((tm, w), row),
            pl.BlockSpec((tm, w), row),
            pl.BlockSpec((1, w), fixed),
            pl.BlockSpec((w, 2 * w), fixed),
            pl.BlockSpec((1, sb), fixed),
            pl.BlockSpec((1, w), fixed),
            pl.BlockSpec((sb + w, d), fixed),
            pl.BlockSpec((1, d), fixed),
            pl.BlockSpec((n_exp, d), fixed),
            pl.BlockSpec((n_exp, 1), fixed),
            pl.BlockSpec((tm, tm), fixed),
        ],
        out_specs=[
            pl.BlockSpec((tm, d), row),
            pl.BlockSpec((tm, d), row),
            pl.BlockSpec((SUBLANES, tm), lambda i: (0, i)),
            pl.BlockSpec((tm, TOP_K), row),
            pl.BlockSpec((SUBLANES, tm), lambda i: (0, i)),
            pl.BlockSpec((n_exp, 1), fixed),
        ],
        out_shape=[
            jax.ShapeDtypeStruct((t, d), F32),
            jax.ShapeDtypeStruct((t, d), F32),
            jax.ShapeDtypeStruct((SUBLANES, t), I32),
            jax.ShapeDtypeStruct((t, TOP_K), F32),
            jax.ShapeDtypeStruct((SUBLANES, t), I32),
            jax.ShapeDtypeStruct((n_exp, 1), F32),
        ],
        scratch_shapes=[pltpu.VMEM((n_exp, 1), F32), pltpu.VMEM((w, 2 * w), BF16),
                        pltpu.VMEM((sb + w, d), BF16)],
        compiler_params=pltpu.CompilerParams(
            dimension_semantics=("arbitrary",), vmem_limit_bytes=48 * MIB),
        name="post_mixer_router",
    )(x2, y_sb, y_ss, u, ssm_d.reshape(1, w), w_glu, g_sb.reshape(1, sb),
      g_ssm.reshape(1, w), w_out, ln2_g.reshape(1, d), w_router.T,
      b_router.reshape(n_exp, 1), earlier)


def _plan(top_idx, rank, counts, rows_per_block):
    k, t = top_idx.shape
    n_exp = counts.shape[0]
    counts = counts.reshape(n_exp).astype(I32)
    padded = ((counts + rows_per_block - 1) // rows_per_block) * rows_per_block
    pad_ends = jnp.cumsum(padded)
    pad_starts = pad_ends - padded
    start_of = jnp.sum(jnp.where(top_idx[None] == jnp.arange(n_exp, dtype=I32)[:, None, None],
                                 pad_starts[:, None, None], 0), axis=0)
    dest = (start_of + rank).astype(I32)
    n_rows = t * k + n_exp * rows_per_block
    n_blocks = n_rows // rows_per_block
    block_start = jnp.arange(n_blocks, dtype=I32) * rows_per_block
    block_expert = jnp.minimum(
        jnp.sum((pad_ends[None, :] <= block_start[:, None]).astype(I32), axis=1), n_exp - 1)
    n_used = (pad_ends[-1] // rows_per_block).astype(I32).reshape(1)
    fill_start = jnp.concatenate([jnp.maximum(pad_ends - rows_per_block, 0).astype(I32), n_used])
    experts = jnp.arange(n_exp, dtype=I32)
    nonempty = counts > 0
    order = jnp.cumsum(nonempty.astype(I32)) - nonempty.astype(I32)
    later_ne = jnp.where((experts[None, :] > experts[:, None]) & nonempty[None, :],
                         experts[None, :], n_exp)
    next_tbl = jnp.min(later_ne, axis=1)
    onehot = block_expert[:, None] == experts[None, :]
    w_slot = jnp.sum(jnp.where(onehot, order & 1, 0), axis=1).astype(I32)
    w_next = jnp.sum(jnp.where(onehot, next_tbl, 0), axis=1).astype(I32)
    return dest, (block_expert, n_used, w_slot, w_next), fill_start, n_blocks


def _by_tile(dest, tile):
    k, t = dest.shape
    return dest.reshape(k, t // tile, tile).transpose(1, 0, 2).reshape(k * t)


def _dispatch_kernel(fill_ref, dest_hbm, xn_ref, xs_hbm, idx_smem, zeros_ref, idx_sem, row_sem,
                     fill_sem, *, tile, n_exp, fill_rows):
    i = pl.program_id(0)
    n = pl.num_programs(0)
    per_tile = TOP_K * tile

    def idx_copy(b):
        src = dest_hbm.at[pl.ds(pl.multiple_of(b * per_tile, per_tile), per_tile)]
        dst = idx_smem.at[pl.ds(pl.multiple_of((b % 2) * per_tile, per_tile), per_tile)]
        return pltpu.make_async_copy(src, dst, idx_sem.at[b % 2])

    @pl.when(i == 0)
    def _():
        idx_copy(0).start()
        zeros_ref[...] = jnp.zeros_like(zeros_ref)
        def fill(start):
            return pltpu.make_async_copy(
                zeros_ref, xs_hbm.at[pl.ds(pl.multiple_of(start, SUBLANES), fill_rows), 0, :],
                fill_sem)

        def start_unused(b, carry):
            fill(b * fill_rows).start()
            return carry

        def wait_unused(b, carry):
            fill(b * fill_rows).wait()
            return carry

        n_blocks = xs_hbm.shape[0] // fill_rows
        for e in range(n_exp):
            fill(fill_ref[e]).start()
        lax.fori_loop(fill_ref[n_exp], n_blocks, start_unused, 0)
        for e in range(n_exp):
            fill(fill_ref[e]).wait()
        lax.fori_loop(fill_ref[n_exp], n_blocks, wait_unused, 0)

    idx_copy(i).wait()

    @pl.when(i + 1 < n)
    def _():
        idx_copy(i + 1).start()

    base = pl.multiple_of((i % 2) * per_tile, per_tile)
    quarter = tile // TOP_K
    for part in range(TOP_K):
        def issue(g, carry, part=part):
            group = part * (quarter // SUBLANES) + g
            for s in range(SUBLANES):
                for k in range(TOP_K):
                    dst_row = idx_smem[base + k * tile + group * SUBLANES + s]
                    pltpu.make_async_copy(xn_ref.at[group, pl.ds(s, 1), :], xs_hbm.at[dst_row],
                                          row_sem.at[part]).start(priority=k % 2)
            return carry

        lax.fori_loop(0, quarter // SUBLANES, issue, 0)
    for part in range(TOP_K):
        pltpu.make_async_copy(xn_ref, xn_ref, row_sem.at[part]).wait()


def _dispatch(xn, dest, fill_start, n_rows):
    t, d = xn.shape
    tile = min(DISPATCH_TILE, t)
    n_exp = fill_start.shape[0] - 1
    grid_spec = pltpu.PrefetchScalarGridSpec(
        num_scalar_prefetch=1,
        grid=(t // tile,),
        in_specs=[
            pl.BlockSpec(memory_space=pl.ANY),
            pl.BlockSpec((tile // SUBLANES, SUBLANES, d), lambda i, fs: (i, 0, 0)),
        ],
        out_specs=pl.BlockSpec(memory_space=pl.ANY),
        scratch_shapes=[
            pltpu.SMEM((2 * TOP_K * tile,), I32),
            pltpu.VMEM((EXPERT_ROWS, d), F32),
            pltpu.SemaphoreType.DMA((2,)),
            pltpu.SemaphoreType.DMA((TOP_K,)),
            pltpu.SemaphoreType.DMA,
        ],
    )
    return pl.pallas_call(
        functools.partial(_dispatch_kernel, tile=tile, n_exp=n_exp, fill_rows=EXPERT_ROWS),
        grid_spec=grid_spec,
        out_shape=jax.ShapeDtypeStruct((n_rows, 1, d), F32),
        compiler_params=pltpu.CompilerParams(
            dimension_semantics=("arbitrary",), vmem_limit_bytes=40 * MIB),
        name="moe_dispatch",
    )(fill_start, _by_tile(dest, tile), xn.reshape(t // SUBLANES, SUBLANES, d))


def _expert_kernel(be_ref, nused_ref, wslot_ref, wnext_ref, xs_hbm, wg_hbm, bg_ref, wu_hbm, bu_ref,
                   wd_hbm, bd_ref, y_hbm, xbuf, ybuf, x_sem, y_sem, wg32, wu32, wd32, w_sem,
                   wg_bf, wu_bf, wd_bf, *, rows, n_exp):
    i = pl.program_id(0)
    n = nused_ref[0]

    def w_copies(e, slot):
        return [pltpu.make_async_copy(src.at[e], dst.at[slot], w_sem.at[slot])
                for src, dst in ((wg_hbm, wg32), (wu_hbm, wu32), (wd_hbm, wd32))]

    def x_copy(b):
        return pltpu.make_async_copy(
            xs_hbm.at[pl.ds(pl.multiple_of(b * rows, rows), rows), 0, :], xbuf.at[b % 2],
            x_sem.at[b % 2])

    def y_copy(b):
        return pltpu.make_async_copy(
            ybuf.at[b % 2], y_hbm.at[pl.ds(pl.multiple_of(b * rows, rows), rows), 0, :],
            y_sem.at[b % 2])

    @pl.when(i < n)
    def _():
        @pl.when(i == 0)
        def _():
            x_copy(0).start()
            for cp in w_copies(be_ref[0], wslot_ref[0]):
                cp.start()

        @pl.when(i + 1 < n)
        def _():
            x_copy(i + 1).start()

        changed = jnp.logical_or(i == 0, be_ref[i] != be_ref[jnp.maximum(i - 1, 0)])

        @pl.when(changed)
        def _():
            slot = wslot_ref[i]
            for cp in w_copies(be_ref[i], slot):
                cp.wait()

            @pl.when(wnext_ref[i] < n_exp)
            def _():
                for cp in w_copies(wnext_ref[i], 1 - slot):
                    cp.start(priority=1)
            wg_bf[...] = wg32[slot].astype(BF16)
            wu_bf[...] = wu32[slot].astype(BF16)
            wd_bf[...] = wd32[slot].astype(BF16)

        @pl.when(i >= 2)
        def _():
            y_copy(i - 2).wait()

        x_copy(i).wait()
        x = xbuf[i % 2].astype(BF16)
        gate = jnp.dot(x, wg_bf[...], preferred_element_type=F32) + bg_ref[...]
        up = jnp.dot(x, wu_bf[...], preferred_element_type=F32) + bu_ref[...]
        gate = jnp.minimum(gate, SWIGLU_LIMIT)
        up = jnp.clip(up, -SWIGLU_LIMIT, SWIGLU_LIMIT)
        glu = gate * (1.0 / (1.0 + jnp.exp(-SWIGLU_ALPHA * gate)))
        hidden = ((up + 1.0) * glu).astype(BF16)
        ybuf[i % 2] = jnp.dot(hidden, wd_bf[...], preferred_element_type=F32) + bd_ref[...]
        y_copy(i).start()

        @pl.when(i == n - 1)
        def _():
            y_copy(i).wait()

            @pl.when(i >= 1)
            def _():
                y_copy(i - 1).wait()

            def zero_copy(b):
                return pltpu.make_async_copy(
                    ybuf.at[0], y_hbm.at[pl.ds(pl.multiple_of(b * rows, rows), rows), 0, :],
                    y_sem.at[0])

            def start_zero(b, carry):
                zero_copy(b).start()
                return carry

            def wait_zero(b, carry):
                zero_copy(b).wait()
                return carry

            ybuf[0] = jnp.zeros((rows, ybuf.shape[2]), F32)
            lax.fori_loop(n, pl.num_programs(0), start_zero, 0)
            lax.fori_loop(n, pl.num_programs(0), wait_zero, 0)


def _experts(x_sorted, block_plan, n_blocks, w_gate, b_gate, w_up, b_up, w_down, b_down):
    n_rows, _, d = x_sorted.shape
    n_exp, _, f = w_gate.shape
    rows = EXPERT_ROWS
    bmap = lambda i, be, nu, ws, wn: (be[i], 0, 0)
    hbm = pl.BlockSpec(memory_space=pl.ANY)
    grid_spec = pltpu.PrefetchScalarGridSpec(
        num_scalar_prefetch=4,
        grid=(n_blocks,),
        in_specs=[
            hbm,
            hbm, pl.BlockSpec((None, 1, f), bmap),
            hbm, pl.BlockSpec((None, 1, f), bmap),
            hbm, pl.BlockSpec((None, 1, d), bmap),
        ],
        out_specs=hbm,
        scratch_shapes=[
            pltpu.VMEM((2, rows, d), F32),
            pltpu.VMEM((2, rows, d), F32),
            pltpu.SemaphoreType.DMA((2,)),
            pltpu.SemaphoreType.DMA((2,)),
            pltpu.VMEM((2, d, f), F32),
            pltpu.VMEM((2, d, f), F32),
            pltpu.VMEM((2, f, d), F32),
            pltpu.SemaphoreType.DMA((2,)),
            pltpu.VMEM((d, f), BF16),
            pltpu.VMEM((d, f), BF16),
            pltpu.VMEM((f, d), BF16),
        ],
    )
    return pl.pallas_call(
        functools.partial(_expert_kernel, rows=rows, n_exp=n_exp),
        grid_spec=grid_spec,
        out_shape=jax.ShapeDtypeStruct((n_rows, 1, d), F32),
        compiler_params=pltpu.CompilerParams(
            dimension_semantics=("arbitrary",), vmem_limit_bytes=52 * MIB),
        name="moe_experts",
    )(*block_plan, x_sorted,
      w_gate, b_gate.reshape(n_exp, 1, f), w_up, b_up.reshape(n_exp, 1, f),
      w_down, b_down.reshape(n_exp, 1, d))


def _combine_kernel(dest_hbm, y_hbm, x1_ref, gate_ref, lnf_ref, o_ref, idx_smem, ybuf,
                    idx_sem, row_sem, *, tile):
    i = pl.program_id(0)
    n = pl.num_programs(0)
    per_tile = TOP_K * tile

    def slot_base(b):
        return pl.multiple_of((b % 3) * per_tile, per_tile)

    def idx_copy(b):
        src = dest_hbm.at[pl.ds(pl.multiple_of(b * per_tile, per_tile), per_tile)]
        return pltpu.make_async_copy(src, idx_smem.at[pl.ds(slot_base(b), per_tile)],
                                     idx_sem.at[b % 3])

    def gather(b):
        base = slot_base(b)
        buf = ybuf.at[b % 2]
        sem = row_sem.at[b % 2]

        def issue(g, carry):
            for s in range(SUBLANES):
                pltpu.make_async_copy(y_hbm.at[idx_smem[base + g * SUBLANES + s]],
                                      buf.at[g, pl.ds(s, 1), :], sem).start(priority=s % 2)
            return carry

        lax.fori_loop(0, per_tile // SUBLANES, issue, 0)

    @pl.when(i == 0)
    def _():
        idx_copy(0).start()
        idx_copy(0).wait()
        gather(0)

        @pl.when(n > 1)
        def _():
            idx_copy(1).start()

    @pl.when(i + 1 < n)
    def _():
        idx_copy(i + 1).wait()

        @pl.when(i + 2 < n)
        def _():
            idx_copy(i + 2).start()
        gather(i + 1)

    cur = i % 2
    pltpu.make_async_copy(ybuf.at[1 - cur], ybuf.at[cur], row_sem.at[cur]).wait()
    gates = gate_ref[...]
    x = x1_ref[...]
    groups = tile // SUBLANES
    for k in range(TOP_K):
        yk = ybuf[cur, k * groups:(k + 1) * groups].reshape(tile, x.shape[1])
        x = x + gates[:, k:k + 1] * yk
    o_ref[...] = _rms(x, lnf_ref[...])


def _combine(dest, y_rows, x1, gates, ln_f_g):
    t, d = x1.shape
    tile = min(COMBINE_TILE, t)
    n_tiles = t // tile
    dest_tiles = _by_tile(dest, tile)
    return pl.pallas_call(
        functools.partial(_combine_kernel, tile=tile),
        grid=(n_tiles,),
        in_specs=[
            pl.BlockSpec(memory_space=pl.ANY),
            pl.BlockSpec(memory_space=pl.ANY),
            pl.BlockSpec((tile, d), lambda i: (i, 0)),
            pl.BlockSpec((tile, TOP_K), lambda i: (i, 0)),
            pl.BlockSpec((1, d), lambda i: (0, 0)),
        ],
        out_specs=pl.BlockSpec((tile, d), lambda i: (i, 0)),
        out_shape=jax.ShapeDtypeStruct((t, d), F32),
        scratch_shapes=[
            pltpu.SMEM((3 * TOP_K * tile,), I32),
            pltpu.VMEM((2, TOP_K * tile // SUBLANES, SUBLANES, d), F32),
            pltpu.SemaphoreType.DMA((3,)),
            pltpu.SemaphoreType.DMA((2,)),
        ],
        compiler_params=pltpu.CompilerParams(
            dimension_semantics=("arbitrary",), vmem_limit_bytes=40 * MIB),
        name="moe_combine",
    )(dest_tiles, y_rows, x1, gates, ln_f_g.reshape(1, d))


def kernel(x, ln1_g, w_in, lam_re, lam_im, log_dt, ssm_b_re, ssm_b_im, ssm_c_re, ssm_c_im,
           ssm_d, w_glu, g_sb, g_ssm, w_out, ln2_g, w_router, b_router, w_gate, b_gate,
           w_up, b_up, w_down, b_down, ln_f_g):
    bsz, seq, d = x.shape
    assert ln1_g.shape[0] == 1, "depth-1 block only"
    ssm_w = ssm_d.shape[1]
    sb = g_sb.shape[1]
    x2 = x.reshape(bsz * seq, d)
    q, k, v, u = _in_proj(x2, ln1_g[0], w_in[0], sb, ssm_w)
    y_sb = _attention(q, k, v, bsz, seq)
    y_ss = _ssm(u, bsz, seq, lam_re[0], lam_im[0], log_dt[0], ssm_b_re[0], ssm_b_im[0],
                ssm_c_re[0], ssm_c_im[0])
    x1, xn, top_idx, gates, rank, counts = _post(
        x2, y_sb, y_ss, u, ssm_d[0], w_glu[0], g_sb[0], g_ssm[0], w_out[0], ln2_g[0],
        w_router[0], b_router[0])
    dest, block_plan, fill_start, n_blocks = _plan(top_idx[:TOP_K], rank[:TOP_K], counts,
                                                   EXPERT_ROWS)
    x_sorted = _dispatch(xn, dest, fill_start, n_blocks * EXPERT_ROWS)
    y_rows = _experts(x_sorted, block_plan, n_blocks, w_gate[0], b_gate[0],
                      w_up[0], b_up[0], w_down[0], b_down[0])
    out = _combine(dest, y_rows, x1, gates, ln_f_g)
    return out.reshape(bsz, seq, d)
```

```python
import functools
import math

import jax
import jax.numpy as jnp
from jax import lax
from jax.experimental import pallas as pl
from jax.experimental.pallas import tpu as pltpu

F32 = jnp.float32
BF16 = jnp.bfloat16
I32 = jnp.int32

EPS = 1e-5
SB_HEAD_DIM = 64
SSM_GROUP = 16
SSM_STATE = 64
TOP_K = 4
SWIGLU_LIMIT = 7.0
SWIGLU_ALPHA = 1.702

LANES = 128
SUBLANES = 8
HEADS_PER_BLOCK = LANES // SB_HEAD_DIM
ATTN_BLOCK = 256
ATTN_EXP_FLOOR = 160.0
ATTN_STREAMS = 4
SSM_CHUNK = 16
SSM_FOLD = 8
TOKEN_TILE = 512
EXPERT_ROWS = 256
COMBINE_TILE = 256
DISPATCH_TILE = 2048
MIB = 1024 * 1024
LOG2E = 1.4426950408889634


def _rms(x, g):
    return x * lax.rsqrt(jnp.mean(x * x, axis=-1, keepdims=True) + EPS) * g


def _in_proj_kernel(x_ref, g_ref, w_ref, q_ref, k_ref, v_ref, u_ref, w_bf, *, sb, scale):
    @pl.when(pl.program_id(0) == 0)
    def _():
        w_bf[...] = w_ref[...].astype(BF16)

    h = _rms(x_ref[...], g_ref[...])
    proj = jnp.dot(h.astype(BF16), w_bf[...], preferred_element_type=F32)
    k_ref[...] = proj[:, sb:2 * sb].astype(BF16)
    u_ref[...] = proj[:, 3 * sb:]
    lane = lax.broadcasted_iota(I32, (1, LANES), 1)
    q = (proj[:, :sb] * scale).astype(BF16)
    v = proj[:, 2 * sb:3 * sb].astype(BF16)
    zero = jnp.zeros((), BF16)
    for pair in range(sb // LANES):
        cols = slice(pair * LANES, (pair + 1) * LANES)
        for head in range(HEADS_PER_BLOCK):
            own = (lane // SB_HEAD_DIM) == head
            out = slice((pair * HEADS_PER_BLOCK + head) * LANES,
                        (pair * HEADS_PER_BLOCK + head + 1) * LANES)
            q_ref[:, out] = jnp.where(own, q[:, cols], zero)
            v_ref[:, out] = jnp.where(own, v[:, cols], zero)


def _in_proj(x2, ln1_g, w_in, sb, ssm_w):
    t, d = x2.shape
    n_in = w_in.shape[1]
    tm = min(TOKEN_TILE, t)
    scale = LOG2E / math.sqrt(SB_HEAD_DIM)
    return pl.pallas_call(
        functools.partial(_in_proj_kernel, sb=sb, scale=scale),
        grid=(t // tm,),
        in_specs=[
            pl.BlockSpec((tm, d), lambda i: (i, 0)),
            pl.BlockSpec((1, d), lambda i: (0, 0)),
            pl.BlockSpec((d, n_in), lambda i: (0, 0)),
        ],
        out_specs=[
            pl.BlockSpec((tm, HEADS_PER_BLOCK * sb), lambda i: (i, 0)),
            pl.BlockSpec((tm, sb), lambda i: (i, 0)),
            pl.BlockSpec((tm, HEADS_PER_BLOCK * sb), lambda i: (i, 0)),
            pl.BlockSpec((tm, ssm_w), lambda i: (i, 0)),
        ],
        out_shape=[
            jax.ShapeDtypeStruct((t, HEADS_PER_BLOCK * sb), BF16),
            jax.ShapeDtypeStruct((t, sb), BF16),
            jax.ShapeDtypeStruct((t, HEADS_PER_BLOCK * sb), BF16),
            jax.ShapeDtypeStruct((t, ssm_w), F32),
        ],
        scratch_shapes=[pltpu.VMEM((d, n_in), BF16)],
        compiler_params=pltpu.CompilerParams(
            dimension_semantics=("arbitrary",), vmem_limit_bytes=48 * MIB),
        name="in_proj",
    )(x2, ln1_g.reshape(1, d), w_in)


def _attn_kernel(ti_ref, tj_ref, tn_ref, q_ref, k_ref, v_ref, tri_ref, mask_ref, o_ref,
                 dbuf, spbuf, lbuf, rsbuf, acc_ref, r_ref, *, blk, n_items, streams):
    sign_bit = jnp.uint32(0x80000000)
    last = n_items - 1
    ALL, CAUSAL, NONE = 0, 1, 2

    for ref in (dbuf, spbuf, lbuf, rsbuf, acc_ref, r_ref):
        ref[...] = jnp.zeros_like(ref)

    def both_heads(ref, start):
        rows = pl.ds(pl.multiple_of(start, blk), blk)
        return jnp.concatenate([ref[rows, h * LANES:(h + 1) * LANES]
                                for h in range(HEADS_PER_BLOCK)], axis=0)

    def item(idx):
        c = jnp.clip(idx, 0, last)
        return ti_ref[c], tj_ref[c], tn_ref[c]

    n_streams = len(streams)
    stream_end = [end for _, end in streams]

    def step(state, x):
        cur, p1, p2 = list(state[:n_streams]), state[n_streams], state[n_streams + 1]
        out_blk = list(state[n_streams + 2:])
        slot, other = x % 2, 1 - x % 2
        x3 = (x - 2) % n_streams
        c1 = jnp.where(cur[x] < stream_end[x], cur[x], n_items)
        cur[x] = jnp.minimum(cur[x] + 1, stream_end[x])
        c3 = p2
        i1, j1, _ = item(c1)
        ks = k_ref[pl.ds(pl.multiple_of(j1 * blk, blk), blk), :]
        w = lax.dot_general(both_heads(q_ref, i1 * blk), ks, (((1,), (1,)), ((), ())),
                            preferred_element_type=F32)
        i3, j3, next_block = item(c3)
        valid = c3 <= last
        first = jnp.logical_and(i3 == j3, valid)
        mask3 = mask_ref[jnp.where(valid, (i3 == j3).astype(I32), NONE)]
        i3 = jnp.where(valid, i3, out_blk[x3])
        r_prev = jnp.where(first, 0.0, r_ref[x3])
        r_new = r_prev + rsbuf[other]
        r_ref[x3] = r_new
        done = jnp.logical_and(valid, jnp.min(r_new) >= ATTN_EXP_FLOOR)
        cur[x3] = jnp.where(done, jnp.maximum(cur[x3], next_block), cur[x3])
        out_blk[x3] = i3
        r_wide = jnp.concatenate([r_prev] * (blk // LANES), axis=1)
        a = jnp.exp2(dbuf[slot] - (lbuf[other] + r_wide))
        ab = a.astype(BF16) * jnp.concatenate([mask3, mask3], axis=0)
        a_cat = jnp.concatenate([ab[:blk], ab[blk:]], axis=1)
        acc = jnp.where(first, 0.0, acc_ref[x3]) + jnp.dot(
            a_cat, both_heads(v_ref, j3 * blk), preferred_element_type=F32)
        acc_ref[x3] = acc
        o_ref[pl.ds(pl.multiple_of(i3 * blk, blk), blk), :] = acc.astype(o_ref.dtype)
        sums = jnp.dot(spbuf[other], tri_ref[...], preferred_element_type=F32)
        lbuf[slot] = sums[:, :blk]
        rsbuf[slot] = sums[:, blk:]
        mask1 = mask_ref[(i1 == j1).astype(I32)]
        neg_abs = lax.bitcast_convert_type(
            lax.bitcast_convert_type(w, jnp.uint32) | sign_bit, F32)
        sp2 = jnp.maximum(w, 0.0) + jnp.log(1.0 + jnp.exp2(neg_abs)) * LOG2E
        dbuf[slot] = w - sp2
        spbuf[slot] = sp2.astype(BF16) * jnp.concatenate([mask1, mask1], axis=0)
        return (*cur, c1, p1, *out_blk)

    def in_flight(state):
        p1, p2 = state[n_streams], state[n_streams + 1]
        live = [state[x] < stream_end[x] for x in range(n_streams)]
        return functools.reduce(jnp.logical_or, live + [p1 <= last, p2 <= last])

    def one_round(state):
        for x in range(n_streams):
            state = step(state, x)
        return state

    none = jnp.int32(n_items)
    first_items = [jnp.int32(start) for start, _ in streams]
    first_blocks = [ti_ref[start] for start, _ in streams]
    lax.while_loop(in_flight, one_round, (*first_items, none, none, *first_blocks))


def _attention(q, k, v, bsz, seq):
    sb = k.shape[-1]
    blk = min(ATTN_BLOCK, seq)
    n_blk = seq // blk
    n_pairs = sb // LANES
    q3, k3, v3 = (a.reshape(bsz, seq, a.shape[-1]) for a in (q, k, v))
    per_head = HEADS_PER_BLOCK * LANES
    jj = lax.broadcasted_iota(I32, (blk, blk), 0)
    ss = lax.broadcasted_iota(I32, (blk, blk), 1)
    tri = jnp.concatenate([(jj > ss).astype(BF16), jnp.ones((blk, LANES), BF16)], axis=1)
    masks = jnp.stack([jnp.ones((blk, blk), BF16), (ss < jj).astype(BF16),
                       jnp.zeros((blk, blk), BF16)])
    n_streams = ATTN_STREAMS if n_blk % ATTN_STREAMS == 0 else 2
    assert n_blk % n_streams == 0, (n_blk, n_streams)
    items, next_block, streams = [], [], []
    for x in range(n_streams):
        start = len(items)
        for i in range(x, n_blk, n_streams):
            block_end = len(items) + i + 1
            items += [(i, j) for j in range(i, -1, -1)]
            next_block += [block_end] * (i + 1)
        streams.append((start, len(items)))
    item_i = jnp.asarray([i for i, _ in items], I32)
    item_j = jnp.asarray([j for _, j in items], I32)
    item_next = jnp.asarray(next_block, I32)
    whole = lambda b, p, ti, tj, tn: (b, 0, p)
    grid_spec = pltpu.PrefetchScalarGridSpec(
        num_scalar_prefetch=3,
        grid=(bsz, n_pairs),
        in_specs=[
            pl.BlockSpec((None, seq, per_head), whole),
            pl.BlockSpec((None, seq, LANES), whole),
            pl.BlockSpec((None, seq, per_head), whole),
            pl.BlockSpec((blk, blk + LANES), lambda b, p, ti, tj, tn: (0, 0)),
            pl.BlockSpec((3, blk, blk), lambda b, p, ti, tj, tn: (0, 0, 0)),
        ],
        out_specs=pl.BlockSpec((None, seq, LANES), whole),
        scratch_shapes=[
            pltpu.VMEM((2, HEADS_PER_BLOCK * blk, blk), F32),
            pltpu.VMEM((2, HEADS_PER_BLOCK * blk, blk), BF16),
            pltpu.VMEM((2, HEADS_PER_BLOCK * blk, blk), F32),
            pltpu.VMEM((2, HEADS_PER_BLOCK * blk, LANES), F32),
            pltpu.VMEM((n_streams, blk, LANES), F32),
            pltpu.VMEM((n_streams, HEADS_PER_BLOCK * blk, LANES), F32),
        ],
    )
    out = pl.pallas_call(
        functools.partial(_attn_kernel, blk=blk, n_items=len(items), streams=tuple(streams)),
        grid_spec=grid_spec,
        out_shape=jax.ShapeDtypeStruct((bsz, seq, sb), BF16),
        compiler_params=pltpu.CompilerParams(
            dimension_semantics=("arbitrary", "arbitrary"), vmem_limit_bytes=40 * MIB),
        name="sb_attention",
    )(item_i, item_j, item_next, q3, k3, v3, tri, masks)
    return out.reshape(bsz * seq, sb)


def _ssm_params(lam_re, lam_im, log_dt, b_re, b_im, c_re, c_im, chunk, n_chunks):
    lam = lax.complex(lam_re.astype(F32), lam_im.astype(F32))
    dt = jnp.exp(log_dt.astype(F32))[:, None]
    lam_dt = lam * dt
    lam_bar = jnp.exp(lam_dt)
    b_bar = ((lam_bar - 1.0) / lam)[:, :, None] * lax.complex(b_re.astype(F32), b_im.astype(F32))
    b_t = jnp.swapaxes(b_bar, 1, 2)
    c_mat = lax.complex(c_re.astype(F32), c_im.astype(F32))
    steps = jnp.arange(chunk + 1, dtype=F32)
    pw = jnp.exp(lam_dt[:, None, :] * steps[None, :, None])
    npw = jnp.exp(-lam_dt[:, None, :] * steps[None, :chunk, None])

    def halves(lo, hi):
        return jnp.concatenate([lo, hi], axis=-1)

    def b_side(z):
        return [halves(jnp.real(z), jnp.real(z)), halves(-jnp.imag(z), jnp.imag(z))]

    def c_side(z):
        return [halves(jnp.real(z), jnp.imag(z)), halves(jnp.imag(z), jnp.real(z))]

    time_terms = jnp.stack(
        b_side(npw)
        + b_side(pw[:, chunk - 1::-1])
        + c_side(pw[:, :chunk])
        + c_side(pw[:, 1:]), axis=1)
    chan_terms = jnp.stack(
        [halves(jnp.real(b_t), jnp.imag(b_t)), halves(jnp.imag(b_t), jnp.real(b_t)),
         halves(jnp.real(c_mat), -jnp.real(c_mat)), halves(-jnp.imag(c_mat), -jnp.imag(c_mat))],
        axis=1)
    n_steps = max(1, (n_chunks - 1).bit_length())
    powers = []
    cur = pw[:, chunk, :]
    for _ in range(n_steps):
        powers.append(cur)
        cur = cur * cur
    lam_pow = jnp.stack(powers, axis=1)
    a1 = halves(jnp.real(lam_pow), jnp.real(lam_pow))
    a2 = halves(-jnp.imag(lam_pow), jnp.imag(lam_pow))
    return time_terms, chan_terms, a1, a2


def _ssm_kernel(u_ref, time_ref, chan_ref, a1_ref, a2_ref, perm_ref, perm_t_ref, y_ref,
                u8_ref, y8_ref, toep_ref, *, n_chunks, n_steps, group):
    seq = u_ref.shape[0]
    tiles = seq // SSM_FOLD
    lc = toep_ref.shape[0]
    per_chunk = (lc // group) // SSM_FOLD
    p2 = a1_ref.shape[2]
    group_shift = group.bit_length() - 1
    nt = (((1,), (1,)), ((), ()))

    folded = jnp.concatenate(
        [u_ref[pl.ds(s, tiles, stride=SSM_FOLD), :] for s in range(SSM_FOLD)], axis=1)
    regrouped = jnp.dot(folded.astype(BF16), perm_ref[...], preferred_element_type=F32)
    n_groups = LANES // group
    for g in range(n_groups):
        u8_ref[g] = regrouped[:, g * LANES:(g + 1) * LANES]

    for g in range(n_groups):
        def table(t1, t2, v1, v2):
            full = (time_ref[g, t1][:, None, :] * chan_ref[g, v1][None, :, :]
                    + time_ref[g, t2][:, None, :] * chan_ref[g, v2][None, :, :])
            return full.reshape(lc, p2).astype(BF16)

        src = table(0, 1, 0, 1)
        to_state = table(2, 3, 0, 1)
        dst = table(4, 5, 2, 3)
        from_state = table(6, 7, 2, 3)
        cb = min(256, lc)
        for j in range(lc // cb):
            blk = lax.dot_general(src, dst[j * cb:(j + 1) * cb], nt, preferred_element_type=F32)
            s_idx = lax.broadcasted_iota(I32, (lc, cb), 0) >> group_shift
            t_idx = (lax.broadcasted_iota(I32, (lc, cb), 1) + j * cb) >> group_shift
            toep_ref[:, j * cb:(j + 1) * cb] = jnp.where(s_idx <= t_idx, blk, 0.0).astype(BF16)
        u = jnp.concatenate(
            [u8_ref[g, pl.ds(tau, n_chunks, stride=per_chunk), :] for tau in range(per_chunk)],
            axis=1).astype(BF16)
        y = jnp.dot(u, toep_ref[...], preferred_element_type=F32)
        z = jnp.dot(u, to_state, preferred_element_type=F32)
        n = lax.broadcasted_iota(I32, z.shape, 0)
        x = jnp.where(n >= 1, pltpu.roll(z, 1, 0), 0.0)
        for k in range(n_steps):
            sh = 1 << k
            xs = jnp.where(n >= sh, pltpu.roll(x, sh, 0), 0.0)
            x = (x + a1_ref[g, k:k + 1, :] * xs
                 + a2_ref[g, k:k + 1, :] * pltpu.roll(xs, p2 // 2, 1))
        y = y + lax.dot_general(x.astype(BF16), from_state, nt, preferred_element_type=F32)
        for tau in range(per_chunk):
            y8_ref[g, pl.ds(tau, n_chunks, stride=per_chunk), :] = y[:, tau * LANES:(tau + 1) * LANES]

    y8 = jnp.concatenate([y8_ref[g] for g in range(n_groups)], axis=1)
    unfolded = jnp.dot(y8.astype(BF16), perm_t_ref[...], preferred_element_type=F32)
    for s in range(SSM_FOLD):
        y_ref[pl.ds(s, tiles, stride=SSM_FOLD), :] = unfolded[:, s * LANES:(s + 1) * LANES]


def _ssm(u, bsz, seq, lam_re, lam_im, log_dt, b_re, b_im, c_re, c_im):
    g, p = lam_re.shape
    c = b_re.shape[-1]
    chunk = min(SSM_CHUNK, seq)
    n_chunks = seq // chunk
    assert n_chunks & (n_chunks - 1) == 0 and c & (c - 1) == 0, (n_chunks, c)
    assert SSM_FOLD * c == LANES and chunk % SSM_FOLD == 0
    lc = chunk * c
    per_block = LANES // c
    time_terms, chan_terms, a1, a2 = _ssm_params(
        lam_re, lam_im, log_dt, b_re, b_im, c_re, c_im, chunk, n_chunks)
    n_steps = a1.shape[1]
    src_idx = jnp.arange(SSM_FOLD * LANES, dtype=I32)
    s_of, g_of, c_of = src_idx // LANES, (src_idx % LANES) // c, src_idx % c
    dst_idx = g_of * LANES + s_of * c + c_of
    perm = (dst_idx[:, None] == src_idx[None, :]).astype(BF16)
    block = lambda cb, b: (cb, 0, 0)
    fixed = lambda cb, b: (0, 0)
    y = pl.pallas_call(
        functools.partial(_ssm_kernel, n_chunks=n_chunks, n_steps=n_steps, group=c),
        grid=(g // per_block, bsz),
        in_specs=[
            pl.BlockSpec((None, seq, LANES), lambda cb, b: (b, 0, cb)),
            pl.BlockSpec((per_block,) + time_terms.shape[1:], lambda cb, b: (cb, 0, 0, 0)),
            pl.BlockSpec((per_block,) + chan_terms.shape[1:], lambda cb, b: (cb, 0, 0, 0)),
            pl.BlockSpec((per_block, n_steps, 2 * p), block),
            pl.BlockSpec((per_block, n_steps, 2 * p), block),
            pl.BlockSpec((SSM_FOLD * LANES, SSM_FOLD * LANES), fixed),
            pl.BlockSpec((SSM_FOLD * LANES, SSM_FOLD * LANES), fixed),
        ],
        out_specs=pl.BlockSpec((None, seq, LANES), lambda cb, b: (b, 0, cb)),
        out_shape=jax.ShapeDtypeStruct((bsz, seq, g * c), F32),
        scratch_shapes=[
            pltpu.VMEM((per_block, seq // SSM_FOLD, LANES), F32),
            pltpu.VMEM((per_block, seq // SSM_FOLD, LANES), F32),
            pltpu.VMEM((lc, lc), BF16),
        ],
        compiler_params=pltpu.CompilerParams(
            dimension_semantics=("arbitrary", "arbitrary"), vmem_limit_bytes=56 * MIB),
        name="s5_chunked_scan",
    )(u.reshape(bsz, seq, g * c), time_terms, chan_terms, a1, a2, perm, perm.T)
    return y.reshape(bsz * seq, g * c)


def _post_kernel(x_ref, ysb_ref, yss_ref, u_ref, d_ref, wglu_ref, gsb_ref, gssm_ref, wout_ref,
                 ln2_ref, wrt_ref, br_ref, tri_ref, x1_ref, xn_ref, idx_ref, gate_ref, rank_ref,
                 count_ref, running_ref, wglu_bf, wout_bf, *, sb, n_exp):
    @pl.when(pl.program_id(0) == 0)
    def _():
        running_ref[...] = jnp.zeros_like(running_ref)
        wglu_bf[...] = wglu_ref[...].astype(BF16)
        wout_bf[...] = wout_ref[...].astype(BF16)

    u = u_ref[...].astype(F32)
    y = yss_ref[...].astype(F32) + d_ref[...] * u
    y = y * (0.5 * (1.0 + jnp.tanh(math.sqrt(2.0 / math.pi) * (y + 0.044715 * (y * y * y)))))
    ab = jnp.dot(y.astype(BF16), wglu_bf[...], preferred_element_type=F32)
    w = ab.shape[1] // 2
    y_ssm = ab[:, :w] * (1.0 / (1.0 + jnp.exp(-ab[:, w:])))
    m_sb = _rms(ysb_ref[...].astype(F32), gsb_ref[...])
    m_ssm = _rms(y_ssm, gssm_ref[...])
    x1 = (x_ref[...]
          + jnp.dot(m_sb.astype(BF16), wout_bf[:sb, :], preferred_element_type=F32)
          + jnp.dot(m_ssm.astype(BF16), wout_bf[sb:, :], preferred_element_type=F32))
    x1_ref[...] = x1
    xn = _rms(x1, ln2_ref[...])
    xn_ref[...] = xn
    def split(a):
        hi = a.astype(BF16)
        return hi, (a - hi.astype(F32)).astype(BF16)

    xn_hi, xn_lo = split(xn)
    wt_hi, wt_lo = split(wrt_ref[...])
    nt = (((1,), (1,)), ((), ()))
    both = lax.dot_general(jnp.concatenate([wt_hi, wt_lo], axis=0), xn_hi, nt,
                           preferred_element_type=F32)
    low = lax.dot_general(wt_hi, xn_lo, nt, preferred_element_type=F32)
    logits = (both[:n_exp] + both[n_exp:] + low) + br_ref[...]
    tokens = logits.shape[1]
    expert = lax.broadcasted_iota(I32, logits.shape, 0).astype(F32)
    out_row = lax.broadcasted_iota(I32, (SUBLANES, tokens), 0)
    idx_out = jnp.zeros((SUBLANES, tokens), F32)
    val_out = jnp.zeros((SUBLANES, tokens), F32)
    top = None
    denom = None
    work = logits
    chosen = []
    for k in range(TOP_K):
        m = jnp.max(work, axis=0, keepdims=True)
        sel = jnp.min(jnp.where(work == m, expert, float(n_exp)), axis=0, keepdims=True)
        hit = expert == sel
        chosen.append(hit)
        work = jnp.where(hit, -jnp.inf, work)
        if k == 0:
            top = m
        e = jnp.exp(m - top)
        denom = e if denom is None else denom + e
        idx_out = jnp.where(out_row == k, sel, idx_out)
        val_out = jnp.where(out_row == k, e, val_out)
    idx_ref[...] = idx_out.astype(I32)
    gates = jnp.concatenate([val_out / denom, jnp.zeros((LANES - SUBLANES, tokens), F32)], axis=0)
    gate_ref[...] = jnp.transpose(gates)[:, :TOP_K]
    member = functools.reduce(jnp.logical_or, chosen).astype(BF16)
    before = running_ref[...] + jnp.dot(member, tri_ref[...], preferred_element_type=F32)
    rank_out = jnp.zeros((SUBLANES, tokens), F32)
    for k in range(TOP_K):
        rk = jnp.sum(jnp.where(chosen[k], before, 0.0), axis=0, keepdims=True)
        rank_out = jnp.where(out_row == k, rk, rank_out)
    rank_ref[...] = rank_out.astype(I32)
    running = running_ref[...] + jnp.sum(member.astype(F32), axis=1, keepdims=True)
    running_ref[...] = running
    count_ref[...] = running


def _post(x2, y_sb, y_ss, u, ssm_d, w_glu, g_sb, g_ssm, w_out, ln2_g, w_router, b_router):
    t, d = x2.shape
    sb = y_sb.shape[1]
    w = y_ss.shape[1]
    n_exp = w_router.shape[1]
    tm = min(TOKEN_TILE, t)
    row = lambda i: (i, 0)
    fixed = lambda i: (0, 0)
    earlier = (lax.broadcasted_iota(I32, (tm, tm), 0)
               < lax.broadcasted_iota(I32, (tm, tm), 1)).astype(BF16)
    return pl.pallas_call(
        functools.partial(_post_kernel, sb=sb, n_exp=n_exp),
        grid=(t // tm,),
        in_specs=[
            pl.BlockSpec((tm, d), row),
            pl.BlockSpec((tm, sb), row),
            pl.BlockSpec((tm, w), row),
            pl.BlockSpec((tm, w), row),
            pl.BlockSpec((1, w), fixed),
            pl.BlockSpec((w, 2 * w), fixed),
            pl.BlockSpec((1, sb), fixed),
            pl.BlockSpec((1, w), fixed),
            pl.BlockSpec((sb + w, d), fixed),
            pl.BlockSpec((1, d), fixed),
            pl.BlockSpec((n_exp, d), fixed),
            pl.BlockSpec((n_exp, 1), fixed),
            pl.BlockSpec((tm, tm), fixed),
        ],
        out_specs=[
            pl.BlockSpec((tm, d), row),
            pl.BlockSpec((tm, d), row),
            pl.BlockSpec((SUBLANES, tm), lambda i: (0, i)),
            pl.BlockSpec((tm, TOP_K), row),
            pl.BlockSpec((SUBLANES, tm), lambda i: (0, i)),
            pl.BlockSpec((n_exp, 1), fixed),
        ],
        out_shape=[
            jax.ShapeDtypeStruct((t, d), F32),
            jax.ShapeDtypeStruct((t, d), F32),
            jax.ShapeDtypeStruct((SUBLANES, t), I32),
            jax.ShapeDtypeStruct((t, TOP_K), F32),
            jax.ShapeDtypeStruct((SUBLANES, t), I32),
            jax.ShapeDtypeStruct((n_exp, 1), F32),
        ],
        scratch_shapes=[pltpu.VMEM((n_exp, 1), F32), pltpu.VMEM((w, 2 * w), BF16),
                        pltpu.VMEM((sb + w, d), BF16)],
        compiler_params=pltpu.CompilerParams(
            dimension_semantics=("arbitrary",), vmem_limit_bytes=48 * MIB),
        name="post_mixer_router",
    )(x2, y_sb, y_ss, u, ssm_d.reshape(1, w), w_glu, g_sb.reshape(1, sb),
      g_ssm.reshape(1, w), w_out, ln2_g.reshape(1, d), w_router.T,
      b_router.reshape(n_exp, 1), earlier)


def _plan(top_idx, rank, counts, rows_per_block):
    k, t = top_idx.shape
    n_exp = counts.shape[0]
    counts = counts.reshape(n_exp).astype(I32)
    padded = ((counts + rows_per_block - 1) // rows_per_block) * rows_per_block
    pad_ends = jnp.cumsum(padded)
    pad_starts = pad_ends - padded
    start_of = jnp.sum(jnp.where(top_idx[None] == jnp.arange(n_exp, dtype=I32)[:, None, None],
                                 pad_starts[:, None, None], 0), axis=0)
    dest = (start_of + rank).astype(I32)
    n_rows = t * k + n_exp * rows_per_block
    n_blocks = n_rows // rows_per_block
    block_start = jnp.arange(n_blocks, dtype=I32) * rows_per_block
    block_expert = jnp.minimum(
        jnp.sum((pad_ends[None, :] <= block_start[:, None]).astype(I32), axis=1), n_exp - 1)
    n_used = (pad_ends[-1] // rows_per_block).astype(I32).reshape(1)
    fill_start = jnp.concatenate([jnp.maximum(pad_ends - rows_per_block, 0).astype(I32), n_used])
    experts = jnp.arange(n_exp, dtype=I32)
    nonempty = counts > 0
    order = jnp.cumsum(nonempty.astype(I32)) - nonempty.astype(I32)
    later_ne = jnp.where((experts[None, :] > experts[:, None]) & nonempty[None, :],
                         experts[None, :], n_exp)
    next_tbl = jnp.min(later_ne, axis=1)
    onehot = block_expert[:, None] == experts[None, :]
    w_slot = jnp.sum(jnp.where(onehot, order & 1, 0), axis=1).astype(I32)
    w_next = jnp.sum(jnp.where(onehot, next_tbl, 0), axis=1).astype(I32)
    return dest, (block_expert, n_used, w_slot, w_next), fill_start, n_blocks


def _by_tile(dest, tile):
    k, t = dest.shape
    return dest.reshape(k, t // tile, tile).transpose(1, 0, 2).reshape(k * t)


def _dispatch_kernel(fill_ref, dest_hbm, xn_ref, xs_hbm, idx_smem, zeros_ref, idx_sem, row_sem,
                     fill_sem, *, tile, n_exp, fill_rows):
    i = pl.program_id(0)
    n = pl.num_programs(0)
    per_tile = TOP_K * tile

    def idx_copy(b):
        src = dest_hbm.at[pl.ds(pl.multiple_of(b * per_tile, per_tile), per_tile)]
        dst = idx_smem.at[pl.ds(pl.multiple_of((b % 2) * per_tile, per_tile), per_tile)]
        return pltpu.make_async_copy(src, dst, idx_sem.at[b % 2])

    @pl.when(i == 0)
    def _():
        idx_copy(0).start()
        zeros_ref[...] = jnp.zeros_like(zeros_ref)
        def fill(start):
            return pltpu.make_async_copy(
                zeros_ref, xs_hbm.at[pl.ds(pl.multiple_of(start, SUBLANES), fill_rows), 0, :],
                fill_sem)

        def start_unused(b, carry):
            fill(b * fill_rows).start()
            return carry

        def wait_unused(b, carry):
            fill(b * fill_rows).wait()
            return carry

        n_blocks = xs_hbm.shape[0] // fill_rows
        for e in range(n_exp):
            fill(fill_ref[e]).start()
        lax.fori_loop(fill_ref[n_exp], n_blocks, start_unused, 0)
        for e in range(n_exp):
            fill(fill_ref[e]).wait()
        lax.fori_loop(fill_ref[n_exp], n_blocks, wait_unused, 0)

    idx_copy(i).wait()

    @pl.when(i + 1 < n)
    def _():
        idx_copy(i + 1).start()

    base = pl.multiple_of((i % 2) * per_tile, per_tile)
    quarter = tile // TOP_K
    for part in range(TOP_K):
        def issue(g, carry, part=part):
            group = part * (quarter // SUBLANES) + g
            for s in range(SUBLANES):
                for k in range(TOP_K):
                    dst_row = idx_smem[base + k * tile + group * SUBLANES + s]
                    pltpu.make_async_copy(xn_ref.at[group, pl.ds(s, 1), :], xs_hbm.at[dst_row],
                                          row_sem.at[part]).start(priority=k % 2)
            return carry

        lax.fori_loop(0, quarter // SUBLANES, issue, 0)
    for part in range(TOP_K):
        pltpu.make_async_copy(xn_ref, xn_ref, row_sem.at[part]).wait()


def _dispatch(xn, dest, fill_start, n_rows):
    t, d = xn.shape
    tile = min(DISPATCH_TILE, t)
    n_exp = fill_start.shape[0] - 1
    grid_spec = pltpu.PrefetchScalarGridSpec(
        num_scalar_prefetch=1,
        grid=(t // tile,),
        in_specs=[
            pl.BlockSpec(memory_space=pl.ANY),
            pl.BlockSpec((tile // SUBLANES, SUBLANES, d), lambda i, fs: (i, 0, 0)),
        ],
        out_specs=pl.BlockSpec(memory_space=pl.ANY),
        scratch_shapes=[
            pltpu.SMEM((2 * TOP_K * tile,), I32),
            pltpu.VMEM((EXPERT_ROWS, d), F32),
            pltpu.SemaphoreType.DMA((2,)),
            pltpu.SemaphoreType.DMA((TOP_K,)),
            pltpu.SemaphoreType.DMA,
        ],
    )
    return pl.pallas_call(
        functools.partial(_dispatch_kernel, tile=tile, n_exp=n_exp, fill_rows=EXPERT_ROWS),
        grid_spec=grid_spec,
        out_shape=jax.ShapeDtypeStruct((n_rows, 1, d), F32),
        compiler_params=pltpu.CompilerParams(
            dimension_semantics=("arbitrary",), vmem_limit_bytes=40 * MIB),
        name="moe_dispatch",
    )(fill_start, _by_tile(dest, tile), xn.reshape(t // SUBLANES, SUBLANES, d))


def _expert_kernel(be_ref, nused_ref, wslot_ref, wnext_ref, xs_hbm, wg_hbm, bg_ref, wu_hbm, bu_ref,
                   wd_hbm, bd_ref, y_hbm, xbuf, ybuf, x_sem, y_sem, wg32, wu32, wd32, w_sem,
                   wg_bf, wu_bf, wd_bf, *, rows, n_exp):
    i = pl.program_id(0)
    n = nused_ref[0]

    def w_copies(e, slot):
        return [pltpu.make_async_copy(src.at[e], dst.at[slot], w_sem.at[slot])
                for src, dst in ((wg_hbm, wg32), (wu_hbm, wu32), (wd_hbm, wd32))]

    def x_copy(b):
        return pltpu.make_async_copy(
            xs_hbm.at[pl.ds(pl.multiple_of(b * rows, rows), rows), 0, :], xbuf.at[b % 2],
            x_sem.at[b % 2])

    def y_copy(b):
        return pltpu.make_async_copy(
            ybuf.at[b % 2], y_hbm.at[pl.ds(pl.multiple_of(b * rows, rows), rows), 0, :],
            y_sem.at[b % 2])

    @pl.when(i < n)
    def _():
        @pl.when(i == 0)
        def _():
            x_copy(0).start()
            for cp in w_copies(be_ref[0], wslot_ref[0]):
                cp.start()

        @pl.when(i + 1 < n)
        def _():
            x_copy(i + 1).start()

        changed = jnp.logical_or(i == 0, be_ref[i] != be_ref[jnp.maximum(i - 1, 0)])

        @pl.when(changed)
        def _():
            slot = wslot_ref[i]
            for cp in w_copies(be_ref[i], slot):
                cp.wait()

            @pl.when(wnext_ref[i] < n_exp)
            def _():
                for cp in w_copies(wnext_ref[i], 1 - slot):
                    cp.start(priority=1)
            wg_bf[...] = wg32[slot].astype(BF16)
            wu_bf[...] = wu32[slot].astype(BF16)
            wd_bf[...] = wd32[slot].astype(BF16)

        @pl.when(i >= 2)
        def _():
            y_copy(i - 2).wait()

        x_copy(i).wait()
        x = xbuf[i % 2].astype(BF16)
        gate = jnp.dot(x, wg_bf[...], preferred_element_type=F32) + bg_ref[...]
        up = jnp.dot(x, wu_bf[...], preferred_element_type=F32) + bu_ref[...]
        gate = jnp.minimum(gate, SWIGLU_LIMIT)
        up = jnp.clip(up, -SWIGLU_LIMIT, SWIGLU_LIMIT)
        glu = gate * (1.0 / (1.0 + jnp.exp(-SWIGLU_ALPHA * gate)))
        hidden = ((up + 1.0) * glu).astype(BF16)
        ybuf[i % 2] = jnp.dot(hidden, wd_bf[...], preferred_element_type=F32) + bd_ref[...]
        y_copy(i).start()

        @pl.when(i == n - 1)
        def _():
            y_copy(i).wait()

            @pl.when(i >= 1)
            def _():
                y_copy(i - 1).wait()

            def zero_copy(b):
                return pltpu.make_async_copy(
                    ybuf.at[0], y_hbm.at[pl.ds(pl.multiple_of(b * rows, rows), rows), 0, :],
                    y_sem.at[0])

            def start_zero(b, carry):
                zero_copy(b).start()
                return carry

            def wait_zero(b, carry):
                zero_copy(b).wait()
                return carry

            ybuf[0] = jnp.zeros((rows, ybuf.shape[2]), F32)
            lax.fori_loop(n, pl.num_programs(0), start_zero, 0)
            lax.fori_loop(n, pl.num_programs(0), wait_zero, 0)


def _experts(x_sorted, block_plan, n_blocks, w_gate, b_gate, w_up, b_up, w_down, b_down):
    n_rows, _, d = x_sorted.shape
    n_exp, _, f = w_gate.shape
    rows = EXPERT_ROWS
    bmap = lambda i, be, nu, ws, wn: (be[i], 0, 0)
    hbm = pl.BlockSpec(memory_space=pl.ANY)
    grid_spec = pltpu.PrefetchScalarGridSpec(
        num_scalar_prefetch=4,
        grid=(n_blocks,),
        in_specs=[
            hbm,
            hbm, pl.BlockSpec((None, 1, f), bmap),
            hbm, pl.BlockSpec((None, 1, f), bmap),
            hbm, pl.BlockSpec((None, 1, d), bmap),
        ],
        out_specs=hbm,
        scratch_shapes=[
            pltpu.VMEM((2, rows, d), F32),
            pltpu.VMEM((2, rows, d), F32),
            pltpu.SemaphoreType.DMA((2,)),
            pltpu.SemaphoreType.DMA((2,)),
            pltpu.VMEM((2, d, f), F32),
            pltpu.VMEM((2, d, f), F32),
            pltpu.VMEM((2, f, d), F32),
            pltpu.SemaphoreType.DMA((2,)),
            pltpu.VMEM((d, f), BF16),
            pltpu.VMEM((d, f), BF16),
            pltpu.VMEM((f, d), BF16),
        ],
    )
    return pl.pallas_call(
        functools.partial(_expert_kernel, rows=rows, n_exp=n_exp),
        grid_spec=grid_spec,
        out_shape=jax.ShapeDtypeStruct((n_rows, 1, d), F32),
        compiler_params=pltpu.CompilerParams(
            dimension_semantics=("arbitrary",), vmem_limit_bytes=52 * MIB),
        name="moe_experts",
    )(*block_plan, x_sorted,
      w_gate, b_gate.reshape(n_exp, 1, f), w_up, b_up.reshape(n_exp, 1, f),
      w_down, b_down.reshape(n_exp, 1, d))


def _combine_kernel(dest_hbm, y_hbm, x1_ref, gate_ref, lnf_ref, o_ref, idx_smem, ybuf,
                    idx_sem, row_sem, *, tile):
    i = pl.program_id(0)
    n = pl.num_programs(0)
    per_tile = TOP_K * tile

    def slot_base(b):
        return pl.multiple_of((b % 3) * per_tile, per_tile)

    def idx_copy(b):
        src = dest_hbm.at[pl.ds(pl.multiple_of(b * per_tile, per_tile), per_tile)]
        return pltpu.make_async_copy(src, idx_smem.at[pl.ds(slot_base(b), per_tile)],
                                     idx_sem.at[b % 3])

    def gather(b):
        base = slot_base(b)
        buf = ybuf.at[b % 2]
        sem = row_sem.at[b % 2]

        def issue(g, carry):
            for s in range(SUBLANES):
                pltpu.make_async_copy(y_hbm.at[idx_smem[base + g * SUBLANES + s]],
                                      buf.at[g, pl.ds(s, 1), :], sem).start(priority=s % 2)
            return carry

        lax.fori_loop(0, per_tile // SUBLANES, issue, 0)

    @pl.when(i == 0)
    def _():
        idx_copy(0).start()
        idx_copy(0).wait()
        gather(0)

        @pl.when(n > 1)
        def _():
            idx_copy(1).start()

    @pl.when(i + 1 < n)
    def _():
        idx_copy(i + 1).wait()

        @pl.when(i + 2 < n)
        def _():
            idx_copy(i + 2).start()
        gather(i + 1)

    cur = i % 2
    pltpu.make_async_copy(ybuf.at[1 - cur], ybuf.at[cur], row_sem.at[cur]).wait()
    gates = gate_ref[...]
    x = x1_ref[...]
    groups = tile // SUBLANES
    for k in range(TOP_K):
        yk = ybuf[cur, k * groups:(k + 1) * groups].reshape(tile, x.shape[1])
        x = x + gates[:, k:k + 1] * yk
    o_ref[...] = _rms(x, lnf_ref[...])


def _combine(dest, y_rows, x1, gates, ln_f_g):
    t, d = x1.shape
    tile = min(COMBINE_TILE, t)
    n_tiles = t // tile
    dest_tiles = _by_tile(dest, tile)
    return pl.pallas_call(
        functools.partial(_combine_kernel, tile=tile),
        grid=(n_tiles,),
        in_specs=[
            pl.BlockSpec(memory_space=pl.ANY),
            pl.BlockSpec(memory_space=pl.ANY),
            pl.BlockSpec((tile, d), lambda i: (i, 0)),
            pl.BlockSpec((tile, TOP_K), lambda i: (i, 0)),
            pl.BlockSpec((1, d), lambda i: (0, 0)),
        ],
        out_specs=pl.BlockSpec((tile, d), lambda i: (i, 0)),
        out_shape=jax.ShapeDtypeStruct((t, d), F32),
        scratch_shapes=[
            pltpu.SMEM((3 * TOP_K * tile,), I32),
            pltpu.VMEM((2, TOP_K * tile // SUBLANES, SUBLANES, d), F32),
            pltpu.SemaphoreType.DMA((3,)),
            pltpu.SemaphoreType.DMA((2,)),
        ],
        compiler_params=pltpu.CompilerParams(
            dimension_semantics=("arbitrary",), vmem_limit_bytes=40 * MIB),
        name="moe_combine",
    )(dest_tiles, y_rows, x1, gates, ln_f_g.reshape(1, d))


def kernel(x, ln1_g, w_in, lam_re, lam_im, log_dt, ssm_b_re, ssm_b_im, ssm_c_re, ssm_c_im,
           ssm_d, w_glu, g_sb, g_ssm, w_out, ln2_g, w_router, b_router, w_gate, b_gate,
           w_up, b_up, w_down, b_down, ln_f_g):
    bsz, seq, d = x.shape
    assert ln1_g.shape[0] == 1, "depth-1 block only"
    ssm_w = ssm_d.shape[1]
    sb = g_sb.shape[1]
    x2 = x.reshape(bsz * seq, d)
    q, k, v, u = _in_proj(x2, ln1_g[0], w_in[0], sb, ssm_w)
    y_sb = _attention(q, k, v, bsz, seq)
    y_ss = _ssm(u, bsz, seq, lam_re[0], lam_im[0], log_dt[0], ssm_b_re[0], ssm_b_im[0],
                ssm_c_re[0], ssm_c_im[0])
    x1, xn, top_idx, gates, rank, counts = _post(
        x2, y_sb, y_ss, u, ssm_d[0], w_glu[0], g_sb[0], g_ssm[0], w_out[0], ln2_g[0],
        w_router[0], b_router[0])
    dest, block_plan, fill_start, n_blocks = _plan(top_idx[:TOP_K], rank[:TOP_K], counts,
                                                   EXPERT_ROWS)
    x_sorted = _dispatch(xn, dest, fill_start, n_blocks * EXPERT_ROWS)
    y_rows = _experts(x_sorted, block_plan, n_blocks, w_gate[0], b_gate[0],
                      w_up[0], b_up[0], w_down[0], b_down[0])
    out = _combine(dest, y_rows, x1, gates, ln_f_g)
    return out.reshape(bsz, seq, d)
```

```python
import functools
import math

import jax
import jax.numpy as jnp
from jax import lax
from jax.experimental import pallas as pl
from jax.experimental.pallas import tpu as pltpu

F32 = jnp.float32
BF16 = jnp.bfloat16
I32 = jnp.int32

EPS = 1e-5
SB_HEAD_DIM = 64
SSM_GROUP = 16
SSM_STATE = 64
TOP_K = 4
SWIGLU_LIMIT = 7.0
SWIGLU_ALPHA = 1.702

LANES = 128
SUBLANES = 8
HEADS_PER_BLOCK = LANES // SB_HEAD_DIM
ATTN_BLOCK = 256
ATTN_EXP_FLOOR = 160.0
ATTN_STREAMS = 4
SSM_CHUNK = 16
SSM_FOLD = 8
TOKEN_TILE = 512
EXPERT_ROWS = 256
COMBINE_TILE = 256
COMBINE_PIECES = 8
DISPATCH_TILE = 2048
MIB = 1024 * 1024
LOG2E = 1.4426950408889634


def _rms(x, g):
    return x * lax.rsqrt(jnp.mean(x * x, axis=-1, keepdims=True) + EPS) * g


def _in_proj_kernel(x_ref, g_ref, w_ref, q_ref, k_ref, v_ref, u_ref, w_bf, *, sb, scale):
    @pl.when(pl.program_id(0) == 0)
    def _():
        w_bf[...] = w_ref[...].astype(BF16)

    h = _rms(x_ref[...], g_ref[...])
    proj = jnp.dot(h.astype(BF16), w_bf[...], preferred_element_type=F32)
    k_ref[...] = proj[:, sb:2 * sb].astype(BF16)
    u_ref[...] = proj[:, 3 * sb:]
    lane = lax.broadcasted_iota(I32, (1, LANES), 1)
    q = (proj[:, :sb] * scale).astype(BF16)
    v = proj[:, 2 * sb:3 * sb].astype(BF16)
    zero = jnp.zeros((), BF16)
    for pair in range(sb // LANES):
        cols = slice(pair * LANES, (pair + 1) * LANES)
        for head in range(HEADS_PER_BLOCK):
            own = (lane // SB_HEAD_DIM) == head
            out = slice((pair * HEADS_PER_BLOCK + head) * LANES,
                        (pair * HEADS_PER_BLOCK + head + 1) * LANES)
            q_ref[:, out] = jnp.where(own, q[:, cols], zero)
            v_ref[:, out] = jnp.where(own, v[:, cols], zero)


def _in_proj(x2, ln1_g, w_in, sb, ssm_w):
    t, d = x2.shape
    n_in = w_in.shape[1]
    tm = min(TOKEN_TILE, t)
    scale = LOG2E / math.sqrt(SB_HEAD_DIM)
    return pl.pallas_call(
        functools.partial(_in_proj_kernel, sb=sb, scale=scale),
        grid=(t // tm,),
        in_specs=[
            pl.BlockSpec((tm, d), lambda i: (i, 0)),
            pl.BlockSpec((1, d), lambda i: (0, 0)),
            pl.BlockSpec((d, n_in), lambda i: (0, 0)),
        ],
        out_specs=[
            pl.BlockSpec((tm, HEADS_PER_BLOCK * sb), lambda i: (i, 0)),
            pl.BlockSpec((tm, sb), lambda i: (i, 0)),
            pl.BlockSpec((tm, HEADS_PER_BLOCK * sb), lambda i: (i, 0)),
            pl.BlockSpec((tm, ssm_w), lambda i: (i, 0)),
        ],
        out_shape=[
            jax.ShapeDtypeStruct((t, HEADS_PER_BLOCK * sb), BF16),
            jax.ShapeDtypeStruct((t, sb), BF16),
            jax.ShapeDtypeStruct((t, HEADS_PER_BLOCK * sb), BF16),
            jax.ShapeDtypeStruct((t, ssm_w), F32),
        ],
        scratch_shapes=[pltpu.VMEM((d, n_in), BF16)],
        compiler_params=pltpu.CompilerParams(
            dimension_semantics=("arbitrary",), vmem_limit_bytes=48 * MIB),
        name="in_proj",
    )(x2, ln1_g.reshape(1, d), w_in)


def _attn_kernel(ti_ref, tj_ref, tn_ref, q_ref, k_ref, v_ref, tri_ref, mask_ref, o_ref,
                 dbuf, spbuf, lbuf, rsbuf, acc_ref, r_ref, *, blk, n_items, streams):
    sign_bit = jnp.uint32(0x80000000)
    last = n_items - 1
    ALL, CAUSAL, NONE = 0, 1, 2

    for ref in (dbuf, spbuf, lbuf, rsbuf, acc_ref, r_ref):
        ref[...] = jnp.zeros_like(ref)

    def both_heads(ref, start):
        rows = pl.ds(pl.multiple_of(start, blk), blk)
        return jnp.concatenate([ref[rows, h * LANES:(h + 1) * LANES]
                                for h in range(HEADS_PER_BLOCK)], axis=0)

    def item(idx):
        c = jnp.clip(idx, 0, last)
        return ti_ref[c], tj_ref[c], tn_ref[c]

    n_streams = len(streams)
    stream_end = [end for _, end in streams]

    def step(state, x):
        cur, p1, p2 = list(state[:n_streams]), state[n_streams], state[n_streams + 1]
        out_blk = list(state[n_streams + 2:])
        slot, other = x % 2, 1 - x % 2
        x3 = (x - 2) % n_streams
        c1 = jnp.where(cur[x] < stream_end[x], cur[x], n_items)
        cur[x] = jnp.minimum(cur[x] + 1, stream_end[x])
        c3 = p2
        i1, j1, _ = item(c1)
        ks = k_ref[pl.ds(pl.multiple_of(j1 * blk, blk), blk), :]
        w = lax.dot_general(both_heads(q_ref, i1 * blk), ks, (((1,), (1,)), ((), ())),
                            preferred_element_type=F32)
        i3, j3, next_block = item(c3)
        valid = c3 <= last
        first = jnp.logical_and(i3 == j3, valid)
        mask3 = mask_ref[jnp.where(valid, (i3 == j3).astype(I32), NONE)]
        i3 = jnp.where(valid, i3, out_blk[x3])
        r_prev = jnp.where(first, 0.0, r_ref[x3])
        r_new = r_prev + rsbuf[other]
        r_ref[x3] = r_new
        done = jnp.logical_and(valid, jnp.min(r_new) >= ATTN_EXP_FLOOR)
        cur[x3] = jnp.where(done, jnp.maximum(cur[x3], next_block), cur[x3])
        out_blk[x3] = i3
        r_wide = jnp.concatenate([r_prev] * (blk // LANES), axis=1)
        a = jnp.exp2(dbuf[slot] - (lbuf[other] + r_wide))
        ab = a.astype(BF16) * jnp.concatenate([mask3, mask3], axis=0)
        a_cat = jnp.concatenate([ab[:blk], ab[blk:]], axis=1)
        acc = jnp.where(first, 0.0, acc_ref[x3]) + jnp.dot(
            a_cat, both_heads(v_ref, j3 * blk), preferred_element_type=F32)
        acc_ref[x3] = acc
        o_ref[pl.ds(pl.multiple_of(i3 * blk, blk), blk), :] = acc.astype(o_ref.dtype)
        sums = jnp.dot(spbuf[other], tri_ref[...], preferred_element_type=F32)
        lbuf[slot] = sums[:, :blk]
        rsbuf[slot] = sums[:, blk:]
        mask1 = mask_ref[(i1 == j1).astype(I32)]
        neg_abs = lax.bitcast_convert_type(
            lax.bitcast_convert_type(w, jnp.uint32) | sign_bit, F32)
        sp2 = jnp.maximum(w, 0.0) + jnp.log(1.0 + jnp.exp2(neg_abs)) * LOG2E
        dbuf[slot] = w - sp2
        spbuf[slot] = sp2.astype(BF16) * jnp.concatenate([mask1, mask1], axis=0)
        return (*cur, c1, p1, *out_blk)

    def in_flight(state):
        p1, p2 = state[n_streams], state[n_streams + 1]
        live = [state[x] < stream_end[x] for x in range(n_streams)]
        return functools.reduce(jnp.logical_or, live + [p1 <= last, p2 <= last])

    def one_round(state):
        for x in range(n_streams):
            state = step(state, x)
        return state

    none = jnp.int32(n_items)
    first_items = [jnp.int32(start) for start, _ in streams]
    first_blocks = [ti_ref[start] for start, _ in streams]
    lax.while_loop(in_flight, one_round, (*first_items, none, none, *first_blocks))


def _attention(q, k, v, bsz, seq):
    sb = k.shape[-1]
    blk = min(ATTN_BLOCK, seq)
    n_blk = seq // blk
    n_pairs = sb // LANES
    q3, k3, v3 = (a.reshape(bsz, seq, a.shape[-1]) for a in (q, k, v))
    per_head = HEADS_PER_BLOCK * LANES
    jj = lax.broadcasted_iota(I32, (blk, blk), 0)
    ss = lax.broadcasted_iota(I32, (blk, blk), 1)
    tri = jnp.concatenate([(jj > ss).astype(BF16), jnp.ones((blk, LANES), BF16)], axis=1)
    masks = jnp.stack([jnp.ones((blk, blk), BF16), (ss < jj).astype(BF16),
                       jnp.zeros((blk, blk), BF16)])
    n_streams = ATTN_STREAMS if n_blk % ATTN_STREAMS == 0 else 2
    assert n_blk % n_streams == 0, (n_blk, n_streams)
    items, next_block, streams = [], [], []
    for x in range(n_streams):
        start = len(items)
        for i in range(x, n_blk, n_streams):
            block_end = len(items) + i + 1
            items += [(i, j) for j in range(i, -1, -1)]
            next_block += [block_end] * (i + 1)
        streams.append((start, len(items)))
    item_i = jnp.asarray([i for i, _ in items], I32)
    item_j = jnp.asarray([j for _, j in items], I32)
    item_next = jnp.asarray(next_block, I32)
    whole = lambda b, p, ti, tj, tn: (b, 0, p)
    grid_spec = pltpu.PrefetchScalarGridSpec(
        num_scalar_prefetch=3,
        grid=(bsz, n_pairs),
        in_specs=[
            pl.BlockSpec((None, seq, per_head), whole),
            pl.BlockSpec((None, seq, LANES), whole),
            pl.BlockSpec((None, seq, per_head), whole),
            pl.BlockSpec((blk, blk + LANES), lambda b, p, ti, tj, tn: (0, 0)),
            pl.BlockSpec((3, blk, blk), lambda b, p, ti, tj, tn: (0, 0, 0)),
        ],
        out_specs=pl.BlockSpec((None, seq, LANES), whole),
        scratch_shapes=[
            pltpu.VMEM((2, HEADS_PER_BLOCK * blk, blk), F32),
            pltpu.VMEM((2, HEADS_PER_BLOCK * blk, blk), BF16),
            pltpu.VMEM((2, HEADS_PER_BLOCK * blk, blk), F32),
            pltpu.VMEM((2, HEADS_PER_BLOCK * blk, LANES), F32),
            pltpu.VMEM((n_streams, blk, LANES), F32),
            pltpu.VMEM((n_streams, HEADS_PER_BLOCK * blk, LANES), F32),
        ],
    )
    out = pl.pallas_call(
        functools.partial(_attn_kernel, blk=blk, n_items=len(items), streams=tuple(streams)),
        grid_spec=grid_spec,
        out_shape=jax.ShapeDtypeStruct((bsz, seq, sb), BF16),
        compiler_params=pltpu.CompilerParams(
            dimension_semantics=("arbitrary", "arbitrary"), vmem_limit_bytes=40 * MIB),
        name="sb_attention",
    )(item_i, item_j, item_next, q3, k3, v3, tri, masks)
    return out.reshape(bsz * seq, sb)


def _ssm_params(lam_re, lam_im, log_dt, b_re, b_im, c_re, c_im, chunk, n_chunks):
    lam = lax.complex(lam_re.astype(F32), lam_im.astype(F32))
    dt = jnp.exp(log_dt.astype(F32))[:, None]
    lam_dt = lam * dt
    lam_bar = jnp.exp(lam_dt)
    b_bar = ((lam_bar - 1.0) / lam)[:, :, None] * lax.complex(b_re.astype(F32), b_im.astype(F32))
    b_t = jnp.swapaxes(b_bar, 1, 2)
    c_mat = lax.complex(c_re.astype(F32), c_im.astype(F32))
    steps = jnp.arange(chunk + 1, dtype=F32)
    pw = jnp.exp(lam_dt[:, None, :] * steps[None, :, None])
    npw = jnp.exp(-lam_dt[:, None, :] * steps[None, :chunk, None])

    def halves(lo, hi):
        return jnp.concatenate([lo, hi], axis=-1)

    def b_side(z):
        return [halves(jnp.real(z), jnp.real(z)), halves(-jnp.imag(z), jnp.imag(z))]

    def c_side(z):
        return [halves(jnp.real(z), jnp.imag(z)), halves(jnp.imag(z), jnp.real(z))]

    time_terms = jnp.stack(
        b_side(npw)
        + b_side(pw[:, chunk - 1::-1])
        + c_side(pw[:, :chunk])
        + c_side(pw[:, 1:]), axis=1)
    chan_terms = jnp.stack(
        [halves(jnp.real(b_t), jnp.imag(b_t)), halves(jnp.imag(b_t), jnp.real(b_t)),
         halves(jnp.real(c_mat), -jnp.real(c_mat)), halves(-jnp.imag(c_mat), -jnp.imag(c_mat))],
        axis=1)
    n_steps = max(1, (n_chunks - 1).bit_length())
    powers = []
    cur = pw[:, chunk, :]
    for _ in range(n_steps):
        powers.append(cur)
        cur = cur * cur
    lam_pow = jnp.stack(powers, axis=1)
    a1 = halves(jnp.real(lam_pow), jnp.real(lam_pow))
    a2 = halves(-jnp.imag(lam_pow), jnp.imag(lam_pow))
    return time_terms, chan_terms, a1, a2


def _ssm_kernel(u_ref, time_ref, chan_ref, a1_ref, a2_ref, perm_ref, perm_t_ref, y_ref,
                u8_ref, y8_ref, toep_ref, *, n_chunks, n_steps, group):
    seq = u_ref.shape[0]
    tiles = seq // SSM_FOLD
    lc = toep_ref.shape[0]
    per_chunk = (lc // group) // SSM_FOLD
    p2 = a1_ref.shape[2]
    group_shift = group.bit_length() - 1
    nt = (((1,), (1,)), ((), ()))

    folded = jnp.concatenate(
        [u_ref[pl.ds(s, tiles, stride=SSM_FOLD), :] for s in range(SSM_FOLD)], axis=1)
    regrouped = jnp.dot(folded.astype(BF16), perm_ref[...], preferred_element_type=F32)
    n_groups = LANES // group
    for g in range(n_groups):
        u8_ref[g] = regrouped[:, g * LANES:(g + 1) * LANES]

    for g in range(n_groups):
        def table(t1, t2, v1, v2):
            full = (time_ref[g, t1][:, None, :] * chan_ref[g, v1][None, :, :]
                    + time_ref[g, t2][:, None, :] * chan_ref[g, v2][None, :, :])
            return full.reshape(lc, p2).astype(BF16)

        src = table(0, 1, 0, 1)
        to_state = table(2, 3, 0, 1)
        dst = table(4, 5, 2, 3)
        from_state = table(6, 7, 2, 3)
        cb = min(256, lc)
        for j in range(lc // cb):
            blk = lax.dot_general(src, dst[j * cb:(j + 1) * cb], nt, preferred_element_type=F32)
            s_idx = lax.broadcasted_iota(I32, (lc, cb), 0) >> group_shift
            t_idx = (lax.broadcasted_iota(I32, (lc, cb), 1) + j * cb) >> group_shift
            toep_ref[:, j * cb:(j + 1) * cb] = jnp.where(s_idx <= t_idx, blk, 0.0).astype(BF16)
        u = jnp.concatenate(
            [u8_ref[g, pl.ds(tau, n_chunks, stride=per_chunk), :] for tau in range(per_chunk)],
            axis=1).astype(BF16)
        y = jnp.dot(u, toep_ref[...], preferred_element_type=F32)
        z = jnp.dot(u, to_state, preferred_element_type=F32)
        n = lax.broadcasted_iota(I32, z.shape, 0)
        x = jnp.where(n >= 1, pltpu.roll(z, 1, 0), 0.0)
        for k in range(n_steps):
            sh = 1 << k
            xs = jnp.where(n >= sh, pltpu.roll(x, sh, 0), 0.0)
            x = (x + a1_ref[g, k:k + 1, :] * xs
                 + a2_ref[g, k:k + 1, :] * pltpu.roll(xs, p2 // 2, 1))
        y = y + lax.dot_general(x.astype(BF16), from_state, nt, preferred_element_type=F32)
        for tau in range(per_chunk):
            y8_ref[g, pl.ds(tau, n_chunks, stride=per_chunk), :] = y[:, tau * LANES:(tau + 1) * LANES]

    y8 = jnp.concatenate([y8_ref[g] for g in range(n_groups)], axis=1)
    unfolded = jnp.dot(y8.astype(BF16), perm_t_ref[...], preferred_element_type=F32)
    for s in range(SSM_FOLD):
        y_ref[pl.ds(s, tiles, stride=SSM_FOLD), :] = unfolded[:, s * LANES:(s + 1) * LANES]


def _ssm(u, bsz, seq, lam_re, lam_im, log_dt, b_re, b_im, c_re, c_im):
    g, p = lam_re.shape
    c = b_re.shape[-1]
    chunk = min(SSM_CHUNK, seq)
    n_chunks = seq // chunk
    assert n_chunks & (n_chunks - 1) == 0 and c & (c - 1) == 0, (n_chunks, c)
    assert SSM_FOLD * c == LANES and chunk % SSM_FOLD == 0
    lc = chunk * c
    per_block = LANES // c
    time_terms, chan_terms, a1, a2 = _ssm_params(
        lam_re, lam_im, log_dt, b_re, b_im, c_re, c_im, chunk, n_chunks)
    n_steps = a1.shape[1]
    src_idx = jnp.arange(SSM_FOLD * LANES, dtype=I32)
    s_of, g_of, c_of = src_idx // LANES, (src_idx % LANES) // c, src_idx % c
    dst_idx = g_of * LANES + s_of * c + c_of
    perm = (dst_idx[:, None] == src_idx[None, :]).astype(BF16)
    block = lambda cb, b: (cb, 0, 0)
    fixed = lambda cb, b: (0, 0)
    y = pl.pallas_call(
        functools.partial(_ssm_kernel, n_chunks=n_chunks, n_steps=n_steps, group=c),
        grid=(g // per_block, bsz),
        in_specs=[
            pl.BlockSpec((None, seq, LANES), lambda cb, b: (b, 0, cb)),
            pl.BlockSpec((per_block,) + time_terms.shape[1:], lambda cb, b: (cb, 0, 0, 0)),
            pl.BlockSpec((per_block,) + chan_terms.shape[1:], lambda cb, b: (cb, 0, 0, 0)),
            pl.BlockSpec((per_block, n_steps, 2 * p), block),
            pl.BlockSpec((per_block, n_steps, 2 * p), block),
            pl.BlockSpec((SSM_FOLD * LANES, SSM_FOLD * LANES), fixed),
            pl.BlockSpec((SSM_FOLD * LANES, SSM_FOLD * LANES), fixed),
        ],
        out_specs=pl.BlockSpec((None, seq, LANES), lambda cb, b: (b, 0, cb)),
        out_shape=jax.ShapeDtypeStruct((bsz, seq, g * c), F32),
        scratch_shapes=[
            pltpu.VMEM((per_block, seq // SSM_FOLD, LANES), F32),
            pltpu.VMEM((per_block, seq // SSM_FOLD, LANES), F32),
            pltpu.VMEM((lc, lc), BF16),
        ],
        compiler_params=pltpu.CompilerParams(
            dimension_semantics=("arbitrary", "arbitrary"), vmem_limit_bytes=56 * MIB),
        name="s5_chunked_scan",
    )(u.reshape(bsz, seq, g * c), time_terms, chan_terms, a1, a2, perm, perm.T)
    return y.reshape(bsz * seq, g * c)


def _post_kernel(x_ref, ysb_ref, yss_ref, u_ref, d_ref, wglu_ref, gsb_ref, gssm_ref, wout_ref,
                 ln2_ref, wrt_ref, br_ref, tri_ref, x1_ref, xn_ref, idx_ref, gate_ref, rank_ref,
                 count_ref, running_ref, wglu_bf, wout_bf, *, sb, n_exp):
    @pl.when(pl.program_id(0) == 0)
    def _():
        running_ref[...] = jnp.zeros_like(running_ref)
        wglu_bf[...] = wglu_ref[...].astype(BF16)
        wout_bf[...] = wout_ref[...].astype(BF16)

    u = u_ref[...].astype(F32)
    y = yss_ref[...].astype(F32) + d_ref[...] * u
    y = y * (0.5 * (1.0 + jnp.tanh(math.sqrt(2.0 / math.pi) * (y + 0.044715 * (y * y * y)))))
    ab = jnp.dot(y.astype(BF16), wglu_bf[...], preferred_element_type=F32)
    w = ab.shape[1] // 2
    y_ssm = ab[:, :w] * (1.0 / (1.0 + jnp.exp(-ab[:, w:])))
    m_sb = _rms(ysb_ref[...].astype(F32), gsb_ref[...])
    m_ssm = _rms(y_ssm, gssm_ref[...])
    x1 = (x_ref[...]
          + jnp.dot(m_sb.astype(BF16), wout_bf[:sb, :], preferred_element_type=F32)
          + jnp.dot(m_ssm.astype(BF16), wout_bf[sb:, :], preferred_element_type=F32))
    x1_ref[...] = x1
    xn = _rms(x1, ln2_ref[...])
    xn_ref[...] = xn
    def split(a):
        hi = a.astype(BF16)
        return hi, (a - hi.astype(F32)).astype(BF16)

    xn_hi, xn_lo = split(xn)
    wt_hi, wt_lo = split(wrt_ref[...])
    nt = (((1,), (1,)), ((), ()))
    both = lax.dot_general(jnp.concatenate([wt_hi, wt_lo], axis=0), xn_hi, nt,
                           preferred_element_type=F32)
    low = lax.dot_general(wt_hi, xn_lo, nt, preferred_element_type=F32)
    logits = (both[:n_exp] + both[n_exp:] + low) + br_ref[...]
    tokens = logits.shape[1]
    expert = lax.broadcasted_iota(I32, logits.shape, 0).astype(F32)
    out_row = lax.broadcasted_iota(I32, (SUBLANES, tokens), 0)
    idx_out = jnp.zeros((SUBLANES, tokens), F32)
    val_out = jnp.zeros((SUBLANES, tokens), F32)
    top = None
    denom = None
    work = logits
    chosen = []
    for k in range(TOP_K):
        m = jnp.max(work, axis=0, keepdims=True)
        sel = jnp.min(jnp.where(work == m, expert, float(n_exp)), axis=0, keepdims=True)
        hit = expert == sel
        chosen.append(hit)
        work = jnp.where(hit, -jnp.inf, work)
        if k == 0:
            top = m
        e = jnp.exp(m - top)
        denom = e if denom is None else denom + e
        idx_out = jnp.where(out_row == k, sel, idx_out)
        val_out = jnp.where(out_row == k, e, val_out)
    idx_ref[...] = idx_out.astype(I32)
    gates = jnp.concatenate([val_out / denom, jnp.zeros((LANES - SUBLANES, tokens), F32)], axis=0)
    gate_ref[...] = jnp.transpose(gates)[:, :TOP_K]
    member = functools.reduce(jnp.logical_or, chosen).astype(BF16)
    before = running_ref[...] + jnp.dot(member, tri_ref[...], preferred_element_type=F32)
    rank_out = jnp.zeros((SUBLANES, tokens), F32)
    for k in range(TOP_K):
        rk = jnp.sum(jnp.where(chosen[k], before, 0.0), axis=0, keepdims=True)
        rank_out = jnp.where(out_row == k, rk, rank_out)
    rank_ref[...] = rank_out.astype(I32)
    running = running_ref[...] + jnp.sum(member.astype(F32), axis=1, keepdims=True)
    running_ref[...] = running
    count_ref[...] = running


def _post(x2, y_sb, y_ss, u, ssm_d, w_glu, g_sb, g_ssm, w_out, ln2_g, w_router, b_router):
    t, d = x2.shape
    sb = y_sb.shape[1]
    w = y_ss.shape[1]
    n_exp = w_router.shape[1]
    tm = min(TOKEN_TILE, t)
    row = lambda i: (i, 0)
    fixed = lambda i: (0, 0)
    earlier = (lax.broadcasted_iota(I32, (tm, tm), 0)
               < lax.broadcasted_iota(I32, (tm, tm), 1)).astype(BF16)
    return pl.pallas_call(
        functools.partial(_post_kernel, sb=sb, n_exp=n_exp),
        grid=(t // tm,),
        in_specs=[
            pl.BlockSpec((tm, d), row),
            pl.BlockSpec((tm, sb), row),
            pl.BlockSpec((tm, w), row),
            pl.BlockSpec((tm, w), row),
            pl.BlockSpec((1, w), fixed),
            pl.BlockSpec((w, 2 * w), fixed),
            pl.BlockSpec((1, sb), fixed),
            pl.BlockSpec((1, w), fixed),
            pl.BlockSpec((sb + w, d), fixed),
            pl.BlockSpec((1, d), fixed),
            pl.BlockSpec((n_exp, d), fixed),
            pl.BlockSpec((n_exp, 1), fixed),
            pl.BlockSpec((tm, tm), fixed),
        ],
        out_specs=[
            pl.BlockSpec((tm, d), row),
            pl.BlockSpec((tm, d), row),
            pl.BlockSpec((SUBLANES, tm), lambda i: (0, i)),
            pl.BlockSpec((tm, TOP_K), row),
            pl.BlockSpec((SUBLANES, tm), lambda i: (0, i)),
            pl.BlockSpec((n_exp, 1), fixed),
        ],
        out_shape=[
            jax.ShapeDtypeStruct((t, d), F32),
            jax.ShapeDtypeStruct((t, d), F32),
            jax.ShapeDtypeStruct((SUBLANES, t), I32),
            jax.ShapeDtypeStruct((t, TOP_K), F32),
            jax.ShapeDtypeStruct((SUBLANES, t), I32),
            jax.ShapeDtypeStruct((n_exp, 1), F32),
        ],
        scratch_shapes=[pltpu.VMEM((n_exp, 1), F32), pltpu.VMEM((w, 2 * w), BF16),
                        pltpu.VMEM((sb + w, d), BF16)],
        compiler_params=pltpu.CompilerParams(
            dimension_semantics=("arbitrary",), vmem_limit_bytes=48 * MIB),
        name="post_mixer_router",
    )(x2, y_sb, y_ss, u, ssm_d.reshape(1, w), w_glu, g_sb.reshape(1, sb),
      g_ssm.reshape(1, w), w_out, ln2_g.reshape(1, d), w_router.T,
      b_router.reshape(n_exp, 1), earlier)


def _plan(top_idx, rank, counts, rows_per_block):
    k, t = top_idx.shape
    n_exp = counts.shape[0]
    counts = counts.reshape(n_exp).astype(I32)
    padded = ((counts + rows_per_block - 1) // rows_per_block) * rows_per_block
    pad_ends = jnp.cumsum(padded)
    pad_starts = pad_ends - padded
    start_of = jnp.sum(jnp.where(top_idx[None] == jnp.arange(n_exp, dtype=I32)[:, None, None],
                                 pad_starts[:, None, None], 0), axis=0)
    dest = (start_of + rank).astype(I32)
    n_rows = t * k + n_exp * rows_per_block
    n_blocks = n_rows // rows_per_block
    block_start = jnp.arange(n_blocks, dtype=I32) * rows_per_block
    block_expert = jnp.minimum(
        jnp.sum((pad_ends[None, :] <= block_start[:, None]).astype(I32), axis=1), n_exp - 1)
    n_used = (pad_ends[-1] // rows_per_block).astype(I32).reshape(1)
    fill_start = jnp.concatenate([jnp.maximum(pad_ends - rows_per_block, 0).astype(I32), n_used])
    experts = jnp.arange(n_exp, dtype=I32)
    nonempty = counts > 0
    order = jnp.cumsum(nonempty.astype(I32)) - nonempty.astype(I32)
    later_ne = jnp.where((experts[None, :] > experts[:, None]) & nonempty[None, :],
                         experts[None, :], n_exp)
    next_tbl = jnp.min(later_ne, axis=1)
    onehot = block_expert[:, None] == experts[None, :]
    w_slot = jnp.sum(jnp.where(onehot, order & 1, 0), axis=1).astype(I32)
    w_next = jnp.sum(jnp.where(onehot, next_tbl, 0), axis=1).astype(I32)
    return dest, (block_expert, n_used, w_slot, w_next), fill_start, n_blocks


def _by_tile(dest, tile):
    k, t = dest.shape
    return dest.reshape(k, t // tile, tile).transpose(1, 0, 2).reshape(k * t)


def _dispatch_kernel(fill_ref, dest_hbm, xn_ref, xs_hbm, idx_smem, zeros_ref, idx_sem, row_sem,
                     fill_sem, *, tile, n_exp, fill_rows):
    i = pl.program_id(0)
    n = pl.num_programs(0)
    per_tile = TOP_K * tile

    def idx_copy(b):
        src = dest_hbm.at[pl.ds(pl.multiple_of(b * per_tile, per_tile), per_tile)]
        dst = idx_smem.at[pl.ds(pl.multiple_of((b % 2) * per_tile, per_tile), per_tile)]
        return pltpu.make_async_copy(src, dst, idx_sem.at[b % 2])

    @pl.when(i == 0)
    def _():
        idx_copy(0).start()
        zeros_ref[...] = jnp.zeros_like(zeros_ref)
        def fill(start):
            return pltpu.make_async_copy(
                zeros_ref, xs_hbm.at[pl.ds(pl.multiple_of(start, SUBLANES), fill_rows), 0, :],
                fill_sem)

        def start_unused(b, carry):
            fill(b * fill_rows).start()
            return carry

        def wait_unused(b, carry):
            fill(b * fill_rows).wait()
            return carry

        n_blocks = xs_hbm.shape[0] // fill_rows
        for e in range(n_exp):
            fill(fill_ref[e]).start()
        lax.fori_loop(fill_ref[n_exp], n_blocks, start_unused, 0)
        for e in range(n_exp):
            fill(fill_ref[e]).wait()
        lax.fori_loop(fill_ref[n_exp], n_blocks, wait_unused, 0)

    idx_copy(i).wait()

    @pl.when(i + 1 < n)
    def _():
        idx_copy(i + 1).start()

    base = pl.multiple_of((i % 2) * per_tile, per_tile)
    quarter = tile // TOP_K
    for part in range(TOP_K):
        def issue(g, carry, part=part):
            group = part * (quarter // SUBLANES) + g
            for s in range(SUBLANES):
                for k in range(TOP_K):
                    dst_row = idx_smem[base + k * tile + group * SUBLANES + s]
                    pltpu.make_async_copy(xn_ref.at[group, pl.ds(s, 1), :], xs_hbm.at[dst_row],
                                          row_sem.at[part]).start(priority=k % 2)
            return carry

        lax.fori_loop(0, quarter // SUBLANES, issue, 0)
    for part in range(TOP_K):
        pltpu.make_async_copy(xn_ref, xn_ref, row_sem.at[part]).wait()


def _dispatch(xn, dest, fill_start, n_rows):
    t, d = xn.shape
    tile = min(DISPATCH_TILE, t)
    n_exp = fill_start.shape[0] - 1
    grid_spec = pltpu.PrefetchScalarGridSpec(
        num_scalar_prefetch=1,
        grid=(t // tile,),
        in_specs=[
            pl.BlockSpec(memory_space=pl.ANY),
            pl.BlockSpec((tile // SUBLANES, SUBLANES, d), lambda i, fs: (i, 0, 0)),
        ],
        out_specs=pl.BlockSpec(memory_space=pl.ANY),
        scratch_shapes=[
            pltpu.SMEM((2 * TOP_K * tile,), I32),
            pltpu.VMEM((EXPERT_ROWS, d), F32),
            pltpu.SemaphoreType.DMA((2,)),
            pltpu.SemaphoreType.DMA((TOP_K,)),
            pltpu.SemaphoreType.DMA,
        ],
    )
    return pl.pallas_call(
        functools.partial(_dispatch_kernel, tile=tile, n_exp=n_exp, fill_rows=EXPERT_ROWS),
        grid_spec=grid_spec,
        out_shape=jax.ShapeDtypeStruct((n_rows, 1, d), F32),
        compiler_params=pltpu.CompilerParams(
            dimension_semantics=("arbitrary",), vmem_limit_bytes=40 * MIB),
        name="moe_dispatch",
    )(fill_start, _by_tile(dest, tile), xn.reshape(t // SUBLANES, SUBLANES, d))


def _expert_kernel(be_ref, nused_ref, wslot_ref, wnext_ref, xs_hbm, wg_hbm, bg_ref, wu_hbm, bu_ref,
                   wd_hbm, bd_ref, y_hbm, xbuf, ybuf, x_sem, y_sem, wg32, wu32, wd32, w_sem,
                   wg_bf, wu_bf, wd_bf, *, rows, n_exp):
    i = pl.program_id(0)
    n = nused_ref[0]

    def w_copies(e, slot):
        return [pltpu.make_async_copy(src.at[e], dst.at[slot], w_sem.at[slot])
                for src, dst in ((wg_hbm, wg32), (wu_hbm, wu32), (wd_hbm, wd32))]

    def x_copy(b):
        return pltpu.make_async_copy(
            xs_hbm.at[pl.ds(pl.multiple_of(b * rows, rows), rows), 0, :], xbuf.at[b % 2],
            x_sem.at[b % 2])

    def y_copy(b):
        return pltpu.make_async_copy(
            ybuf.at[b % 2], y_hbm.at[pl.ds(pl.multiple_of(b * rows, rows), rows), 0, :],
            y_sem.at[b % 2])

    @pl.when(i < n)
    def _():
        @pl.when(i == 0)
        def _():
            x_copy(0).start()
            for cp in w_copies(be_ref[0], wslot_ref[0]):
                cp.start()

        @pl.when(i + 1 < n)
        def _():
            x_copy(i + 1).start()

        changed = jnp.logical_or(i == 0, be_ref[i] != be_ref[jnp.maximum(i - 1, 0)])

        @pl.when(changed)
        def _():
            slot = wslot_ref[i]
            for cp in w_copies(be_ref[i], slot):
                cp.wait()

            @pl.when(wnext_ref[i] < n_exp)
            def _():
                for cp in w_copies(wnext_ref[i], 1 - slot):
                    cp.start(priority=1)
            wg_bf[...] = wg32[slot].astype(BF16)
            wu_bf[...] = wu32[slot].astype(BF16)
            wd_bf[...] = wd32[slot].astype(BF16)

        @pl.when(i >= 2)
        def _():
            y_copy(i - 2).wait()

        x_copy(i).wait()
        x = xbuf[i % 2].astype(BF16)
        gate = jnp.dot(x, wg_bf[...], preferred_element_type=F32) + bg_ref[...]
        up = jnp.dot(x, wu_bf[...], preferred_element_type=F32) + bu_ref[...]
        gate = jnp.minimum(gate, SWIGLU_LIMIT)
        up = jnp.clip(up, -SWIGLU_LIMIT, SWIGLU_LIMIT)
        glu = gate * (1.0 / (1.0 + jnp.exp(-SWIGLU_ALPHA * gate)))
        hidden = ((up + 1.0) * glu).astype(BF16)
        ybuf[i % 2] = jnp.dot(hidden, wd_bf[...], preferred_element_type=F32) + bd_ref[...]
        y_copy(i).start()

        @pl.when(i == n - 1)
        def _():
            y_copy(i).wait()

            @pl.when(i >= 1)
            def _():
                y_copy(i - 1).wait()

            def zero_copy(b):
                return pltpu.make_async_copy(
                    ybuf.at[0], y_hbm.at[pl.ds(pl.multiple_of(b * rows, rows), rows), 0, :],
                    y_sem.at[0])

            def start_zero(b, carry):
                zero_copy(b).start()
                return carry

            def wait_zero(b, carry):
                zero_copy(b).wait()
                return carry

            ybuf[0] = jnp.zeros((rows, ybuf.shape[2]), F32)
            lax.fori_loop(n, pl.num_programs(0), start_zero, 0)
            lax.fori_loop(n, pl.num_programs(0), wait_zero, 0)


def _experts(x_sorted, block_plan, n_blocks, w_gate, b_gate, w_up, b_up, w_down, b_down):
    n_rows, _, d = x_sorted.shape
    n_exp, _, f = w_gate.shape
    rows = EXPERT_ROWS
    bmap = lambda i, be, nu, ws, wn: (be[i], 0, 0)
    hbm = pl.BlockSpec(memory_space=pl.ANY)
    grid_spec = pltpu.PrefetchScalarGridSpec(
        num_scalar_prefetch=4,
        grid=(n_blocks,),
        in_specs=[
            hbm,
            hbm, pl.BlockSpec((None, 1, f), bmap),
            hbm, pl.BlockSpec((None, 1, f), bmap),
            hbm, pl.BlockSpec((None, 1, d), bmap),
        ],
        out_specs=hbm,
        scratch_shapes=[
            pltpu.VMEM((2, rows, d), F32),
            pltpu.VMEM((2, rows, d), F32),
            pltpu.SemaphoreType.DMA((2,)),
            pltpu.SemaphoreType.DMA((2,)),
            pltpu.VMEM((2, d, f), F32),
            pltpu.VMEM((2, d, f), F32),
            pltpu.VMEM((2, f, d), F32),
            pltpu.SemaphoreType.DMA((2,)),
            pltpu.VMEM((d, f), BF16),
            pltpu.VMEM((d, f), BF16),
            pltpu.VMEM((f, d), BF16),
        ],
    )
    return pl.pallas_call(
        functools.partial(_expert_kernel, rows=rows, n_exp=n_exp),
        grid_spec=grid_spec,
        out_shape=jax.ShapeDtypeStruct((n_rows, 1, d), F32),
        compiler_params=pltpu.CompilerParams(
            dimension_semantics=("arbitrary",), vmem_limit_bytes=52 * MIB),
        name="moe_experts",
    )(*block_plan, x_sorted,
      w_gate, b_gate.reshape(n_exp, 1, f), w_up, b_up.reshape(n_exp, 1, f),
      w_down, b_down.reshape(n_exp, 1, d))


def _combine_kernel(dest_hbm, y_hbm, x1_ref, gate_ref, lnf_ref, o_ref, idx_smem, ybuf,
                    idx_sem, row_sem, *, tile):
    i = pl.program_id(0)
    n = pl.num_programs(0)
    per_tile = TOP_K * tile

    def slot_base(b):
        return pl.multiple_of((b % 3) * per_tile, per_tile)

    def idx_copy(b):
        src = dest_hbm.at[pl.ds(pl.multiple_of(b * per_tile, per_tile), per_tile)]
        return pltpu.make_async_copy(src, idx_smem.at[pl.ds(slot_base(b), per_tile)],
                                     idx_sem.at[b % 3])

    row_groups = per_tile // SUBLANES

    def gather(b, first, count):
        base = slot_base(b)
        buf = ybuf.at[b % 2]
        sem = row_sem.at[b % 2]

        def issue(g, carry):
            for s in range(SUBLANES):
                pltpu.make_async_copy(y_hbm.at[idx_smem[base + g * SUBLANES + s]],
                                      buf.at[g, pl.ds(s, 1), :], sem).start(priority=s % 2)
            return carry

        lax.fori_loop(first, first + count, issue, 0)

    @pl.when(i == 0)
    def _():
        idx_copy(0).start()
        idx_copy(0).wait()
        gather(0, 0, row_groups)

        @pl.when(n > 1)
        def _():
            idx_copy(1).start()

    @pl.when(i + 1 < n)
    def _():
        idx_copy(i + 1).wait()

        @pl.when(i + 2 < n)
        def _():
            idx_copy(i + 2).start()

    cur = i % 2
    pltpu.make_async_copy(ybuf.at[1 - cur], ybuf.at[cur], row_sem.at[cur]).wait()
    groups = tile // SUBLANES
    piece_groups = groups // COMBINE_PIECES
    piece_rows = piece_groups * SUBLANES

    def piece(c, carry):
        @pl.when(i + 1 < n)
        def _():
            gather(i + 1, c * (row_groups // COMBINE_PIECES), row_groups // COMBINE_PIECES)

        rows = pl.ds(pl.multiple_of(c * piece_rows, piece_rows), piece_rows)
        x = x1_ref[rows, :]
        for k in range(TOP_K):
            yk = ybuf[cur, pl.ds(k * groups + c * piece_groups, piece_groups)]
            x = x + gate_ref[rows, k:k + 1] * yk.reshape(x.shape)
        o_ref[rows, :] = _rms(x, lnf_ref[...])
        return carry

    lax.fori_loop(0, COMBINE_PIECES, piece, 0)


def _combine(dest, y_rows, x1, gates, ln_f_g):
    t, d = x1.shape
    tile = min(COMBINE_TILE, t)
    n_tiles = t // tile
    dest_tiles = _by_tile(dest, tile)
    return pl.pallas_call(
        functools.partial(_combine_kernel, tile=tile),
        grid=(n_tiles,),
        in_specs=[
            pl.BlockSpec(memory_space=pl.ANY),
            pl.BlockSpec(memory_space=pl.ANY),
            pl.BlockSpec((tile, d), lambda i: (i, 0)),
            pl.BlockSpec((tile, TOP_K), lambda i: (i, 0)),
            pl.BlockSpec((1, d), lambda i: (0, 0)),
        ],
        out_specs=pl.BlockSpec((tile, d), lambda i: (i, 0)),
        out_shape=jax.ShapeDtypeStruct((t, d), F32),
        scratch_shapes=[
            pltpu.SMEM((3 * TOP_K * tile,), I32),
            pltpu.VMEM((2, TOP_K * tile // SUBLANES, SUBLANES, d), F32),
            pltpu.SemaphoreType.DMA((3,)),
            pltpu.SemaphoreType.DMA((2,)),
        ],
        compiler_params=pltpu.CompilerParams(
            dimension_semantics=("arbitrary",), vmem_limit_bytes=40 * MIB),
        name="moe_combine",
    )(dest_tiles, y_rows, x1, gates, ln_f_g.reshape(1, d))


def kernel(x, ln1_g, w_in, lam_re, lam_im, log_dt, ssm_b_re, ssm_b_im, ssm_c_re, ssm_c_im,
           ssm_d, w_glu, g_sb, g_ssm, w_out, ln2_g, w_router, b_router, w_gate, b_gate,
           w_up, b_up, w_down, b_down, ln_f_g):
    bsz, seq, d = x.shape
    assert ln1_g.shape[0] == 1, "depth-1 block only"
    ssm_w = ssm_d.shape[1]
    sb = g_sb.shape[1]
    x2 = x.reshape(bsz * seq, d)
    q, k, v, u = _in_proj(x2, ln1_g[0], w_in[0], sb, ssm_w)
    y_sb = _attention(q, k, v, bsz, seq)
    y_ss = _ssm(u, bsz, seq, lam_re[0], lam_im[0], log_dt[0], ssm_b_re[0], ssm_b_im[0],
                ssm_c_re[0], ssm_c_im[0])
    x1, xn, top_idx, gates, rank, counts = _post(
        x2, y_sb, y_ss, u, ssm_d[0], w_glu[0], g_sb[0], g_ssm[0], w_out[0], ln2_g[0],
        w_router[0], b_router[0])
    dest, block_plan, fill_start, n_blocks = _plan(top_idx[:TOP_K], rank[:TOP_K], counts,
                                                   EXPERT_ROWS)
    x_sorted = _dispatch(xn, dest, fill_start, n_blocks * EXPERT_ROWS)
    y_rows = _experts(x_sorted, block_plan, n_blocks, w_gate[0], b_gate[0],
                      w_up[0], b_up[0], w_down[0], b_down[0])
    out = _combine(dest, y_rows, x1, gates, ln_f_g)
    return out.reshape(bsz, seq, d)
```

```python
import functools
import math

import jax
import jax.numpy as jnp
from jax import lax
from jax.experimental import pallas as pl
from jax.experimental.pallas import tpu as pltpu

F32 = jnp.float32
BF16 = jnp.bfloat16
I32 = jnp.int32

EPS = 1e-5
SB_HEAD_DIM = 64
SSM_GROUP = 16
SSM_STATE = 64
TOP_K = 4
SWIGLU_LIMIT = 7.0
SWIGLU_ALPHA = 1.702

LANES = 128
SUBLANES = 8
HEADS_PER_BLOCK = LANES // SB_HEAD_DIM
ATTN_BLOCK = 256
ATTN_EXP_FLOOR = 160.0
ATTN_STREAMS = 4
SSM_CHUNK = 16
SSM_FOLD = 8
TOKEN_TILE = 512
EXPERT_ROWS = 256
COMBINE_TILE = 512
DISPATCH_TILE = 2048
MIB = 1024 * 1024
LOG2E = 1.4426950408889634


def _rms(x, g):
    return x * lax.rsqrt(jnp.mean(x * x, axis=-1, keepdims=True) + EPS) * g


def _in_proj_kernel(x_ref, g_ref, w_ref, q_ref, k_ref, v_ref, u_ref, w_bf, *, sb, scale):
    @pl.when(pl.program_id(0) == 0)
    def _():
        w_bf[...] = w_ref[...].astype(BF16)

    h = _rms(x_ref[...], g_ref[...])
    proj = jnp.dot(h.astype(BF16), w_bf[...], preferred_element_type=F32)
    k_ref[...] = proj[:, sb:2 * sb].astype(BF16)
    u_ref[...] = proj[:, 3 * sb:]
    lane = lax.broadcasted_iota(I32, (1, LANES), 1)
    q = (proj[:, :sb] * scale).astype(BF16)
    v = proj[:, 2 * sb:3 * sb].astype(BF16)
    zero = jnp.zeros((), BF16)
    for pair in range(sb // LANES):
        cols = slice(pair * LANES, (pair + 1) * LANES)
        for head in range(HEADS_PER_BLOCK):
            own = (lane // SB_HEAD_DIM) == head
            out = slice((pair * HEADS_PER_BLOCK + head) * LANES,
                        (pair * HEADS_PER_BLOCK + head + 1) * LANES)
            q_ref[:, out] = jnp.where(own, q[:, cols], zero)
            v_ref[:, out] = jnp.where(own, v[:, cols], zero)


def _in_proj(x2, ln1_g, w_in, sb, ssm_w):
    t, d = x2.shape
    n_in = w_in.shape[1]
    tm = min(TOKEN_TILE, t)
    scale = LOG2E / math.sqrt(SB_HEAD_DIM)
    return pl.pallas_call(
        functools.partial(_in_proj_kernel, sb=sb, scale=scale),
        grid=(t // tm,),
        in_specs=[
            pl.BlockSpec((tm, d), lambda i: (i, 0)),
            pl.BlockSpec((1, d), lambda i: (0, 0)),
            pl.BlockSpec((d, n_in), lambda i: (0, 0)),
        ],
        out_specs=[
            pl.BlockSpec((tm, HEADS_PER_BLOCK * sb), lambda i: (i, 0)),
            pl.BlockSpec((tm, sb), lambda i: (i, 0)),
            pl.BlockSpec((tm, HEADS_PER_BLOCK * sb), lambda i: (i, 0)),
            pl.BlockSpec((tm, ssm_w), lambda i: (i, 0)),
        ],
        out_shape=[
            jax.ShapeDtypeStruct((t, HEADS_PER_BLOCK * sb), BF16),
            jax.ShapeDtypeStruct((t, sb), BF16),
            jax.ShapeDtypeStruct((t, HEADS_PER_BLOCK * sb), BF16),
            jax.ShapeDtypeStruct((t, ssm_w), F32),
        ],
        scratch_shapes=[pltpu.VMEM((d, n_in), BF16)],
        compiler_params=pltpu.CompilerParams(
            dimension_semantics=("arbitrary",), vmem_limit_bytes=48 * MIB),
        name="in_proj",
    )(x2, ln1_g.reshape(1, d), w_in)


def _attn_kernel(ti_ref, tj_ref, tn_ref, q_ref, k_ref, v_ref, tri_ref, mask_ref, o_ref,
                 dbuf, spbuf, lbuf, rsbuf, acc_ref, r_ref, *, blk, n_items, streams):
    sign_bit = jnp.uint32(0x80000000)
    last = n_items - 1
    ALL, CAUSAL, NONE = 0, 1, 2

    for ref in (dbuf, spbuf, lbuf, rsbuf, acc_ref, r_ref):
        ref[...] = jnp.zeros_like(ref)

    def both_heads(ref, start):
        rows = pl.ds(pl.multiple_of(start, blk), blk)
        return jnp.concatenate([ref[rows, h * LANES:(h + 1) * LANES]
                                for h in range(HEADS_PER_BLOCK)], axis=0)

    def item(idx):
        c = jnp.clip(idx, 0, last)
        return ti_ref[c], tj_ref[c], tn_ref[c]

    n_streams = len(streams)
    stream_end = [end for _, end in streams]

    def step(state, x):
        cur, p1, p2 = list(state[:n_streams]), state[n_streams], state[n_streams + 1]
        out_blk = list(state[n_streams + 2:])
        slot, other = x % 2, 1 - x % 2
        x3 = (x - 2) % n_streams
        c1 = jnp.where(cur[x] < stream_end[x], cur[x], n_items)
        cur[x] = jnp.minimum(cur[x] + 1, stream_end[x])
        c3 = p2
        i1, j1, _ = item(c1)
        ks = k_ref[pl.ds(pl.multiple_of(j1 * blk, blk), blk), :]
        w = lax.dot_general(both_heads(q_ref, i1 * blk), ks, (((1,), (1,)), ((), ())),
                            preferred_element_type=F32)
        i3, j3, next_block = item(c3)
        valid = c3 <= last
        first = jnp.logical_and(i3 == j3, valid)
        mask3 = mask_ref[jnp.where(valid, (i3 == j3).astype(I32), NONE)]
        i3 = jnp.where(valid, i3, out_blk[x3])
        r_prev = jnp.where(first, 0.0, r_ref[x3])
        r_new = r_prev + rsbuf[other]
        r_ref[x3] = r_new
        done = jnp.logical_and(valid, jnp.min(r_new) >= ATTN_EXP_FLOOR)
        cur[x3] = jnp.where(done, jnp.maximum(cur[x3], next_block), cur[x3])
        out_blk[x3] = i3
        r_wide = jnp.concatenate([r_prev] * (blk // LANES), axis=1)
        a = jnp.exp2(dbuf[slot] - (lbuf[other] + r_wide))
        ab = a.astype(BF16) * jnp.concatenate([mask3, mask3], axis=0)
        a_cat = jnp.concatenate([ab[:blk], ab[blk:]], axis=1)
        acc = jnp.where(first, 0.0, acc_ref[x3]) + jnp.dot(
            a_cat, both_heads(v_ref, j3 * blk), preferred_element_type=F32)
        acc_ref[x3] = acc
        o_ref[pl.ds(pl.multiple_of(i3 * blk, blk), blk), :] = acc.astype(o_ref.dtype)
        sums = jnp.dot(spbuf[other], tri_ref[...], preferred_element_type=F32)
        lbuf[slot] = sums[:, :blk]
        rsbuf[slot] = sums[:, blk:]
        mask1 = mask_ref[(i1 == j1).astype(I32)]
        neg_abs = lax.bitcast_convert_type(
            lax.bitcast_convert_type(w, jnp.uint32) | sign_bit, F32)
        sp2 = jnp.maximum(w, 0.0) + jnp.log(1.0 + jnp.exp2(neg_abs)) * LOG2E
        dbuf[slot] = w - sp2
        spbuf[slot] = sp2.astype(BF16) * jnp.concatenate([mask1, mask1], axis=0)
        return (*cur, c1, p1, *out_blk)

    def in_flight(state):
        p1, p2 = state[n_streams], state[n_streams + 1]
        live = [state[x] < stream_end[x] for x in range(n_streams)]
        return functools.reduce(jnp.logical_or, live + [p1 <= last, p2 <= last])

    def one_round(state):
        for x in range(n_streams):
            state = step(state, x)
        return state

    none = jnp.int32(n_items)
    first_items = [jnp.int32(start) for start, _ in streams]
    first_blocks = [ti_ref[start] for start, _ in streams]
    lax.while_loop(in_flight, one_round, (*first_items, none, none, *first_blocks))


def _attention(q, k, v, bsz, seq):
    sb = k.shape[-1]
    blk = min(ATTN_BLOCK, seq)
    n_blk = seq // blk
    n_pairs = sb // LANES
    q3, k3, v3 = (a.reshape(bsz, seq, a.shape[-1]) for a in (q, k, v))
    per_head = HEADS_PER_BLOCK * LANES
    jj = lax.broadcasted_iota(I32, (blk, blk), 0)
    ss = lax.broadcasted_iota(I32, (blk, blk), 1)
    tri = jnp.concatenate([(jj > ss).astype(BF16), jnp.ones((blk, LANES), BF16)], axis=1)
    masks = jnp.stack([jnp.ones((blk, blk), BF16), (ss < jj).astype(BF16),
                       jnp.zeros((blk, blk), BF16)])
    n_streams = ATTN_STREAMS if n_blk % ATTN_STREAMS == 0 else 2
    assert n_blk % n_streams == 0, (n_blk, n_streams)
    items, next_block, streams = [], [], []
    for x in range(n_streams):
        start = len(items)
        for i in range(x, n_blk, n_streams):
            block_end = len(items) + i + 1
            items += [(i, j) for j in range(i, -1, -1)]
            next_block += [block_end] * (i + 1)
        streams.append((start, len(items)))
    item_i = jnp.asarray([i for i, _ in items], I32)
    item_j = jnp.asarray([j for _, j in items], I32)
    item_next = jnp.asarray(next_block, I32)
    whole = lambda b, p, ti, tj, tn: (b, 0, p)
    grid_spec = pltpu.PrefetchScalarGridSpec(
        num_scalar_prefetch=3,
        grid=(bsz, n_pairs),
        in_specs=[
            pl.BlockSpec((None, seq, per_head), whole),
            pl.BlockSpec((None, seq, LANES), whole),
            pl.BlockSpec((None, seq, per_head), whole),
            pl.BlockSpec((blk, blk + LANES), lambda b, p, ti, tj, tn: (0, 0)),
            pl.BlockSpec((3, blk, blk), lambda b, p, ti, tj, tn: (0, 0, 0)),
        ],
        out_specs=pl.BlockSpec((None, seq, LANES), whole),
        scratch_shapes=[
            pltpu.VMEM((2, HEADS_PER_BLOCK * blk, blk), F32),
            pltpu.VMEM((2, HEADS_PER_BLOCK * blk, blk), BF16),
            pltpu.VMEM((2, HEADS_PER_BLOCK * blk, blk), F32),
            pltpu.VMEM((2, HEADS_PER_BLOCK * blk, LANES), F32),
            pltpu.VMEM((n_streams, blk, LANES), F32),
            pltpu.VMEM((n_streams, HEADS_PER_BLOCK * blk, LANES), F32),
        ],
    )
    out = pl.pallas_call(
        functools.partial(_attn_kernel, blk=blk, n_items=len(items), streams=tuple(streams)),
        grid_spec=grid_spec,
        out_shape=jax.ShapeDtypeStruct((bsz, seq, sb), BF16),
        compiler_params=pltpu.CompilerParams(
            dimension_semantics=("arbitrary", "arbitrary"), vmem_limit_bytes=40 * MIB),
        name="sb_attention",
    )(item_i, item_j, item_next, q3, k3, v3, tri, masks)
    return out.reshape(bsz * seq, sb)


def _ssm_params(lam_re, lam_im, log_dt, b_re, b_im, c_re, c_im, chunk, n_chunks):
    lam = lax.complex(lam_re.astype(F32), lam_im.astype(F32))
    dt = jnp.exp(log_dt.astype(F32))[:, None]
    lam_dt = lam * dt
    lam_bar = jnp.exp(lam_dt)
    b_bar = ((lam_bar - 1.0) / lam)[:, :, None] * lax.complex(b_re.astype(F32), b_im.astype(F32))
    b_t = jnp.swapaxes(b_bar, 1, 2)
    c_mat = lax.complex(c_re.astype(F32), c_im.astype(F32))
    steps = jnp.arange(chunk + 1, dtype=F32)
    pw = jnp.exp(lam_dt[:, None, :] * steps[None, :, None])
    npw = jnp.exp(-lam_dt[:, None, :] * steps[None, :chunk, None])

    def halves(lo, hi):
        return jnp.concatenate([lo, hi], axis=-1)

    def b_side(z):
        return [halves(jnp.real(z), jnp.real(z)), halves(-jnp.imag(z), jnp.imag(z))]

    def c_side(z):
        return [halves(jnp.real(z), jnp.imag(z)), halves(jnp.imag(z), jnp.real(z))]

    time_terms = jnp.stack(
        b_side(npw)
        + b_side(pw[:, chunk - 1::-1])
        + c_side(pw[:, :chunk])
        + c_side(pw[:, 1:]), axis=1)
    chan_terms = jnp.stack(
        [halves(jnp.real(b_t), jnp.imag(b_t)), halves(jnp.imag(b_t), jnp.real(b_t)),
         halves(jnp.real(c_mat), -jnp.real(c_mat)), halves(-jnp.imag(c_mat), -jnp.imag(c_mat))],
        axis=1)
    n_steps = max(1, (n_chunks - 1).bit_length())
    powers = []
    cur = pw[:, chunk, :]
    for _ in range(n_steps):
        powers.append(cur)
        cur = cur * cur
    lam_pow = jnp.stack(powers, axis=1)
    a1 = halves(jnp.real(lam_pow), jnp.real(lam_pow))
    a2 = halves(-jnp.imag(lam_pow), jnp.imag(lam_pow))
    return time_terms, chan_terms, a1, a2


def _ssm_kernel(u_ref, time_ref, chan_ref, a1_ref, a2_ref, perm_ref, perm_t_ref, y_ref,
                u8_ref, y8_ref, toep_ref, *, n_chunks, n_steps, group):
    seq = u_ref.shape[0]
    tiles = seq // SSM_FOLD
    lc = toep_ref.shape[0]
    per_chunk = (lc // group) // SSM_FOLD
    p2 = a1_ref.shape[2]
    group_shift = group.bit_length() - 1
    nt = (((1,), (1,)), ((), ()))

    folded = jnp.concatenate(
        [u_ref[pl.ds(s, tiles, stride=SSM_FOLD), :] for s in range(SSM_FOLD)], axis=1)
    regrouped = jnp.dot(folded.astype(BF16), perm_ref[...], preferred_element_type=F32)
    n_groups = LANES // group
    for g in range(n_groups):
        u8_ref[g] = regrouped[:, g * LANES:(g + 1) * LANES]

    for g in range(n_groups):
        def table(t1, t2, v1, v2):
            full = (time_ref[g, t1][:, None, :] * chan_ref[g, v1][None, :, :]
                    + time_ref[g, t2][:, None, :] * chan_ref[g, v2][None, :, :])
            return full.reshape(lc, p2).astype(BF16)

        src = table(0, 1, 0, 1)
        to_state = table(2, 3, 0, 1)
        dst = table(4, 5, 2, 3)
        from_state = table(6, 7, 2, 3)
        cb = min(256, lc)
        for j in range(lc // cb):
            blk = lax.dot_general(src, dst[j * cb:(j + 1) * cb], nt, preferred_element_type=F32)
            s_idx = lax.broadcasted_iota(I32, (lc, cb), 0) >> group_shift
            t_idx = (lax.broadcasted_iota(I32, (lc, cb), 1) + j * cb) >> group_shift
            toep_ref[:, j * cb:(j + 1) * cb] = jnp.where(s_idx <= t_idx, blk, 0.0).astype(BF16)
        u = jnp.concatenate(
            [u8_ref[g, pl.ds(tau, n_chunks, stride=per_chunk), :] for tau in range(per_chunk)],
            axis=1).astype(BF16)
        y = jnp.dot(u, toep_ref[...], preferred_element_type=F32)
        z = jnp.dot(u, to_state, preferred_element_type=F32)
        n = lax.broadcasted_iota(I32, z.shape, 0)
        x = jnp.where(n >= 1, pltpu.roll(z, 1, 0), 0.0)
        for k in range(n_steps):
            sh = 1 << k
            xs = jnp.where(n >= sh, pltpu.roll(x, sh, 0), 0.0)
            x = (x + a1_ref[g, k:k + 1, :] * xs
                 + a2_ref[g, k:k + 1, :] * pltpu.roll(xs, p2 // 2, 1))
        y = y + lax.dot_general(x.astype(BF16), from_state, nt, preferred_element_type=F32)
        for tau in range(per_chunk):
            y8_ref[g, pl.ds(tau, n_chunks, stride=per_chunk), :] = y[:, tau * LANES:(tau + 1) * LANES]

    y8 = jnp.concatenate([y8_ref[g] for g in range(n_groups)], axis=1)
    unfolded = jnp.dot(y8.astype(BF16), perm_t_ref[...], preferred_element_type=F32)
    for s in range(SSM_FOLD):
        y_ref[pl.ds(s, tiles, stride=SSM_FOLD), :] = unfolded[:, s * LANES:(s + 1) * LANES]


def _ssm(u, bsz, seq, lam_re, lam_im, log_dt, b_re, b_im, c_re, c_im):
    g, p = lam_re.shape
    c = b_re.shape[-1]
    chunk = min(SSM_CHUNK, seq)
    n_chunks = seq // chunk
    assert n_chunks & (n_chunks - 1) == 0 and c & (c - 1) == 0, (n_chunks, c)
    assert SSM_FOLD * c == LANES and chunk % SSM_FOLD == 0
    lc = chunk * c
    per_block = LANES // c
    time_terms, chan_terms, a1, a2 = _ssm_params(
        lam_re, lam_im, log_dt, b_re, b_im, c_re, c_im, chunk, n_chunks)
    n_steps = a1.shape[1]
    src_idx = jnp.arange(SSM_FOLD * LANES, dtype=I32)
    s_of, g_of, c_of = src_idx // LANES, (src_idx % LANES) // c, src_idx % c
    dst_idx = g_of * LANES + s_of * c + c_of
    perm = (dst_idx[:, None] == src_idx[None, :]).astype(BF16)
    block = lambda cb, b: (cb, 0, 0)
    fixed = lambda cb, b: (0, 0)
    y = pl.pallas_call(
        functools.partial(_ssm_kernel, n_chunks=n_chunks, n_steps=n_steps, group=c),
        grid=(g // per_block, bsz),
        in_specs=[
            pl.BlockSpec((None, seq, LANES), lambda cb, b: (b, 0, cb)),
            pl.BlockSpec((per_block,) + time_terms.shape[1:], lambda cb, b: (cb, 0, 0, 0)),
            pl.BlockSpec((per_block,) + chan_terms.shape[1:], lambda cb, b: (cb, 0, 0, 0)),
            pl.BlockSpec((per_block, n_steps, 2 * p), block),
            pl.BlockSpec((per_block, n_steps, 2 * p), block),
            pl.BlockSpec((SSM_FOLD * LANES, SSM_FOLD * LANES), fixed),
            pl.BlockSpec((SSM_FOLD * LANES, SSM_FOLD * LANES), fixed),
        ],
        out_specs=pl.BlockSpec((None, seq, LANES), lambda cb, b: (b, 0, cb)),
        out_shape=jax.ShapeDtypeStruct((bsz, seq, g * c), F32),
        scratch_shapes=[
            pltpu.VMEM((per_block, seq // SSM_FOLD, LANES), F32),
            pltpu.VMEM((per_block, seq // SSM_FOLD, LANES), F32),
            pltpu.VMEM((lc, lc), BF16),
        ],
        compiler_params=pltpu.CompilerParams(
            dimension_semantics=("arbitrary", "arbitrary"), vmem_limit_bytes=56 * MIB),
        name="s5_chunked_scan",
    )(u.reshape(bsz, seq, g * c), time_terms, chan_terms, a1, a2, perm, perm.T)
    return y.reshape(bsz * seq, g * c)


def _post_kernel(x_ref, ysb_ref, yss_ref, u_ref, d_ref, wglu_ref, gsb_ref, gssm_ref, wout_ref,
                 ln2_ref, wrt_ref, br_ref, tri_ref, x1_ref, xn_ref, idx_ref, gate_ref, rank_ref,
                 count_ref, running_ref, wglu_bf, wout_bf, *, sb, n_exp):
    @pl.when(pl.program_id(0) == 0)
    def _():
        running_ref[...] = jnp.zeros_like(running_ref)
        wglu_bf[...] = wglu_ref[...].astype(BF16)
        wout_bf[...] = wout_ref[...].astype(BF16)

    u = u_ref[...].astype(F32)
    y = yss_ref[...].astype(F32) + d_ref[...] * u
    y = y * (0.5 * (1.0 + jnp.tanh(math.sqrt(2.0 / math.pi) * (y + 0.044715 * (y * y * y)))))
    ab = jnp.dot(y.astype(BF16), wglu_bf[...], preferred_element_type=F32)
    w = ab.shape[1] // 2
    y_ssm = ab[:, :w] * (1.0 / (1.0 + jnp.exp(-ab[:, w:])))
    m_sb = _rms(ysb_ref[...].astype(F32), gsb_ref[...])
    m_ssm = _rms(y_ssm, gssm_ref[...])
    x1 = (x_ref[...]
          + jnp.dot(m_sb.astype(BF16), wout_bf[:sb, :], preferred_element_type=F32)
          + jnp.dot(m_ssm.astype(BF16), wout_bf[sb:, :], preferred_element_type=F32))
    x1_ref[...] = x1
    xn = _rms(x1, ln2_ref[...])
    xn_ref[...] = xn
    def split(a):
        hi = a.astype(BF16)
        return hi, (a - hi.astype(F32)).astype(BF16)

    xn_hi, xn_lo = split(xn)
    wt_hi, wt_lo = split(wrt_ref[...])
    nt = (((1,), (1,)), ((), ()))
    both = lax.dot_general(jnp.concatenate([wt_hi, wt_lo], axis=0), xn_hi, nt,
                           preferred_element_type=F32)
    low = lax.dot_general(wt_hi, xn_lo, nt, preferred_element_type=F32)
    logits = (both[:n_exp] + both[n_exp:] + low) + br_ref[...]
    tokens = logits.shape[1]
    expert = lax.broadcasted_iota(I32, logits.shape, 0).astype(F32)
    out_row = lax.broadcasted_iota(I32, (SUBLANES, tokens), 0)
    idx_out = jnp.zeros((SUBLANES, tokens), F32)
    val_out = jnp.zeros((SUBLANES, tokens), F32)
    top = None
    denom = None
    work = logits
    chosen = []
    for k in range(TOP_K):
        m = jnp.max(work, axis=0, keepdims=True)
        sel = jnp.min(jnp.where(work == m, expert, float(n_exp)), axis=0, keepdims=True)
        hit = expert == sel
        chosen.append(hit)
        work = jnp.where(hit, -jnp.inf, work)
        if k == 0:
            top = m
        e = jnp.exp(m - top)
        denom = e if denom is None else denom + e
        idx_out = jnp.where(out_row == k, sel, idx_out)
        val_out = jnp.where(out_row == k, e, val_out)
    idx_ref[...] = idx_out.astype(I32)
    gates = jnp.concatenate([val_out / denom, jnp.zeros((LANES - SUBLANES, tokens), F32)], axis=0)
    gate_ref[...] = jnp.transpose(gates)[:, :TOP_K]
    member = functools.reduce(jnp.logical_or, chosen).astype(BF16)
    before = running_ref[...] + jnp.dot(member, tri_ref[...], preferred_element_type=F32)
    rank_out = jnp.zeros((SUBLANES, tokens), F32)
    for k in range(TOP_K):
        rk = jnp.sum(jnp.where(chosen[k], before, 0.0), axis=0, keepdims=True)
        rank_out = jnp.where(out_row == k, rk, rank_out)
    rank_ref[...] = rank_out.astype(I32)
    running = running_ref[...] + jnp.sum(member.astype(F32), axis=1, keepdims=True)
    running_ref[...] = running
    count_ref[...] = running


def _post(x2, y_sb, y_ss, u, ssm_d, w_glu, g_sb, g_ssm, w_out, ln2_g, w_router, b_router):
    t, d = x2.shape
    sb = y_sb.shape[1]
    w = y_ss.shape[1]
    n_exp = w_router.shape[1]
    tm = min(TOKEN_TILE, t)
    row = lambda i: (i, 0)
    fixed = lambda i: (0, 0)
    earlier = (lax.broadcasted_iota(I32, (tm, tm), 0)
               < lax.broadcasted_iota(I32, (tm, tm), 1)).astype(BF16)
    return pl.pallas_call(
        functools.partial(_post_kernel, sb=sb, n_exp=n_exp),
        grid=(t // tm,),
        in_specs=[
            pl.BlockSpec((tm, d), row),
            pl.BlockSpec((tm, sb), row),
            pl.BlockSpec((tm, w), row),
            pl.BlockSpec((tm, w), row),
            pl.BlockSpec((1, w), fixed),
            pl.BlockSpec((w, 2 * w), fixed),
            pl.BlockSpec((1, sb), fixed),
            pl.BlockSpec((1, w), fixed),
            pl.BlockSpec((sb + w, d), fixed),
            pl.BlockSpec((1, d), fixed),
            pl.BlockSpec((n_exp, d), fixed),
            pl.BlockSpec((n_exp, 1), fixed),
            pl.BlockSpec((tm, tm), fixed),
        ],
        out_specs=[
            pl.BlockSpec((tm, d), row),
            pl.BlockSpec((tm, d), row),
            pl.BlockSpec((SUBLANES, tm), lambda i: (0, i)),
            pl.BlockSpec((tm, TOP_K), row),
            pl.BlockSpec((SUBLANES, tm), lambda i: (0, i)),
            pl.BlockSpec((n_exp, 1), fixed),
        ],
        out_shape=[
            jax.ShapeDtypeStruct((t, d), F32),
            jax.ShapeDtypeStruct((t, d), F32),
            jax.ShapeDtypeStruct((SUBLANES, t), I32),
            jax.ShapeDtypeStruct((t, TOP_K), F32),
            jax.ShapeDtypeStruct((SUBLANES, t), I32),
            jax.ShapeDtypeStruct((n_exp, 1), F32),
        ],
        scratch_shapes=[pltpu.VMEM((n_exp, 1), F32), pltpu.VMEM((w, 2 * w), BF16),
                        pltpu.VMEM((sb + w, d), BF16)],
        compiler_params=pltpu.CompilerParams(
            dimension_semantics=("arbitrary",), vmem_limit_bytes=48 * MIB),
        name="post_mixer_router",
    )(x2, y_sb, y_ss, u, ssm_d.reshape(1, w), w_glu, g_sb.reshape(1, sb),
      g_ssm.reshape(1, w), w_out, ln2_g.reshape(1, d), w_router.T,
      b_router.reshape(n_exp, 1), earlier)


def _plan(top_idx, rank, counts, rows_per_block):
    k, t = top_idx.shape
    n_exp = counts.shape[0]
    counts = counts.reshape(n_exp).astype(I32)
    padded = ((counts + rows_per_block - 1) // rows_per_block) * rows_per_block
    pad_ends = jnp.cumsum(padded)
    pad_starts = pad_ends - padded
    start_of = jnp.sum(jnp.where(top_idx[None] == jnp.arange(n_exp, dtype=I32)[:, None, None],
                                 pad_starts[:, None, None], 0), axis=0)
    dest = (start_of + rank).astype(I32)
    n_rows = t * k + n_exp * rows_per_block
    n_blocks = n_rows // rows_per_block
    block_start = jnp.arange(n_blocks, dtype=I32) * rows_per_block
    block_expert = jnp.minimum(
        jnp.sum((pad_ends[None, :] <= block_start[:, None]).astype(I32), axis=1), n_exp - 1)
    n_used = (pad_ends[-1] // rows_per_block).astype(I32).reshape(1)
    fill_start = jnp.concatenate([jnp.maximum(pad_ends - rows_per_block, 0).astype(I32), n_used])
    experts = jnp.arange(n_exp, dtype=I32)
    nonempty = counts > 0
    order = jnp.cumsum(nonempty.astype(I32)) - nonempty.astype(I32)
    later_ne = jnp.where((experts[None, :] > experts[:, None]) & nonempty[None, :],
                         experts[None, :], n_exp)
    next_tbl = jnp.min(later_ne, axis=1)
    onehot = block_expert[:, None] == experts[None, :]
    w_slot = jnp.sum(jnp.where(onehot, order & 1, 0), axis=1).astype(I32)
    w_next = jnp.sum(jnp.where(onehot, next_tbl, 0), axis=1).astype(I32)
    return dest, (block_expert, n_used, w_slot, w_next), fill_start, n_blocks


def _by_tile(dest, tile):
    k, t = dest.shape
    return dest.reshape(k, t // tile, tile).transpose(1, 0, 2).reshape(k * t)


def _dispatch_kernel(fill_ref, dest_hbm, xn_ref, xs_hbm, idx_smem, zeros_ref, idx_sem, row_sem,
                     fill_sem, *, tile, n_exp, fill_rows):
    i = pl.program_id(0)
    n = pl.num_programs(0)
    per_tile = TOP_K * tile

    def idx_copy(b):
        src = dest_hbm.at[pl.ds(pl.multiple_of(b * per_tile, per_tile), per_tile)]
        dst = idx_smem.at[pl.ds(pl.multiple_of((b % 2) * per_tile, per_tile), per_tile)]
        return pltpu.make_async_copy(src, dst, idx_sem.at[b % 2])

    @pl.when(i == 0)
    def _():
        idx_copy(0).start()
        zeros_ref[...] = jnp.zeros_like(zeros_ref)
        def fill(start):
            return pltpu.make_async_copy(
                zeros_ref, xs_hbm.at[pl.ds(pl.multiple_of(start, SUBLANES), fill_rows), 0, :],
                fill_sem)

        def start_unused(b, carry):
            fill(b * fill_rows).start()
            return carry

        def wait_unused(b, carry):
            fill(b * fill_rows).wait()
            return carry

        n_blocks = xs_hbm.shape[0] // fill_rows
        for e in range(n_exp):
            fill(fill_ref[e]).start()
        lax.fori_loop(fill_ref[n_exp], n_blocks, start_unused, 0)
        for e in range(n_exp):
            fill(fill_ref[e]).wait()
        lax.fori_loop(fill_ref[n_exp], n_blocks, wait_unused, 0)

    idx_copy(i).wait()

    @pl.when(i + 1 < n)
    def _():
        idx_copy(i + 1).start()

    base = pl.multiple_of((i % 2) * per_tile, per_tile)
    quarter = tile // TOP_K
    for part in range(TOP_K):
        def issue(g, carry, part=part):
            group = part * (quarter // SUBLANES) + g
            for s in range(SUBLANES):
                for k in range(TOP_K):
                    dst_row = idx_smem[base + k * tile + group * SUBLANES + s]
                    pltpu.make_async_copy(xn_ref.at[group, pl.ds(s, 1), :], xs_hbm.at[dst_row],
                                          row_sem.at[part]).start(priority=k % 2)
            return carry

        lax.fori_loop(0, quarter // SUBLANES, issue, 0)
    for part in range(TOP_K):
        pltpu.make_async_copy(xn_ref, xn_ref, row_sem.at[part]).wait()


def _dispatch(xn, dest, fill_start, n_rows):
    t, d = xn.shape
    tile = min(DISPATCH_TILE, t)
    n_exp = fill_start.shape[0] - 1
    grid_spec = pltpu.PrefetchScalarGridSpec(
        num_scalar_prefetch=1,
        grid=(t // tile,),
        in_specs=[
            pl.BlockSpec(memory_space=pl.ANY),
            pl.BlockSpec((tile // SUBLANES, SUBLANES, d), lambda i, fs: (i, 0, 0)),
        ],
        out_specs=pl.BlockSpec(memory_space=pl.ANY),
        scratch_shapes=[
            pltpu.SMEM((2 * TOP_K * tile,), I32),
            pltpu.VMEM((EXPERT_ROWS, d), F32),
            pltpu.SemaphoreType.DMA((2,)),
            pltpu.SemaphoreType.DMA((TOP_K,)),
            pltpu.SemaphoreType.DMA,
        ],
    )
    return pl.pallas_call(
        functools.partial(_dispatch_kernel, tile=tile, n_exp=n_exp, fill_rows=EXPERT_ROWS),
        grid_spec=grid_spec,
        out_shape=jax.ShapeDtypeStruct((n_rows, 1, d), F32),
        compiler_params=pltpu.CompilerParams(
            dimension_semantics=("arbitrary",), vmem_limit_bytes=40 * MIB),
        name="moe_dispatch",
    )(fill_start, _by_tile(dest, tile), xn.reshape(t // SUBLANES, SUBLANES, d))


def _expert_kernel(be_ref, nused_ref, wslot_ref, wnext_ref, xs_hbm, wg_hbm, bg_ref, wu_hbm, bu_ref,
                   wd_hbm, bd_ref, y_hbm, xbuf, ybuf, x_sem, y_sem, wg32, wu32, wd32, w_sem,
                   wg_bf, wu_bf, wd_bf, *, rows, n_exp):
    i = pl.program_id(0)
    n = nused_ref[0]

    def w_copies(e, slot):
        return [pltpu.make_async_copy(src.at[e], dst.at[slot], w_sem.at[slot])
                for src, dst in ((wg_hbm, wg32), (wu_hbm, wu32), (wd_hbm, wd32))]

    def x_copy(b):
        return pltpu.make_async_copy(
            xs_hbm.at[pl.ds(pl.multiple_of(b * rows, rows), rows), 0, :], xbuf.at[b % 2],
            x_sem.at[b % 2])

    def y_copy(b):
        return pltpu.make_async_copy(
            ybuf.at[b % 2], y_hbm.at[pl.ds(pl.multiple_of(b * rows, rows), rows), 0, :],
            y_sem.at[b % 2])

    @pl.when(i < n)
    def _():
        @pl.when(i == 0)
        def _():
            x_copy(0).start()
            for cp in w_copies(be_ref[0], wslot_ref[0]):
                cp.start()

        @pl.when(i + 1 < n)
        def _():
            x_copy(i + 1).start()

        changed = jnp.logical_or(i == 0, be_ref[i] != be_ref[jnp.maximum(i - 1, 0)])

        @pl.when(changed)
        def _():
            slot = wslot_ref[i]
            for cp in w_copies(be_ref[i], slot):
                cp.wait()

            @pl.when(wnext_ref[i] < n_exp)
            def _():
                for cp in w_copies(wnext_ref[i], 1 - slot):
                    cp.start(priority=1)
            wg_bf[...] = wg32[slot].astype(BF16)
            wu_bf[...] = wu32[slot].astype(BF16)
            wd_bf[...] = wd32[slot].astype(BF16)

        @pl.when(i >= 2)
        def _():
            y_copy(i - 2).wait()

        x_copy(i).wait()
        x = xbuf[i % 2].astype(BF16)
        gate = jnp.dot(x, wg_bf[...], preferred_element_type=F32) + bg_ref[...]
        up = jnp.dot(x, wu_bf[...], preferred_element_type=F32) + bu_ref[...]
        gate = jnp.minimum(gate, SWIGLU_LIMIT)
        up = jnp.clip(up, -SWIGLU_LIMIT, SWIGLU_LIMIT)
        glu = gate * (1.0 / (1.0 + jnp.exp(-SWIGLU_ALPHA * gate)))
        hidden = ((up + 1.0) * glu).astype(BF16)
        ybuf[i % 2] = jnp.dot(hidden, wd_bf[...], preferred_element_type=F32) + bd_ref[...]
        y_copy(i).start()

        @pl.when(i == n - 1)
        def _():
            y_copy(i).wait()

            @pl.when(i >= 1)
            def _():
                y_copy(i - 1).wait()

            def zero_copy(b):
                return pltpu.make_async_copy(
                    ybuf.at[0], y_hbm.at[pl.ds(pl.multiple_of(b * rows, rows), rows), 0, :],
                    y_sem.at[0])

            def start_zero(b, carry):
                zero_copy(b).start()
                return carry

            def wait_zero(b, carry):
                zero_copy(b).wait()
                return carry

            ybuf[0] = jnp.zeros((rows, ybuf.shape[2]), F32)
            lax.fori_loop(n, pl.num_programs(0), start_zero, 0)
            lax.fori_loop(n, pl.num_programs(0), wait_zero, 0)


def _experts(x_sorted, block_plan, n_blocks, w_gate, b_gate, w_up, b_up, w_down, b_down):
    n_rows, _, d = x_sorted.shape
    n_exp, _, f = w_gate.shape
    rows = EXPERT_ROWS
    bmap = lambda i, be, nu, ws, wn: (be[i], 0, 0)
    hbm = pl.BlockSpec(memory_space=pl.ANY)
    grid_spec = pltpu.PrefetchScalarGridSpec(
        num_scalar_prefetch=4,
        grid=(n_blocks,),
        in_specs=[
            hbm,
            hbm, pl.BlockSpec((None, 1, f), bmap),
            hbm, pl.BlockSpec((None, 1, f), bmap),
            hbm, pl.BlockSpec((None, 1, d), bmap),
        ],
        out_specs=hbm,
        scratch_shapes=[
            pltpu.VMEM((2, rows, d), F32),
            pltpu.VMEM((2, rows, d), F32),
            pltpu.SemaphoreType.DMA((2,)),
            pltpu.SemaphoreType.DMA((2,)),
            pltpu.VMEM((2, d, f), F32),
            pltpu.VMEM((2, d, f), F32),
            pltpu.VMEM((2, f, d), F32),
            pltpu.SemaphoreType.DMA((2,)),
            pltpu.VMEM((d, f), BF16),
            pltpu.VMEM((d, f), BF16),
            pltpu.VMEM((f, d), BF16),
        ],
    )
    return pl.pallas_call(
        functools.partial(_expert_kernel, rows=rows, n_exp=n_exp),
        grid_spec=grid_spec,
        out_shape=jax.ShapeDtypeStruct((n_rows, 1, d), F32),
        compiler_params=pltpu.CompilerParams(
            dimension_semantics=("arbitrary",), vmem_limit_bytes=52 * MIB),
        name="moe_experts",
    )(*block_plan, x_sorted,
      w_gate, b_gate.reshape(n_exp, 1, f), w_up, b_up.reshape(n_exp, 1, f),
      w_down, b_down.reshape(n_exp, 1, d))


def _combine_kernel(dest_hbm, y_hbm, x1_ref, gate_ref, lnf_ref, o_ref, idx_smem, ybuf,
                    idx_sem, row_sem, *, tile):
    i = pl.program_id(0)
    n = pl.num_programs(0)
    per_tile = TOP_K * tile

    def slot_base(b):
        return pl.multiple_of((b % 3) * per_tile, per_tile)

    def idx_copy(b):
        src = dest_hbm.at[pl.ds(pl.multiple_of(b * per_tile, per_tile), per_tile)]
        return pltpu.make_async_copy(src, idx_smem.at[pl.ds(slot_base(b), per_tile)],
                                     idx_sem.at[b % 3])

    def gather(b):
        base = slot_base(b)
        buf = ybuf.at[b % 2]
        sem = row_sem.at[b % 2]

        def issue(g, carry):
            for s in range(SUBLANES):
                pltpu.make_async_copy(y_hbm.at[idx_smem[base + g * SUBLANES + s]],
                                      buf.at[g, pl.ds(s, 1), :], sem).start(priority=s % 2)
            return carry

        lax.fori_loop(0, per_tile // SUBLANES, issue, 0)

    @pl.when(i == 0)
    def _():
        idx_copy(0).start()
        idx_copy(0).wait()
        gather(0)

        @pl.when(n > 1)
        def _():
            idx_copy(1).start()

    @pl.when(i + 1 < n)
    def _():
        idx_copy(i + 1).wait()

        @pl.when(i + 2 < n)
        def _():
            idx_copy(i + 2).start()
        gather(i + 1)

    cur = i % 2
    pltpu.make_async_copy(ybuf.at[1 - cur], ybuf.at[cur], row_sem.at[cur]).wait()
    gates = gate_ref[...]
    x = x1_ref[...]
    groups = tile // SUBLANES
    for k in range(TOP_K):
        yk = ybuf[cur, k * groups:(k + 1) * groups].reshape(tile, x.shape[1])
        x = x + gates[:, k:k + 1] * yk
    o_ref[...] = _rms(x, lnf_ref[...])


def _combine(dest, y_rows, x1, gates, ln_f_g):
    t, d = x1.shape
    tile = min(COMBINE_TILE, t)
    n_tiles = t // tile
    dest_tiles = _by_tile(dest, tile)
    return pl.pallas_call(
        functools.partial(_combine_kernel, tile=tile),
        grid=(n_tiles,),
        in_specs=[
            pl.BlockSpec(memory_space=pl.ANY),
            pl.BlockSpec(memory_space=pl.ANY),
            pl.BlockSpec((tile, d), lambda i: (i, 0)),
            pl.BlockSpec((tile, TOP_K), lambda i: (i, 0)),
            pl.BlockSpec((1, d), lambda i: (0, 0)),
        ],
        out_specs=pl.BlockSpec((tile, d), lambda i: (i, 0)),
        out_shape=jax.ShapeDtypeStruct((t, d), F32),
        scratch_shapes=[
            pltpu.SMEM((3 * TOP_K * tile,), I32),
            pltpu.VMEM((2, TOP_K * tile // SUBLANES, SUBLANES, d), F32),
            pltpu.SemaphoreType.DMA((3,)),
            pltpu.SemaphoreType.DMA((2,)),
        ],
        compiler_params=pltpu.CompilerParams(
            dimension_semantics=("arbitrary",), vmem_limit_bytes=40 * MIB),
        name="moe_combine",
    )(dest_tiles, y_rows, x1, gates, ln_f_g.reshape(1, d))


def kernel(x, ln1_g, w_in, lam_re, lam_im, log_dt, ssm_b_re, ssm_b_im, ssm_c_re, ssm_c_im,
           ssm_d, w_glu, g_sb, g_ssm, w_out, ln2_g, w_router, b_router, w_gate, b_gate,
           w_up, b_up, w_down, b_down, ln_f_g):
    bsz, seq, d = x.shape
    assert ln1_g.shape[0] == 1, "depth-1 block only"
    ssm_w = ssm_d.shape[1]
    sb = g_sb.shape[1]
    x2 = x.reshape(bsz * seq, d)
    q, k, v, u = _in_proj(x2, ln1_g[0], w_in[0], sb, ssm_w)
    y_sb = _attention(q, k, v, bsz, seq)
    y_ss = _ssm(u, bsz, seq, lam_re[0], lam_im[0], log_dt[0], ssm_b_re[0], ssm_b_im[0],
                ssm_c_re[0], ssm_c_im[0])
    x1, xn, top_idx, gates, rank, counts = _post(
        x2, y_sb, y_ss, u, ssm_d[0], w_glu[0], g_sb[0], g_ssm[0], w_out[0], ln2_g[0],
        w_router[0], b_router[0])
    dest, block_plan, fill_start, n_blocks = _plan(top_idx[:TOP_K], rank[:TOP_K], counts,
                                                   EXPERT_ROWS)
    x_sorted = _dispatch(xn, dest, fill_start, n_blocks * EXPERT_ROWS)
    y_rows = _experts(x_sorted, block_plan, n_blocks, w_gate[0], b_gate[0],
                      w_up[0], b_up[0], w_down[0], b_down[0])
    out = _combine(dest, y_rows, x1, gates, ln_f_g)
    return out.reshape(bsz, seq, d)
```

```python
import functools
import math

import jax
import jax.numpy as jnp
from jax import lax
from jax.experimental import pallas as pl
from jax.experimental.pallas import tpu as pltpu

F32 = jnp.float32
BF16 = jnp.bfloat16
I32 = jnp.int32

EPS = 1e-5
SB_HEAD_DIM = 64
SSM_GROUP = 16
SSM_STATE = 64
TOP_K = 4
SWIGLU_LIMIT = 7.0
SWIGLU_ALPHA = 1.702

LANES = 128
SUBLANES = 8
HEADS_PER_BLOCK = LANES // SB_HEAD_DIM
ATTN_BLOCK = 256
ATTN_EXP_FLOOR = 160.0
ATTN_STREAMS = 4
SSM_CHUNK = 16
SSM_FOLD = 8
TOKEN_TILE = 512
EXPERT_ROWS = 256
COMBINE_TILE = 512
DISPATCH_TILE = 2048
MIB = 1024 * 1024
LOG2E = 1.4426950408889634


def _rms(x, g):
    return x * lax.rsqrt(jnp.mean(x * x, axis=-1, keepdims=True) + EPS) * g


def _in_proj_kernel(x_ref, g_ref, w_ref, q_ref, k_ref, v_ref, u_ref, w_bf, *, sb, scale):
    @pl.when(pl.program_id(0) == 0)
    def _():
        w_bf[...] = w_ref[...].astype(BF16)

    h = _rms(x_ref[...], g_ref[...])
    proj = jnp.dot(h.astype(BF16), w_bf[...], preferred_element_type=F32)
    k_ref[...] = proj[:, sb:2 * sb].astype(BF16)
    u_ref[...] = proj[:, 3 * sb:]
    lane = lax.broadcasted_iota(I32, (1, LANES), 1)
    q = (proj[:, :sb] * scale).astype(BF16)
    v = proj[:, 2 * sb:3 * sb].astype(BF16)
    zero = jnp.zeros((), BF16)
    for pair in range(sb // LANES):
        cols = slice(pair * LANES, (pair + 1) * LANES)
        for head in range(HEADS_PER_BLOCK):
            own = (lane // SB_HEAD_DIM) == head
            out = slice((pair * HEADS_PER_BLOCK + head) * LANES,
                        (pair * HEADS_PER_BLOCK + head + 1) * LANES)
            q_ref[:, out] = jnp.where(own, q[:, cols], zero)
            v_ref[:, out] = jnp.where(own, v[:, cols], zero)


def _in_proj(x2, ln1_g, w_in, sb, ssm_w):
    t, d = x2.shape
    n_in = w_in.shape[1]
    tm = min(TOKEN_TILE, t)
    scale = LOG2E / math.sqrt(SB_HEAD_DIM)
    return pl.pallas_call(
        functools.partial(_in_proj_kernel, sb=sb, scale=scale),
        grid=(t // tm,),
        in_specs=[
            pl.BlockSpec((tm, d), lambda i: (i, 0)),
            pl.BlockSpec((1, d), lambda i: (0, 0)),
            pl.BlockSpec((d, n_in), lambda i: (0, 0)),
        ],
        out_specs=[
            pl.BlockSpec((tm, HEADS_PER_BLOCK * sb), lambda i: (i, 0)),
            pl.BlockSpec((tm, sb), lambda i: (i, 0)),
            pl.BlockSpec((tm, HEADS_PER_BLOCK * sb), lambda i: (i, 0)),
            pl.BlockSpec((tm, ssm_w), lambda i: (i, 0)),
        ],
        out_shape=[
            jax.ShapeDtypeStruct((t, HEADS_PER_BLOCK * sb), BF16),
            jax.ShapeDtypeStruct((t, sb), BF16),
            jax.ShapeDtypeStruct((t, HEADS_PER_BLOCK * sb), BF16),
            jax.ShapeDtypeStruct((t, ssm_w), F32),
        ],
        scratch_shapes=[pltpu.VMEM((d, n_in), BF16)],
        compiler_params=pltpu.CompilerParams(
            dimension_semantics=("arbitrary",), vmem_limit_bytes=48 * MIB),
        name="in_proj",
    )(x2, ln1_g.reshape(1, d), w_in)


def _attn_kernel(ti_ref, tj_ref, tn_ref, q_ref, k_ref, v_ref, tri_ref, mask_ref, o_ref,
                 dbuf, spbuf, lbuf, rsbuf, acc_ref, r_ref, *, blk, n_items, streams):
    sign_bit = jnp.uint32(0x80000000)
    last = n_items - 1
    ALL, CAUSAL, NONE = 0, 1, 2

    for ref in (dbuf, spbuf, lbuf, rsbuf, acc_ref, r_ref):
        ref[...] = jnp.zeros_like(ref)

    def both_heads(ref, start):
        rows = pl.ds(pl.multiple_of(start, blk), blk)
        return jnp.concatenate([ref[rows, h * LANES:(h + 1) * LANES]
                                for h in range(HEADS_PER_BLOCK)], axis=0)

    def item(idx):
        c = jnp.clip(idx, 0, last)
        return ti_ref[c], tj_ref[c], tn_ref[c]

    n_streams = len(streams)
    stream_end = [end for _, end in streams]

    def step(state, x):
        cur, p1, p2 = list(state[:n_streams]), state[n_streams], state[n_streams + 1]
        out_blk = list(state[n_streams + 2:])
        slot, other = x % 2, 1 - x % 2
        x3 = (x - 2) % n_streams
        c1 = jnp.where(cur[x] < stream_end[x], cur[x], n_items)
        cur[x] = jnp.minimum(cur[x] + 1, stream_end[x])
        c3 = p2
        i1, j1, _ = item(c1)
        ks = k_ref[pl.ds(pl.multiple_of(j1 * blk, blk), blk), :]
        w = lax.dot_general(both_heads(q_ref, i1 * blk), ks, (((1,), (1,)), ((), ())),
                            preferred_element_type=F32)
        i3, j3, next_block = item(c3)
        valid = c3 <= last
        first = jnp.logical_and(i3 == j3, valid)
        mask3 = mask_ref[jnp.where(valid, (i3 == j3).astype(I32), NONE)]
        i3 = jnp.where(valid, i3, out_blk[x3])
        r_prev = jnp.where(first, 0.0, r_ref[x3])
        r_new = r_prev + rsbuf[other]
        r_ref[x3] = r_new
        done = jnp.logical_and(valid, jnp.min(r_new) >= ATTN_EXP_FLOOR)
        cur[x3] = jnp.where(done, jnp.maximum(cur[x3], next_block), cur[x3])
        out_blk[x3] = i3
        r_wide = jnp.concatenate([r_prev] * (blk // LANES), axis=1)
        a = jnp.exp2(dbuf[slot] - (lbuf[other] + r_wide))
        ab = a.astype(BF16) * jnp.concatenate([mask3, mask3], axis=0)
        a_cat = jnp.concatenate([ab[:blk], ab[blk:]], axis=1)
        acc = jnp.where(first, 0.0, acc_ref[x3]) + jnp.dot(
            a_cat, both_heads(v_ref, j3 * blk), preferred_element_type=F32)
        acc_ref[x3] = acc
        o_ref[pl.ds(pl.multiple_of(i3 * blk, blk), blk), :] = acc.astype(o_ref.dtype)
        sums = jnp.dot(spbuf[other], tri_ref[...], preferred_element_type=F32)
        lbuf[slot] = sums[:, :blk]
        rsbuf[slot] = sums[:, blk:]
        mask1 = mask_ref[(i1 == j1).astype(I32)]
        neg_abs = lax.bitcast_convert_type(
            lax.bitcast_convert_type(w, jnp.uint32) | sign_bit, F32)
        sp2 = jnp.maximum(w, 0.0) + jnp.log(1.0 + jnp.exp2(neg_abs)) * LOG2E
        dbuf[slot] = w - sp2
        spbuf[slot] = sp2.astype(BF16) * jnp.concatenate([mask1, mask1], axis=0)
        return (*cur, c1, p1, *out_blk)

    def in_flight(state):
        p1, p2 = state[n_streams], state[n_streams + 1]
        live = [state[x] < stream_end[x] for x in range(n_streams)]
        return functools.reduce(jnp.logical_or, live + [p1 <= last, p2 <= last])

    def one_round(state):
        for x in range(n_streams):
            state = step(state, x)
        return state

    none = jnp.int32(n_items)
    first_items = [jnp.int32(start) for start, _ in streams]
    first_blocks = [ti_ref[start] for start, _ in streams]
    lax.while_loop(in_flight, one_round, (*first_items, none, none, *first_blocks))


def _attention(q, k, v, bsz, seq):
    sb = k.shape[-1]
    blk = min(ATTN_BLOCK, seq)
    n_blk = seq // blk
    n_pairs = sb // LANES
    q3, k3, v3 = (a.reshape(bsz, seq, a.shape[-1]) for a in (q, k, v))
    per_head = HEADS_PER_BLOCK * LANES
    jj = lax.broadcasted_iota(I32, (blk, blk), 0)
    ss = lax.broadcasted_iota(I32, (blk, blk), 1)
    tri = jnp.concatenate([(jj > ss).astype(BF16), jnp.ones((blk, LANES), BF16)], axis=1)
    masks = jnp.stack([jnp.ones((blk, blk), BF16), (ss < jj).astype(BF16),
                       jnp.zeros((blk, blk), BF16)])
    n_streams = ATTN_STREAMS if n_blk % ATTN_STREAMS == 0 else 2
    assert n_blk % n_streams == 0, (n_blk, n_streams)
    items, next_block, streams = [], [], []
    for x in range(n_streams):
        start = len(items)
        for i in range(x, n_blk, n_streams):
            block_end = len(items) + i + 1
            items += [(i, j) for j in range(i, -1, -1)]
            next_block += [block_end] * (i + 1)
        streams.append((start, len(items)))
    item_i = jnp.asarray([i for i, _ in items], I32)
    item_j = jnp.asarray([j for _, j in items], I32)
    item_next = jnp.asarray(next_block, I32)
    whole = lambda b, p, ti, tj, tn: (b, 0, p)
    grid_spec = pltpu.PrefetchScalarGridSpec(
        num_scalar_prefetch=3,
        grid=(bsz, n_pairs),
        in_specs=[
            pl.BlockSpec((None, seq, per_head), whole),
            pl.BlockSpec((None, seq, LANES), whole),
            pl.BlockSpec((None, seq, per_head), whole),
            pl.BlockSpec((blk, blk + LANES), lambda b, p, ti, tj, tn: (0, 0)),
            pl.BlockSpec((3, blk, blk), lambda b, p, ti, tj, tn: (0, 0, 0)),
        ],
        out_specs=pl.BlockSpec((None, seq, LANES), whole),
        scratch_shapes=[
            pltpu.VMEM((2, HEADS_PER_BLOCK * blk, blk), F32),
            pltpu.VMEM((2, HEADS_PER_BLOCK * blk, blk), BF16),
            pltpu.VMEM((2, HEADS_PER_BLOCK * blk, blk), F32),
            pltpu.VMEM((2, HEADS_PER_BLOCK * blk, LANES), F32),
            pltpu.VMEM((n_streams, blk, LANES), F32),
            pltpu.VMEM((n_streams, HEADS_PER_BLOCK * blk, LANES), F32),
        ],
    )
    out = pl.pallas_call(
        functools.partial(_attn_kernel, blk=blk, n_items=len(items), streams=tuple(streams)),
        grid_spec=grid_spec,
        out_shape=jax.ShapeDtypeStruct((bsz, seq, sb), BF16),
        compiler_params=pltpu.CompilerParams(
            dimension_semantics=("arbitrary", "arbitrary"), vmem_limit_bytes=40 * MIB),
        name="sb_attention",
    )(item_i, item_j, item_next, q3, k3, v3, tri, masks)
    return out.reshape(bsz * seq, sb)


def _ssm_params(lam_re, lam_im, log_dt, b_re, b_im, c_re, c_im, chunk, n_chunks):
    lam = lax.complex(lam_re.astype(F32), lam_im.astype(F32))
    dt = jnp.exp(log_dt.astype(F32))[:, None]
    lam_dt = lam * dt
    lam_bar = jnp.exp(lam_dt)
    b_bar = ((lam_bar - 1.0) / lam)[:, :, None] * lax.complex(b_re.astype(F32), b_im.astype(F32))
    b_t = jnp.swapaxes(b_bar, 1, 2)
    c_mat = lax.complex(c_re.astype(F32), c_im.astype(F32))
    steps = jnp.arange(chunk + 1, dtype=F32)
    pw = jnp.exp(lam_dt[:, None, :] * steps[None, :, None])
    npw = jnp.exp(-lam_dt[:, None, :] * steps[None, :chunk, None])

    def halves(lo, hi):
        return jnp.concatenate([lo, hi], axis=-1)

    def b_side(z):
        return [halves(jnp.real(z), jnp.real(z)), halves(-jnp.imag(z), jnp.imag(z))]

    def c_side(z):
        return [halves(jnp.real(z), jnp.imag(z)), halves(jnp.imag(z), jnp.real(z))]

    time_terms = jnp.stack(
        b_side(npw)
        + b_side(pw[:, chunk - 1::-1])
        + c_side(pw[:, :chunk])
        + c_side(pw[:, 1:]), axis=1)
    chan_terms = jnp.stack(
        [halves(jnp.real(b_t), jnp.imag(b_t)), halves(jnp.imag(b_t), jnp.real(b_t)),
         halves(jnp.real(c_mat), -jnp.real(c_mat)), halves(-jnp.imag(c_mat), -jnp.imag(c_mat))],
        axis=1)
    n_steps = max(1, (n_chunks - 1).bit_length())
    powers = []
    cur = pw[:, chunk, :]
    for _ in range(n_steps):
        powers.append(cur)
        cur = cur * cur
    lam_pow = jnp.stack(powers, axis=1)
    a1 = halves(jnp.real(lam_pow), jnp.real(lam_pow))
    a2 = halves(-jnp.imag(lam_pow), jnp.imag(lam_pow))
    return time_terms, chan_terms, a1, a2


def _ssm_kernel(u_ref, time_ref, chan_ref, a1_ref, a2_ref, perm_ref, perm_t_ref, y_ref,
                u8_ref, y8_ref, toep_ref, *, n_chunks, n_steps, group):
    seq = u_ref.shape[0]
    tiles = seq // SSM_FOLD
    lc = toep_ref.shape[0]
    per_chunk = (lc // group) // SSM_FOLD
    p2 = a1_ref.shape[2]
    group_shift = group.bit_length() - 1
    nt = (((1,), (1,)), ((), ()))

    folded = jnp.concatenate(
        [u_ref[pl.ds(s, tiles, stride=SSM_FOLD), :] for s in range(SSM_FOLD)], axis=1)
    regrouped = jnp.dot(folded.astype(BF16), perm_ref[...], preferred_element_type=F32)
    n_groups = LANES // group
    for g in range(n_groups):
        u8_ref[g] = regrouped[:, g * LANES:(g + 1) * LANES]

    for g in range(n_groups):
        def table(t1, t2, v1, v2):
            full = (time_ref[g, t1][:, None, :] * chan_ref[g, v1][None, :, :]
                    + time_ref[g, t2][:, None, :] * chan_ref[g, v2][None, :, :])
            return full.reshape(lc, p2).astype(BF16)

        src = table(0, 1, 0, 1)
        to_state = table(2, 3, 0, 1)
        dst = table(4, 5, 2, 3)
        from_state = table(6, 7, 2, 3)
        cb = min(256, lc)
        for j in range(lc // cb):
            blk = lax.dot_general(src, dst[j * cb:(j + 1) * cb], nt, preferred_element_type=F32)
            s_idx = lax.broadcasted_iota(I32, (lc, cb), 0) >> group_shift
            t_idx = (lax.broadcasted_iota(I32, (lc, cb), 1) + j * cb) >> group_shift
            toep_ref[:, j * cb:(j + 1) * cb] = jnp.where(s_idx <= t_idx, blk, 0.0).astype(BF16)
        u = jnp.concatenate(
            [u8_ref[g, pl.ds(tau, n_chunks, stride=per_chunk), :] for tau in range(per_chunk)],
            axis=1).astype(BF16)
        y = jnp.dot(u, toep_ref[...], preferred_element_type=F32)
        z = jnp.dot(u, to_state, preferred_element_type=F32)
        n = lax.broadcasted_iota(I32, z.shape, 0)
        x = jnp.where(n >= 1, pltpu.roll(z, 1, 0), 0.0)
        for k in range(n_steps):
            sh = 1 << k
            xs = jnp.where(n >= sh, pltpu.roll(x, sh, 0), 0.0)
            x = (x + a1_ref[g, k:k + 1, :] * xs
                 + a2_ref[g, k:k + 1, :] * pltpu.roll(xs, p2 // 2, 1))
        y = y + lax.dot_general(x.astype(BF16), from_state, nt, preferred_element_type=F32)
        for tau in range(per_chunk):
            y8_ref[g, pl.ds(tau, n_chunks, stride=per_chunk), :] = y[:, tau * LANES:(tau + 1) * LANES]

    y8 = jnp.concatenate([y8_ref[g] for g in range(n_groups)], axis=1)
    unfolded = jnp.dot(y8.astype(BF16), perm_t_ref[...], preferred_element_type=F32)
    for s in range(SSM_FOLD):
        y_ref[pl.ds(s, tiles, stride=SSM_FOLD), :] = unfolded[:, s * LANES:(s + 1) * LANES]


def _ssm(u, bsz, seq, lam_re, lam_im, log_dt, b_re, b_im, c_re, c_im):
    g, p = lam_re.shape
    c = b_re.shape[-1]
    chunk = min(SSM_CHUNK, seq)
    n_chunks = seq // chunk
    assert n_chunks & (n_chunks - 1) == 0 and c & (c - 1) == 0, (n_chunks, c)
    assert SSM_FOLD * c == LANES and chunk % SSM_FOLD == 0
    lc = chunk * c
    per_block = LANES // c
    time_terms, chan_terms, a1, a2 = _ssm_params(
        lam_re, lam_im, log_dt, b_re, b_im, c_re, c_im, chunk, n_chunks)
    n_steps = a1.shape[1]
    src_idx = jnp.arange(SSM_FOLD * LANES, dtype=I32)
    s_of, g_of, c_of = src_idx // LANES, (src_idx % LANES) // c, src_idx % c
    dst_idx = g_of * LANES + s_of * c + c_of
    perm = (dst_idx[:, None] == src_idx[None, :]).astype(BF16)
    block = lambda cb, b: (cb, 0, 0)
    fixed = lambda cb, b: (0, 0)
    y = pl.pallas_call(
        functools.partial(_ssm_kernel, n_chunks=n_chunks, n_steps=n_steps, group=c),
        grid=(g // per_block, bsz),
        in_specs=[
            pl.BlockSpec((None, seq, LANES), lambda cb, b: (b, 0, cb)),
            pl.BlockSpec((per_block,) + time_terms.shape[1:], lambda cb, b: (cb, 0, 0, 0)),
            pl.BlockSpec((per_block,) + chan_terms.shape[1:], lambda cb, b: (cb, 0, 0, 0)),
            pl.BlockSpec((per_block, n_steps, 2 * p), block),
            pl.BlockSpec((per_block, n_steps, 2 * p), block),
            pl.BlockSpec((SSM_FOLD * LANES, SSM_FOLD * LANES), fixed),
            pl.BlockSpec((SSM_FOLD * LANES, SSM_FOLD * LANES), fixed),
        ],
        out_specs=pl.BlockSpec((None, seq, LANES), lambda cb, b: (b, 0, cb)),
        out_shape=jax.ShapeDtypeStruct((bsz, seq, g * c), F32),
        scratch_shapes=[
            pltpu.VMEM((per_block, seq // SSM_FOLD, LANES), F32),
            pltpu.VMEM((per_block, seq // SSM_FOLD, LANES), F32),
            pltpu.VMEM((lc, lc), BF16),
        ],
        compiler_params=pltpu.CompilerParams(
            dimension_semantics=("arbitrary", "arbitrary"), vmem_limit_bytes=56 * MIB),
        name="s5_chunked_scan",
    )(u.reshape(bsz, seq, g * c), time_terms, chan_terms, a1, a2, perm, perm.T)
    return y.reshape(bsz * seq, g * c)


def _post_kernel(x_ref, ysb_ref, yss_ref, u_ref, d_ref, wglu_ref, gsb_ref, gssm_ref, wout_ref,
                 ln2_ref, wrt_ref, br_ref, tri_ref, x1_ref, xn_ref, idx_ref, gate_ref, rank_ref,
                 count_ref, running_ref, wglu_bf, wout_bf, *, sb, n_exp):
    @pl.when(pl.program_id(0) == 0)
    def _():
        running_ref[...] = jnp.zeros_like(running_ref)
        wglu_bf[...] = wglu_ref[...].astype(BF16)
        wout_bf[...] = wout_ref[...].astype(BF16)

    u = u_ref[...].astype(F32)
    y = yss_ref[...].astype(F32) + d_ref[...] * u
    y = y * (0.5 * (1.0 + jnp.tanh(math.sqrt(2.0 / math.pi) * (y + 0.044715 * (y * y * y)))))
    ab = jnp.dot(y.astype(BF16), wglu_bf[...], preferred_element_type=F32)
    w = ab.shape[1] // 2
    y_ssm = ab[:, :w] * (1.0 / (1.0 + jnp.exp(-ab[:, w:])))
    m_sb = _rms(ysb_ref[...].astype(F32), gsb_ref[...])
    m_ssm = _rms(y_ssm, gssm_ref[...])
    x1 = (x_ref[...]
          + jnp.dot(m_sb.astype(BF16), wout_bf[:sb, :], preferred_element_type=F32)
          + jnp.dot(m_ssm.astype(BF16), wout_bf[sb:, :], preferred_element_type=F32))
    x1_ref[...] = x1
    xn = _rms(x1, ln2_ref[...])
    xn_ref[...] = xn
    def split(a):
        hi = a.astype(BF16)
        return hi, (a - hi.astype(F32)).astype(BF16)

    xn_hi, xn_lo = split(xn)
    wt_hi, wt_lo = split(wrt_ref[...])
    nt = (((1,), (1,)), ((), ()))
    both = lax.dot_general(jnp.concatenate([wt_hi, wt_lo], axis=0), xn_hi, nt,
                           preferred_element_type=F32)
    low = lax.dot_general(wt_hi, xn_lo, nt, preferred_element_type=F32)
    logits = (both[:n_exp] + both[n_exp:] + low) + br_ref[...]
    tokens = logits.shape[1]
    expert = lax.broadcasted_iota(I32, logits.shape, 0).astype(F32)
    out_row = lax.broadcasted_iota(I32, (SUBLANES, tokens), 0)
    idx_out = jnp.zeros((SUBLANES, tokens), F32)
    val_out = jnp.zeros((SUBLANES, tokens), F32)
    top = None
    denom = None
    work = logits
    chosen = []
    for k in range(TOP_K):
        m = jnp.max(work, axis=0, keepdims=True)
        sel = jnp.min(jnp.where(work == m, expert, float(n_exp)), axis=0, keepdims=True)
        hit = expert == sel
        chosen.append(hit)
        work = jnp.where(hit, -jnp.inf, work)
        if k == 0:
            top = m
        e = jnp.exp(m - top)
        denom = e if denom is None else denom + e
        idx_out = jnp.where(out_row == k, sel, idx_out)
        val_out = jnp.where(out_row == k, e, val_out)
    idx_ref[...] = idx_out.astype(I32)
    gates = jnp.concatenate([val_out / denom, jnp.zeros((LANES - SUBLANES, tokens), F32)], axis=0)
    gate_ref[...] = jnp.transpose(gates)[:, :TOP_K]
    member = functools.reduce(jnp.logical_or, chosen).astype(BF16)
    before = running_ref[...] + jnp.dot(member, tri_ref[...], preferred_element_type=F32)
    rank_out = jnp.zeros((SUBLANES, tokens), F32)
    for k in range(TOP_K):
        rk = jnp.sum(jnp.where(chosen[k], before, 0.0), axis=0, keepdims=True)
        rank_out = jnp.where(out_row == k, rk, rank_out)
    rank_ref[...] = rank_out.astype(I32)
    running = running_ref[...] + jnp.sum(member.astype(F32), axis=1, keepdims=True)
    running_ref[...] = running
    count_ref[...] = running


def _post(x2, y_sb, y_ss, u, ssm_d, w_glu, g_sb, g_ssm, w_out, ln2_g, w_router, b_router):
    t, d = x2.shape
    sb = y_sb.shape[1]
    w = y_ss.shape[1]
    n_exp = w_router.shape[1]
    tm = min(TOKEN_TILE, t)
    row = lambda i: (i, 0)
    fixed = lambda i: (0, 0)
    earlier = (lax.broadcasted_iota(I32, (tm, tm), 0)
               < lax.broadcasted_iota(I32, (tm, tm), 1)).astype(BF16)
    return pl.pallas_call(
        functools.partial(_post_kernel, sb=sb, n_exp=n_exp),
        grid=(t // tm,),
        in_specs=[
            pl.BlockSpec((tm, d), row),
            pl.BlockSpec((tm, sb), row),
            pl.BlockSpec((tm, w), row),
            pl.BlockSpec((tm, w), row),
            pl.BlockSpec((1, w), fixed),
            pl.BlockSpec((w, 2 * w), fixed),
            pl.BlockSpec((1, sb), fixed),
            pl.BlockSpec((1, w), fixed),
            pl.BlockSpec((sb + w, d), fixed),
            pl.BlockSpec((1, d), fixed),
            pl.BlockSpec((n_exp, d), fixed),
            pl.BlockSpec((n_exp, 1), fixed),
            pl.BlockSpec((tm, tm), fixed),
        ],
        out_specs=[
            pl.BlockSpec((tm, d), row),
            pl.BlockSpec((tm, d), row),
            pl.BlockSpec((SUBLANES, tm), lambda i: (0, i)),
            pl.BlockSpec((tm, TOP_K), row),
            pl.BlockSpec((SUBLANES, tm), lambda i: (0, i)),
            pl.BlockSpec((n_exp, 1), fixed),
        ],
        out_shape=[
            jax.ShapeDtypeStruct((t, d), F32),
            jax.ShapeDtypeStruct((t, d), F32),
            jax.ShapeDtypeStruct((SUBLANES, t), I32),
            jax.ShapeDtypeStruct((t, TOP_K), F32),
            jax.ShapeDtypeStruct((SUBLANES, t), I32),
            jax.ShapeDtypeStruct((n_exp, 1), F32),
        ],
        scratch_shapes=[pltpu.VMEM((n_exp, 1), F32), pltpu.VMEM((w, 2 * w), BF16),
                        pltpu.VMEM((sb + w, d), BF16)],
        compiler_params=pltpu.CompilerParams(
            dimension_semantics=("arbitrary",), vmem_limit_bytes=48 * MIB),
        name="post_mixer_router",
    )(x2, y_sb, y_ss, u, ssm_d.reshape(1, w), w_glu, g_sb.reshape(1, sb),
      g_ssm.reshape(1, w), w_out, ln2_g.reshape(1, d), w_router.T,
      b_router.reshape(n_exp, 1), earlier)


def _plan(top_idx, rank, counts, rows_per_block):
    k, t = top_idx.shape
    n_exp = counts.shape[0]
    counts = counts.reshape(n_exp).astype(I32)
    padded = ((counts + rows_per_block - 1) // rows_per_block) * rows_per_block
    pad_ends = jnp.cumsum(padded)
    pad_starts = pad_ends - padded
    start_of = jnp.sum(jnp.where(top_idx[None] == jnp.arange(n_exp, dtype=I32)[:, None, None],
                                 pad_starts[:, None, None], 0), axis=0)
    dest = (start_of + rank).astype(I32)
    n_rows = t * k + n_exp * rows_per_block
    n_blocks = n_rows // rows_per_block
    block_start = jnp.arange(n_blocks, dtype=I32) * rows_per_block
    block_expert = jnp.minimum(
        jnp.sum((pad_ends[None, :] <= block_start[:, None]).astype(I32), axis=1), n_exp - 1)
    n_used = (pad_ends[-1] // rows_per_block).astype(I32).reshape(1)
    fill_start = jnp.concatenate([jnp.maximum(pad_ends - rows_per_block, 0).astype(I32), n_used])
    experts = jnp.arange(n_exp, dtype=I32)
    nonempty = counts > 0
    order = jnp.cumsum(nonempty.astype(I32)) - nonempty.astype(I32)
    later_ne = jnp.where((experts[None, :] > experts[:, None]) & nonempty[None, :],
                         experts[None, :], n_exp)
    next_tbl = jnp.min(later_ne, axis=1)
    onehot = block_expert[:, None] == experts[None, :]
    w_slot = jnp.sum(jnp.where(onehot, order & 1, 0), axis=1).astype(I32)
    w_next = jnp.sum(jnp.where(onehot, next_tbl, 0), axis=1).astype(I32)
    return dest, (block_expert, n_used, w_slot, w_next), fill_start, n_blocks


def _by_tile(dest, tile):
    k, t = dest.shape
    return dest.reshape(k, t // tile, tile).transpose(1, 0, 2).reshape(k * t)


def _dispatch_kernel(fill_ref, dest_hbm, xn_ref, xs_hbm, idx_smem, zeros_ref, idx_sem, row_sem,
                     fill_sem, *, tile, n_exp, fill_rows):
    i = pl.program_id(0)
    n = pl.num_programs(0)
    per_tile = TOP_K * tile

    def idx_copy(b):
        src = dest_hbm.at[pl.ds(pl.multiple_of(b * per_tile, per_tile), per_tile)]
        dst = idx_smem.at[pl.ds(pl.multiple_of((b % 2) * per_tile, per_tile), per_tile)]
        return pltpu.make_async_copy(src, dst, idx_sem.at[b % 2])

    @pl.when(i == 0)
    def _():
        idx_copy(0).start()
        zeros_ref[...] = jnp.zeros_like(zeros_ref)
        def fill(start):
            return pltpu.make_async_copy(
                zeros_ref, xs_hbm.at[pl.ds(pl.multiple_of(start, SUBLANES), fill_rows), 0, :],
                fill_sem)

        def start_unused(b, carry):
            fill(b * fill_rows).start()
            return carry

        def wait_unused(b, carry):
            fill(b * fill_rows).wait()
            return carry

        n_blocks = xs_hbm.shape[0] // fill_rows
        for e in range(n_exp):
            fill(fill_ref[e]).start()
        lax.fori_loop(fill_ref[n_exp], n_blocks, start_unused, 0)
        for e in range(n_exp):
            fill(fill_ref[e]).wait()
        lax.fori_loop(fill_ref[n_exp], n_blocks, wait_unused, 0)

    idx_copy(i).wait()

    @pl.when(i + 1 < n)
    def _():
        idx_copy(i + 1).start()

    base = pl.multiple_of((i % 2) * per_tile, per_tile)
    quarter = tile // TOP_K
    for part in range(TOP_K):
        def issue(g, carry, part=part):
            group = part * (quarter // SUBLANES) + g
            for s in range(SUBLANES):
                for k in range(TOP_K):
                    dst_row = idx_smem[base + k * tile + group * SUBLANES + s]
                    pltpu.make_async_copy(xn_ref.at[group, pl.ds(s, 1), :], xs_hbm.at[dst_row],
                                          row_sem.at[part]).start(priority=k % 2)
            return carry

        lax.fori_loop(0, quarter // SUBLANES, issue, 0)
    for part in range(TOP_K):
        pltpu.make_async_copy(xn_ref, xn_ref, row_sem.at[part]).wait()


def _dispatch(xn, dest, fill_start, n_rows):
    t, d = xn.shape
    tile = min(DISPATCH_TILE, t)
    n_exp = fill_start.shape[0] - 1
    grid_spec = pltpu.PrefetchScalarGridSpec(
        num_scalar_prefetch=1,
        grid=(t // tile,),
        in_specs=[
            pl.BlockSpec(memory_space=pl.ANY),
            pl.BlockSpec((tile // SUBLANES, SUBLANES, d), lambda i, fs: (i, 0, 0)),
        ],
        out_specs=pl.BlockSpec(memory_space=pl.ANY),
        scratch_shapes=[
            pltpu.SMEM((2 * TOP_K * tile,), I32),
            pltpu.VMEM((EXPERT_ROWS, d), F32),
            pltpu.SemaphoreType.DMA((2,)),
            pltpu.SemaphoreType.DMA((TOP_K,)),
            pltpu.SemaphoreType.DMA,
        ],
    )
    return pl.pallas_call(
        functools.partial(_dispatch_kernel, tile=tile, n_exp=n_exp, fill_rows=EXPERT_ROWS),
        grid_spec=grid_spec,
        out_shape=jax.ShapeDtypeStruct((n_rows, 1, d), F32),
        compiler_params=pltpu.CompilerParams(
            dimension_semantics=("arbitrary",), vmem_limit_bytes=40 * MIB),
        name="moe_dispatch",
    )(fill_start, _by_tile(dest, tile), xn.reshape(t // SUBLANES, SUBLANES, d))


def _expert_kernel(be_ref, nused_ref, wslot_ref, wnext_ref, xs_hbm, wg_hbm, bg_ref, wu_hbm, bu_ref,
                   wd_hbm, bd_ref, y_hbm, xbuf, ybuf, x_sem, y_sem, wg32, wu32, wd32, w_sem,
                   wg_bf, wu_bf, wd_bf, *, rows, n_exp):
    i = pl.program_id(0)
    n = nused_ref[0]

    def w_copies(e, slot):
        return [pltpu.make_async_copy(src.at[e], dst.at[slot], w_sem.at[slot])
                for src, dst in ((wg_hbm, wg32), (wu_hbm, wu32), (wd_hbm, wd32))]

    def x_copy(b):
        return pltpu.make_async_copy(
            xs_hbm.at[pl.ds(pl.multiple_of(b * rows, rows), rows), 0, :], xbuf.at[b % 2],
            x_sem.at[b % 2])

    def y_copy(b):
        return pltpu.make_async_copy(
            ybuf.at[b % 2], y_hbm.at[pl.ds(pl.multiple_of(b * rows, rows), rows), 0, :],
            y_sem.at[b % 2])

    @pl.when(i < n)
    def _():
        @pl.when(i == 0)
        def _():
            x_copy(0).start()
            for cp in w_copies(be_ref[0], wslot_ref[0]):
                cp.start()

        @pl.when(i + 1 < n)
        def _():
            x_copy(i + 1).start()

        changed = jnp.logical_or(i == 0, be_ref[i] != be_ref[jnp.maximum(i - 1, 0)])

        @pl.when(changed)
        def _():
            slot = wslot_ref[i]
            for cp in w_copies(be_ref[i], slot):
                cp.wait()

            @pl.when(wnext_ref[i] < n_exp)
            def _():
                for cp in w_copies(wnext_ref[i], 1 - slot):
                    cp.start(priority=1)
            wg_bf[...] = wg32[slot].astype(BF16)
            wu_bf[...] = wu32[slot].astype(BF16)
            wd_bf[...] = wd32[slot].astype(BF16)

        @pl.when(i >= 2)
        def _():
            y_copy(i - 2).wait()

        x_copy(i).wait()
        x = xbuf[i % 2].astype(BF16)
        own = pl.ds(be_ref[i], 1)
        gate = jnp.dot(x, wg_bf[...], preferred_element_type=F32) + bg_ref[own, :]
        up = jnp.dot(x, wu_bf[...], preferred_element_type=F32) + bu_ref[own, :]
        gate = jnp.minimum(gate, SWIGLU_LIMIT)
        up = jnp.clip(up, -SWIGLU_LIMIT, SWIGLU_LIMIT)
        glu = gate * (1.0 / (1.0 + jnp.exp(-SWIGLU_ALPHA * gate)))
        hidden = ((up + 1.0) * glu).astype(BF16)
        ybuf[i % 2] = jnp.dot(hidden, wd_bf[...], preferred_element_type=F32) + bd_ref[own, :]
        y_copy(i).start()

        @pl.when(i == n - 1)
        def _():
            y_copy(i).wait()

            @pl.when(i >= 1)
            def _():
                y_copy(i - 1).wait()

            def zero_copy(b):
                return pltpu.make_async_copy(
                    ybuf.at[0], y_hbm.at[pl.ds(pl.multiple_of(b * rows, rows), rows), 0, :],
                    y_sem.at[0])

            def start_zero(b, carry):
                zero_copy(b).start()
                return carry

            def wait_zero(b, carry):
                zero_copy(b).wait()
                return carry

            ybuf[0] = jnp.zeros((rows, ybuf.shape[2]), F32)
            lax.fori_loop(n, pl.num_programs(0), start_zero, 0)
            lax.fori_loop(n, pl.num_programs(0), wait_zero, 0)


def _experts(x_sorted, block_plan, n_blocks, w_gate, b_gate, w_up, b_up, w_down, b_down):
    n_rows, _, d = x_sorted.shape
    n_exp, _, f = w_gate.shape
    rows = EXPERT_ROWS
    whole = lambda i, be, nu, ws, wn: (0, 0)
    hbm = pl.BlockSpec(memory_space=pl.ANY)
    grid_spec = pltpu.PrefetchScalarGridSpec(
        num_scalar_prefetch=4,
        grid=(n_blocks,),
        in_specs=[
            hbm,
            hbm, pl.BlockSpec((n_exp, f), whole),
            hbm, pl.BlockSpec((n_exp, f), whole),
            hbm, pl.BlockSpec((n_exp, d), whole),
        ],
        out_specs=hbm,
        scratch_shapes=[
            pltpu.VMEM((2, rows, d), F32),
            pltpu.VMEM((2, rows, d), F32),
            pltpu.SemaphoreType.DMA((2,)),
            pltpu.SemaphoreType.DMA((2,)),
            pltpu.VMEM((2, d, f), F32),
            pltpu.VMEM((2, d, f), F32),
            pltpu.VMEM((2, f, d), F32),
            pltpu.SemaphoreType.DMA((2,)),
            pltpu.VMEM((d, f), BF16),
            pltpu.VMEM((d, f), BF16),
            pltpu.VMEM((f, d), BF16),
        ],
    )
    return pl.pallas_call(
        functools.partial(_expert_kernel, rows=rows, n_exp=n_exp),
        grid_spec=grid_spec,
        out_shape=jax.ShapeDtypeStruct((n_rows, 1, d), F32),
        compiler_params=pltpu.CompilerParams(
            dimension_semantics=("arbitrary",), vmem_limit_bytes=52 * MIB),
        name="moe_experts",
    )(*block_plan, x_sorted,
      w_gate, b_gate, w_up, b_up, w_down, b_down)


def _combine_kernel(dest_hbm, y_hbm, x1_ref, gate_ref, lnf_ref, o_ref, idx_smem, ybuf,
                    idx_sem, row_sem, *, tile):
    i = pl.program_id(0)
    n = pl.num_programs(0)
    per_tile = TOP_K * tile

    def slot_base(b):
        return pl.multiple_of((b % 3) * per_tile, per_tile)

    def idx_copy(b):
        src = dest_hbm.at[pl.ds(pl.multiple_of(b * per_tile, per_tile), per_tile)]
        return pltpu.make_async_copy(src, idx_smem.at[pl.ds(slot_base(b), per_tile)],
                                     idx_sem.at[b % 3])

    def gather(b):
        base = slot_base(b)
        buf = ybuf.at[b % 2]
        sem = row_sem.at[b % 2]

        def issue(g, carry):
            for s in range(SUBLANES):
                pltpu.make_async_copy(y_hbm.at[idx_smem[base + g * SUBLANES + s]],
                                      buf.at[g, pl.ds(s, 1), :], sem).start(priority=s % 2)
            return carry

        lax.fori_loop(0, per_tile // SUBLANES, issue, 0)

    @pl.when(i == 0)
    def _():
        idx_copy(0).start()
        idx_copy(0).wait()
        gather(0)

        @pl.when(n > 1)
        def _():
            idx_copy(1).start()

    @pl.when(i + 1 < n)
    def _():
        idx_copy(i + 1).wait()

        @pl.when(i + 2 < n)
        def _():
            idx_copy(i + 2).start()
        gather(i + 1)

    cur = i % 2
    pltpu.make_async_copy(ybuf.at[1 - cur], ybuf.at[cur], row_sem.at[cur]).wait()
    gates = gate_ref[...]
    x = x1_ref[...]
    groups = tile // SUBLANES
    for k in range(TOP_K):
        yk = ybuf[cur, k * groups:(k + 1) * groups].reshape(tile, x.shape[1])
        x = x + gates[:, k:k + 1] * yk
    o_ref[...] = _rms(x, lnf_ref[...])


def _combine(dest, y_rows, x1, gates, ln_f_g):
    t, d = x1.shape
    tile = min(COMBINE_TILE, t)
    n_tiles = t // tile
    dest_tiles = _by_tile(dest, tile)
    return pl.pallas_call(
        functools.partial(_combine_kernel, tile=tile),
        grid=(n_tiles,),
        in_specs=[
            pl.BlockSpec(memory_space=pl.ANY),
            pl.BlockSpec(memory_space=pl.ANY),
            pl.BlockSpec((tile, d), lambda i: (i, 0)),
            pl.BlockSpec((tile, TOP_K), lambda i: (i, 0)),
            pl.BlockSpec((1, d), lambda i: (0, 0)),
        ],
        out_specs=pl.BlockSpec((tile, d), lambda i: (i, 0)),
        out_shape=jax.ShapeDtypeStruct((t, d), F32),
        scratch_shapes=[
            pltpu.SMEM((3 * TOP_K * tile,), I32),
            pltpu.VMEM((2, TOP_K * tile // SUBLANES, SUBLANES, d), F32),
            pltpu.SemaphoreType.DMA((3,)),
            pltpu.SemaphoreType.DMA((2,)),
        ],
        compiler_params=pltpu.CompilerParams(
            dimension_semantics=("arbitrary",), vmem_limit_bytes=40 * MIB),
        name="moe_combine",
    )(dest_tiles, y_rows, x1, gates, ln_f_g.reshape(1, d))


def kernel(x, ln1_g, w_in, lam_re, lam_im, log_dt, ssm_b_re, ssm_b_im, ssm_c_re, ssm_c_im,
           ssm_d, w_glu, g_sb, g_ssm, w_out, ln2_g, w_router, b_router, w_gate, b_gate,
           w_up, b_up, w_down, b_down, ln_f_g):
    bsz, seq, d = x.shape
    assert ln1_g.shape[0] == 1, "depth-1 block only"
    ssm_w = ssm_d.shape[1]
    sb = g_sb.shape[1]
    x2 = x.reshape(bsz * seq, d)
    q, k, v, u = _in_proj(x2, ln1_g[0], w_in[0], sb, ssm_w)
    y_sb = _attention(q, k, v, bsz, seq)
    y_ss = _ssm(u, bsz, seq, lam_re[0], lam_im[0], log_dt[0], ssm_b_re[0], ssm_b_im[0],
                ssm_c_re[0], ssm_c_im[0])
    x1, xn, top_idx, gates, rank, counts = _post(
        x2, y_sb, y_ss, u, ssm_d[0], w_glu[0], g_sb[0], g_ssm[0], w_out[0], ln2_g[0],
        w_router[0], b_router[0])
    dest, block_plan, fill_start, n_blocks = _plan(top_idx[:TOP_K], rank[:TOP_K], counts,
                                                   EXPERT_ROWS)
    x_sorted = _dispatch(xn, dest, fill_start, n_blocks * EXPERT_ROWS)
    y_rows = _experts(x_sorted, block_plan, n_blocks, w_gate[0], b_gate[0],
                      w_up[0], b_up[0], w_down[0], b_down[0])
    out = _combine(dest, y_rows, x1, gates, ln_f_g)
    return out.reshape(bsz, seq, d)
```

```python
import functools
import math

import jax
import jax.numpy as jnp
from jax import lax
from jax.experimental import pallas as pl
from jax.experimental.pallas import tpu as pltpu

F32 = jnp.float32
BF16 = jnp.bfloat16
I32 = jnp.int32

EPS = 1e-5
SB_HEAD_DIM = 64
SSM_GROUP = 16
SSM_STATE = 64
TOP_K = 4
SWIGLU_LIMIT = 7.0
SWIGLU_ALPHA = 1.702

LANES = 128
SUBLANES = 8
HEADS_PER_BLOCK = LANES // SB_HEAD_DIM
ATTN_BLOCK = 256
ATTN_EXP_FLOOR = 160.0
ATTN_STREAMS = 4
SSM_CHUNK = 16
SSM_FOLD = 8
TOKEN_TILE = 512
RING_SLOTS = 3
EXPERT_ROWS = 256
COMBINE_TILE = 512
DISPATCH_TILE = 2048
MIB = 1024 * 1024
LOG2E = 1.4426950408889634


def _rms(x, g):
    return x * lax.rsqrt(jnp.mean(x * x, axis=-1, keepdims=True) + EPS) * g


def _ring_fetch(x_hbm, xbuf, sem):
    i = pl.program_id(0)
    n = pl.num_programs(0)
    slots, rows = xbuf.shape[0], xbuf.shape[1]

    def copy(s):
        return pltpu.make_async_copy(
            x_hbm.at[pl.ds(pl.multiple_of(s * rows, rows), rows), :], xbuf.at[s % slots],
            sem.at[s % slots])

    @pl.when(i == 0)
    def _():
        for s in range(slots - 1):
            @pl.when(s < n)
            def _(s=s):
                copy(s).start()

    @pl.when(i + slots - 1 < n)
    def _():
        copy(i + slots - 1).start()

    copy(i).wait()
    return xbuf.at[i % slots]


def _in_proj_kernel(x_hbm, g_ref, w_ref, q_ref, k_ref, v_ref, u_ref, w_bf, xbuf, x_sem,
                    *, sb, scale):
    @pl.when(pl.program_id(0) == 0)
    def _():
        w_bf[...] = w_ref[...].astype(BF16)

    h = _rms(_ring_fetch(x_hbm, xbuf, x_sem)[...], g_ref[...])
    proj = jnp.dot(h.astype(BF16), w_bf[...], preferred_element_type=F32)
    k_ref[...] = proj[:, sb:2 * sb].astype(BF16)
    u_ref[...] = proj[:, 3 * sb:]
    lane = lax.broadcasted_iota(I32, (1, LANES), 1)
    q = (proj[:, :sb] * scale).astype(BF16)
    v = proj[:, 2 * sb:3 * sb].astype(BF16)
    zero = jnp.zeros((), BF16)
    for pair in range(sb // LANES):
        cols = slice(pair * LANES, (pair + 1) * LANES)
        for head in range(HEADS_PER_BLOCK):
            own = (lane // SB_HEAD_DIM) == head
            out = slice((pair * HEADS_PER_BLOCK + head) * LANES,
                        (pair * HEADS_PER_BLOCK + head + 1) * LANES)
            q_ref[:, out] = jnp.where(own, q[:, cols], zero)
            v_ref[:, out] = jnp.where(own, v[:, cols], zero)


def _in_proj(x2, ln1_g, w_in, sb, ssm_w):
    t, d = x2.shape
    n_in = w_in.shape[1]
    tm = min(TOKEN_TILE, t)
    scale = LOG2E / math.sqrt(SB_HEAD_DIM)
    return pl.pallas_call(
        functools.partial(_in_proj_kernel, sb=sb, scale=scale),
        grid=(t // tm,),
        in_specs=[
            pl.BlockSpec(memory_space=pl.ANY),
            pl.BlockSpec((1, d), lambda i: (0, 0)),
            pl.BlockSpec((d, n_in), lambda i: (0, 0)),
        ],
        out_specs=[
            pl.BlockSpec((tm, HEADS_PER_BLOCK * sb), lambda i: (i, 0)),
            pl.BlockSpec((tm, sb), lambda i: (i, 0)),
            pl.BlockSpec((tm, HEADS_PER_BLOCK * sb), lambda i: (i, 0)),
            pl.BlockSpec((tm, ssm_w), lambda i: (i, 0)),
        ],
        out_shape=[
            jax.ShapeDtypeStruct((t, HEADS_PER_BLOCK * sb), BF16),
            jax.ShapeDtypeStruct((t, sb), BF16),
            jax.ShapeDtypeStruct((t, HEADS_PER_BLOCK * sb), BF16),
            jax.ShapeDtypeStruct((t, ssm_w), F32),
        ],
        scratch_shapes=[pltpu.VMEM((d, n_in), BF16), pltpu.VMEM((RING_SLOTS, tm, d), F32),
                        pltpu.SemaphoreType.DMA((RING_SLOTS,))],
        compiler_params=pltpu.CompilerParams(
            dimension_semantics=("arbitrary",), vmem_limit_bytes=48 * MIB),
        name="in_proj",
    )(x2, ln1_g.reshape(1, d), w_in)


def _attn_kernel(ti_ref, tj_ref, tn_ref, q_ref, k_ref, v_ref, tri_ref, mask_ref, o_ref,
                 dbuf, spbuf, lbuf, rsbuf, acc_ref, r_ref, *, blk, n_items, streams):
    sign_bit = jnp.uint32(0x80000000)
    last = n_items - 1
    ALL, CAUSAL, NONE = 0, 1, 2

    for ref in (dbuf, spbuf, lbuf, rsbuf, acc_ref, r_ref):
        ref[...] = jnp.zeros_like(ref)

    def both_heads(ref, start):
        rows = pl.ds(pl.multiple_of(start, blk), blk)
        return jnp.concatenate([ref[rows, h * LANES:(h + 1) * LANES]
                                for h in range(HEADS_PER_BLOCK)], axis=0)

    def item(idx):
        c = jnp.clip(idx, 0, last)
        return ti_ref[c], tj_ref[c], tn_ref[c]

    n_streams = len(streams)
    stream_end = [end for _, end in streams]

    def step(state, x):
        cur, p1, p2 = list(state[:n_streams]), state[n_streams], state[n_streams + 1]
        out_blk = list(state[n_streams + 2:])
        slot, other = x % 2, 1 - x % 2
        x3 = (x - 2) % n_streams
        c1 = jnp.where(cur[x] < stream_end[x], cur[x], n_items)
        cur[x] = jnp.minimum(cur[x] + 1, stream_end[x])
        c3 = p2
        i1, j1, _ = item(c1)
        ks = k_ref[pl.ds(pl.multiple_of(j1 * blk, blk), blk), :]
        w = lax.dot_general(both_heads(q_ref, i1 * blk), ks, (((1,), (1,)), ((), ())),
                            preferred_element_type=F32)
        i3, j3, next_block = item(c3)
        valid = c3 <= last
        first = jnp.logical_and(i3 == j3, valid)
        mask3 = mask_ref[jnp.where(valid, (i3 == j3).astype(I32), NONE)]
        i3 = jnp.where(valid, i3, out_blk[x3])
        r_prev = jnp.where(first, 0.0, r_ref[x3])
        r_new = r_prev + rsbuf[other]
        r_ref[x3] = r_new
        done = jnp.logical_and(valid, jnp.min(r_new) >= ATTN_EXP_FLOOR)
        cur[x3] = jnp.where(done, jnp.maximum(cur[x3], next_block), cur[x3])
        out_blk[x3] = i3
        r_wide = jnp.concatenate([r_prev] * (blk // LANES), axis=1)
        a = jnp.exp2(dbuf[slot] - (lbuf[other] + r_wide))
        ab = a.astype(BF16) * jnp.concatenate([mask3, mask3], axis=0)
        a_cat = jnp.concatenate([ab[:blk], ab[blk:]], axis=1)
        acc = jnp.where(first, 0.0, acc_ref[x3]) + jnp.dot(
            a_cat, both_heads(v_ref, j3 * blk), preferred_element_type=F32)
        acc_ref[x3] = acc
        o_ref[pl.ds(pl.multiple_of(i3 * blk, blk), blk), :] = acc.astype(o_ref.dtype)
        sums = jnp.dot(spbuf[other], tri_ref[...], preferred_element_type=F32)
        lbuf[slot] = sums[:, :blk]
        rsbuf[slot] = sums[:, blk:]
        mask1 = mask_ref[(i1 == j1).astype(I32)]
        neg_abs = lax.bitcast_convert_type(
            lax.bitcast_convert_type(w, jnp.uint32) | sign_bit, F32)
        sp2 = jnp.maximum(w, 0.0) + jnp.log(1.0 + jnp.exp2(neg_abs)) * LOG2E
        dbuf[slot] = w - sp2
        spbuf[slot] = sp2.astype(BF16) * jnp.concatenate([mask1, mask1], axis=0)
        return (*cur, c1, p1, *out_blk)

    def in_flight(state):
        p1, p2 = state[n_streams], state[n_streams + 1]
        live = [state[x] < stream_end[x] for x in range(n_streams)]
        return functools.reduce(jnp.logical_or, live + [p1 <= last, p2 <= last])

    def one_round(state):
        for x in range(n_streams):
            state = step(state, x)
        return state

    none = jnp.int32(n_items)
    first_items = [jnp.int32(start) for start, _ in streams]
    first_blocks = [ti_ref[start] for start, _ in streams]
    lax.while_loop(in_flight, one_round, (*first_items, none, none, *first_blocks))


def _attention(q, k, v, bsz, seq):
    sb = k.shape[-1]
    blk = min(ATTN_BLOCK, seq)
    n_blk = seq // blk
    n_pairs = sb // LANES
    q3, k3, v3 = (a.reshape(bsz, seq, a.shape[-1]) for a in (q, k, v))
    per_head = HEADS_PER_BLOCK * LANES
    jj = lax.broadcasted_iota(I32, (blk, blk), 0)
    ss = lax.broadcasted_iota(I32, (blk, blk), 1)
    tri = jnp.concatenate([(jj > ss).astype(BF16), jnp.ones((blk, LANES), BF16)], axis=1)
    masks = jnp.stack([jnp.ones((blk, blk), BF16), (ss < jj).astype(BF16),
                       jnp.zeros((blk, blk), BF16)])
    n_streams = ATTN_STREAMS if n_blk % ATTN_STREAMS == 0 else 2
    assert n_blk % n_streams == 0, (n_blk, n_streams)
    items, next_block, streams = [], [], []
    for x in range(n_streams):
        start = len(items)
        for i in range(x, n_blk, n_streams):
            block_end = len(items) + i + 1
            items += [(i, j) for j in range(i, -1, -1)]
            next_block += [block_end] * (i + 1)
        streams.append((start, len(items)))
    item_i = jnp.asarray([i for i, _ in items], I32)
    item_j = jnp.asarray([j for _, j in items], I32)
    item_next = jnp.asarray(next_block, I32)
    whole = lambda b, p, ti, tj, tn: (b, 0, p)
    grid_spec = pltpu.PrefetchScalarGridSpec(
        num_scalar_prefetch=3,
        grid=(bsz, n_pairs),
        in_specs=[
            pl.BlockSpec((None, seq, per_head), whole),
            pl.BlockSpec((None, seq, LANES), whole),
            pl.BlockSpec((None, seq, per_head), whole),
            pl.BlockSpec((blk, blk + LANES), lambda b, p, ti, tj, tn: (0, 0)),
            pl.BlockSpec((3, blk, blk), lambda b, p, ti, tj, tn: (0, 0, 0)),
        ],
        out_specs=pl.BlockSpec((None, seq, LANES), whole),
        scratch_shapes=[
            pltpu.VMEM((2, HEADS_PER_BLOCK * blk, blk), F32),
            pltpu.VMEM((2, HEADS_PER_BLOCK * blk, blk), BF16),
            pltpu.VMEM((2, HEADS_PER_BLOCK * blk, blk), F32),
            pltpu.VMEM((2, HEADS_PER_BLOCK * blk, LANES), F32),
            pltpu.VMEM((n_streams, blk, LANES), F32),
            pltpu.VMEM((n_streams, HEADS_PER_BLOCK * blk, LANES), F32),
        ],
    )
    out = pl.pallas_call(
        functools.partial(_attn_kernel, blk=blk, n_items=len(items), streams=tuple(streams)),
        grid_spec=grid_spec,
        out_shape=jax.ShapeDtypeStruct((bsz, seq, sb), BF16),
        compiler_params=pltpu.CompilerParams(
            dimension_semantics=("arbitrary", "arbitrary"), vmem_limit_bytes=40 * MIB),
        name="sb_attention",
    )(item_i, item_j, item_next, q3, k3, v3, tri, masks)
    return out.reshape(bsz * seq, sb)


def _ssm_params(lam_re, lam_im, log_dt, b_re, b_im, c_re, c_im, chunk, n_chunks):
    lam = lax.complex(lam_re.astype(F32), lam_im.astype(F32))
    dt = jnp.exp(log_dt.astype(F32))[:, None]
    lam_dt = lam * dt
    lam_bar = jnp.exp(lam_dt)
    b_bar = ((lam_bar - 1.0) / lam)[:, :, None] * lax.complex(b_re.astype(F32), b_im.astype(F32))
    b_t = jnp.swapaxes(b_bar, 1, 2)
    c_mat = lax.complex(c_re.astype(F32), c_im.astype(F32))
    steps = jnp.arange(chunk + 1, dtype=F32)
    pw = jnp.exp(lam_dt[:, None, :] * steps[None, :, None])
    npw = jnp.exp(-lam_dt[:, None, :] * steps[None, :chunk, None])

    def halves(lo, hi):
        return jnp.concatenate([lo, hi], axis=-1)

    def b_side(z):
        return [halves(jnp.real(z), jnp.real(z)), halves(-jnp.imag(z), jnp.imag(z))]

    def c_side(z):
        return [halves(jnp.real(z), jnp.imag(z)), halves(jnp.imag(z), jnp.real(z))]

    time_terms = jnp.stack(
        b_side(npw)
        + b_side(pw[:, chunk - 1::-1])
        + c_side(pw[:, :chunk])
        + c_side(pw[:, 1:]), axis=1)
    chan_terms = jnp.stack(
        [halves(jnp.real(b_t), jnp.imag(b_t)), halves(jnp.imag(b_t), jnp.real(b_t)),
         halves(jnp.real(c_mat), -jnp.real(c_mat)), halves(-jnp.imag(c_mat), -jnp.imag(c_mat))],
        axis=1)
    n_steps = max(1, (n_chunks - 1).bit_length())
    powers = []
    cur = pw[:, chunk, :]
    for _ in range(n_steps):
        powers.append(cur)
        cur = cur * cur
    lam_pow = jnp.stack(powers, axis=1)
    a1 = halves(jnp.real(lam_pow), jnp.real(lam_pow))
    a2 = halves(-jnp.imag(lam_pow), jnp.imag(lam_pow))
    return time_terms, chan_terms, a1, a2


def _ssm_kernel(u_ref, time_ref, chan_ref, a1_ref, a2_ref, perm_ref, perm_t_ref, y_ref,
                u8_ref, y8_ref, toep_ref, *, n_chunks, n_steps, group):
    seq = u_ref.shape[0]
    tiles = seq // SSM_FOLD
    lc = toep_ref.shape[0]
    per_chunk = (lc // group) // SSM_FOLD
    p2 = a1_ref.shape[2]
    group_shift = group.bit_length() - 1
    nt = (((1,), (1,)), ((), ()))

    folded = jnp.concatenate(
        [u_ref[pl.ds(s, tiles, stride=SSM_FOLD), :] for s in range(SSM_FOLD)], axis=1)
    regrouped = jnp.dot(folded.astype(BF16), perm_ref[...], preferred_element_type=F32)
    n_groups = LANES // group
    for g in range(n_groups):
        u8_ref[g] = regrouped[:, g * LANES:(g + 1) * LANES]

    for g in range(n_groups):
        def table(t1, t2, v1, v2):
            full = (time_ref[g, t1][:, None, :] * chan_ref[g, v1][None, :, :]
                    + time_ref[g, t2][:, None, :] * chan_ref[g, v2][None, :, :])
            return full.reshape(lc, p2).astype(BF16)

        src = table(0, 1, 0, 1)
        to_state = table(2, 3, 0, 1)
        dst = table(4, 5, 2, 3)
        from_state = table(6, 7, 2, 3)
        cb = min(256, lc)
        for j in range(lc // cb):
            blk = lax.dot_general(src, dst[j * cb:(j + 1) * cb], nt, preferred_element_type=F32)
            s_idx = lax.broadcasted_iota(I32, (lc, cb), 0) >> group_shift
            t_idx = (lax.broadcasted_iota(I32, (lc, cb), 1) + j * cb) >> group_shift
            toep_ref[:, j * cb:(j + 1) * cb] = jnp.where(s_idx <= t_idx, blk, 0.0).astype(BF16)
        u = jnp.concatenate(
            [u8_ref[g, pl.ds(tau, n_chunks, stride=per_chunk), :] for tau in range(per_chunk)],
            axis=1).astype(BF16)
        y = jnp.dot(u, toep_ref[...], preferred_element_type=F32)
        z = jnp.dot(u, to_state, preferred_element_type=F32)
        n = lax.broadcasted_iota(I32, z.shape, 0)
        x = jnp.where(n >= 1, pltpu.roll(z, 1, 0), 0.0)
        for k in range(n_steps):
            sh = 1 << k
            xs = jnp.where(n >= sh, pltpu.roll(x, sh, 0), 0.0)
            x = (x + a1_ref[g, k:k + 1, :] * xs
                 + a2_ref[g, k:k + 1, :] * pltpu.roll(xs, p2 // 2, 1))
        y = y + lax.dot_general(x.astype(BF16), from_state, nt, preferred_element_type=F32)
        for tau in range(per_chunk):
            y8_ref[g, pl.ds(tau, n_chunks, stride=per_chunk), :] = y[:, tau * LANES:(tau + 1) * LANES]

    y8 = jnp.concatenate([y8_ref[g] for g in range(n_groups)], axis=1)
    unfolded = jnp.dot(y8.astype(BF16), perm_t_ref[...], preferred_element_type=F32)
    for s in range(SSM_FOLD):
        y_ref[pl.ds(s, tiles, stride=SSM_FOLD), :] = unfolded[:, s * LANES:(s + 1) * LANES]


def _ssm(u, bsz, seq, lam_re, lam_im, log_dt, b_re, b_im, c_re, c_im):
    g, p = lam_re.shape
    c = b_re.shape[-1]
    chunk = min(SSM_CHUNK, seq)
    n_chunks = seq // chunk
    assert n_chunks & (n_chunks - 1) == 0 and c & (c - 1) == 0, (n_chunks, c)
    assert SSM_FOLD * c == LANES and chunk % SSM_FOLD == 0
    lc = chunk * c
    per_block = LANES // c
    time_terms, chan_terms, a1, a2 = _ssm_params(
        lam_re, lam_im, log_dt, b_re, b_im, c_re, c_im, chunk, n_chunks)
    n_steps = a1.shape[1]
    src_idx = jnp.arange(SSM_FOLD * LANES, dtype=I32)
    s_of, g_of, c_of = src_idx // LANES, (src_idx % LANES) // c, src_idx % c
    dst_idx = g_of * LANES + s_of * c + c_of
    perm = (dst_idx[:, None] == src_idx[None, :]).astype(BF16)
    block = lambda cb, b: (cb, 0, 0)
    fixed = lambda cb, b: (0, 0)
    y = pl.pallas_call(
        functools.partial(_ssm_kernel, n_chunks=n_chunks, n_steps=n_steps, group=c),
        grid=(g // per_block, bsz),
        in_specs=[
            pl.BlockSpec((None, seq, LANES), lambda cb, b: (b, 0, cb)),
            pl.BlockSpec((per_block,) + time_terms.shape[1:], lambda cb, b: (cb, 0, 0, 0)),
            pl.BlockSpec((per_block,) + chan_terms.shape[1:], lambda cb, b: (cb, 0, 0, 0)),
            pl.BlockSpec((per_block, n_steps, 2 * p), block),
            pl.BlockSpec((per_block, n_steps, 2 * p), block),
            pl.BlockSpec((SSM_FOLD * LANES, SSM_FOLD * LANES), fixed),
            pl.BlockSpec((SSM_FOLD * LANES, SSM_FOLD * LANES), fixed),
        ],
        out_specs=pl.BlockSpec((None, seq, LANES), lambda cb, b: (b, 0, cb)),
        out_shape=jax.ShapeDtypeStruct((bsz, seq, g * c), F32),
        scratch_shapes=[
            pltpu.VMEM((per_block, seq // SSM_FOLD, LANES), F32),
            pltpu.VMEM((per_block, seq // SSM_FOLD, LANES), F32),
            pltpu.VMEM((lc, lc), BF16),
        ],
        compiler_params=pltpu.CompilerParams(
            dimension_semantics=("arbitrary", "arbitrary"), vmem_limit_bytes=56 * MIB),
        name="s5_chunked_scan",
    )(u.reshape(bsz, seq, g * c), time_terms, chan_terms, a1, a2, perm, perm.T)
    return y.reshape(bsz * seq, g * c)


def _post_kernel(x_hbm, ysb_ref, yss_ref, u_ref, d_ref, wglu_ref, gsb_ref, gssm_ref, wout_ref,
                 ln2_ref, wrt_ref, br_ref, tri_ref, x1_ref, xn_ref, idx_ref, gate_ref, rank_ref,
                 count_ref, running_ref, wglu_bf, wout_bf, xbuf, x_sem, *, sb, n_exp):
    @pl.when(pl.program_id(0) == 0)
    def _():
        running_ref[...] = jnp.zeros_like(running_ref)
        wglu_bf[...] = wglu_ref[...].astype(BF16)
        wout_bf[...] = wout_ref[...].astype(BF16)

    x_ref = _ring_fetch(x_hbm, xbuf, x_sem)

    u = u_ref[...].astype(F32)
    y = yss_ref[...].astype(F32) + d_ref[...] * u
    y = y * (0.5 * (1.0 + jnp.tanh(math.sqrt(2.0 / math.pi) * (y + 0.044715 * (y * y * y)))))
    ab = jnp.dot(y.astype(BF16), wglu_bf[...], preferred_element_type=F32)
    w = ab.shape[1] // 2
    y_ssm = ab[:, :w] * (1.0 / (1.0 + jnp.exp(-ab[:, w:])))
    m_sb = _rms(ysb_ref[...].astype(F32), gsb_ref[...])
    m_ssm = _rms(y_ssm, gssm_ref[...])
    x1 = (x_ref[...]
          + jnp.dot(m_sb.astype(BF16), wout_bf[:sb, :], preferred_element_type=F32)
          + jnp.dot(m_ssm.astype(BF16), wout_bf[sb:, :], preferred_element_type=F32))
    x1_ref[...] = x1
    xn = _rms(x1, ln2_ref[...])
    xn_ref[...] = xn
    def split(a):
        hi = a.astype(BF16)
        return hi, (a - hi.astype(F32)).astype(BF16)

    xn_hi, xn_lo = split(xn)
    wt_hi, wt_lo = split(wrt_ref[...])
    nt = (((1,), (1,)), ((), ()))
    both = lax.dot_general(jnp.concatenate([wt_hi, wt_lo], axis=0), xn_hi, nt,
                           preferred_element_type=F32)
    low = lax.dot_general(wt_hi, xn_lo, nt, preferred_element_type=F32)
    logits = (both[:n_exp] + both[n_exp:] + low) + br_ref[...]
    tokens = logits.shape[1]
    expert = lax.broadcasted_iota(I32, logits.shape, 0).astype(F32)
    out_row = lax.broadcasted_iota(I32, (SUBLANES, tokens), 0)
    idx_out = jnp.zeros((SUBLANES, tokens), F32)
    val_out = jnp.zeros((SUBLANES, tokens), F32)
    top = None
    denom = None
    work = logits
    chosen = []
    for k in range(TOP_K):
        m = jnp.max(work, axis=0, keepdims=True)
        sel = jnp.min(jnp.where(work == m, expert, float(n_exp)), axis=0, keepdims=True)
        hit = expert == sel
        chosen.append(hit)
        work = jnp.where(hit, -jnp.inf, work)
        if k == 0:
            top = m
        e = jnp.exp(m - top)
        denom = e if denom is None else denom + e
        idx_out = jnp.where(out_row == k, sel, idx_out)
        val_out = jnp.where(out_row == k, e, val_out)
    idx_ref[...] = idx_out.astype(I32)
    gates = jnp.concatenate([val_out / denom, jnp.zeros((LANES - SUBLANES, tokens), F32)], axis=0)
    gate_ref[...] = jnp.transpose(gates)[:, :TOP_K]
    member = functools.reduce(jnp.logical_or, chosen).astype(BF16)
    before = running_ref[...] + jnp.dot(member, tri_ref[...], preferred_element_type=F32)
    rank_out = jnp.zeros((SUBLANES, tokens), F32)
    for k in range(TOP_K):
        rk = jnp.sum(jnp.where(chosen[k], before, 0.0), axis=0, keepdims=True)
        rank_out = jnp.where(out_row == k, rk, rank_out)
    rank_ref[...] = rank_out.astype(I32)
    running = running_ref[...] + jnp.sum(member.astype(F32), axis=1, keepdims=True)
    running_ref[...] = running
    count_ref[...] = running


def _post(x2, y_sb, y_ss, u, ssm_d, w_glu, g_sb, g_ssm, w_out, ln2_g, w_router, b_router):
    t, d = x2.shape
    sb = y_sb.shape[1]
    w = y_ss.shape[1]
    n_exp = w_router.shape[1]
    tm = min(TOKEN_TILE, t)
    row = lambda i: (i, 0)
    fixed = lambda i: (0, 0)
    earlier = (lax.broadcasted_iota(I32, (tm, tm), 0)
               < lax.broadcasted_iota(I32, (tm, tm), 1)).astype(BF16)
    return pl.pallas_call(
        functools.partial(_post_kernel, sb=sb, n_exp=n_exp),
        grid=(t // tm,),
        in_specs=[
            pl.BlockSpec(memory_space=pl.ANY),
            pl.BlockSpec((tm, sb), row),
            pl.BlockSpec((tm, w), row),
            pl.BlockSpec((tm, w), row),
            pl.BlockSpec((1, w), fixed),
            pl.BlockSpec((w, 2 * w), fixed),
            pl.BlockSpec((1, sb), fixed),
            pl.BlockSpec((1, w), fixed),
            pl.BlockSpec((sb + w, d), fixed),
            pl.BlockSpec((1, d), fixed),
            pl.BlockSpec((n_exp, d), fixed),
            pl.BlockSpec((n_exp, 1), fixed),
            pl.BlockSpec((tm, tm), fixed),
        ],
        out_specs=[
            pl.BlockSpec((tm, d), row),
            pl.BlockSpec((tm, d), row),
            pl.BlockSpec((SUBLANES, tm), lambda i: (0, i)),
            pl.BlockSpec((tm, TOP_K), row),
            pl.BlockSpec((SUBLANES, tm), lambda i: (0, i)),
            pl.BlockSpec((n_exp, 1), fixed),
        ],
        out_shape=[
            jax.ShapeDtypeStruct((t, d), F32),
            jax.ShapeDtypeStruct((t, d), F32),
            jax.ShapeDtypeStruct((SUBLANES, t), I32),
            jax.ShapeDtypeStruct((t, TOP_K), F32),
            jax.ShapeDtypeStruct((SUBLANES, t), I32),
            jax.ShapeDtypeStruct((n_exp, 1), F32),
        ],
        scratch_shapes=[pltpu.VMEM((n_exp, 1), F32), pltpu.VMEM((w, 2 * w), BF16),
                        pltpu.VMEM((sb + w, d), BF16), pltpu.VMEM((RING_SLOTS, tm, d), F32),
                        pltpu.SemaphoreType.DMA((RING_SLOTS,))],
        compiler_params=pltpu.CompilerParams(
            dimension_semantics=("arbitrary",), vmem_limit_bytes=48 * MIB),
        name="post_mixer_router",
    )(x2, y_sb, y_ss, u, ssm_d.reshape(1, w), w_glu, g_sb.reshape(1, sb),
      g_ssm.reshape(1, w), w_out, ln2_g.reshape(1, d), w_router.T,
      b_router.reshape(n_exp, 1), earlier)


def _plan(top_idx, rank, counts, rows_per_block):
    k, t = top_idx.shape
    n_exp = counts.shape[0]
    counts = counts.reshape(n_exp).astype(I32)
    padded = ((counts + rows_per_block - 1) // rows_per_block) * rows_per_block
    pad_ends = jnp.cumsum(padded)
    pad_starts = pad_ends - padded
    start_of = jnp.sum(jnp.where(top_idx[None] == jnp.arange(n_exp, dtype=I32)[:, None, None],
                                 pad_starts[:, None, None], 0), axis=0)
    dest = (start_of + rank).astype(I32)
    n_rows = t * k + n_exp * rows_per_block
    n_blocks = n_rows // rows_per_block
    block_start = jnp.arange(n_blocks, dtype=I32) * rows_per_block
    block_expert = jnp.minimum(
        jnp.sum((pad_ends[None, :] <= block_start[:, None]).astype(I32), axis=1), n_exp - 1)
    n_used = (pad_ends[-1] // rows_per_block).astype(I32).reshape(1)
    fill_start = jnp.concatenate([jnp.maximum(pad_ends - rows_per_block, 0).astype(I32), n_used])
    experts = jnp.arange(n_exp, dtype=I32)
    nonempty = counts > 0
    order = jnp.cumsum(nonempty.astype(I32)) - nonempty.astype(I32)
    later_ne = jnp.where((experts[None, :] > experts[:, None]) & nonempty[None, :],
                         experts[None, :], n_exp)
    next_tbl = jnp.min(later_ne, axis=1)
    onehot = block_expert[:, None] == experts[None, :]
    w_slot = jnp.sum(jnp.where(onehot, order & 1, 0), axis=1).astype(I32)
    w_next = jnp.sum(jnp.where(onehot, next_tbl, 0), axis=1).astype(I32)
    return dest, (block_expert, n_used, w_slot, w_next), fill_start, n_blocks


def _by_tile(dest, tile):
    k, t = dest.shape
    return dest.reshape(k, t // tile, tile).transpose(1, 0, 2).reshape(k * t)


def _dispatch_kernel(fill_ref, dest_hbm, xn_ref, xs_hbm, idx_smem, zeros_ref, idx_sem, row_sem,
                     fill_sem, *, tile, n_exp, fill_rows):
    i = pl.program_id(0)
    n = pl.num_programs(0)
    per_tile = TOP_K * tile

    def idx_copy(b):
        src = dest_hbm.at[pl.ds(pl.multiple_of(b * per_tile, per_tile), per_tile)]
        dst = idx_smem.at[pl.ds(pl.multiple_of((b % 2) * per_tile, per_tile), per_tile)]
        return pltpu.make_async_copy(src, dst, idx_sem.at[b % 2])

    @pl.when(i == 0)
    def _():
        idx_copy(0).start()
        zeros_ref[...] = jnp.zeros_like(zeros_ref)
        def fill(start):
            return pltpu.make_async_copy(
                zeros_ref, xs_hbm.at[pl.ds(pl.multiple_of(start, SUBLANES), fill_rows), 0, :],
                fill_sem)

        def start_unused(b, carry):
            fill(b * fill_rows).start()
            return carry

        def wait_unused(b, carry):
            fill(b * fill_rows).wait()
            return carry

        n_blocks = xs_hbm.shape[0] // fill_rows
        for e in range(n_exp):
            fill(fill_ref[e]).start()
        lax.fori_loop(fill_ref[n_exp], n_blocks, start_unused, 0)
        for e in range(n_exp):
            fill(fill_ref[e]).wait()
        lax.fori_loop(fill_ref[n_exp], n_blocks, wait_unused, 0)

    idx_copy(i).wait()

    @pl.when(i + 1 < n)
    def _():
        idx_copy(i + 1).start()

    base = pl.multiple_of((i % 2) * per_tile, per_tile)
    quarter = tile // TOP_K
    for part in range(TOP_K):
        def issue(g, carry, part=part):
            group = part * (quarter // SUBLANES) + g
            for s in range(SUBLANES):
                for k in range(TOP_K):
                    dst_row = idx_smem[base + k * tile + group * SUBLANES + s]
                    pltpu.make_async_copy(xn_ref.at[group, pl.ds(s, 1), :], xs_hbm.at[dst_row],
                                          row_sem.at[part]).start(priority=k % 2)
            return carry

        lax.fori_loop(0, quarter // SUBLANES, issue, 0)
    for part in range(TOP_K):
        pltpu.make_async_copy(xn_ref, xn_ref, row_sem.at[part]).wait()


def _dispatch(xn, dest, fill_start, n_rows):
    t, d = xn.shape
    tile = min(DISPATCH_TILE, t)
    n_exp = fill_start.shape[0] - 1
    grid_spec = pltpu.PrefetchScalarGridSpec(
        num_scalar_prefetch=1,
        grid=(t // tile,),
        in_specs=[
            pl.BlockSpec(memory_space=pl.ANY),
            pl.BlockSpec((tile // SUBLANES, SUBLANES, d), lambda i, fs: (i, 0, 0)),
        ],
        out_specs=pl.BlockSpec(memory_space=pl.ANY),
        scratch_shapes=[
            pltpu.SMEM((2 * TOP_K * tile,), I32),
            pltpu.VMEM((EXPERT_ROWS, d), F32),
            pltpu.SemaphoreType.DMA((2,)),
            pltpu.SemaphoreType.DMA((TOP_K,)),
            pltpu.SemaphoreType.DMA,
        ],
    )
    return pl.pallas_call(
        functools.partial(_dispatch_kernel, tile=tile, n_exp=n_exp, fill_rows=EXPERT_ROWS),
        grid_spec=grid_spec,
        out_shape=jax.ShapeDtypeStruct((n_rows, 1, d), F32),
        compiler_params=pltpu.CompilerParams(
            dimension_semantics=("arbitrary",), vmem_limit_bytes=40 * MIB),
        name="moe_dispatch",
    )(fill_start, _by_tile(dest, tile), xn.reshape(t // SUBLANES, SUBLANES, d))


def _expert_kernel(be_ref, nused_ref, wslot_ref, wnext_ref, xs_hbm, wg_hbm, bg_ref, wu_hbm, bu_ref,
                   wd_hbm, bd_ref, y_hbm, xbuf, ybuf, x_sem, y_sem, wg32, wu32, wd32, w_sem,
                   wg_bf, wu_bf, wd_bf, *, rows, n_exp):
    i = pl.program_id(0)
    n = nused_ref[0]

    def w_copies(e, slot):
        return [pltpu.make_async_copy(src.at[e], dst.at[slot], w_sem.at[slot])
                for src, dst in ((wg_hbm, wg32), (wu_hbm, wu32), (wd_hbm, wd32))]

    def x_copy(b):
        return pltpu.make_async_copy(
            xs_hbm.at[pl.ds(pl.multiple_of(b * rows, rows), rows), 0, :], xbuf.at[b % 2],
            x_sem.at[b % 2])

    def y_copy(b):
        return pltpu.make_async_copy(
            ybuf.at[b % 2], y_hbm.at[pl.ds(pl.multiple_of(b * rows, rows), rows), 0, :],
            y_sem.at[b % 2])

    @pl.when(i < n)
    def _():
        @pl.when(i == 0)
        def _():
            x_copy(0).start()
            for cp in w_copies(be_ref[0], wslot_ref[0]):
                cp.start()

        @pl.when(i + 1 < n)
        def _():
            x_copy(i + 1).start()

        changed = jnp.logical_or(i == 0, be_ref[i] != be_ref[jnp.maximum(i - 1, 0)])

        @pl.when(changed)
        def _():
            slot = wslot_ref[i]
            for cp in w_copies(be_ref[i], slot):
                cp.wait()

            @pl.when(wnext_ref[i] < n_exp)
            def _():
                for cp in w_copies(wnext_ref[i], 1 - slot):
                    cp.start(priority=1)
            wg_bf[...] = wg32[slot].astype(BF16)
            wu_bf[...] = wu32[slot].astype(BF16)
            wd_bf[...] = wd32[slot].astype(BF16)

        @pl.when(i >= 2)
        def _():
            y_copy(i - 2).wait()

        x_copy(i).wait()
        x = xbuf[i % 2].astype(BF16)
        own = pl.ds(be_ref[i], 1)
        gate = jnp.dot(x, wg_bf[...], preferred_element_type=F32) + bg_ref[own, :]
        up = jnp.dot(x, wu_bf[...], preferred_element_type=F32) + bu_ref[own, :]
        gate = jnp.minimum(gate, SWIGLU_LIMIT)
        up = jnp.clip(up, -SWIGLU_LIMIT, SWIGLU_LIMIT)
        glu = gate * (1.0 / (1.0 + jnp.exp(-SWIGLU_ALPHA * gate)))
        hidden = ((up + 1.0) * glu).astype(BF16)
        ybuf[i % 2] = jnp.dot(hidden, wd_bf[...], preferred_element_type=F32) + bd_ref[own, :]
        y_copy(i).start()

        @pl.when(i == n - 1)
        def _():
            y_copy(i).wait()

            @pl.when(i >= 1)
            def _():
                y_copy(i - 1).wait()

            def zero_copy(b):
                return pltpu.make_async_copy(
                    ybuf.at[0], y_hbm.at[pl.ds(pl.multiple_of(b * rows, rows), rows), 0, :],
                    y_sem.at[0])

            def start_zero(b, carry):
                zero_copy(b).start()
                return carry

            def wait_zero(b, carry):
                zero_copy(b).wait()
                return carry

            ybuf[0] = jnp.zeros((rows, ybuf.shape[2]), F32)
            lax.fori_loop(n, pl.num_programs(0), start_zero, 0)
            lax.fori_loop(n, pl.num_programs(0), wait_zero, 0)


def _experts(x_sorted, block_plan, n_blocks, w_gate, b_gate, w_up, b_up, w_down, b_down):
    n_rows, _, d = x_sorted.shape
    n_exp, _, f = w_gate.shape
    rows = EXPERT_ROWS
    whole = lambda i, be, nu, ws, wn: (0, 0)
    hbm = pl.BlockSpec(memory_space=pl.ANY)
    grid_spec = pltpu.PrefetchScalarGridSpec(
        num_scalar_prefetch=4,
        grid=(n_blocks,),
        in_specs=[
            hbm,
            hbm, pl.BlockSpec((n_exp, f), whole),
            hbm, pl.BlockSpec((n_exp, f), whole),
            hbm, pl.BlockSpec((n_exp, d), whole),
        ],
        out_specs=hbm,
        scratch_shapes=[
            pltpu.VMEM((2, rows, d), F32),
            pltpu.VMEM((2, rows, d), F32),
            pltpu.SemaphoreType.DMA((2,)),
            pltpu.SemaphoreType.DMA((2,)),
            pltpu.VMEM((2, d, f), F32),
            pltpu.VMEM((2, d, f), F32),
            pltpu.VMEM((2, f, d), F32),
            pltpu.SemaphoreType.DMA((2,)),
            pltpu.VMEM((d, f), BF16),
            pltpu.VMEM((d, f), BF16),
            pltpu.VMEM((f, d), BF16),
        ],
    )
    return pl.pallas_call(
        functools.partial(_expert_kernel, rows=rows, n_exp=n_exp),
        grid_spec=grid_spec,
        out_shape=jax.ShapeDtypeStruct((n_rows, 1, d), F32),
        compiler_params=pltpu.CompilerParams(
            dimension_semantics=("arbitrary",), vmem_limit_bytes=52 * MIB),
        name="moe_experts",
    )(*block_plan, x_sorted,
      w_gate, b_gate, w_up, b_up, w_down, b_down)


def _combine_kernel(dest_hbm, y_hbm, x1_ref, gate_ref, lnf_ref, o_ref, idx_smem, ybuf,
                    idx_sem, row_sem, *, tile):
    i = pl.program_id(0)
    n = pl.num_programs(0)
    per_tile = TOP_K * tile

    def slot_base(b):
        return pl.multiple_of((b % 3) * per_tile, per_tile)

    def idx_copy(b):
        src = dest_hbm.at[pl.ds(pl.multiple_of(b * per_tile, per_tile), per_tile)]
        return pltpu.make_async_copy(src, idx_smem.at[pl.ds(slot_base(b), per_tile)],
                                     idx_sem.at[b % 3])

    def gather(b):
        base = slot_base(b)
        buf = ybuf.at[b % 2]
        sem = row_sem.at[b % 2]

        def issue(g, carry):
            for s in range(SUBLANES):
                pltpu.make_async_copy(y_hbm.at[idx_smem[base + g * SUBLANES + s]],
                                      buf.at[g, pl.ds(s, 1), :], sem).start(priority=s % 2)
            return carry

        lax.fori_loop(0, per_tile // SUBLANES, issue, 0)

    @pl.when(i == 0)
    def _():
        idx_copy(0).start()
        idx_copy(0).wait()
        gather(0)

        @pl.when(n > 1)
        def _():
            idx_copy(1).start()

    @pl.when(i + 1 < n)
    def _():
        idx_copy(i + 1).wait()

        @pl.when(i + 2 < n)
        def _():
            idx_copy(i + 2).start()
        gather(i + 1)

    cur = i % 2
    pltpu.make_async_copy(ybuf.at[1 - cur], ybuf.at[cur], row_sem.at[cur]).wait()
    gates = gate_ref[...]
    x = x1_ref[...]
    groups = tile // SUBLANES
    for k in range(TOP_K):
        yk = ybuf[cur, k * groups:(k + 1) * groups].reshape(tile, x.shape[1])
        x = x + gates[:, k:k + 1] * yk
    o_ref[...] = _rms(x, lnf_ref[...])


def _combine(dest, y_rows, x1, gates, ln_f_g):
    t, d = x1.shape
    tile = min(COMBINE_TILE, t)
    n_tiles = t // tile
    dest_tiles = _by_tile(dest, tile)
    return pl.pallas_call(
        functools.partial(_combine_kernel, tile=tile),
        grid=(n_tiles,),
        in_specs=[
            pl.BlockSpec(memory_space=pl.ANY),
            pl.BlockSpec(memory_space=pl.ANY),
            pl.BlockSpec((tile, d), lambda i: (i, 0)),
            pl.BlockSpec((tile, TOP_K), lambda i: (i, 0)),
            pl.BlockSpec((1, d), lambda i: (0, 0)),
        ],
        out_specs=pl.BlockSpec((tile, d), lambda i: (i, 0)),
        out_shape=jax.ShapeDtypeStruct((t, d), F32),
        scratch_shapes=[
            pltpu.SMEM((3 * TOP_K * tile,), I32),
            pltpu.VMEM((2, TOP_K * tile // SUBLANES, SUBLANES, d), F32),
            pltpu.SemaphoreType.DMA((3,)),
            pltpu.SemaphoreType.DMA((2,)),
        ],
        compiler_params=pltpu.CompilerParams(
            dimension_semantics=("arbitrary",), vmem_limit_bytes=40 * MIB),
        name="moe_combine",
    )(dest_tiles, y_rows, x1, gates, ln_f_g.reshape(1, d))


def kernel(x, ln1_g, w_in, lam_re, lam_im, log_dt, ssm_b_re, ssm_b_im, ssm_c_re, ssm_c_im,
           ssm_d, w_glu, g_sb, g_ssm, w_out, ln2_g, w_router, b_router, w_gate, b_gate,
           w_up, b_up, w_down, b_down, ln_f_g):
    bsz, seq, d = x.shape
    assert ln1_g.shape[0] == 1, "depth-1 block only"
    ssm_w = ssm_d.shape[1]
    sb = g_sb.shape[1]
    x2 = x.reshape(bsz * seq, d)
    q, k, v, u = _in_proj(x2, ln1_g[0], w_in[0], sb, ssm_w)
    y_sb = _attention(q, k, v, bsz, seq)
    y_ss = _ssm(u, bsz, seq, lam_re[0], lam_im[0], log_dt[0], ssm_b_re[0], ssm_b_im[0],
                ssm_c_re[0], ssm_c_im[0])
    x1, xn, top_idx, gates, rank, counts = _post(
        x2, y_sb, y_ss, u, ssm_d[0], w_glu[0], g_sb[0], g_ssm[0], w_out[0], ln2_g[0],
        w_router[0], b_router[0])
    dest, block_plan, fill_start, n_blocks = _plan(top_idx[:TOP_K], rank[:TOP_K], counts,
                                                   EXPERT_ROWS)
    x_sorted = _dispatch(xn, dest, fill_start, n_blocks * EXPERT_ROWS)
    y_rows = _experts(x_sorted, block_plan, n_blocks, w_gate[0], b_gate[0],
                      w_up[0], b_up[0], w_down[0], b_down[0])
    out = _combine(dest, y_rows, x1, gates, ln_f_g)
    return out.reshape(bsz, seq, d)
```
